```python
import jax, jax.numpy as jnp
from jax import lax
import numpy as np

D_MODEL = 1024
BATCH = 8
SEQ = 16384
DEPTH = 1

CHUNK = 64
N_META = 16
META_PAD = CHUNK - N_META
Q_BLOCK = 128
FOX_HEADS = 8
FOX_HEAD_DIM = 64
FOX_WIDTH = FOX_HEADS * FOX_HEAD_DIM
GDN_HEADS = 8
GDN_HEAD_DIM = 64
GDN_WIDTH = GDN_HEADS * GDN_HEAD_DIM
GDN_CONV = 4
D_FF = 2816
FFN_CONV = 3
IN_SPLITS = (3 * FOX_WIDTH, FOX_HEADS, 3 * GDN_WIDTH, GDN_WIDTH, GDN_HEADS, GDN_HEADS, 2 * D_MODEL)
IN_WIDTH = 3 * FOX_WIDTH + FOX_HEADS + 4 * GDN_WIDTH + 2 * GDN_HEADS + 2 * D_MODEL
RMS_EPS = 1e-6
NEG_INF = -1e30

kernel_name = "hybrid_fox_gdn_convffn_meta"


def rms_norm(x, gain):
    xf = x.astype(jnp.float32)
    y = xf * lax.rsqrt(jnp.mean(xf * xf, axis=-1, keepdims=True) + RMS_EPS)
    return (y * gain.astype(jnp.float32)).astype(x.dtype)


def l2_norm(x):
    return x * lax.rsqrt(jnp.sum(x * x, axis=-1, keepdims=True) + RMS_EPS)


def causal_dwconv(x, w, b=None):
    k_width, ch = w.shape
    y = lax.conv_general_dilated(
        x, w[:, None, :].astype(x.dtype), window_strides=(1,), padding=[(k_width - 1, 0)],
        dimension_numbers=('NWC', 'WIO', 'NWC'), feature_group_count=ch)
    if b is not None:
        y = y + b.astype(x.dtype)
    return y


def split_cols(a, sizes):
    idx = []
    acc = 0
    for s in sizes[:-1]:
        acc += s
        idx.append(acc)
    return jnp.split(a, idx, axis=-1)


def forgetting_attention(q, k, v, log_f):
    bn, seq_len, heads, dh = q.shape
    lp = -(-seq_len // Q_BLOCK) * Q_BLOCK
    pad = lp - seq_len
    q, k, v = [jnp.pad(a, ((0, 0), (0, pad), (0, 0), (0, 0))) for a in (q, k, v)]
    c = jnp.cumsum(jnp.pad(log_f, ((0, 0), (0, pad), (0, 0))), axis=1).transpose(0, 2, 1)
    nblk = lp // Q_BLOCK
    qb = q.reshape(bn, nblk, Q_BLOCK, heads, dh).transpose(1, 0, 3, 2, 4)
    cb = c.reshape(bn, heads, nblk, Q_BLOCK).transpose(2, 0, 1, 3)
    kpos = jnp.arange(lp)
    scale = dh ** -0.5

    def block(args):
        q_i, c_i, i = args
        s = jnp.einsum('bhqd,bkhd->bhqk', q_i, k, preferred_element_type=jnp.float32) * scale
        s = s + c_i[..., None] - c[:, :, None, :]
        qpos = i * Q_BLOCK + jnp.arange(Q_BLOCK)
        s = jnp.where(kpos[None, :] <= qpos[:, None], s, NEG_INF)
        p = jax.nn.softmax(s, axis=-1).astype(v.dtype)
        return jnp.einsum('bhqk,bkhd->bqhd', p, v)

    o = lax.map(block, (qb, cb, jnp.arange(nblk)))
    return o.transpose(1, 0, 2, 3, 4).reshape(bn, lp, heads * dh)[:, :seq_len]


def gated_delta_rule(q, k, v, beta, g):
    bn, seq_len, heads, dk = q.shape
    dv = v.shape[-1]
    q = l2_norm(q.astype(jnp.float32)) * (dk ** -0.5)
    k = l2_norm(k.astype(jnp.float32))
    v = v.astype(jnp.float32)
    total = META_PAD + seq_len
    back = (-total) % CHUNK
    total = total + back
    n_chunks = total // CHUNK
    pad4 = ((0, 0), (META_PAD, back), (0, 0), (0, 0))
    pad3 = ((0, 0), (META_PAD, back), (0, 0))
    q, k, v = [jnp.pad(a, pad4).reshape(bn, n_chunks, CHUNK, heads, -1).transpose(0, 3, 1, 2, 4)
               for a in (q, k, v)]
    beta, g = [jnp.pad(a, pad3).reshape(bn, n_chunks, CHUNK, heads).transpose(0, 3, 1, 2)
               for a in (beta, g)]
    gc = jnp.cumsum(g, axis=-1)
    tril = jnp.tril(jnp.ones((CHUNK, CHUNK), dtype=bool))
    strict = jnp.tril(jnp.ones((CHUNK, CHUNK), dtype=bool), -1)
    decay = jnp.exp(jnp.where(tril, gc[..., :, None] - gc[..., None, :], NEG_INF))
    kb = k * beta[..., None]
    vb = v * beta[..., None]
    lmat = jnp.where(strict, jnp.einsum('bhncd,bhnsd->bhncs', kb, k) * decay, 0.0)
    eye = jnp.eye(CHUNK, dtype=jnp.float32)
    rhs = jnp.concatenate([vb, kb * jnp.exp(gc)[..., None]], axis=-1)
    sol = lax.linalg.triangular_solve(lmat + eye, rhs, left_side=True, lower=True, unit_diagonal=True)
    value, k_cumdecay = sol[..., :dv], sol[..., dv:]
    attn_intra = jnp.einsum('bhncd,bhnsd->bhncs', q, k) * decay
    q_dec = q * jnp.exp(gc)[..., None]
    k_tail = k * jnp.exp(gc[..., -1:] - gc)[..., None]
    chunk_decay = jnp.exp(gc[..., -1])

    def step(state, xs):
        value_n, kcd_n, attn_n, qd_n, kt_n, cd_n = xs
        v_new = value_n - jnp.einsum('bhcd,bhdv->bhcv', kcd_n, state)
        o_n = jnp.einsum('bhcd,bhdv->bhcv', qd_n, state) + jnp.einsum('bhcs,bhsv->bhcv', attn_n, v_new)
        state = state * cd_n[..., None, None] + jnp.einsum('bhcd,bhcv->bhdv', kt_n, v_new)
        return state, o_n

    xs = tuple(jnp.moveaxis(a, 2, 0) for a in (value, k_cumdecay, attn_intra, q_dec, k_tail, chunk_decay))
    state0 = jnp.zeros((bn, heads, dk, dv), jnp.float32)
    _, o = lax.scan(step, state0, xs)
    o = o.transpose(1, 0, 3, 2, 4).reshape(bn, total, heads, dv)
    return o[:, META_PAD:META_PAD + seq_len]


def hybrid_mixer(h, w_in, fgt_bias, gdn_conv_w, gdn_a_log, gdn_dt_bias, gdn_norm_w, gate_bias,
                 w_branch_fox, w_branch_gdn, w_out):
    bn, seq_len, _ = h.shape
    proj = h @ w_in.astype(h.dtype)
    fox_qkv, fox_f, gdn_qkv, gdn_z, gdn_b, gdn_a, gates = split_cols(proj, IN_SPLITS)
    fox_qkv = fox_qkv.reshape(bn, seq_len, 3, FOX_HEADS, FOX_HEAD_DIM)
    log_f = jax.nn.log_sigmoid(fox_f.astype(jnp.float32) + fgt_bias.astype(jnp.float32))
    o_fox = forgetting_attention(fox_qkv[:, :, 0], fox_qkv[:, :, 1], fox_qkv[:, :, 2], log_f)
    gdn_qkv = jax.nn.silu(causal_dwconv(gdn_qkv, gdn_conv_w))
    gdn_qkv = gdn_qkv.reshape(bn, seq_len, 3, GDN_HEADS, GDN_HEAD_DIM)
    beta = jax.nn.sigmoid(gdn_b.astype(jnp.float32))
    g = -jnp.exp(gdn_a_log.astype(jnp.float32)) * jax.nn.softplus(
        gdn_a.astype(jnp.float32) + gdn_dt_bias.astype(jnp.float32))
    o_gdn = gated_delta_rule(gdn_qkv[:, :, 0], gdn_qkv[:, :, 1], gdn_qkv[:, :, 2], beta, g)
    z = gdn_z.reshape(bn, seq_len, GDN_HEADS, GDN_HEAD_DIM).astype(jnp.float32)
    o_gdn = (rms_norm(o_gdn, gdn_norm_w) * jax.nn.silu(z)).astype(h.dtype).reshape(bn, seq_len, GDN_WIDTH)
    y_fox = o_fox @ w_branch_fox.astype(h.dtype)
    y_gdn = o_gdn @ w_branch_gdn.astype(h.dtype)
    gate = jax.nn.sigmoid(gates.astype(jnp.float32) + gate_bias.astype(jnp.float32)).astype(h.dtype)
    g_fox, g_gdn = gate[..., :D_MODEL], gate[..., D_MODEL:]
    return (g_fox * y_fox + g_gdn * y_gdn) @ w_out.astype(h.dtype)


def conv_ffn(h, w_up, conv_w, conv_b, w_down):
    u = causal_dwconv(h @ w_up.astype(h.dtype), conv_w, conv_b)
    gate, up = u[..., :D_FF], u[..., D_FF:]
    return (jax.nn.silu(gate) * up) @ w_down.astype(h.dtype)


def _fwd_setup_inputs(seed: int = 0) -> dict:
    key = jax.random.key(seed)
    ks = jax.random.split(key, 20)
    f32 = jnp.float32

    def nrm(k, shape, scale):
        return jax.random.normal(k, shape, f32) * scale

    dt = jnp.exp(jax.random.uniform(ks[5], (DEPTH, GDN_HEADS), f32, np.log(1e-3), np.log(1e-1)))
    return {
        "x": nrm(ks[0], (BATCH, SEQ, D_MODEL), 1.0),
        "meta_tokens": nrm(ks[1], (N_META, D_MODEL), 1.0),
        "w_in": nrm(ks[2], (DEPTH, D_MODEL, IN_WIDTH), D_MODEL ** -0.5),
        "fgt_bias": 2.0 + nrm(ks[3], (DEPTH, FOX_HEADS), 0.5),
        "gdn_conv_w": nrm(ks[4], (DEPTH, GDN_CONV, 3 * GDN_WIDTH), GDN_CONV ** -0.5),
        "gdn_a_log": jnp.log(jax.random.uniform(ks[6], (DEPTH, GDN_HEADS), f32, 1.0, 16.0)),
        "gdn_dt_bias": dt + jnp.log(-jnp.expm1(-dt)),
        "gdn_norm_w": 1.0 + nrm(ks[7], (DEPTH, GDN_HEAD_DIM), 0.01),
        "gate_bias": nrm(ks[8], (DEPTH, 2 * D_MODEL), 0.01),
        "w_branch_fox": nrm(ks[9], (DEPTH, FOX_WIDTH, D_MODEL), FOX_WIDTH ** -0.5),
        "w_branch_gdn": nrm(ks[10], (DEPTH, GDN_WIDTH, D_MODEL), GDN_WIDTH ** -0.5),
        "w_out": nrm(ks[11], (DEPTH, D_MODEL, D_MODEL), D_MODEL ** -0.5),
        "norm_mix_w": 1.0 + nrm(ks[12], (DEPTH, D_MODEL), 0.01),
        "norm_ffn_w": 1.0 + nrm(ks[13], (DEPTH, D_MODEL), 0.01),
        "ffn_w_up": nrm(ks[14], (DEPTH, D_MODEL, 2 * D_FF), D_MODEL ** -0.5),
        "ffn_conv_w": nrm(ks[15], (DEPTH, FFN_CONV, 2 * D_FF), FFN_CONV ** -0.5),
        "ffn_conv_b": nrm(ks[16], (DEPTH, 2 * D_FF), 0.01),
        "ffn_w_down": nrm(ks[17], (DEPTH, D_FF, D_MODEL), D_FF ** -0.5),
        "norm_final_w": 1.0 + nrm(ks[18], (D_MODEL,), 0.01),
    }


def _fwd_reference(x, meta_tokens, w_in, fgt_bias, gdn_conv_w, gdn_a_log, gdn_dt_bias, gdn_norm_w,
              gate_bias, w_branch_fox, w_branch_gdn, w_out, norm_mix_w, norm_ffn_w, ffn_w_up,
              ffn_conv_w, ffn_conv_b, ffn_w_down, norm_final_w):
    bn = x.shape[0]
    meta = jnp.broadcast_to(meta_tokens[None].astype(x.dtype), (bn, N_META, D_MODEL))
    h = jnp.concatenate([meta, x], axis=1)
    for layer in range(DEPTH):
        h = h + hybrid_mixer(rms_norm(h, norm_mix_w[layer]), w_in[layer], fgt_bias[layer],
                             gdn_conv_w[layer], gdn_a_log[layer], gdn_dt_bias[layer],
                             gdn_norm_w[layer], gate_bias[layer], w_branch_fox[layer],
                             w_branch_gdn[layer], w_out[layer])
        h = h + conv_ffn(rms_norm(h, norm_ffn_w[layer]), ffn_w_up[layer], ffn_conv_w[layer],
                         ffn_conv_b[layer], ffn_w_down[layer])
    return rms_norm(h, norm_final_w)[:, N_META:]


import jax as _jax
import jax.numpy as _jnp

TWIN_FORMAT = 'train_step'
FWD_PARAMS = ['x', 'meta_tokens', 'w_in', 'fgt_bias', 'gdn_conv_w', 'gdn_a_log', 'gdn_dt_bias', 'gdn_norm_w', 'gate_bias', 'w_branch_fox', 'w_branch_gdn', 'w_out', 'norm_mix_w', 'norm_ffn_w', 'ffn_w_up', 'ffn_conv_w', 'ffn_conv_b', 'ffn_w_down', 'norm_final_w']
TWIN_WEIGHTS = ['meta_tokens', 'w_in', 'fgt_bias', 'gdn_conv_w', 'gdn_a_log', 'gdn_dt_bias', 'gdn_norm_w', 'gate_bias', 'w_branch_fox', 'w_branch_gdn', 'w_out', 'norm_mix_w', 'norm_ffn_w', 'ffn_w_up', 'ffn_conv_w', 'ffn_conv_b', 'ffn_w_down', 'norm_final_w']
TWIN_DIFF_INPUT = 'x'
TWIN_INPUTS = ['x', 'meta_tokens', 'w_in', 'fgt_bias', 'gdn_conv_w', 'gdn_a_log', 'gdn_dt_bias', 'gdn_norm_w', 'gate_bias', 'w_branch_fox', 'w_branch_gdn', 'w_out', 'norm_mix_w', 'norm_ffn_w', 'ffn_w_up', 'ffn_conv_w', 'ffn_conv_b', 'ffn_w_down', 'norm_final_w', 'loss_target', 'm_meta_tokens', 'm_w_in', 'm_fgt_bias', 'm_gdn_conv_w', 'm_gdn_a_log', 'm_gdn_dt_bias', 'm_gdn_norm_w', 'm_gate_bias', 'm_w_branch_fox', 'm_w_branch_gdn', 'm_w_out', 'm_norm_mix_w', 'm_norm_ffn_w', 'm_ffn_w_up', 'm_ffn_conv_w', 'm_ffn_conv_b', 'm_ffn_w_down', 'm_norm_final_w', 'v_meta_tokens', 'v_w_in', 'v_fgt_bias', 'v_gdn_conv_w', 'v_gdn_a_log', 'v_gdn_dt_bias', 'v_gdn_norm_w', 'v_gate_bias', 'v_w_branch_fox', 'v_w_branch_gdn', 'v_w_out', 'v_norm_mix_w', 'v_norm_ffn_w', 'v_ffn_w_up', 'v_ffn_conv_w', 'v_ffn_conv_b', 'v_ffn_w_down', 'v_norm_final_w']
TWIN_OUTPUTS = ['loss', 'grad_x', 'grad_meta_tokens', 'grad_w_in', 'grad_fgt_bias', 'grad_gdn_conv_w', 'grad_gdn_a_log', 'grad_gdn_dt_bias', 'grad_gdn_norm_w', 'grad_gate_bias', 'grad_w_branch_fox', 'grad_w_branch_gdn', 'grad_w_out', 'grad_norm_mix_w', 'grad_norm_ffn_w', 'grad_ffn_w_up', 'grad_ffn_conv_w', 'grad_ffn_conv_b', 'grad_ffn_w_down', 'grad_norm_final_w', 'delta_meta_tokens', 'delta_w_in', 'delta_fgt_bias', 'delta_gdn_conv_w', 'delta_gdn_a_log', 'delta_gdn_dt_bias', 'delta_gdn_norm_w', 'delta_gate_bias', 'delta_w_branch_fox', 'delta_w_branch_gdn', 'delta_w_out', 'delta_norm_mix_w', 'delta_norm_ffn_w', 'delta_ffn_w_up', 'delta_ffn_conv_w', 'delta_ffn_conv_b', 'delta_ffn_w_down', 'delta_norm_final_w', 'new_m_meta_tokens', 'new_m_w_in', 'new_m_fgt_bias', 'new_m_gdn_conv_w', 'new_m_gdn_a_log', 'new_m_gdn_dt_bias', 'new_m_gdn_norm_w', 'new_m_gate_bias', 'new_m_w_branch_fox', 'new_m_w_branch_gdn', 'new_m_w_out', 'new_m_norm_mix_w', 'new_m_norm_ffn_w', 'new_m_ffn_w_up', 'new_m_ffn_conv_w', 'new_m_ffn_conv_b', 'new_m_ffn_w_down', 'new_m_norm_final_w', 'new_v_meta_tokens', 'new_v_w_in', 'new_v_fgt_bias', 'new_v_gdn_conv_w', 'new_v_gdn_a_log', 'new_v_gdn_dt_bias', 'new_v_gdn_norm_w', 'new_v_gate_bias', 'new_v_w_branch_fox', 'new_v_w_branch_gdn', 'new_v_w_out', 'new_v_norm_mix_w', 'new_v_norm_ffn_w', 'new_v_ffn_w_up', 'new_v_ffn_conv_w', 'new_v_ffn_conv_b', 'new_v_ffn_w_down', 'new_v_norm_final_w']
TWIN_LEAF_KINDS = {'loss': 'loss', 'grad_x': 'grad_x', 'grad_meta_tokens': 'grad_w', 'grad_w_in': 'grad_w', 'grad_fgt_bias': 'grad_w', 'grad_gdn_conv_w': 'grad_w', 'grad_gdn_a_log': 'grad_w', 'grad_gdn_dt_bias': 'grad_w', 'grad_gdn_norm_w': 'grad_w', 'grad_gate_bias': 'grad_w', 'grad_w_branch_fox': 'grad_w', 'grad_w_branch_gdn': 'grad_w', 'grad_w_out': 'grad_w', 'grad_norm_mix_w': 'grad_w', 'grad_norm_ffn_w': 'grad_w', 'grad_ffn_w_up': 'grad_w', 'grad_ffn_conv_w': 'grad_w', 'grad_ffn_conv_b': 'grad_w', 'grad_ffn_w_down': 'grad_w', 'grad_norm_final_w': 'grad_w', 'delta_meta_tokens': 'delta_w', 'delta_w_in': 'delta_w', 'delta_fgt_bias': 'delta_w', 'delta_gdn_conv_w': 'delta_w', 'delta_gdn_a_log': 'delta_w', 'delta_gdn_dt_bias': 'delta_w', 'delta_gdn_norm_w': 'delta_w', 'delta_gate_bias': 'delta_w', 'delta_w_branch_fox': 'delta_w', 'delta_w_branch_gdn': 'delta_w', 'delta_w_out': 'delta_w', 'delta_norm_mix_w': 'delta_w', 'delta_norm_ffn_w': 'delta_w', 'delta_ffn_w_up': 'delta_w', 'delta_ffn_conv_w': 'delta_w', 'delta_ffn_conv_b': 'delta_w', 'delta_ffn_w_down': 'delta_w', 'delta_norm_final_w': 'delta_w', 'new_m_meta_tokens': 'new_m', 'new_m_w_in': 'new_m', 'new_m_fgt_bias': 'new_m', 'new_m_gdn_conv_w': 'new_m', 'new_m_gdn_a_log': 'new_m', 'new_m_gdn_dt_bias': 'new_m', 'new_m_gdn_norm_w': 'new_m', 'new_m_gate_bias': 'new_m', 'new_m_w_branch_fox': 'new_m', 'new_m_w_branch_gdn': 'new_m', 'new_m_w_out': 'new_m', 'new_m_norm_mix_w': 'new_m', 'new_m_norm_ffn_w': 'new_m', 'new_m_ffn_w_up': 'new_m', 'new_m_ffn_conv_w': 'new_m', 'new_m_ffn_conv_b': 'new_m', 'new_m_ffn_w_down': 'new_m', 'new_m_norm_final_w': 'new_m', 'new_v_meta_tokens': 'new_v', 'new_v_w_in': 'new_v', 'new_v_fgt_bias': 'new_v', 'new_v_gdn_conv_w': 'new_v', 'new_v_gdn_a_log': 'new_v', 'new_v_gdn_dt_bias': 'new_v', 'new_v_gdn_norm_w': 'new_v', 'new_v_gate_bias': 'new_v', 'new_v_w_branch_fox': 'new_v', 'new_v_w_branch_gdn': 'new_v', 'new_v_w_out': 'new_v', 'new_v_norm_mix_w': 'new_v', 'new_v_norm_ffn_w': 'new_v', 'new_v_ffn_w_up': 'new_v', 'new_v_ffn_conv_w': 'new_v', 'new_v_ffn_conv_b': 'new_v', 'new_v_ffn_w_down': 'new_v', 'new_v_norm_final_w': 'new_v'}


def _forward(args):
    return _fwd_reference(*[args[k] for k in FWD_PARAMS])


def _output_shape():
    def fwd():
        inp = _fwd_setup_inputs(0)
        return _fwd_reference(*[inp[k] for k in FWD_PARAMS])
    out = _jax.eval_shape(fwd)
    return out.shape, out.dtype

N_MICROBATCH = 1
ADAM_LR = 0.001
ADAM_B1 = 0.9
ADAM_B2 = 0.999
ADAM_EPS = 1e-08
ADAM_WD = 0.01
ADAM_STEP = 10
PER_EXAMPLE_BATCH_AXIS = {'x': 0, 'loss_target': 0}
SHARED_INPUTS = []
_WEIGHT_DTYPES = {'meta_tokens': _jnp.float32, 'w_in': _jnp.float32, 'fgt_bias': _jnp.float32, 'gdn_conv_w': _jnp.float32, 'gdn_a_log': _jnp.float32, 'gdn_dt_bias': _jnp.float32, 'gdn_norm_w': _jnp.float32, 'gate_bias': _jnp.float32, 'w_branch_fox': _jnp.float32, 'w_branch_gdn': _jnp.float32, 'w_out': _jnp.float32, 'norm_mix_w': _jnp.float32, 'norm_ffn_w': _jnp.float32, 'ffn_w_up': _jnp.float32, 'ffn_conv_w': _jnp.float32, 'ffn_conv_b': _jnp.float32, 'ffn_w_down': _jnp.float32, 'norm_final_w': _jnp.float32}
MOMENT_SCALE = {'meta_tokens': 7.498754e-03, 'w_in': 1.167344e-01, 'fgt_bias': 8.636724e-01, 'gdn_conv_w': 1.654931e-01, 'gdn_a_log': 7.329938e-01, 'gdn_dt_bias': 7.421751e-01, 'gdn_norm_w': 4.916646e-01, 'gate_bias': 4.003007e-02, 'w_branch_fox': 8.311634e-02, 'w_branch_gdn': 1.172446e-01, 'w_out': 1.438511e-01, 'norm_mix_w': 2.932954e-01, 'norm_ffn_w': 2.541004e-01, 'ffn_w_up': 1.090978e-01, 'ffn_conv_w': 1.070916e-01, 'ffn_conv_b': 1.050800e-01, 'ffn_w_down': 1.780838e-01, 'norm_final_w': 1.278612e+02}


def _to_microbatches(a, axis):
    t = _jnp.moveaxis(a, axis, 0)
    t = t.reshape((N_MICROBATCH, t.shape[0] // N_MICROBATCH) + t.shape[1:])
    return _jnp.moveaxis(t, 1, axis + 1)


def setup_inputs(seed: int = 0) -> dict:
    inp = _fwd_setup_inputs(seed)
    key = _jax.random.fold_in(_jax.random.key(seed), 7919)
    shape, _ = _output_shape()
    out = dict(inp)
    out["loss_target"] = _jax.random.normal(_jax.random.fold_in(key, 0), shape, _jnp.float32)
    for i, name in enumerate(TWIN_WEIGHTS):
        w = inp[name].astype(_jnp.float32)
        if MOMENT_SCALE is None:
            s = _jnp.sqrt(_jnp.mean(_jnp.square(w)) + 1e-30)
        else:
            s = MOMENT_SCALE[name]
        km, kv = _jax.random.split(_jax.random.fold_in(key, i + 1))
        out[name] = w
        out["m_" + name] = s * _jax.random.normal(km, w.shape, _jnp.float32)
        out["v_" + name] = (s * s) * _jax.random.uniform(kv, w.shape, _jnp.float32, 0.5, 1.5)
    if N_MICROBATCH > 1:
        for name, axis in PER_EXAMPLE_BATCH_AXIS.items():
            out[name] = _to_microbatches(out[name], axis)
    return {'x': out['x'], 'meta_tokens': out['meta_tokens'], 'w_in': out['w_in'], 'fgt_bias': out['fgt_bias'], 'gdn_conv_w': out['gdn_conv_w'], 'gdn_a_log': out['gdn_a_log'], 'gdn_dt_bias': out['gdn_dt_bias'], 'gdn_norm_w': out['gdn_norm_w'], 'gate_bias': out['gate_bias'], 'w_branch_fox': out['w_branch_fox'], 'w_branch_gdn': out['w_branch_gdn'], 'w_out': out['w_out'], 'norm_mix_w': out['norm_mix_w'], 'norm_ffn_w': out['norm_ffn_w'], 'ffn_w_up': out['ffn_w_up'], 'ffn_conv_w': out['ffn_conv_w'], 'ffn_conv_b': out['ffn_conv_b'], 'ffn_w_down': out['ffn_w_down'], 'norm_final_w': out['norm_final_w'], 'loss_target': out['loss_target'], 'm_meta_tokens': out['m_meta_tokens'], 'm_w_in': out['m_w_in'], 'm_fgt_bias': out['m_fgt_bias'], 'm_gdn_conv_w': out['m_gdn_conv_w'], 'm_gdn_a_log': out['m_gdn_a_log'], 'm_gdn_dt_bias': out['m_gdn_dt_bias'], 'm_gdn_norm_w': out['m_gdn_norm_w'], 'm_gate_bias': out['m_gate_bias'], 'm_w_branch_fox': out['m_w_branch_fox'], 'm_w_branch_gdn': out['m_w_branch_gdn'], 'm_w_out': out['m_w_out'], 'm_norm_mix_w': out['m_norm_mix_w'], 'm_norm_ffn_w': out['m_norm_ffn_w'], 'm_ffn_w_up': out['m_ffn_w_up'], 'm_ffn_conv_w': out['m_ffn_conv_w'], 'm_ffn_conv_b': out['m_ffn_conv_b'], 'm_ffn_w_down': out['m_ffn_w_down'], 'm_norm_final_w': out['m_norm_final_w'], 'v_meta_tokens': out['v_meta_tokens'], 'v_w_in': out['v_w_in'], 'v_fgt_bias': out['v_fgt_bias'], 'v_gdn_conv_w': out['v_gdn_conv_w'], 'v_gdn_a_log': out['v_gdn_a_log'], 'v_gdn_dt_bias': out['v_gdn_dt_bias'], 'v_gdn_norm_w': out['v_gdn_norm_w'], 'v_gate_bias': out['v_gate_bias'], 'v_w_branch_fox': out['v_w_branch_fox'], 'v_w_branch_gdn': out['v_w_branch_gdn'], 'v_w_out': out['v_w_out'], 'v_norm_mix_w': out['v_norm_mix_w'], 'v_norm_ffn_w': out['v_norm_ffn_w'], 'v_ffn_w_up': out['v_ffn_w_up'], 'v_ffn_conv_w': out['v_ffn_conv_w'], 'v_ffn_conv_b': out['v_ffn_conv_b'], 'v_ffn_w_down': out['v_ffn_w_down'], 'v_norm_final_w': out['v_norm_final_w']}


def _loss(weights, diff, rest, loss_target):
    with _jax.named_scope("forward"):
        args = {**rest, TWIN_DIFF_INPUT: diff, **{k: w.astype(_WEIGHT_DTYPES[k]) for k, w in weights.items()}}
        y = _forward(args)
    with _jax.named_scope("loss_head"):
        err = _jnp.square(y.astype(_jnp.float32) - loss_target)
        return 0.5 * _jnp.sum(_jnp.mean(err, axis=-1)) if err.ndim else 0.5 * err


def _adamw(w, g, m, v):
    m = ADAM_B1 * m + (1.0 - ADAM_B1) * g
    v = ADAM_B2 * v + (1.0 - ADAM_B2) * _jnp.square(g)
    m_hat = m / (1.0 - ADAM_B1 ** ADAM_STEP)
    v_hat = v / (1.0 - ADAM_B2 ** ADAM_STEP)
    delta = -ADAM_LR * (m_hat / (_jnp.sqrt(v_hat) + ADAM_EPS) + ADAM_WD * w)
    return delta, m, v


def reference(x, meta_tokens, w_in, fgt_bias, gdn_conv_w, gdn_a_log, gdn_dt_bias, gdn_norm_w, gate_bias, w_branch_fox, w_branch_gdn, w_out, norm_mix_w, norm_ffn_w, ffn_w_up, ffn_conv_w, ffn_conv_b, ffn_w_down, norm_final_w, loss_target, m_meta_tokens, m_w_in, m_fgt_bias, m_gdn_conv_w, m_gdn_a_log, m_gdn_dt_bias, m_gdn_norm_w, m_gate_bias, m_w_branch_fox, m_w_branch_gdn, m_w_out, m_norm_mix_w, m_norm_ffn_w, m_ffn_w_up, m_ffn_conv_w, m_ffn_conv_b, m_ffn_w_down, m_norm_final_w, v_meta_tokens, v_w_in, v_fgt_bias, v_gdn_conv_w, v_gdn_a_log, v_gdn_dt_bias, v_gdn_norm_w, v_gate_bias, v_w_branch_fox, v_w_branch_gdn, v_w_out, v_norm_mix_w, v_norm_ffn_w, v_ffn_w_up, v_ffn_conv_w, v_ffn_conv_b, v_ffn_w_down, v_norm_final_w):
    given = dict(x=x, meta_tokens=meta_tokens, w_in=w_in, fgt_bias=fgt_bias, gdn_conv_w=gdn_conv_w, gdn_a_log=gdn_a_log, gdn_dt_bias=gdn_dt_bias, gdn_norm_w=gdn_norm_w, gate_bias=gate_bias, w_branch_fox=w_branch_fox, w_branch_gdn=w_branch_gdn, w_out=w_out, norm_mix_w=norm_mix_w, norm_ffn_w=norm_ffn_w, ffn_w_up=ffn_w_up, ffn_conv_w=ffn_conv_w, ffn_conv_b=ffn_conv_b, ffn_w_down=ffn_w_down, norm_final_w=norm_final_w, loss_target=loss_target, m_meta_tokens=m_meta_tokens, m_w_in=m_w_in, m_fgt_bias=m_fgt_bias, m_gdn_conv_w=m_gdn_conv_w, m_gdn_a_log=m_gdn_a_log, m_gdn_dt_bias=m_gdn_dt_bias, m_gdn_norm_w=m_gdn_norm_w, m_gate_bias=m_gate_bias, m_w_branch_fox=m_w_branch_fox, m_w_branch_gdn=m_w_branch_gdn, m_w_out=m_w_out, m_norm_mix_w=m_norm_mix_w, m_norm_ffn_w=m_norm_ffn_w, m_ffn_w_up=m_ffn_w_up, m_ffn_conv_w=m_ffn_conv_w, m_ffn_conv_b=m_ffn_conv_b, m_ffn_w_down=m_ffn_w_down, m_norm_final_w=m_norm_final_w, v_meta_tokens=v_meta_tokens, v_w_in=v_w_in, v_fgt_bias=v_fgt_bias, v_gdn_conv_w=v_gdn_conv_w, v_gdn_a_log=v_gdn_a_log, v_gdn_dt_bias=v_gdn_dt_bias, v_gdn_norm_w=v_gdn_norm_w, v_gate_bias=v_gate_bias, v_w_branch_fox=v_w_branch_fox, v_w_branch_gdn=v_w_branch_gdn, v_w_out=v_w_out, v_norm_mix_w=v_norm_mix_w, v_norm_ffn_w=v_norm_ffn_w, v_ffn_w_up=v_ffn_w_up, v_ffn_conv_w=v_ffn_conv_w, v_ffn_conv_b=v_ffn_conv_b, v_ffn_w_down=v_ffn_w_down, v_norm_final_w=v_norm_final_w)
    weights = {n: given[n] for n in TWIN_WEIGHTS}
    shared = {n: given[n] for n in SHARED_INPUTS}
    per_example = {n: given[n] for n in ['x']}
    grad_fn = _jax.value_and_grad(_loss, argnums=(0, 1))

    def one_microbatch(ex, loss_target):
        ex = dict(ex)
        diff = ex.pop(TWIN_DIFF_INPUT)
        return grad_fn(weights, diff, {**shared, **ex}, loss_target)

    if N_MICROBATCH == 1:
        loss, (grad_w, grad_x) = one_microbatch(per_example, given["loss_target"])
    else:
        def body(carry, xs):
            loss_sum, grad_sum = carry
            l_k, (gw_k, gx_k) = one_microbatch(xs[0], xs[1])
            with _jax.named_scope("update"):
                return (loss_sum + l_k, _jax.tree.map(_jnp.add, grad_sum, gw_k)), gx_k

        init = (_jnp.zeros((), _jnp.float32), _jax.tree.map(_jnp.zeros_like, weights))
        (loss, grad_w), grad_x = _jax.lax.scan(body, init, (per_example, given["loss_target"]))
    with _jax.named_scope("update"):
        delta_w, new_m, new_v = {}, {}, {}
        for n in TWIN_WEIGHTS:
            delta_w[n], new_m[n], new_v[n] = _adamw(weights[n], grad_w[n], given["m_" + n], given["v_" + n])
    return (loss, grad_x, *[grad_w[n] for n in TWIN_WEIGHTS], *[delta_w[n] for n in TWIN_WEIGHTS],
            *[new_m[n] for n in TWIN_WEIGHTS], *[new_v[n] for n in TWIN_WEIGHTS])
```

```python
import functools

import jax
import jax.numpy as jnp
import numpy as np
from jax import lax
from jax.experimental import pallas as pl
from jax.experimental.pallas import tpu as pltpu

F32 = jnp.float32
BF16 = jnp.bfloat16

D_MODEL = 1024
N_META = 16
HEADS = 8
HEAD_DIM = 64
WIDTH = HEADS * HEAD_DIM
CHUNK = 64
GDN_CONV = 4
D_FF = 2816
FFN_CONV = 3
IN_WIDTH = 5656
IN_PAD = 5760
RMS_EPS = 1e-6
NEG = -1e30
AUG = 128
N_DEV = 8
LANES = 128

ADAM_LR = 0.001
ADAM_B1 = 0.9
ADAM_B2 = 0.999
ADAM_EPS = 1e-08
ADAM_WD = 0.01
ADAM_STEP = 10

VMEM_LIMIT = 56 * 1024 * 1024
HI = lax.Precision.HIGH
MESH_ID = pl.DeviceIdType.MESH


def _pick(n, cands):
    for c in cands:
        if n % c == 0:
            return c
    raise ValueError(f"no tile for {n} in {cands}")


def _params(*sem):
    return pltpu.CompilerParams(dimension_semantics=sem if sem else None, vmem_limit_bytes=VMEM_LIMIT)


def _padded_tokens(seq):
    t = -(-(N_META + seq) // 128) * 128
    if t > 1280 and t % 640:
        t = -(-t // 640) * 640
    return t


ROW_TILES = (640, 512, 384, 256, 128)


def _rmsnorm_fwd(h, gain):
    t, d = h.shape
    tr = _pick(t, ROW_TILES)

    def body(h_ref, g_ref, o_ref):
        x = h_ref[...]
        r = lax.rsqrt(jnp.mean(x * x, axis=-1, keepdims=True) + RMS_EPS)
        o_ref[...] = (x * r * g_ref[...]).astype(o_ref.dtype)

    return pl.pallas_call(
        body, grid=(t // tr,), name="rmsnorm_fwd",
        in_specs=[pl.BlockSpec((tr, d), lambda i: (i, 0)), pl.BlockSpec((1, d), lambda i: (0, 0))],
        out_specs=pl.BlockSpec((tr, d), lambda i: (i, 0)),
        out_shape=jax.ShapeDtypeStruct((t, d), BF16),
        compiler_params=_params("arbitrary"),
    )(h, gain)


def _rmsnorm_bwd(h, dy, gain, dres):
    t, d = h.shape
    tr = _pick(t, (320, 256, 128))

    def body(h_ref, dy_ref, g_ref, dres_ref, dh_ref, dhb_ref, dg_ref):
        x = h_ref[...]
        dyv = dy_ref[...]
        r = lax.rsqrt(jnp.mean(x * x, axis=-1, keepdims=True) + RMS_EPS)
        gy = dyv * g_ref[...]
        m = jnp.mean(gy * x, axis=-1, keepdims=True)
        dh = dres_ref[...] + r * gy - x * (r * r * r * m)
        dh_ref[...] = dh
        dhb_ref[...] = dh.astype(BF16)

        @pl.when(pl.program_id(0) == 0)
        def _():
            dg_ref[...] = jnp.zeros_like(dg_ref)

        dg_ref[...] += jnp.sum(dyv * x * r, axis=0, keepdims=True)

    row = pl.BlockSpec((tr, d), lambda i: (i, 0))
    vec = pl.BlockSpec((1, d), lambda i: (0, 0))
    return pl.pallas_call(
        body, grid=(t // tr,), name="rmsnorm_bwd",
        in_specs=[row, row, vec, row], out_specs=[row, row, vec],
        out_shape=[jax.ShapeDtypeStruct((t, d), F32), jax.ShapeDtypeStruct((t, d), BF16),
                   jax.ShapeDtypeStruct((1, d), F32)],
        compiler_params=_params("arbitrary"),
    )(h, dy, gain, dres)


def _mm(a, b, out_dtype, name, res=None):
    m, k = a.shape
    _, n = b.shape
    tm = _pick(m, ROW_TILES)
    tn = _pick(n, (512, 384, 256, 128))
    tk = k if k <= 2048 else _pick(k, (1408, 1152, 1024, 512))
    nk = k // tk

    def body(*refs):
        if res is None:
            a_ref, b_ref, o_ref, acc_ref = refs
        else:
            a_ref, b_ref, r_ref, o_ref, acc_ref = refs
        kk = pl.program_id(2)

        @pl.when(kk == 0)
        def _():
            acc_ref[...] = jnp.zeros_like(acc_ref)

        acc_ref[...] += jnp.dot(a_ref[...], b_ref[...], preferred_element_type=F32)

        @pl.when(kk == nk - 1)
        def _():
            out = acc_ref[...]
            if res is not None:
                out = out + r_ref[...]
            o_ref[...] = out.astype(o_ref.dtype)

    in_specs = [pl.BlockSpec((tm, tk), lambda i, j, kk: (i, kk)), pl.BlockSpec((tk, tn), lambda i, j, kk: (kk, j))]
    args = [a, b]
    if res is not None:
        in_specs.append(pl.BlockSpec((tm, tn), lambda i, j, kk: (i, j)))
        args.append(res)
    return pl.pallas_call(
        body, grid=(m // tm, n // tn, nk), name=name,
        in_specs=in_specs, out_specs=pl.BlockSpec((tm, tn), lambda i, j, kk: (i, j)),
        out_shape=jax.ShapeDtypeStruct((m, n), out_dtype),
        scratch_shapes=[pltpu.VMEM((tm, tn), F32)],
        compiler_params=_params("parallel", "parallel", "arbitrary"),
    )(*args)


def _mm_tn(a, g, name):
    t, k = a.shape
    _, n = g.shape
    tt = _pick(t, ROW_TILES)
    tk = _pick(k, (1024, 1408, 512))
    tn = _pick(n, (512, 640, 384, 256, 128))
    nt = t // tt

    def body(a_ref, g_ref, o_ref):
        @pl.when(pl.program_id(2) == 0)
        def _():
            o_ref[...] = jnp.zeros_like(o_ref)

        o_ref[...] += lax.dot_general(a_ref[...], g_ref[...], (((0,), (0,)), ((), ())),
                                      preferred_element_type=F32)

    return pl.pallas_call(
        body, grid=(k // tk, n // tn, nt), name=name,
        in_specs=[pl.BlockSpec((tt, tk), lambda i, j, s: (s, i)), pl.BlockSpec((tt, tn), lambda i, j, s: (s, j))],
        out_specs=pl.BlockSpec((tk, tn), lambda i, j, s: (i, j)),
        out_shape=jax.ShapeDtypeStruct((k, n), F32),
        compiler_params=_params("parallel", "parallel", "arbitrary"),
    )(a, g)


def _fox_fwd(qa, ka, va, tq=None):
    h, t, _ = qa.shape
    tq = tq or _pick(t, ROW_TILES)

    def body(q_ref, k_ref, v_ref, o_ref):
        i = pl.program_id(1)
        q = q_ref[...]
        row = lax.broadcasted_iota(jnp.int32, (tq, tq), 0)
        col = lax.broadcasted_iota(jnp.int32, (tq, tq), 1)

        def step(j, carry, masked):
            m, acc = carry
            start = pl.multiple_of(j * tq, tq)
            kj = k_ref[pl.ds(start, tq), :]
            vj = v_ref[pl.ds(start, tq), :]
            s = lax.dot_general(q, kj, (((1,), (1,)), ((), ())), preferred_element_type=F32)
            if masked:
                s = jnp.where(row >= col, s, NEG)
            m_new = jnp.maximum(m, jnp.max(s, axis=-1, keepdims=True))
            p = jnp.exp(s - m_new)
            alpha = jnp.exp(m - m_new)
            acc = acc * alpha + jnp.dot(p.astype(BF16), vj, preferred_element_type=F32)
            return m_new, acc

        carry = (jnp.full((tq, 1), NEG, F32), jnp.zeros((tq, AUG), F32))
        carry = lax.fori_loop(0, i, lambda j, c: step(j, c, False), carry)
        m, acc = step(i, carry, True)
        lane = lax.broadcasted_iota(jnp.int32, (tq, AUG), 1)
        l = jnp.sum(jnp.where(lane == HEAD_DIM, acc, 0.0), axis=-1, keepdims=True)
        o_ref[...] = jnp.where(lane < HEAD_DIM, acc / l, m + jnp.log(l))

    return pl.pallas_call(
        body, grid=(h, t // tq), name="fox_fwd",
        in_specs=[pl.BlockSpec((None, tq, AUG), lambda hh, i: (hh, i, 0)),
                  pl.BlockSpec((None, t, AUG), lambda hh, i: (hh, 0, 0)),
                  pl.BlockSpec((None, t, AUG), lambda hh, i: (hh, 0, 0))],
        out_specs=pl.BlockSpec((None, tq, AUG), lambda hh, i: (hh, i, 0)),
        out_shape=jax.ShapeDtypeStruct((h, t, AUG), F32),
        compiler_params=_params("parallel", "arbitrary"),
    )(qa, ka, va)


def _fox_bwd(qb, ka, va, doa, tq=None):
    h, t, _ = qb.shape
    tq = tq or _pick(t, ROW_TILES)
    nq = t // tq

    def body(q_ref, k_ref, v_ref, do_ref, dq_ref, dk_ref, dv_ref):
        j = pl.program_id(1)

        @pl.when(j == 0)
        def _():
            dq_ref[...] = jnp.zeros_like(dq_ref)

        kj = k_ref[...]
        vj = v_ref[...]
        krow = lax.broadcasted_iota(jnp.int32, (tq, tq), 0)
        qcol = lax.broadcasted_iota(jnp.int32, (tq, tq), 1)

        def step(i, carry, masked):
            dk, dv = carry
            start = pl.multiple_of(i * tq, tq)
            qi = q_ref[pl.ds(start, tq), :]
            doi = do_ref[pl.ds(start, tq), :]
            st = lax.dot_general(kj, qi, (((1,), (1,)), ((), ())), preferred_element_type=F32)
            if masked:
                st = jnp.where(qcol >= krow, st, NEG)
            pt = jnp.exp(st)
            dpt = lax.dot_general(vj, doi, (((1,), (1,)), ((), ())), preferred_element_type=F32)
            dst = (pt * dpt).astype(BF16)
            dv = dv + jnp.dot(pt.astype(BF16), doi, preferred_element_type=F32)
            dk = dk + jnp.dot(dst, qi, preferred_element_type=F32)
            dq_ref[pl.ds(start, tq), :] += lax.dot_general(dst, kj, (((0,), (0,)), ((), ())),
                                                            preferred_element_type=F32)
            return dk, dv

        zero = jnp.zeros((tq, AUG), F32)
        carry = step(j, (zero, zero), True)
        dk, dv = lax.fori_loop(j + 1, nq, lambda i, c: step(i, c, False), carry)
        dk_ref[...] = dk
        dv_ref[...] = dv

    full = pl.BlockSpec((None, t, AUG), lambda hh, j: (hh, 0, 0))
    blk = pl.BlockSpec((None, tq, AUG), lambda hh, j: (hh, j, 0))
    shp = jax.ShapeDtypeStruct((h, t, AUG), F32)
    return pl.pallas_call(
        body, grid=(h, nq), name="fox_bwd",
        in_specs=[full, blk, blk, full], out_specs=[full, blk, blk], out_shape=[shp, shp, shp],
        compiler_params=_params("parallel", "arbitrary"),
    )(qb, ka, va, doa)


def _split3(x):
    hi = lax.reduce_precision(x, 8, 7)
    r1 = x - hi
    mid = lax.reduce_precision(r1, 8, 7)
    lo = r1 - mid
    return [hi.astype(BF16), mid.astype(BF16), lo.astype(BF16)]


def _heads_major(x):
    t = x.shape[0]
    return x.reshape(t, HEADS, HEAD_DIM).transpose(1, 0, 2)


def _tokens_major(x):
    t = x.shape[1]
    return x.transpose(1, 0, 2).reshape(t, WIDTH)


def _aug(main, cols):
    h, t, _ = main.shape
    extra = jnp.stack(cols, axis=-1)
    pad = jnp.zeros((h, t, AUG - HEAD_DIM - len(cols)), BF16)
    return jnp.concatenate([main, extra, pad], axis=-1)


def _seg_matrix():
    idx = np.arange(WIDTH) // HEAD_DIM
    return jnp.asarray((idx[:, None] == idx[None, :]).astype(np.float32))


def _segsum(x, e):
    return jnp.dot(x, e, precision=HI, preferred_element_type=F32)


def _silu(x):
    return x * jax.nn.sigmoid(x)


def _silu_grad(x):
    s = jax.nn.sigmoid(x)
    return s * (1.0 + x * (1.0 - s))


def _shift_down(x, prev8, k):
    r = pltpu.roll(x, k, axis=0)
    p = pltpu.roll(prev8, k, axis=0)
    row = lax.broadcasted_iota(jnp.int32, prev8.shape, 0)
    head = jnp.where(row < k, p, r[:8])
    return jnp.concatenate([head, r[8:]], axis=0)


def _shift_up(x, next8, k):
    n = x.shape[0]
    r = pltpu.roll(x, n - k, axis=0)
    p = pltpu.roll(next8, 8 - k, axis=0)
    row = lax.broadcasted_iota(jnp.int32, next8.shape, 0)
    tail = jnp.where(row >= 8 - k, p, r[n - 8:])
    return jnp.concatenate([r[:n - 8], tail], axis=0)


def _causal_conv(x, prev8, w_ref, width):
    y = x * w_ref[width - 1:width, :]
    for k in range(1, width):
        y = y + _shift_down(x, prev8, k) * w_ref[width - 1 - k:width - k, :]
    return y


def _causal_conv_bwd(x, prev8, dy, dnext8, w_ref, dw_ref, width):
    dx = dy * w_ref[width - 1:width, :]
    dw_ref[width - 1:width, :] += jnp.sum(dy * x, axis=0, keepdims=True)
    for k in range(1, width):
        dx = dx + _shift_up(dy, dnext8, k) * w_ref[width - 1 - k:width - k, :]
        dw_ref[width - 1 - k:width - k, :] += jnp.sum(dy * _shift_down(x, prev8, k), axis=0, keepdims=True)
    return dx


def _prev_spec(tt, width):
    return pl.BlockSpec((8, width), lambda i: (jnp.maximum(i * (tt // 8) - 1, 0), 0))


def _gdn_pre_fwd(xg, conv_w, seg):
    t = xg.shape[0]
    c3 = 3 * WIDTH
    tt = _pick(t, (320, 256, 128))

    def body(x_ref, p_ref, w_ref, e_ref, q_ref, k_ref, v_ref):
        x = x_ref[...]
        prev = jnp.where(pl.program_id(0) == 0, 0.0, p_ref[...])
        s = _silu(_causal_conv(x, prev, w_ref, GDN_CONV))
        e = e_ref[...]
        q = s[:, :WIDTH]
        k = s[:, WIDTH:2 * WIDTH]
        q_ref[...] = q * lax.rsqrt(_segsum(q * q, e) + RMS_EPS) * (HEAD_DIM ** -0.5)
        k_ref[...] = k * lax.rsqrt(_segsum(k * k, e) + RMS_EPS)
        v_ref[...] = s[:, 2 * WIDTH:]

    out = pl.BlockSpec((tt, WIDTH), lambda i: (i, 0))
    shp = jax.ShapeDtypeStruct((t, WIDTH), F32)
    return pl.pallas_call(
        body, grid=(t // tt,), name="gdn_pre_fwd",
        in_specs=[pl.BlockSpec((tt, c3), lambda i: (i, 0)), _prev_spec(tt, c3),
                  pl.BlockSpec((GDN_CONV, c3), lambda i: (0, 0)), pl.BlockSpec((WIDTH, WIDTH), lambda i: (0, 0))],
        out_specs=[out, out, out], out_shape=[shp, shp, shp],
        compiler_params=_params("arbitrary"),
    )(xg, xg, conv_w, seg)


def _gdn_pre_bwd(xg, conv_w, seg, dqn, dkn, dv):
    t = xg.shape[0]
    c3 = 3 * WIDTH
    tt = _pick(t, (320, 256, 128))
    nt = t // tt

    def body(x_ref, p_ref, w_ref, e_ref, dq_ref, dk_ref, dv_ref, dx_ref, dw_ref, carry_ref):
        step = pl.program_id(0)
        x = x_ref[...]
        e = e_ref[...]
        prev = jnp.where(step == nt - 1, 0.0, p_ref[...])
        y = _causal_conv(x, prev, w_ref, GDN_CONV)
        s = _silu(y)
        q = s[:, :WIDTH]
        k = s[:, WIDTH:2 * WIDTH]
        rq = lax.rsqrt(_segsum(q * q, e) + RMS_EPS)
        rk = lax.rsqrt(_segsum(k * k, e) + RMS_EPS)
        gq = dq_ref[...] * (HEAD_DIM ** -0.5)
        gk = dk_ref[...]
        dq = rq * gq - q * (rq * rq * rq) * _segsum(gq * q, e)
        dk = rk * gk - k * (rk * rk * rk) * _segsum(gk * k, e)
        dy = jnp.concatenate([dq, dk, dv_ref[...]], axis=1) * _silu_grad(y)

        @pl.when(step == 0)
        def _():
            carry_ref[...] = jnp.zeros_like(carry_ref)
            dw_ref[...] = jnp.zeros_like(dw_ref)

        dx = _causal_conv_bwd(x, prev, dy, carry_ref[...], w_ref, dw_ref, GDN_CONV)
        dx_ref[...] = dx.astype(dx_ref.dtype)
        carry_ref[...] = dy[:8]

    rev = lambda i: (nt - 1 - i, 0)
    blk = pl.BlockSpec((tt, WIDTH), rev)
    return pl.pallas_call(
        body, grid=(nt,), name="gdn_pre_bwd",
        in_specs=[pl.BlockSpec((tt, c3), rev),
                  pl.BlockSpec((8, c3), lambda i: (jnp.maximum((nt - 1 - i) * (tt // 8) - 1, 0), 0)),
                  pl.BlockSpec((GDN_CONV, c3), lambda i: (0, 0)), pl.BlockSpec((WIDTH, WIDTH), lambda i: (0, 0)),
                  blk, blk, blk],
        out_specs=[pl.BlockSpec((tt, c3), rev), pl.BlockSpec((GDN_CONV, c3), lambda i: (0, 0))],
        out_shape=[jax.ShapeDtypeStruct((t, c3), BF16), jax.ShapeDtypeStruct((GDN_CONV, c3), F32)],
        scratch_shapes=[pltpu.VMEM((8, c3), F32)],
        compiler_params=_params("arbitrary"),
    )(xg, xg, conv_w, seg, dqn, dkn, dv)


def _bmm(a, b, ca, cb, precision=None):
    return lax.dot_general(a, b, (((ca,), (cb,)), ((0,), (0,))), precision=precision, preferred_element_type=F32)


def _bf(x):
    return x.astype(BF16)


def _tri_inverse(a, eye):
    x = -a
    tinv = eye + x
    pw = x
    for _ in range(5):
        pb = _bf(pw)
        pw = _bmm(pb, pb, 2, 1)
        tinv = tinv + _bmm(_bf(tinv), _bf(pw), 2, 1)
    resid = eye - _bmm(eye + a, tinv, 2, 1, precision=HI)
    return tinv + _bmm(_bf(tinv), _bf(resid), 2, 1)


def _gdn_intra(q, k, v, bc, gcc, gcr):
    ii = lax.broadcasted_iota(jnp.int32, (CHUNK, CHUNK), 0)
    jj = lax.broadcasted_iota(jnp.int32, (CHUNK, CHUNK), 1)
    tril = (ii >= jj)[None]
    strict = (ii > jj)[None]
    eye = jnp.where(ii == jj, 1.0, 0.0).astype(F32)[None]
    last = (ii == CHUNK - 1)[None]
    dm = jnp.exp(jnp.where(tril, gcc - gcr, NEG))
    gam = jnp.exp(gcc)
    kb = k * bc
    vb = v * bc
    kk = _bmm(_bf(kb), _bf(k), 2, 2)
    a = jnp.where(strict, kk * dm, 0.0)
    tinv = _tri_inverse(a, eye)
    kbg = kb * gam
    u = _bmm(tinv, vb, 2, 1, precision=HI)
    wk = _bmm(tinv, kbg, 2, 1, precision=HI)
    qk = _bmm(_bf(q), _bf(k), 2, 2)
    p = jnp.where(tril, qk * dm, 0.0)
    gl = jnp.sum(jnp.where(last, gcc, 0.0), axis=1, keepdims=True)
    edec = jnp.exp(gl - gcc)
    return dict(tril=tril, strict=strict, dm=dm, gam=gam, kb=kb, kk=kk, a=a, tinv=tinv, u=u, wk=wk, qk=qk, p=p,
                qg=q * gam, kt=k * edec, edec=edec, gaml=jnp.exp(gl), last=last)


def _gdn_fwd(q, k, v, bc, gcc, gcr, nb=None):
    h, t, dh = q.shape
    nc = t // CHUNK
    nb = nb or _pick(nc, (4, 2, 1))
    bsz = h * nb

    def body(q_ref, k_ref, v_ref, b_ref, gc_ref, gr_ref, o_ref, s0_ref, state_ref):
        @pl.when(pl.program_id(0) == 0)
        def _():
            state_ref[...] = jnp.zeros_like(state_ref)

        ld = lambda r: r[...].reshape(bsz, CHUNK, dh)
        z = _gdn_intra(ld(q_ref), ld(k_ref), ld(v_ref), ld(b_ref), ld(gc_ref), ld(gr_ref))
        per = lambda x: x.reshape((h, nb) + x.shape[1:])
        u, wk, p, qg, kt, gaml = (per(z[n]) for n in ("u", "wk", "p", "qg", "kt", "gaml"))
        s = state_ref[...]
        for n in range(nb):
            s0_ref[:, n] = s
            sb = _bf(s)
            vn = u[:, n] - _bmm(_bf(wk[:, n]), sb, 2, 1)
            o_ref[:, n * CHUNK:(n + 1) * CHUNK, :] = _bmm(_bf(qg[:, n]), sb, 2, 1) + _bmm(_bf(p[:, n]), _bf(vn), 2, 1)
            s = s * gaml[:, n] + _bmm(_bf(kt[:, n]), _bf(vn), 1, 1)
        state_ref[...] = s

    blk = pl.BlockSpec((h, nb * CHUNK, dh), lambda i: (0, i, 0))
    return pl.pallas_call(
        body, grid=(nc // nb,), name="gdn_fwd",
        in_specs=[blk] * 6,
        out_specs=[blk, pl.BlockSpec((h, nb, dh, dh), lambda i: (0, i, 0, 0))],
        out_shape=[jax.ShapeDtypeStruct((h, t, dh), F32), jax.ShapeDtypeStruct((h, nc, dh, dh), F32)],
        scratch_shapes=[pltpu.VMEM((h, dh, dh), F32)],
        compiler_params=_params("arbitrary"),
    )(q, k, v, bc, gcc, gcr)


def _gdn_bwd(q, k, v, bc, gcc, gcr, s0s, do, nb=None):
    h, t, dh = q.shape
    nc = t // CHUNK
    nb = nb or _pick(nc, (2, 1))
    bsz = h * nb
    ng = nc // nb

    def body(q_ref, k_ref, v_ref, b_ref, gc_ref, gr_ref, s0_ref, do_ref,
             dq_ref, dk_ref, dv_ref, db_ref, dgc_ref, dgr_ref, ds_ref):
        @pl.when(pl.program_id(0) == 0)
        def _():
            ds_ref[...] = jnp.zeros_like(ds_ref)

        ld = lambda r: r[...].reshape(bsz, CHUNK, dh)
        q, k, v, bc, gcc = ld(q_ref), ld(k_ref), ld(v_ref), ld(b_ref), ld(gc_ref)
        z = _gdn_intra(q, k, v, bc, gcc, ld(gr_ref))
        per = lambda x: x.reshape((h, nb) + x.shape[1:])
        u, wk, p, qg, kt, gaml = (per(z[n]) for n in ("u", "wk", "p", "qg", "kt", "gaml"))
        dout = per(ld(do_ref))
        ds = ds_ref[...]
        d_u, d_wk, d_p, d_qg, d_kt, d_gaml = ([None] * nb for _ in range(6))
        for n in reversed(range(nb)):
            s0 = s0_ref[:, n]
            s0b, dsb, dob = _bf(s0), _bf(ds), _bf(dout[:, n])
            wkb, qgb = _bf(wk[:, n]), _bf(qg[:, n])
            vn = u[:, n] - _bmm(wkb, s0b, 2, 1)
            dvn = _bmm(_bf(p[:, n]), dob, 1, 1) + _bmm(_bf(kt[:, n]), dsb, 2, 1)
            dvnb = _bf(dvn)
            d_u[n] = dvn
            d_p[n] = _bmm(dob, _bf(vn), 2, 2)
            d_qg[n] = _bmm(dob, s0b, 2, 2)
            d_kt[n] = _bmm(_bf(vn), dsb, 2, 2)
            d_gaml[n] = jnp.sum(s0 * ds, axis=1, keepdims=True)
            d_wk[n] = -_bmm(dvnb, s0b, 2, 2)
            ds = _bmm(qgb, dob, 1, 1) + gaml[:, n] * ds - _bmm(wkb, dvnb, 1, 1)
        ds_ref[...] = ds

        flat = lambda xs: jnp.stack(xs, axis=1).reshape((bsz,) + xs[0].shape[1:])
        d_u, d_wk, d_p, d_qg, d_kt, d_gaml = (flat(x) for x in (d_u, d_wk, d_p, d_qg, d_kt, d_gaml))
        tinv, gam, kb, dm = z["tinv"], z["gam"], z["kb"], z["dm"]
        drv = _bmm(tinv, d_u, 1, 1, precision=HI)
        drk = _bmm(tinv, d_wk, 1, 1, precision=HI)
        da = -(_bmm(_bf(drv), _bf(z["u"]), 2, 2) + _bmm(_bf(drk), _bf(z["wk"]), 2, 2))
        da = jnp.where(z["strict"], da, 0.0)
        d_p = jnp.where(z["tril"], d_p, 0.0)
        dkk = _bf(da * dm)
        dqk = _bf(d_p * dm)
        dkb = _bmm(dkk, _bf(k), 2, 1) + drk * gam
        dk = _bmm(dkk, _bf(kb), 1, 1) + _bmm(dqk, _bf(q), 1, 1) + dkb * bc + d_kt * z["edec"]
        dq = _bmm(dqk, _bf(k), 2, 1) + d_qg * gam
        mm = da * z["a"] + d_p * z["p"]
        lsum = lambda x: jnp.sum(x, axis=2, keepdims=True)
        dkt_kt = lsum(d_kt * z["kt"])
        dgl = jnp.sum(dkt_kt, axis=1, keepdims=True) + lsum(d_gaml * z["gaml"])
        dgc = lsum(mm) + lsum(d_qg * z["qg"]) + lsum(drk * kb * gam) - dkt_kt
        last_row = (lax.broadcasted_iota(jnp.int32, (CHUNK, 1), 0) == CHUNK - 1)[None]
        dgc = dgc + jnp.where(last_row, dgl, 0.0)
        dq_ref[...] = dq.reshape(h, nb * CHUNK, dh)
        dk_ref[...] = dk.reshape(h, nb * CHUNK, dh)
        dv_ref[...] = (drv * bc).reshape(h, nb * CHUNK, dh)
        db_ref[...] = jnp.broadcast_to(lsum(dkb * k + drv * v), (bsz, CHUNK, dh)).reshape(h, nb * CHUNK, dh)
        dgc_ref[...] = jnp.broadcast_to(dgc, (bsz, CHUNK, dh)).reshape(h, nb * CHUNK, dh)
        dgr_ref[...] = (-jnp.sum(mm, axis=1, keepdims=True)).reshape(h, nb, 1, CHUNK)

    blk = pl.BlockSpec((h, nb * CHUNK, dh), lambda i: (0, ng - 1 - i, 0))
    shp = jax.ShapeDtypeStruct((h, t, dh), F32)
    return pl.pallas_call(
        body, grid=(ng,), name="gdn_bwd",
        in_specs=[blk] * 6 + [pl.BlockSpec((h, nb, dh, dh), lambda i: (0, ng - 1 - i, 0, 0)), blk],
        out_specs=[blk] * 5 + [pl.BlockSpec((h, nb, 1, CHUNK), lambda i: (0, ng - 1 - i, 0, 0))],
        out_shape=[shp] * 5 + [jax.ShapeDtypeStruct((h, nc, 1, CHUNK), F32)],
        scratch_shapes=[pltpu.VMEM((h, dh, dh), F32)],
        compiler_params=_params("arbitrary"),
    )(q, k, v, bc, gcc, gcr, s0s, do)


def _gdn_post_fwd(o, xg, gain, seg):
    t = o.shape[0]
    tt = _pick(t, ROW_TILES)

    def body(o_ref, z_ref, g_ref, e_ref, y_ref):
        x = o_ref[...]
        r = lax.rsqrt(_segsum(x * x, e_ref[...]) * (1.0 / HEAD_DIM) + RMS_EPS)
        y_ref[...] = (x * r * g_ref[...] * _silu(z_ref[...])).astype(y_ref.dtype)

    return pl.pallas_call(
        body, grid=(t // tt,), name="gdn_post_fwd",
        in_specs=[pl.BlockSpec((tt, WIDTH), lambda i: (i, 0)), pl.BlockSpec((tt, WIDTH), lambda i: (i, 3)),
                  pl.BlockSpec((1, WIDTH), lambda i: (0, 0)), pl.BlockSpec((WIDTH, WIDTH), lambda i: (0, 0))],
        out_specs=pl.BlockSpec((tt, WIDTH), lambda i: (i, 0)),
        out_shape=jax.ShapeDtypeStruct((t, WIDTH), BF16),
        compiler_params=_params("arbitrary"),
    )(o, xg, gain, seg)


def _gdn_post_bwd(o, xg, gain, seg, dy):
    t = o.shape[0]
    tt = _pick(t, ROW_TILES)

    def body(o_ref, z_ref, g_ref, e_ref, dy_ref, do_ref, dz_ref, dg_ref):
        x = o_ref[...]
        zz = z_ref[...]
        e = e_ref[...]
        gain_v = g_ref[...]
        d = dy_ref[...]
        r = lax.rsqrt(_segsum(x * x, e) * (1.0 / HEAD_DIM) + RMS_EPS)
        xr = x * r
        don = d * _silu(zz)
        dz_ref[...] = (d * xr * gain_v * _silu_grad(zz)).astype(dz_ref.dtype)
        gy = don * gain_v
        do_ref[...] = r * gy - xr * (r * r) * (_segsum(gy * x, e) * (1.0 / HEAD_DIM))

        @pl.when(pl.program_id(0) == 0)
        def _():
            dg_ref[...] = jnp.zeros_like(dg_ref)

        dg_ref[...] += jnp.sum(don * xr, axis=0, keepdims=True)

    row = pl.BlockSpec((tt, WIDTH), lambda i: (i, 0))
    vec = pl.BlockSpec((1, WIDTH), lambda i: (0, 0))
    return pl.pallas_call(
        body, grid=(t // tt,), name="gdn_post_bwd",
        in_specs=[row, pl.BlockSpec((tt, WIDTH), lambda i: (i, 3)), vec,
                  pl.BlockSpec((WIDTH, WIDTH), lambda i: (0, 0)), row],
        out_specs=[row, row, vec],
        out_shape=[jax.ShapeDtypeStruct((t, WIDTH), F32), jax.ShapeDtypeStruct((t, WIDTH), BF16),
                   jax.ShapeDtypeStruct((1, WIDTH), F32)],
        compiler_params=_params("arbitrary"),
    )(o, xg, gain, seg, dy)


def _head_dot(a, b, seg):
    t = a.shape[0]
    tt = _pick(t, ROW_TILES)

    def body(a_ref, b_ref, e_ref, o_ref):
        o_ref[...] = _segsum(a_ref[...] * b_ref[...].astype(F32), e_ref[...])

    row = pl.BlockSpec((tt, WIDTH), lambda i: (i, 0))
    return pl.pallas_call(
        body, grid=(t // tt,), name="head_dot",
        in_specs=[row, row, pl.BlockSpec((WIDTH, WIDTH), lambda i: (0, 0))], out_specs=row,
        out_shape=jax.ShapeDtypeStruct((t, WIDTH), F32),
        compiler_params=_params("arbitrary"),
    )(a, b, seg)


def _mix_fwd(yf, yg, gates, bias):
    t, d = yf.shape
    tt = _pick(t, (320, 256, 128))

    def body(yf_ref, yg_ref, g1_ref, g2_ref, b1_ref, b2_ref, o_ref):
        g1 = jax.nn.sigmoid(g1_ref[...] + b1_ref[...])
        g2 = jax.nn.sigmoid(g2_ref[...] + b2_ref[...])
        o_ref[...] = (g1 * yf_ref[...] + g2 * yg_ref[...]).astype(o_ref.dtype)

    row = pl.BlockSpec((tt, d), lambda i: (i, 0))
    return pl.pallas_call(
        body, grid=(t // tt,), name="mix_fwd",
        in_specs=[row, row, row, pl.BlockSpec((tt, d), lambda i: (i, 1)),
                  pl.BlockSpec((1, d), lambda i: (0, 0)), pl.BlockSpec((1, d), lambda i: (0, 1))],
        out_specs=row, out_shape=jax.ShapeDtypeStruct((t, d), BF16),
        compiler_params=_params("arbitrary"),
    )(yf, yg, gates, gates, bias, bias)


def _mix_bwd(dmix, yf, yg, gates, bias):
    t, d = yf.shape
    tt = _pick(t, (320, 256, 128))

    def body(dm_ref, yf_ref, yg_ref, g1_ref, g2_ref, b1_ref, b2_ref, dyf_ref, dyg_ref, dg_ref, db_ref):
        dm = dm_ref[...]
        g1 = jax.nn.sigmoid(g1_ref[...] + b1_ref[...])
        g2 = jax.nn.sigmoid(g2_ref[...] + b2_ref[...])
        dyf_ref[...] = (dm * g1).astype(BF16)
        dyg_ref[...] = (dm * g2).astype(BF16)
        dgate = jnp.concatenate([dm * yf_ref[...] * g1 * (1.0 - g1), dm * yg_ref[...] * g2 * (1.0 - g2)], axis=1)
        dg_ref[...] = dgate.astype(BF16)

        @pl.when(pl.program_id(0) == 0)
        def _():
            db_ref[...] = jnp.zeros_like(db_ref)

        db_ref[...] += jnp.sum(dgate, axis=0, keepdims=True)

    row = pl.BlockSpec((tt, d), lambda i: (i, 0))
    wide = pl.BlockSpec((tt, 2 * d), lambda i: (i, 0))
    return pl.pallas_call(
        body, grid=(t // tt,), name="mix_bwd",
        in_specs=[row, row, row, row, pl.BlockSpec((tt, d), lambda i: (i, 1)),
                  pl.BlockSpec((1, d), lambda i: (0, 0)), pl.BlockSpec((1, d), lambda i: (0, 1))],
        out_specs=[row, row, wide, pl.BlockSpec((1, 2 * d), lambda i: (0, 0))],
        out_shape=[jax.ShapeDtypeStruct((t, d), BF16), jax.ShapeDtypeStruct((t, d), BF16),
                   jax.ShapeDtypeStruct((t, 2 * d), BF16), jax.ShapeDtypeStruct((1, 2 * d), F32)],
        compiler_params=_params("arbitrary"),
    )(dmix, yf, yg, gates, gates, bias, bias)


def _ffn_act_fwd(up, conv_w, conv_b):
    t, c = up.shape
    tt = 128

    def body(x_ref, p_ref, w_ref, b_ref, o_ref):
        prev = jnp.where(pl.program_id(0) == 0, 0.0, p_ref[...])
        u = _causal_conv(x_ref[...], prev, w_ref, FFN_CONV) + b_ref[...]
        o_ref[...] = (_silu(u[:, :D_FF]) * u[:, D_FF:]).astype(o_ref.dtype)

    return pl.pallas_call(
        body, grid=(t // tt,), name="ffn_act_fwd",
        in_specs=[pl.BlockSpec((tt, c), lambda i: (i, 0)), _prev_spec(tt, c),
                  pl.BlockSpec((FFN_CONV, c), lambda i: (0, 0)), pl.BlockSpec((1, c), lambda i: (0, 0))],
        out_specs=pl.BlockSpec((tt, D_FF), lambda i: (i, 0)),
        out_shape=jax.ShapeDtypeStruct((t, D_FF), BF16),
        compiler_params=_params("arbitrary"),
    )(up, up, conv_w, conv_b)


def _ffn_act_bwd(up, conv_w, conv_b, dact):
    t, c = up.shape
    tt = 128
    nt = t // tt

    def body(x_ref, p_ref, w_ref, b_ref, da_ref, dx_ref, dw_ref, db_ref, carry_ref):
        step = pl.program_id(0)
        x = x_ref[...]
        prev = jnp.where(step == nt - 1, 0.0, p_ref[...])
        u = _causal_conv(x, prev, w_ref, FFN_CONV) + b_ref[...]
        gate, val = u[:, :D_FF], u[:, D_FF:]
        da = da_ref[...]
        du = jnp.concatenate([da * val * _silu_grad(gate), da * _silu(gate)], axis=1)

        @pl.when(step == 0)
        def _():
            carry_ref[...] = jnp.zeros_like(carry_ref)
            dw_ref[...] = jnp.zeros_like(dw_ref)
            db_ref[...] = jnp.zeros_like(db_ref)

        dx = _causal_conv_bwd(x, prev, du, carry_ref[...], w_ref, dw_ref, FFN_CONV)
        dx_ref[...] = dx.astype(dx_ref.dtype)
        db_ref[...] += jnp.sum(du, axis=0, keepdims=True)
        carry_ref[...] = du[:8]

    rev = lambda i: (nt - 1 - i, 0)
    return pl.pallas_call(
        body, grid=(nt,), name="ffn_act_bwd",
        in_specs=[pl.BlockSpec((tt, c), rev),
                  pl.BlockSpec((8, c), lambda i: (jnp.maximum((nt - 1 - i) * (tt // 8) - 1, 0), 0)),
                  pl.BlockSpec((FFN_CONV, c), lambda i: (0, 0)), pl.BlockSpec((1, c), lambda i: (0, 0)),
                  pl.BlockSpec((tt, D_FF), rev)],
        out_specs=[pl.BlockSpec((tt, c), rev), pl.BlockSpec((FFN_CONV, c), lambda i: (0, 0)),
                   pl.BlockSpec((1, c), lambda i: (0, 0))],
        out_shape=[jax.ShapeDtypeStruct((t, c), BF16), jax.ShapeDtypeStruct((FFN_CONV, c), F32),
                   jax.ShapeDtypeStruct((1, c), F32)],
        scratch_shapes=[pltpu.VMEM((8, c), F32)],
        compiler_params=_params("arbitrary"),
    )(up, up, conv_w, conv_b, dact)


def _final_loss(h2, target, gain, seq):
    t, d = h2.shape
    tr = _pick(t, (320, 256, 128))

    def body(h_ref, t_ref, g_ref, loss_ref, dh_ref, dhb_ref, dg_ref):
        i = pl.program_id(0)
        x = h_ref[...]
        gain_v = g_ref[...]
        r = lax.rsqrt(jnp.mean(x * x, axis=-1, keepdims=True) + RMS_EPS)
        xr = x * r
        rows = i * tr + lax.broadcasted_iota(jnp.int32, (tr, 1), 0)
        real = (rows >= N_META) & (rows < N_META + seq)
        err = jnp.where(real, xr * gain_v - t_ref[...], 0.0)
        dy = err * (1.0 / d)
        gy = dy * gain_v
        dh = r * (gy - xr * jnp.mean(gy * xr, axis=-1, keepdims=True))
        dh_ref[...] = dh
        dhb_ref[...] = dh.astype(BF16)

        @pl.when(i == 0)
        def _():
            loss_ref[...] = jnp.zeros_like(loss_ref)
            dg_ref[...] = jnp.zeros_like(dg_ref)

        part = jnp.sum(jnp.sum(err * err, axis=-1, keepdims=True), axis=0, keepdims=True)
        loss_ref[...] += jnp.broadcast_to(part * (0.5 / d), loss_ref.shape)
        dg_ref[...] += jnp.sum(dy * xr, axis=0, keepdims=True)

    row = pl.BlockSpec((tr, d), lambda i: (i, 0))
    vec = pl.BlockSpec((1, d), lambda i: (0, 0))
    return pl.pallas_call(
        body, grid=(t // tr,), name="final_loss",
        in_specs=[row, row, vec],
        out_specs=[pl.BlockSpec((1, LANES), lambda i: (0, 0)), row, row, vec],
        out_shape=[jax.ShapeDtypeStruct((1, LANES), F32), jax.ShapeDtypeStruct((t, d), F32),
                   jax.ShapeDtypeStruct((t, d), BF16), jax.ShapeDtypeStruct((1, d), F32)],
        compiler_params=_params("arbitrary"),
    )(h2, target, gain)


def _adamw(w, m, v, grecv, name):
    r = w.shape[0]
    tr = _pick(r, (1024, 512, 256, 128, 64, 32, 16, 8))

    def body(w_ref, m_ref, v_ref, g_ref, go_ref, d_ref, mo_ref, vo_ref):
        g = g_ref[0].astype(F32)
        for s in range(1, N_DEV):
            g = g + g_ref[s].astype(F32)
        wv = w_ref[...]
        mn = ADAM_B1 * m_ref[...] + (1.0 - ADAM_B1) * g
        vn = ADAM_B2 * v_ref[...] + (1.0 - ADAM_B2) * (g * g)
        m_hat = mn / (1.0 - ADAM_B1 ** ADAM_STEP)
        v_hat = vn / (1.0 - ADAM_B2 ** ADAM_STEP)
        go_ref[...] = g
        d_ref[...] = -ADAM_LR * (m_hat / (jnp.sqrt(v_hat) + ADAM_EPS) + ADAM_WD * wv)
        mo_ref[...] = mn
        vo_ref[...] = vn

    row = pl.BlockSpec((tr, LANES), lambda i: (i, 0))
    shp = jax.ShapeDtypeStruct((r, LANES), F32)
    return pl.pallas_call(
        body, grid=(r // tr,), name=name,
        in_specs=[row, row, row, pl.BlockSpec((N_DEV, tr, LANES), lambda i: (0, i, 0))],
        out_specs=[row] * 4, out_shape=[shp] * 4,
        compiler_params=_params("parallel"),
    )(w, m, v, grecv)


def _mesh_pos():
    return lax.axis_index("x"), lax.axis_index("y"), lax.axis_index("c")


def _all_gather(shard):
    r = shard.shape[0]

    def body(x_ref, out_ref, send_sems, recv_sems, local_sem):
        x, y, c = _mesh_pos()
        me, sibling = (x, y, c), (x, y, 1 - c)
        chips = [(1 - x, y), (x, 1 - y), (1 - x, 1 - y)]

        def slot(px, py, pc):
            return out_ref.at[4 * px + 2 * py + pc]

        def copy(kk, block, to, src=None):
            return pltpu.make_async_remote_copy(
                src_ref=slot(*block) if src is None else src, dst_ref=slot(*block),
                send_sem=send_sems.at[kk], recv_sem=recv_sems.at[kk], device_id=to, device_id_type=MESH_ID)

        mine = pltpu.make_async_copy(x_ref, slot(*me), local_sem)
        mine.start()
        first = [copy(0, me, sibling, src=x_ref)]
        first += [copy(1 + j, me, (*chip, c), src=x_ref) for j, chip in enumerate(chips)]
        for cp in first:
            cp.start()
        passed = [copy(4 + j, (*chip, c), sibling) for j, chip in enumerate(chips)]
        for j, chip in enumerate(chips):
            copy(1 + j, (*chip, c), me).wait_recv()
            passed[j].start()
        copy(0, sibling, me).wait_recv()
        for j, chip in enumerate(chips):
            copy(4 + j, (*chip, 1 - c), me).wait_recv()
        for cp in first + passed:
            cp.wait_send()
        mine.wait()

    return pl.pallas_call(
        body, name="weight_all_gather",
        in_specs=[pl.BlockSpec(memory_space=pl.ANY)], out_specs=pl.BlockSpec(memory_space=pl.ANY),
        out_shape=jax.ShapeDtypeStruct((N_DEV, r, LANES), shard.dtype),
        scratch_shapes=[pltpu.SemaphoreType.DMA((7,)), pltpu.SemaphoreType.DMA((7,)), pltpu.SemaphoreType.DMA],
    )(shard)


def _grad_exchange(big, small):
    def body(big_ref, small_ref, rbig_ref, rsmall_ref, send_sems, recv_sems, local_sems):
        x, y, c = _mesh_pos()
        me = 4 * x + 2 * y + c
        copies = []
        for kk in range(1, N_DEV):
            px = 1 - x if kk & 4 else x
            py = 1 - y if kk & 2 else y
            pc = 1 - c if kk & 1 else c
            peer = 4 * px + 2 * py + pc
            copies.append(pltpu.make_async_remote_copy(
                src_ref=big_ref.at[peer], dst_ref=rbig_ref.at[me],
                send_sem=send_sems.at[kk - 1], recv_sem=recv_sems.at[kk - 1],
                device_id=(px, py, pc), device_id_type=MESH_ID))
            copies.append(pltpu.make_async_remote_copy(
                src_ref=small_ref, dst_ref=rsmall_ref.at[me],
                send_sem=send_sems.at[6 + kk], recv_sem=recv_sems.at[6 + kk],
                device_id=(px, py, pc), device_id_type=MESH_ID))
        own = [pltpu.make_async_copy(big_ref.at[me], rbig_ref.at[me], local_sems.at[0]),
               pltpu.make_async_copy(small_ref, rsmall_ref.at[me], local_sems.at[1])]
        for cp in own + copies:
            cp.start()
        for cp in copies + own:
            cp.wait()

    hbm = pl.BlockSpec(memory_space=pl.ANY)
    return pl.pallas_call(
        body, name="grad_exchange", in_specs=[hbm, hbm], out_specs=[hbm, hbm],
        out_shape=[jax.ShapeDtypeStruct(big.shape, big.dtype),
                   jax.ShapeDtypeStruct((N_DEV,) + small.shape, small.dtype)],
        scratch_shapes=[pltpu.SemaphoreType.DMA((14,)), pltpu.SemaphoreType.DMA((14,)),
                        pltpu.SemaphoreType.DMA((2,))],
    )(big, small)


def _pad_flat(parts, rows):
    flat = jnp.concatenate([p.reshape(-1) for p in parts])
    return jnp.pad(flat, (0, rows * LANES - flat.shape[0])).reshape(rows, LANES)


def _rows_for(n_elems, mult=1024):
    rows = -(-n_elems // LANES)
    return -(-rows // mult) * mult


SHARDED = ("meta_tokens", "w_in", "gdn_conv_w", "w_branch_fox", "w_branch_gdn", "w_out", "ffn_w_up", "ffn_conv_w",
           "ffn_w_down")
REPLICATED = ("fgt_bias", "gdn_a_log", "gdn_dt_bias", "gdn_norm_w", "gate_bias", "norm_mix_w", "norm_ffn_w",
              "ffn_conv_b", "norm_final_w")
WEIGHTS = ("meta_tokens", "w_in", "fgt_bias", "gdn_conv_w", "gdn_a_log", "gdn_dt_bias", "gdn_norm_w", "gate_bias",
           "w_branch_fox", "w_branch_gdn", "w_out", "norm_mix_w", "norm_ffn_w", "ffn_w_up", "ffn_conv_w",
           "ffn_conv_b", "ffn_w_down", "norm_final_w")


def _unpack(buf, shapes):
    flat = buf.reshape(-1)
    out, off = [], 0
    for s in shapes:
        n = int(np.prod(s))
        out.append(flat[off:off + n].reshape(s))
        off += n
    return out


def _unpack_gathered(buf, shapes):
    flat = buf.reshape(N_DEV, -1)
    out, off = [], 0
    for s in shapes:
        n = int(np.prod(s))
        out.append(flat[:, off:off + n].reshape((N_DEV,) + tuple(s)))
        off += n
    return out


def _cat_cols(g):
    return g.transpose(1, 0, 2).reshape(g.shape[1], -1)


def _col_blocks(full, width):
    return full.reshape(full.shape[0], N_DEV, width).transpose(1, 0, 2)


def _gate_scalars(small, fgt_bias, a_log, dt_bias):
    t = small.shape[0]
    log_f = jax.nn.log_sigmoid(small[:, 0:HEADS] + fgt_bias)
    c = jnp.cumsum(log_f, axis=0)
    beta = jax.nn.sigmoid(small[:, HEADS:2 * HEADS])
    g = -jnp.exp(a_log) * jax.nn.softplus(small[:, 2 * HEADS:3 * HEADS] + dt_bias)
    gc = jnp.cumsum(g.reshape(t // CHUNK, CHUNK, HEADS), axis=1).reshape(t, HEADS)
    return c, beta, gc


def _lane_repeat(x):
    return jnp.broadcast_to(x.T[:, :, None], (HEADS, x.shape[0], HEAD_DIM))


def _local_step(x, target, w):
    seq = x.shape[0]
    t = _padded_tokens(seq)
    pad = t - N_META - seq
    seg = _seg_matrix()
    zrows = jnp.zeros((pad, D_MODEL), F32)
    h0 = jnp.concatenate([w["meta_tokens"], x, zrows], axis=0)
    tgt = jnp.concatenate([jnp.zeros((N_META, D_MODEL), F32), target, zrows], axis=0)

    w_in = w["w_in"]
    o_f, o_g, o_z, o_b, o_a, o_gate = 1536, 1544, 3080, 3592, 3600, 3608
    w_small = jnp.concatenate([w_in[:, o_f:o_f + 8], w_in[:, o_b:o_b + 8], w_in[:, o_a:o_a + 8],
                               jnp.zeros((D_MODEL, LANES - 24), BF16)], axis=1)
    w_r = jnp.concatenate([w_in[:, :1536], w_in[:, o_g:o_z], w_in[:, o_z:o_b], w_in[:, o_gate:], w_small], axis=1)

    a1 = _rmsnorm_fwd(h0, w["norm_mix_w"])
    fq = _mm(a1, w_r[:, :1536], BF16, "proj_fox")
    xg = _mm(a1, w_r[:, 1536:3584], F32, "proj_gdn")
    gt = _mm(a1, w_r[:, 3584:5632], F32, "proj_gates")
    sm = _mm(a1, w_r[:, 5632:], F32, "proj_small")

    (c, beta, gc), gate_vjp = jax.vjp(_gate_scalars, sm, w["fgt_bias"][0], w["gdn_a_log"][0], w["gdn_dt_bias"][0])

    ones = jnp.ones((HEADS, t), BF16)
    zeros = jnp.zeros((HEADS, t), BF16)
    c3 = _split3(c.T)
    nc3 = _split3(-c.T)
    q_h = _heads_major(fq[:, :WIDTH] * jnp.asarray(HEAD_DIM ** -0.5, BF16))
    ka = _aug(_heads_major(fq[:, WIDTH:2 * WIDTH]), [ones] * 3 + nc3 + [ones] * 3)
    va = _aug(_heads_major(fq[:, 2 * WIDTH:]), [ones] * 3)
    oa = _fox_fwd(_aug(q_h, c3 + [ones] * 3 + [zeros] * 3), ka, va)
    o_fox = _tokens_major(oa[:, :, :HEAD_DIM]).astype(BF16)
    lse = oa[:, :, HEAD_DIM]

    qn, kn, vv = _gdn_pre_fwd(xg, w["gdn_conv_w"], seg)
    qh, kh, vh = _heads_major(qn), _heads_major(kn), _heads_major(vv)
    bc = _lane_repeat(beta)
    gcc = _lane_repeat(gc)
    gcr = jnp.broadcast_to(gc.T.reshape(HEADS, t // CHUNK, 1, CHUNK),
                           (HEADS, t // CHUNK, CHUNK, CHUNK)).reshape(HEADS, t, CHUNK)
    og, s0s = _gdn_fwd(qh, kh, vh, bc, gcc, gcr)
    og_t = _tokens_major(og)
    norm_w = jnp.tile(w["gdn_norm_w"], (1, HEADS))
    ogn = _gdn_post_fwd(og_t, xg, norm_w, seg)

    yf = _mm(o_fox, w["w_branch_fox"], F32, "branch_fox")
    yg = _mm(ogn, w["w_branch_gdn"], F32, "branch_gdn")
    mix = _mix_fwd(yf, yg, gt, w["gate_bias"])
    h1 = _mm(mix, w["w_out"], F32, "out_proj", res=h0)
    a2 = _rmsnorm_fwd(h1, w["norm_ffn_w"])
    up = _mm(a2, w["ffn_w_up"], F32, "ffn_up")
    act = _ffn_act_fwd(up, w["ffn_conv_w"], w["ffn_conv_b"])
    h2 = _mm(act, w["ffn_w_down"], F32, "ffn_down", res=h1)
    loss, dh2, dh2b, g_final = _final_loss(h2, tgt, w["norm_final_w"].reshape(1, D_MODEL), seq)

    grads = {"norm_final_w": g_final.reshape(D_MODEL)}
    grads["ffn_w_down"] = _mm_tn(act, dh2b, "wgrad_ffn_down")
    dact = _mm(dh2b, w["ffn_w_down"].T, F32, "dgrad_ffn_down")
    dup, g_cw, g_cb = _ffn_act_bwd(up, w["ffn_conv_w"], w["ffn_conv_b"], dact)
    grads["ffn_conv_w"], grads["ffn_conv_b"] = g_cw, g_cb
    grads["ffn_w_up"] = _mm_tn(a2, dup, "wgrad_ffn_up")
    da2 = _mm(dup, w["ffn_w_up"].T, F32, "dgrad_ffn_up")
    dh1, dh1b, grads["norm_ffn_w"] = _rmsnorm_bwd(h1, da2, w["norm_ffn_w"], dh2)
    grads["w_out"] = _mm_tn(mix, dh1b, "wgrad_out")
    dmix = _mm(dh1b, w["w_out"].T, F32, "dgrad_out")
    dyf, dyg, dgt, grads["gate_bias"] = _mix_bwd(dmix, yf, yg, gt, w["gate_bias"])
    grads["w_branch_fox"] = _mm_tn(o_fox, dyf, "wgrad_branch_fox")
    grads["w_branch_gdn"] = _mm_tn(ogn, dyg, "wgrad_branch_gdn")
    do_fox = _mm(dyf, w["w_branch_fox"].T, F32, "dgrad_branch_fox")
    dogn = _mm(dyg, w["w_branch_gdn"].T, F32, "dgrad_branch_gdn")

    dog_t, dz, g_nw = _gdn_post_bwd(og_t, xg, norm_w, seg, dogn)
    grads["gdn_norm_w"] = g_nw.reshape(HEADS, HEAD_DIM).sum(axis=0)[None]
    dqh, dkh, dvh, dbl, dgcl, dgr = _gdn_bwd(qh, kh, vh, bc, gcc, gcr, s0s, _heads_major(dog_t))
    dxg, grads["gdn_conv_w"] = _gdn_pre_bwd(xg, w["gdn_conv_w"], seg, _tokens_major(dqh), _tokens_major(dkh),
                                            _tokens_major(dvh))
    dbeta = dbl[:, :, 0].T
    dgc = (dgcl[:, :, 0] + dgr.reshape(HEADS, t)).T

    delta = _head_dot(do_fox, _tokens_major(oa[:, :, :HEAD_DIM]), seg)[:, ::HEAD_DIM].T
    doa = _aug(_heads_major(do_fox.astype(BF16)), _split3(-delta))
    qb = _aug(q_h, c3 + [ones] * 3 + _split3(-lse))
    dqa, dka, dva = _fox_bwd(qb, ka, va, doa)
    dq = _tokens_major(dqa[:, :, :HEAD_DIM]) * (HEAD_DIM ** -0.5)
    dfq = jnp.concatenate([dq, _tokens_major(dka[:, :, :HEAD_DIM]), _tokens_major(dva[:, :, :HEAD_DIM])],
                          axis=1).astype(BF16)
    dc = (dqa[:, :, HEAD_DIM] - dka[:, :, HEAD_DIM + 3]).T

    dsm, g_fb, g_al, g_dt = gate_vjp((dc, dbeta, dgc))
    grads["fgt_bias"], grads["gdn_a_log"], grads["gdn_dt_bias"] = g_fb[None], g_al[None], g_dt[None]

    dproj = jnp.concatenate([dfq, dxg, dz, dgt, dsm.astype(BF16)], axis=1)
    g_r = _mm_tn(a1, dproj, "wgrad_in")
    grads["w_in"] = jnp.concatenate([g_r[:, :1536], g_r[:, 5632:5640], g_r[:, 1536:3072], g_r[:, 3072:3584],
                                     g_r[:, 5640:5648], g_r[:, 5648:5656], g_r[:, 3584:5632]], axis=1)
    da1 = _mm(dproj, w_r.T, F32, "dgrad_in")
    dh0, _, grads["norm_mix_w"] = _rmsnorm_bwd(h0, da1, w["norm_mix_w"], dh1)
    grads["meta_tokens"] = dh0[:N_META]
    return loss, dh0[N_META:N_META + seq], grads


def _shard_pieces(arrs):
    return [arrs[n][0] if arrs[n].ndim == 3 else arrs[n] for n in SHARDED]


def _full_grad_blocks(grads):
    g = grads
    cols = lambda a, wd: _col_blocks(a, wd)
    rows = lambda a: a.reshape(N_DEV, a.shape[0] // N_DEV, a.shape[1])
    per = [cols(g["meta_tokens"], 128), cols(g["w_in"], IN_WIDTH // N_DEV), cols(g["gdn_conv_w"], 3 * WIDTH // N_DEV),
           cols(g["w_branch_fox"], D_MODEL // N_DEV), cols(g["w_branch_gdn"], D_MODEL // N_DEV), rows(g["w_out"]),
           cols(g["ffn_w_up"], 2 * D_FF // N_DEV), cols(g["ffn_conv_w"], 2 * D_FF // N_DEV), rows(g["ffn_w_down"])]
    return per


def kernel(x, meta_tokens, w_in, fgt_bias, gdn_conv_w, gdn_a_log, gdn_dt_bias, gdn_norm_w, gate_bias, w_branch_fox, w_branch_gdn, w_out, norm_mix_w, norm_ffn_w, ffn_w_up, ffn_conv_w, ffn_conv_b, ffn_w_down, norm_final_w, loss_target, m_meta_tokens, m_w_in, m_fgt_bias, m_gdn_conv_w, m_gdn_a_log, m_gdn_dt_bias, m_gdn_norm_w, m_gate_bias, m_w_branch_fox, m_w_branch_gdn, m_w_out, m_norm_mix_w, m_norm_ffn_w, m_ffn_w_up, m_ffn_conv_w, m_ffn_conv_b, m_ffn_w_down, m_norm_final_w, v_meta_tokens, v_w_in, v_fgt_bias, v_gdn_conv_w, v_gdn_a_log, v_gdn_dt_bias, v_gdn_norm_w, v_gate_bias, v_w_branch_fox, v_w_branch_gdn, v_w_out, v_norm_mix_w, v_norm_ffn_w, v_ffn_w_up, v_ffn_conv_w, v_ffn_conv_b, v_ffn_w_down, v_norm_final_w):
    wts = dict(meta_tokens=meta_tokens, w_in=w_in, fgt_bias=fgt_bias, gdn_conv_w=gdn_conv_w, gdn_a_log=gdn_a_log,
               gdn_dt_bias=gdn_dt_bias, gdn_norm_w=gdn_norm_w, gate_bias=gate_bias, w_branch_fox=w_branch_fox,
               w_branch_gdn=w_branch_gdn, w_out=w_out, norm_mix_w=norm_mix_w, norm_ffn_w=norm_ffn_w,
               ffn_w_up=ffn_w_up, ffn_conv_w=ffn_conv_w, ffn_conv_b=ffn_conv_b, ffn_w_down=ffn_w_down,
               norm_final_w=norm_final_w)
    mom = dict(meta_tokens=m_meta_tokens, w_in=m_w_in, fgt_bias=m_fgt_bias, gdn_conv_w=m_gdn_conv_w,
               gdn_a_log=m_gdn_a_log, gdn_dt_bias=m_gdn_dt_bias, gdn_norm_w=m_gdn_norm_w, gate_bias=m_gate_bias,
               w_branch_fox=m_w_branch_fox, w_branch_gdn=m_w_branch_gdn, w_out=m_w_out, norm_mix_w=m_norm_mix_w,
               norm_ffn_w=m_norm_ffn_w, ffn_w_up=m_ffn_w_up, ffn_conv_w=m_ffn_conv_w, ffn_conv_b=m_ffn_conv_b,
               ffn_w_down=m_ffn_w_down, norm_final_w=m_norm_final_w)
    var = dict(meta_tokens=v_meta_tokens, w_in=v_w_in, fgt_bias=v_fgt_bias, gdn_conv_w=v_gdn_conv_w,
               gdn_a_log=v_gdn_a_log, gdn_dt_bias=v_gdn_dt_bias, gdn_norm_w=v_gdn_norm_w, gate_bias=v_gate_bias,
               w_branch_fox=v_w_branch_fox, w_branch_gdn=v_w_branch_gdn, w_out=v_w_out, norm_mix_w=v_norm_mix_w,
               norm_ffn_w=v_norm_ffn_w, ffn_w_up=v_ffn_w_up, ffn_conv_w=v_ffn_conv_w, ffn_conv_b=v_ffn_conv_b,
               ffn_w_down=v_ffn_w_down, norm_final_w=v_norm_final_w)

    w_pieces = _shard_pieces(wts)
    shard_shapes = [p.shape for p in w_pieces]
    n_shard = sum(int(np.prod(s)) for s in shard_shapes)
    rows_big = _rows_for(n_shard)
    sh = dict(zip(SHARDED, w_pieces))
    exact = ("meta_tokens", "gdn_conv_w", "ffn_conv_w")
    mm_names = [n for n in SHARDED if n not in exact]
    gather_parts = [sh[n].astype(BF16) for n in mm_names] + [p for n in exact for p in _split3(sh[n])]
    rows_gather = _rows_for(sum(int(np.prod(p.shape)) for p in gather_parts), 16)
    gathered = _all_gather(_pad_flat(gather_parts, rows_gather))
    parts = _unpack_gathered(gathered, [p.shape for p in gather_parts])
    g_in, g_bf, g_bg, g_out, g_up, g_down = parts[:6]
    f32_full = [_cat_cols(parts[6 + 3 * i].astype(F32) + parts[7 + 3 * i].astype(F32) + parts[8 + 3 * i].astype(F32))
                for i in range(len(exact))]
    full = dict(
        meta_tokens=f32_full[0], w_in=_cat_cols(g_in), gdn_conv_w=f32_full[1],
        w_branch_fox=_cat_cols(g_bf), w_branch_gdn=_cat_cols(g_bg), w_out=g_out.reshape(D_MODEL, D_MODEL),
        ffn_w_up=_cat_cols(g_up), ffn_conv_w=f32_full[2], ffn_w_down=g_down.reshape(D_FF, D_MODEL),
        fgt_bias=fgt_bias, gdn_a_log=gdn_a_log, gdn_dt_bias=gdn_dt_bias, gdn_norm_w=gdn_norm_w, gate_bias=gate_bias,
        norm_mix_w=norm_mix_w, norm_ffn_w=norm_ffn_w, ffn_conv_b=ffn_conv_b, norm_final_w=norm_final_w)

    loss, grad_x, grads = _local_step(x[0], loss_target[0], full)

    blocks = _full_grad_blocks(grads)
    big = jnp.stack([_pad_flat([b[j] for b in blocks], rows_big) for j in range(N_DEV)]).astype(BF16)
    rep_parts = [grads[n] for n in REPLICATED] + [loss[:, :1]]
    rep_shapes = [wts[n].shape for n in REPLICATED]
    rows_small = _rows_for(sum(int(np.prod(p.shape)) for p in rep_parts), 8)
    rbig, rsmall = _grad_exchange(big, _pad_flat(rep_parts, rows_small))

    pack = lambda d, names, rows: _pad_flat([d[n][0] if (d[n].ndim == 3 and n in SHARDED) else d[n] for n in names],
                                            rows)
    outs_s = _adamw(pack(wts, SHARDED, rows_big), pack(mom, SHARDED, rows_big), pack(var, SHARDED, rows_big), rbig,
                    "adamw_sharded")
    rep_w = _pad_flat([wts[n] for n in REPLICATED] + [jnp.zeros((1, 1), F32)], rows_small)
    rep_m = _pad_flat([mom[n] for n in REPLICATED] + [jnp.zeros((1, 1), F32)], rows_small)
    rep_v = _pad_flat([var[n] for n in REPLICATED] + [jnp.ones((1, 1), F32)], rows_small)
    outs_r = _adamw(rep_w, rep_m, rep_v, rsmall, "adamw_replicated")

    result = {}
    for kind, bs, br in zip(("grad", "delta", "new_m", "new_v"), outs_s, outs_r):
        for n, a in zip(SHARDED, _unpack(bs, shard_shapes)):
            result[kind, n] = a.reshape(wts[n].shape)
        for n, a in zip(REPLICATED, _unpack(br, rep_shapes)):
            result[kind, n] = a
    n_rep = sum(int(np.prod(s)) for s in rep_shapes)
    total_loss = outs_r[0].reshape(-1)[n_rep]
    out = [total_loss, grad_x[None]]
    for kind in ("grad", "delta", "new_m", "new_v"):
        out += [result[kind, n] for n in WEIGHTS]
    return tuple(out)
```

```python
import functools

import jax
import jax.numpy as jnp
import numpy as np
from jax import lax
from jax.experimental import pallas as pl
from jax.experimental.pallas import tpu as pltpu

F32 = jnp.float32
BF16 = jnp.bfloat16

D_MODEL = 1024
N_META = 16
HEADS = 8
HEAD_DIM = 64
WIDTH = HEADS * HEAD_DIM
CHUNK = 64
GDN_CONV = 4
D_FF = 2816
FFN_CONV = 3
IN_WIDTH = 5656
IN_PAD = 5760
RMS_EPS = 1e-6
NEG = -1e30
AUG = 128
N_DEV = 8
LANES = 128

ADAM_LR = 0.001
ADAM_B1 = 0.9
ADAM_B2 = 0.999
ADAM_EPS = 1e-08
ADAM_WD = 0.01
ADAM_STEP = 10

VMEM_LIMIT = 56 * 1024 * 1024
HI = lax.Precision.HIGH
MESH_ID = pl.DeviceIdType.MESH


def _pick(n, cands):
    for c in cands:
        if n % c == 0:
            return c
    raise ValueError(f"no tile for {n} in {cands}")


def _params(*sem):
    return pltpu.CompilerParams(dimension_semantics=sem if sem else None, vmem_limit_bytes=VMEM_LIMIT)


def _padded_tokens(seq):
    t = -(-(N_META + seq) // 128) * 128
    if t > 1280 and t % 640:
        t = -(-t // 640) * 640
    return t


ROW_TILES = (640, 512, 384, 256, 128)


def _rmsnorm_fwd(h, gain):
    t, d = h.shape
    tr = _pick(t, ROW_TILES)

    def body(h_ref, g_ref, o_ref):
        x = h_ref[...]
        r = lax.rsqrt(jnp.mean(x * x, axis=-1, keepdims=True) + RMS_EPS)
        o_ref[...] = (x * r * g_ref[...]).astype(o_ref.dtype)

    return pl.pallas_call(
        body, grid=(t // tr,), name="rmsnorm_fwd",
        in_specs=[pl.BlockSpec((tr, d), lambda i: (i, 0)), pl.BlockSpec((1, d), lambda i: (0, 0))],
        out_specs=pl.BlockSpec((tr, d), lambda i: (i, 0)),
        out_shape=jax.ShapeDtypeStruct((t, d), BF16),
        compiler_params=_params("arbitrary"),
    )(h, gain)


def _rmsnorm_bwd(h, dy, gain, dres):
    t, d = h.shape
    tr = _pick(t, (320, 256, 128))

    def body(h_ref, dy_ref, g_ref, dres_ref, dh_ref, dhb_ref, dg_ref):
        x = h_ref[...]
        dyv = dy_ref[...]
        r = lax.rsqrt(jnp.mean(x * x, axis=-1, keepdims=True) + RMS_EPS)
        gy = dyv * g_ref[...]
        m = jnp.mean(gy * x, axis=-1, keepdims=True)
        dh = dres_ref[...] + r * gy - x * (r * r * r * m)
        dh_ref[...] = dh
        dhb_ref[...] = dh.astype(BF16)

        @pl.when(pl.program_id(0) == 0)
        def _():
            dg_ref[...] = jnp.zeros_like(dg_ref)

        dg_ref[...] += jnp.sum(dyv * x * r, axis=0, keepdims=True)

    row = pl.BlockSpec((tr, d), lambda i: (i, 0))
    vec = pl.BlockSpec((1, d), lambda i: (0, 0))
    return pl.pallas_call(
        body, grid=(t // tr,), name="rmsnorm_bwd",
        in_specs=[row, row, vec, row], out_specs=[row, row, vec],
        out_shape=[jax.ShapeDtypeStruct((t, d), F32), jax.ShapeDtypeStruct((t, d), BF16),
                   jax.ShapeDtypeStruct((1, d), F32)],
        compiler_params=_params("arbitrary"),
    )(h, dy, gain, dres)


def _mm(a, b, out_dtype, name, res=None):
    m, k = a.shape
    _, n = b.shape
    tm = _pick(m, ROW_TILES)
    tn = _pick(n, (512, 384, 256, 128))
    tk = k if k <= 2048 else _pick(k, (1408, 1152, 1024, 512))
    nk = k // tk

    def body(*refs):
        if res is None:
            a_ref, b_ref, o_ref, acc_ref = refs
        else:
            a_ref, b_ref, r_ref, o_ref, acc_ref = refs
        kk = pl.program_id(2)

        @pl.when(kk == 0)
        def _():
            acc_ref[...] = jnp.zeros_like(acc_ref)

        acc_ref[...] += jnp.dot(a_ref[...], b_ref[...], preferred_element_type=F32)

        @pl.when(kk == nk - 1)
        def _():
            out = acc_ref[...]
            if res is not None:
                out = out + r_ref[...]
            o_ref[...] = out.astype(o_ref.dtype)

    in_specs = [pl.BlockSpec((tm, tk), lambda i, j, kk: (i, kk)), pl.BlockSpec((tk, tn), lambda i, j, kk: (kk, j))]
    args = [a, b]
    if res is not None:
        in_specs.append(pl.BlockSpec((tm, tn), lambda i, j, kk: (i, j)))
        args.append(res)
    return pl.pallas_call(
        body, grid=(m // tm, n // tn, nk), name=name,
        in_specs=in_specs, out_specs=pl.BlockSpec((tm, tn), lambda i, j, kk: (i, j)),
        out_shape=jax.ShapeDtypeStruct((m, n), out_dtype),
        scratch_shapes=[pltpu.VMEM((tm, tn), F32)],
        compiler_params=_params("parallel", "parallel", "arbitrary"),
    )(*args)


def _mm_tn(a, g, name):
    t, k = a.shape
    _, n = g.shape
    tt = _pick(t, ROW_TILES)
    tk = _pick(k, (1024, 1408, 512))
    tn = _pick(n, (512, 640, 384, 256, 128))
    nt = t // tt

    def body(a_ref, g_ref, o_ref):
        @pl.when(pl.program_id(2) == 0)
        def _():
            o_ref[...] = jnp.zeros_like(o_ref)

        o_ref[...] += lax.dot_general(a_ref[...], g_ref[...], (((0,), (0,)), ((), ())),
                                      preferred_element_type=F32)

    return pl.pallas_call(
        body, grid=(k // tk, n // tn, nt), name=name,
        in_specs=[pl.BlockSpec((tt, tk), lambda i, j, s: (s, i)), pl.BlockSpec((tt, tn), lambda i, j, s: (s, j))],
        out_specs=pl.BlockSpec((tk, tn), lambda i, j, s: (i, j)),
        out_shape=jax.ShapeDtypeStruct((k, n), F32),
        compiler_params=_params("parallel", "parallel", "arbitrary"),
    )(a, g)


def _split3_exact(x):
    def top(v):
        return lax.bitcast_convert_type(lax.bitcast_convert_type(v, jnp.int32) & jnp.int32(-65536), F32)

    hi = top(x)
    r1 = x - hi
    mid = top(r1)
    return hi, mid, r1 - mid


def _pair_head(ref, h, rows):
    x = ref[:, 128 * (h // 2):128 * (h // 2) + 128].astype(F32)
    return pltpu.roll(x, HEAD_DIM, axis=1) if h % 2 else x


def _lanes(rows):
    return lax.broadcasted_iota(jnp.int32, (rows, AUG), 1)


def _fox_prep(fq, scal):
    t = fq.shape[0]
    tt = _pick(t, (256, 128))

    def body(q_ref, k_ref, v_ref, s_ref, qa_ref, ka_ref, va_ref):
        lane = _lanes(tt)
        chi, cmid, clo = _split3_exact(s_ref[...])
        ones = lambda lo: jnp.where((lane >= lo) & (lane < lo + 3), 1.0, 0.0)
        for h in range(HEADS):
            col = lambda a: jnp.broadcast_to(a[:, h:h + 1], (tt, AUG))
            c1, c2, c3 = col(chi), col(cmid), col(clo)
            qx = jnp.where(lane == 64, c1, jnp.where(lane == 65, c2, jnp.where(lane == 66, c3, ones(67))))
            kx = jnp.where(lane == 67, -c1, jnp.where(lane == 68, -c2, jnp.where(lane == 69, -c3, ones(64) + ones(70))))
            qa_ref[h] = jnp.where(lane < HEAD_DIM, _pair_head(q_ref, h, tt) * (HEAD_DIM ** -0.5), qx).astype(BF16)
            ka_ref[h] = jnp.where(lane < HEAD_DIM, _pair_head(k_ref, h, tt), kx).astype(BF16)
            va_ref[h] = jnp.where(lane < HEAD_DIM, _pair_head(v_ref, h, tt), ones(64)).astype(BF16)

    out = pl.BlockSpec((HEADS, tt, AUG), lambda i: (0, i, 0))
    shp = jax.ShapeDtypeStruct((HEADS, t, AUG), BF16)
    return pl.pallas_call(
        body, grid=(t // tt,), name="fox_prep",
        in_specs=[pl.BlockSpec((tt, WIDTH), lambda i: (i, 0)), pl.BlockSpec((tt, WIDTH), lambda i: (i, 1)),
                  pl.BlockSpec((tt, WIDTH), lambda i: (i, 2)), pl.BlockSpec((tt, LANES), lambda i: (i, 0))],
        out_specs=[out, out, out], out_shape=[shp, shp, shp],
        compiler_params=_params("parallel"),
    )(fq, fq, fq, scal)


def _fox_post(oa):
    t = oa.shape[1]
    tt = _pick(t, (256, 128))

    def body(o_ref, out_ref):
        out_ref[...] = jnp.concatenate([o_ref[h][:, :HEAD_DIM] for h in range(HEADS)], axis=1).astype(BF16)

    return pl.pallas_call(
        body, grid=(t // tt,), name="fox_post",
        in_specs=[pl.BlockSpec((HEADS, tt, AUG), lambda i: (0, i, 0))],
        out_specs=pl.BlockSpec((tt, WIDTH), lambda i: (i, 0)),
        out_shape=jax.ShapeDtypeStruct((t, WIDTH), BF16),
        compiler_params=_params("parallel"),
    )(oa)


def _fox_bwd_prep(do, oa):
    t = do.shape[0]
    tt = _pick(t, (256, 128))

    def body(d_ref, o_ref, out_ref):
        lane = _lanes(tt)
        for h in range(HEADS):
            x = _pair_head(d_ref, h, tt)
            delta = jnp.sum(jnp.where(lane < HEAD_DIM, x * o_ref[h], 0.0), axis=1, keepdims=True)
            hi, mid, lo = _split3_exact(jnp.broadcast_to(-delta, (tt, AUG)))
            ex = jnp.where(lane == 64, hi, jnp.where(lane == 65, mid, jnp.where(lane == 66, lo, 0.0)))
            out_ref[h] = jnp.where(lane < HEAD_DIM, x, ex).astype(BF16)

    hm = pl.BlockSpec((HEADS, tt, AUG), lambda i: (0, i, 0))
    return pl.pallas_call(
        body, grid=(t // tt,), name="fox_bwd_prep",
        in_specs=[pl.BlockSpec((tt, WIDTH), lambda i: (i, 0)), hm], out_specs=hm,
        out_shape=jax.ShapeDtypeStruct((HEADS, t, AUG), BF16),
        compiler_params=_params("parallel"),
    )(do, oa)


def _fox_bwd_post(dqa, dka, dva):
    t = dqa.shape[1]
    tt = _pick(t, (256, 128))

    def body(dq_ref, dk_ref, dv_ref, out_ref, dsc_ref):
        lane = _lanes(tt)
        heads = lambda ref: jnp.concatenate([ref[h][:, :HEAD_DIM] for h in range(HEADS)], axis=1)
        out_ref[:, 0:WIDTH] = (heads(dq_ref) * (HEAD_DIM ** -0.5)).astype(BF16)
        out_ref[:, WIDTH:2 * WIDTH] = heads(dk_ref).astype(BF16)
        out_ref[:, 2 * WIDTH:] = heads(dv_ref).astype(BF16)
        dsc = jnp.zeros((tt, LANES), F32)
        for h in range(HEADS):
            both = jnp.where(lane == HEAD_DIM, dq_ref[h], 0.0) - jnp.where(lane == HEAD_DIM + 3, dk_ref[h], 0.0)
            dsc = jnp.where(lane == h, jnp.sum(both, axis=1, keepdims=True), dsc)
        dsc_ref[...] = dsc

    hm = pl.BlockSpec((HEADS, tt, AUG), lambda i: (0, i, 0))
    return pl.pallas_call(
        body, grid=(t // tt,), name="fox_bwd_post",
        in_specs=[hm, hm, hm],
        out_specs=[pl.BlockSpec((tt, 3 * WIDTH), lambda i: (i, 0)), pl.BlockSpec((tt, LANES), lambda i: (i, 0))],
        out_shape=[jax.ShapeDtypeStruct((t, 3 * WIDTH), BF16), jax.ShapeDtypeStruct((t, LANES), F32)],
        compiler_params=_params("parallel"),
    )(dqa, dka, dva)


def _fox_fwd(qa, ka, va, tq=None):
    h, t, _ = qa.shape
    tq = tq or _pick(t, ROW_TILES)

    def body(q_ref, k_ref, v_ref, o_ref, qb_ref):
        i = pl.program_id(1)
        q = q_ref[...]
        row = lax.broadcasted_iota(jnp.int32, (tq, tq), 0)
        col = lax.broadcasted_iota(jnp.int32, (tq, tq), 1)

        def step(j, carry, masked):
            m, acc = carry
            start = pl.multiple_of(j * tq, tq)
            kj = k_ref[pl.ds(start, tq), :]
            vj = v_ref[pl.ds(start, tq), :]
            s = lax.dot_general(q, kj, (((1,), (1,)), ((), ())), preferred_element_type=F32)
            if masked:
                s = jnp.where(row >= col, s, NEG)
            m_new = jnp.maximum(m, jnp.max(s, axis=-1, keepdims=True))
            p = jnp.exp(s - m_new)
            alpha = jnp.exp(m - m_new)
            acc = acc * alpha + jnp.dot(p.astype(BF16), vj, preferred_element_type=F32)
            return m_new, acc

        carry = (jnp.full((tq, 1), NEG, F32), jnp.zeros((tq, AUG), F32))
        carry = lax.fori_loop(0, i, lambda j, c: step(j, c, False), carry)
        m, acc = step(i, carry, True)
        lane = lax.broadcasted_iota(jnp.int32, (tq, AUG), 1)
        l = jnp.sum(jnp.where(lane == HEAD_DIM, acc, 0.0), axis=-1, keepdims=True)
        lse = jnp.broadcast_to(m + jnp.log(l), (tq, AUG))
        o_ref[...] = jnp.where(lane < HEAD_DIM, acc / l, lse)
        hi, mid, lo = _split3_exact(-lse)
        qb = jnp.where(lane == 70, hi, jnp.where(lane == 71, mid, jnp.where(lane == 72, lo, q.astype(F32))))
        qb_ref[...] = qb.astype(BF16)

    blk = pl.BlockSpec((None, tq, AUG), lambda hh, i: (hh, i, 0))
    return pl.pallas_call(
        body, grid=(h, t // tq), name="fox_fwd",
        in_specs=[blk, pl.BlockSpec((None, t, AUG), lambda hh, i: (hh, 0, 0)),
                  pl.BlockSpec((None, t, AUG), lambda hh, i: (hh, 0, 0))],
        out_specs=[blk, blk],
        out_shape=[jax.ShapeDtypeStruct((h, t, AUG), F32), jax.ShapeDtypeStruct((h, t, AUG), BF16)],
        compiler_params=_params("parallel", "arbitrary"),
    )(qa, ka, va)


def _fox_bwd(qb, ka, va, doa, tq=None):
    h, t, _ = qb.shape
    tq = tq or _pick(t, ROW_TILES)
    nq = t // tq

    def body(q_ref, k_ref, v_ref, do_ref, dq_ref, dk_ref, dv_ref):
        j = pl.program_id(1)

        @pl.when(j == 0)
        def _():
            dq_ref[...] = jnp.zeros_like(dq_ref)

        kj = k_ref[...]
        vj = v_ref[...]
        krow = lax.broadcasted_iota(jnp.int32, (tq, tq), 0)
        qcol = lax.broadcasted_iota(jnp.int32, (tq, tq), 1)

        def step(i, carry, masked):
            dk, dv = carry
            start = pl.multiple_of(i * tq, tq)
            qi = q_ref[pl.ds(start, tq), :]
            doi = do_ref[pl.ds(start, tq), :]
            st = lax.dot_general(kj, qi, (((1,), (1,)), ((), ())), preferred_element_type=F32)
            if masked:
                st = jnp.where(qcol >= krow, st, NEG)
            pt = jnp.exp(st)
            dpt = lax.dot_general(vj, doi, (((1,), (1,)), ((), ())), preferred_element_type=F32)
            dst = (pt * dpt).astype(BF16)
            dv = dv + jnp.dot(pt.astype(BF16), doi, preferred_element_type=F32)
            dk = dk + jnp.dot(dst, qi, preferred_element_type=F32)
            dq_ref[pl.ds(start, tq), :] += lax.dot_general(dst, kj, (((0,), (0,)), ((), ())),
                                                            preferred_element_type=F32)
            return dk, dv

        zero = jnp.zeros((tq, AUG), F32)
        carry = step(j, (zero, zero), True)
        dk, dv = lax.fori_loop(j + 1, nq, lambda i, c: step(i, c, False), carry)
        dk_ref[...] = dk
        dv_ref[...] = dv

    full = pl.BlockSpec((None, t, AUG), lambda hh, j: (hh, 0, 0))
    blk = pl.BlockSpec((None, tq, AUG), lambda hh, j: (hh, j, 0))
    shp = jax.ShapeDtypeStruct((h, t, AUG), F32)
    return pl.pallas_call(
        body, grid=(h, nq), name="fox_bwd",
        in_specs=[full, blk, blk, full], out_specs=[full, blk, blk], out_shape=[shp, shp, shp],
        compiler_params=_params("parallel", "arbitrary"),
    )(qb, ka, va, doa)


def _split3(x):
    hi = lax.reduce_precision(x, 8, 7)
    r1 = x - hi
    mid = lax.reduce_precision(r1, 8, 7)
    lo = r1 - mid
    return [hi.astype(BF16), mid.astype(BF16), lo.astype(BF16)]


def _seg_matrix():
    idx = np.arange(WIDTH) // HEAD_DIM
    return jnp.asarray((idx[:, None] == idx[None, :]).astype(np.float32))


def _segsum(x, e):
    return jnp.dot(x, e, precision=HI, preferred_element_type=F32)


def _silu(x):
    return x * jax.nn.sigmoid(x)


def _silu_grad(x):
    s = jax.nn.sigmoid(x)
    return s * (1.0 + x * (1.0 - s))


def _shift_down(x, prev8, k):
    r = pltpu.roll(x, k, axis=0)
    p = pltpu.roll(prev8, k, axis=0)
    row = lax.broadcasted_iota(jnp.int32, prev8.shape, 0)
    head = jnp.where(row < k, p, r[:8])
    return jnp.concatenate([head, r[8:]], axis=0)


def _shift_up(x, next8, k):
    n = x.shape[0]
    r = pltpu.roll(x, n - k, axis=0)
    p = pltpu.roll(next8, 8 - k, axis=0)
    row = lax.broadcasted_iota(jnp.int32, next8.shape, 0)
    tail = jnp.where(row >= 8 - k, p, r[n - 8:])
    return jnp.concatenate([r[:n - 8], tail], axis=0)


def _causal_conv(x, prev8, w_ref, width):
    y = x * w_ref[width - 1:width, :]
    for k in range(1, width):
        y = y + _shift_down(x, prev8, k) * w_ref[width - 1 - k:width - k, :]
    return y


def _causal_conv_bwd(x, prev8, dy, dnext8, w_ref, dw_ref, width):
    dx = dy * w_ref[width - 1:width, :]
    dw_ref[width - 1:width, :] += jnp.sum(dy * x, axis=0, keepdims=True)
    for k in range(1, width):
        dx = dx + _shift_up(dy, dnext8, k) * w_ref[width - 1 - k:width - k, :]
        dw_ref[width - 1 - k:width - k, :] += jnp.sum(dy * _shift_down(x, prev8, k), axis=0, keepdims=True)
    return dx


def _prev_spec(tt, width):
    return pl.BlockSpec((8, width), lambda i: (jnp.maximum(i * (tt // 8) - 1, 0), 0))


def _store_heads(ref, x):
    for h in range(HEADS):
        ref[h] = x[:, HEAD_DIM * h:HEAD_DIM * (h + 1)]


def _load_heads(ref):
    return jnp.concatenate([ref[h] for h in range(HEADS)], axis=1)


def _softplus(z):
    return jnp.maximum(z, 0.0) + jnp.log1p(jnp.exp(-jnp.abs(z)))


def _tri_masks(tt):
    r = lax.broadcasted_iota(jnp.int32, (tt, tt), 0)
    c = lax.broadcasted_iota(jnp.int32, (tt, tt), 1)
    same_chunk = lax.shift_right_logical(r, 6) == lax.shift_right_logical(c, 6)
    return r, c, same_chunk


def _gate_fwd(small, pbias, pscale):
    t = small.shape[0]
    tt = _pick(t, (256, 128))

    def body(x_ref, pb_ref, ps_ref, o_ref, carry_ref):
        @pl.when(pl.program_id(0) == 0)
        def _():
            carry_ref[...] = jnp.zeros_like(carry_ref)

        lane = lax.broadcasted_iota(jnp.int32, (tt, LANES), 1)
        z = x_ref[...] + pb_ref[...]
        log_f = jnp.where(lane < HEADS, -_softplus(-z), 0.0)
        g = jnp.where((lane >= 2 * HEADS) & (lane < 3 * HEADS), ps_ref[...] * _softplus(z), 0.0)
        r, c, same_chunk = _tri_masks(tt)
        lower = jnp.where(r >= c, 1.0, 0.0)
        lower_chunk = jnp.where((r >= c) & same_chunk, 1.0, 0.0)
        csum = jnp.dot(lower, log_f, precision=lax.Precision.HIGHEST, preferred_element_type=F32) + carry_ref[...]
        gc = jnp.dot(lower_chunk, g, precision=lax.Precision.HIGHEST, preferred_element_type=F32)
        carry_ref[...] += jnp.sum(log_f, axis=0, keepdims=True)
        o_ref[...] = jnp.where(lane < HEADS, csum, jnp.where(lane < 2 * HEADS, jax.nn.sigmoid(z), gc))

    row = pl.BlockSpec((tt, LANES), lambda i: (i, 0))
    vec = pl.BlockSpec((1, LANES), lambda i: (0, 0))
    return pl.pallas_call(
        body, grid=(t // tt,), name="gate_fwd", in_specs=[row, vec, vec], out_specs=row,
        out_shape=jax.ShapeDtypeStruct((t, LANES), F32),
        scratch_shapes=[pltpu.VMEM((1, LANES), F32)],
        compiler_params=_params("arbitrary"),
    )(small, pbias, pscale)


def _gate_bwd(small, pbias, pscale, dscal):
    t = small.shape[0]
    tt = _pick(t, (256, 128))
    nt = t // tt

    def body(x_ref, pb_ref, ps_ref, d_ref, dx_ref, dpb_ref, dps_ref, carry_ref):
        @pl.when(pl.program_id(0) == 0)
        def _():
            carry_ref[...] = jnp.zeros_like(carry_ref)
            dpb_ref[...] = jnp.zeros_like(dpb_ref)
            dps_ref[...] = jnp.zeros_like(dps_ref)

        lane = lax.broadcasted_iota(jnp.int32, (tt, LANES), 1)
        z = x_ref[...] + pb_ref[...]
        d = d_ref[...]
        dc = jnp.where(lane < HEADS, d, 0.0)
        dbeta = jnp.where((lane >= HEADS) & (lane < 2 * HEADS), d, 0.0)
        dgc = jnp.where((lane >= 2 * HEADS) & (lane < 3 * HEADS), d, 0.0)
        r, c, same_chunk = _tri_masks(tt)
        upper = jnp.where(r <= c, 1.0, 0.0)
        upper_chunk = jnp.where((r <= c) & same_chunk, 1.0, 0.0)
        dlogf = jnp.dot(upper, dc, precision=lax.Precision.HIGHEST, preferred_element_type=F32) + carry_ref[...]
        dg = jnp.dot(upper_chunk, dgc, precision=lax.Precision.HIGHEST, preferred_element_type=F32)
        carry_ref[...] += jnp.sum(dc, axis=0, keepdims=True)
        sg = jax.nn.sigmoid(z)
        dz = dlogf * (1.0 - sg) + dbeta * sg * (1.0 - sg) + dg * ps_ref[...] * sg
        dx_ref[...] = dz.astype(dx_ref.dtype)
        dpb_ref[...] += jnp.sum(dz, axis=0, keepdims=True)
        dps_ref[...] += jnp.sum(dg * _softplus(z), axis=0, keepdims=True)

    row = pl.BlockSpec((tt, LANES), lambda i: (nt - 1 - i, 0))
    vec = pl.BlockSpec((1, LANES), lambda i: (0, 0))
    return pl.pallas_call(
        body, grid=(nt,), name="gate_bwd", in_specs=[row, vec, vec, row], out_specs=[row, vec, vec],
        out_shape=[jax.ShapeDtypeStruct((t, LANES), BF16), jax.ShapeDtypeStruct((1, LANES), F32),
                   jax.ShapeDtypeStruct((1, LANES), F32)],
        scratch_shapes=[pltpu.VMEM((1, LANES), F32)],
        compiler_params=_params("arbitrary"),
    )(small, pbias, pscale, dscal)


def _gdn_pre_fwd(xg, conv_w, seg):
    t = xg.shape[0]
    c3 = 3 * WIDTH
    tt = _pick(t, (320, 256, 128))

    def body(x_ref, p_ref, w_ref, e_ref, q_ref, k_ref, v_ref):
        x = x_ref[...]
        prev = jnp.where(pl.program_id(0) == 0, 0.0, p_ref[...])
        s = _silu(_causal_conv(x, prev, w_ref, GDN_CONV))
        e = e_ref[...]
        q = s[:, :WIDTH]
        k = s[:, WIDTH:2 * WIDTH]
        _store_heads(q_ref, q * lax.rsqrt(_segsum(q * q, e) + RMS_EPS) * (HEAD_DIM ** -0.5))
        _store_heads(k_ref, k * lax.rsqrt(_segsum(k * k, e) + RMS_EPS))
        _store_heads(v_ref, s[:, 2 * WIDTH:])

    out = pl.BlockSpec((HEADS, tt, HEAD_DIM), lambda i: (0, i, 0))
    shp = jax.ShapeDtypeStruct((HEADS, t, HEAD_DIM), F32)
    return pl.pallas_call(
        body, grid=(t // tt,), name="gdn_pre_fwd",
        in_specs=[pl.BlockSpec((tt, c3), lambda i: (i, 0)), _prev_spec(tt, c3),
                  pl.BlockSpec((GDN_CONV, c3), lambda i: (0, 0)), pl.BlockSpec((WIDTH, WIDTH), lambda i: (0, 0))],
        out_specs=[out, out, out], out_shape=[shp, shp, shp],
        compiler_params=_params("arbitrary"),
    )(xg, xg, conv_w, seg)


def _gdn_pre_bwd(xg, conv_w, seg, dqn, dkn, dv):
    t = xg.shape[0]
    c3 = 3 * WIDTH
    tt = _pick(t, (320, 256, 128))
    nt = t // tt

    def body(x_ref, p_ref, w_ref, e_ref, dq_ref, dk_ref, dv_ref, dx_ref, dw_ref, carry_ref):
        step = pl.program_id(0)
        x = x_ref[...]
        e = e_ref[...]
        prev = jnp.where(step == nt - 1, 0.0, p_ref[...])
        y = _causal_conv(x, prev, w_ref, GDN_CONV)
        s = _silu(y)
        q = s[:, :WIDTH]
        k = s[:, WIDTH:2 * WIDTH]
        rq = lax.rsqrt(_segsum(q * q, e) + RMS_EPS)
        rk = lax.rsqrt(_segsum(k * k, e) + RMS_EPS)
        gq = _load_heads(dq_ref) * (HEAD_DIM ** -0.5)
        gk = _load_heads(dk_ref)
        dq = rq * gq - q * (rq * rq * rq) * _segsum(gq * q, e)
        dk = rk * gk - k * (rk * rk * rk) * _segsum(gk * k, e)
        dy = jnp.concatenate([dq, dk, _load_heads(dv_ref)], axis=1) * _silu_grad(y)

        @pl.when(step == 0)
        def _():
            carry_ref[...] = jnp.zeros_like(carry_ref)
            dw_ref[...] = jnp.zeros_like(dw_ref)

        dx = _causal_conv_bwd(x, prev, dy, carry_ref[...], w_ref, dw_ref, GDN_CONV)
        dx_ref[...] = dx.astype(dx_ref.dtype)
        carry_ref[...] = dy[:8]

    rev = lambda i: (nt - 1 - i, 0)
    blk = pl.BlockSpec((HEADS, tt, HEAD_DIM), lambda i: (0, nt - 1 - i, 0))
    return pl.pallas_call(
        body, grid=(nt,), name="gdn_pre_bwd",
        in_specs=[pl.BlockSpec((tt, c3), rev),
                  pl.BlockSpec((8, c3), lambda i: (jnp.maximum((nt - 1 - i) * (tt // 8) - 1, 0), 0)),
                  pl.BlockSpec((GDN_CONV, c3), lambda i: (0, 0)), pl.BlockSpec((WIDTH, WIDTH), lambda i: (0, 0)),
                  blk, blk, blk],
        out_specs=[pl.BlockSpec((tt, c3), rev), pl.BlockSpec((GDN_CONV, c3), lambda i: (0, 0))],
        out_shape=[jax.ShapeDtypeStruct((t, c3), BF16), jax.ShapeDtypeStruct((GDN_CONV, c3), F32)],
        scratch_shapes=[pltpu.VMEM((8, c3), F32)],
        compiler_params=_params("arbitrary"),
    )(xg, xg, conv_w, seg, dqn, dkn, dv)


def _bmm(a, b, ca, cb, precision=None):
    return lax.dot_general(a, b, (((ca,), (cb,)), ((0,), (0,))), precision=precision, preferred_element_type=F32)


def _bf(x):
    return x.astype(BF16)


def _tri_inverse(a, eye):
    x = -a
    tinv = eye + x
    pw = x
    for _ in range(5):
        pb = _bf(pw)
        pw = _bmm(pb, pb, 2, 1)
        tinv = tinv + _bmm(_bf(tinv), _bf(pw), 2, 1)
    resid = eye - _bmm(eye + a, tinv, 2, 1, precision=HI)
    return tinv + _bmm(_bf(tinv), _bf(resid), 2, 1)


def _gdn_intra(q, k, v, bc, gcc, gcr):
    ii = lax.broadcasted_iota(jnp.int32, (CHUNK, CHUNK), 0)
    jj = lax.broadcasted_iota(jnp.int32, (CHUNK, CHUNK), 1)
    tril = (ii >= jj)[None]
    strict = (ii > jj)[None]
    eye = jnp.where(ii == jj, 1.0, 0.0).astype(F32)[None]
    last = (ii == CHUNK - 1)[None]
    dm = jnp.exp(jnp.where(tril, gcc - gcr, NEG))
    gam = jnp.exp(gcc)
    kb = k * bc
    vb = v * bc
    kk = _bmm(_bf(kb), _bf(k), 2, 2)
    a = jnp.where(strict, kk * dm, 0.0)
    tinv = _tri_inverse(a, eye)
    kbg = kb * gam
    u = _bmm(tinv, vb, 2, 1, precision=HI)
    wk = _bmm(tinv, kbg, 2, 1, precision=HI)
    qk = _bmm(_bf(q), _bf(k), 2, 2)
    p = jnp.where(tril, qk * dm, 0.0)
    gl = jnp.sum(jnp.where(last, gcc, 0.0), axis=1, keepdims=True)
    edec = jnp.exp(gl - gcc)
    return dict(tril=tril, strict=strict, dm=dm, gam=gam, kb=kb, kk=kk, a=a, tinv=tinv, u=u, wk=wk, qk=qk, p=p,
                qg=q * gam, kt=k * edec, edec=edec, gaml=jnp.exp(gl), last=last)


def _gate_tiles(sc, gct, nb):
    rows = nb * CHUNK
    cols = lambda lane0: jnp.stack([jnp.broadcast_to(sc[:, lane0 + h:lane0 + h + 1], (rows, HEAD_DIM))
                                    for h in range(HEADS)], axis=0).reshape(HEADS * nb, CHUNK, HEAD_DIM)
    gcr = jnp.stack([jnp.broadcast_to(gct[h:h + 1, n * CHUNK:(n + 1) * CHUNK], (CHUNK, CHUNK))
                     for h in range(HEADS) for n in range(nb)], axis=0)
    return cols(HEADS), cols(2 * HEADS), gcr


def _gdn_fwd(q, k, v, scal, gct, nb=None):
    h, t, dh = q.shape
    nc = t // CHUNK
    nb = nb or _pick(nc, (4, 2))
    bsz = h * nb

    def body(q_ref, k_ref, v_ref, sc_ref, gt_ref, o_ref, s0_ref, state_ref):
        @pl.when(pl.program_id(0) == 0)
        def _():
            state_ref[...] = jnp.zeros_like(state_ref)

        ld = lambda r: r[...].reshape(bsz, CHUNK, dh)
        bc, gcc, gcr = _gate_tiles(sc_ref[...], gt_ref[...], nb)
        z = _gdn_intra(ld(q_ref), ld(k_ref), ld(v_ref), bc, gcc, gcr)
        per = lambda x: x.reshape((h, nb) + x.shape[1:])
        u, wk, p, qg, kt, gaml = (per(z[n]) for n in ("u", "wk", "p", "qg", "kt", "gaml"))
        s = state_ref[...]
        for n in range(nb):
            s0_ref[:, n] = s
            sb = _bf(s)
            vn = u[:, n] - _bmm(_bf(wk[:, n]), sb, 2, 1)
            o_ref[:, n * CHUNK:(n + 1) * CHUNK, :] = _bmm(_bf(qg[:, n]), sb, 2, 1) + _bmm(_bf(p[:, n]), _bf(vn), 2, 1)
            s = s * gaml[:, n] + _bmm(_bf(kt[:, n]), _bf(vn), 1, 1)
        state_ref[...] = s

    blk = pl.BlockSpec((h, nb * CHUNK, dh), lambda i: (0, i, 0))
    return pl.pallas_call(
        body, grid=(nc // nb,), name="gdn_fwd",
        in_specs=[blk] * 3 + [pl.BlockSpec((nb * CHUNK, LANES), lambda i: (i, 0)),
                              pl.BlockSpec((h, nb * CHUNK), lambda i: (0, i))],
        out_specs=[blk, pl.BlockSpec((h, nb, dh, dh), lambda i: (0, i, 0, 0))],
        out_shape=[jax.ShapeDtypeStruct((h, t, dh), F32), jax.ShapeDtypeStruct((h, nc, dh, dh), F32)],
        scratch_shapes=[pltpu.VMEM((h, dh, dh), F32)],
        compiler_params=_params("arbitrary"),
    )(q, k, v, scal, gct)


def _gdn_bwd(q, k, v, scal, gct, s0s, do, nb=None):
    h, t, dh = q.shape
    nc = t // CHUNK
    nb = nb or _pick(nc, (2,))
    bsz = h * nb
    ng = nc // nb
    rows = nb * CHUNK

    def body(q_ref, k_ref, v_ref, sc_ref, gt_ref, s0_ref, do_ref,
             dq_ref, dk_ref, dv_ref, dsc_ref, dgt_ref, ds_ref):
        @pl.when(pl.program_id(0) == 0)
        def _():
            ds_ref[...] = jnp.zeros_like(ds_ref)

        ld = lambda r: r[...].reshape(bsz, CHUNK, dh)
        q, k, v = ld(q_ref), ld(k_ref), ld(v_ref)
        bc, gcc, gcr = _gate_tiles(sc_ref[...], gt_ref[...], nb)
        z = _gdn_intra(q, k, v, bc, gcc, gcr)
        per = lambda x: x.reshape((h, nb) + x.shape[1:])
        u, wk, p, qg, kt, gaml = (per(z[n]) for n in ("u", "wk", "p", "qg", "kt", "gaml"))
        dout = per(ld(do_ref))
        ds = ds_ref[...]
        d_u, d_wk, d_p, d_qg, d_kt, d_gaml = ([None] * nb for _ in range(6))
        for n in reversed(range(nb)):
            s0 = s0_ref[:, n]
            s0b, dsb, dob = _bf(s0), _bf(ds), _bf(dout[:, n])
            wkb, qgb = _bf(wk[:, n]), _bf(qg[:, n])
            vn = u[:, n] - _bmm(wkb, s0b, 2, 1)
            dvn = _bmm(_bf(p[:, n]), dob, 1, 1) + _bmm(_bf(kt[:, n]), dsb, 2, 1)
            dvnb = _bf(dvn)
            d_u[n] = dvn
            d_p[n] = _bmm(dob, _bf(vn), 2, 2)
            d_qg[n] = _bmm(dob, s0b, 2, 2)
            d_kt[n] = _bmm(_bf(vn), dsb, 2, 2)
            d_gaml[n] = jnp.sum(s0 * ds, axis=1, keepdims=True)
            d_wk[n] = -_bmm(dvnb, s0b, 2, 2)
            ds = _bmm(qgb, dob, 1, 1) + gaml[:, n] * ds - _bmm(wkb, dvnb, 1, 1)
        ds_ref[...] = ds

        flat = lambda xs: jnp.stack(xs, axis=1).reshape((bsz,) + xs[0].shape[1:])
        d_u, d_wk, d_p, d_qg, d_kt, d_gaml = (flat(x) for x in (d_u, d_wk, d_p, d_qg, d_kt, d_gaml))
        tinv, gam, kb, dm = z["tinv"], z["gam"], z["kb"], z["dm"]
        drv = _bmm(tinv, d_u, 1, 1, precision=HI)
        drk = _bmm(tinv, d_wk, 1, 1, precision=HI)
        da = -(_bmm(_bf(drv), _bf(z["u"]), 2, 2) + _bmm(_bf(drk), _bf(z["wk"]), 2, 2))
        da = jnp.where(z["strict"], da, 0.0)
        d_p = jnp.where(z["tril"], d_p, 0.0)
        dkk = _bf(da * dm)
        dqk = _bf(d_p * dm)
        dkb = _bmm(dkk, _bf(k), 2, 1) + drk * gam
        dk = _bmm(dkk, _bf(kb), 1, 1) + _bmm(dqk, _bf(q), 1, 1) + dkb * bc + d_kt * z["edec"]
        dq = _bmm(dqk, _bf(k), 2, 1) + d_qg * gam
        mm = da * z["a"] + d_p * z["p"]
        dkt_kt = d_kt * z["kt"]
        dgl = jnp.sum(dkt_kt, axis=1, keepdims=True) + d_gaml * z["gaml"]
        dgc = mm + d_qg * z["qg"] + drk * kb * gam - dkt_kt + jnp.where(z["last"], dgl, 0.0)
        dq_ref[...] = dq.reshape(h, rows, dh)
        dk_ref[...] = dk.reshape(h, rows, dh)
        dv_ref[...] = (drv * bc).reshape(h, rows, dh)
        dbeta = (dkb * k + drv * v).reshape(h, rows, dh)
        dgc = dgc.reshape(h, rows, dh)
        lane = lax.broadcasted_iota(jnp.int32, (rows, LANES), 1)
        dsc = jnp.zeros((rows, LANES), F32)
        for hh in range(h):
            dsc = jnp.where(lane == HEADS + hh, jnp.sum(dbeta[hh], axis=1, keepdims=True), dsc)
            dsc = jnp.where(lane == 2 * HEADS + hh, jnp.sum(dgc[hh], axis=1, keepdims=True), dsc)
        dsc_ref[...] = dsc
        dgr = -jnp.sum(mm, axis=1, keepdims=True)
        for hh in range(h):
            for n in range(nb):
                dgt_ref[hh:hh + 1, n * CHUNK:(n + 1) * CHUNK] = dgr[hh * nb + n]

    blk = pl.BlockSpec((h, rows, dh), lambda i: (0, ng - 1 - i, 0))
    shp = jax.ShapeDtypeStruct((h, t, dh), F32)
    sc_spec = pl.BlockSpec((rows, LANES), lambda i: (ng - 1 - i, 0))
    gt_spec = pl.BlockSpec((h, rows), lambda i: (0, ng - 1 - i))
    return pl.pallas_call(
        body, grid=(ng,), name="gdn_bwd",
        in_specs=[blk] * 3 + [sc_spec, gt_spec, pl.BlockSpec((h, nb, dh, dh), lambda i: (0, ng - 1 - i, 0, 0)), blk],
        out_specs=[blk] * 3 + [sc_spec, gt_spec],
        out_shape=[shp] * 3 + [jax.ShapeDtypeStruct((t, LANES), F32), jax.ShapeDtypeStruct((h, t), F32)],
        scratch_shapes=[pltpu.VMEM((h, dh, dh), F32)],
        compiler_params=_params("arbitrary"),
    )(q, k, v, scal, gct, s0s, do)


def _gdn_post_fwd(o, xg, gain, seg):
    t = o.shape[1]
    tt = _pick(t, (320, 256, 128))

    def body(o_ref, z_ref, g_ref, e_ref, y_ref):
        x = _load_heads(o_ref)
        r = lax.rsqrt(_segsum(x * x, e_ref[...]) * (1.0 / HEAD_DIM) + RMS_EPS)
        y_ref[...] = (x * r * g_ref[...] * _silu(z_ref[...])).astype(y_ref.dtype)

    return pl.pallas_call(
        body, grid=(t // tt,), name="gdn_post_fwd",
        in_specs=[pl.BlockSpec((HEADS, tt, HEAD_DIM), lambda i: (0, i, 0)), pl.BlockSpec((tt, WIDTH), lambda i: (i, 3)),
                  pl.BlockSpec((1, WIDTH), lambda i: (0, 0)), pl.BlockSpec((WIDTH, WIDTH), lambda i: (0, 0))],
        out_specs=pl.BlockSpec((tt, WIDTH), lambda i: (i, 0)),
        out_shape=jax.ShapeDtypeStruct((t, WIDTH), BF16),
        compiler_params=_params("arbitrary"),
    )(o, xg, gain, seg)


def _gdn_post_bwd(o, xg, gain, seg, dy):
    t = o.shape[1]
    tt = _pick(t, (320, 256, 128))

    def body(o_ref, z_ref, g_ref, e_ref, dy_ref, do_ref, dz_ref, dg_ref):
        x = _load_heads(o_ref)
        zz = z_ref[...]
        e = e_ref[...]
        gain_v = g_ref[...]
        d = dy_ref[...]
        r = lax.rsqrt(_segsum(x * x, e) * (1.0 / HEAD_DIM) + RMS_EPS)
        xr = x * r
        don = d * _silu(zz)
        dz_ref[...] = (d * xr * gain_v * _silu_grad(zz)).astype(dz_ref.dtype)
        gy = don * gain_v
        _store_heads(do_ref, r * gy - xr * (r * r) * (_segsum(gy * x, e) * (1.0 / HEAD_DIM)))

        @pl.when(pl.program_id(0) == 0)
        def _():
            dg_ref[...] = jnp.zeros_like(dg_ref)

        dg_ref[...] += jnp.sum(don * xr, axis=0, keepdims=True)

    row = pl.BlockSpec((tt, WIDTH), lambda i: (i, 0))
    vec = pl.BlockSpec((1, WIDTH), lambda i: (0, 0))
    hm = pl.BlockSpec((HEADS, tt, HEAD_DIM), lambda i: (0, i, 0))
    return pl.pallas_call(
        body, grid=(t // tt,), name="gdn_post_bwd",
        in_specs=[hm, pl.BlockSpec((tt, WIDTH), lambda i: (i, 3)), vec,
                  pl.BlockSpec((WIDTH, WIDTH), lambda i: (0, 0)), row],
        out_specs=[hm, row, vec],
        out_shape=[jax.ShapeDtypeStruct((HEADS, t, HEAD_DIM), F32), jax.ShapeDtypeStruct((t, WIDTH), BF16),
                   jax.ShapeDtypeStruct((1, WIDTH), F32)],
        compiler_params=_params("arbitrary"),
    )(o, xg, gain, seg, dy)


def _mix_fwd(yf, yg, gates, bias):
    t, d = yf.shape
    tt = _pick(t, (320, 256, 128))

    def body(yf_ref, yg_ref, g1_ref, g2_ref, b1_ref, b2_ref, o_ref):
        g1 = jax.nn.sigmoid(g1_ref[...] + b1_ref[...])
        g2 = jax.nn.sigmoid(g2_ref[...] + b2_ref[...])
        o_ref[...] = (g1 * yf_ref[...] + g2 * yg_ref[...]).astype(o_ref.dtype)

    row = pl.BlockSpec((tt, d), lambda i: (i, 0))
    return pl.pallas_call(
        body, grid=(t // tt,), name="mix_fwd",
        in_specs=[row, row, row, pl.BlockSpec((tt, d), lambda i: (i, 1)),
                  pl.BlockSpec((1, d), lambda i: (0, 0)), pl.BlockSpec((1, d), lambda i: (0, 1))],
        out_specs=row, out_shape=jax.ShapeDtypeStruct((t, d), BF16),
        compiler_params=_params("arbitrary"),
    )(yf, yg, gates, gates, bias, bias)


def _mix_bwd(dmix, yf, yg, gates, bias):
    t, d = yf.shape
    tt = _pick(t, (320, 256, 128))

    def body(dm_ref, yf_ref, yg_ref, g1_ref, g2_ref, b1_ref, b2_ref, dyf_ref, dyg_ref, dg_ref, db_ref):
        dm = dm_ref[...]
        g1 = jax.nn.sigmoid(g1_ref[...] + b1_ref[...])
        g2 = jax.nn.sigmoid(g2_ref[...] + b2_ref[...])
        dyf_ref[...] = (dm * g1).astype(BF16)
        dyg_ref[...] = (dm * g2).astype(BF16)
        dgate = jnp.concatenate([dm * yf_ref[...] * g1 * (1.0 - g1), dm * yg_ref[...] * g2 * (1.0 - g2)], axis=1)
        dg_ref[...] = dgate.astype(BF16)

        @pl.when(pl.program_id(0) == 0)
        def _():
            db_ref[...] = jnp.zeros_like(db_ref)

        db_ref[...] += jnp.sum(dgate, axis=0, keepdims=True)

    row = pl.BlockSpec((tt, d), lambda i: (i, 0))
    wide = pl.BlockSpec((tt, 2 * d), lambda i: (i, 0))
    return pl.pallas_call(
        body, grid=(t // tt,), name="mix_bwd",
        in_specs=[row, row, row, row, pl.BlockSpec((tt, d), lambda i: (i, 1)),
                  pl.BlockSpec((1, d), lambda i: (0, 0)), pl.BlockSpec((1, d), lambda i: (0, 1))],
        out_specs=[row, row, wide, pl.BlockSpec((1, 2 * d), lambda i: (0, 0))],
        out_shape=[jax.ShapeDtypeStruct((t, d), BF16), jax.ShapeDtypeStruct((t, d), BF16),
                   jax.ShapeDtypeStruct((t, 2 * d), BF16), jax.ShapeDtypeStruct((1, 2 * d), F32)],
        compiler_params=_params("arbitrary"),
    )(dmix, yf, yg, gates, gates, bias, bias)


def _ffn_act_fwd(up, conv_w, conv_b):
    t, c = up.shape
    tt = 128

    def body(x_ref, p_ref, w_ref, b_ref, o_ref):
        prev = jnp.where(pl.program_id(0) == 0, 0.0, p_ref[...])
        u = _causal_conv(x_ref[...], prev, w_ref, FFN_CONV) + b_ref[...]
        o_ref[...] = (_silu(u[:, :D_FF]) * u[:, D_FF:]).astype(o_ref.dtype)

    return pl.pallas_call(
        body, grid=(t // tt,), name="ffn_act_fwd",
        in_specs=[pl.BlockSpec((tt, c), lambda i: (i, 0)), _prev_spec(tt, c),
                  pl.BlockSpec((FFN_CONV, c), lambda i: (0, 0)), pl.BlockSpec((1, c), lambda i: (0, 0))],
        out_specs=pl.BlockSpec((tt, D_FF), lambda i: (i, 0)),
        out_shape=jax.ShapeDtypeStruct((t, D_FF), BF16),
        compiler_params=_params("arbitrary"),
    )(up, up, conv_w, conv_b)


def _ffn_act_bwd(up, conv_w, conv_b, dact):
    t, c = up.shape
    tt = 128
    nt = t // tt

    def body(x_ref, p_ref, w_ref, b_ref, da_ref, dx_ref, dw_ref, db_ref, carry_ref):
        step = pl.program_id(0)
        x = x_ref[...]
        prev = jnp.where(step == nt - 1, 0.0, p_ref[...])
        u = _causal_conv(x, prev, w_ref, FFN_CONV) + b_ref[...]
        gate, val = u[:, :D_FF], u[:, D_FF:]
        da = da_ref[...]
        du = jnp.concatenate([da * val * _silu_grad(gate), da * _silu(gate)], axis=1)

        @pl.when(step == 0)
        def _():
            carry_ref[...] = jnp.zeros_like(carry_ref)
            dw_ref[...] = jnp.zeros_like(dw_ref)
            db_ref[...] = jnp.zeros_like(db_ref)

        dx = _causal_conv_bwd(x, prev, du, carry_ref[...], w_ref, dw_ref, FFN_CONV)
        dx_ref[...] = dx.astype(dx_ref.dtype)
        db_ref[...] += jnp.sum(du, axis=0, keepdims=True)
        carry_ref[...] = du[:8]

    rev = lambda i: (nt - 1 - i, 0)
    return pl.pallas_call(
        body, grid=(nt,), name="ffn_act_bwd",
        in_specs=[pl.BlockSpec((tt, c), rev),
                  pl.BlockSpec((8, c), lambda i: (jnp.maximum((nt - 1 - i) * (tt // 8) - 1, 0), 0)),
                  pl.BlockSpec((FFN_CONV, c), lambda i: (0, 0)), pl.BlockSpec((1, c), lambda i: (0, 0)),
                  pl.BlockSpec((tt, D_FF), rev)],
        out_specs=[pl.BlockSpec((tt, c), rev), pl.BlockSpec((FFN_CONV, c), lambda i: (0, 0)),
                   pl.BlockSpec((1, c), lambda i: (0, 0))],
        out_shape=[jax.ShapeDtypeStruct((t, c), BF16), jax.ShapeDtypeStruct((FFN_CONV, c), F32),
                   jax.ShapeDtypeStruct((1, c), F32)],
        scratch_shapes=[pltpu.VMEM((8, c), F32)],
        compiler_params=_params("arbitrary"),
    )(up, up, conv_w, conv_b, dact)


def _final_loss(h2, target, gain, seq):
    t, d = h2.shape
    tr = _pick(t, (320, 256, 128))

    def body(h_ref, t_ref, g_ref, loss_ref, dh_ref, dhb_ref, dg_ref):
        i = pl.program_id(0)
        x = h_ref[...]
        gain_v = g_ref[...]
        r = lax.rsqrt(jnp.mean(x * x, axis=-1, keepdims=True) + RMS_EPS)
        xr = x * r
        rows = i * tr + lax.broadcasted_iota(jnp.int32, (tr, 1), 0)
        real = (rows >= N_META) & (rows < N_META + seq)
        err = jnp.where(real, xr * gain_v - t_ref[...], 0.0)
        dy = err * (1.0 / d)
        gy = dy * gain_v
        dh = r * (gy - xr * jnp.mean(gy * xr, axis=-1, keepdims=True))
        dh_ref[...] = dh
        dhb_ref[...] = dh.astype(BF16)

        @pl.when(i == 0)
        def _():
            loss_ref[...] = jnp.zeros_like(loss_ref)
            dg_ref[...] = jnp.zeros_like(dg_ref)

        part = jnp.sum(jnp.sum(err * err, axis=-1, keepdims=True), axis=0, keepdims=True)
        loss_ref[...] += jnp.broadcast_to(part * (0.5 / d), loss_ref.shape)
        dg_ref[...] += jnp.sum(dy * xr, axis=0, keepdims=True)

    row = pl.BlockSpec((tr, d), lambda i: (i, 0))
    vec = pl.BlockSpec((1, d), lambda i: (0, 0))
    return pl.pallas_call(
        body, grid=(t // tr,), name="final_loss",
        in_specs=[row, row, vec],
        out_specs=[pl.BlockSpec((1, LANES), lambda i: (0, 0)), row, row, vec],
        out_shape=[jax.ShapeDtypeStruct((1, LANES), F32), jax.ShapeDtypeStruct((t, d), F32),
                   jax.ShapeDtypeStruct((t, d), BF16), jax.ShapeDtypeStruct((1, d), F32)],
        compiler_params=_params("arbitrary"),
    )(h2, target, gain)


def _adamw(w, m, v, grecv, name):
    r = w.shape[0]
    tr = _pick(r, (1024, 512, 256, 128, 64, 32, 16, 8))

    def body(w_ref, m_ref, v_ref, g_ref, go_ref, d_ref, mo_ref, vo_ref):
        g = g_ref[0].astype(F32)
        for s in range(1, N_DEV):
            g = g + g_ref[s].astype(F32)
        wv = w_ref[...]
        mn = ADAM_B1 * m_ref[...] + (1.0 - ADAM_B1) * g
        vn = ADAM_B2 * v_ref[...] + (1.0 - ADAM_B2) * (g * g)
        m_hat = mn / (1.0 - ADAM_B1 ** ADAM_STEP)
        v_hat = vn / (1.0 - ADAM_B2 ** ADAM_STEP)
        go_ref[...] = g
        d_ref[...] = -ADAM_LR * (m_hat / (jnp.sqrt(v_hat) + ADAM_EPS) + ADAM_WD * wv)
        mo_ref[...] = mn
        vo_ref[...] = vn

    row = pl.BlockSpec((tr, LANES), lambda i: (i, 0))
    shp = jax.ShapeDtypeStruct((r, LANES), F32)
    return pl.pallas_call(
        body, grid=(r // tr,), name=name,
        in_specs=[row, row, row, pl.BlockSpec((N_DEV, tr, LANES), lambda i: (0, i, 0))],
        out_specs=[row] * 4, out_shape=[shp] * 4,
        compiler_params=_params("parallel"),
    )(w, m, v, grecv)


def _mesh_pos():
    return lax.axis_index("x"), lax.axis_index("y"), lax.axis_index("c")


def _all_gather(shard):
    r = shard.shape[0]

    def body(x_ref, out_ref, send_sems, recv_sems, local_sem):
        x, y, c = _mesh_pos()
        me, sibling = (x, y, c), (x, y, 1 - c)
        chips = [(1 - x, y), (x, 1 - y), (1 - x, 1 - y)]

        def slot(px, py, pc):
            return out_ref.at[4 * px + 2 * py + pc]

        def copy(kk, block, to, src=None):
            return pltpu.make_async_remote_copy(
                src_ref=slot(*block) if src is None else src, dst_ref=slot(*block),
                send_sem=send_sems.at[kk], recv_sem=recv_sems.at[kk], device_id=to, device_id_type=MESH_ID)

        mine = pltpu.make_async_copy(x_ref, slot(*me), local_sem)
        mine.start()
        first = [copy(0, me, sibling, src=x_ref)]
        first += [copy(1 + j, me, (*chip, c), src=x_ref) for j, chip in enumerate(chips)]
        for cp in first:
            cp.start()
        passed = [copy(4 + j, (*chip, c), sibling) for j, chip in enumerate(chips)]
        for j, chip in enumerate(chips):
            copy(1 + j, (*chip, c), me).wait_recv()
            passed[j].start()
        copy(0, sibling, me).wait_recv()
        for j, chip in enumerate(chips):
            copy(4 + j, (*chip, 1 - c), me).wait_recv()
        for cp in first + passed:
            cp.wait_send()
        mine.wait()

    return pl.pallas_call(
        body, name="weight_all_gather",
        in_specs=[pl.BlockSpec(memory_space=pl.ANY)], out_specs=pl.BlockSpec(memory_space=pl.ANY),
        out_shape=jax.ShapeDtypeStruct((N_DEV, r, LANES), shard.dtype),
        scratch_shapes=[pltpu.SemaphoreType.DMA((7,)), pltpu.SemaphoreType.DMA((7,)), pltpu.SemaphoreType.DMA],
    )(shard)


def _grad_exchange(big, small):
    def body(big_ref, small_ref, rbig_ref, rsmall_ref, send_sems, recv_sems, local_sems):
        x, y, c = _mesh_pos()
        me = 4 * x + 2 * y + c
        copies = []
        for kk in range(1, N_DEV):
            px = 1 - x if kk & 4 else x
            py = 1 - y if kk & 2 else y
            pc = 1 - c if kk & 1 else c
            peer = 4 * px + 2 * py + pc
            copies.append(pltpu.make_async_remote_copy(
                src_ref=big_ref.at[peer], dst_ref=rbig_ref.at[me],
                send_sem=send_sems.at[kk - 1], recv_sem=recv_sems.at[kk - 1],
                device_id=(px, py, pc), device_id_type=MESH_ID))
            copies.append(pltpu.make_async_remote_copy(
                src_ref=small_ref, dst_ref=rsmall_ref.at[me],
                send_sem=send_sems.at[6 + kk], recv_sem=recv_sems.at[6 + kk],
                device_id=(px, py, pc), device_id_type=MESH_ID))
        own = [pltpu.make_async_copy(big_ref.at[me], rbig_ref.at[me], local_sems.at[0]),
               pltpu.make_async_copy(small_ref, rsmall_ref.at[me], local_sems.at[1])]
        for cp in own + copies:
            cp.start()
        for cp in copies + own:
            cp.wait()

    hbm = pl.BlockSpec(memory_space=pl.ANY)
    return pl.pallas_call(
        body, name="grad_exchange", in_specs=[hbm, hbm], out_specs=[hbm, hbm],
        out_shape=[jax.ShapeDtypeStruct(big.shape, big.dtype),
                   jax.ShapeDtypeStruct((N_DEV,) + small.shape, small.dtype)],
        scratch_shapes=[pltpu.SemaphoreType.DMA((14,)), pltpu.SemaphoreType.DMA((14,)),
                        pltpu.SemaphoreType.DMA((2,))],
    )(big, small)


def _pad_flat(parts, rows):
    flat = jnp.concatenate([p.reshape(-1) for p in parts])
    return jnp.pad(flat, (0, rows * LANES - flat.shape[0])).reshape(rows, LANES)


def _rows_for(n_elems, mult=1024):
    rows = -(-n_elems // LANES)
    return -(-rows // mult) * mult


SHARDED = ("meta_tokens", "w_in", "gdn_conv_w", "w_branch_fox", "w_branch_gdn", "w_out", "ffn_w_up", "ffn_conv_w",
           "ffn_w_down")
REPLICATED = ("fgt_bias", "gdn_a_log", "gdn_dt_bias", "gdn_norm_w", "gate_bias", "norm_mix_w", "norm_ffn_w",
              "ffn_conv_b", "norm_final_w")
WEIGHTS = ("meta_tokens", "w_in", "fgt_bias", "gdn_conv_w", "gdn_a_log", "gdn_dt_bias", "gdn_norm_w", "gate_bias",
           "w_branch_fox", "w_branch_gdn", "w_out", "norm_mix_w", "norm_ffn_w", "ffn_w_up", "ffn_conv_w",
           "ffn_conv_b", "ffn_w_down", "norm_final_w")


def _unpack(buf, shapes):
    flat = buf.reshape(-1)
    out, off = [], 0
    for s in shapes:
        n = int(np.prod(s))
        out.append(flat[off:off + n].reshape(s))
        off += n
    return out


def _unpack_gathered(buf, shapes):
    flat = buf.reshape(N_DEV, -1)
    out, off = [], 0
    for s in shapes:
        n = int(np.prod(s))
        out.append(flat[:, off:off + n].reshape((N_DEV,) + tuple(s)))
        off += n
    return out


def _cat_cols(g):
    return g.transpose(1, 0, 2).reshape(g.shape[1], -1)


def _col_blocks(full, width):
    return full.reshape(full.shape[0], N_DEV, width).transpose(1, 0, 2)


def _local_step(x, target, w):
    seq = x.shape[0]
    t = _padded_tokens(seq)
    pad = t - N_META - seq
    seg = _seg_matrix()
    zrows = jnp.zeros((pad, D_MODEL), F32)
    h0 = jnp.concatenate([w["meta_tokens"], x, zrows], axis=0)
    tgt = jnp.concatenate([jnp.zeros((N_META, D_MODEL), F32), target, zrows], axis=0)

    w_in = w["w_in"]
    o_f, o_g, o_z, o_b, o_a, o_gate = 1536, 1544, 3080, 3592, 3600, 3608
    w_small = jnp.concatenate([w_in[:, o_f:o_f + 8], w_in[:, o_b:o_b + 8], w_in[:, o_a:o_a + 8],
                               jnp.zeros((D_MODEL, LANES - 24), BF16)], axis=1)
    w_r = jnp.concatenate([w_in[:, :1536], w_in[:, o_g:o_z], w_in[:, o_z:o_b], w_in[:, o_gate:], w_small], axis=1)

    a1 = _rmsnorm_fwd(h0, w["norm_mix_w"])
    fq = _mm(a1, w_r[:, :1536], BF16, "proj_fox")
    xg = _mm(a1, w_r[:, 1536:3584], F32, "proj_gdn")
    gt = _mm(a1, w_r[:, 3584:5632], F32, "proj_gates")
    sm = _mm(a1, w_r[:, 5632:], F32, "proj_small")

    lanes_pad = lambda a, lo: jnp.pad(a, ((0, 0), (lo, LANES - lo - a.shape[1])))
    neg_exp_a = -jnp.exp(w["gdn_a_log"])
    pbias = lanes_pad(w["fgt_bias"], 0) + lanes_pad(w["gdn_dt_bias"], 2 * HEADS)
    pscale = lanes_pad(neg_exp_a, 2 * HEADS)
    scal = _gate_fwd(sm, pbias, pscale)
    gct = scal[:, 2 * HEADS:3 * HEADS].T

    qa, ka, va = _fox_prep(fq, scal)
    oa, qb = _fox_fwd(qa, ka, va)
    o_fox = _fox_post(oa)

    qh, kh, vh = _gdn_pre_fwd(xg, w["gdn_conv_w"], seg)
    og, s0s = _gdn_fwd(qh, kh, vh, scal, gct)
    norm_w = jnp.tile(w["gdn_norm_w"], (1, HEADS))
    ogn = _gdn_post_fwd(og, xg, norm_w, seg)

    yf = _mm(o_fox, w["w_branch_fox"], F32, "branch_fox")
    yg = _mm(ogn, w["w_branch_gdn"], F32, "branch_gdn")
    mix = _mix_fwd(yf, yg, gt, w["gate_bias"])
    h1 = _mm(mix, w["w_out"], F32, "out_proj", res=h0)
    a2 = _rmsnorm_fwd(h1, w["norm_ffn_w"])
    up = _mm(a2, w["ffn_w_up"], F32, "ffn_up")
    act = _ffn_act_fwd(up, w["ffn_conv_w"], w["ffn_conv_b"])
    h2 = _mm(act, w["ffn_w_down"], F32, "ffn_down", res=h1)
    loss, dh2, dh2b, g_final = _final_loss(h2, tgt, w["norm_final_w"].reshape(1, D_MODEL), seq)

    grads = {"norm_final_w": g_final.reshape(D_MODEL)}
    grads["ffn_w_down"] = _mm_tn(act, dh2b, "wgrad_ffn_down")
    dact = _mm(dh2b, w["ffn_w_down"].T, F32, "dgrad_ffn_down")
    dup, g_cw, g_cb = _ffn_act_bwd(up, w["ffn_conv_w"], w["ffn_conv_b"], dact)
    grads["ffn_conv_w"], grads["ffn_conv_b"] = g_cw, g_cb
    grads["ffn_w_up"] = _mm_tn(a2, dup, "wgrad_ffn_up")
    da2 = _mm(dup, w["ffn_w_up"].T, F32, "dgrad_ffn_up")
    dh1, dh1b, grads["norm_ffn_w"] = _rmsnorm_bwd(h1, da2, w["norm_ffn_w"], dh2)
    grads["w_out"] = _mm_tn(mix, dh1b, "wgrad_out")
    dmix = _mm(dh1b, w["w_out"].T, F32, "dgrad_out")
    dyf, dyg, dgt, grads["gate_bias"] = _mix_bwd(dmix, yf, yg, gt, w["gate_bias"])
    grads["w_branch_fox"] = _mm_tn(o_fox, dyf, "wgrad_branch_fox")
    grads["w_branch_gdn"] = _mm_tn(ogn, dyg, "wgrad_branch_gdn")
    do_fox = _mm(dyf, w["w_branch_fox"].T, F32, "dgrad_branch_fox")
    dogn = _mm(dyg, w["w_branch_gdn"].T, F32, "dgrad_branch_gdn")

    dog, dz, g_nw = _gdn_post_bwd(og, xg, norm_w, seg, dogn)
    grads["gdn_norm_w"] = g_nw.reshape(HEADS, HEAD_DIM).sum(axis=0)[None]
    dqh, dkh, dvh, dscal_g, dgct = _gdn_bwd(qh, kh, vh, scal, gct, s0s, dog)
    dxg, grads["gdn_conv_w"] = _gdn_pre_bwd(xg, w["gdn_conv_w"], seg, dqh, dkh, dvh)

    dqa, dka, dva = _fox_bwd(qb, ka, va, _fox_bwd_prep(do_fox, oa))
    dfq, dscal_c = _fox_bwd_post(dqa, dka, dva)

    dscal = dscal_c + dscal_g + lanes_pad(dgct.T, 2 * HEADS)
    dsm, dpb, dps = _gate_bwd(sm, pbias, pscale, dscal)
    grads["fgt_bias"] = dpb[:, :HEADS]
    grads["gdn_dt_bias"] = dpb[:, 2 * HEADS:3 * HEADS]
    grads["gdn_a_log"] = dps[:, 2 * HEADS:3 * HEADS] * neg_exp_a

    dproj = jnp.concatenate([dfq, dxg, dz, dgt, dsm], axis=1)
    g_r = _mm_tn(a1, dproj, "wgrad_in")
    grads["w_in"] = jnp.concatenate([g_r[:, :1536], g_r[:, 5632:5640], g_r[:, 1536:3072], g_r[:, 3072:3584],
                                     g_r[:, 5640:5648], g_r[:, 5648:5656], g_r[:, 3584:5632]], axis=1)
    da1 = _mm(dproj, w_r.T, F32, "dgrad_in")
    dh0, _, grads["norm_mix_w"] = _rmsnorm_bwd(h0, da1, w["norm_mix_w"], dh1)
    grads["meta_tokens"] = dh0[:N_META]
    return loss, dh0[N_META:N_META + seq], grads


def _shard_pieces(arrs):
    return [arrs[n][0] if arrs[n].ndim == 3 else arrs[n] for n in SHARDED]


def _full_grad_blocks(grads):
    g = grads
    cols = lambda a, wd: _col_blocks(a, wd)
    rows = lambda a: a.reshape(N_DEV, a.shape[0] // N_DEV, a.shape[1])
    per = [cols(g["meta_tokens"], 128), cols(g["w_in"], IN_WIDTH // N_DEV), cols(g["gdn_conv_w"], 3 * WIDTH // N_DEV),
           cols(g["w_branch_fox"], D_MODEL // N_DEV), cols(g["w_branch_gdn"], D_MODEL // N_DEV), rows(g["w_out"]),
           cols(g["ffn_w_up"], 2 * D_FF // N_DEV), cols(g["ffn_conv_w"], 2 * D_FF // N_DEV), rows(g["ffn_w_down"])]
    return per


def kernel(x, meta_tokens, w_in, fgt_bias, gdn_conv_w, gdn_a_log, gdn_dt_bias, gdn_norm_w, gate_bias, w_branch_fox, w_branch_gdn, w_out, norm_mix_w, norm_ffn_w, ffn_w_up, ffn_conv_w, ffn_conv_b, ffn_w_down, norm_final_w, loss_target, m_meta_tokens, m_w_in, m_fgt_bias, m_gdn_conv_w, m_gdn_a_log, m_gdn_dt_bias, m_gdn_norm_w, m_gate_bias, m_w_branch_fox, m_w_branch_gdn, m_w_out, m_norm_mix_w, m_norm_ffn_w, m_ffn_w_up, m_ffn_conv_w, m_ffn_conv_b, m_ffn_w_down, m_norm_final_w, v_meta_tokens, v_w_in, v_fgt_bias, v_gdn_conv_w, v_gdn_a_log, v_gdn_dt_bias, v_gdn_norm_w, v_gate_bias, v_w_branch_fox, v_w_branch_gdn, v_w_out, v_norm_mix_w, v_norm_ffn_w, v_ffn_w_up, v_ffn_conv_w, v_ffn_conv_b, v_ffn_w_down, v_norm_final_w):
    wts = dict(meta_tokens=meta_tokens, w_in=w_in, fgt_bias=fgt_bias, gdn_conv_w=gdn_conv_w, gdn_a_log=gdn_a_log,
               gdn_dt_bias=gdn_dt_bias, gdn_norm_w=gdn_norm_w, gate_bias=gate_bias, w_branch_fox=w_branch_fox,
               w_branch_gdn=w_branch_gdn, w_out=w_out, norm_mix_w=norm_mix_w, norm_ffn_w=norm_ffn_w,
               ffn_w_up=ffn_w_up, ffn_conv_w=ffn_conv_w, ffn_conv_b=ffn_conv_b, ffn_w_down=ffn_w_down,
               norm_final_w=norm_final_w)
    mom = dict(meta_tokens=m_meta_tokens, w_in=m_w_in, fgt_bias=m_fgt_bias, gdn_conv_w=m_gdn_conv_w,
               gdn_a_log=m_gdn_a_log, gdn_dt_bias=m_gdn_dt_bias, gdn_norm_w=m_gdn_norm_w, gate_bias=m_gate_bias,
               w_branch_fox=m_w_branch_fox, w_branch_gdn=m_w_branch_gdn, w_out=m_w_out, norm_mix_w=m_norm_mix_w,
               norm_ffn_w=m_norm_ffn_w, ffn_w_up=m_ffn_w_up, ffn_conv_w=m_ffn_conv_w, ffn_conv_b=m_ffn_conv_b,
               ffn_w_down=m_ffn_w_down, norm_final_w=m_norm_final_w)
    var = dict(meta_tokens=v_meta_tokens, w_in=v_w_in, fgt_bias=v_fgt_bias, gdn_conv_w=v_gdn_conv_w,
               gdn_a_log=v_gdn_a_log, gdn_dt_bias=v_gdn_dt_bias, gdn_norm_w=v_gdn_norm_w, gate_bias=v_gate_bias,
               w_branch_fox=v_w_branch_fox, w_branch_gdn=v_w_branch_gdn, w_out=v_w_out, norm_mix_w=v_norm_mix_w,
               norm_ffn_w=v_norm_ffn_w, ffn_w_up=v_ffn_w_up, ffn_conv_w=v_ffn_conv_w, ffn_conv_b=v_ffn_conv_b,
               ffn_w_down=v_ffn_w_down, norm_final_w=v_norm_final_w)

    w_pieces = _shard_pieces(wts)
    shard_shapes = [p.shape for p in w_pieces]
    n_shard = sum(int(np.prod(s)) for s in shard_shapes)
    rows_big = _rows_for(n_shard)
    sh = dict(zip(SHARDED, w_pieces))
    exact = ("meta_tokens", "gdn_conv_w", "ffn_conv_w")
    mm_names = [n for n in SHARDED if n not in exact]
    gather_parts = [sh[n].astype(BF16) for n in mm_names] + [p for n in exact for p in _split3(sh[n])]
    rows_gather = _rows_for(sum(int(np.prod(p.shape)) for p in gather_parts), 16)
    gathered = _all_gather(_pad_flat(gather_parts, rows_gather))
    parts = _unpack_gathered(gathered, [p.shape for p in gather_parts])
    g_in, g_bf, g_bg, g_out, g_up, g_down = parts[:6]
    f32_full = [_cat_cols(parts[6 + 3 * i].astype(F32) + parts[7 + 3 * i].astype(F32) + parts[8 + 3 * i].astype(F32))
                for i in range(len(exact))]
    full = dict(
        meta_tokens=f32_full[0], w_in=_cat_cols(g_in), gdn_conv_w=f32_full[1],
        w_branch_fox=_cat_cols(g_bf), w_branch_gdn=_cat_cols(g_bg), w_out=g_out.reshape(D_MODEL, D_MODEL),
        ffn_w_up=_cat_cols(g_up), ffn_conv_w=f32_full[2], ffn_w_down=g_down.reshape(D_FF, D_MODEL),
        fgt_bias=fgt_bias, gdn_a_log=gdn_a_log, gdn_dt_bias=gdn_dt_bias, gdn_norm_w=gdn_norm_w, gate_bias=gate_bias,
        norm_mix_w=norm_mix_w, norm_ffn_w=norm_ffn_w, ffn_conv_b=ffn_conv_b, norm_final_w=norm_final_w)

    loss, grad_x, grads = _local_step(x[0], loss_target[0], full)

    blocks = _full_grad_blocks(grads)
    big = jnp.stack([_pad_flat([b[j] for b in blocks], rows_big) for j in range(N_DEV)]).astype(BF16)
    rep_parts = [grads[n] for n in REPLICATED] + [loss[:, :1]]
    rep_shapes = [wts[n].shape for n in REPLICATED]
    rows_small = _rows_for(sum(int(np.prod(p.shape)) for p in rep_parts), 8)
    rbig, rsmall = _grad_exchange(big, _pad_flat(rep_parts, rows_small))

    pack = lambda d, names, rows: _pad_flat([d[n][0] if (d[n].ndim == 3 and n in SHARDED) else d[n] for n in names],
                                            rows)
    outs_s = _adamw(pack(wts, SHARDED, rows_big), pack(mom, SHARDED, rows_big), pack(var, SHARDED, rows_big), rbig,
                    "adamw_sharded")
    rep_w = _pad_flat([wts[n] for n in REPLICATED] + [jnp.zeros((1, 1), F32)], rows_small)
    rep_m = _pad_flat([mom[n] for n in REPLICATED] + [jnp.zeros((1, 1), F32)], rows_small)
    rep_v = _pad_flat([var[n] for n in REPLICATED] + [jnp.ones((1, 1), F32)], rows_small)
    outs_r = _adamw(rep_w, rep_m, rep_v, rsmall, "adamw_replicated")

    result = {}
    for kind, bs, br in zip(("grad", "delta", "new_m", "new_v"), outs_s, outs_r):
        for n, a in zip(SHARDED, _unpack(bs, shard_shapes)):
            result[kind, n] = a.reshape(wts[n].shape)
        for n, a in zip(REPLICATED, _unpack(br, rep_shapes)):
            result[kind, n] = a
    n_rep = sum(int(np.prod(s)) for s in rep_shapes)
    total_loss = outs_r[0].reshape(-1)[n_rep]
    out = [total_loss, grad_x[None]]
    for kind in ("grad", "delta", "new_m", "new_v"):
        out += [result[kind, n] for n in WEIGHTS]
    return tuple(out)
```

```python
import functools

import jax
import jax.numpy as jnp
import numpy as np
from jax import lax
from jax.experimental import pallas as pl
from jax.experimental.pallas import tpu as pltpu

F32 = jnp.float32
BF16 = jnp.bfloat16

D_MODEL = 1024
N_META = 16
HEADS = 8
HEAD_DIM = 64
WIDTH = HEADS * HEAD_DIM
CHUNK = 64
GDN_CONV = 4
D_FF = 2816
FFN_CONV = 3
IN_WIDTH = 5656
IN_PAD = 5760
RMS_EPS = 1e-6
NEG = -1e30
AUG = 128
N_DEV = 8
LANES = 128

ADAM_LR = 0.001
ADAM_B1 = 0.9
ADAM_B2 = 0.999
ADAM_EPS = 1e-08
ADAM_WD = 0.01
ADAM_STEP = 10

VMEM_LIMIT = 56 * 1024 * 1024
MM_VMEM_BUDGET = 36 * 1024 * 1024
HI = lax.Precision.HIGH
MESH_ID = pl.DeviceIdType.MESH


def _pick(n, cands):
    for c in cands:
        if n % c == 0:
            return c
    raise ValueError(f"no tile for {n} in {cands}")


def _params(*sem):
    return pltpu.CompilerParams(dimension_semantics=sem if sem else None, vmem_limit_bytes=VMEM_LIMIT)


def _padded_tokens(seq):
    t = -(-(N_META + seq) // 128) * 128
    if t > 1280 and t % 640:
        t = -(-t // 640) * 640
    return t


ROW_TILES = (640, 512, 384, 256, 128)


def _rmsnorm_fwd(h, gain):
    t, d = h.shape
    tr = _pick(t, ROW_TILES)

    def body(h_ref, g_ref, o_ref):
        x = h_ref[...]
        r = lax.rsqrt(jnp.mean(x * x, axis=-1, keepdims=True) + RMS_EPS)
        o_ref[...] = (x * r * g_ref[...]).astype(o_ref.dtype)

    return pl.pallas_call(
        body, grid=(t // tr,), name="rmsnorm_fwd",
        in_specs=[pl.BlockSpec((tr, d), lambda i: (i, 0)), pl.BlockSpec((1, d), lambda i: (0, 0))],
        out_specs=pl.BlockSpec((tr, d), lambda i: (i, 0)),
        out_shape=jax.ShapeDtypeStruct((t, d), BF16),
        compiler_params=_params("arbitrary"),
    )(h, gain)


def _rmsnorm_bwd(h, dy, gain, dres):
    t, d = h.shape
    tr = _pick(t, (320, 256, 128))

    def body(h_ref, dy_ref, g_ref, dres_ref, dh_ref, dhb_ref, dg_ref):
        x = h_ref[...]
        dyv = dy_ref[...]
        r = lax.rsqrt(jnp.mean(x * x, axis=-1, keepdims=True) + RMS_EPS)
        gy = dyv * g_ref[...]
        m = jnp.mean(gy * x, axis=-1, keepdims=True)
        dh = dres_ref[...] + r * gy - x * (r * r * r * m)
        dh_ref[...] = dh
        dhb_ref[...] = dh.astype(BF16)

        @pl.when(pl.program_id(0) == 0)
        def _():
            dg_ref[...] = jnp.zeros_like(dg_ref)

        dg_ref[...] += jnp.sum(dyv * x * r, axis=0, keepdims=True)

    row = pl.BlockSpec((tr, d), lambda i: (i, 0))
    vec = pl.BlockSpec((1, d), lambda i: (0, 0))
    return pl.pallas_call(
        body, grid=(t // tr,), name="rmsnorm_bwd",
        in_specs=[row, row, vec, row], out_specs=[row, row, vec],
        out_shape=[jax.ShapeDtypeStruct((t, d), F32), jax.ShapeDtypeStruct((t, d), BF16),
                   jax.ShapeDtypeStruct((1, d), F32)],
        compiler_params=_params("arbitrary"),
    )(h, dy, gain, dres)


def _mm(a, b, out_dtype, name, res=None):
    m, k = a.shape
    _, n = b.shape
    tm = _pick(m, ROW_TILES)
    out_bytes = jnp.dtype(out_dtype).itemsize + (4 if res is not None else 0)
    fits = lambda tn: 4 * tm * k + 4 * k * tn + 2 * tm * tn * out_bytes <= MM_VMEM_BUDGET
    tn = next(c for c in (n, 2816, 2048, 1536, 1408, 1024, 512, 384, 256, 128) if n % c == 0 and fits(c))

    def body(*refs):
        if res is None:
            a_ref, b_ref, o_ref = refs
        else:
            a_ref, b_ref, r_ref, o_ref = refs
        out = jnp.dot(a_ref[...], b_ref[...], preferred_element_type=F32)
        if res is not None:
            out = out + r_ref[...]
        o_ref[...] = out.astype(o_ref.dtype)

    in_specs = [pl.BlockSpec((tm, k), lambda i, j: (i, 0)), pl.BlockSpec((k, tn), lambda i, j: (0, j))]
    args = [a, b]
    if res is not None:
        in_specs.append(pl.BlockSpec((tm, tn), lambda i, j: (i, j)))
        args.append(res)
    return pl.pallas_call(
        body, grid=(m // tm, n // tn), name=name,
        in_specs=in_specs, out_specs=pl.BlockSpec((tm, tn), lambda i, j: (i, j)),
        out_shape=jax.ShapeDtypeStruct((m, n), out_dtype),
        compiler_params=_params("parallel", "parallel"),
    )(*args)


def _mm_tn(a, g, name):
    t, k = a.shape
    _, n = g.shape
    tk = _pick(k, (1024, 1408, 512))
    tn = _pick(n, (512, 640, 384, 256, 128))
    tt = next(c for c in (3328, 1280) + ROW_TILES
              if t % c == 0 and 4 * c * (tk + tn) + 8 * tk * tn <= MM_VMEM_BUDGET)
    nt = t // tt

    def body(a_ref, g_ref, o_ref):
        @pl.when(pl.program_id(2) == 0)
        def _():
            o_ref[...] = jnp.zeros_like(o_ref)

        o_ref[...] += lax.dot_general(a_ref[...], g_ref[...], (((0,), (0,)), ((), ())),
                                      preferred_element_type=F32)

    return pl.pallas_call(
        body, grid=(k // tk, n // tn, nt), name=name,
        in_specs=[pl.BlockSpec((tt, tk), lambda i, j, s: (s, i)), pl.BlockSpec((tt, tn), lambda i, j, s: (s, j))],
        out_specs=pl.BlockSpec((tk, tn), lambda i, j, s: (i, j)),
        out_shape=jax.ShapeDtypeStruct((k, n), F32),
        compiler_params=_params("parallel", "parallel", "arbitrary"),
    )(a, g)


def _split3_exact(x):
    def top(v):
        return lax.bitcast_convert_type(lax.bitcast_convert_type(v, jnp.int32) & jnp.int32(-65536), F32)

    hi = top(x)
    r1 = x - hi
    mid = top(r1)
    return hi, mid, r1 - mid


def _pair_head(ref, h, rows):
    x = ref[:, 128 * (h // 2):128 * (h // 2) + 128].astype(F32)
    return pltpu.roll(x, HEAD_DIM, axis=1) if h % 2 else x


def _lanes(rows):
    return lax.broadcasted_iota(jnp.int32, (rows, AUG), 1)


def _fox_prep(fq, scal):
    t = fq.shape[0]
    tt = _pick(t, (256, 128))

    def body(q_ref, k_ref, v_ref, s_ref, qa_ref, ka_ref, va_ref):
        lane = _lanes(tt)
        chi, cmid, clo = _split3_exact(s_ref[...])
        ones = lambda lo: jnp.where((lane >= lo) & (lane < lo + 3), 1.0, 0.0)
        for h in range(HEADS):
            col = lambda a: jnp.broadcast_to(a[:, h:h + 1], (tt, AUG))
            c1, c2, c3 = col(chi), col(cmid), col(clo)
            qx = jnp.where(lane == 64, c1, jnp.where(lane == 65, c2, jnp.where(lane == 66, c3, ones(67))))
            kx = jnp.where(lane == 67, -c1, jnp.where(lane == 68, -c2, jnp.where(lane == 69, -c3, ones(64) + ones(70))))
            qa_ref[h] = jnp.where(lane < HEAD_DIM, _pair_head(q_ref, h, tt) * (HEAD_DIM ** -0.5), qx).astype(BF16)
            ka_ref[h] = jnp.where(lane < HEAD_DIM, _pair_head(k_ref, h, tt), kx).astype(BF16)
            va_ref[h] = jnp.where(lane < HEAD_DIM, _pair_head(v_ref, h, tt), ones(64)).astype(BF16)

    out = pl.BlockSpec((HEADS, tt, AUG), lambda i: (0, i, 0))
    shp = jax.ShapeDtypeStruct((HEADS, t, AUG), BF16)
    return pl.pallas_call(
        body, grid=(t // tt,), name="fox_prep",
        in_specs=[pl.BlockSpec((tt, WIDTH), lambda i: (i, 0)), pl.BlockSpec((tt, WIDTH), lambda i: (i, 1)),
                  pl.BlockSpec((tt, WIDTH), lambda i: (i, 2)), pl.BlockSpec((tt, LANES), lambda i: (i, 0))],
        out_specs=[out, out, out], out_shape=[shp, shp, shp],
        compiler_params=_params("parallel"),
    )(fq, fq, fq, scal)


def _fox_post(oa):
    t = oa.shape[1]
    tt = _pick(t, (256, 128))

    def body(o_ref, out_ref):
        out_ref[...] = jnp.concatenate([o_ref[h][:, :HEAD_DIM] for h in range(HEADS)], axis=1).astype(BF16)

    return pl.pallas_call(
        body, grid=(t // tt,), name="fox_post",
        in_specs=[pl.BlockSpec((HEADS, tt, AUG), lambda i: (0, i, 0))],
        out_specs=pl.BlockSpec((tt, WIDTH), lambda i: (i, 0)),
        out_shape=jax.ShapeDtypeStruct((t, WIDTH), BF16),
        compiler_params=_params("parallel"),
    )(oa)


def _fox_bwd_prep(do, oa):
    t = do.shape[0]
    tt = _pick(t, (256, 128))

    def body(d_ref, o_ref, out_ref):
        lane = _lanes(tt)
        for h in range(HEADS):
            x = _pair_head(d_ref, h, tt)
            delta = jnp.sum(jnp.where(lane < HEAD_DIM, x * o_ref[h], 0.0), axis=1, keepdims=True)
            hi, mid, lo = _split3_exact(jnp.broadcast_to(-delta, (tt, AUG)))
            ex = jnp.where(lane == 64, hi, jnp.where(lane == 65, mid, jnp.where(lane == 66, lo, 0.0)))
            out_ref[h] = jnp.where(lane < HEAD_DIM, x, ex).astype(BF16)

    hm = pl.BlockSpec((HEADS, tt, AUG), lambda i: (0, i, 0))
    return pl.pallas_call(
        body, grid=(t // tt,), name="fox_bwd_prep",
        in_specs=[pl.BlockSpec((tt, WIDTH), lambda i: (i, 0)), hm], out_specs=hm,
        out_shape=jax.ShapeDtypeStruct((HEADS, t, AUG), BF16),
        compiler_params=_params("parallel"),
    )(do, oa)


def _fox_bwd_post(dqa, dka, dva):
    t = dqa.shape[1]
    tt = _pick(t, (256, 128))

    def body(dq_ref, dk_ref, dv_ref, out_ref, dsc_ref):
        lane = _lanes(tt)
        heads = lambda ref: jnp.concatenate([ref[h][:, :HEAD_DIM] for h in range(HEADS)], axis=1)
        out_ref[:, 0:WIDTH] = (heads(dq_ref) * (HEAD_DIM ** -0.5)).astype(BF16)
        out_ref[:, WIDTH:2 * WIDTH] = heads(dk_ref).astype(BF16)
        out_ref[:, 2 * WIDTH:] = heads(dv_ref).astype(BF16)
        dsc = jnp.zeros((tt, LANES), F32)
        for h in range(HEADS):
            both = jnp.where(lane == HEAD_DIM, dq_ref[h], 0.0) - jnp.where(lane == HEAD_DIM + 3, dk_ref[h], 0.0)
            dsc = jnp.where(lane == h, jnp.sum(both, axis=1, keepdims=True), dsc)
        dsc_ref[...] = dsc

    hm = pl.BlockSpec((HEADS, tt, AUG), lambda i: (0, i, 0))
    return pl.pallas_call(
        body, grid=(t // tt,), name="fox_bwd_post",
        in_specs=[hm, hm, hm],
        out_specs=[pl.BlockSpec((tt, 3 * WIDTH), lambda i: (i, 0)), pl.BlockSpec((tt, LANES), lambda i: (i, 0))],
        out_shape=[jax.ShapeDtypeStruct((t, 3 * WIDTH), BF16), jax.ShapeDtypeStruct((t, LANES), F32)],
        compiler_params=_params("parallel"),
    )(dqa, dka, dva)


def _fox_fwd(qa, ka, va, tq=None):
    h, t, _ = qa.shape
    tq = tq or _pick(t, ROW_TILES)

    def body(q_ref, k_ref, v_ref, o_ref, qb_ref):
        i = pl.program_id(1)
        q = q_ref[...]
        row = lax.broadcasted_iota(jnp.int32, (tq, tq), 0)
        col = lax.broadcasted_iota(jnp.int32, (tq, tq), 1)

        def step(j, carry, masked):
            m, acc = carry
            start = pl.multiple_of(j * tq, tq)
            kj = k_ref[pl.ds(start, tq), :]
            vj = v_ref[pl.ds(start, tq), :]
            s = lax.dot_general(q, kj, (((1,), (1,)), ((), ())), preferred_element_type=F32)
            if masked:
                s = jnp.where(row >= col, s, NEG)
            m_new = jnp.maximum(m, jnp.max(s, axis=-1, keepdims=True))
            p = jnp.exp(s - m_new)
            alpha = jnp.exp(m - m_new)
            acc = acc * alpha + jnp.dot(p.astype(BF16), vj, preferred_element_type=F32)
            return m_new, acc

        carry = (jnp.full((tq, 1), NEG, F32), jnp.zeros((tq, AUG), F32))
        carry = lax.fori_loop(0, i, lambda j, c: step(j, c, False), carry)
        m, acc = step(i, carry, True)
        lane = lax.broadcasted_iota(jnp.int32, (tq, AUG), 1)
        l = jnp.sum(jnp.where(lane == HEAD_DIM, acc, 0.0), axis=-1, keepdims=True)
        lse = jnp.broadcast_to(m + jnp.log(l), (tq, AUG))
        o_ref[...] = jnp.where(lane < HEAD_DIM, acc / l, lse)
        hi, mid, lo = _split3_exact(-lse)
        qb = jnp.where(lane == 70, hi, jnp.where(lane == 71, mid, jnp.where(lane == 72, lo, q.astype(F32))))
        qb_ref[...] = qb.astype(BF16)

    blk = pl.BlockSpec((None, tq, AUG), lambda hh, i: (hh, i, 0))
    return pl.pallas_call(
        body, grid=(h, t // tq), name="fox_fwd",
        in_specs=[blk, pl.BlockSpec((None, t, AUG), lambda hh, i: (hh, 0, 0)),
                  pl.BlockSpec((None, t, AUG), lambda hh, i: (hh, 0, 0))],
        out_specs=[blk, blk],
        out_shape=[jax.ShapeDtypeStruct((h, t, AUG), F32), jax.ShapeDtypeStruct((h, t, AUG), BF16)],
        compiler_params=_params("parallel", "arbitrary"),
    )(qa, ka, va)


def _fox_bwd(qb, ka, va, doa, tq=None):
    h, t, _ = qb.shape
    tq = tq or _pick(t, ROW_TILES)
    nq = t // tq

    def body(q_ref, k_ref, v_ref, do_ref, dq_ref, dk_ref, dv_ref):
        j = pl.program_id(1)

        @pl.when(j == 0)
        def _():
            dq_ref[...] = jnp.zeros_like(dq_ref)

        kj = k_ref[...]
        vj = v_ref[...]
        krow = lax.broadcasted_iota(jnp.int32, (tq, tq), 0)
        qcol = lax.broadcasted_iota(jnp.int32, (tq, tq), 1)

        def step(i, carry, masked):
            dk, dv = carry
            start = pl.multiple_of(i * tq, tq)
            qi = q_ref[pl.ds(start, tq), :]
            doi = do_ref[pl.ds(start, tq), :]
            st = lax.dot_general(kj, qi, (((1,), (1,)), ((), ())), preferred_element_type=F32)
            if masked:
                st = jnp.where(qcol >= krow, st, NEG)
            pt = jnp.exp(st)
            dpt = lax.dot_general(vj, doi, (((1,), (1,)), ((), ())), preferred_element_type=F32)
            dst = (pt * dpt).astype(BF16)
            dv = dv + jnp.dot(pt.astype(BF16), doi, preferred_element_type=F32)
            dk = dk + jnp.dot(dst, qi, preferred_element_type=F32)
            dq_ref[pl.ds(start, tq), :] += lax.dot_general(dst, kj, (((0,), (0,)), ((), ())),
                                                            preferred_element_type=F32)
            return dk, dv

        zero = jnp.zeros((tq, AUG), F32)
        carry = step(j, (zero, zero), True)
        dk, dv = lax.fori_loop(j + 1, nq, lambda i, c: step(i, c, False), carry)
        dk_ref[...] = dk
        dv_ref[...] = dv

    full = pl.BlockSpec((None, t, AUG), lambda hh, j: (hh, 0, 0))
    blk = pl.BlockSpec((None, tq, AUG), lambda hh, j: (hh, j, 0))
    shp = jax.ShapeDtypeStruct((h, t, AUG), F32)
    return pl.pallas_call(
        body, grid=(h, nq), name="fox_bwd",
        in_specs=[full, blk, blk, full], out_specs=[full, blk, blk], out_shape=[shp, shp, shp],
        compiler_params=_params("parallel", "arbitrary"),
    )(qb, ka, va, doa)


def _seg_matrix():
    idx = np.arange(WIDTH) // HEAD_DIM
    return jnp.asarray((idx[:, None] == idx[None, :]).astype(np.float32))


def _segsum(x, e):
    return jnp.dot(x, e, precision=HI, preferred_element_type=F32)


def _silu(x):
    return x * jax.nn.sigmoid(x)


def _silu_grad(x):
    s = jax.nn.sigmoid(x)
    return s * (1.0 + x * (1.0 - s))


def _shift_down(x, prev8, k):
    r = pltpu.roll(x, k, axis=0)
    p = pltpu.roll(prev8, k, axis=0)
    row = lax.broadcasted_iota(jnp.int32, prev8.shape, 0)
    head = jnp.where(row < k, p, r[:8])
    return jnp.concatenate([head, r[8:]], axis=0)


def _shift_up(x, next8, k):
    n = x.shape[0]
    r = pltpu.roll(x, n - k, axis=0)
    p = pltpu.roll(next8, 8 - k, axis=0)
    row = lax.broadcasted_iota(jnp.int32, next8.shape, 0)
    tail = jnp.where(row >= 8 - k, p, r[n - 8:])
    return jnp.concatenate([r[:n - 8], tail], axis=0)


def _causal_conv(x, prev8, w_ref, width):
    y = x * w_ref[width - 1:width, :]
    for k in range(1, width):
        y = y + _shift_down(x, prev8, k) * w_ref[width - 1 - k:width - k, :]
    return y


def _causal_conv_bwd(x, prev8, dy, dnext8, w_ref, dw_ref, width):
    dx = dy * w_ref[width - 1:width, :]
    dw_ref[width - 1:width, :] += jnp.sum(dy * x, axis=0, keepdims=True)
    for k in range(1, width):
        dx = dx + _shift_up(dy, dnext8, k) * w_ref[width - 1 - k:width - k, :]
        dw_ref[width - 1 - k:width - k, :] += jnp.sum(dy * _shift_down(x, prev8, k), axis=0, keepdims=True)
    return dx


def _prev_spec(tt, width):
    return pl.BlockSpec((8, width), lambda i: (jnp.maximum(i * (tt // 8) - 1, 0), 0))


def _store_heads(ref, x):
    for h in range(HEADS):
        ref[h] = x[:, HEAD_DIM * h:HEAD_DIM * (h + 1)]


def _load_heads(ref):
    return jnp.concatenate([ref[h] for h in range(HEADS)], axis=1)


def _softplus(z):
    return jnp.maximum(z, 0.0) + jnp.log1p(jnp.exp(-jnp.abs(z)))


def _tri_masks(tt):
    r = lax.broadcasted_iota(jnp.int32, (tt, tt), 0)
    c = lax.broadcasted_iota(jnp.int32, (tt, tt), 1)
    same_chunk = lax.shift_right_logical(r, 6) == lax.shift_right_logical(c, 6)
    return r, c, same_chunk


def _gate_fwd(small, pbias, pscale):
    t = small.shape[0]
    tt = _pick(t, (256, 128))

    def body(x_ref, pb_ref, ps_ref, o_ref, carry_ref):
        @pl.when(pl.program_id(0) == 0)
        def _():
            carry_ref[...] = jnp.zeros_like(carry_ref)

        lane = lax.broadcasted_iota(jnp.int32, (tt, LANES), 1)
        z = x_ref[...] + pb_ref[...]
        log_f = jnp.where(lane < HEADS, -_softplus(-z), 0.0)
        g = jnp.where((lane >= 2 * HEADS) & (lane < 3 * HEADS), ps_ref[...] * _softplus(z), 0.0)
        r, c, same_chunk = _tri_masks(tt)
        lower = jnp.where(r >= c, 1.0, 0.0)
        lower_chunk = jnp.where((r >= c) & same_chunk, 1.0, 0.0)
        csum = jnp.dot(lower, log_f, precision=lax.Precision.HIGHEST, preferred_element_type=F32) + carry_ref[...]
        gc = jnp.dot(lower_chunk, g, precision=lax.Precision.HIGHEST, preferred_element_type=F32)
        carry_ref[...] += jnp.sum(log_f, axis=0, keepdims=True)
        o_ref[...] = jnp.where(lane < HEADS, csum, jnp.where(lane < 2 * HEADS, jax.nn.sigmoid(z), gc))

    row = pl.BlockSpec((tt, LANES), lambda i: (i, 0))
    vec = pl.BlockSpec((1, LANES), lambda i: (0, 0))
    return pl.pallas_call(
        body, grid=(t // tt,), name="gate_fwd", in_specs=[row, vec, vec], out_specs=row,
        out_shape=jax.ShapeDtypeStruct((t, LANES), F32),
        scratch_shapes=[pltpu.VMEM((1, LANES), F32)],
        compiler_params=_params("arbitrary"),
    )(small, pbias, pscale)


def _gate_bwd(small, pbias, pscale, dscal):
    t = small.shape[0]
    tt = _pick(t, (256, 128))
    nt = t // tt

    def body(x_ref, pb_ref, ps_ref, d_ref, dx_ref, dpb_ref, dps_ref, carry_ref):
        @pl.when(pl.program_id(0) == 0)
        def _():
            carry_ref[...] = jnp.zeros_like(carry_ref)
            dpb_ref[...] = jnp.zeros_like(dpb_ref)
            dps_ref[...] = jnp.zeros_like(dps_ref)

        lane = lax.broadcasted_iota(jnp.int32, (tt, LANES), 1)
        z = x_ref[...] + pb_ref[...]
        d = d_ref[...]
        dc = jnp.where(lane < HEADS, d, 0.0)
        dbeta = jnp.where((lane >= HEADS) & (lane < 2 * HEADS), d, 0.0)
        dgc = jnp.where((lane >= 2 * HEADS) & (lane < 3 * HEADS), d, 0.0)
        r, c, same_chunk = _tri_masks(tt)
        upper = jnp.where(r <= c, 1.0, 0.0)
        upper_chunk = jnp.where((r <= c) & same_chunk, 1.0, 0.0)
        dlogf = jnp.dot(upper, dc, precision=lax.Precision.HIGHEST, preferred_element_type=F32) + carry_ref[...]
        dg = jnp.dot(upper_chunk, dgc, precision=lax.Precision.HIGHEST, preferred_element_type=F32)
        carry_ref[...] += jnp.sum(dc, axis=0, keepdims=True)
        sg = jax.nn.sigmoid(z)
        dz = dlogf * (1.0 - sg) + dbeta * sg * (1.0 - sg) + dg * ps_ref[...] * sg
        dx_ref[...] = dz.astype(dx_ref.dtype)
        dpb_ref[...] += jnp.sum(dz, axis=0, keepdims=True)
        dps_ref[...] += jnp.sum(dg * _softplus(z), axis=0, keepdims=True)

    row = pl.BlockSpec((tt, LANES), lambda i: (nt - 1 - i, 0))
    vec = pl.BlockSpec((1, LANES), lambda i: (0, 0))
    return pl.pallas_call(
        body, grid=(nt,), name="gate_bwd", in_specs=[row, vec, vec, row], out_specs=[row, vec, vec],
        out_shape=[jax.ShapeDtypeStruct((t, LANES), BF16), jax.ShapeDtypeStruct((1, LANES), F32),
                   jax.ShapeDtypeStruct((1, LANES), F32)],
        scratch_shapes=[pltpu.VMEM((1, LANES), F32)],
        compiler_params=_params("arbitrary"),
    )(small, pbias, pscale, dscal)


def _gdn_pre_fwd(xg, conv_w, seg):
    t = xg.shape[0]
    c3 = 3 * WIDTH
    tt = _pick(t, (320, 256, 128))

    def body(x_ref, p_ref, w_ref, e_ref, q_ref, k_ref, v_ref):
        x = x_ref[...]
        prev = jnp.where(pl.program_id(0) == 0, 0.0, p_ref[...])
        s = _silu(_causal_conv(x, prev, w_ref, GDN_CONV))
        e = e_ref[...]
        q = s[:, :WIDTH]
        k = s[:, WIDTH:2 * WIDTH]
        _store_heads(q_ref, q * lax.rsqrt(_segsum(q * q, e) + RMS_EPS) * (HEAD_DIM ** -0.5))
        _store_heads(k_ref, k * lax.rsqrt(_segsum(k * k, e) + RMS_EPS))
        _store_heads(v_ref, s[:, 2 * WIDTH:])

    out = pl.BlockSpec((HEADS, tt, HEAD_DIM), lambda i: (0, i, 0))
    shp = jax.ShapeDtypeStruct((HEADS, t, HEAD_DIM), F32)
    return pl.pallas_call(
        body, grid=(t // tt,), name="gdn_pre_fwd",
        in_specs=[pl.BlockSpec((tt, c3), lambda i: (i, 0)), _prev_spec(tt, c3),
                  pl.BlockSpec((GDN_CONV, c3), lambda i: (0, 0)), pl.BlockSpec((WIDTH, WIDTH), lambda i: (0, 0))],
        out_specs=[out, out, out], out_shape=[shp, shp, shp],
        compiler_params=_params("arbitrary"),
    )(xg, xg, conv_w, seg)


def _gdn_pre_bwd(xg, conv_w, seg, dqn, dkn, dv):
    t = xg.shape[0]
    c3 = 3 * WIDTH
    tt = _pick(t, (320, 256, 128))
    nt = t // tt

    def body(x_ref, p_ref, w_ref, e_ref, dq_ref, dk_ref, dv_ref, dx_ref, dw_ref, carry_ref):
        step = pl.program_id(0)
        x = x_ref[...]
        e = e_ref[...]
        prev = jnp.where(step == nt - 1, 0.0, p_ref[...])
        y = _causal_conv(x, prev, w_ref, GDN_CONV)
        s = _silu(y)
        q = s[:, :WIDTH]
        k = s[:, WIDTH:2 * WIDTH]
        rq = lax.rsqrt(_segsum(q * q, e) + RMS_EPS)
        rk = lax.rsqrt(_segsum(k * k, e) + RMS_EPS)
        gq = _load_heads(dq_ref) * (HEAD_DIM ** -0.5)
        gk = _load_heads(dk_ref)
        dq = rq * gq - q * (rq * rq * rq) * _segsum(gq * q, e)
        dk = rk * gk - k * (rk * rk * rk) * _segsum(gk * k, e)
        dy = jnp.concatenate([dq, dk, _load_heads(dv_ref)], axis=1) * _silu_grad(y)

        @pl.when(step == 0)
        def _():
            carry_ref[...] = jnp.zeros_like(carry_ref)
            dw_ref[...] = jnp.zeros_like(dw_ref)

        dx = _causal_conv_bwd(x, prev, dy, carry_ref[...], w_ref, dw_ref, GDN_CONV)
        dx_ref[...] = dx.astype(dx_ref.dtype)
        carry_ref[...] = dy[:8]

    rev = lambda i: (nt - 1 - i, 0)
    blk = pl.BlockSpec((HEADS, tt, HEAD_DIM), lambda i: (0, nt - 1 - i, 0))
    return pl.pallas_call(
        body, grid=(nt,), name="gdn_pre_bwd",
        in_specs=[pl.BlockSpec((tt, c3), rev),
                  pl.BlockSpec((8, c3), lambda i: (jnp.maximum((nt - 1 - i) * (tt // 8) - 1, 0), 0)),
                  pl.BlockSpec((GDN_CONV, c3), lambda i: (0, 0)), pl.BlockSpec((WIDTH, WIDTH), lambda i: (0, 0)),
                  blk, blk, blk],
        out_specs=[pl.BlockSpec((tt, c3), rev), pl.BlockSpec((GDN_CONV, c3), lambda i: (0, 0))],
        out_shape=[jax.ShapeDtypeStruct((t, c3), BF16), jax.ShapeDtypeStruct((GDN_CONV, c3), F32)],
        scratch_shapes=[pltpu.VMEM((8, c3), F32)],
        compiler_params=_params("arbitrary"),
    )(xg, xg, conv_w, seg, dqn, dkn, dv)


def _bmm(a, b, ca, cb, precision=None):
    return lax.dot_general(a, b, (((ca,), (cb,)), ((0,), (0,))), precision=precision, preferred_element_type=F32)


def _bf(x):
    return x.astype(BF16)


def _tri_inverse(a, eye):
    x = -a
    tinv = eye + x
    pw = x
    for _ in range(5):
        pb = _bf(pw)
        pw = _bmm(pb, pb, 2, 1)
        tinv = tinv + _bmm(_bf(tinv), _bf(pw), 2, 1)
    resid = eye - _bmm(eye + a, tinv, 2, 1, precision=HI)
    return tinv + _bmm(_bf(tinv), _bf(resid), 2, 1)


def _gdn_intra(q, k, v, bc, gcc, gcr):
    ii = lax.broadcasted_iota(jnp.int32, (CHUNK, CHUNK), 0)
    jj = lax.broadcasted_iota(jnp.int32, (CHUNK, CHUNK), 1)
    tril = (ii >= jj)[None]
    strict = (ii > jj)[None]
    eye = jnp.where(ii == jj, 1.0, 0.0).astype(F32)[None]
    last = (ii == CHUNK - 1)[None]
    dm = jnp.exp(jnp.where(tril, gcc - gcr, NEG))
    gam = jnp.exp(gcc)
    kb = k * bc
    vb = v * bc
    kk = _bmm(_bf(kb), _bf(k), 2, 2)
    a = jnp.where(strict, kk * dm, 0.0)
    tinv = _tri_inverse(a, eye)
    kbg = kb * gam
    u = _bmm(tinv, vb, 2, 1, precision=HI)
    wk = _bmm(tinv, kbg, 2, 1, precision=HI)
    qk = _bmm(_bf(q), _bf(k), 2, 2)
    p = jnp.where(tril, qk * dm, 0.0)
    gl = jnp.sum(jnp.where(last, gcc, 0.0), axis=1, keepdims=True)
    edec = jnp.exp(gl - gcc)
    return dict(tril=tril, strict=strict, dm=dm, gam=gam, kb=kb, kk=kk, a=a, tinv=tinv, u=u, wk=wk, qk=qk, p=p,
                qg=q * gam, kt=k * edec, edec=edec, gaml=jnp.exp(gl), last=last)


def _gate_tiles(sc, gct, nb):
    rows = nb * CHUNK
    cols = lambda lane0: jnp.stack([jnp.broadcast_to(sc[:, lane0 + h:lane0 + h + 1], (rows, HEAD_DIM))
                                    for h in range(HEADS)], axis=0).reshape(HEADS * nb, CHUNK, HEAD_DIM)
    gcr = jnp.stack([jnp.broadcast_to(gct[h:h + 1, n * CHUNK:(n + 1) * CHUNK], (CHUNK, CHUNK))
                     for h in range(HEADS) for n in range(nb)], axis=0)
    return cols(HEADS), cols(2 * HEADS), gcr


def _gdn_fwd(q, k, v, scal, gct, nb=None):
    h, t, dh = q.shape
    nc = t // CHUNK
    nb = nb or _pick(nc, (4, 2))
    bsz = h * nb

    def body(q_ref, k_ref, v_ref, sc_ref, gt_ref, o_ref, s0_ref, state_ref):
        @pl.when(pl.program_id(0) == 0)
        def _():
            state_ref[...] = jnp.zeros_like(state_ref)

        ld = lambda r: r[...].reshape(bsz, CHUNK, dh)
        bc, gcc, gcr = _gate_tiles(sc_ref[...], gt_ref[...], nb)
        z = _gdn_intra(ld(q_ref), ld(k_ref), ld(v_ref), bc, gcc, gcr)
        per = lambda x: x.reshape((h, nb) + x.shape[1:])
        u, wk, p, qg, kt, gaml = (per(z[n]) for n in ("u", "wk", "p", "qg", "kt", "gaml"))
        s = state_ref[...]
        for n in range(nb):
            s0_ref[:, n] = s
            sb = _bf(s)
            vn = u[:, n] - _bmm(_bf(wk[:, n]), sb, 2, 1)
            o_ref[:, n * CHUNK:(n + 1) * CHUNK, :] = _bmm(_bf(qg[:, n]), sb, 2, 1) + _bmm(_bf(p[:, n]), _bf(vn), 2, 1)
            s = s * gaml[:, n] + _bmm(_bf(kt[:, n]), _bf(vn), 1, 1)
        state_ref[...] = s

    blk = pl.BlockSpec((h, nb * CHUNK, dh), lambda i: (0, i, 0))
    return pl.pallas_call(
        body, grid=(nc // nb,), name="gdn_fwd",
        in_specs=[blk] * 3 + [pl.BlockSpec((nb * CHUNK, LANES), lambda i: (i, 0)),
                              pl.BlockSpec((h, nb * CHUNK), lambda i: (0, i))],
        out_specs=[blk, pl.BlockSpec((h, nb, dh, dh), lambda i: (0, i, 0, 0))],
        out_shape=[jax.ShapeDtypeStruct((h, t, dh), F32), jax.ShapeDtypeStruct((h, nc, dh, dh), F32)],
        scratch_shapes=[pltpu.VMEM((h, dh, dh), F32)],
        compiler_params=_params("arbitrary"),
    )(q, k, v, scal, gct)


def _gdn_bwd(q, k, v, scal, gct, s0s, do, nb=None):
    h, t, dh = q.shape
    nc = t // CHUNK
    nb = nb or _pick(nc, (2,))
    bsz = h * nb
    ng = nc // nb
    rows = nb * CHUNK

    def body(q_ref, k_ref, v_ref, sc_ref, gt_ref, s0_ref, do_ref,
             dq_ref, dk_ref, dv_ref, dsc_ref, dgt_ref, ds_ref):
        @pl.when(pl.program_id(0) == 0)
        def _():
            ds_ref[...] = jnp.zeros_like(ds_ref)

        ld = lambda r: r[...].reshape(bsz, CHUNK, dh)
        q, k, v = ld(q_ref), ld(k_ref), ld(v_ref)
        bc, gcc, gcr = _gate_tiles(sc_ref[...], gt_ref[...], nb)
        z = _gdn_intra(q, k, v, bc, gcc, gcr)
        per = lambda x: x.reshape((h, nb) + x.shape[1:])
        u, wk, p, qg, kt, gaml = (per(z[n]) for n in ("u", "wk", "p", "qg", "kt", "gaml"))
        dout = per(ld(do_ref))
        ds = ds_ref[...]
        d_u, d_wk, d_p, d_qg, d_kt, d_gaml = ([None] * nb for _ in range(6))
        for n in reversed(range(nb)):
            s0 = s0_ref[:, n]
            s0b, dsb, dob = _bf(s0), _bf(ds), _bf(dout[:, n])
            wkb, qgb = _bf(wk[:, n]), _bf(qg[:, n])
            vn = u[:, n] - _bmm(wkb, s0b, 2, 1)
            dvn = _bmm(_bf(p[:, n]), dob, 1, 1) + _bmm(_bf(kt[:, n]), dsb, 2, 1)
            dvnb = _bf(dvn)
            d_u[n] = dvn
            d_p[n] = _bmm(dob, _bf(vn), 2, 2)
            d_qg[n] = _bmm(dob, s0b, 2, 2)
            d_kt[n] = _bmm(_bf(vn), dsb, 2, 2)
            d_gaml[n] = jnp.sum(s0 * ds, axis=1, keepdims=True)
            d_wk[n] = -_bmm(dvnb, s0b, 2, 2)
            ds = _bmm(qgb, dob, 1, 1) + gaml[:, n] * ds - _bmm(wkb, dvnb, 1, 1)
        ds_ref[...] = ds

        flat = lambda xs: jnp.stack(xs, axis=1).reshape((bsz,) + xs[0].shape[1:])
        d_u, d_wk, d_p, d_qg, d_kt, d_gaml = (flat(x) for x in (d_u, d_wk, d_p, d_qg, d_kt, d_gaml))
        tinv, gam, kb, dm = z["tinv"], z["gam"], z["kb"], z["dm"]
        drv = _bmm(tinv, d_u, 1, 1, precision=HI)
        drk = _bmm(tinv, d_wk, 1, 1, precision=HI)
        da = -(_bmm(_bf(drv), _bf(z["u"]), 2, 2) + _bmm(_bf(drk), _bf(z["wk"]), 2, 2))
        da = jnp.where(z["strict"], da, 0.0)
        d_p = jnp.where(z["tril"], d_p, 0.0)
        dkk = _bf(da * dm)
        dqk = _bf(d_p * dm)
        dkb = _bmm(dkk, _bf(k), 2, 1) + drk * gam
        dk = _bmm(dkk, _bf(kb), 1, 1) + _bmm(dqk, _bf(q), 1, 1) + dkb * bc + d_kt * z["edec"]
        dq = _bmm(dqk, _bf(k), 2, 1) + d_qg * gam
        mm = da * z["a"] + d_p * z["p"]
        dkt_kt = d_kt * z["kt"]
        dgl = jnp.sum(dkt_kt, axis=1, keepdims=True) + d_gaml * z["gaml"]
        dgc = mm + d_qg * z["qg"] + drk * kb * gam - dkt_kt + jnp.where(z["last"], dgl, 0.0)
        dq_ref[...] = dq.reshape(h, rows, dh)
        dk_ref[...] = dk.reshape(h, rows, dh)
        dv_ref[...] = (drv * bc).reshape(h, rows, dh)
        dbeta = (dkb * k + drv * v).reshape(h, rows, dh)
        dgc = dgc.reshape(h, rows, dh)
        lane = lax.broadcasted_iota(jnp.int32, (rows, LANES), 1)
        dsc = jnp.zeros((rows, LANES), F32)
        for hh in range(h):
            dsc = jnp.where(lane == HEADS + hh, jnp.sum(dbeta[hh], axis=1, keepdims=True), dsc)
            dsc = jnp.where(lane == 2 * HEADS + hh, jnp.sum(dgc[hh], axis=1, keepdims=True), dsc)
        dsc_ref[...] = dsc
        dgr = -jnp.sum(mm, axis=1, keepdims=True)
        for hh in range(h):
            for n in range(nb):
                dgt_ref[hh:hh + 1, n * CHUNK:(n + 1) * CHUNK] = dgr[hh * nb + n]

    blk = pl.BlockSpec((h, rows, dh), lambda i: (0, ng - 1 - i, 0))
    shp = jax.ShapeDtypeStruct((h, t, dh), F32)
    sc_spec = pl.BlockSpec((rows, LANES), lambda i: (ng - 1 - i, 0))
    gt_spec = pl.BlockSpec((h, rows), lambda i: (0, ng - 1 - i))
    return pl.pallas_call(
        body, grid=(ng,), name="gdn_bwd",
        in_specs=[blk] * 3 + [sc_spec, gt_spec, pl.BlockSpec((h, nb, dh, dh), lambda i: (0, ng - 1 - i, 0, 0)), blk],
        out_specs=[blk] * 3 + [sc_spec, gt_spec],
        out_shape=[shp] * 3 + [jax.ShapeDtypeStruct((t, LANES), F32), jax.ShapeDtypeStruct((h, t), F32)],
        scratch_shapes=[pltpu.VMEM((h, dh, dh), F32)],
        compiler_params=_params("arbitrary"),
    )(q, k, v, scal, gct, s0s, do)


def _gdn_post_fwd(o, xg, gain, seg):
    t = o.shape[1]
    tt = _pick(t, (320, 256, 128))

    def body(o_ref, z_ref, g_ref, e_ref, y_ref):
        x = _load_heads(o_ref)
        r = lax.rsqrt(_segsum(x * x, e_ref[...]) * (1.0 / HEAD_DIM) + RMS_EPS)
        y_ref[...] = (x * r * g_ref[...] * _silu(z_ref[...])).astype(y_ref.dtype)

    return pl.pallas_call(
        body, grid=(t // tt,), name="gdn_post_fwd",
        in_specs=[pl.BlockSpec((HEADS, tt, HEAD_DIM), lambda i: (0, i, 0)), pl.BlockSpec((tt, WIDTH), lambda i: (i, 3)),
                  pl.BlockSpec((1, WIDTH), lambda i: (0, 0)), pl.BlockSpec((WIDTH, WIDTH), lambda i: (0, 0))],
        out_specs=pl.BlockSpec((tt, WIDTH), lambda i: (i, 0)),
        out_shape=jax.ShapeDtypeStruct((t, WIDTH), BF16),
        compiler_params=_params("arbitrary"),
    )(o, xg, gain, seg)


def _gdn_post_bwd(o, xg, gain, seg, dy):
    t = o.shape[1]
    tt = _pick(t, (320, 256, 128))

    def body(o_ref, z_ref, g_ref, e_ref, dy_ref, do_ref, dz_ref, dg_ref):
        x = _load_heads(o_ref)
        zz = z_ref[...]
        e = e_ref[...]
        gain_v = g_ref[...]
        d = dy_ref[...]
        r = lax.rsqrt(_segsum(x * x, e) * (1.0 / HEAD_DIM) + RMS_EPS)
        xr = x * r
        don = d * _silu(zz)
        dz_ref[...] = (d * xr * gain_v * _silu_grad(zz)).astype(dz_ref.dtype)
        gy = don * gain_v
        _store_heads(do_ref, r * gy - xr * (r * r) * (_segsum(gy * x, e) * (1.0 / HEAD_DIM)))

        @pl.when(pl.program_id(0) == 0)
        def _():
            dg_ref[...] = jnp.zeros_like(dg_ref)

        dg_ref[...] += jnp.sum(don * xr, axis=0, keepdims=True)

    row = pl.BlockSpec((tt, WIDTH), lambda i: (i, 0))
    vec = pl.BlockSpec((1, WIDTH), lambda i: (0, 0))
    hm = pl.BlockSpec((HEADS, tt, HEAD_DIM), lambda i: (0, i, 0))
    return pl.pallas_call(
        body, grid=(t // tt,), name="gdn_post_bwd",
        in_specs=[hm, pl.BlockSpec((tt, WIDTH), lambda i: (i, 3)), vec,
                  pl.BlockSpec((WIDTH, WIDTH), lambda i: (0, 0)), row],
        out_specs=[hm, row, vec],
        out_shape=[jax.ShapeDtypeStruct((HEADS, t, HEAD_DIM), F32), jax.ShapeDtypeStruct((t, WIDTH), BF16),
                   jax.ShapeDtypeStruct((1, WIDTH), F32)],
        compiler_params=_params("arbitrary"),
    )(o, xg, gain, seg, dy)


def _mix_fwd(yf, yg, gates, bias):
    t, d = yf.shape
    tt = _pick(t, (320, 256, 128))

    def body(yf_ref, yg_ref, g1_ref, g2_ref, b1_ref, b2_ref, o_ref):
        g1 = jax.nn.sigmoid(g1_ref[...] + b1_ref[...])
        g2 = jax.nn.sigmoid(g2_ref[...] + b2_ref[...])
        o_ref[...] = (g1 * yf_ref[...] + g2 * yg_ref[...]).astype(o_ref.dtype)

    row = pl.BlockSpec((tt, d), lambda i: (i, 0))
    return pl.pallas_call(
        body, grid=(t // tt,), name="mix_fwd",
        in_specs=[row, row, row, pl.BlockSpec((tt, d), lambda i: (i, 1)),
                  pl.BlockSpec((1, d), lambda i: (0, 0)), pl.BlockSpec((1, d), lambda i: (0, 1))],
        out_specs=row, out_shape=jax.ShapeDtypeStruct((t, d), BF16),
        compiler_params=_params("arbitrary"),
    )(yf, yg, gates, gates, bias, bias)


def _mix_bwd(dmix, yf, yg, gates, bias):
    t, d = yf.shape
    tt = _pick(t, (320, 256, 128))

    def body(dm_ref, yf_ref, yg_ref, g1_ref, g2_ref, b1_ref, b2_ref, dyf_ref, dyg_ref, dg_ref, db_ref):
        dm = dm_ref[...]
        g1 = jax.nn.sigmoid(g1_ref[...] + b1_ref[...])
        g2 = jax.nn.sigmoid(g2_ref[...] + b2_ref[...])
        dyf_ref[...] = (dm * g1).astype(BF16)
        dyg_ref[...] = (dm * g2).astype(BF16)
        dgate = jnp.concatenate([dm * yf_ref[...] * g1 * (1.0 - g1), dm * yg_ref[...] * g2 * (1.0 - g2)], axis=1)
        dg_ref[...] = dgate.astype(BF16)

        @pl.when(pl.program_id(0) == 0)
        def _():
            db_ref[...] = jnp.zeros_like(db_ref)

        db_ref[...] += jnp.sum(dgate, axis=0, keepdims=True)

    row = pl.BlockSpec((tt, d), lambda i: (i, 0))
    wide = pl.BlockSpec((tt, 2 * d), lambda i: (i, 0))
    return pl.pallas_call(
        body, grid=(t // tt,), name="mix_bwd",
        in_specs=[row, row, row, row, pl.BlockSpec((tt, d), lambda i: (i, 1)),
                  pl.BlockSpec((1, d), lambda i: (0, 0)), pl.BlockSpec((1, d), lambda i: (0, 1))],
        out_specs=[row, row, wide, pl.BlockSpec((1, 2 * d), lambda i: (0, 0))],
        out_shape=[jax.ShapeDtypeStruct((t, d), BF16), jax.ShapeDtypeStruct((t, d), BF16),
                   jax.ShapeDtypeStruct((t, 2 * d), BF16), jax.ShapeDtypeStruct((1, 2 * d), F32)],
        compiler_params=_params("arbitrary"),
    )(dmix, yf, yg, gates, gates, bias, bias)


def _ffn_act_fwd(up, conv_w, conv_b):
    t, c = up.shape
    tt = 128

    def body(x_ref, p_ref, w_ref, b_ref, o_ref):
        prev = jnp.where(pl.program_id(0) == 0, 0.0, p_ref[...])
        u = _causal_conv(x_ref[...], prev, w_ref, FFN_CONV) + b_ref[...]
        o_ref[...] = (_silu(u[:, :D_FF]) * u[:, D_FF:]).astype(o_ref.dtype)

    return pl.pallas_call(
        body, grid=(t // tt,), name="ffn_act_fwd",
        in_specs=[pl.BlockSpec((tt, c), lambda i: (i, 0)), _prev_spec(tt, c),
                  pl.BlockSpec((FFN_CONV, c), lambda i: (0, 0)), pl.BlockSpec((1, c), lambda i: (0, 0))],
        out_specs=pl.BlockSpec((tt, D_FF), lambda i: (i, 0)),
        out_shape=jax.ShapeDtypeStruct((t, D_FF), BF16),
        compiler_params=_params("arbitrary"),
    )(up, up, conv_w, conv_b)


def _ffn_act_bwd(up, conv_w, conv_b, dact):
    t, c = up.shape
    tt = 128
    nt = t // tt

    def body(x_ref, p_ref, w_ref, b_ref, da_ref, dx_ref, dw_ref, db_ref, carry_ref):
        step = pl.program_id(0)
        x = x_ref[...]
        prev = jnp.where(step == nt - 1, 0.0, p_ref[...])
        u = _causal_conv(x, prev, w_ref, FFN_CONV) + b_ref[...]
        gate, val = u[:, :D_FF], u[:, D_FF:]
        da = da_ref[...]
        du = jnp.concatenate([da * val * _silu_grad(gate), da * _silu(gate)], axis=1)

        @pl.when(step == 0)
        def _():
            carry_ref[...] = jnp.zeros_like(carry_ref)
            dw_ref[...] = jnp.zeros_like(dw_ref)
            db_ref[...] = jnp.zeros_like(db_ref)

        dx = _causal_conv_bwd(x, prev, du, carry_ref[...], w_ref, dw_ref, FFN_CONV)
        dx_ref[...] = dx.astype(dx_ref.dtype)
        db_ref[...] += jnp.sum(du, axis=0, keepdims=True)
        carry_ref[...] = du[:8]

    rev = lambda i: (nt - 1 - i, 0)
    return pl.pallas_call(
        body, grid=(nt,), name="ffn_act_bwd",
        in_specs=[pl.BlockSpec((tt, c), rev),
                  pl.BlockSpec((8, c), lambda i: (jnp.maximum((nt - 1 - i) * (tt // 8) - 1, 0), 0)),
                  pl.BlockSpec((FFN_CONV, c), lambda i: (0, 0)), pl.BlockSpec((1, c), lambda i: (0, 0)),
                  pl.BlockSpec((tt, D_FF), rev)],
        out_specs=[pl.BlockSpec((tt, c), rev), pl.BlockSpec((FFN_CONV, c), lambda i: (0, 0)),
                   pl.BlockSpec((1, c), lambda i: (0, 0))],
        out_shape=[jax.ShapeDtypeStruct((t, c), BF16), jax.ShapeDtypeStruct((FFN_CONV, c), F32),
                   jax.ShapeDtypeStruct((1, c), F32)],
        scratch_shapes=[pltpu.VMEM((8, c), F32)],
        compiler_params=_params("arbitrary"),
    )(up, up, conv_w, conv_b, dact)


def _final_loss(h2, target, gain, seq):
    t, d = h2.shape
    tr = _pick(t, (320, 256, 128))

    def body(h_ref, t_ref, g_ref, loss_ref, dh_ref, dhb_ref, dg_ref):
        i = pl.program_id(0)
        x = h_ref[...]
        gain_v = g_ref[...]
        r = lax.rsqrt(jnp.mean(x * x, axis=-1, keepdims=True) + RMS_EPS)
        xr = x * r
        rows = i * tr + lax.broadcasted_iota(jnp.int32, (tr, 1), 0)
        real = (rows >= N_META) & (rows < N_META + seq)
        err = jnp.where(real, xr * gain_v - t_ref[...], 0.0)
        dy = err * (1.0 / d)
        gy = dy * gain_v
        dh = r * (gy - xr * jnp.mean(gy * xr, axis=-1, keepdims=True))
        dh_ref[...] = dh
        dhb_ref[...] = dh.astype(BF16)

        @pl.when(i == 0)
        def _():
            loss_ref[...] = jnp.zeros_like(loss_ref)
            dg_ref[...] = jnp.zeros_like(dg_ref)

        part = jnp.sum(jnp.sum(err * err, axis=-1, keepdims=True), axis=0, keepdims=True)
        loss_ref[...] += jnp.broadcast_to(part * (0.5 / d), loss_ref.shape)
        dg_ref[...] += jnp.sum(dy * xr, axis=0, keepdims=True)

    row = pl.BlockSpec((tr, d), lambda i: (i, 0))
    vec = pl.BlockSpec((1, d), lambda i: (0, 0))
    return pl.pallas_call(
        body, grid=(t // tr,), name="final_loss",
        in_specs=[row, row, vec],
        out_specs=[pl.BlockSpec((1, LANES), lambda i: (0, 0)), row, row, vec],
        out_shape=[jax.ShapeDtypeStruct((1, LANES), F32), jax.ShapeDtypeStruct((t, d), F32),
                   jax.ShapeDtypeStruct((t, d), BF16), jax.ShapeDtypeStruct((1, d), F32)],
        compiler_params=_params("arbitrary"),
    )(h2, target, gain)


ADAM_TILE_BYTES = 1 << 20


def _adamw(w, m, v, grecv, name):
    r, cols = w.shape
    tr = r
    if r * cols * 4 > ADAM_TILE_BYTES:
        tr = max(d for d in range(8, r + 1, 8) if r % d == 0 and d * cols * 4 <= ADAM_TILE_BYTES)

    def body(w_ref, m_ref, v_ref, g_ref, go_ref, d_ref, mo_ref, vo_ref):
        g = g_ref[0].astype(F32)
        for s in range(1, N_DEV):
            g = g + g_ref[s].astype(F32)
        wv = w_ref[...]
        mn = ADAM_B1 * m_ref[...] + (1.0 - ADAM_B1) * g
        vn = ADAM_B2 * v_ref[...] + (1.0 - ADAM_B2) * (g * g)
        m_hat = mn / (1.0 - ADAM_B1 ** ADAM_STEP)
        v_hat = vn / (1.0 - ADAM_B2 ** ADAM_STEP)
        go_ref[...] = g
        d_ref[...] = -ADAM_LR * (m_hat / (jnp.sqrt(v_hat) + ADAM_EPS) + ADAM_WD * wv)
        mo_ref[...] = mn
        vo_ref[...] = vn

    row = pl.BlockSpec((tr, cols), lambda i: (i, 0))
    shp = jax.ShapeDtypeStruct((r, cols), F32)
    return pl.pallas_call(
        body, grid=(r // tr,), name=name,
        in_specs=[row, row, row, pl.BlockSpec((N_DEV, tr, cols), lambda i: (0, i, 0))],
        out_specs=[row] * 4, out_shape=[shp] * 4,
        compiler_params=_params("parallel"),
    )(w, m, v, grecv)


def _mesh_pos():
    return lax.axis_index("x"), lax.axis_index("y"), lax.axis_index("c")


def _all_gather(shards):
    n = len(shards)

    def body(*refs):
        x_refs, out_refs = refs[:n], refs[n:2 * n]
        send_sems, recv_sems, local_sems = refs[2 * n:]
        x, y, c = _mesh_pos()
        me, sibling = (x, y, c), (x, y, 1 - c)
        chips = [(1 - x, y), (x, 1 - y), (1 - x, 1 - y)]

        def slot(a, px, py, pc):
            return out_refs[a].at[4 * px + 2 * py + pc]

        def copy(a, kk, block, to, src=None):
            return pltpu.make_async_remote_copy(
                src_ref=slot(a, *block) if src is None else src, dst_ref=slot(a, *block),
                send_sem=send_sems.at[7 * a + kk], recv_sem=recv_sems.at[7 * a + kk],
                device_id=to, device_id_type=MESH_ID)

        mine = [pltpu.make_async_copy(x_refs[a], slot(a, *me), local_sems.at[a]) for a in range(n)]
        first = []
        for a in range(n):
            first.append(copy(a, 0, me, sibling, src=x_refs[a]))
            first += [copy(a, 1 + j, me, (*chip, c), src=x_refs[a]) for j, chip in enumerate(chips)]
        for cp in mine + first:
            cp.start()
        passed = []
        for j, chip in enumerate(chips):
            for a in range(n):
                copy(a, 1 + j, (*chip, c), me).wait_recv()
                passed.append(copy(a, 4 + j, (*chip, c), sibling))
                passed[-1].start()
        for a in range(n):
            copy(a, 0, sibling, me).wait_recv()
        for j, chip in enumerate(chips):
            for a in range(n):
                copy(a, 4 + j, (*chip, 1 - c), me).wait_recv()
        for cp in first + passed:
            cp.wait_send()
        for cp in mine:
            cp.wait()

    hbm = pl.BlockSpec(memory_space=pl.ANY)
    return pl.pallas_call(
        body, name="weight_all_gather", in_specs=[hbm] * n, out_specs=[hbm] * n,
        out_shape=[jax.ShapeDtypeStruct((N_DEV,) + s.shape, s.dtype) for s in shards],
        scratch_shapes=[pltpu.SemaphoreType.DMA((7 * n,)), pltpu.SemaphoreType.DMA((7 * n,)),
                        pltpu.SemaphoreType.DMA((n,))],
    )(*shards)


def _grad_exchange(blocks, small):
    n = len(blocks)

    def body(*refs):
        src_refs, dst_refs = refs[:n + 1], refs[n + 1:2 * n + 2]
        send_sems, recv_sems, local_sems = refs[2 * n + 2:]
        x, y, c = _mesh_pos()
        me = 4 * x + 2 * y + c
        copies = []
        for kk in range(1, N_DEV):
            px = 1 - x if kk & 4 else x
            py = 1 - y if kk & 2 else y
            pc = 1 - c if kk & 1 else c
            peer = 4 * px + 2 * py + pc
            for a in range(n + 1):
                copies.append(pltpu.make_async_remote_copy(
                    src_ref=src_refs[a].at[peer] if a < n else src_refs[a], dst_ref=dst_refs[a].at[me],
                    send_sem=send_sems.at[7 * a + kk - 1], recv_sem=recv_sems.at[7 * a + kk - 1],
                    device_id=(px, py, pc), device_id_type=MESH_ID))
        own = [pltpu.make_async_copy(src_refs[a].at[me] if a < n else src_refs[a], dst_refs[a].at[me],
                                     local_sems.at[a]) for a in range(n + 1)]
        for cp in own + copies:
            cp.start()
        for cp in copies + own:
            cp.wait()

    hbm = pl.BlockSpec(memory_space=pl.ANY)
    return pl.pallas_call(
        body, name="grad_exchange", in_specs=[hbm] * (n + 1), out_specs=[hbm] * (n + 1),
        out_shape=[jax.ShapeDtypeStruct(b.shape, b.dtype) for b in blocks]
        + [jax.ShapeDtypeStruct((N_DEV,) + small.shape, small.dtype)],
        scratch_shapes=[pltpu.SemaphoreType.DMA((7 * (n + 1),)), pltpu.SemaphoreType.DMA((7 * (n + 1),)),
                        pltpu.SemaphoreType.DMA((n + 1,))],
    )(*blocks, small)


def _pad_flat(parts, rows):
    flat = jnp.concatenate([p.reshape(-1) for p in parts])
    return jnp.pad(flat, (0, rows * LANES - flat.shape[0])).reshape(rows, LANES)


def _rows_for(n_elems, mult=1024):
    rows = -(-n_elems // LANES)
    return -(-rows // mult) * mult


SHARDED = ("meta_tokens", "w_in", "gdn_conv_w", "w_branch_fox", "w_branch_gdn", "w_out", "ffn_w_up", "ffn_conv_w",
           "ffn_w_down")
MATMUL = ("w_in", "w_branch_fox", "w_branch_gdn", "w_out", "ffn_w_up", "ffn_w_down")
EXACT = ("meta_tokens", "gdn_conv_w", "ffn_conv_w")
REPLICATED = ("fgt_bias", "gdn_a_log", "gdn_dt_bias", "gdn_norm_w", "gate_bias", "norm_mix_w", "norm_ffn_w",
              "ffn_conv_b", "norm_final_w")
WEIGHTS = ("meta_tokens", "w_in", "fgt_bias", "gdn_conv_w", "gdn_a_log", "gdn_dt_bias", "gdn_norm_w", "gate_bias",
           "w_branch_fox", "w_branch_gdn", "w_out", "norm_mix_w", "norm_ffn_w", "ffn_w_up", "ffn_conv_w",
           "ffn_conv_b", "ffn_w_down", "norm_final_w")


def _unpack(buf, shapes):
    flat = buf.reshape(-1)
    out, off = [], 0
    for s in shapes:
        n = int(np.prod(s))
        out.append(flat[off:off + n].reshape(s))
        off += n
    return out


def _unpack_gathered(buf, shapes):
    flat = buf.reshape(N_DEV, -1)
    out, off = [], 0
    for s in shapes:
        n = int(np.prod(s))
        out.append(flat[:, off:off + n].reshape((N_DEV,) + tuple(s)))
        off += n
    return out


def _cat_cols(g):
    return g.transpose(1, 0, 2).reshape(g.shape[1], -1)


def _col_blocks(full, width):
    return full.reshape(full.shape[0], N_DEV, width).transpose(1, 0, 2)


def _local_step(x, target, w):
    seq = x.shape[0]
    t = _padded_tokens(seq)
    pad = t - N_META - seq
    seg = _seg_matrix()
    zrows = jnp.zeros((pad, D_MODEL), F32)
    h0 = jnp.concatenate([w["meta_tokens"], x, zrows], axis=0)
    tgt = jnp.concatenate([jnp.zeros((N_META, D_MODEL), F32), target, zrows], axis=0)

    w_in = w["w_in"]
    o_f, o_g, o_z, o_b, o_a, o_gate = 1536, 1544, 3080, 3592, 3600, 3608
    w_small = jnp.concatenate([w_in[:, o_f:o_f + 8], w_in[:, o_b:o_b + 8], w_in[:, o_a:o_a + 8],
                               jnp.zeros((D_MODEL, LANES - 24), BF16)], axis=1)
    w_r = jnp.concatenate([w_in[:, :1536], w_in[:, o_g:o_z], w_in[:, o_z:o_b], w_in[:, o_gate:], w_small], axis=1)

    a1 = _rmsnorm_fwd(h0, w["norm_mix_w"])
    fq = _mm(a1, w_r[:, :1536], BF16, "proj_fox")
    xg = _mm(a1, w_r[:, 1536:3584], F32, "proj_gdn")
    gt = _mm(a1, w_r[:, 3584:5632], F32, "proj_gates")
    sm = _mm(a1, w_r[:, 5632:], F32, "proj_small")

    lanes_pad = lambda a, lo: jnp.pad(a, ((0, 0), (lo, LANES - lo - a.shape[1])))
    neg_exp_a = -jnp.exp(w["gdn_a_log"])
    pbias = lanes_pad(w["fgt_bias"], 0) + lanes_pad(w["gdn_dt_bias"], 2 * HEADS)
    pscale = lanes_pad(neg_exp_a, 2 * HEADS)
    scal = _gate_fwd(sm, pbias, pscale)
    gct = scal[:, 2 * HEADS:3 * HEADS].T

    qa, ka, va = _fox_prep(fq, scal)
    oa, qb = _fox_fwd(qa, ka, va)
    o_fox = _fox_post(oa)

    qh, kh, vh = _gdn_pre_fwd(xg, w["gdn_conv_w"], seg)
    og, s0s = _gdn_fwd(qh, kh, vh, scal, gct)
    norm_w = jnp.tile(w["gdn_norm_w"], (1, HEADS))
    ogn = _gdn_post_fwd(og, xg, norm_w, seg)

    yf = _mm(o_fox, w["w_branch_fox"], F32, "branch_fox")
    yg = _mm(ogn, w["w_branch_gdn"], F32, "branch_gdn")
    mix = _mix_fwd(yf, yg, gt, w["gate_bias"])
    h1 = _mm(mix, w["w_out"], F32, "out_proj", res=h0)
    a2 = _rmsnorm_fwd(h1, w["norm_ffn_w"])
    up = _mm(a2, w["ffn_w_up"], F32, "ffn_up")
    act = _ffn_act_fwd(up, w["ffn_conv_w"], w["ffn_conv_b"])
    h2 = _mm(act, w["ffn_w_down"], F32, "ffn_down", res=h1)
    loss, dh2, dh2b, g_final = _final_loss(h2, tgt, w["norm_final_w"].reshape(1, D_MODEL), seq)

    grads = {"norm_final_w": g_final.reshape(D_MODEL)}
    grads["ffn_w_down"] = _mm_tn(act, dh2b, "wgrad_ffn_down")
    dact = _mm(dh2b, w["ffn_w_down"].T, F32, "dgrad_ffn_down")
    dup, g_cw, g_cb = _ffn_act_bwd(up, w["ffn_conv_w"], w["ffn_conv_b"], dact)
    grads["ffn_conv_w"], grads["ffn_conv_b"] = g_cw, g_cb
    grads["ffn_w_up"] = _mm_tn(a2, dup, "wgrad_ffn_up")
    da2 = _mm(dup, w["ffn_w_up"].T, F32, "dgrad_ffn_up")
    dh1, dh1b, grads["norm_ffn_w"] = _rmsnorm_bwd(h1, da2, w["norm_ffn_w"], dh2)
    grads["w_out"] = _mm_tn(mix, dh1b, "wgrad_out")
    dmix = _mm(dh1b, w["w_out"].T, F32, "dgrad_out")
    dyf, dyg, dgt, grads["gate_bias"] = _mix_bwd(dmix, yf, yg, gt, w["gate_bias"])
    grads["w_branch_fox"] = _mm_tn(o_fox, dyf, "wgrad_branch_fox")
    grads["w_branch_gdn"] = _mm_tn(ogn, dyg, "wgrad_branch_gdn")
    do_fox = _mm(dyf, w["w_branch_fox"].T, F32, "dgrad_branch_fox")
    dogn = _mm(dyg, w["w_branch_gdn"].T, F32, "dgrad_branch_gdn")

    dog, dz, g_nw = _gdn_post_bwd(og, xg, norm_w, seg, dogn)
    grads["gdn_norm_w"] = g_nw.reshape(HEADS, HEAD_DIM).sum(axis=0)[None]
    dqh, dkh, dvh, dscal_g, dgct = _gdn_bwd(qh, kh, vh, scal, gct, s0s, dog)
    dxg, grads["gdn_conv_w"] = _gdn_pre_bwd(xg, w["gdn_conv_w"], seg, dqh, dkh, dvh)

    dqa, dka, dva = _fox_bwd(qb, ka, va, _fox_bwd_prep(do_fox, oa))
    dfq, dscal_c = _fox_bwd_post(dqa, dka, dva)

    dscal = dscal_c + dscal_g + lanes_pad(dgct.T, 2 * HEADS)
    dsm, dpb, dps = _gate_bwd(sm, pbias, pscale, dscal)
    grads["fgt_bias"] = dpb[:, :HEADS]
    grads["gdn_dt_bias"] = dpb[:, 2 * HEADS:3 * HEADS]
    grads["gdn_a_log"] = dps[:, 2 * HEADS:3 * HEADS] * neg_exp_a

    dproj = jnp.concatenate([dfq, dxg, dz, dgt, dsm], axis=1)
    g_r = _mm_tn(a1, dproj, "wgrad_in")
    grads["w_in"] = jnp.concatenate([g_r[:, :1536], g_r[:, 5632:5640], g_r[:, 1536:3072], g_r[:, 3072:3584],
                                     g_r[:, 5640:5648], g_r[:, 5648:5656], g_r[:, 3584:5632]], axis=1)
    da1 = _mm(dproj, w_r.T, F32, "dgrad_in")
    dh0, _, grads["norm_mix_w"] = _rmsnorm_bwd(h0, da1, w["norm_mix_w"], dh1)
    grads["meta_tokens"] = dh0[:N_META]
    return loss, dh0[N_META:N_META + seq], grads


def _shard_pieces(arrs):
    return [arrs[n][0] if arrs[n].ndim == 3 else arrs[n] for n in SHARDED]


def _full_grad_blocks(grads):
    g = grads
    cols = lambda a, wd: _col_blocks(a, wd)
    rows = lambda a: a.reshape(N_DEV, a.shape[0] // N_DEV, a.shape[1])
    per = [cols(g["meta_tokens"], 128), cols(g["w_in"], IN_WIDTH // N_DEV), cols(g["gdn_conv_w"], 3 * WIDTH // N_DEV),
           cols(g["w_branch_fox"], D_MODEL // N_DEV), cols(g["w_branch_gdn"], D_MODEL // N_DEV), rows(g["w_out"]),
           cols(g["ffn_w_up"], 2 * D_FF // N_DEV), cols(g["ffn_conv_w"], 2 * D_FF // N_DEV), rows(g["ffn_w_down"])]
    return per


def kernel(x, meta_tokens, w_in, fgt_bias, gdn_conv_w, gdn_a_log, gdn_dt_bias, gdn_norm_w, gate_bias, w_branch_fox, w_branch_gdn, w_out, norm_mix_w, norm_ffn_w, ffn_w_up, ffn_conv_w, ffn_conv_b, ffn_w_down, norm_final_w, loss_target, m_meta_tokens, m_w_in, m_fgt_bias, m_gdn_conv_w, m_gdn_a_log, m_gdn_dt_bias, m_gdn_norm_w, m_gate_bias, m_w_branch_fox, m_w_branch_gdn, m_w_out, m_norm_mix_w, m_norm_ffn_w, m_ffn_w_up, m_ffn_conv_w, m_ffn_conv_b, m_ffn_w_down, m_norm_final_w, v_meta_tokens, v_w_in, v_fgt_bias, v_gdn_conv_w, v_gdn_a_log, v_gdn_dt_bias, v_gdn_norm_w, v_gate_bias, v_w_branch_fox, v_w_branch_gdn, v_w_out, v_norm_mix_w, v_norm_ffn_w, v_ffn_w_up, v_ffn_conv_w, v_ffn_conv_b, v_ffn_w_down, v_norm_final_w):
    wts = dict(meta_tokens=meta_tokens, w_in=w_in, fgt_bias=fgt_bias, gdn_conv_w=gdn_conv_w, gdn_a_log=gdn_a_log,
               gdn_dt_bias=gdn_dt_bias, gdn_norm_w=gdn_norm_w, gate_bias=gate_bias, w_branch_fox=w_branch_fox,
               w_branch_gdn=w_branch_gdn, w_out=w_out, norm_mix_w=norm_mix_w, norm_ffn_w=norm_ffn_w,
               ffn_w_up=ffn_w_up, ffn_conv_w=ffn_conv_w, ffn_conv_b=ffn_conv_b, ffn_w_down=ffn_w_down,
               norm_final_w=norm_final_w)
    mom = dict(meta_tokens=m_meta_tokens, w_in=m_w_in, fgt_bias=m_fgt_bias, gdn_conv_w=m_gdn_conv_w,
               gdn_a_log=m_gdn_a_log, gdn_dt_bias=m_gdn_dt_bias, gdn_norm_w=m_gdn_norm_w, gate_bias=m_gate_bias,
               w_branch_fox=m_w_branch_fox, w_branch_gdn=m_w_branch_gdn, w_out=m_w_out, norm_mix_w=m_norm_mix_w,
               norm_ffn_w=m_norm_ffn_w, ffn_w_up=m_ffn_w_up, ffn_conv_w=m_ffn_conv_w, ffn_conv_b=m_ffn_conv_b,
               ffn_w_down=m_ffn_w_down, norm_final_w=m_norm_final_w)
    var = dict(meta_tokens=v_meta_tokens, w_in=v_w_in, fgt_bias=v_fgt_bias, gdn_conv_w=v_gdn_conv_w,
               gdn_a_log=v_gdn_a_log, gdn_dt_bias=v_gdn_dt_bias, gdn_norm_w=v_gdn_norm_w, gate_bias=v_gate_bias,
               w_branch_fox=v_w_branch_fox, w_branch_gdn=v_w_branch_gdn, w_out=v_w_out, norm_mix_w=v_norm_mix_w,
               norm_ffn_w=v_norm_ffn_w, ffn_w_up=v_ffn_w_up, ffn_conv_w=v_ffn_conv_w, ffn_conv_b=v_ffn_conv_b,
               ffn_w_down=v_ffn_w_down, norm_final_w=v_norm_final_w)

    sh = dict(zip(SHARDED, _shard_pieces(wts)))
    exact_shapes = [sh[n].shape for n in EXACT]
    rows_exact = _rows_for(sum(int(np.prod(s)) for s in exact_shapes), 8)
    gathered = _all_gather([sh[n].astype(BF16) for n in MATMUL] + [_pad_flat([sh[n] for n in EXACT], rows_exact)])
    g_in, g_bf, g_bg, g_out, g_up, g_down = gathered[:6]
    meta_full, conv_full, fconv_full = (_cat_cols(a) for a in _unpack_gathered(gathered[6], exact_shapes))
    full = dict(
        meta_tokens=meta_full, w_in=_cat_cols(g_in), gdn_conv_w=conv_full,
        w_branch_fox=_cat_cols(g_bf), w_branch_gdn=_cat_cols(g_bg), w_out=g_out.reshape(D_MODEL, D_MODEL),
        ffn_w_up=_cat_cols(g_up), ffn_conv_w=fconv_full, ffn_w_down=g_down.reshape(D_FF, D_MODEL),
        fgt_bias=fgt_bias, gdn_a_log=gdn_a_log, gdn_dt_bias=gdn_dt_bias, gdn_norm_w=gdn_norm_w, gate_bias=gate_bias,
        norm_mix_w=norm_mix_w, norm_ffn_w=norm_ffn_w, ffn_conv_b=ffn_conv_b, norm_final_w=norm_final_w)

    loss, grad_x, grads = _local_step(x[0], loss_target[0], full)

    blocks = [b.astype(BF16) for b in _full_grad_blocks(grads)]
    rep_parts = [grads[n] for n in REPLICATED] + [loss[:, :1]]
    rep_shapes = [wts[n].shape for n in REPLICATED]
    rows_small = _rows_for(sum(int(np.prod(p.shape)) for p in rep_parts), 8)
    received = _grad_exchange(blocks, _pad_flat(rep_parts, rows_small))

    result = {}
    kinds = ("grad", "delta", "new_m", "new_v")
    for n, recv in zip(SHARDED, received[:-1]):
        outs = _adamw(sh[n], _shard_pieces(mom)[SHARDED.index(n)], _shard_pieces(var)[SHARDED.index(n)], recv,
                      "adamw_" + n)
        for kind, a in zip(kinds, outs):
            result[kind, n] = a.reshape(wts[n].shape)
    rep_w = _pad_flat([wts[n] for n in REPLICATED] + [jnp.zeros((1, 1), F32)], rows_small)
    rep_m = _pad_flat([mom[n] for n in REPLICATED] + [jnp.zeros((1, 1), F32)], rows_small)
    rep_v = _pad_flat([var[n] for n in REPLICATED] + [jnp.ones((1, 1), F32)], rows_small)
    outs_r = _adamw(rep_w, rep_m, rep_v, received[-1], "adamw_replicated")
    for kind, br in zip(kinds, outs_r):
        for n, a in zip(REPLICATED, _unpack(br, rep_shapes)):
            result[kind, n] = a
    n_rep = sum(int(np.prod(s)) for s in rep_shapes)
    total_loss = outs_r[0].reshape(-1)[n_rep]
    out = [total_loss, grad_x[None]]
    for kind in ("grad", "delta", "new_m", "new_v"):
        out += [result[kind, n] for n in WEIGHTS]
    return tuple(out)
```

```python
import functools

import jax
import jax.numpy as jnp
import numpy as np
from jax import lax
from jax.experimental import pallas as pl
from jax.experimental.pallas import tpu as pltpu

F32 = jnp.float32
BF16 = jnp.bfloat16

D_MODEL = 1024
N_META = 16
HEADS = 8
HEAD_DIM = 64
WIDTH = HEADS * HEAD_DIM
CHUNK = 64
GDN_CONV = 4
D_FF = 2816
FFN_CONV = 3
IN_WIDTH = 5656
IN_PAD = 5760
RMS_EPS = 1e-6
NEG = -1e30
AUG = 128
N_DEV = 8
LANES = 128

ADAM_LR = 0.001
ADAM_B1 = 0.9
ADAM_B2 = 0.999
ADAM_EPS = 1e-08
ADAM_WD = 0.01
ADAM_STEP = 10

VMEM_LIMIT = 56 * 1024 * 1024
MM_VMEM_BUDGET = 36 * 1024 * 1024
FFN_LANES = 128
HI = lax.Precision.HIGH
MESH_ID = pl.DeviceIdType.MESH


def _pick(n, cands):
    for c in cands:
        if n % c == 0:
            return c
    raise ValueError(f"no tile for {n} in {cands}")


def _params(*sem):
    return pltpu.CompilerParams(dimension_semantics=sem if sem else None, vmem_limit_bytes=VMEM_LIMIT)


def _padded_tokens(seq):
    t = -(-(N_META + seq) // 128) * 128
    if t > 1280 and t % 640:
        t = -(-t // 640) * 640
    return t


ROW_TILES = (640, 512, 384, 256, 128)


def _rmsnorm_fwd(h, gain):
    t, d = h.shape
    tr = _pick(t, ROW_TILES)

    def body(h_ref, g_ref, o_ref):
        x = h_ref[...]
        r = lax.rsqrt(jnp.mean(x * x, axis=-1, keepdims=True) + RMS_EPS)
        o_ref[...] = (x * r * g_ref[...]).astype(o_ref.dtype)

    return pl.pallas_call(
        body, grid=(t // tr,), name="rmsnorm_fwd",
        in_specs=[pl.BlockSpec((tr, d), lambda i: (i, 0)), pl.BlockSpec((1, d), lambda i: (0, 0))],
        out_specs=pl.BlockSpec((tr, d), lambda i: (i, 0)),
        out_shape=jax.ShapeDtypeStruct((t, d), BF16),
        compiler_params=_params("arbitrary"),
    )(h, gain)


def _rmsnorm_bwd(h, dy, gain, dres):
    t, d = h.shape
    tr = _pick(t, (320, 256, 128))

    def body(h_ref, dy_ref, g_ref, dres_ref, dh_ref, dhb_ref, dg_ref):
        x = h_ref[...]
        dyv = dy_ref[...]
        r = lax.rsqrt(jnp.mean(x * x, axis=-1, keepdims=True) + RMS_EPS)
        gy = dyv * g_ref[...]
        m = jnp.mean(gy * x, axis=-1, keepdims=True)
        dh = dres_ref[...] + r * gy - x * (r * r * r * m)
        dh_ref[...] = dh
        dhb_ref[...] = dh.astype(BF16)

        @pl.when(pl.program_id(0) == 0)
        def _():
            dg_ref[...] = jnp.zeros_like(dg_ref)

        dg_ref[...] += jnp.sum(dyv * x * r, axis=0, keepdims=True)

    row = pl.BlockSpec((tr, d), lambda i: (i, 0))
    vec = pl.BlockSpec((1, d), lambda i: (0, 0))
    return pl.pallas_call(
        body, grid=(t // tr,), name="rmsnorm_bwd",
        in_specs=[row, row, vec, row], out_specs=[row, row, vec],
        out_shape=[jax.ShapeDtypeStruct((t, d), F32), jax.ShapeDtypeStruct((t, d), BF16),
                   jax.ShapeDtypeStruct((1, d), F32)],
        compiler_params=_params("arbitrary"),
    )(h, dy, gain, dres)


def _mm(a, b, out_dtype, name, res=None):
    m, k = a.shape
    _, n = b.shape
    tm = _pick(m, ROW_TILES)
    out_bytes = jnp.dtype(out_dtype).itemsize + (4 if res is not None else 0)
    fits = lambda tn: 4 * tm * k + 4 * k * tn + 2 * tm * tn * out_bytes <= MM_VMEM_BUDGET
    tn = next(c for c in (n, 2816, 2048, 1536, 1408, 1024, 512, 384, 256, 128) if n % c == 0 and fits(c))

    def body(*refs):
        if res is None:
            a_ref, b_ref, o_ref = refs
        else:
            a_ref, b_ref, r_ref, o_ref = refs
        out = jnp.dot(a_ref[...], b_ref[...], preferred_element_type=F32)
        if res is not None:
            out = out + r_ref[...]
        o_ref[...] = out.astype(o_ref.dtype)

    in_specs = [pl.BlockSpec((tm, k), lambda i, j: (i, 0)), pl.BlockSpec((k, tn), lambda i, j: (0, j))]
    args = [a, b]
    if res is not None:
        in_specs.append(pl.BlockSpec((tm, tn), lambda i, j: (i, j)))
        args.append(res)
    return pl.pallas_call(
        body, grid=(m // tm, n // tn), name=name,
        in_specs=in_specs, out_specs=pl.BlockSpec((tm, tn), lambda i, j: (i, j)),
        out_shape=jax.ShapeDtypeStruct((m, n), out_dtype),
        compiler_params=_params("parallel", "parallel"),
    )(*args)


def _mm_tn(a, g, name):
    t, k = a.shape
    _, n = g.shape
    tk = _pick(k, (1024, 1408, 512))
    tn = _pick(n, (512, 640, 384, 256, 128))
    tt = next(c for c in (3328, 1280) + ROW_TILES
              if t % c == 0 and 4 * c * (tk + tn) + 8 * tk * tn <= MM_VMEM_BUDGET)
    nt = t // tt

    def body(a_ref, g_ref, o_ref):
        @pl.when(pl.program_id(2) == 0)
        def _():
            o_ref[...] = jnp.zeros_like(o_ref)

        o_ref[...] += lax.dot_general(a_ref[...], g_ref[...], (((0,), (0,)), ((), ())),
                                      preferred_element_type=F32)

    return pl.pallas_call(
        body, grid=(k // tk, n // tn, nt), name=name,
        in_specs=[pl.BlockSpec((tt, tk), lambda i, j, s: (s, i)), pl.BlockSpec((tt, tn), lambda i, j, s: (s, j))],
        out_specs=pl.BlockSpec((tk, tn), lambda i, j, s: (i, j)),
        out_shape=jax.ShapeDtypeStruct((k, n), F32),
        compiler_params=_params("parallel", "parallel", "arbitrary"),
    )(a, g)


def _split3_exact(x):
    def top(v):
        return lax.bitcast_convert_type(lax.bitcast_convert_type(v, jnp.int32) & jnp.int32(-65536), F32)

    hi = top(x)
    r1 = x - hi
    mid = top(r1)
    return hi, mid, r1 - mid


def _pair_head(ref, h, rows):
    x = ref[:, 128 * (h // 2):128 * (h // 2) + 128].astype(F32)
    return pltpu.roll(x, HEAD_DIM, axis=1) if h % 2 else x


def _lanes(rows):
    return lax.broadcasted_iota(jnp.int32, (rows, AUG), 1)


def _fox_prep(fq, scal):
    t = fq.shape[0]
    tt = _pick(t, (256, 128))

    def body(q_ref, k_ref, v_ref, s_ref, qa_ref, ka_ref, va_ref):
        lane = _lanes(tt)
        chi, cmid, clo = _split3_exact(s_ref[...])
        ones = lambda lo: jnp.where((lane >= lo) & (lane < lo + 3), 1.0, 0.0)
        for h in range(HEADS):
            col = lambda a: jnp.broadcast_to(a[:, h:h + 1], (tt, AUG))
            c1, c2, c3 = col(chi), col(cmid), col(clo)
            qx = jnp.where(lane == 64, c1, jnp.where(lane == 65, c2, jnp.where(lane == 66, c3, ones(67))))
            kx = jnp.where(lane == 67, -c1, jnp.where(lane == 68, -c2, jnp.where(lane == 69, -c3, ones(64) + ones(70))))
            qa_ref[h] = jnp.where(lane < HEAD_DIM, _pair_head(q_ref, h, tt) * (HEAD_DIM ** -0.5), qx).astype(BF16)
            ka_ref[h] = jnp.where(lane < HEAD_DIM, _pair_head(k_ref, h, tt), kx).astype(BF16)
            va_ref[h] = jnp.where(lane < HEAD_DIM, _pair_head(v_ref, h, tt), ones(64)).astype(BF16)

    out = pl.BlockSpec((HEADS, tt, AUG), lambda i: (0, i, 0))
    shp = jax.ShapeDtypeStruct((HEADS, t, AUG), BF16)
    return pl.pallas_call(
        body, grid=(t // tt,), name="fox_prep",
        in_specs=[pl.BlockSpec((tt, WIDTH), lambda i: (i, 0)), pl.BlockSpec((tt, WIDTH), lambda i: (i, 1)),
                  pl.BlockSpec((tt, WIDTH), lambda i: (i, 2)), pl.BlockSpec((tt, LANES), lambda i: (i, 0))],
        out_specs=[out, out, out], out_shape=[shp, shp, shp],
        compiler_params=_params("parallel"),
    )(fq, fq, fq, scal)


def _fox_post(oa):
    t = oa.shape[1]
    tt = _pick(t, (256, 128))

    def body(o_ref, out_ref):
        out_ref[...] = jnp.concatenate([o_ref[h][:, :HEAD_DIM] for h in range(HEADS)], axis=1).astype(BF16)

    return pl.pallas_call(
        body, grid=(t // tt,), name="fox_post",
        in_specs=[pl.BlockSpec((HEADS, tt, AUG), lambda i: (0, i, 0))],
        out_specs=pl.BlockSpec((tt, WIDTH), lambda i: (i, 0)),
        out_shape=jax.ShapeDtypeStruct((t, WIDTH), BF16),
        compiler_params=_params("parallel"),
    )(oa)


def _fox_bwd_prep(do, oa):
    t = do.shape[0]
    tt = _pick(t, (256, 128))

    def body(d_ref, o_ref, out_ref):
        lane = _lanes(tt)
        for h in range(HEADS):
            x = _pair_head(d_ref, h, tt)
            delta = jnp.sum(jnp.where(lane < HEAD_DIM, x * o_ref[h], 0.0), axis=1, keepdims=True)
            hi, mid, lo = _split3_exact(jnp.broadcast_to(-delta, (tt, AUG)))
            ex = jnp.where(lane == 64, hi, jnp.where(lane == 65, mid, jnp.where(lane == 66, lo, 0.0)))
            out_ref[h] = jnp.where(lane < HEAD_DIM, x, ex).astype(BF16)

    hm = pl.BlockSpec((HEADS, tt, AUG), lambda i: (0, i, 0))
    return pl.pallas_call(
        body, grid=(t // tt,), name="fox_bwd_prep",
        in_specs=[pl.BlockSpec((tt, WIDTH), lambda i: (i, 0)), hm], out_specs=hm,
        out_shape=jax.ShapeDtypeStruct((HEADS, t, AUG), BF16),
        compiler_params=_params("parallel"),
    )(do, oa)


def _fox_bwd_post(dqa, dka, dva):
    t = dqa.shape[1]
    tt = _pick(t, (256, 128))

    def body(dq_ref, dk_ref, dv_ref, out_ref, dsc_ref):
        lane = _lanes(tt)
        heads = lambda ref: jnp.concatenate([ref[h][:, :HEAD_DIM] for h in range(HEADS)], axis=1)
        out_ref[:, 0:WIDTH] = (heads(dq_ref) * (HEAD_DIM ** -0.5)).astype(BF16)
        out_ref[:, WIDTH:2 * WIDTH] = heads(dk_ref).astype(BF16)
        out_ref[:, 2 * WIDTH:] = heads(dv_ref).astype(BF16)
        dsc = jnp.zeros((tt, LANES), F32)
        for h in range(HEADS):
            both = jnp.where(lane == HEAD_DIM, dq_ref[h], 0.0) - jnp.where(lane == HEAD_DIM + 3, dk_ref[h], 0.0)
            dsc = jnp.where(lane == h, jnp.sum(both, axis=1, keepdims=True), dsc)
        dsc_ref[...] = dsc

    hm = pl.BlockSpec((HEADS, tt, AUG), lambda i: (0, i, 0))
    return pl.pallas_call(
        body, grid=(t // tt,), name="fox_bwd_post",
        in_specs=[hm, hm, hm],
        out_specs=[pl.BlockSpec((tt, 3 * WIDTH), lambda i: (i, 0)), pl.BlockSpec((tt, LANES), lambda i: (i, 0))],
        out_shape=[jax.ShapeDtypeStruct((t, 3 * WIDTH), BF16), jax.ShapeDtypeStruct((t, LANES), F32)],
        compiler_params=_params("parallel"),
    )(dqa, dka, dva)


def _fox_fwd(qa, ka, va, tq=None):
    h, t, _ = qa.shape
    tq = tq or _pick(t, ROW_TILES)

    def body(q_ref, k_ref, v_ref, o_ref, qb_ref, s_ref):
        i = pl.program_id(1)
        q = q_ref[...]
        row = lax.broadcasted_iota(jnp.int32, (tq, tq), 0)
        col = lax.broadcasted_iota(jnp.int32, (tq, tq), 1)

        def scores(j, slot):
            kj = k_ref[pl.ds(pl.multiple_of(j * tq, tq), tq), :]
            s_ref[slot] = lax.dot_general(q, kj, (((1,), (1,)), ((), ())), preferred_element_type=F32)

        def update(j, slot, carry, masked):
            m, acc = carry
            vj = v_ref[pl.ds(pl.multiple_of(j * tq, tq), tq), :]
            s = s_ref[slot]
            if masked:
                s = jnp.where(row >= col, s, NEG)
            m_new = jnp.maximum(m, jnp.max(s, axis=-1, keepdims=True))
            p = jnp.exp(s - m_new)
            alpha = jnp.exp(m - m_new)
            return m_new, acc * alpha + jnp.dot(p.astype(BF16), vj, preferred_element_type=F32)

        def pair(jj, carry):
            j = 2 * jj
            scores(j + 1, 1)
            carry = update(j, 0, carry, False)
            scores(j + 2, 0)
            return update(j + 1, 1, carry, False)

        def odd_tail(carry):
            scores(i, 1)
            return update(i, 1, update(i - 1, 0, carry, False), True)

        scores(0, 0)
        carry = (jnp.full((tq, 1), NEG, F32), jnp.zeros((tq, AUG), F32))
        carry = lax.fori_loop(0, i // 2, pair, carry)
        m, acc = lax.cond(i % 2 == 1, odd_tail, lambda c: update(i, 0, c, True), carry)
        lane = lax.broadcasted_iota(jnp.int32, (tq, AUG), 1)
        l = jnp.sum(jnp.where(lane == HEAD_DIM, acc, 0.0), axis=-1, keepdims=True)
        lse = jnp.broadcast_to(m + jnp.log(l), (tq, AUG))
        o_ref[...] = jnp.where(lane < HEAD_DIM, acc / l, lse)
        hi, mid, lo = _split3_exact(-lse)
        qb = jnp.where(lane == 70, hi, jnp.where(lane == 71, mid, jnp.where(lane == 72, lo, q.astype(F32))))
        qb_ref[...] = qb.astype(BF16)

    blk = pl.BlockSpec((None, tq, AUG), lambda hh, i: (hh, i, 0))
    return pl.pallas_call(
        body, grid=(h, t // tq), name="fox_fwd",
        in_specs=[blk, pl.BlockSpec((None, t, AUG), lambda hh, i: (hh, 0, 0)),
                  pl.BlockSpec((None, t, AUG), lambda hh, i: (hh, 0, 0))],
        out_specs=[blk, blk],
        out_shape=[jax.ShapeDtypeStruct((h, t, AUG), F32), jax.ShapeDtypeStruct((h, t, AUG), BF16)],
        scratch_shapes=[pltpu.VMEM((2, tq, tq), F32)],
        compiler_params=_params("parallel", "arbitrary"),
    )(qa, ka, va)


def _fox_bwd(qb, ka, va, doa, tq=None):
    h, t, _ = qb.shape
    tq = tq or _pick(t, ROW_TILES)
    nq = t // tq

    def body(q_ref, k_ref, v_ref, do_ref, dq_ref, dk_ref, dv_ref):
        j = pl.program_id(1)

        @pl.when(j == 0)
        def _():
            dq_ref[...] = jnp.zeros_like(dq_ref)

        kj = k_ref[...]
        vj = v_ref[...]
        krow = lax.broadcasted_iota(jnp.int32, (tq, tq), 0)
        qcol = lax.broadcasted_iota(jnp.int32, (tq, tq), 1)

        def step(i, carry, masked):
            dk, dv = carry
            start = pl.multiple_of(i * tq, tq)
            qi = q_ref[pl.ds(start, tq), :]
            doi = do_ref[pl.ds(start, tq), :]
            st = lax.dot_general(kj, qi, (((1,), (1,)), ((), ())), preferred_element_type=F32)
            if masked:
                st = jnp.where(qcol >= krow, st, NEG)
            pt = jnp.exp(st)
            dpt = lax.dot_general(vj, doi, (((1,), (1,)), ((), ())), preferred_element_type=F32)
            dst = (pt * dpt).astype(BF16)
            dv = dv + jnp.dot(pt.astype(BF16), doi, preferred_element_type=F32)
            dk = dk + jnp.dot(dst, qi, preferred_element_type=F32)
            dq_ref[pl.ds(start, tq), :] += lax.dot_general(dst, kj, (((0,), (0,)), ((), ())),
                                                            preferred_element_type=F32)
            return dk, dv

        zero = jnp.zeros((tq, AUG), F32)
        carry = step(j, (zero, zero), True)
        dk, dv = lax.fori_loop(j + 1, nq, lambda i, c: step(i, c, False), carry)
        dk_ref[...] = dk
        dv_ref[...] = dv

    full = pl.BlockSpec((None, t, AUG), lambda hh, j: (hh, 0, 0))
    blk = pl.BlockSpec((None, tq, AUG), lambda hh, j: (hh, j, 0))
    shp = jax.ShapeDtypeStruct((h, t, AUG), F32)
    return pl.pallas_call(
        body, grid=(h, nq), name="fox_bwd",
        in_specs=[full, blk, blk, full], out_specs=[full, blk, blk], out_shape=[shp, shp, shp],
        compiler_params=_params("parallel", "arbitrary"),
    )(qb, ka, va, doa)


def _seg_matrix():
    idx = np.arange(WIDTH) // HEAD_DIM
    return jnp.asarray((idx[:, None] == idx[None, :]).astype(np.float32))


def _segsum(x, e):
    return jnp.dot(x, e, precision=HI, preferred_element_type=F32)


def _silu(x):
    return x * jax.nn.sigmoid(x)


def _silu_grad(x):
    s = jax.nn.sigmoid(x)
    return s * (1.0 + x * (1.0 - s))


def _shift_down(x, prev8, k):
    r = pltpu.roll(x, k, axis=0)
    p = pltpu.roll(prev8, k, axis=0)
    row = lax.broadcasted_iota(jnp.int32, prev8.shape, 0)
    head = jnp.where(row < k, p, r[:8])
    return jnp.concatenate([head, r[8:]], axis=0)


def _shift_up(x, next8, k):
    n = x.shape[0]
    r = pltpu.roll(x, n - k, axis=0)
    p = pltpu.roll(next8, 8 - k, axis=0)
    row = lax.broadcasted_iota(jnp.int32, next8.shape, 0)
    tail = jnp.where(row >= 8 - k, p, r[n - 8:])
    return jnp.concatenate([r[:n - 8], tail], axis=0)


def _causal_conv(x, prev8, w_ref, width, cols=slice(None)):
    y = x * w_ref[width - 1:width, cols]
    for k in range(1, width):
        y = y + _shift_down(x, prev8, k) * w_ref[width - 1 - k:width - k, cols]
    return y


def _causal_conv_bwd(x, prev8, dy, dnext8, w_ref, dw_ref, width, cols=slice(None)):
    dx = dy * w_ref[width - 1:width, cols]
    dw_ref[width - 1:width, cols] += jnp.sum(dy * x, axis=0, keepdims=True)
    for k in range(1, width):
        dx = dx + _shift_up(dy, dnext8, k) * w_ref[width - 1 - k:width - k, cols]
        dw_ref[width - 1 - k:width - k, cols] += jnp.sum(dy * _shift_down(x, prev8, k), axis=0, keepdims=True)
    return dx


def _prev_spec(tt, width):
    return pl.BlockSpec((8, width), lambda i: (jnp.maximum(i * (tt // 8) - 1, 0), 0))


def _store_heads(ref, x):
    for h in range(HEADS):
        ref[h] = x[:, HEAD_DIM * h:HEAD_DIM * (h + 1)]


def _load_heads(ref):
    return jnp.concatenate([ref[h] for h in range(HEADS)], axis=1)


def _softplus(z):
    return jnp.maximum(z, 0.0) + jnp.log1p(jnp.exp(-jnp.abs(z)))


def _tri_masks(tt):
    r = lax.broadcasted_iota(jnp.int32, (tt, tt), 0)
    c = lax.broadcasted_iota(jnp.int32, (tt, tt), 1)
    same_chunk = lax.shift_right_logical(r, 6) == lax.shift_right_logical(c, 6)
    return r, c, same_chunk


def _gate_fwd(small, pbias, pscale):
    t = small.shape[0]
    tt = _pick(t, (256, 128))

    def body(x_ref, pb_ref, ps_ref, o_ref, carry_ref):
        @pl.when(pl.program_id(0) == 0)
        def _():
            carry_ref[...] = jnp.zeros_like(carry_ref)

        lane = lax.broadcasted_iota(jnp.int32, (tt, LANES), 1)
        z = x_ref[...] + pb_ref[...]
        log_f = jnp.where(lane < HEADS, -_softplus(-z), 0.0)
        g = jnp.where((lane >= 2 * HEADS) & (lane < 3 * HEADS), ps_ref[...] * _softplus(z), 0.0)
        r, c, same_chunk = _tri_masks(tt)
        lower = jnp.where(r >= c, 1.0, 0.0)
        lower_chunk = jnp.where((r >= c) & same_chunk, 1.0, 0.0)
        csum = jnp.dot(lower, log_f, precision=lax.Precision.HIGHEST, preferred_element_type=F32) + carry_ref[...]
        gc = jnp.dot(lower_chunk, g, precision=lax.Precision.HIGHEST, preferred_element_type=F32)
        carry_ref[...] += jnp.sum(log_f, axis=0, keepdims=True)
        o_ref[...] = jnp.where(lane < HEADS, csum, jnp.where(lane < 2 * HEADS, jax.nn.sigmoid(z), gc))

    row = pl.BlockSpec((tt, LANES), lambda i: (i, 0))
    vec = pl.BlockSpec((1, LANES), lambda i: (0, 0))
    return pl.pallas_call(
        body, grid=(t // tt,), name="gate_fwd", in_specs=[row, vec, vec], out_specs=row,
        out_shape=jax.ShapeDtypeStruct((t, LANES), F32),
        scratch_shapes=[pltpu.VMEM((1, LANES), F32)],
        compiler_params=_params("arbitrary"),
    )(small, pbias, pscale)


def _gate_bwd(small, pbias, pscale, dscal):
    t = small.shape[0]
    tt = _pick(t, (256, 128))
    nt = t // tt

    def body(x_ref, pb_ref, ps_ref, d_ref, dx_ref, dpb_ref, dps_ref, carry_ref):
        @pl.when(pl.program_id(0) == 0)
        def _():
            carry_ref[...] = jnp.zeros_like(carry_ref)
            dpb_ref[...] = jnp.zeros_like(dpb_ref)
            dps_ref[...] = jnp.zeros_like(dps_ref)

        lane = lax.broadcasted_iota(jnp.int32, (tt, LANES), 1)
        z = x_ref[...] + pb_ref[...]
        d = d_ref[...]
        dc = jnp.where(lane < HEADS, d, 0.0)
        dbeta = jnp.where((lane >= HEADS) & (lane < 2 * HEADS), d, 0.0)
        dgc = jnp.where((lane >= 2 * HEADS) & (lane < 3 * HEADS), d, 0.0)
        r, c, same_chunk = _tri_masks(tt)
        upper = jnp.where(r <= c, 1.0, 0.0)
        upper_chunk = jnp.where((r <= c) & same_chunk, 1.0, 0.0)
        dlogf = jnp.dot(upper, dc, precision=lax.Precision.HIGHEST, preferred_element_type=F32) + carry_ref[...]
        dg = jnp.dot(upper_chunk, dgc, precision=lax.Precision.HIGHEST, preferred_element_type=F32)
        carry_ref[...] += jnp.sum(dc, axis=0, keepdims=True)
        sg = jax.nn.sigmoid(z)
        dz = dlogf * (1.0 - sg) + dbeta * sg * (1.0 - sg) + dg * ps_ref[...] * sg
        dx_ref[...] = dz.astype(dx_ref.dtype)
        dpb_ref[...] += jnp.sum(dz, axis=0, keepdims=True)
        dps_ref[...] += jnp.sum(dg * _softplus(z), axis=0, keepdims=True)

    row = pl.BlockSpec((tt, LANES), lambda i: (nt - 1 - i, 0))
    vec = pl.BlockSpec((1, LANES), lambda i: (0, 0))
    return pl.pallas_call(
        body, grid=(nt,), name="gate_bwd", in_specs=[row, vec, vec, row], out_specs=[row, vec, vec],
        out_shape=[jax.ShapeDtypeStruct((t, LANES), BF16), jax.ShapeDtypeStruct((1, LANES), F32),
                   jax.ShapeDtypeStruct((1, LANES), F32)],
        scratch_shapes=[pltpu.VMEM((1, LANES), F32)],
        compiler_params=_params("arbitrary"),
    )(small, pbias, pscale, dscal)


def _gdn_pre_fwd(xg, conv_w, seg):
    t = xg.shape[0]
    c3 = 3 * WIDTH
    tt = _pick(t, (320, 256, 128))

    def body(x_ref, p_ref, w_ref, e_ref, q_ref, k_ref, v_ref):
        x = x_ref[...]
        prev = jnp.where(pl.program_id(0) == 0, 0.0, p_ref[...])
        s = _silu(_causal_conv(x, prev, w_ref, GDN_CONV))
        e = e_ref[...]
        q = s[:, :WIDTH]
        k = s[:, WIDTH:2 * WIDTH]
        _store_heads(q_ref, q * lax.rsqrt(_segsum(q * q, e) + RMS_EPS) * (HEAD_DIM ** -0.5))
        _store_heads(k_ref, k * lax.rsqrt(_segsum(k * k, e) + RMS_EPS))
        _store_heads(v_ref, s[:, 2 * WIDTH:])

    out = pl.BlockSpec((HEADS, tt, HEAD_DIM), lambda i: (0, i, 0))
    shp = jax.ShapeDtypeStruct((HEADS, t, HEAD_DIM), F32)
    return pl.pallas_call(
        body, grid=(t // tt,), name="gdn_pre_fwd",
        in_specs=[pl.BlockSpec((tt, c3), lambda i: (i, 0)), _prev_spec(tt, c3),
                  pl.BlockSpec((GDN_CONV, c3), lambda i: (0, 0)), pl.BlockSpec((WIDTH, WIDTH), lambda i: (0, 0))],
        out_specs=[out, out, out], out_shape=[shp, shp, shp],
        compiler_params=_params("arbitrary"),
    )(xg, xg, conv_w, seg)


def _gdn_pre_bwd(xg, conv_w, seg, dqn, dkn, dv):
    t = xg.shape[0]
    c3 = 3 * WIDTH
    tt = _pick(t, (320, 256, 128))
    nt = t // tt

    def body(x_ref, p_ref, w_ref, e_ref, dq_ref, dk_ref, dv_ref, dx_ref, dw_ref, carry_ref):
        step = pl.program_id(0)
        x = x_ref[...]
        e = e_ref[...]
        prev = jnp.where(step == nt - 1, 0.0, p_ref[...])
        y = _causal_conv(x, prev, w_ref, GDN_CONV)
        s = _silu(y)
        q = s[:, :WIDTH]
        k = s[:, WIDTH:2 * WIDTH]
        rq = lax.rsqrt(_segsum(q * q, e) + RMS_EPS)
        rk = lax.rsqrt(_segsum(k * k, e) + RMS_EPS)
        gq = _load_heads(dq_ref) * (HEAD_DIM ** -0.5)
        gk = _load_heads(dk_ref)
        dq = rq * gq - q * (rq * rq * rq) * _segsum(gq * q, e)
        dk = rk * gk - k * (rk * rk * rk) * _segsum(gk * k, e)
        dy = jnp.concatenate([dq, dk, _load_heads(dv_ref)], axis=1) * _silu_grad(y)

        @pl.when(step == 0)
        def _():
            carry_ref[...] = jnp.zeros_like(carry_ref)
            dw_ref[...] = jnp.zeros_like(dw_ref)

        dx = _causal_conv_bwd(x, prev, dy, carry_ref[...], w_ref, dw_ref, GDN_CONV)
        dx_ref[...] = dx.astype(dx_ref.dtype)
        carry_ref[...] = dy[:8]

    rev = lambda i: (nt - 1 - i, 0)
    blk = pl.BlockSpec((HEADS, tt, HEAD_DIM), lambda i: (0, nt - 1 - i, 0))
    return pl.pallas_call(
        body, grid=(nt,), name="gdn_pre_bwd",
        in_specs=[pl.BlockSpec((tt, c3), rev),
                  pl.BlockSpec((8, c3), lambda i: (jnp.maximum((nt - 1 - i) * (tt // 8) - 1, 0), 0)),
                  pl.BlockSpec((GDN_CONV, c3), lambda i: (0, 0)), pl.BlockSpec((WIDTH, WIDTH), lambda i: (0, 0)),
                  blk, blk, blk],
        out_specs=[pl.BlockSpec((tt, c3), rev), pl.BlockSpec((GDN_CONV, c3), lambda i: (0, 0))],
        out_shape=[jax.ShapeDtypeStruct((t, c3), BF16), jax.ShapeDtypeStruct((GDN_CONV, c3), F32)],
        scratch_shapes=[pltpu.VMEM((8, c3), F32)],
        compiler_params=_params("arbitrary"),
    )(xg, xg, conv_w, seg, dqn, dkn, dv)


def _bmm(a, b, ca, cb, precision=None):
    return lax.dot_general(a, b, (((ca,), (cb,)), ((0,), (0,))), precision=precision, preferred_element_type=F32)


def _bf(x):
    return x.astype(BF16)


def _tri_inverse(a, eye):
    x = -a
    tinv = eye + x
    pw = x
    for _ in range(5):
        pb = _bf(pw)
        pw = _bmm(pb, pb, 2, 1)
        tinv = tinv + _bmm(_bf(tinv), _bf(pw), 2, 1)
    resid = eye - _bmm(eye + a, tinv, 2, 1, precision=HI)
    return tinv + _bmm(_bf(tinv), _bf(resid), 2, 1)


def _gdn_intra(q, k, v, bc, gcc, gcr):
    ii = lax.broadcasted_iota(jnp.int32, (CHUNK, CHUNK), 0)
    jj = lax.broadcasted_iota(jnp.int32, (CHUNK, CHUNK), 1)
    tril = (ii >= jj)[None]
    strict = (ii > jj)[None]
    eye = jnp.where(ii == jj, 1.0, 0.0).astype(F32)[None]
    last = (ii == CHUNK - 1)[None]
    dm = jnp.exp(jnp.where(tril, gcc - gcr, NEG))
    gam = jnp.exp(gcc)
    kb = k * bc
    vb = v * bc
    kk = _bmm(_bf(kb), _bf(k), 2, 2)
    a = jnp.where(strict, kk * dm, 0.0)
    tinv = _tri_inverse(a, eye)
    kbg = kb * gam
    u = _bmm(tinv, vb, 2, 1, precision=HI)
    wk = _bmm(tinv, kbg, 2, 1, precision=HI)
    qk = _bmm(_bf(q), _bf(k), 2, 2)
    p = jnp.where(tril, qk * dm, 0.0)
    gl = jnp.sum(jnp.where(last, gcc, 0.0), axis=1, keepdims=True)
    edec = jnp.exp(gl - gcc)
    return dict(tril=tril, strict=strict, dm=dm, gam=gam, kb=kb, kk=kk, a=a, tinv=tinv, u=u, wk=wk, qk=qk, p=p,
                qg=q * gam, kt=k * edec, edec=edec, gaml=jnp.exp(gl), last=last)


def _gate_tiles(sc, gct, nb):
    rows = nb * CHUNK
    cols = lambda lane0: jnp.stack([jnp.broadcast_to(sc[:, lane0 + h:lane0 + h + 1], (rows, HEAD_DIM))
                                    for h in range(HEADS)], axis=0).reshape(HEADS * nb, CHUNK, HEAD_DIM)
    gcr = jnp.stack([jnp.broadcast_to(gct[h:h + 1, n * CHUNK:(n + 1) * CHUNK], (CHUNK, CHUNK))
                     for h in range(HEADS) for n in range(nb)], axis=0)
    return cols(HEADS), cols(2 * HEADS), gcr


def _gdn_fwd(q, k, v, scal, gct, nb=None):
    h, t, dh = q.shape
    nc = t // CHUNK
    nb = nb or _pick(nc, (4, 2))
    bsz = h * nb

    def body(q_ref, k_ref, v_ref, sc_ref, gt_ref, o_ref, s0_ref, state_ref):
        @pl.when(pl.program_id(0) == 0)
        def _():
            state_ref[...] = jnp.zeros_like(state_ref)

        ld = lambda r: r[...].reshape(bsz, CHUNK, dh)
        bc, gcc, gcr = _gate_tiles(sc_ref[...], gt_ref[...], nb)
        z = _gdn_intra(ld(q_ref), ld(k_ref), ld(v_ref), bc, gcc, gcr)
        per = lambda x: x.reshape((h, nb) + x.shape[1:])
        u, wk, p, qg, kt, gaml = (per(z[n]) for n in ("u", "wk", "p", "qg", "kt", "gaml"))
        s = state_ref[...]
        for n in range(nb):
            s0_ref[:, n] = s
            sb = _bf(s)
            vn = u[:, n] - _bmm(_bf(wk[:, n]), sb, 2, 1)
            o_ref[:, n * CHUNK:(n + 1) * CHUNK, :] = _bmm(_bf(qg[:, n]), sb, 2, 1) + _bmm(_bf(p[:, n]), _bf(vn), 2, 1)
            s = s * gaml[:, n] + _bmm(_bf(kt[:, n]), _bf(vn), 1, 1)
        state_ref[...] = s

    blk = pl.BlockSpec((h, nb * CHUNK, dh), lambda i: (0, i, 0))
    return pl.pallas_call(
        body, grid=(nc // nb,), name="gdn_fwd",
        in_specs=[blk] * 3 + [pl.BlockSpec((nb * CHUNK, LANES), lambda i: (i, 0)),
                              pl.BlockSpec((h, nb * CHUNK), lambda i: (0, i))],
        out_specs=[blk, pl.BlockSpec((h, nb, dh, dh), lambda i: (0, i, 0, 0))],
        out_shape=[jax.ShapeDtypeStruct((h, t, dh), F32), jax.ShapeDtypeStruct((h, nc, dh, dh), F32)],
        scratch_shapes=[pltpu.VMEM((h, dh, dh), F32)],
        compiler_params=_params("arbitrary"),
    )(q, k, v, scal, gct)


def _gdn_bwd(q, k, v, scal, gct, s0s, do, nb=None):
    h, t, dh = q.shape
    nc = t // CHUNK
    nb = nb or _pick(nc, (2,))
    bsz = h * nb
    ng = nc // nb
    rows = nb * CHUNK

    def body(q_ref, k_ref, v_ref, sc_ref, gt_ref, s0_ref, do_ref,
             dq_ref, dk_ref, dv_ref, dsc_ref, dgt_ref, ds_ref):
        @pl.when(pl.program_id(0) == 0)
        def _():
            ds_ref[...] = jnp.zeros_like(ds_ref)

        ld = lambda r: r[...].reshape(bsz, CHUNK, dh)
        q, k, v = ld(q_ref), ld(k_ref), ld(v_ref)
        bc, gcc, gcr = _gate_tiles(sc_ref[...], gt_ref[...], nb)
        z = _gdn_intra(q, k, v, bc, gcc, gcr)
        per = lambda x: x.reshape((h, nb) + x.shape[1:])
        u, wk, p, qg, kt, gaml = (per(z[n]) for n in ("u", "wk", "p", "qg", "kt", "gaml"))
        dout = per(ld(do_ref))
        ds = ds_ref[...]
        d_u, d_wk, d_p, d_qg, d_kt, d_gaml = ([None] * nb for _ in range(6))
        for n in reversed(range(nb)):
            s0 = s0_ref[:, n]
            s0b, dsb, dob = _bf(s0), _bf(ds), _bf(dout[:, n])
            wkb, qgb = _bf(wk[:, n]), _bf(qg[:, n])
            vn = u[:, n] - _bmm(wkb, s0b, 2, 1)
            dvn = _bmm(_bf(p[:, n]), dob, 1, 1) + _bmm(_bf(kt[:, n]), dsb, 2, 1)
            dvnb = _bf(dvn)
            d_u[n] = dvn
            d_p[n] = _bmm(dob, _bf(vn), 2, 2)
            d_qg[n] = _bmm(dob, s0b, 2, 2)
            d_kt[n] = _bmm(_bf(vn), dsb, 2, 2)
            d_gaml[n] = jnp.sum(s0 * ds, axis=1, keepdims=True)
            d_wk[n] = -_bmm(dvnb, s0b, 2, 2)
            ds = _bmm(qgb, dob, 1, 1) + gaml[:, n] * ds - _bmm(wkb, dvnb, 1, 1)
        ds_ref[...] = ds

        flat = lambda xs: jnp.stack(xs, axis=1).reshape((bsz,) + xs[0].shape[1:])
        d_u, d_wk, d_p, d_qg, d_kt, d_gaml = (flat(x) for x in (d_u, d_wk, d_p, d_qg, d_kt, d_gaml))
        tinv, gam, kb, dm = z["tinv"], z["gam"], z["kb"], z["dm"]
        drv = _bmm(tinv, d_u, 1, 1, precision=HI)
        drk = _bmm(tinv, d_wk, 1, 1, precision=HI)
        da = -(_bmm(_bf(drv), _bf(z["u"]), 2, 2) + _bmm(_bf(drk), _bf(z["wk"]), 2, 2))
        da = jnp.where(z["strict"], da, 0.0)
        d_p = jnp.where(z["tril"], d_p, 0.0)
        dkk = _bf(da * dm)
        dqk = _bf(d_p * dm)
        dkb = _bmm(dkk, _bf(k), 2, 1) + drk * gam
        dk = _bmm(dkk, _bf(kb), 1, 1) + _bmm(dqk, _bf(q), 1, 1) + dkb * bc + d_kt * z["edec"]
        dq = _bmm(dqk, _bf(k), 2, 1) + d_qg * gam
        mm = da * z["a"] + d_p * z["p"]
        dkt_kt = d_kt * z["kt"]
        dgl = jnp.sum(dkt_kt, axis=1, keepdims=True) + d_gaml * z["gaml"]
        dgc = mm + d_qg * z["qg"] + drk * kb * gam - dkt_kt + jnp.where(z["last"], dgl, 0.0)
        dq_ref[...] = dq.reshape(h, rows, dh)
        dk_ref[...] = dk.reshape(h, rows, dh)
        dv_ref[...] = (drv * bc).reshape(h, rows, dh)
        dbeta = (dkb * k + drv * v).reshape(h, rows, dh)
        dgc = dgc.reshape(h, rows, dh)
        lane = lax.broadcasted_iota(jnp.int32, (rows, LANES), 1)
        dsc = jnp.zeros((rows, LANES), F32)
        for hh in range(h):
            dsc = jnp.where(lane == HEADS + hh, jnp.sum(dbeta[hh], axis=1, keepdims=True), dsc)
            dsc = jnp.where(lane == 2 * HEADS + hh, jnp.sum(dgc[hh], axis=1, keepdims=True), dsc)
        dsc_ref[...] = dsc
        dgr = -jnp.sum(mm, axis=1, keepdims=True)
        for hh in range(h):
            for n in range(nb):
                dgt_ref[hh:hh + 1, n * CHUNK:(n + 1) * CHUNK] = dgr[hh * nb + n]

    blk = pl.BlockSpec((h, rows, dh), lambda i: (0, ng - 1 - i, 0))
    shp = jax.ShapeDtypeStruct((h, t, dh), F32)
    sc_spec = pl.BlockSpec((rows, LANES), lambda i: (ng - 1 - i, 0))
    gt_spec = pl.BlockSpec((h, rows), lambda i: (0, ng - 1 - i))
    return pl.pallas_call(
        body, grid=(ng,), name="gdn_bwd",
        in_specs=[blk] * 3 + [sc_spec, gt_spec, pl.BlockSpec((h, nb, dh, dh), lambda i: (0, ng - 1 - i, 0, 0)), blk],
        out_specs=[blk] * 3 + [sc_spec, gt_spec],
        out_shape=[shp] * 3 + [jax.ShapeDtypeStruct((t, LANES), F32), jax.ShapeDtypeStruct((h, t), F32)],
        scratch_shapes=[pltpu.VMEM((h, dh, dh), F32)],
        compiler_params=_params("arbitrary"),
    )(q, k, v, scal, gct, s0s, do)


def _gdn_post_fwd(o, xg, gain, seg):
    t = o.shape[1]
    tt = _pick(t, (320, 256, 128))

    def body(o_ref, z_ref, g_ref, e_ref, y_ref):
        x = _load_heads(o_ref)
        r = lax.rsqrt(_segsum(x * x, e_ref[...]) * (1.0 / HEAD_DIM) + RMS_EPS)
        y_ref[...] = (x * r * g_ref[...] * _silu(z_ref[...])).astype(y_ref.dtype)

    return pl.pallas_call(
        body, grid=(t // tt,), name="gdn_post_fwd",
        in_specs=[pl.BlockSpec((HEADS, tt, HEAD_DIM), lambda i: (0, i, 0)), pl.BlockSpec((tt, WIDTH), lambda i: (i, 3)),
                  pl.BlockSpec((1, WIDTH), lambda i: (0, 0)), pl.BlockSpec((WIDTH, WIDTH), lambda i: (0, 0))],
        out_specs=pl.BlockSpec((tt, WIDTH), lambda i: (i, 0)),
        out_shape=jax.ShapeDtypeStruct((t, WIDTH), BF16),
        compiler_params=_params("arbitrary"),
    )(o, xg, gain, seg)


def _gdn_post_bwd(o, xg, gain, seg, dy):
    t = o.shape[1]
    tt = _pick(t, (320, 256, 128))

    def body(o_ref, z_ref, g_ref, e_ref, dy_ref, do_ref, dz_ref, dg_ref):
        x = _load_heads(o_ref)
        zz = z_ref[...]
        e = e_ref[...]
        gain_v = g_ref[...]
        d = dy_ref[...]
        r = lax.rsqrt(_segsum(x * x, e) * (1.0 / HEAD_DIM) + RMS_EPS)
        xr = x * r
        don = d * _silu(zz)
        dz_ref[...] = (d * xr * gain_v * _silu_grad(zz)).astype(dz_ref.dtype)
        gy = don * gain_v
        _store_heads(do_ref, r * gy - xr * (r * r) * (_segsum(gy * x, e) * (1.0 / HEAD_DIM)))

        @pl.when(pl.program_id(0) == 0)
        def _():
            dg_ref[...] = jnp.zeros_like(dg_ref)

        dg_ref[...] += jnp.sum(don * xr, axis=0, keepdims=True)

    row = pl.BlockSpec((tt, WIDTH), lambda i: (i, 0))
    vec = pl.BlockSpec((1, WIDTH), lambda i: (0, 0))
    hm = pl.BlockSpec((HEADS, tt, HEAD_DIM), lambda i: (0, i, 0))
    return pl.pallas_call(
        body, grid=(t // tt,), name="gdn_post_bwd",
        in_specs=[hm, pl.BlockSpec((tt, WIDTH), lambda i: (i, 3)), vec,
                  pl.BlockSpec((WIDTH, WIDTH), lambda i: (0, 0)), row],
        out_specs=[hm, row, vec],
        out_shape=[jax.ShapeDtypeStruct((HEADS, t, HEAD_DIM), F32), jax.ShapeDtypeStruct((t, WIDTH), BF16),
                   jax.ShapeDtypeStruct((1, WIDTH), F32)],
        compiler_params=_params("arbitrary"),
    )(o, xg, gain, seg, dy)


def _mix_fwd(yf, yg, gates, bias):
    t, d = yf.shape
    tt = _pick(t, (320, 256, 128))

    def body(yf_ref, yg_ref, g1_ref, g2_ref, b1_ref, b2_ref, o_ref):
        g1 = jax.nn.sigmoid(g1_ref[...] + b1_ref[...])
        g2 = jax.nn.sigmoid(g2_ref[...] + b2_ref[...])
        o_ref[...] = (g1 * yf_ref[...] + g2 * yg_ref[...]).astype(o_ref.dtype)

    row = pl.BlockSpec((tt, d), lambda i: (i, 0))
    return pl.pallas_call(
        body, grid=(t // tt,), name="mix_fwd",
        in_specs=[row, row, row, pl.BlockSpec((tt, d), lambda i: (i, 1)),
                  pl.BlockSpec((1, d), lambda i: (0, 0)), pl.BlockSpec((1, d), lambda i: (0, 1))],
        out_specs=row, out_shape=jax.ShapeDtypeStruct((t, d), BF16),
        compiler_params=_params("arbitrary"),
    )(yf, yg, gates, gates, bias, bias)


def _mix_bwd(dmix, yf, yg, gates, bias):
    t, d = yf.shape
    tt = _pick(t, (320, 256, 128))

    def body(dm_ref, yf_ref, yg_ref, g1_ref, g2_ref, b1_ref, b2_ref, dyf_ref, dyg_ref, dg_ref, db_ref):
        dm = dm_ref[...]
        g1 = jax.nn.sigmoid(g1_ref[...] + b1_ref[...])
        g2 = jax.nn.sigmoid(g2_ref[...] + b2_ref[...])
        dyf_ref[...] = (dm * g1).astype(BF16)
        dyg_ref[...] = (dm * g2).astype(BF16)
        dgate = jnp.concatenate([dm * yf_ref[...] * g1 * (1.0 - g1), dm * yg_ref[...] * g2 * (1.0 - g2)], axis=1)
        dg_ref[...] = dgate.astype(BF16)

        @pl.when(pl.program_id(0) == 0)
        def _():
            db_ref[...] = jnp.zeros_like(db_ref)

        db_ref[...] += jnp.sum(dgate, axis=0, keepdims=True)

    row = pl.BlockSpec((tt, d), lambda i: (i, 0))
    wide = pl.BlockSpec((tt, 2 * d), lambda i: (i, 0))
    return pl.pallas_call(
        body, grid=(t // tt,), name="mix_bwd",
        in_specs=[row, row, row, row, pl.BlockSpec((tt, d), lambda i: (i, 1)),
                  pl.BlockSpec((1, d), lambda i: (0, 0)), pl.BlockSpec((1, d), lambda i: (0, 1))],
        out_specs=[row, row, wide, pl.BlockSpec((1, 2 * d), lambda i: (0, 0))],
        out_shape=[jax.ShapeDtypeStruct((t, d), BF16), jax.ShapeDtypeStruct((t, d), BF16),
                   jax.ShapeDtypeStruct((t, 2 * d), BF16), jax.ShapeDtypeStruct((1, 2 * d), F32)],
        compiler_params=_params("arbitrary"),
    )(dmix, yf, yg, gates, gates, bias, bias)


def _ffn_act_fwd(up, conv_w, conv_b):
    t, c = up.shape
    tt = 128

    def body(x_ref, p_ref, w_ref, b_ref, o_ref):
        first = pl.program_id(0) == 0

        def conv(cols):
            prev = jnp.where(first, 0.0, p_ref[:, cols])
            return _causal_conv(x_ref[:, cols], prev, w_ref, FFN_CONV, cols) + b_ref[:, cols]

        for lo in range(0, D_FF, FFN_LANES):
            gate = conv(slice(lo, lo + FFN_LANES))
            val = conv(slice(D_FF + lo, D_FF + lo + FFN_LANES))
            o_ref[:, lo:lo + FFN_LANES] = (_silu(gate) * val).astype(o_ref.dtype)

    return pl.pallas_call(
        body, grid=(t // tt,), name="ffn_act_fwd",
        in_specs=[pl.BlockSpec((tt, c), lambda i: (i, 0)), _prev_spec(tt, c),
                  pl.BlockSpec((FFN_CONV, c), lambda i: (0, 0)), pl.BlockSpec((1, c), lambda i: (0, 0))],
        out_specs=pl.BlockSpec((tt, D_FF), lambda i: (i, 0)),
        out_shape=jax.ShapeDtypeStruct((t, D_FF), BF16),
        compiler_params=_params("arbitrary"),
    )(up, up, conv_w, conv_b)


def _ffn_act_bwd(up, conv_w, conv_b, dact):
    t, c = up.shape
    tt = 128
    nt = t // tt

    def body(x_ref, p_ref, w_ref, b_ref, da_ref, dx_ref, dw_ref, db_ref, carry_ref):
        step = pl.program_id(0)

        @pl.when(step == 0)
        def _():
            carry_ref[...] = jnp.zeros_like(carry_ref)
            dw_ref[...] = jnp.zeros_like(dw_ref)
            db_ref[...] = jnp.zeros_like(db_ref)

        def conv(cols):
            x = x_ref[:, cols]
            prev = jnp.where(step == nt - 1, 0.0, p_ref[:, cols])
            return x, prev, _causal_conv(x, prev, w_ref, FFN_CONV, cols) + b_ref[:, cols]

        def back(cols, x, prev, du):
            dx = _causal_conv_bwd(x, prev, du, carry_ref[:, cols], w_ref, dw_ref, FFN_CONV, cols)
            dx_ref[:, cols] = dx.astype(dx_ref.dtype)
            db_ref[:, cols] += jnp.sum(du, axis=0, keepdims=True)
            carry_ref[:, cols] = du[:8]

        for lo in range(0, D_FF, FFN_LANES):
            gcols, vcols = slice(lo, lo + FFN_LANES), slice(D_FF + lo, D_FF + lo + FFN_LANES)
            xg, pg, gate = conv(gcols)
            xv, pv, val = conv(vcols)
            da = da_ref[:, gcols]
            back(gcols, xg, pg, da * val * _silu_grad(gate))
            back(vcols, xv, pv, da * _silu(gate))

    rev = lambda i: (nt - 1 - i, 0)
    return pl.pallas_call(
        body, grid=(nt,), name="ffn_act_bwd",
        in_specs=[pl.BlockSpec((tt, c), rev),
                  pl.BlockSpec((8, c), lambda i: (jnp.maximum((nt - 1 - i) * (tt // 8) - 1, 0), 0)),
                  pl.BlockSpec((FFN_CONV, c), lambda i: (0, 0)), pl.BlockSpec((1, c), lambda i: (0, 0)),
                  pl.BlockSpec((tt, D_FF), rev)],
        out_specs=[pl.BlockSpec((tt, c), rev), pl.BlockSpec((FFN_CONV, c), lambda i: (0, 0)),
                   pl.BlockSpec((1, c), lambda i: (0, 0))],
        out_shape=[jax.ShapeDtypeStruct((t, c), BF16), jax.ShapeDtypeStruct((FFN_CONV, c), F32),
                   jax.ShapeDtypeStruct((1, c), F32)],
        scratch_shapes=[pltpu.VMEM((8, c), F32)],
        compiler_params=_params("arbitrary"),
    )(up, up, conv_w, conv_b, dact)


def _final_loss(h2, target, gain, seq):
    t, d = h2.shape
    tr = _pick(t, (320, 256, 128))

    def body(h_ref, t_ref, g_ref, loss_ref, dh_ref, dhb_ref, dg_ref):
        i = pl.program_id(0)
        x = h_ref[...]
        gain_v = g_ref[...]
        r = lax.rsqrt(jnp.mean(x * x, axis=-1, keepdims=True) + RMS_EPS)
        xr = x * r
        rows = i * tr + lax.broadcasted_iota(jnp.int32, (tr, 1), 0)
        real = (rows >= N_META) & (rows < N_META + seq)
        err = jnp.where(real, xr * gain_v - t_ref[...], 0.0)
        dy = err * (1.0 / d)
        gy = dy * gain_v
        dh = r * (gy - xr * jnp.mean(gy * xr, axis=-1, keepdims=True))
        dh_ref[...] = dh
        dhb_ref[...] = dh.astype(BF16)

        @pl.when(i == 0)
        def _():
            loss_ref[...] = jnp.zeros_like(loss_ref)
            dg_ref[...] = jnp.zeros_like(dg_ref)

        part = jnp.sum(jnp.sum(err * err, axis=-1, keepdims=True), axis=0, keepdims=True)
        loss_ref[...] += jnp.broadcast_to(part * (0.5 / d), loss_ref.shape)
        dg_ref[...] += jnp.sum(dy * xr, axis=0, keepdims=True)

    row = pl.BlockSpec((tr, d), lambda i: (i, 0))
    vec = pl.BlockSpec((1, d), lambda i: (0, 0))
    return pl.pallas_call(
        body, grid=(t // tr,), name="final_loss",
        in_specs=[row, row, vec],
        out_specs=[pl.BlockSpec((1, LANES), lambda i: (0, 0)), row, row, vec],
        out_shape=[jax.ShapeDtypeStruct((1, LANES), F32), jax.ShapeDtypeStruct((t, d), F32),
                   jax.ShapeDtypeStruct((t, d), BF16), jax.ShapeDtypeStruct((1, d), F32)],
        compiler_params=_params("arbitrary"),
    )(h2, target, gain)


ADAM_TILE_BYTES = 1 << 20


def _adamw(w, m, v, grecv, name):
    r, cols = w.shape
    tr = r
    if r * cols * 4 > ADAM_TILE_BYTES:
        tr = max(d for d in range(8, r + 1, 8) if r % d == 0 and d * cols * 4 <= ADAM_TILE_BYTES)

    def body(w_ref, m_ref, v_ref, g_ref, go_ref, d_ref, mo_ref, vo_ref):
        g = g_ref[0].astype(F32)
        for s in range(1, N_DEV):
            g = g + g_ref[s].astype(F32)
        wv = w_ref[...]
        mn = ADAM_B1 * m_ref[...] + (1.0 - ADAM_B1) * g
        vn = ADAM_B2 * v_ref[...] + (1.0 - ADAM_B2) * (g * g)
        m_hat = mn / (1.0 - ADAM_B1 ** ADAM_STEP)
        v_hat = vn / (1.0 - ADAM_B2 ** ADAM_STEP)
        go_ref[...] = g
        d_ref[...] = -ADAM_LR * (m_hat / (jnp.sqrt(v_hat) + ADAM_EPS) + ADAM_WD * wv)
        mo_ref[...] = mn
        vo_ref[...] = vn

    row = pl.BlockSpec((tr, cols), lambda i: (i, 0))
    shp = jax.ShapeDtypeStruct((r, cols), F32)
    return pl.pallas_call(
        body, grid=(r // tr,), name=name,
        in_specs=[row, row, row, pl.BlockSpec((N_DEV, tr, cols), lambda i: (0, i, 0))],
        out_specs=[row] * 4, out_shape=[shp] * 4,
        compiler_params=_params("parallel"),
    )(w, m, v, grecv)


def _mesh_pos():
    return lax.axis_index("x"), lax.axis_index("y"), lax.axis_index("c")


def _all_gather(shards):
    n = len(shards)

    def body(*refs):
        x_refs, out_refs = refs[:n], refs[n:2 * n]
        send_sems, recv_sems, local_sems = refs[2 * n:]
        x, y, c = _mesh_pos()
        me, sibling = (x, y, c), (x, y, 1 - c)
        chips = [(1 - x, y), (x, 1 - y), (1 - x, 1 - y)]

        def slot(a, px, py, pc):
            return out_refs[a].at[4 * px + 2 * py + pc]

        def copy(a, kk, block, to, src=None):
            return pltpu.make_async_remote_copy(
                src_ref=slot(a, *block) if src is None else src, dst_ref=slot(a, *block),
                send_sem=send_sems.at[7 * a + kk], recv_sem=recv_sems.at[7 * a + kk],
                device_id=to, device_id_type=MESH_ID)

        mine = [pltpu.make_async_copy(x_refs[a], slot(a, *me), local_sems.at[a]) for a in range(n)]
        first = []
        for a in range(n):
            first.append(copy(a, 0, me, sibling, src=x_refs[a]))
            first += [copy(a, 1 + j, me, (*chip, c), src=x_refs[a]) for j, chip in enumerate(chips)]
        for cp in mine + first:
            cp.start()
        passed = []
        for j, chip in enumerate(chips):
            for a in range(n):
                copy(a, 1 + j, (*chip, c), me).wait_recv()
                passed.append(copy(a, 4 + j, (*chip, c), sibling))
                passed[-1].start()
        for a in range(n):
            copy(a, 0, sibling, me).wait_recv()
        for j, chip in enumerate(chips):
            for a in range(n):
                copy(a, 4 + j, (*chip, 1 - c), me).wait_recv()
        for cp in first + passed:
            cp.wait_send()
        for cp in mine:
            cp.wait()

    hbm = pl.BlockSpec(memory_space=pl.ANY)
    return pl.pallas_call(
        body, name="weight_all_gather", in_specs=[hbm] * n, out_specs=[hbm] * n,
        out_shape=[jax.ShapeDtypeStruct((N_DEV,) + s.shape, s.dtype) for s in shards],
        scratch_shapes=[pltpu.SemaphoreType.DMA((7 * n,)), pltpu.SemaphoreType.DMA((7 * n,)),
                        pltpu.SemaphoreType.DMA((n,))],
    )(*shards)


def _grad_exchange(blocks, small):
    n = len(blocks)

    def body(*refs):
        src_refs, dst_refs = refs[:n + 1], refs[n + 1:2 * n + 2]
        send_sems, recv_sems, local_sems = refs[2 * n + 2:]
        x, y, c = _mesh_pos()
        me = 4 * x + 2 * y + c
        copies = []
        for kk in range(1, N_DEV):
            px = 1 - x if kk & 4 else x
            py = 1 - y if kk & 2 else y
            pc = 1 - c if kk & 1 else c
            peer = 4 * px + 2 * py + pc
            for a in range(n + 1):
                copies.append(pltpu.make_async_remote_copy(
                    src_ref=src_refs[a].at[peer] if a < n else src_refs[a], dst_ref=dst_refs[a].at[me],
                    send_sem=send_sems.at[7 * a + kk - 1], recv_sem=recv_sems.at[7 * a + kk - 1],
                    device_id=(px, py, pc), device_id_type=MESH_ID))
        own = [pltpu.make_async_copy(src_refs[a].at[me] if a < n else src_refs[a], dst_refs[a].at[me],
                                     local_sems.at[a]) for a in range(n + 1)]
        for cp in own + copies:
            cp.start()
        for cp in copies + own:
            cp.wait()

    hbm = pl.BlockSpec(memory_space=pl.ANY)
    return pl.pallas_call(
        body, name="grad_exchange", in_specs=[hbm] * (n + 1), out_specs=[hbm] * (n + 1),
        out_shape=[jax.ShapeDtypeStruct(b.shape, b.dtype) for b in blocks]
        + [jax.ShapeDtypeStruct((N_DEV,) + small.shape, small.dtype)],
        scratch_shapes=[pltpu.SemaphoreType.DMA((7 * (n + 1),)), pltpu.SemaphoreType.DMA((7 * (n + 1),)),
                        pltpu.SemaphoreType.DMA((n + 1,))],
    )(*blocks, small)


def _pad_flat(parts, rows):
    flat = jnp.concatenate([p.reshape(-1) for p in parts])
    return jnp.pad(flat, (0, rows * LANES - flat.shape[0])).reshape(rows, LANES)


def _rows_for(n_elems, mult=1024):
    rows = -(-n_elems // LANES)
    return -(-rows // mult) * mult


SHARDED = ("meta_tokens", "w_in", "gdn_conv_w", "w_branch_fox", "w_branch_gdn", "w_out", "ffn_w_up", "ffn_conv_w",
           "ffn_w_down")
MATMUL = ("w_in", "w_branch_fox", "w_branch_gdn", "w_out", "ffn_w_up", "ffn_w_down")
EXACT = ("meta_tokens", "gdn_conv_w", "ffn_conv_w")
REPLICATED = ("fgt_bias", "gdn_a_log", "gdn_dt_bias", "gdn_norm_w", "gate_bias", "norm_mix_w", "norm_ffn_w",
              "ffn_conv_b", "norm_final_w")
WEIGHTS = ("meta_tokens", "w_in", "fgt_bias", "gdn_conv_w", "gdn_a_log", "gdn_dt_bias", "gdn_norm_w", "gate_bias",
           "w_branch_fox", "w_branch_gdn", "w_out", "norm_mix_w", "norm_ffn_w", "ffn_w_up", "ffn_conv_w",
           "ffn_conv_b", "ffn_w_down", "norm_final_w")


def _unpack(buf, shapes):
    flat = buf.reshape(-1)
    out, off = [], 0
    for s in shapes:
        n = int(np.prod(s))
        out.append(flat[off:off + n].reshape(s))
        off += n
    return out


def _unpack_gathered(buf, shapes):
    flat = buf.reshape(N_DEV, -1)
    out, off = [], 0
    for s in shapes:
        n = int(np.prod(s))
        out.append(flat[:, off:off + n].reshape((N_DEV,) + tuple(s)))
        off += n
    return out


def _cat_cols(g):
    return g.transpose(1, 0, 2).reshape(g.shape[1], -1)


def _col_blocks(full, width):
    return full.reshape(full.shape[0], N_DEV, width).transpose(1, 0, 2)


def _local_step(x, target, w):
    seq = x.shape[0]
    t = _padded_tokens(seq)
    pad = t - N_META - seq
    seg = _seg_matrix()
    zrows = jnp.zeros((pad, D_MODEL), F32)
    h0 = jnp.concatenate([w["meta_tokens"], x, zrows], axis=0)
    tgt = jnp.concatenate([jnp.zeros((N_META, D_MODEL), F32), target, zrows], axis=0)

    w_in = w["w_in"]
    o_f, o_g, o_z, o_b, o_a, o_gate = 1536, 1544, 3080, 3592, 3600, 3608
    w_small = jnp.concatenate([w_in[:, o_f:o_f + 8], w_in[:, o_b:o_b + 8], w_in[:, o_a:o_a + 8],
                               jnp.zeros((D_MODEL, LANES - 24), BF16)], axis=1)
    w_r = jnp.concatenate([w_in[:, :1536], w_in[:, o_g:o_z], w_in[:, o_z:o_b], w_in[:, o_gate:], w_small], axis=1)

    a1 = _rmsnorm_fwd(h0, w["norm_mix_w"])
    fq = _mm(a1, w_r[:, :1536], BF16, "proj_fox")
    xg = _mm(a1, w_r[:, 1536:3584], F32, "proj_gdn")
    gt = _mm(a1, w_r[:, 3584:5632], F32, "proj_gates")
    sm = _mm(a1, w_r[:, 5632:], F32, "proj_small")

    lanes_pad = lambda a, lo: jnp.pad(a, ((0, 0), (lo, LANES - lo - a.shape[1])))
    neg_exp_a = -jnp.exp(w["gdn_a_log"])
    pbias = lanes_pad(w["fgt_bias"], 0) + lanes_pad(w["gdn_dt_bias"], 2 * HEADS)
    pscale = lanes_pad(neg_exp_a, 2 * HEADS)
    scal = _gate_fwd(sm, pbias, pscale)
    gct = scal[:, 2 * HEADS:3 * HEADS].T

    qa, ka, va = _fox_prep(fq, scal)
    oa, qb = _fox_fwd(qa, ka, va)
    o_fox = _fox_post(oa)

    qh, kh, vh = _gdn_pre_fwd(xg, w["gdn_conv_w"], seg)
    og, s0s = _gdn_fwd(qh, kh, vh, scal, gct)
    norm_w = jnp.tile(w["gdn_norm_w"], (1, HEADS))
    ogn = _gdn_post_fwd(og, xg, norm_w, seg)

    yf = _mm(o_fox, w["w_branch_fox"], F32, "branch_fox")
    yg = _mm(ogn, w["w_branch_gdn"], F32, "branch_gdn")
    mix = _mix_fwd(yf, yg, gt, w["gate_bias"])
    h1 = _mm(mix, w["w_out"], F32, "out_proj", res=h0)
    a2 = _rmsnorm_fwd(h1, w["norm_ffn_w"])
    up = _mm(a2, w["ffn_w_up"], F32, "ffn_up")
    act = _ffn_act_fwd(up, w["ffn_conv_w"], w["ffn_conv_b"])
    h2 = _mm(act, w["ffn_w_down"], F32, "ffn_down", res=h1)
    loss, dh2, dh2b, g_final = _final_loss(h2, tgt, w["norm_final_w"].reshape(1, D_MODEL), seq)

    grads = {"norm_final_w": g_final.reshape(D_MODEL)}
    grads["ffn_w_down"] = _mm_tn(act, dh2b, "wgrad_ffn_down")
    dact = _mm(dh2b, w["ffn_w_down"].T, F32, "dgrad_ffn_down")
    dup, g_cw, g_cb = _ffn_act_bwd(up, w["ffn_conv_w"], w["ffn_conv_b"], dact)
    grads["ffn_conv_w"], grads["ffn_conv_b"] = g_cw, g_cb
    grads["ffn_w_up"] = _mm_tn(a2, dup, "wgrad_ffn_up")
    da2 = _mm(dup, w["ffn_w_up"].T, F32, "dgrad_ffn_up")
    dh1, dh1b, grads["norm_ffn_w"] = _rmsnorm_bwd(h1, da2, w["norm_ffn_w"], dh2)
    grads["w_out"] = _mm_tn(mix, dh1b, "wgrad_out")
    dmix = _mm(dh1b, w["w_out"].T, F32, "dgrad_out")
    dyf, dyg, dgt, grads["gate_bias"] = _mix_bwd(dmix, yf, yg, gt, w["gate_bias"])
    grads["w_branch_fox"] = _mm_tn(o_fox, dyf, "wgrad_branch_fox")
    grads["w_branch_gdn"] = _mm_tn(ogn, dyg, "wgrad_branch_gdn")
    do_fox = _mm(dyf, w["w_branch_fox"].T, F32, "dgrad_branch_fox")
    dogn = _mm(dyg, w["w_branch_gdn"].T, F32, "dgrad_branch_gdn")

    dog, dz, g_nw = _gdn_post_bwd(og, xg, norm_w, seg, dogn)
    grads["gdn_norm_w"] = g_nw.reshape(HEADS, HEAD_DIM).sum(axis=0)[None]
    dqh, dkh, dvh, dscal_g, dgct = _gdn_bwd(qh, kh, vh, scal, gct, s0s, dog)
    dxg, grads["gdn_conv_w"] = _gdn_pre_bwd(xg, w["gdn_conv_w"], seg, dqh, dkh, dvh)

    dqa, dka, dva = _fox_bwd(qb, ka, va, _fox_bwd_prep(do_fox, oa))
    dfq, dscal_c = _fox_bwd_post(dqa, dka, dva)

    dscal = dscal_c + dscal_g + lanes_pad(dgct.T, 2 * HEADS)
    dsm, dpb, dps = _gate_bwd(sm, pbias, pscale, dscal)
    grads["fgt_bias"] = dpb[:, :HEADS]
    grads["gdn_dt_bias"] = dpb[:, 2 * HEADS:3 * HEADS]
    grads["gdn_a_log"] = dps[:, 2 * HEADS:3 * HEADS] * neg_exp_a

    dproj = jnp.concatenate([dfq, dxg, dz, dgt, dsm], axis=1)
    g_r = _mm_tn(a1, dproj, "wgrad_in")
    grads["w_in"] = jnp.concatenate([g_r[:, :1536], g_r[:, 5632:5640], g_r[:, 1536:3072], g_r[:, 3072:3584],
                                     g_r[:, 5640:5648], g_r[:, 5648:5656], g_r[:, 3584:5632]], axis=1)
    da1 = _mm(dproj, w_r.T, F32, "dgrad_in")
    dh0, _, grads["norm_mix_w"] = _rmsnorm_bwd(h0, da1, w["norm_mix_w"], dh1)
    grads["meta_tokens"] = dh0[:N_META]
    return loss, dh0[N_META:N_META + seq], grads


def _shard_pieces(arrs):
    return [arrs[n][0] if arrs[n].ndim == 3 else arrs[n] for n in SHARDED]


def _full_grad_blocks(grads):
    g = grads
    cols = lambda a, wd: _col_blocks(a, wd)
    rows = lambda a: a.reshape(N_DEV, a.shape[0] // N_DEV, a.shape[1])
    per = [cols(g["meta_tokens"], 128), cols(g["w_in"], IN_WIDTH // N_DEV), cols(g["gdn_conv_w"], 3 * WIDTH // N_DEV),
           cols(g["w_branch_fox"], D_MODEL // N_DEV), cols(g["w_branch_gdn"], D_MODEL // N_DEV), rows(g["w_out"]),
           cols(g["ffn_w_up"], 2 * D_FF // N_DEV), cols(g["ffn_conv_w"], 2 * D_FF // N_DEV), rows(g["ffn_w_down"])]
    return per


def kernel(x, meta_tokens, w_in, fgt_bias, gdn_conv_w, gdn_a_log, gdn_dt_bias, gdn_norm_w, gate_bias, w_branch_fox, w_branch_gdn, w_out, norm_mix_w, norm_ffn_w, ffn_w_up, ffn_conv_w, ffn_conv_b, ffn_w_down, norm_final_w, loss_target, m_meta_tokens, m_w_in, m_fgt_bias, m_gdn_conv_w, m_gdn_a_log, m_gdn_dt_bias, m_gdn_norm_w, m_gate_bias, m_w_branch_fox, m_w_branch_gdn, m_w_out, m_norm_mix_w, m_norm_ffn_w, m_ffn_w_up, m_ffn_conv_w, m_ffn_conv_b, m_ffn_w_down, m_norm_final_w, v_meta_tokens, v_w_in, v_fgt_bias, v_gdn_conv_w, v_gdn_a_log, v_gdn_dt_bias, v_gdn_norm_w, v_gate_bias, v_w_branch_fox, v_w_branch_gdn, v_w_out, v_norm_mix_w, v_norm_ffn_w, v_ffn_w_up, v_ffn_conv_w, v_ffn_conv_b, v_ffn_w_down, v_norm_final_w):
    wts = dict(meta_tokens=meta_tokens, w_in=w_in, fgt_bias=fgt_bias, gdn_conv_w=gdn_conv_w, gdn_a_log=gdn_a_log,
               gdn_dt_bias=gdn_dt_bias, gdn_norm_w=gdn_norm_w, gate_bias=gate_bias, w_branch_fox=w_branch_fox,
               w_branch_gdn=w_branch_gdn, w_out=w_out, norm_mix_w=norm_mix_w, norm_ffn_w=norm_ffn_w,
               ffn_w_up=ffn_w_up, ffn_conv_w=ffn_conv_w, ffn_conv_b=ffn_conv_b, ffn_w_down=ffn_w_down,
               norm_final_w=norm_final_w)
    mom = dict(meta_tokens=m_meta_tokens, w_in=m_w_in, fgt_bias=m_fgt_bias, gdn_conv_w=m_gdn_conv_w,
               gdn_a_log=m_gdn_a_log, gdn_dt_bias=m_gdn_dt_bias, gdn_norm_w=m_gdn_norm_w, gate_bias=m_gate_bias,
               w_branch_fox=m_w_branch_fox, w_branch_gdn=m_w_branch_gdn, w_out=m_w_out, norm_mix_w=m_norm_mix_w,
               norm_ffn_w=m_norm_ffn_w, ffn_w_up=m_ffn_w_up, ffn_conv_w=m_ffn_conv_w, ffn_conv_b=m_ffn_conv_b,
               ffn_w_down=m_ffn_w_down, norm_final_w=m_norm_final_w)
    var = dict(meta_tokens=v_meta_tokens, w_in=v_w_in, fgt_bias=v_fgt_bias, gdn_conv_w=v_gdn_conv_w,
               gdn_a_log=v_gdn_a_log, gdn_dt_bias=v_gdn_dt_bias, gdn_norm_w=v_gdn_norm_w, gate_bias=v_gate_bias,
               w_branch_fox=v_w_branch_fox, w_branch_gdn=v_w_branch_gdn, w_out=v_w_out, norm_mix_w=v_norm_mix_w,
               norm_ffn_w=v_norm_ffn_w, ffn_w_up=v_ffn_w_up, ffn_conv_w=v_ffn_conv_w, ffn_conv_b=v_ffn_conv_b,
               ffn_w_down=v_ffn_w_down, norm_final_w=v_norm_final_w)

    sh = dict(zip(SHARDED, _shard_pieces(wts)))
    exact_shapes = [sh[n].shape for n in EXACT]
    rows_exact = _rows_for(sum(int(np.prod(s)) for s in exact_shapes), 8)
    gathered = _all_gather([sh[n].astype(BF16) for n in MATMUL] + [_pad_flat([sh[n] for n in EXACT], rows_exact)])
    g_in, g_bf, g_bg, g_out, g_up, g_down = gathered[:6]
    meta_full, conv_full, fconv_full = (_cat_cols(a) for a in _unpack_gathered(gathered[6], exact_shapes))
    full = dict(
        meta_tokens=meta_full, w_in=_cat_cols(g_in), gdn_conv_w=conv_full,
        w_branch_fox=_cat_cols(g_bf), w_branch_gdn=_cat_cols(g_bg), w_out=g_out.reshape(D_MODEL, D_MODEL),
        ffn_w_up=_cat_cols(g_up), ffn_conv_w=fconv_full, ffn_w_down=g_down.reshape(D_FF, D_MODEL),
        fgt_bias=fgt_bias, gdn_a_log=gdn_a_log, gdn_dt_bias=gdn_dt_bias, gdn_norm_w=gdn_norm_w, gate_bias=gate_bias,
        norm_mix_w=norm_mix_w, norm_ffn_w=norm_ffn_w, ffn_conv_b=ffn_conv_b, norm_final_w=norm_final_w)

    loss, grad_x, grads = _local_step(x[0], loss_target[0], full)

    blocks = [b.astype(BF16) for b in _full_grad_blocks(grads)]
    rep_parts = [grads[n] for n in REPLICATED] + [loss[:, :1]]
    rep_shapes = [wts[n].shape for n in REPLICATED]
    rows_small = _rows_for(sum(int(np.prod(p.shape)) for p in rep_parts), 8)
    received = _grad_exchange(blocks, _pad_flat(rep_parts, rows_small))

    result = {}
    kinds = ("grad", "delta", "new_m", "new_v")
    for n, recv in zip(SHARDED, received[:-1]):
        outs = _adamw(sh[n], _shard_pieces(mom)[SHARDED.index(n)], _shard_pieces(var)[SHARDED.index(n)], recv,
                      "adamw_" + n)
        for kind, a in zip(kinds, outs):
            result[kind, n] = a.reshape(wts[n].shape)
    rep_w = _pad_flat([wts[n] for n in REPLICATED] + [jnp.zeros((1, 1), F32)], rows_small)
    rep_m = _pad_flat([mom[n] for n in REPLICATED] + [jnp.zeros((1, 1), F32)], rows_small)
    rep_v = _pad_flat([var[n] for n in REPLICATED] + [jnp.ones((1, 1), F32)], rows_small)
    outs_r = _adamw(rep_w, rep_m, rep_v, received[-1], "adamw_replicated")
    for kind, br in zip(kinds, outs_r):
        for n, a in zip(REPLICATED, _unpack(br, rep_shapes)):
            result[kind, n] = a
    n_rep = sum(int(np.prod(s)) for s in rep_shapes)
    total_loss = outs_r[0].reshape(-1)[n_rep]
    out = [total_loss, grad_x[None]]
    for kind in ("grad", "delta", "new_m", "new_v"):
        out += [result[kind, n] for n in WEIGHTS]
    return tuple(out)
```

```python
import functools

import jax
import jax.numpy as jnp
import numpy as np
from jax import lax
from jax.experimental import pallas as pl
from jax.experimental.pallas import tpu as pltpu

F32 = jnp.float32
BF16 = jnp.bfloat16

D_MODEL = 1024
N_META = 16
HEADS = 8
HEAD_DIM = 64
WIDTH = HEADS * HEAD_DIM
CHUNK = 64
GDN_CONV = 4
D_FF = 2816
FFN_CONV = 3
IN_WIDTH = 5656
IN_PAD = 5760
RMS_EPS = 1e-6
NEG = -1e30
AUG = 128
N_DEV = 8
LANES = 128

ADAM_LR = 0.001
ADAM_B1 = 0.9
ADAM_B2 = 0.999
ADAM_EPS = 1e-08
ADAM_WD = 0.01
ADAM_STEP = 10

VMEM_LIMIT = 56 * 1024 * 1024
MM_VMEM_BUDGET = 36 * 1024 * 1024
FFN_LANES = 128
HI = lax.Precision.HIGH
MESH_ID = pl.DeviceIdType.MESH


def _pick(n, cands):
    for c in cands:
        if n % c == 0:
            return c
    raise ValueError(f"no tile for {n} in {cands}")


def _params(*sem):
    return pltpu.CompilerParams(dimension_semantics=sem if sem else None, vmem_limit_bytes=VMEM_LIMIT)


def _padded_tokens(seq):
    t = -(-(N_META + seq) // 128) * 128
    if t > 1280 and t % 640:
        t = -(-t // 640) * 640
    return t


ROW_TILES = (640, 512, 384, 256, 128)


def _rmsnorm_fwd(h, gain):
    t, d = h.shape
    tr = _pick(t, ROW_TILES)

    def body(h_ref, g_ref, o_ref):
        x = h_ref[...]
        r = lax.rsqrt(jnp.mean(x * x, axis=-1, keepdims=True) + RMS_EPS)
        o_ref[...] = (x * r * g_ref[...]).astype(o_ref.dtype)

    return pl.pallas_call(
        body, grid=(t // tr,), name="rmsnorm_fwd",
        in_specs=[pl.BlockSpec((tr, d), lambda i: (i, 0)), pl.BlockSpec((1, d), lambda i: (0, 0))],
        out_specs=pl.BlockSpec((tr, d), lambda i: (i, 0)),
        out_shape=jax.ShapeDtypeStruct((t, d), BF16),
        compiler_params=_params("arbitrary"),
    )(h, gain)


def _rmsnorm_bwd(h, dy, gain, dres):
    t, d = h.shape
    tr = _pick(t, (320, 256, 128))

    def body(h_ref, dy_ref, g_ref, dres_ref, dh_ref, dhb_ref, dg_ref):
        x = h_ref[...]
        dyv = dy_ref[...]
        r = lax.rsqrt(jnp.mean(x * x, axis=-1, keepdims=True) + RMS_EPS)
        gy = dyv * g_ref[...]
        m = jnp.mean(gy * x, axis=-1, keepdims=True)
        dh = dres_ref[...] + r * gy - x * (r * r * r * m)
        dh_ref[...] = dh
        dhb_ref[...] = dh.astype(BF16)

        @pl.when(pl.program_id(0) == 0)
        def _():
            dg_ref[...] = jnp.zeros_like(dg_ref)

        dg_ref[...] += jnp.sum(dyv * x * r, axis=0, keepdims=True)

    row = pl.BlockSpec((tr, d), lambda i: (i, 0))
    vec = pl.BlockSpec((1, d), lambda i: (0, 0))
    return pl.pallas_call(
        body, grid=(t // tr,), name="rmsnorm_bwd",
        in_specs=[row, row, vec, row], out_specs=[row, row, vec],
        out_shape=[jax.ShapeDtypeStruct((t, d), F32), jax.ShapeDtypeStruct((t, d), BF16),
                   jax.ShapeDtypeStruct((1, d), F32)],
        compiler_params=_params("arbitrary"),
    )(h, dy, gain, dres)


def _mm(a, b, out_dtype, name, res=None):
    m, k = a.shape
    _, n = b.shape
    tm = _pick(m, ROW_TILES)
    out_bytes = jnp.dtype(out_dtype).itemsize + (4 if res is not None else 0)
    fits = lambda tn: 4 * tm * k + 4 * k * tn + 2 * tm * tn * out_bytes <= MM_VMEM_BUDGET
    tn = next(c for c in (n, 2816, 2048, 1536, 1408, 1024, 512, 384, 256, 128) if n % c == 0 and fits(c))

    def body(*refs):
        if res is None:
            a_ref, b_ref, o_ref = refs
        else:
            a_ref, b_ref, r_ref, o_ref = refs
        out = jnp.dot(a_ref[...], b_ref[...], preferred_element_type=F32)
        if res is not None:
            out = out + r_ref[...]
        o_ref[...] = out.astype(o_ref.dtype)

    in_specs = [pl.BlockSpec((tm, k), lambda i, j: (i, 0)), pl.BlockSpec((k, tn), lambda i, j: (0, j))]
    args = [a, b]
    if res is not None:
        in_specs.append(pl.BlockSpec((tm, tn), lambda i, j: (i, j)))
        args.append(res)
    return pl.pallas_call(
        body, grid=(m // tm, n // tn), name=name,
        in_specs=in_specs, out_specs=pl.BlockSpec((tm, tn), lambda i, j: (i, j)),
        out_shape=jax.ShapeDtypeStruct((m, n), out_dtype),
        compiler_params=_params("parallel", "parallel"),
    )(*args)


def _mm_tn(a, g, name):
    t, k = a.shape
    _, n = g.shape
    tk = _pick(k, (1024, 1408, 512))
    tn = _pick(n, (512, 640, 384, 256, 128))
    tt = next(c for c in (3328, 1280) + ROW_TILES
              if t % c == 0 and 4 * c * (tk + tn) + 8 * tk * tn <= MM_VMEM_BUDGET)
    nt = t // tt

    def body(a_ref, g_ref, o_ref):
        @pl.when(pl.program_id(2) == 0)
        def _():
            o_ref[...] = jnp.zeros_like(o_ref)

        o_ref[...] += lax.dot_general(a_ref[...], g_ref[...], (((0,), (0,)), ((), ())),
                                      preferred_element_type=F32)

    return pl.pallas_call(
        body, grid=(k // tk, n // tn, nt), name=name,
        in_specs=[pl.BlockSpec((tt, tk), lambda i, j, s: (s, i)), pl.BlockSpec((tt, tn), lambda i, j, s: (s, j))],
        out_specs=pl.BlockSpec((tk, tn), lambda i, j, s: (i, j)),
        out_shape=jax.ShapeDtypeStruct((k, n), F32),
        compiler_params=_params("parallel", "parallel", "arbitrary"),
    )(a, g)


def _split3_exact(x):
    def top(v):
        return lax.bitcast_convert_type(lax.bitcast_convert_type(v, jnp.int32) & jnp.int32(-65536), F32)

    hi = top(x)
    r1 = x - hi
    mid = top(r1)
    return hi, mid, r1 - mid


def _pair_head(ref, h, rows):
    x = ref[:, 128 * (h // 2):128 * (h // 2) + 128].astype(F32)
    return pltpu.roll(x, HEAD_DIM, axis=1) if h % 2 else x


def _lanes(rows):
    return lax.broadcasted_iota(jnp.int32, (rows, AUG), 1)


def _fox_prep(fq, scal):
    t = fq.shape[0]
    tt = _pick(t, (256, 128))

    def body(q_ref, k_ref, v_ref, s_ref, qa_ref, ka_ref, va_ref):
        lane = _lanes(tt)
        chi, cmid, clo = _split3_exact(s_ref[...])
        ones = lambda lo: jnp.where((lane >= lo) & (lane < lo + 3), 1.0, 0.0)
        for h in range(HEADS):
            col = lambda a: jnp.broadcast_to(a[:, h:h + 1], (tt, AUG))
            c1, c2, c3 = col(chi), col(cmid), col(clo)
            qx = jnp.where(lane == 64, c1, jnp.where(lane == 65, c2, jnp.where(lane == 66, c3, ones(67))))
            kx = jnp.where(lane == 67, -c1, jnp.where(lane == 68, -c2, jnp.where(lane == 69, -c3, ones(64) + ones(70))))
            qa_ref[h] = jnp.where(lane < HEAD_DIM, _pair_head(q_ref, h, tt) * (HEAD_DIM ** -0.5), qx).astype(BF16)
            ka_ref[h] = jnp.where(lane < HEAD_DIM, _pair_head(k_ref, h, tt), kx).astype(BF16)
            va_ref[h] = jnp.where(lane < HEAD_DIM, _pair_head(v_ref, h, tt), ones(64)).astype(BF16)

    out = pl.BlockSpec((HEADS, tt, AUG), lambda i: (0, i, 0))
    shp = jax.ShapeDtypeStruct((HEADS, t, AUG), BF16)
    return pl.pallas_call(
        body, grid=(t // tt,), name="fox_prep",
        in_specs=[pl.BlockSpec((tt, WIDTH), lambda i: (i, 0)), pl.BlockSpec((tt, WIDTH), lambda i: (i, 1)),
                  pl.BlockSpec((tt, WIDTH), lambda i: (i, 2)), pl.BlockSpec((tt, LANES), lambda i: (i, 0))],
        out_specs=[out, out, out], out_shape=[shp, shp, shp],
        compiler_params=_params("parallel"),
    )(fq, fq, fq, scal)


def _fox_post(oa):
    t = oa.shape[1]
    tt = _pick(t, (256, 128))

    def body(o_ref, out_ref):
        out_ref[...] = jnp.concatenate([o_ref[h][:, :HEAD_DIM] for h in range(HEADS)], axis=1).astype(BF16)

    return pl.pallas_call(
        body, grid=(t // tt,), name="fox_post",
        in_specs=[pl.BlockSpec((HEADS, tt, AUG), lambda i: (0, i, 0))],
        out_specs=pl.BlockSpec((tt, WIDTH), lambda i: (i, 0)),
        out_shape=jax.ShapeDtypeStruct((t, WIDTH), BF16),
        compiler_params=_params("parallel"),
    )(oa)


def _fox_bwd_prep(do, oa):
    t = do.shape[0]
    tt = _pick(t, (256, 128))

    def body(d_ref, o_ref, out_ref):
        lane = _lanes(tt)
        for h in range(HEADS):
            x = _pair_head(d_ref, h, tt)
            delta = jnp.sum(jnp.where(lane < HEAD_DIM, x * o_ref[h], 0.0), axis=1, keepdims=True)
            hi, mid, lo = _split3_exact(jnp.broadcast_to(-delta, (tt, AUG)))
            ex = jnp.where(lane == 64, hi, jnp.where(lane == 65, mid, jnp.where(lane == 66, lo, 0.0)))
            out_ref[h] = jnp.where(lane < HEAD_DIM, x, ex).astype(BF16)

    hm = pl.BlockSpec((HEADS, tt, AUG), lambda i: (0, i, 0))
    return pl.pallas_call(
        body, grid=(t // tt,), name="fox_bwd_prep",
        in_specs=[pl.BlockSpec((tt, WIDTH), lambda i: (i, 0)), hm], out_specs=hm,
        out_shape=jax.ShapeDtypeStruct((HEADS, t, AUG), BF16),
        compiler_params=_params("parallel"),
    )(do, oa)


def _fox_bwd_post(dqa, dka, dva):
    t = dqa.shape[1]
    tt = _pick(t, (256, 128))

    def body(dq_ref, dk_ref, dv_ref, out_ref, dsc_ref):
        lane = _lanes(tt)
        heads = lambda ref: jnp.concatenate([ref[h][:, :HEAD_DIM] for h in range(HEADS)], axis=1)
        out_ref[:, 0:WIDTH] = (heads(dq_ref) * (HEAD_DIM ** -0.5)).astype(BF16)
        out_ref[:, WIDTH:2 * WIDTH] = heads(dk_ref).astype(BF16)
        out_ref[:, 2 * WIDTH:] = heads(dv_ref).astype(BF16)
        dsc = jnp.zeros((tt, LANES), F32)
        for h in range(HEADS):
            both = jnp.where(lane == HEAD_DIM, dq_ref[h], 0.0) - jnp.where(lane == HEAD_DIM + 3, dk_ref[h], 0.0)
            dsc = jnp.where(lane == h, jnp.sum(both, axis=1, keepdims=True), dsc)
        dsc_ref[...] = dsc

    hm = pl.BlockSpec((HEADS, tt, AUG), lambda i: (0, i, 0))
    return pl.pallas_call(
        body, grid=(t // tt,), name="fox_bwd_post",
        in_specs=[hm, hm, hm],
        out_specs=[pl.BlockSpec((tt, 3 * WIDTH), lambda i: (i, 0)), pl.BlockSpec((tt, LANES), lambda i: (i, 0))],
        out_shape=[jax.ShapeDtypeStruct((t, 3 * WIDTH), BF16), jax.ShapeDtypeStruct((t, LANES), F32)],
        compiler_params=_params("parallel"),
    )(dqa, dka, dva)


def _fox_fwd(qa, ka, va, tq=None):
    h, t, _ = qa.shape
    tq = tq or _pick(t, ROW_TILES)

    def body(q_ref, k_ref, v_ref, o_ref, qb_ref, s_ref):
        i = pl.program_id(1)
        q = q_ref[...]
        row = lax.broadcasted_iota(jnp.int32, (tq, tq), 0)
        col = lax.broadcasted_iota(jnp.int32, (tq, tq), 1)

        def scores(j, slot):
            kj = k_ref[pl.ds(pl.multiple_of(j * tq, tq), tq), :]
            s_ref[slot] = lax.dot_general(q, kj, (((1,), (1,)), ((), ())), preferred_element_type=F32)

        def update(j, slot, carry, masked):
            m, acc = carry
            vj = v_ref[pl.ds(pl.multiple_of(j * tq, tq), tq), :]
            s = s_ref[slot]
            if masked:
                s = jnp.where(row >= col, s, NEG)
            m_new = jnp.maximum(m, jnp.max(s, axis=-1, keepdims=True))
            p = jnp.exp(s - m_new)
            alpha = jnp.exp(m - m_new)
            return m_new, acc * alpha + jnp.dot(p.astype(BF16), vj, preferred_element_type=F32)

        def pair(jj, carry):
            j = 2 * jj
            scores(j + 1, 1)
            carry = update(j, 0, carry, False)
            scores(j + 2, 0)
            return update(j + 1, 1, carry, False)

        def odd_tail(carry):
            scores(i, 1)
            return update(i, 1, update(i - 1, 0, carry, False), True)

        scores(0, 0)
        carry = (jnp.full((tq, 1), NEG, F32), jnp.zeros((tq, AUG), F32))
        carry = lax.fori_loop(0, i // 2, pair, carry)
        m, acc = lax.cond(i % 2 == 1, odd_tail, lambda c: update(i, 0, c, True), carry)
        lane = lax.broadcasted_iota(jnp.int32, (tq, AUG), 1)
        l = jnp.sum(jnp.where(lane == HEAD_DIM, acc, 0.0), axis=-1, keepdims=True)
        lse = jnp.broadcast_to(m + jnp.log(l), (tq, AUG))
        o_ref[...] = jnp.where(lane < HEAD_DIM, acc / l, lse)
        hi, mid, lo = _split3_exact(-lse)
        qb = jnp.where(lane == 70, hi, jnp.where(lane == 71, mid, jnp.where(lane == 72, lo, q.astype(F32))))
        qb_ref[...] = qb.astype(BF16)

    blk = pl.BlockSpec((None, tq, AUG), lambda hh, i: (hh, i, 0))
    return pl.pallas_call(
        body, grid=(h, t // tq), name="fox_fwd",
        in_specs=[blk, pl.BlockSpec((None, t, AUG), lambda hh, i: (hh, 0, 0)),
                  pl.BlockSpec((None, t, AUG), lambda hh, i: (hh, 0, 0))],
        out_specs=[blk, blk],
        out_shape=[jax.ShapeDtypeStruct((h, t, AUG), F32), jax.ShapeDtypeStruct((h, t, AUG), BF16)],
        scratch_shapes=[pltpu.VMEM((2, tq, tq), F32)],
        compiler_params=_params("parallel", "arbitrary"),
    )(qa, ka, va)


def _fox_bwd(qb, ka, va, doa, tq=None):
    h, t, _ = qb.shape
    tq = tq or _pick(t, ROW_TILES)
    nq = t // tq

    def body(q_ref, k_ref, v_ref, do_ref, dq_ref, dk_ref, dv_ref, st_ref, dpt_ref):
        j = pl.program_id(1)
        n = nq - j

        @pl.when(j == 0)
        def _():
            dq_ref[...] = jnp.zeros_like(dq_ref)

        dk_ref[...] = jnp.zeros_like(dk_ref)
        dv_ref[...] = jnp.zeros_like(dv_ref)
        kj = k_ref[...]
        vj = v_ref[...]
        krow = lax.broadcasted_iota(jnp.int32, (tq, tq), 0)
        qcol = lax.broadcasted_iota(jnp.int32, (tq, tq), 1)
        rows = lambda i: pl.ds(pl.multiple_of(i * tq, tq), tq)

        def scores(i, slot):
            st_ref[slot] = lax.dot_general(kj, q_ref[rows(i), :], (((1,), (1,)), ((), ())),
                                           preferred_element_type=F32)
            dpt_ref[slot] = lax.dot_general(vj, do_ref[rows(i), :], (((1,), (1,)), ((), ())),
                                            preferred_element_type=F32)

        def update(i, slot):
            qi = q_ref[rows(i), :]
            doi = do_ref[rows(i), :]
            pt = jnp.exp(jnp.where((qcol >= krow) | (i > j), st_ref[slot], NEG))
            dst = (pt * dpt_ref[slot]).astype(BF16)
            dv_ref[...] += jnp.dot(pt.astype(BF16), doi, preferred_element_type=F32)
            dk_ref[...] += jnp.dot(dst, qi, preferred_element_type=F32)
            dq_ref[rows(i), :] += lax.dot_general(dst, kj, (((0,), (0,)), ((), ())), preferred_element_type=F32)

        def pair(kk, carry):
            i0 = j + 2 * kk
            scores(i0 + 1, 1)
            update(i0, 0)
            scores(jnp.minimum(i0 + 2, nq - 1), 0)
            update(i0 + 1, 1)
            return carry

        scores(j, 0)
        lax.fori_loop(0, n // 2, pair, 0)

        @pl.when(n % 2 == 1)
        def _():
            update(nq - 1, 0)

    full = pl.BlockSpec((None, t, AUG), lambda hh, j: (hh, 0, 0))
    blk = pl.BlockSpec((None, tq, AUG), lambda hh, j: (hh, j, 0))
    shp = jax.ShapeDtypeStruct((h, t, AUG), F32)
    return pl.pallas_call(
        body, grid=(h, nq), name="fox_bwd",
        in_specs=[full, blk, blk, full], out_specs=[full, blk, blk], out_shape=[shp, shp, shp],
        scratch_shapes=[pltpu.VMEM((2, tq, tq), F32), pltpu.VMEM((2, tq, tq), F32)],
        compiler_params=_params("parallel", "arbitrary"),
    )(qb, ka, va, doa)


def _seg_matrix():
    idx = np.arange(WIDTH) // HEAD_DIM
    return jnp.asarray((idx[:, None] == idx[None, :]).astype(np.float32))


def _segsum(x, e):
    return jnp.dot(x, e, precision=HI, preferred_element_type=F32)


def _silu(x):
    return x * jax.nn.sigmoid(x)


def _silu_grad(x):
    s = jax.nn.sigmoid(x)
    return s * (1.0 + x * (1.0 - s))


def _shift_down(x, prev8, k):
    r = pltpu.roll(x, k, axis=0)
    p = pltpu.roll(prev8, k, axis=0)
    row = lax.broadcasted_iota(jnp.int32, prev8.shape, 0)
    head = jnp.where(row < k, p, r[:8])
    return jnp.concatenate([head, r[8:]], axis=0)


def _shift_up(x, next8, k):
    n = x.shape[0]
    r = pltpu.roll(x, n - k, axis=0)
    p = pltpu.roll(next8, 8 - k, axis=0)
    row = lax.broadcasted_iota(jnp.int32, next8.shape, 0)
    tail = jnp.where(row >= 8 - k, p, r[n - 8:])
    return jnp.concatenate([r[:n - 8], tail], axis=0)


def _causal_conv(x, prev8, w_ref, width, cols=slice(None)):
    y = x * w_ref[width - 1:width, cols]
    for k in range(1, width):
        y = y + _shift_down(x, prev8, k) * w_ref[width - 1 - k:width - k, cols]
    return y


def _causal_conv_bwd(x, prev8, dy, dnext8, w_ref, dw_ref, width, cols=slice(None)):
    dx = dy * w_ref[width - 1:width, cols]
    dw_ref[width - 1:width, cols] += jnp.sum(dy * x, axis=0, keepdims=True)
    for k in range(1, width):
        dx = dx + _shift_up(dy, dnext8, k) * w_ref[width - 1 - k:width - k, cols]
        dw_ref[width - 1 - k:width - k, cols] += jnp.sum(dy * _shift_down(x, prev8, k), axis=0, keepdims=True)
    return dx


HALO = 16


def _prev_spec(tt, width, tile=lambda i: i):
    return pl.BlockSpec((HALO, width), lambda i: (jnp.maximum(tile(i) * (tt // HALO) - 1, 0), 0))


def _prev8(p_ref, cols=slice(None)):
    return p_ref[:, cols].astype(F32)[HALO - 8:]


def _store_heads(ref, x):
    for h in range(HEADS):
        ref[h] = x[:, HEAD_DIM * h:HEAD_DIM * (h + 1)]


def _load_heads(ref):
    return jnp.concatenate([ref[h] for h in range(HEADS)], axis=1)


def _softplus(z):
    return jnp.maximum(z, 0.0) + jnp.log1p(jnp.exp(-jnp.abs(z)))


def _tri_masks(tt):
    r = lax.broadcasted_iota(jnp.int32, (tt, tt), 0)
    c = lax.broadcasted_iota(jnp.int32, (tt, tt), 1)
    same_chunk = lax.shift_right_logical(r, 6) == lax.shift_right_logical(c, 6)
    return r, c, same_chunk


def _gate_fwd(small, pbias, pscale):
    t = small.shape[0]
    tt = _pick(t, (256, 128))

    def body(x_ref, pb_ref, ps_ref, o_ref, carry_ref):
        @pl.when(pl.program_id(0) == 0)
        def _():
            carry_ref[...] = jnp.zeros_like(carry_ref)

        lane = lax.broadcasted_iota(jnp.int32, (tt, LANES), 1)
        z = x_ref[...] + pb_ref[...]
        log_f = jnp.where(lane < HEADS, -_softplus(-z), 0.0)
        g = jnp.where((lane >= 2 * HEADS) & (lane < 3 * HEADS), ps_ref[...] * _softplus(z), 0.0)
        r, c, same_chunk = _tri_masks(tt)
        lower = jnp.where(r >= c, 1.0, 0.0)
        lower_chunk = jnp.where((r >= c) & same_chunk, 1.0, 0.0)
        csum = jnp.dot(lower, log_f, precision=lax.Precision.HIGHEST, preferred_element_type=F32) + carry_ref[...]
        gc = jnp.dot(lower_chunk, g, precision=lax.Precision.HIGHEST, preferred_element_type=F32)
        carry_ref[...] += jnp.sum(log_f, axis=0, keepdims=True)
        o_ref[...] = jnp.where(lane < HEADS, csum, jnp.where(lane < 2 * HEADS, jax.nn.sigmoid(z), gc))

    row = pl.BlockSpec((tt, LANES), lambda i: (i, 0))
    vec = pl.BlockSpec((1, LANES), lambda i: (0, 0))
    return pl.pallas_call(
        body, grid=(t // tt,), name="gate_fwd", in_specs=[row, vec, vec], out_specs=row,
        out_shape=jax.ShapeDtypeStruct((t, LANES), F32),
        scratch_shapes=[pltpu.VMEM((1, LANES), F32)],
        compiler_params=_params("arbitrary"),
    )(small, pbias, pscale)


def _gate_bwd(small, pbias, pscale, dscal):
    t = small.shape[0]
    tt = _pick(t, (256, 128))
    nt = t // tt

    def body(x_ref, pb_ref, ps_ref, d_ref, dx_ref, dpb_ref, dps_ref, carry_ref):
        @pl.when(pl.program_id(0) == 0)
        def _():
            carry_ref[...] = jnp.zeros_like(carry_ref)
            dpb_ref[...] = jnp.zeros_like(dpb_ref)
            dps_ref[...] = jnp.zeros_like(dps_ref)

        lane = lax.broadcasted_iota(jnp.int32, (tt, LANES), 1)
        z = x_ref[...] + pb_ref[...]
        d = d_ref[...]
        dc = jnp.where(lane < HEADS, d, 0.0)
        dbeta = jnp.where((lane >= HEADS) & (lane < 2 * HEADS), d, 0.0)
        dgc = jnp.where((lane >= 2 * HEADS) & (lane < 3 * HEADS), d, 0.0)
        r, c, same_chunk = _tri_masks(tt)
        upper = jnp.where(r <= c, 1.0, 0.0)
        upper_chunk = jnp.where((r <= c) & same_chunk, 1.0, 0.0)
        dlogf = jnp.dot(upper, dc, precision=lax.Precision.HIGHEST, preferred_element_type=F32) + carry_ref[...]
        dg = jnp.dot(upper_chunk, dgc, precision=lax.Precision.HIGHEST, preferred_element_type=F32)
        carry_ref[...] += jnp.sum(dc, axis=0, keepdims=True)
        sg = jax.nn.sigmoid(z)
        dz = dlogf * (1.0 - sg) + dbeta * sg * (1.0 - sg) + dg * ps_ref[...] * sg
        dx_ref[...] = dz.astype(dx_ref.dtype)
        dpb_ref[...] += jnp.sum(dz, axis=0, keepdims=True)
        dps_ref[...] += jnp.sum(dg * _softplus(z), axis=0, keepdims=True)

    row = pl.BlockSpec((tt, LANES), lambda i: (nt - 1 - i, 0))
    vec = pl.BlockSpec((1, LANES), lambda i: (0, 0))
    return pl.pallas_call(
        body, grid=(nt,), name="gate_bwd", in_specs=[row, vec, vec, row], out_specs=[row, vec, vec],
        out_shape=[jax.ShapeDtypeStruct((t, LANES), BF16), jax.ShapeDtypeStruct((1, LANES), F32),
                   jax.ShapeDtypeStruct((1, LANES), F32)],
        scratch_shapes=[pltpu.VMEM((1, LANES), F32)],
        compiler_params=_params("arbitrary"),
    )(small, pbias, pscale, dscal)


def _gdn_pre_fwd(xg, conv_w, seg):
    t = xg.shape[0]
    c3 = 3 * WIDTH
    tt = _pick(t, (320, 256, 128))

    def body(x_ref, p_ref, w_ref, e_ref, q_ref, k_ref, v_ref):
        x = x_ref[...].astype(F32)
        prev = jnp.where(pl.program_id(0) == 0, 0.0, _prev8(p_ref))
        s = _silu(_causal_conv(x, prev, w_ref, GDN_CONV))
        e = e_ref[...]
        q = s[:, :WIDTH]
        k = s[:, WIDTH:2 * WIDTH]
        _store_heads(q_ref, q * lax.rsqrt(_segsum(q * q, e) + RMS_EPS) * (HEAD_DIM ** -0.5))
        _store_heads(k_ref, k * lax.rsqrt(_segsum(k * k, e) + RMS_EPS))
        _store_heads(v_ref, s[:, 2 * WIDTH:])

    out = pl.BlockSpec((HEADS, tt, HEAD_DIM), lambda i: (0, i, 0))
    shp = jax.ShapeDtypeStruct((HEADS, t, HEAD_DIM), F32)
    return pl.pallas_call(
        body, grid=(t // tt,), name="gdn_pre_fwd",
        in_specs=[pl.BlockSpec((tt, c3), lambda i: (i, 0)), _prev_spec(tt, c3),
                  pl.BlockSpec((GDN_CONV, c3), lambda i: (0, 0)), pl.BlockSpec((WIDTH, WIDTH), lambda i: (0, 0))],
        out_specs=[out, out, out], out_shape=[shp, shp, shp],
        compiler_params=_params("arbitrary"),
    )(xg, xg, conv_w, seg)


def _gdn_pre_bwd(xg, conv_w, seg, dqn, dkn, dv):
    t = xg.shape[0]
    c3 = 3 * WIDTH
    tt = _pick(t, (320, 256, 128))
    nt = t // tt

    def body(x_ref, p_ref, w_ref, e_ref, dq_ref, dk_ref, dv_ref, dx_ref, dw_ref, carry_ref):
        step = pl.program_id(0)
        x = x_ref[...].astype(F32)
        e = e_ref[...]
        prev = jnp.where(step == nt - 1, 0.0, _prev8(p_ref))
        y = _causal_conv(x, prev, w_ref, GDN_CONV)
        s = _silu(y)
        q = s[:, :WIDTH]
        k = s[:, WIDTH:2 * WIDTH]
        rq = lax.rsqrt(_segsum(q * q, e) + RMS_EPS)
        rk = lax.rsqrt(_segsum(k * k, e) + RMS_EPS)
        gq = _load_heads(dq_ref) * (HEAD_DIM ** -0.5)
        gk = _load_heads(dk_ref)
        dq = rq * gq - q * (rq * rq * rq) * _segsum(gq * q, e)
        dk = rk * gk - k * (rk * rk * rk) * _segsum(gk * k, e)
        dy = jnp.concatenate([dq, dk, _load_heads(dv_ref)], axis=1) * _silu_grad(y)

        @pl.when(step == 0)
        def _():
            carry_ref[...] = jnp.zeros_like(carry_ref)
            dw_ref[...] = jnp.zeros_like(dw_ref)

        dx = _causal_conv_bwd(x, prev, dy, carry_ref[...], w_ref, dw_ref, GDN_CONV)
        dx_ref[...] = dx.astype(dx_ref.dtype)
        carry_ref[...] = dy[:8]

    rev = lambda i: (nt - 1 - i, 0)
    blk = pl.BlockSpec((HEADS, tt, HEAD_DIM), lambda i: (0, nt - 1 - i, 0))
    return pl.pallas_call(
        body, grid=(nt,), name="gdn_pre_bwd",
        in_specs=[pl.BlockSpec((tt, c3), rev), _prev_spec(tt, c3, lambda i: nt - 1 - i),
                  pl.BlockSpec((GDN_CONV, c3), lambda i: (0, 0)), pl.BlockSpec((WIDTH, WIDTH), lambda i: (0, 0)),
                  blk, blk, blk],
        out_specs=[pl.BlockSpec((tt, c3), rev), pl.BlockSpec((GDN_CONV, c3), lambda i: (0, 0))],
        out_shape=[jax.ShapeDtypeStruct((t, c3), BF16), jax.ShapeDtypeStruct((GDN_CONV, c3), F32)],
        scratch_shapes=[pltpu.VMEM((8, c3), F32)],
        compiler_params=_params("arbitrary"),
    )(xg, xg, conv_w, seg, dqn, dkn, dv)


def _bmm(a, b, ca, cb, precision=None):
    return lax.dot_general(a, b, (((ca,), (cb,)), ((0,), (0,))), precision=precision, preferred_element_type=F32)


def _bf(x):
    return x.astype(BF16)


def _tri_inverse(a, eye):
    x = -a
    tinv = eye + x
    pw = x
    for _ in range(5):
        pb = _bf(pw)
        pw = _bmm(pb, pb, 2, 1)
        tinv = tinv + _bmm(_bf(tinv), _bf(pw), 2, 1)
    resid = eye - _bmm(eye + a, tinv, 2, 1, precision=HI)
    return tinv + _bmm(_bf(tinv), _bf(resid), 2, 1)


def _gdn_intra(q, k, v, bc, gcc, gcr):
    ii = lax.broadcasted_iota(jnp.int32, (CHUNK, CHUNK), 0)
    jj = lax.broadcasted_iota(jnp.int32, (CHUNK, CHUNK), 1)
    tril = (ii >= jj)[None]
    strict = (ii > jj)[None]
    eye = jnp.where(ii == jj, 1.0, 0.0).astype(F32)[None]
    last = (ii == CHUNK - 1)[None]
    dm = jnp.exp(jnp.where(tril, gcc - gcr, NEG))
    gam = jnp.exp(gcc)
    kb = k * bc
    vb = v * bc
    kk = _bmm(_bf(kb), _bf(k), 2, 2)
    a = jnp.where(strict, kk * dm, 0.0)
    tinv = _tri_inverse(a, eye)
    kbg = kb * gam
    u = _bmm(tinv, vb, 2, 1, precision=HI)
    wk = _bmm(tinv, kbg, 2, 1, precision=HI)
    qk = _bmm(_bf(q), _bf(k), 2, 2)
    p = jnp.where(tril, qk * dm, 0.0)
    gl = jnp.sum(jnp.where(last, gcc, 0.0), axis=1, keepdims=True)
    edec = jnp.exp(gl - gcc)
    return dict(tril=tril, strict=strict, dm=dm, gam=gam, kb=kb, kk=kk, a=a, tinv=tinv, u=u, wk=wk, qk=qk, p=p,
                qg=q * gam, kt=k * edec, edec=edec, gaml=jnp.exp(gl), last=last)


def _gate_tiles(sc, gct, nb):
    rows = nb * CHUNK
    cols = lambda lane0: jnp.stack([jnp.broadcast_to(sc[:, lane0 + h:lane0 + h + 1], (rows, HEAD_DIM))
                                    for h in range(HEADS)], axis=0).reshape(HEADS * nb, CHUNK, HEAD_DIM)
    gcr = jnp.stack([jnp.broadcast_to(gct[h:h + 1, n * CHUNK:(n + 1) * CHUNK], (CHUNK, CHUNK))
                     for h in range(HEADS) for n in range(nb)], axis=0)
    return cols(HEADS), cols(2 * HEADS), gcr


def _gdn_fwd(q, k, v, scal, gct, nb=None):
    h, t, dh = q.shape
    nc = t // CHUNK
    nb = nb or _pick(nc, (4, 2))
    bsz = h * nb

    def body(q_ref, k_ref, v_ref, sc_ref, gt_ref, o_ref, s0_ref, state_ref):
        @pl.when(pl.program_id(0) == 0)
        def _():
            state_ref[...] = jnp.zeros_like(state_ref)

        ld = lambda r: r[...].reshape(bsz, CHUNK, dh)
        bc, gcc, gcr = _gate_tiles(sc_ref[...], gt_ref[...], nb)
        z = _gdn_intra(ld(q_ref), ld(k_ref), ld(v_ref), bc, gcc, gcr)
        per = lambda x: x.reshape((h, nb) + x.shape[1:])
        u, wk, p, qg, kt, gaml = (per(z[n]) for n in ("u", "wk", "p", "qg", "kt", "gaml"))
        s = state_ref[...]
        for n in range(nb):
            s0_ref[:, n] = s
            sb = _bf(s)
            vn = u[:, n] - _bmm(_bf(wk[:, n]), sb, 2, 1)
            o_ref[:, n * CHUNK:(n + 1) * CHUNK, :] = _bmm(_bf(qg[:, n]), sb, 2, 1) + _bmm(_bf(p[:, n]), _bf(vn), 2, 1)
            s = s * gaml[:, n] + _bmm(_bf(kt[:, n]), _bf(vn), 1, 1)
        state_ref[...] = s

    blk = pl.BlockSpec((h, nb * CHUNK, dh), lambda i: (0, i, 0))
    return pl.pallas_call(
        body, grid=(nc // nb,), name="gdn_fwd",
        in_specs=[blk] * 3 + [pl.BlockSpec((nb * CHUNK, LANES), lambda i: (i, 0)),
                              pl.BlockSpec((h, nb * CHUNK), lambda i: (0, i))],
        out_specs=[blk, pl.BlockSpec((h, nb, dh, dh), lambda i: (0, i, 0, 0))],
        out_shape=[jax.ShapeDtypeStruct((h, t, dh), F32), jax.ShapeDtypeStruct((h, nc, dh, dh), F32)],
        scratch_shapes=[pltpu.VMEM((h, dh, dh), F32)],
        compiler_params=_params("arbitrary"),
    )(q, k, v, scal, gct)


def _gdn_bwd(q, k, v, scal, gct, s0s, do, nb=None):
    h, t, dh = q.shape
    nc = t // CHUNK
    nb = nb or _pick(nc, (2,))
    bsz = h * nb
    ng = nc // nb
    rows = nb * CHUNK

    def body(q_ref, k_ref, v_ref, sc_ref, gt_ref, s0_ref, do_ref,
             dq_ref, dk_ref, dv_ref, dsc_ref, dgt_ref, ds_ref):
        @pl.when(pl.program_id(0) == 0)
        def _():
            ds_ref[...] = jnp.zeros_like(ds_ref)

        ld = lambda r: r[...].reshape(bsz, CHUNK, dh)
        q, k, v = ld(q_ref), ld(k_ref), ld(v_ref)
        bc, gcc, gcr = _gate_tiles(sc_ref[...], gt_ref[...], nb)
        z = _gdn_intra(q, k, v, bc, gcc, gcr)
        per = lambda x: x.reshape((h, nb) + x.shape[1:])
        u, wk, p, qg, kt, gaml = (per(z[n]) for n in ("u", "wk", "p", "qg", "kt", "gaml"))
        dout = per(ld(do_ref))
        ds = ds_ref[...]
        d_u, d_wk, d_p, d_qg, d_kt, d_gaml = ([None] * nb for _ in range(6))
        for n in reversed(range(nb)):
            s0 = s0_ref[:, n]
            s0b, dsb, dob = _bf(s0), _bf(ds), _bf(dout[:, n])
            wkb, qgb = _bf(wk[:, n]), _bf(qg[:, n])
            vn = u[:, n] - _bmm(wkb, s0b, 2, 1)
            dvn = _bmm(_bf(p[:, n]), dob, 1, 1) + _bmm(_bf(kt[:, n]), dsb, 2, 1)
            dvnb = _bf(dvn)
            d_u[n] = dvn
            d_p[n] = _bmm(dob, _bf(vn), 2, 2)
            d_qg[n] = _bmm(dob, s0b, 2, 2)
            d_kt[n] = _bmm(_bf(vn), dsb, 2, 2)
            d_gaml[n] = jnp.sum(s0 * ds, axis=1, keepdims=True)
            d_wk[n] = -_bmm(dvnb, s0b, 2, 2)
            ds = _bmm(qgb, dob, 1, 1) + gaml[:, n] * ds - _bmm(wkb, dvnb, 1, 1)
        ds_ref[...] = ds

        flat = lambda xs: jnp.stack(xs, axis=1).reshape((bsz,) + xs[0].shape[1:])
        d_u, d_wk, d_p, d_qg, d_kt, d_gaml = (flat(x) for x in (d_u, d_wk, d_p, d_qg, d_kt, d_gaml))
        tinv, gam, kb, dm = z["tinv"], z["gam"], z["kb"], z["dm"]
        drv = _bmm(tinv, d_u, 1, 1, precision=HI)
        drk = _bmm(tinv, d_wk, 1, 1, precision=HI)
        da = -(_bmm(_bf(drv), _bf(z["u"]), 2, 2) + _bmm(_bf(drk), _bf(z["wk"]), 2, 2))
        da = jnp.where(z["strict"], da, 0.0)
        d_p = jnp.where(z["tril"], d_p, 0.0)
        dkk = _bf(da * dm)
        dqk = _bf(d_p * dm)
        dkb = _bmm(dkk, _bf(k), 2, 1) + drk * gam
        dk = _bmm(dkk, _bf(kb), 1, 1) + _bmm(dqk, _bf(q), 1, 1) + dkb * bc + d_kt * z["edec"]
        dq = _bmm(dqk, _bf(k), 2, 1) + d_qg * gam
        mm = da * z["a"] + d_p * z["p"]
        dkt_kt = d_kt * z["kt"]
        dgl = jnp.sum(dkt_kt, axis=1, keepdims=True) + d_gaml * z["gaml"]
        dgc = mm + d_qg * z["qg"] + drk * kb * gam - dkt_kt + jnp.where(z["last"], dgl, 0.0)
        dq_ref[...] = dq.reshape(h, rows, dh)
        dk_ref[...] = dk.reshape(h, rows, dh)
        dv_ref[...] = (drv * bc).reshape(h, rows, dh)
        dbeta = (dkb * k + drv * v).reshape(h, rows, dh)
        dgc = dgc.reshape(h, rows, dh)
        lane = lax.broadcasted_iota(jnp.int32, (rows, LANES), 1)
        dsc = jnp.zeros((rows, LANES), F32)
        for hh in range(h):
            dsc = jnp.where(lane == HEADS + hh, jnp.sum(dbeta[hh], axis=1, keepdims=True), dsc)
            dsc = jnp.where(lane == 2 * HEADS + hh, jnp.sum(dgc[hh], axis=1, keepdims=True), dsc)
        dsc_ref[...] = dsc
        dgr = -jnp.sum(mm, axis=1, keepdims=True)
        for hh in range(h):
            for n in range(nb):
                dgt_ref[hh:hh + 1, n * CHUNK:(n + 1) * CHUNK] = dgr[hh * nb + n]

    blk = pl.BlockSpec((h, rows, dh), lambda i: (0, ng - 1 - i, 0))
    shp = jax.ShapeDtypeStruct((h, t, dh), F32)
    sc_spec = pl.BlockSpec((rows, LANES), lambda i: (ng - 1 - i, 0))
    gt_spec = pl.BlockSpec((h, rows), lambda i: (0, ng - 1 - i))
    return pl.pallas_call(
        body, grid=(ng,), name="gdn_bwd",
        in_specs=[blk] * 3 + [sc_spec, gt_spec, pl.BlockSpec((h, nb, dh, dh), lambda i: (0, ng - 1 - i, 0, 0)), blk],
        out_specs=[blk] * 3 + [sc_spec, gt_spec],
        out_shape=[shp] * 3 + [jax.ShapeDtypeStruct((t, LANES), F32), jax.ShapeDtypeStruct((h, t), F32)],
        scratch_shapes=[pltpu.VMEM((h, dh, dh), F32)],
        compiler_params=_params("arbitrary"),
    )(q, k, v, scal, gct, s0s, do)


def _gdn_post_fwd(o, xg, gain, seg):
    t = o.shape[1]
    tt = _pick(t, (320, 256, 128))

    def body(o_ref, z_ref, g_ref, e_ref, y_ref):
        x = _load_heads(o_ref)
        r = lax.rsqrt(_segsum(x * x, e_ref[...]) * (1.0 / HEAD_DIM) + RMS_EPS)
        y_ref[...] = (x * r * g_ref[...] * _silu(z_ref[...].astype(F32))).astype(y_ref.dtype)

    return pl.pallas_call(
        body, grid=(t // tt,), name="gdn_post_fwd",
        in_specs=[pl.BlockSpec((HEADS, tt, HEAD_DIM), lambda i: (0, i, 0)), pl.BlockSpec((tt, WIDTH), lambda i: (i, 3)),
                  pl.BlockSpec((1, WIDTH), lambda i: (0, 0)), pl.BlockSpec((WIDTH, WIDTH), lambda i: (0, 0))],
        out_specs=pl.BlockSpec((tt, WIDTH), lambda i: (i, 0)),
        out_shape=jax.ShapeDtypeStruct((t, WIDTH), BF16),
        compiler_params=_params("arbitrary"),
    )(o, xg, gain, seg)


def _gdn_post_bwd(o, xg, gain, seg, dy):
    t = o.shape[1]
    tt = _pick(t, (320, 256, 128))

    def body(o_ref, z_ref, g_ref, e_ref, dy_ref, do_ref, dz_ref, dg_ref):
        x = _load_heads(o_ref)
        zz = z_ref[...].astype(F32)
        e = e_ref[...]
        gain_v = g_ref[...]
        d = dy_ref[...]
        r = lax.rsqrt(_segsum(x * x, e) * (1.0 / HEAD_DIM) + RMS_EPS)
        xr = x * r
        don = d * _silu(zz)
        dz_ref[...] = (d * xr * gain_v * _silu_grad(zz)).astype(dz_ref.dtype)
        gy = don * gain_v
        _store_heads(do_ref, r * gy - xr * (r * r) * (_segsum(gy * x, e) * (1.0 / HEAD_DIM)))

        @pl.when(pl.program_id(0) == 0)
        def _():
            dg_ref[...] = jnp.zeros_like(dg_ref)

        dg_ref[...] += jnp.sum(don * xr, axis=0, keepdims=True)

    row = pl.BlockSpec((tt, WIDTH), lambda i: (i, 0))
    vec = pl.BlockSpec((1, WIDTH), lambda i: (0, 0))
    hm = pl.BlockSpec((HEADS, tt, HEAD_DIM), lambda i: (0, i, 0))
    return pl.pallas_call(
        body, grid=(t // tt,), name="gdn_post_bwd",
        in_specs=[hm, pl.BlockSpec((tt, WIDTH), lambda i: (i, 3)), vec,
                  pl.BlockSpec((WIDTH, WIDTH), lambda i: (0, 0)), row],
        out_specs=[hm, row, vec],
        out_shape=[jax.ShapeDtypeStruct((HEADS, t, HEAD_DIM), F32), jax.ShapeDtypeStruct((t, WIDTH), BF16),
                   jax.ShapeDtypeStruct((1, WIDTH), F32)],
        compiler_params=_params("arbitrary"),
    )(o, xg, gain, seg, dy)


def _mix_fwd(yf, yg, gates, bias):
    t, d = yf.shape
    tt = _pick(t, (320, 256, 128))

    def body(yf_ref, yg_ref, g1_ref, g2_ref, b1_ref, b2_ref, o_ref):
        g1 = jax.nn.sigmoid(g1_ref[...].astype(F32) + b1_ref[...])
        g2 = jax.nn.sigmoid(g2_ref[...].astype(F32) + b2_ref[...])
        o_ref[...] = (g1 * yf_ref[...] + g2 * yg_ref[...]).astype(o_ref.dtype)

    row = pl.BlockSpec((tt, d), lambda i: (i, 0))
    return pl.pallas_call(
        body, grid=(t // tt,), name="mix_fwd",
        in_specs=[row, row, row, pl.BlockSpec((tt, d), lambda i: (i, 1)),
                  pl.BlockSpec((1, d), lambda i: (0, 0)), pl.BlockSpec((1, d), lambda i: (0, 1))],
        out_specs=row, out_shape=jax.ShapeDtypeStruct((t, d), BF16),
        compiler_params=_params("arbitrary"),
    )(yf, yg, gates, gates, bias, bias)


def _mix_bwd(dmix, yf, yg, gates, bias):
    t, d = yf.shape
    tt = _pick(t, (320, 256, 128))

    def body(dm_ref, yf_ref, yg_ref, g1_ref, g2_ref, b1_ref, b2_ref, dyf_ref, dyg_ref, dg_ref, db_ref):
        dm = dm_ref[...]
        g1 = jax.nn.sigmoid(g1_ref[...].astype(F32) + b1_ref[...])
        g2 = jax.nn.sigmoid(g2_ref[...].astype(F32) + b2_ref[...])
        dyf_ref[...] = (dm * g1).astype(BF16)
        dyg_ref[...] = (dm * g2).astype(BF16)
        dgate = jnp.concatenate([dm * yf_ref[...] * g1 * (1.0 - g1), dm * yg_ref[...] * g2 * (1.0 - g2)], axis=1)
        dg_ref[...] = dgate.astype(BF16)

        @pl.when(pl.program_id(0) == 0)
        def _():
            db_ref[...] = jnp.zeros_like(db_ref)

        db_ref[...] += jnp.sum(dgate, axis=0, keepdims=True)

    row = pl.BlockSpec((tt, d), lambda i: (i, 0))
    wide = pl.BlockSpec((tt, 2 * d), lambda i: (i, 0))
    return pl.pallas_call(
        body, grid=(t // tt,), name="mix_bwd",
        in_specs=[row, row, row, row, pl.BlockSpec((tt, d), lambda i: (i, 1)),
                  pl.BlockSpec((1, d), lambda i: (0, 0)), pl.BlockSpec((1, d), lambda i: (0, 1))],
        out_specs=[row, row, wide, pl.BlockSpec((1, 2 * d), lambda i: (0, 0))],
        out_shape=[jax.ShapeDtypeStruct((t, d), BF16), jax.ShapeDtypeStruct((t, d), BF16),
                   jax.ShapeDtypeStruct((t, 2 * d), BF16), jax.ShapeDtypeStruct((1, 2 * d), F32)],
        compiler_params=_params("arbitrary"),
    )(dmix, yf, yg, gates, gates, bias, bias)


def _ffn_act_fwd(up, conv_w, conv_b):
    t, c = up.shape
    tt = 128

    def body(x_ref, p_ref, w_ref, b_ref, o_ref):
        first = pl.program_id(0) == 0

        def conv(cols):
            prev = jnp.where(first, 0.0, _prev8(p_ref, cols))
            return _causal_conv(x_ref[:, cols].astype(F32), prev, w_ref, FFN_CONV, cols) + b_ref[:, cols]

        for lo in range(0, D_FF, FFN_LANES):
            gate = conv(slice(lo, lo + FFN_LANES))
            val = conv(slice(D_FF + lo, D_FF + lo + FFN_LANES))
            o_ref[:, lo:lo + FFN_LANES] = (_silu(gate) * val).astype(o_ref.dtype)

    return pl.pallas_call(
        body, grid=(t // tt,), name="ffn_act_fwd",
        in_specs=[pl.BlockSpec((tt, c), lambda i: (i, 0)), _prev_spec(tt, c),
                  pl.BlockSpec((FFN_CONV, c), lambda i: (0, 0)), pl.BlockSpec((1, c), lambda i: (0, 0))],
        out_specs=pl.BlockSpec((tt, D_FF), lambda i: (i, 0)),
        out_shape=jax.ShapeDtypeStruct((t, D_FF), BF16),
        compiler_params=_params("arbitrary"),
    )(up, up, conv_w, conv_b)


def _ffn_act_bwd(up, conv_w, conv_b, dact):
    t, c = up.shape
    tt = 128
    nt = t // tt

    def body(x_ref, p_ref, w_ref, b_ref, da_ref, dx_ref, dw_ref, db_ref, carry_ref):
        step = pl.program_id(0)

        @pl.when(step == 0)
        def _():
            carry_ref[...] = jnp.zeros_like(carry_ref)
            dw_ref[...] = jnp.zeros_like(dw_ref)
            db_ref[...] = jnp.zeros_like(db_ref)

        def conv(cols):
            x = x_ref[:, cols].astype(F32)
            prev = jnp.where(step == nt - 1, 0.0, _prev8(p_ref, cols))
            return x, prev, _causal_conv(x, prev, w_ref, FFN_CONV, cols) + b_ref[:, cols]

        def back(cols, x, prev, du):
            dx = _causal_conv_bwd(x, prev, du, carry_ref[:, cols], w_ref, dw_ref, FFN_CONV, cols)
            dx_ref[:, cols] = dx.astype(dx_ref.dtype)
            db_ref[:, cols] += jnp.sum(du, axis=0, keepdims=True)
            carry_ref[:, cols] = du[:8]

        for lo in range(0, D_FF, FFN_LANES):
            gcols, vcols = slice(lo, lo + FFN_LANES), slice(D_FF + lo, D_FF + lo + FFN_LANES)
            xg, pg, gate = conv(gcols)
            xv, pv, val = conv(vcols)
            da = da_ref[:, gcols]
            back(gcols, xg, pg, da * val * _silu_grad(gate))
            back(vcols, xv, pv, da * _silu(gate))

    rev = lambda i: (nt - 1 - i, 0)
    return pl.pallas_call(
        body, grid=(nt,), name="ffn_act_bwd",
        in_specs=[pl.BlockSpec((tt, c), rev),
                  _prev_spec(tt, c, lambda i: nt - 1 - i),
                  pl.BlockSpec((FFN_CONV, c), lambda i: (0, 0)), pl.BlockSpec((1, c), lambda i: (0, 0)),
                  pl.BlockSpec((tt, D_FF), rev)],
        out_specs=[pl.BlockSpec((tt, c), rev), pl.BlockSpec((FFN_CONV, c), lambda i: (0, 0)),
                   pl.BlockSpec((1, c), lambda i: (0, 0))],
        out_shape=[jax.ShapeDtypeStruct((t, c), BF16), jax.ShapeDtypeStruct((FFN_CONV, c), F32),
                   jax.ShapeDtypeStruct((1, c), F32)],
        scratch_shapes=[pltpu.VMEM((8, c), F32)],
        compiler_params=_params("arbitrary"),
    )(up, up, conv_w, conv_b, dact)


def _final_loss(h2, target, gain, seq):
    t, d = h2.shape
    tr = _pick(t, (320, 256, 128))

    def body(h_ref, t_ref, g_ref, loss_ref, dh_ref, dhb_ref, dg_ref):
        i = pl.program_id(0)
        x = h_ref[...]
        gain_v = g_ref[...]
        r = lax.rsqrt(jnp.mean(x * x, axis=-1, keepdims=True) + RMS_EPS)
        xr = x * r
        rows = i * tr + lax.broadcasted_iota(jnp.int32, (tr, 1), 0)
        real = (rows >= N_META) & (rows < N_META + seq)
        err = jnp.where(real, xr * gain_v - t_ref[...], 0.0)
        dy = err * (1.0 / d)
        gy = dy * gain_v
        dh = r * (gy - xr * jnp.mean(gy * xr, axis=-1, keepdims=True))
        dh_ref[...] = dh
        dhb_ref[...] = dh.astype(BF16)

        @pl.when(i == 0)
        def _():
            loss_ref[...] = jnp.zeros_like(loss_ref)
            dg_ref[...] = jnp.zeros_like(dg_ref)

        part = jnp.sum(jnp.sum(err * err, axis=-1, keepdims=True), axis=0, keepdims=True)
        loss_ref[...] += jnp.broadcast_to(part * (0.5 / d), loss_ref.shape)
        dg_ref[...] += jnp.sum(dy * xr, axis=0, keepdims=True)

    row = pl.BlockSpec((tr, d), lambda i: (i, 0))
    vec = pl.BlockSpec((1, d), lambda i: (0, 0))
    return pl.pallas_call(
        body, grid=(t // tr,), name="final_loss",
        in_specs=[row, row, vec],
        out_specs=[pl.BlockSpec((1, LANES), lambda i: (0, 0)), row, row, vec],
        out_shape=[jax.ShapeDtypeStruct((1, LANES), F32), jax.ShapeDtypeStruct((t, d), F32),
                   jax.ShapeDtypeStruct((t, d), BF16), jax.ShapeDtypeStruct((1, d), F32)],
        compiler_params=_params("arbitrary"),
    )(h2, target, gain)


ADAM_TILE_BYTES = 1 << 20


def _adamw(w, m, v, grecv, name):
    r, cols = w.shape
    tr = r
    if r * cols * 4 > ADAM_TILE_BYTES:
        tr = max(d for d in range(8, r + 1, 8) if r % d == 0 and d * cols * 4 <= ADAM_TILE_BYTES)

    def body(w_ref, m_ref, v_ref, g_ref, go_ref, d_ref, mo_ref, vo_ref):
        g = g_ref[0].astype(F32)
        for s in range(1, N_DEV):
            g = g + g_ref[s].astype(F32)
        wv = w_ref[...]
        mn = ADAM_B1 * m_ref[...] + (1.0 - ADAM_B1) * g
        vn = ADAM_B2 * v_ref[...] + (1.0 - ADAM_B2) * (g * g)
        m_hat = mn / (1.0 - ADAM_B1 ** ADAM_STEP)
        v_hat = vn / (1.0 - ADAM_B2 ** ADAM_STEP)
        go_ref[...] = g
        d_ref[...] = -ADAM_LR * (m_hat / (jnp.sqrt(v_hat) + ADAM_EPS) + ADAM_WD * wv)
        mo_ref[...] = mn
        vo_ref[...] = vn

    row = pl.BlockSpec((tr, cols), lambda i: (i, 0))
    shp = jax.ShapeDtypeStruct((r, cols), F32)
    return pl.pallas_call(
        body, grid=(r // tr,), name=name,
        in_specs=[row, row, row, pl.BlockSpec((N_DEV, tr, cols), lambda i: (0, i, 0))],
        out_specs=[row] * 4, out_shape=[shp] * 4,
        compiler_params=_params("parallel"),
    )(w, m, v, grecv)


def _mesh_pos():
    return lax.axis_index("x"), lax.axis_index("y"), lax.axis_index("c")


def _all_gather(shards):
    n = len(shards)

    def body(*refs):
        x_refs, out_refs = refs[:n], refs[n:2 * n]
        send_sems, recv_sems, local_sems = refs[2 * n:]
        x, y, c = _mesh_pos()
        me, sibling = (x, y, c), (x, y, 1 - c)
        chips = [(1 - x, y), (x, 1 - y), (1 - x, 1 - y)]

        def slot(a, px, py, pc):
            return out_refs[a].at[4 * px + 2 * py + pc]

        def copy(a, kk, block, to, src=None):
            return pltpu.make_async_remote_copy(
                src_ref=slot(a, *block) if src is None else src, dst_ref=slot(a, *block),
                send_sem=send_sems.at[7 * a + kk], recv_sem=recv_sems.at[7 * a + kk],
                device_id=to, device_id_type=MESH_ID)

        mine = [pltpu.make_async_copy(x_refs[a], slot(a, *me), local_sems.at[a]) for a in range(n)]
        first = []
        for a in range(n):
            first.append(copy(a, 0, me, sibling, src=x_refs[a]))
            first += [copy(a, 1 + j, me, (*chip, c), src=x_refs[a]) for j, chip in enumerate(chips)]
        for cp in mine + first:
            cp.start()
        passed = []
        for j, chip in enumerate(chips):
            for a in range(n):
                copy(a, 1 + j, (*chip, c), me).wait_recv()
                passed.append(copy(a, 4 + j, (*chip, c), sibling))
                passed[-1].start()
        for a in range(n):
            copy(a, 0, sibling, me).wait_recv()
        for j, chip in enumerate(chips):
            for a in range(n):
                copy(a, 4 + j, (*chip, 1 - c), me).wait_recv()
        for cp in first + passed:
            cp.wait_send()
        for cp in mine:
            cp.wait()

    hbm = pl.BlockSpec(memory_space=pl.ANY)
    return pl.pallas_call(
        body, name="weight_all_gather", in_specs=[hbm] * n, out_specs=[hbm] * n,
        out_shape=[jax.ShapeDtypeStruct((N_DEV,) + s.shape, s.dtype) for s in shards],
        scratch_shapes=[pltpu.SemaphoreType.DMA((7 * n,)), pltpu.SemaphoreType.DMA((7 * n,)),
                        pltpu.SemaphoreType.DMA((n,))],
    )(*shards)


def _grad_exchange(blocks, small):
    n = len(blocks)

    def body(*refs):
        src_refs, dst_refs = refs[:n + 1], refs[n + 1:2 * n + 2]
        send_sems, recv_sems, local_sems = refs[2 * n + 2:]
        x, y, c = _mesh_pos()
        me = 4 * x + 2 * y + c
        copies = []
        for kk in range(1, N_DEV):
            px = 1 - x if kk & 4 else x
            py = 1 - y if kk & 2 else y
            pc = 1 - c if kk & 1 else c
            peer = 4 * px + 2 * py + pc
            for a in range(n + 1):
                copies.append(pltpu.make_async_remote_copy(
                    src_ref=src_refs[a].at[peer] if a < n else src_refs[a], dst_ref=dst_refs[a].at[me],
                    send_sem=send_sems.at[7 * a + kk - 1], recv_sem=recv_sems.at[7 * a + kk - 1],
                    device_id=(px, py, pc), device_id_type=MESH_ID))
        own = [pltpu.make_async_copy(src_refs[a].at[me] if a < n else src_refs[a], dst_refs[a].at[me],
                                     local_sems.at[a]) for a in range(n + 1)]
        for cp in own + copies:
            cp.start()
        for cp in copies + own:
            cp.wait()

    hbm = pl.BlockSpec(memory_space=pl.ANY)
    return pl.pallas_call(
        body, name="grad_exchange", in_specs=[hbm] * (n + 1), out_specs=[hbm] * (n + 1),
        out_shape=[jax.ShapeDtypeStruct(b.shape, b.dtype) for b in blocks]
        + [jax.ShapeDtypeStruct((N_DEV,) + small.shape, small.dtype)],
        scratch_shapes=[pltpu.SemaphoreType.DMA((7 * (n + 1),)), pltpu.SemaphoreType.DMA((7 * (n + 1),)),
                        pltpu.SemaphoreType.DMA((n + 1,))],
    )(*blocks, small)


def _pad_flat(parts, rows):
    flat = jnp.concatenate([p.reshape(-1) for p in parts])
    return jnp.pad(flat, (0, rows * LANES - flat.shape[0])).reshape(rows, LANES)


def _rows_for(n_elems, mult=1024):
    rows = -(-n_elems // LANES)
    return -(-rows // mult) * mult


SHARDED = ("meta_tokens", "w_in", "gdn_conv_w", "w_branch_fox", "w_branch_gdn", "w_out", "ffn_w_up", "ffn_conv_w",
           "ffn_w_down")
MATMUL = ("w_in", "w_branch_fox", "w_branch_gdn", "w_out", "ffn_w_up", "ffn_w_down")
EXACT = ("meta_tokens", "gdn_conv_w", "ffn_conv_w")
REPLICATED = ("fgt_bias", "gdn_a_log", "gdn_dt_bias", "gdn_norm_w", "gate_bias", "norm_mix_w", "norm_ffn_w",
              "ffn_conv_b", "norm_final_w")
WEIGHTS = ("meta_tokens", "w_in", "fgt_bias", "gdn_conv_w", "gdn_a_log", "gdn_dt_bias", "gdn_norm_w", "gate_bias",
           "w_branch_fox", "w_branch_gdn", "w_out", "norm_mix_w", "norm_ffn_w", "ffn_w_up", "ffn_conv_w",
           "ffn_conv_b", "ffn_w_down", "norm_final_w")


def _unpack(buf, shapes):
    flat = buf.reshape(-1)
    out, off = [], 0
    for s in shapes:
        n = int(np.prod(s))
        out.append(flat[off:off + n].reshape(s))
        off += n
    return out


def _unpack_gathered(buf, shapes):
    flat = buf.reshape(N_DEV, -1)
    out, off = [], 0
    for s in shapes:
        n = int(np.prod(s))
        out.append(flat[:, off:off + n].reshape((N_DEV,) + tuple(s)))
        off += n
    return out


def _cat_cols(g):
    return g.transpose(1, 0, 2).reshape(g.shape[1], -1)


def _col_blocks(full, width):
    return full.reshape(full.shape[0], N_DEV, width).transpose(1, 0, 2)


def _local_step(x, target, w):
    seq = x.shape[0]
    t = _padded_tokens(seq)
    pad = t - N_META - seq
    seg = _seg_matrix()
    zrows = jnp.zeros((pad, D_MODEL), F32)
    h0 = jnp.concatenate([w["meta_tokens"], x, zrows], axis=0)
    tgt = jnp.concatenate([jnp.zeros((N_META, D_MODEL), F32), target, zrows], axis=0)

    w_in = w["w_in"]
    o_f, o_g, o_z, o_b, o_a, o_gate = 1536, 1544, 3080, 3592, 3600, 3608
    w_small = jnp.concatenate([w_in[:, o_f:o_f + 8], w_in[:, o_b:o_b + 8], w_in[:, o_a:o_a + 8],
                               jnp.zeros((D_MODEL, LANES - 24), BF16)], axis=1)
    w_r = jnp.concatenate([w_in[:, :1536], w_in[:, o_g:o_z], w_in[:, o_z:o_b], w_in[:, o_gate:], w_small], axis=1)

    a1 = _rmsnorm_fwd(h0, w["norm_mix_w"])
    fq = _mm(a1, w_r[:, :1536], BF16, "proj_fox")
    xg = _mm(a1, w_r[:, 1536:3584], BF16, "proj_gdn")
    gt = _mm(a1, w_r[:, 3584:5632], BF16, "proj_gates")
    sm = _mm(a1, w_r[:, 5632:], F32, "proj_small")

    lanes_pad = lambda a, lo: jnp.pad(a, ((0, 0), (lo, LANES - lo - a.shape[1])))
    neg_exp_a = -jnp.exp(w["gdn_a_log"])
    pbias = lanes_pad(w["fgt_bias"], 0) + lanes_pad(w["gdn_dt_bias"], 2 * HEADS)
    pscale = lanes_pad(neg_exp_a, 2 * HEADS)
    scal = _gate_fwd(sm, pbias, pscale)
    gct = scal[:, 2 * HEADS:3 * HEADS].T

    qa, ka, va = _fox_prep(fq, scal)
    oa, qb = _fox_fwd(qa, ka, va)
    o_fox = _fox_post(oa)

    qh, kh, vh = _gdn_pre_fwd(xg, w["gdn_conv_w"], seg)
    og, s0s = _gdn_fwd(qh, kh, vh, scal, gct)
    norm_w = jnp.tile(w["gdn_norm_w"], (1, HEADS))
    ogn = _gdn_post_fwd(og, xg, norm_w, seg)

    yf = _mm(o_fox, w["w_branch_fox"], F32, "branch_fox")
    yg = _mm(ogn, w["w_branch_gdn"], F32, "branch_gdn")
    mix = _mix_fwd(yf, yg, gt, w["gate_bias"])
    h1 = _mm(mix, w["w_out"], F32, "out_proj", res=h0)
    a2 = _rmsnorm_fwd(h1, w["norm_ffn_w"])
    up = _mm(a2, w["ffn_w_up"], BF16, "ffn_up")
    act = _ffn_act_fwd(up, w["ffn_conv_w"], w["ffn_conv_b"])
    h2 = _mm(act, w["ffn_w_down"], F32, "ffn_down", res=h1)
    loss, dh2, dh2b, g_final = _final_loss(h2, tgt, w["norm_final_w"].reshape(1, D_MODEL), seq)

    grads = {"norm_final_w": g_final.reshape(D_MODEL)}
    grads["ffn_w_down"] = _mm_tn(act, dh2b, "wgrad_ffn_down")
    dact = _mm(dh2b, w["ffn_w_down"].T, F32, "dgrad_ffn_down")
    dup, g_cw, g_cb = _ffn_act_bwd(up, w["ffn_conv_w"], w["ffn_conv_b"], dact)
    grads["ffn_conv_w"], grads["ffn_conv_b"] = g_cw, g_cb
    grads["ffn_w_up"] = _mm_tn(a2, dup, "wgrad_ffn_up")
    da2 = _mm(dup, w["ffn_w_up"].T, F32, "dgrad_ffn_up")
    dh1, dh1b, grads["norm_ffn_w"] = _rmsnorm_bwd(h1, da2, w["norm_ffn_w"], dh2)
    grads["w_out"] = _mm_tn(mix, dh1b, "wgrad_out")
    dmix = _mm(dh1b, w["w_out"].T, F32, "dgrad_out")
    dyf, dyg, dgt, grads["gate_bias"] = _mix_bwd(dmix, yf, yg, gt, w["gate_bias"])
    grads["w_branch_fox"] = _mm_tn(o_fox, dyf, "wgrad_branch_fox")
    grads["w_branch_gdn"] = _mm_tn(ogn, dyg, "wgrad_branch_gdn")
    do_fox = _mm(dyf, w["w_branch_fox"].T, F32, "dgrad_branch_fox")
    dogn = _mm(dyg, w["w_branch_gdn"].T, F32, "dgrad_branch_gdn")

    dog, dz, g_nw = _gdn_post_bwd(og, xg, norm_w, seg, dogn)
    grads["gdn_norm_w"] = g_nw.reshape(HEADS, HEAD_DIM).sum(axis=0)[None]
    dqh, dkh, dvh, dscal_g, dgct = _gdn_bwd(qh, kh, vh, scal, gct, s0s, dog)
    dxg, grads["gdn_conv_w"] = _gdn_pre_bwd(xg, w["gdn_conv_w"], seg, dqh, dkh, dvh)

    dqa, dka, dva = _fox_bwd(qb, ka, va, _fox_bwd_prep(do_fox, oa))
    dfq, dscal_c = _fox_bwd_post(dqa, dka, dva)

    dscal = dscal_c + dscal_g + lanes_pad(dgct.T, 2 * HEADS)
    dsm, dpb, dps = _gate_bwd(sm, pbias, pscale, dscal)
    grads["fgt_bias"] = dpb[:, :HEADS]
    grads["gdn_dt_bias"] = dpb[:, 2 * HEADS:3 * HEADS]
    grads["gdn_a_log"] = dps[:, 2 * HEADS:3 * HEADS] * neg_exp_a

    dproj = jnp.concatenate([dfq, dxg, dz, dgt, dsm], axis=1)
    g_r = _mm_tn(a1, dproj, "wgrad_in")
    grads["w_in"] = jnp.concatenate([g_r[:, :1536], g_r[:, 5632:5640], g_r[:, 1536:3072], g_r[:, 3072:3584],
                                     g_r[:, 5640:5648], g_r[:, 5648:5656], g_r[:, 3584:5632]], axis=1)
    da1 = _mm(dproj, w_r.T, F32, "dgrad_in")
    dh0, _, grads["norm_mix_w"] = _rmsnorm_bwd(h0, da1, w["norm_mix_w"], dh1)
    grads["meta_tokens"] = dh0[:N_META]
    return loss, dh0[N_META:N_META + seq], grads


def _shard_pieces(arrs):
    return [arrs[n][0] if arrs[n].ndim == 3 else arrs[n] for n in SHARDED]


def _full_grad_blocks(grads):
    g = grads
    cols = lambda a, wd: _col_blocks(a, wd)
    rows = lambda a: a.reshape(N_DEV, a.shape[0] // N_DEV, a.shape[1])
    per = [cols(g["meta_tokens"], 128), cols(g["w_in"], IN_WIDTH // N_DEV), cols(g["gdn_conv_w"], 3 * WIDTH // N_DEV),
           cols(g["w_branch_fox"], D_MODEL // N_DEV), cols(g["w_branch_gdn"], D_MODEL // N_DEV), rows(g["w_out"]),
           cols(g["ffn_w_up"], 2 * D_FF // N_DEV), cols(g["ffn_conv_w"], 2 * D_FF // N_DEV), rows(g["ffn_w_down"])]
    return per


def kernel(x, meta_tokens, w_in, fgt_bias, gdn_conv_w, gdn_a_log, gdn_dt_bias, gdn_norm_w, gate_bias, w_branch_fox, w_branch_gdn, w_out, norm_mix_w, norm_ffn_w, ffn_w_up, ffn_conv_w, ffn_conv_b, ffn_w_down, norm_final_w, loss_target, m_meta_tokens, m_w_in, m_fgt_bias, m_gdn_conv_w, m_gdn_a_log, m_gdn_dt_bias, m_gdn_norm_w, m_gate_bias, m_w_branch_fox, m_w_branch_gdn, m_w_out, m_norm_mix_w, m_norm_ffn_w, m_ffn_w_up, m_ffn_conv_w, m_ffn_conv_b, m_ffn_w_down, m_norm_final_w, v_meta_tokens, v_w_in, v_fgt_bias, v_gdn_conv_w, v_gdn_a_log, v_gdn_dt_bias, v_gdn_norm_w, v_gate_bias, v_w_branch_fox, v_w_branch_gdn, v_w_out, v_norm_mix_w, v_norm_ffn_w, v_ffn_w_up, v_ffn_conv_w, v_ffn_conv_b, v_ffn_w_down, v_norm_final_w):
    wts = dict(meta_tokens=meta_tokens, w_in=w_in, fgt_bias=fgt_bias, gdn_conv_w=gdn_conv_w, gdn_a_log=gdn_a_log,
               gdn_dt_bias=gdn_dt_bias, gdn_norm_w=gdn_norm_w, gate_bias=gate_bias, w_branch_fox=w_branch_fox,
               w_branch_gdn=w_branch_gdn, w_out=w_out, norm_mix_w=norm_mix_w, norm_ffn_w=norm_ffn_w,
               ffn_w_up=ffn_w_up, ffn_conv_w=ffn_conv_w, ffn_conv_b=ffn_conv_b, ffn_w_down=ffn_w_down,
               norm_final_w=norm_final_w)
    mom = dict(meta_tokens=m_meta_tokens, w_in=m_w_in, fgt_bias=m_fgt_bias, gdn_conv_w=m_gdn_conv_w,
               gdn_a_log=m_gdn_a_log, gdn_dt_bias=m_gdn_dt_bias, gdn_norm_w=m_gdn_norm_w, gate_bias=m_gate_bias,
               w_branch_fox=m_w_branch_fox, w_branch_gdn=m_w_branch_gdn, w_out=m_w_out, norm_mix_w=m_norm_mix_w,
               norm_ffn_w=m_norm_ffn_w, ffn_w_up=m_ffn_w_up, ffn_conv_w=m_ffn_conv_w, ffn_conv_b=m_ffn_conv_b,
               ffn_w_down=m_ffn_w_down, norm_final_w=m_norm_final_w)
    var = dict(meta_tokens=v_meta_tokens, w_in=v_w_in, fgt_bias=v_fgt_bias, gdn_conv_w=v_gdn_conv_w,
               gdn_a_log=v_gdn_a_log, gdn_dt_bias=v_gdn_dt_bias, gdn_norm_w=v_gdn_norm_w, gate_bias=v_gate_bias,
               w_branch_fox=v_w_branch_fox, w_branch_gdn=v_w_branch_gdn, w_out=v_w_out, norm_mix_w=v_norm_mix_w,
               norm_ffn_w=v_norm_ffn_w, ffn_w_up=v_ffn_w_up, ffn_conv_w=v_ffn_conv_w, ffn_conv_b=v_ffn_conv_b,
               ffn_w_down=v_ffn_w_down, norm_final_w=v_norm_final_w)

    sh = dict(zip(SHARDED, _shard_pieces(wts)))
    exact_shapes = [sh[n].shape for n in EXACT]
    rows_exact = _rows_for(sum(int(np.prod(s)) for s in exact_shapes), 8)
    gathered = _all_gather([sh[n].astype(BF16) for n in MATMUL] + [_pad_flat([sh[n] for n in EXACT], rows_exact)])
    g_in, g_bf, g_bg, g_out, g_up, g_down = gathered[:6]
    meta_full, conv_full, fconv_full = (_cat_cols(a) for a in _unpack_gathered(gathered[6], exact_shapes))
    full = dict(
        meta_tokens=meta_full, w_in=_cat_cols(g_in), gdn_conv_w=conv_full,
        w_branch_fox=_cat_cols(g_bf), w_branch_gdn=_cat_cols(g_bg), w_out=g_out.reshape(D_MODEL, D_MODEL),
        ffn_w_up=_cat_cols(g_up), ffn_conv_w=fconv_full, ffn_w_down=g_down.reshape(D_FF, D_MODEL),
        fgt_bias=fgt_bias, gdn_a_log=gdn_a_log, gdn_dt_bias=gdn_dt_bias, gdn_norm_w=gdn_norm_w, gate_bias=gate_bias,
        norm_mix_w=norm_mix_w, norm_ffn_w=norm_ffn_w, ffn_conv_b=ffn_conv_b, norm_final_w=norm_final_w)

    loss, grad_x, grads = _local_step(x[0], loss_target[0], full)

    blocks = [b.astype(BF16) for b in _full_grad_blocks(grads)]
    rep_parts = [grads[n] for n in REPLICATED] + [loss[:, :1]]
    rep_shapes = [wts[n].shape for n in REPLICATED]
    rows_small = _rows_for(sum(int(np.prod(p.shape)) for p in rep_parts), 8)
    received = _grad_exchange(blocks, _pad_flat(rep_parts, rows_small))

    result = {}
    kinds = ("grad", "delta", "new_m", "new_v")
    for n, recv in zip(SHARDED, received[:-1]):
        outs = _adamw(sh[n], _shard_pieces(mom)[SHARDED.index(n)], _shard_pieces(var)[SHARDED.index(n)], recv,
                      "adamw_" + n)
        for kind, a in zip(kinds, outs):
            result[kind, n] = a.reshape(wts[n].shape)
    rep_w = _pad_flat([wts[n] for n in REPLICATED] + [jnp.zeros((1, 1), F32)], rows_small)
    rep_m = _pad_flat([mom[n] for n in REPLICATED] + [jnp.zeros((1, 1), F32)], rows_small)
    rep_v = _pad_flat([var[n] for n in REPLICATED] + [jnp.ones((1, 1), F32)], rows_small)
    outs_r = _adamw(rep_w, rep_m, rep_v, received[-1], "adamw_replicated")
    for kind, br in zip(kinds, outs_r):
        for n, a in zip(REPLICATED, _unpack(br, rep_shapes)):
            result[kind, n] = a
    n_rep = sum(int(np.prod(s)) for s in rep_shapes)
    total_loss = outs_r[0].reshape(-1)[n_rep]
    out = [total_loss, grad_x[None]]
    for kind in ("grad", "delta", "new_m", "new_v"):
        out += [result[kind, n] for n in WEIGHTS]
    return tuple(out)
```

```python
import functools

import jax
import jax.numpy as jnp
import numpy as np
from jax import lax
from jax.experimental import pallas as pl
from jax.experimental.pallas import tpu as pltpu

F32 = jnp.float32
BF16 = jnp.bfloat16

D_MODEL = 1024
N_META = 16
HEADS = 8
HEAD_DIM = 64
WIDTH = HEADS * HEAD_DIM
CHUNK = 64
GDN_CONV = 4
D_FF = 2816
FFN_CONV = 3
IN_WIDTH = 5656
IN_PAD = 5760
RMS_EPS = 1e-6
NEG = -1e30
AUG = 128
N_DEV = 8
LANES = 128

ADAM_LR = 0.001
ADAM_B1 = 0.9
ADAM_B2 = 0.999
ADAM_EPS = 1e-08
ADAM_WD = 0.01
ADAM_STEP = 10

VMEM_LIMIT = 56 * 1024 * 1024
MM_VMEM_BUDGET = 36 * 1024 * 1024
FFN_LANES = 128
HI = lax.Precision.HIGH
MESH_ID = pl.DeviceIdType.MESH


def _pick(n, cands):
    for c in cands:
        if n % c == 0:
            return c
    raise ValueError(f"no tile for {n} in {cands}")


def _params(*sem):
    return pltpu.CompilerParams(dimension_semantics=sem if sem else None, vmem_limit_bytes=VMEM_LIMIT)


def _padded_tokens(seq):
    t = -(-(N_META + seq) // 128) * 128
    if t > 1280 and t % 640:
        t = -(-t // 640) * 640
    return t


ROW_TILES = (640, 512, 384, 256, 128)


def _rmsnorm_fwd(h, gain):
    t, d = h.shape
    tr = _pick(t, ROW_TILES)

    def body(h_ref, g_ref, o_ref):
        x = h_ref[...]
        r = lax.rsqrt(jnp.mean(x * x, axis=-1, keepdims=True) + RMS_EPS)
        o_ref[...] = (x * r * g_ref[...]).astype(o_ref.dtype)

    return pl.pallas_call(
        body, grid=(t // tr,), name="rmsnorm_fwd",
        in_specs=[pl.BlockSpec((tr, d), lambda i: (i, 0)), pl.BlockSpec((1, d), lambda i: (0, 0))],
        out_specs=pl.BlockSpec((tr, d), lambda i: (i, 0)),
        out_shape=jax.ShapeDtypeStruct((t, d), BF16),
        compiler_params=_params("arbitrary"),
    )(h, gain)


def _rmsnorm_bwd(h, dy, gain, dres):
    t, d = h.shape
    tr = _pick(t, (320, 256, 128))

    def body(h_ref, dy_ref, g_ref, dres_ref, dh_ref, dhb_ref, dg_ref):
        x = h_ref[...]
        dyv = dy_ref[...]
        r = lax.rsqrt(jnp.mean(x * x, axis=-1, keepdims=True) + RMS_EPS)
        gy = dyv * g_ref[...]
        m = jnp.mean(gy * x, axis=-1, keepdims=True)
        dh = dres_ref[...] + r * gy - x * (r * r * r * m)
        dh_ref[...] = dh
        dhb_ref[...] = dh.astype(BF16)

        @pl.when(pl.program_id(0) == 0)
        def _():
            dg_ref[...] = jnp.zeros_like(dg_ref)

        dg_ref[...] += jnp.sum(dyv * x * r, axis=0, keepdims=True)

    row = pl.BlockSpec((tr, d), lambda i: (i, 0))
    vec = pl.BlockSpec((1, d), lambda i: (0, 0))
    return pl.pallas_call(
        body, grid=(t // tr,), name="rmsnorm_bwd",
        in_specs=[row, row, vec, row], out_specs=[row, row, vec],
        out_shape=[jax.ShapeDtypeStruct((t, d), F32), jax.ShapeDtypeStruct((t, d), BF16),
                   jax.ShapeDtypeStruct((1, d), F32)],
        compiler_params=_params("arbitrary"),
    )(h, dy, gain, dres)


def _mm(a, b, out_dtype, name, res=None):
    m, k = a.shape
    _, n = b.shape
    tm = _pick(m, ROW_TILES)
    out_bytes = jnp.dtype(out_dtype).itemsize + (4 if res is not None else 0)
    fits = lambda tn: 4 * tm * k + 4 * k * tn + 2 * tm * tn * out_bytes <= MM_VMEM_BUDGET
    tn = next(c for c in (n, 2816, 2048, 1536, 1408, 1024, 512, 384, 256, 128) if n % c == 0 and fits(c))

    def body(*refs):
        if res is None:
            a_ref, b_ref, o_ref = refs
        else:
            a_ref, b_ref, r_ref, o_ref = refs
        out = jnp.dot(a_ref[...], b_ref[...], preferred_element_type=F32)
        if res is not None:
            out = out + r_ref[...]
        o_ref[...] = out.astype(o_ref.dtype)

    in_specs = [pl.BlockSpec((tm, k), lambda i, j: (i, 0)), pl.BlockSpec((k, tn), lambda i, j: (0, j))]
    args = [a, b]
    if res is not None:
        in_specs.append(pl.BlockSpec((tm, tn), lambda i, j: (i, j)))
        args.append(res)
    return pl.pallas_call(
        body, grid=(m // tm, n // tn), name=name,
        in_specs=in_specs, out_specs=pl.BlockSpec((tm, tn), lambda i, j: (i, j)),
        out_shape=jax.ShapeDtypeStruct((m, n), out_dtype),
        compiler_params=_params("parallel", "parallel"),
    )(*args)


def _mm_tn(a, g, name):
    t, k = a.shape
    _, n = g.shape
    tk = _pick(k, (1024, 1408, 512))
    tn = _pick(n, (512, 640, 384, 256, 128))
    tt = next(c for c in (3328, 1280) + ROW_TILES
              if t % c == 0 and 4 * c * (tk + tn) + 8 * tk * tn <= MM_VMEM_BUDGET)
    nt = t // tt

    def body(a_ref, g_ref, o_ref):
        @pl.when(pl.program_id(2) == 0)
        def _():
            o_ref[...] = jnp.zeros_like(o_ref)

        o_ref[...] += lax.dot_general(a_ref[...], g_ref[...], (((0,), (0,)), ((), ())),
                                      preferred_element_type=F32)

    return pl.pallas_call(
        body, grid=(k // tk, n // tn, nt), name=name,
        in_specs=[pl.BlockSpec((tt, tk), lambda i, j, s: (s, i)), pl.BlockSpec((tt, tn), lambda i, j, s: (s, j))],
        out_specs=pl.BlockSpec((tk, tn), lambda i, j, s: (i, j)),
        out_shape=jax.ShapeDtypeStruct((k, n), F32),
        compiler_params=_params("parallel", "parallel", "arbitrary"),
    )(a, g)


def _split3_exact(x):
    def top(v):
        return lax.bitcast_convert_type(lax.bitcast_convert_type(v, jnp.int32) & jnp.int32(-65536), F32)

    hi = top(x)
    r1 = x - hi
    mid = top(r1)
    return hi, mid, r1 - mid


def _pair_head(ref, h, rows):
    x = ref[:, 128 * (h // 2):128 * (h // 2) + 128].astype(F32)
    return pltpu.roll(x, HEAD_DIM, axis=1) if h % 2 else x


def _lanes(rows):
    return lax.broadcasted_iota(jnp.int32, (rows, AUG), 1)


def _fox_prep(fq, scal):
    t = fq.shape[0]
    tt = _pick(t, (256, 128))

    def body(q_ref, k_ref, v_ref, s_ref, qa_ref, ka_ref, va_ref, kt_ref, vt_ref):
        lane = _lanes(tt)
        chi, cmid, clo = _split3_exact(s_ref[...])
        ones = lambda lo: jnp.where((lane >= lo) & (lane < lo + 3), 1.0, 0.0)
        for h in range(HEADS):
            col = lambda a: jnp.broadcast_to(a[:, h:h + 1], (tt, AUG))
            c1, c2, c3 = col(chi), col(cmid), col(clo)
            qx = jnp.where(lane == 64, c1, jnp.where(lane == 65, c2, jnp.where(lane == 66, c3, ones(67))))
            kx = jnp.where(lane == 67, -c1, jnp.where(lane == 68, -c2, jnp.where(lane == 69, -c3, ones(64) + ones(70))))
            qa_ref[h] = jnp.where(lane < HEAD_DIM, _pair_head(q_ref, h, tt) * (HEAD_DIM ** -0.5), qx).astype(BF16)
            k_aug = jnp.where(lane < HEAD_DIM, _pair_head(k_ref, h, tt), kx)
            ka_ref[h] = k_aug.astype(BF16)
            kt_ref[h] = k_aug.T.astype(BF16)
            v_aug = jnp.where(lane < HEAD_DIM, _pair_head(v_ref, h, tt), ones(64))
            va_ref[h] = v_aug.astype(BF16)
            vt_ref[h] = v_aug.T.astype(BF16)

    out = pl.BlockSpec((HEADS, tt, AUG), lambda i: (0, i, 0))
    out_t = pl.BlockSpec((HEADS, AUG, tt), lambda i: (0, 0, i))
    shp = jax.ShapeDtypeStruct((HEADS, t, AUG), BF16)
    shp_t = jax.ShapeDtypeStruct((HEADS, AUG, t), BF16)
    return pl.pallas_call(
        body, grid=(t // tt,), name="fox_prep",
        in_specs=[pl.BlockSpec((tt, WIDTH), lambda i: (i, 0)), pl.BlockSpec((tt, WIDTH), lambda i: (i, 1)),
                  pl.BlockSpec((tt, WIDTH), lambda i: (i, 2)), pl.BlockSpec((tt, LANES), lambda i: (i, 0))],
        out_specs=[out, out, out, out_t, out_t], out_shape=[shp, shp, shp, shp_t, shp_t],
        compiler_params=_params("parallel"),
    )(fq, fq, fq, scal)


def _fox_post(oa):
    t = oa.shape[1]
    tt = _pick(t, (256, 128))

    def body(o_ref, out_ref):
        out_ref[...] = jnp.concatenate([o_ref[h][:, :HEAD_DIM] for h in range(HEADS)], axis=1).astype(BF16)

    return pl.pallas_call(
        body, grid=(t // tt,), name="fox_post",
        in_specs=[pl.BlockSpec((HEADS, tt, AUG), lambda i: (0, i, 0))],
        out_specs=pl.BlockSpec((tt, WIDTH), lambda i: (i, 0)),
        out_shape=jax.ShapeDtypeStruct((t, WIDTH), BF16),
        compiler_params=_params("parallel"),
    )(oa)


def _fox_bwd_prep(do, oa):
    t = do.shape[0]
    tt = _pick(t, (256, 128))

    def body(d_ref, o_ref, out_ref, outt_ref):
        lane = _lanes(tt)
        for h in range(HEADS):
            x = _pair_head(d_ref, h, tt)
            delta = jnp.sum(jnp.where(lane < HEAD_DIM, x * o_ref[h], 0.0), axis=1, keepdims=True)
            hi, mid, lo = _split3_exact(jnp.broadcast_to(-delta, (tt, AUG)))
            ex = jnp.where(lane == 64, hi, jnp.where(lane == 65, mid, jnp.where(lane == 66, lo, 0.0)))
            do_aug = jnp.where(lane < HEAD_DIM, x, ex)
            out_ref[h] = do_aug.astype(BF16)
            outt_ref[h] = do_aug.T.astype(BF16)

    hm = pl.BlockSpec((HEADS, tt, AUG), lambda i: (0, i, 0))
    return pl.pallas_call(
        body, grid=(t // tt,), name="fox_bwd_prep",
        in_specs=[pl.BlockSpec((tt, WIDTH), lambda i: (i, 0)), hm],
        out_specs=[hm, pl.BlockSpec((HEADS, AUG, tt), lambda i: (0, 0, i))],
        out_shape=[jax.ShapeDtypeStruct((HEADS, t, AUG), BF16), jax.ShapeDtypeStruct((HEADS, AUG, t), BF16)],
        compiler_params=_params("parallel"),
    )(do, oa)


def _fox_bwd_post(dqt, dkt, dvt):
    t = dqt.shape[2]
    tt = _pick(t, (256, 128))

    def body(dq_ref, dk_ref, dv_ref, out_ref, dsc_ref):
        lane = _lanes(tt)
        dqs = [dq_ref[h].T for h in range(HEADS)]
        dks = [dk_ref[h].T for h in range(HEADS)]
        heads = lambda xs: jnp.concatenate([x[:, :HEAD_DIM] for x in xs], axis=1)
        out_ref[:, 0:WIDTH] = (heads(dqs) * (HEAD_DIM ** -0.5)).astype(BF16)
        out_ref[:, WIDTH:2 * WIDTH] = heads(dks).astype(BF16)
        out_ref[:, 2 * WIDTH:] = heads([dv_ref[h].T for h in range(HEADS)]).astype(BF16)
        dsc = jnp.zeros((tt, LANES), F32)
        for h in range(HEADS):
            both = jnp.where(lane == HEAD_DIM, dqs[h], 0.0) - jnp.where(lane == HEAD_DIM + 3, dks[h], 0.0)
            dsc = jnp.where(lane == h, jnp.sum(both, axis=1, keepdims=True), dsc)
        dsc_ref[...] = dsc

    hm = pl.BlockSpec((HEADS, AUG, tt), lambda i: (0, 0, i))
    return pl.pallas_call(
        body, grid=(t // tt,), name="fox_bwd_post",
        in_specs=[hm, hm, hm],
        out_specs=[pl.BlockSpec((tt, 3 * WIDTH), lambda i: (i, 0)), pl.BlockSpec((tt, LANES), lambda i: (i, 0))],
        out_shape=[jax.ShapeDtypeStruct((t, 3 * WIDTH), BF16), jax.ShapeDtypeStruct((t, LANES), F32)],
        compiler_params=_params("parallel"),
    )(dqt, dkt, dvt)


def _fox_fwd(qa, ka, vat, tq=None):
    h, t, _ = qa.shape
    tq = tq or _pick(t, ROW_TILES)

    def body(q_ref, k_ref, vt_ref, o_ref, qb_ref, qbt_ref, s_ref):
        i = pl.program_id(1)
        q = q_ref[...]
        krow = lax.broadcasted_iota(jnp.int32, (tq, tq), 0)
        qcol = lax.broadcasted_iota(jnp.int32, (tq, tq), 1)
        rows = lambda j: pl.ds(pl.multiple_of(j * tq, tq), tq)

        def scores(j, slot):
            s_ref[slot] = lax.dot_general(k_ref[rows(j), :], q, (((1,), (1,)), ((), ())), preferred_element_type=F32)

        def update(j, slot, carry, masked):
            m, acc = carry
            s = s_ref[slot]
            if masked:
                s = jnp.where(qcol >= krow, s, NEG)
            m_new = jnp.maximum(m, jnp.max(s, axis=0, keepdims=True))
            p = jnp.exp(s - m_new)
            alpha = jnp.exp(m - m_new)
            return m_new, acc * alpha + jnp.dot(vt_ref[:, rows(j)], p.astype(BF16), preferred_element_type=F32)

        def pair(jj, carry):
            j = 2 * jj
            scores(j + 1, 1)
            carry = update(j, 0, carry, False)
            scores(j + 2, 0)
            return update(j + 1, 1, carry, False)

        def odd_tail(carry):
            scores(i, 1)
            return update(i, 1, update(i - 1, 0, carry, False), True)

        scores(0, 0)
        carry = (jnp.full((1, tq), NEG, F32), jnp.zeros((AUG, tq), F32))
        carry = lax.fori_loop(0, i // 2, pair, carry)
        m, acc = lax.cond(i % 2 == 1, odd_tail, lambda c: update(i, 0, c, True), carry)
        sub = lax.broadcasted_iota(jnp.int32, (AUG, tq), 0)
        l = jnp.sum(jnp.where(sub == HEAD_DIM, acc, 0.0), axis=0, keepdims=True)
        out = jnp.where(sub < HEAD_DIM, acc / l, m + jnp.log(l)).T
        o_ref[...] = out
        lane = lax.broadcasted_iota(jnp.int32, (tq, AUG), 1)
        lse = jnp.broadcast_to(jnp.sum(jnp.where(lane == HEAD_DIM, out, 0.0), axis=1, keepdims=True), (tq, AUG))
        hi, mid, lo = _split3_exact(-lse)
        qb = jnp.where(lane == 70, hi, jnp.where(lane == 71, mid, jnp.where(lane == 72, lo, q.astype(F32))))
        qb_ref[...] = qb.astype(BF16)
        qbt_ref[...] = qb.T.astype(BF16)

    blk = pl.BlockSpec((None, tq, AUG), lambda hh, i: (hh, i, 0))
    return pl.pallas_call(
        body, grid=(h, t // tq), name="fox_fwd",
        in_specs=[blk, pl.BlockSpec((None, t, AUG), lambda hh, i: (hh, 0, 0)),
                  pl.BlockSpec((None, AUG, t), lambda hh, i: (hh, 0, 0))],
        out_specs=[blk, blk, pl.BlockSpec((None, AUG, tq), lambda hh, i: (hh, 0, i))],
        out_shape=[jax.ShapeDtypeStruct((h, t, AUG), F32), jax.ShapeDtypeStruct((h, t, AUG), BF16),
                   jax.ShapeDtypeStruct((h, AUG, t), BF16)],
        scratch_shapes=[pltpu.VMEM((2, tq, tq), F32)],
        compiler_params=_params("parallel", "arbitrary"),
    )(qa, ka, vat)


def _fox_bwd(qb, qbt, ka, kat, va, doa, doat, tq=None):
    h, t, _ = qb.shape
    tq = tq or _pick(t, ROW_TILES)
    nq = t // tq

    def body(q_ref, qt_ref, k_ref, kt_ref, v_ref, do_ref, dot_ref, dqt_ref, dkt_ref, dvt_ref, s_ref, dp_ref):
        j = pl.program_id(1)
        n = nq - j

        @pl.when(j == 0)
        def _():
            dqt_ref[...] = jnp.zeros_like(dqt_ref)

        dkt_ref[...] = jnp.zeros_like(dkt_ref)
        dvt_ref[...] = jnp.zeros_like(dvt_ref)
        kj = k_ref[...]
        ktj = kt_ref[...]
        vj = v_ref[...]
        qrow = lax.broadcasted_iota(jnp.int32, (tq, tq), 0)
        kcol = lax.broadcasted_iota(jnp.int32, (tq, tq), 1)
        rows = lambda i: pl.ds(pl.multiple_of(i * tq, tq), tq)
        nt_dims = (((1,), (1,)), ((), ()))

        def scores(i, slot):
            s_ref[slot] = lax.dot_general(q_ref[rows(i), :], kj, nt_dims, preferred_element_type=F32)
            dp_ref[slot] = lax.dot_general(do_ref[rows(i), :], vj, nt_dims, preferred_element_type=F32)

        def update(i, slot):
            p = jnp.exp(jnp.where((qrow >= kcol) | (i > j), s_ref[slot], NEG))
            ds = (p * dp_ref[slot]).astype(BF16)
            dvt_ref[...] += jnp.dot(dot_ref[:, rows(i)], p.astype(BF16), preferred_element_type=F32)
            dkt_ref[...] += jnp.dot(qt_ref[:, rows(i)], ds, preferred_element_type=F32)
            dqt_ref[:, rows(i)] += lax.dot_general(ktj, ds, nt_dims, preferred_element_type=F32)

        def pair(kk, carry):
            i0 = j + 2 * kk
            scores(i0 + 1, 1)
            update(i0, 0)
            scores(jnp.minimum(i0 + 2, nq - 1), 0)
            update(i0 + 1, 1)
            return carry

        scores(j, 0)
        lax.fori_loop(0, n // 2, pair, 0)

        @pl.when(n % 2 == 1)
        def _():
            update(nq - 1, 0)

    once = pl.Buffered(1)
    full = pl.BlockSpec((None, t, AUG), lambda hh, j: (hh, 0, 0), pipeline_mode=once)
    full_t = pl.BlockSpec((None, AUG, t), lambda hh, j: (hh, 0, 0), pipeline_mode=once)
    blk = pl.BlockSpec((None, tq, AUG), lambda hh, j: (hh, j, 0))
    blk_t = pl.BlockSpec((None, AUG, tq), lambda hh, j: (hh, 0, j))
    shp = jax.ShapeDtypeStruct((h, AUG, t), F32)
    return pl.pallas_call(
        body, grid=(h, nq), name="fox_bwd",
        in_specs=[full, full_t, blk, blk_t, blk, full, full_t],
        out_specs=[pl.BlockSpec((None, AUG, t), lambda hh, j: (hh, 0, 0)), blk_t, blk_t], out_shape=[shp, shp, shp],
        scratch_shapes=[pltpu.VMEM((2, tq, tq), F32), pltpu.VMEM((2, tq, tq), F32)],
        compiler_params=_params("parallel", "arbitrary"),
    )(qb, qbt, ka, kat, va, doa, doat)


def _seg_matrix():
    idx = np.arange(WIDTH) // HEAD_DIM
    return jnp.asarray((idx[:, None] == idx[None, :]).astype(np.float32))


def _segsum(x, e):
    return jnp.dot(x, e, precision=HI, preferred_element_type=F32)


def _silu(x):
    return x * jax.nn.sigmoid(x)


def _silu_grad(x):
    s = jax.nn.sigmoid(x)
    return s * (1.0 + x * (1.0 - s))


def _shift_down(x, prev8, k):
    r = pltpu.roll(x, k, axis=0)
    p = pltpu.roll(prev8, k, axis=0)
    row = lax.broadcasted_iota(jnp.int32, prev8.shape, 0)
    head = jnp.where(row < k, p, r[:8])
    return jnp.concatenate([head, r[8:]], axis=0)


def _shift_up(x, next8, k):
    n = x.shape[0]
    r = pltpu.roll(x, n - k, axis=0)
    p = pltpu.roll(next8, 8 - k, axis=0)
    row = lax.broadcasted_iota(jnp.int32, next8.shape, 0)
    tail = jnp.where(row >= 8 - k, p, r[n - 8:])
    return jnp.concatenate([r[:n - 8], tail], axis=0)


def _causal_conv(x, prev8, w_ref, width, cols=slice(None)):
    y = x * w_ref[width - 1:width, cols]
    for k in range(1, width):
        y = y + _shift_down(x, prev8, k) * w_ref[width - 1 - k:width - k, cols]
    return y


def _causal_conv_bwd(x, prev8, dy, dnext8, w_ref, dw_ref, width, cols=slice(None)):
    dx = dy * w_ref[width - 1:width, cols]
    dw_ref[width - 1:width, cols] += jnp.sum(dy * x, axis=0, keepdims=True)
    for k in range(1, width):
        dx = dx + _shift_up(dy, dnext8, k) * w_ref[width - 1 - k:width - k, cols]
        dw_ref[width - 1 - k:width - k, cols] += jnp.sum(dy * _shift_down(x, prev8, k), axis=0, keepdims=True)
    return dx


HALO = 16


def _prev_spec(tt, width, tile=lambda i: i):
    return pl.BlockSpec((HALO, width), lambda i: (jnp.maximum(tile(i) * (tt // HALO) - 1, 0), 0))


def _prev8(p_ref, cols=slice(None)):
    return p_ref[:, cols].astype(F32)[HALO - 8:]


def _store_heads(ref, x):
    for h in range(HEADS):
        ref[h] = x[:, HEAD_DIM * h:HEAD_DIM * (h + 1)]


def _load_heads(ref):
    return jnp.concatenate([ref[h] for h in range(HEADS)], axis=1)


def _softplus(z):
    return jnp.maximum(z, 0.0) + jnp.log1p(jnp.exp(-jnp.abs(z)))


def _tri_masks(tt):
    r = lax.broadcasted_iota(jnp.int32, (tt, tt), 0)
    c = lax.broadcasted_iota(jnp.int32, (tt, tt), 1)
    same_chunk = lax.shift_right_logical(r, 6) == lax.shift_right_logical(c, 6)
    return r, c, same_chunk


def _gate_fwd(small, pbias, pscale):
    t = small.shape[0]
    tt = _pick(t, (256, 128))

    def body(x_ref, pb_ref, ps_ref, o_ref, carry_ref):
        @pl.when(pl.program_id(0) == 0)
        def _():
            carry_ref[...] = jnp.zeros_like(carry_ref)

        lane = lax.broadcasted_iota(jnp.int32, (tt, LANES), 1)
        z = x_ref[...] + pb_ref[...]
        log_f = jnp.where(lane < HEADS, -_softplus(-z), 0.0)
        g = jnp.where((lane >= 2 * HEADS) & (lane < 3 * HEADS), ps_ref[...] * _softplus(z), 0.0)
        r, c, same_chunk = _tri_masks(tt)
        lower = jnp.where(r >= c, 1.0, 0.0)
        lower_chunk = jnp.where((r >= c) & same_chunk, 1.0, 0.0)
        csum = jnp.dot(lower, log_f, precision=lax.Precision.HIGHEST, preferred_element_type=F32) + carry_ref[...]
        gc = jnp.dot(lower_chunk, g, precision=lax.Precision.HIGHEST, preferred_element_type=F32)
        carry_ref[...] += jnp.sum(log_f, axis=0, keepdims=True)
        o_ref[...] = jnp.where(lane < HEADS, csum, jnp.where(lane < 2 * HEADS, jax.nn.sigmoid(z), gc))

    row = pl.BlockSpec((tt, LANES), lambda i: (i, 0))
    vec = pl.BlockSpec((1, LANES), lambda i: (0, 0))
    return pl.pallas_call(
        body, grid=(t // tt,), name="gate_fwd", in_specs=[row, vec, vec], out_specs=row,
        out_shape=jax.ShapeDtypeStruct((t, LANES), F32),
        scratch_shapes=[pltpu.VMEM((1, LANES), F32)],
        compiler_params=_params("arbitrary"),
    )(small, pbias, pscale)


def _gate_bwd(small, pbias, pscale, dscal):
    t = small.shape[0]
    tt = _pick(t, (256, 128))
    nt = t // tt

    def body(x_ref, pb_ref, ps_ref, d_ref, dx_ref, dpb_ref, dps_ref, carry_ref):
        @pl.when(pl.program_id(0) == 0)
        def _():
            carry_ref[...] = jnp.zeros_like(carry_ref)
            dpb_ref[...] = jnp.zeros_like(dpb_ref)
            dps_ref[...] = jnp.zeros_like(dps_ref)

        lane = lax.broadcasted_iota(jnp.int32, (tt, LANES), 1)
        z = x_ref[...] + pb_ref[...]
        d = d_ref[...]
        dc = jnp.where(lane < HEADS, d, 0.0)
        dbeta = jnp.where((lane >= HEADS) & (lane < 2 * HEADS), d, 0.0)
        dgc = jnp.where((lane >= 2 * HEADS) & (lane < 3 * HEADS), d, 0.0)
        r, c, same_chunk = _tri_masks(tt)
        upper = jnp.where(r <= c, 1.0, 0.0)
        upper_chunk = jnp.where((r <= c) & same_chunk, 1.0, 0.0)
        dlogf = jnp.dot(upper, dc, precision=lax.Precision.HIGHEST, preferred_element_type=F32) + carry_ref[...]
        dg = jnp.dot(upper_chunk, dgc, precision=lax.Precision.HIGHEST, preferred_element_type=F32)
        carry_ref[...] += jnp.sum(dc, axis=0, keepdims=True)
        sg = jax.nn.sigmoid(z)
        dz = dlogf * (1.0 - sg) + dbeta * sg * (1.0 - sg) + dg * ps_ref[...] * sg
        dx_ref[...] = dz.astype(dx_ref.dtype)
        dpb_ref[...] += jnp.sum(dz, axis=0, keepdims=True)
        dps_ref[...] += jnp.sum(dg * _softplus(z), axis=0, keepdims=True)

    row = pl.BlockSpec((tt, LANES), lambda i: (nt - 1 - i, 0))
    vec = pl.BlockSpec((1, LANES), lambda i: (0, 0))
    return pl.pallas_call(
        body, grid=(nt,), name="gate_bwd", in_specs=[row, vec, vec, row], out_specs=[row, vec, vec],
        out_shape=[jax.ShapeDtypeStruct((t, LANES), BF16), jax.ShapeDtypeStruct((1, LANES), F32),
                   jax.ShapeDtypeStruct((1, LANES), F32)],
        scratch_shapes=[pltpu.VMEM((1, LANES), F32)],
        compiler_params=_params("arbitrary"),
    )(small, pbias, pscale, dscal)


def _gdn_pre_fwd(xg, conv_w, seg):
    t = xg.shape[0]
    c3 = 3 * WIDTH
    tt = _pick(t, (320, 256, 128))

    def body(x_ref, p_ref, w_ref, e_ref, q_ref, k_ref, v_ref):
        x = x_ref[...].astype(F32)
        prev = jnp.where(pl.program_id(0) == 0, 0.0, _prev8(p_ref))
        s = _silu(_causal_conv(x, prev, w_ref, GDN_CONV))
        e = e_ref[...]
        q = s[:, :WIDTH]
        k = s[:, WIDTH:2 * WIDTH]
        _store_heads(q_ref, q * lax.rsqrt(_segsum(q * q, e) + RMS_EPS) * (HEAD_DIM ** -0.5))
        _store_heads(k_ref, k * lax.rsqrt(_segsum(k * k, e) + RMS_EPS))
        _store_heads(v_ref, s[:, 2 * WIDTH:])

    out = pl.BlockSpec((HEADS, tt, HEAD_DIM), lambda i: (0, i, 0))
    shp = jax.ShapeDtypeStruct((HEADS, t, HEAD_DIM), F32)
    return pl.pallas_call(
        body, grid=(t // tt,), name="gdn_pre_fwd",
        in_specs=[pl.BlockSpec((tt, c3), lambda i: (i, 0)), _prev_spec(tt, c3),
                  pl.BlockSpec((GDN_CONV, c3), lambda i: (0, 0)), pl.BlockSpec((WIDTH, WIDTH), lambda i: (0, 0))],
        out_specs=[out, out, out], out_shape=[shp, shp, shp],
        compiler_params=_params("arbitrary"),
    )(xg, xg, conv_w, seg)


def _gdn_pre_bwd(xg, conv_w, seg, dqn, dkn, dv):
    t = xg.shape[0]
    c3 = 3 * WIDTH
    tt = _pick(t, (320, 256, 128))
    nt = t // tt

    def body(x_ref, p_ref, w_ref, e_ref, dq_ref, dk_ref, dv_ref, dx_ref, dw_ref, carry_ref):
        step = pl.program_id(0)
        x = x_ref[...].astype(F32)
        e = e_ref[...]
        prev = jnp.where(step == nt - 1, 0.0, _prev8(p_ref))
        y = _causal_conv(x, prev, w_ref, GDN_CONV)
        s = _silu(y)
        q = s[:, :WIDTH]
        k = s[:, WIDTH:2 * WIDTH]
        rq = lax.rsqrt(_segsum(q * q, e) + RMS_EPS)
        rk = lax.rsqrt(_segsum(k * k, e) + RMS_EPS)
        gq = _load_heads(dq_ref) * (HEAD_DIM ** -0.5)
        gk = _load_heads(dk_ref)
        dq = rq * gq - q * (rq * rq * rq) * _segsum(gq * q, e)
        dk = rk * gk - k * (rk * rk * rk) * _segsum(gk * k, e)
        dy = jnp.concatenate([dq, dk, _load_heads(dv_ref)], axis=1) * _silu_grad(y)

        @pl.when(step == 0)
        def _():
            carry_ref[...] = jnp.zeros_like(carry_ref)
            dw_ref[...] = jnp.zeros_like(dw_ref)

        dx = _causal_conv_bwd(x, prev, dy, carry_ref[...], w_ref, dw_ref, GDN_CONV)
        dx_ref[...] = dx.astype(dx_ref.dtype)
        carry_ref[...] = dy[:8]

    rev = lambda i: (nt - 1 - i, 0)
    blk = pl.BlockSpec((HEADS, tt, HEAD_DIM), lambda i: (0, nt - 1 - i, 0))
    return pl.pallas_call(
        body, grid=(nt,), name="gdn_pre_bwd",
        in_specs=[pl.BlockSpec((tt, c3), rev), _prev_spec(tt, c3, lambda i: nt - 1 - i),
                  pl.BlockSpec((GDN_CONV, c3), lambda i: (0, 0)), pl.BlockSpec((WIDTH, WIDTH), lambda i: (0, 0)),
                  blk, blk, blk],
        out_specs=[pl.BlockSpec((tt, c3), rev), pl.BlockSpec((GDN_CONV, c3), lambda i: (0, 0))],
        out_shape=[jax.ShapeDtypeStruct((t, c3), BF16), jax.ShapeDtypeStruct((GDN_CONV, c3), F32)],
        scratch_shapes=[pltpu.VMEM((8, c3), F32)],
        compiler_params=_params("arbitrary"),
    )(xg, xg, conv_w, seg, dqn, dkn, dv)


def _bmm(a, b, ca, cb, precision=None):
    return lax.dot_general(a, b, (((ca,), (cb,)), ((0,), (0,))), precision=precision, preferred_element_type=F32)


def _bf(x):
    return x.astype(BF16)


def _tri_inverse(a, eye):
    x = -a
    tinv = eye + x
    pw = x
    for _ in range(5):
        pb = _bf(pw)
        pw = _bmm(pb, pb, 2, 1)
        tinv = tinv + _bmm(_bf(tinv), _bf(pw), 2, 1)
    resid = eye - _bmm(eye + a, tinv, 2, 1, precision=HI)
    return tinv + _bmm(_bf(tinv), _bf(resid), 2, 1)


def _gdn_intra(q, k, v, bc, gcc, gcr):
    ii = lax.broadcasted_iota(jnp.int32, (CHUNK, CHUNK), 0)
    jj = lax.broadcasted_iota(jnp.int32, (CHUNK, CHUNK), 1)
    tril = (ii >= jj)[None]
    strict = (ii > jj)[None]
    eye = jnp.where(ii == jj, 1.0, 0.0).astype(F32)[None]
    last = (ii == CHUNK - 1)[None]
    dm = jnp.exp(jnp.where(tril, gcc - gcr, NEG))
    gam = jnp.exp(gcc)
    kb = k * bc
    vb = v * bc
    kk = _bmm(_bf(kb), _bf(k), 2, 2)
    a = jnp.where(strict, kk * dm, 0.0)
    tinv = _tri_inverse(a, eye)
    kbg = kb * gam
    u = _bmm(tinv, vb, 2, 1, precision=HI)
    wk = _bmm(tinv, kbg, 2, 1, precision=HI)
    qk = _bmm(_bf(q), _bf(k), 2, 2)
    p = jnp.where(tril, qk * dm, 0.0)
    gl = jnp.sum(jnp.where(last, gcc, 0.0), axis=1, keepdims=True)
    edec = jnp.exp(gl - gcc)
    return dict(tril=tril, strict=strict, dm=dm, gam=gam, kb=kb, kk=kk, a=a, tinv=tinv, u=u, wk=wk, qk=qk, p=p,
                qg=q * gam, kt=k * edec, edec=edec, gaml=jnp.exp(gl), last=last)


def _gate_tiles(sc, gct, nb):
    rows = nb * CHUNK
    cols = lambda lane0: jnp.stack([jnp.broadcast_to(sc[:, lane0 + h:lane0 + h + 1], (rows, HEAD_DIM))
                                    for h in range(HEADS)], axis=0).reshape(HEADS * nb, CHUNK, HEAD_DIM)
    gcr = jnp.stack([jnp.broadcast_to(gct[h:h + 1, n * CHUNK:(n + 1) * CHUNK], (CHUNK, CHUNK))
                     for h in range(HEADS) for n in range(nb)], axis=0)
    return cols(HEADS), cols(2 * HEADS), gcr


def _gdn_fwd(q, k, v, scal, gct, nb=None):
    h, t, dh = q.shape
    nc = t // CHUNK
    nb = nb or _pick(nc, (4, 2))
    bsz = h * nb

    def body(q_ref, k_ref, v_ref, sc_ref, gt_ref, o_ref, s0_ref, state_ref):
        @pl.when(pl.program_id(0) == 0)
        def _():
            state_ref[...] = jnp.zeros_like(state_ref)

        ld = lambda r: r[...].reshape(bsz, CHUNK, dh)
        bc, gcc, gcr = _gate_tiles(sc_ref[...], gt_ref[...], nb)
        z = _gdn_intra(ld(q_ref), ld(k_ref), ld(v_ref), bc, gcc, gcr)
        per = lambda x: x.reshape((h, nb) + x.shape[1:])
        u, wk, p, qg, kt, gaml = (per(z[n]) for n in ("u", "wk", "p", "qg", "kt", "gaml"))
        s = state_ref[...]
        for n in range(nb):
            s0_ref[:, n] = s
            sb = _bf(s)
            vn = u[:, n] - _bmm(_bf(wk[:, n]), sb, 2, 1)
            o_ref[:, n * CHUNK:(n + 1) * CHUNK, :] = _bmm(_bf(qg[:, n]), sb, 2, 1) + _bmm(_bf(p[:, n]), _bf(vn), 2, 1)
            s = s * gaml[:, n] + _bmm(_bf(kt[:, n]), _bf(vn), 1, 1)
        state_ref[...] = s

    blk = pl.BlockSpec((h, nb * CHUNK, dh), lambda i: (0, i, 0))
    return pl.pallas_call(
        body, grid=(nc // nb,), name="gdn_fwd",
        in_specs=[blk] * 3 + [pl.BlockSpec((nb * CHUNK, LANES), lambda i: (i, 0)),
                              pl.BlockSpec((h, nb * CHUNK), lambda i: (0, i))],
        out_specs=[blk, pl.BlockSpec((h, nb, dh, dh), lambda i: (0, i, 0, 0))],
        out_shape=[jax.ShapeDtypeStruct((h, t, dh), F32), jax.ShapeDtypeStruct((h, nc, dh, dh), F32)],
        scratch_shapes=[pltpu.VMEM((h, dh, dh), F32)],
        compiler_params=_params("arbitrary"),
    )(q, k, v, scal, gct)


def _gdn_bwd(q, k, v, scal, gct, s0s, do, nb=None):
    h, t, dh = q.shape
    nc = t // CHUNK
    nb = nb or _pick(nc, (2,))
    bsz = h * nb
    ng = nc // nb
    rows = nb * CHUNK

    def body(q_ref, k_ref, v_ref, sc_ref, gt_ref, s0_ref, do_ref,
             dq_ref, dk_ref, dv_ref, dsc_ref, dgt_ref, ds_ref):
        @pl.when(pl.program_id(0) == 0)
        def _():
            ds_ref[...] = jnp.zeros_like(ds_ref)

        ld = lambda r: r[...].reshape(bsz, CHUNK, dh)
        q, k, v = ld(q_ref), ld(k_ref), ld(v_ref)
        bc, gcc, gcr = _gate_tiles(sc_ref[...], gt_ref[...], nb)
        z = _gdn_intra(q, k, v, bc, gcc, gcr)
        per = lambda x: x.reshape((h, nb) + x.shape[1:])
        u, wk, p, qg, kt, gaml = (per(z[n]) for n in ("u", "wk", "p", "qg", "kt", "gaml"))
        dout = per(ld(do_ref))
        ds = ds_ref[...]
        d_u, d_wk, d_p, d_qg, d_kt, d_gaml = ([None] * nb for _ in range(6))
        for n in reversed(range(nb)):
            s0 = s0_ref[:, n]
            s0b, dsb, dob = _bf(s0), _bf(ds), _bf(dout[:, n])
            wkb, qgb = _bf(wk[:, n]), _bf(qg[:, n])
            vn = u[:, n] - _bmm(wkb, s0b, 2, 1)
            dvn = _bmm(_bf(p[:, n]), dob, 1, 1) + _bmm(_bf(kt[:, n]), dsb, 2, 1)
            dvnb = _bf(dvn)
            d_u[n] = dvn
            d_p[n] = _bmm(dob, _bf(vn), 2, 2)
            d_qg[n] = _bmm(dob, s0b, 2, 2)
            d_kt[n] = _bmm(_bf(vn), dsb, 2, 2)
            d_gaml[n] = jnp.sum(s0 * ds, axis=1, keepdims=True)
            d_wk[n] = -_bmm(dvnb, s0b, 2, 2)
            ds = _bmm(qgb, dob, 1, 1) + gaml[:, n] * ds - _bmm(wkb, dvnb, 1, 1)
        ds_ref[...] = ds

        flat = lambda xs: jnp.stack(xs, axis=1).reshape((bsz,) + xs[0].shape[1:])
        d_u, d_wk, d_p, d_qg, d_kt, d_gaml = (flat(x) for x in (d_u, d_wk, d_p, d_qg, d_kt, d_gaml))
        tinv, gam, kb, dm = z["tinv"], z["gam"], z["kb"], z["dm"]
        drv = _bmm(tinv, d_u, 1, 1, precision=HI)
        drk = _bmm(tinv, d_wk, 1, 1, precision=HI)
        da = -(_bmm(_bf(drv), _bf(z["u"]), 2, 2) + _bmm(_bf(drk), _bf(z["wk"]), 2, 2))
        da = jnp.where(z["strict"], da, 0.0)
        d_p = jnp.where(z["tril"], d_p, 0.0)
        dkk = _bf(da * dm)
        dqk = _bf(d_p * dm)
        dkb = _bmm(dkk, _bf(k), 2, 1) + drk * gam
        dk = _bmm(dkk, _bf(kb), 1, 1) + _bmm(dqk, _bf(q), 1, 1) + dkb * bc + d_kt * z["edec"]
        dq = _bmm(dqk, _bf(k), 2, 1) + d_qg * gam
        mm = da * z["a"] + d_p * z["p"]
        dkt_kt = d_kt * z["kt"]
        dgl = jnp.sum(dkt_kt, axis=1, keepdims=True) + d_gaml * z["gaml"]
        dgc = mm + d_qg * z["qg"] + drk * kb * gam - dkt_kt + jnp.where(z["last"], dgl, 0.0)
        dq_ref[...] = dq.reshape(h, rows, dh)
        dk_ref[...] = dk.reshape(h, rows, dh)
        dv_ref[...] = (drv * bc).reshape(h, rows, dh)
        dbeta = (dkb * k + drv * v).reshape(h, rows, dh)
        dgc = dgc.reshape(h, rows, dh)
        lane = lax.broadcasted_iota(jnp.int32, (rows, LANES), 1)
        dsc = jnp.zeros((rows, LANES), F32)
        for hh in range(h):
            dsc = jnp.where(lane == HEADS + hh, jnp.sum(dbeta[hh], axis=1, keepdims=True), dsc)
            dsc = jnp.where(lane == 2 * HEADS + hh, jnp.sum(dgc[hh], axis=1, keepdims=True), dsc)
        dsc_ref[...] = dsc
        dgr = -jnp.sum(mm, axis=1, keepdims=True)
        for hh in range(h):
            for n in range(nb):
                dgt_ref[hh:hh + 1, n * CHUNK:(n + 1) * CHUNK] = dgr[hh * nb + n]

    blk = pl.BlockSpec((h, rows, dh), lambda i: (0, ng - 1 - i, 0))
    shp = jax.ShapeDtypeStruct((h, t, dh), F32)
    sc_spec = pl.BlockSpec((rows, LANES), lambda i: (ng - 1 - i, 0))
    gt_spec = pl.BlockSpec((h, rows), lambda i: (0, ng - 1 - i))
    return pl.pallas_call(
        body, grid=(ng,), name="gdn_bwd",
        in_specs=[blk] * 3 + [sc_spec, gt_spec, pl.BlockSpec((h, nb, dh, dh), lambda i: (0, ng - 1 - i, 0, 0)), blk],
        out_specs=[blk] * 3 + [sc_spec, gt_spec],
        out_shape=[shp] * 3 + [jax.ShapeDtypeStruct((t, LANES), F32), jax.ShapeDtypeStruct((h, t), F32)],
        scratch_shapes=[pltpu.VMEM((h, dh, dh), F32)],
        compiler_params=_params("arbitrary"),
    )(q, k, v, scal, gct, s0s, do)


def _gdn_post_fwd(o, xg, gain, seg):
    t = o.shape[1]
    tt = _pick(t, (320, 256, 128))

    def body(o_ref, z_ref, g_ref, e_ref, y_ref):
        x = _load_heads(o_ref)
        r = lax.rsqrt(_segsum(x * x, e_ref[...]) * (1.0 / HEAD_DIM) + RMS_EPS)
        y_ref[...] = (x * r * g_ref[...] * _silu(z_ref[...].astype(F32))).astype(y_ref.dtype)

    return pl.pallas_call(
        body, grid=(t // tt,), name="gdn_post_fwd",
        in_specs=[pl.BlockSpec((HEADS, tt, HEAD_DIM), lambda i: (0, i, 0)), pl.BlockSpec((tt, WIDTH), lambda i: (i, 3)),
                  pl.BlockSpec((1, WIDTH), lambda i: (0, 0)), pl.BlockSpec((WIDTH, WIDTH), lambda i: (0, 0))],
        out_specs=pl.BlockSpec((tt, WIDTH), lambda i: (i, 0)),
        out_shape=jax.ShapeDtypeStruct((t, WIDTH), BF16),
        compiler_params=_params("arbitrary"),
    )(o, xg, gain, seg)


def _gdn_post_bwd(o, xg, gain, seg, dy):
    t = o.shape[1]
    tt = _pick(t, (320, 256, 128))

    def body(o_ref, z_ref, g_ref, e_ref, dy_ref, do_ref, dz_ref, dg_ref):
        x = _load_heads(o_ref)
        zz = z_ref[...].astype(F32)
        e = e_ref[...]
        gain_v = g_ref[...]
        d = dy_ref[...]
        r = lax.rsqrt(_segsum(x * x, e) * (1.0 / HEAD_DIM) + RMS_EPS)
        xr = x * r
        don = d * _silu(zz)
        dz_ref[...] = (d * xr * gain_v * _silu_grad(zz)).astype(dz_ref.dtype)
        gy = don * gain_v
        _store_heads(do_ref, r * gy - xr * (r * r) * (_segsum(gy * x, e) * (1.0 / HEAD_DIM)))

        @pl.when(pl.program_id(0) == 0)
        def _():
            dg_ref[...] = jnp.zeros_like(dg_ref)

        dg_ref[...] += jnp.sum(don * xr, axis=0, keepdims=True)

    row = pl.BlockSpec((tt, WIDTH), lambda i: (i, 0))
    vec = pl.BlockSpec((1, WIDTH), lambda i: (0, 0))
    hm = pl.BlockSpec((HEADS, tt, HEAD_DIM), lambda i: (0, i, 0))
    return pl.pallas_call(
        body, grid=(t // tt,), name="gdn_post_bwd",
        in_specs=[hm, pl.BlockSpec((tt, WIDTH), lambda i: (i, 3)), vec,
                  pl.BlockSpec((WIDTH, WIDTH), lambda i: (0, 0)), row],
        out_specs=[hm, row, vec],
        out_shape=[jax.ShapeDtypeStruct((HEADS, t, HEAD_DIM), F32), jax.ShapeDtypeStruct((t, WIDTH), BF16),
                   jax.ShapeDtypeStruct((1, WIDTH), F32)],
        compiler_params=_params("arbitrary"),
    )(o, xg, gain, seg, dy)


def _mix_fwd(yf, yg, gates, bias):
    t, d = yf.shape
    tt = _pick(t, (320, 256, 128))

    def body(yf_ref, yg_ref, g1_ref, g2_ref, b1_ref, b2_ref, o_ref):
        g1 = jax.nn.sigmoid(g1_ref[...].astype(F32) + b1_ref[...])
        g2 = jax.nn.sigmoid(g2_ref[...].astype(F32) + b2_ref[...])
        o_ref[...] = (g1 * yf_ref[...] + g2 * yg_ref[...]).astype(o_ref.dtype)

    row = pl.BlockSpec((tt, d), lambda i: (i, 0))
    return pl.pallas_call(
        body, grid=(t // tt,), name="mix_fwd",
        in_specs=[row, row, row, pl.BlockSpec((tt, d), lambda i: (i, 1)),
                  pl.BlockSpec((1, d), lambda i: (0, 0)), pl.BlockSpec((1, d), lambda i: (0, 1))],
        out_specs=row, out_shape=jax.ShapeDtypeStruct((t, d), BF16),
        compiler_params=_params("arbitrary"),
    )(yf, yg, gates, gates, bias, bias)


def _mix_bwd(dmix, yf, yg, gates, bias):
    t, d = yf.shape
    tt = _pick(t, (320, 256, 128))

    def body(dm_ref, yf_ref, yg_ref, g1_ref, g2_ref, b1_ref, b2_ref, dyf_ref, dyg_ref, dg_ref, db_ref):
        dm = dm_ref[...]
        g1 = jax.nn.sigmoid(g1_ref[...].astype(F32) + b1_ref[...])
        g2 = jax.nn.sigmoid(g2_ref[...].astype(F32) + b2_ref[...])
        dyf_ref[...] = (dm * g1).astype(BF16)
        dyg_ref[...] = (dm * g2).astype(BF16)
        dgate = jnp.concatenate([dm * yf_ref[...] * g1 * (1.0 - g1), dm * yg_ref[...] * g2 * (1.0 - g2)], axis=1)
        dg_ref[...] = dgate.astype(BF16)

        @pl.when(pl.program_id(0) == 0)
        def _():
            db_ref[...] = jnp.zeros_like(db_ref)

        db_ref[...] += jnp.sum(dgate, axis=0, keepdims=True)

    row = pl.BlockSpec((tt, d), lambda i: (i, 0))
    wide = pl.BlockSpec((tt, 2 * d), lambda i: (i, 0))
    return pl.pallas_call(
        body, grid=(t // tt,), name="mix_bwd",
        in_specs=[row, row, row, row, pl.BlockSpec((tt, d), lambda i: (i, 1)),
                  pl.BlockSpec((1, d), lambda i: (0, 0)), pl.BlockSpec((1, d), lambda i: (0, 1))],
        out_specs=[row, row, wide, pl.BlockSpec((1, 2 * d), lambda i: (0, 0))],
        out_shape=[jax.ShapeDtypeStruct((t, d), BF16), jax.ShapeDtypeStruct((t, d), BF16),
                   jax.ShapeDtypeStruct((t, 2 * d), BF16), jax.ShapeDtypeStruct((1, 2 * d), F32)],
        compiler_params=_params("arbitrary"),
    )(dmix, yf, yg, gates, gates, bias, bias)


def _ffn_act_fwd(up, conv_w, conv_b):
    t, c = up.shape
    tt = 128

    def body(x_ref, p_ref, w_ref, b_ref, o_ref):
        first = pl.program_id(0) == 0

        def conv(cols):
            prev = jnp.where(first, 0.0, _prev8(p_ref, cols))
            return _causal_conv(x_ref[:, cols].astype(F32), prev, w_ref, FFN_CONV, cols) + b_ref[:, cols]

        for lo in range(0, D_FF, FFN_LANES):
            gate = conv(slice(lo, lo + FFN_LANES))
            val = conv(slice(D_FF + lo, D_FF + lo + FFN_LANES))
            o_ref[:, lo:lo + FFN_LANES] = (_silu(gate) * val).astype(o_ref.dtype)

    return pl.pallas_call(
        body, grid=(t // tt,), name="ffn_act_fwd",
        in_specs=[pl.BlockSpec((tt, c), lambda i: (i, 0)), _prev_spec(tt, c),
                  pl.BlockSpec((FFN_CONV, c), lambda i: (0, 0)), pl.BlockSpec((1, c), lambda i: (0, 0))],
        out_specs=pl.BlockSpec((tt, D_FF), lambda i: (i, 0)),
        out_shape=jax.ShapeDtypeStruct((t, D_FF), BF16),
        compiler_params=_params("arbitrary"),
    )(up, up, conv_w, conv_b)


def _ffn_act_bwd(up, conv_w, conv_b, dact):
    t, c = up.shape
    tt = 128
    nt = t // tt

    def body(x_ref, p_ref, w_ref, b_ref, da_ref, dx_ref, dw_ref, db_ref, carry_ref):
        step = pl.program_id(0)

        @pl.when(step == 0)
        def _():
            carry_ref[...] = jnp.zeros_like(carry_ref)
            dw_ref[...] = jnp.zeros_like(dw_ref)
            db_ref[...] = jnp.zeros_like(db_ref)

        def conv(cols):
            x = x_ref[:, cols].astype(F32)
            prev = jnp.where(step == nt - 1, 0.0, _prev8(p_ref, cols))
            return x, prev, _causal_conv(x, prev, w_ref, FFN_CONV, cols) + b_ref[:, cols]

        def back(cols, x, prev, du):
            dx = _causal_conv_bwd(x, prev, du, carry_ref[:, cols], w_ref, dw_ref, FFN_CONV, cols)
            dx_ref[:, cols] = dx.astype(dx_ref.dtype)
            db_ref[:, cols] += jnp.sum(du, axis=0, keepdims=True)
            carry_ref[:, cols] = du[:8]

        for lo in range(0, D_FF, FFN_LANES):
            gcols, vcols = slice(lo, lo + FFN_LANES), slice(D_FF + lo, D_FF + lo + FFN_LANES)
            xg, pg, gate = conv(gcols)
            xv, pv, val = conv(vcols)
            da = da_ref[:, gcols]
            back(gcols, xg, pg, da * val * _silu_grad(gate))
            back(vcols, xv, pv, da * _silu(gate))

    rev = lambda i: (nt - 1 - i, 0)
    return pl.pallas_call(
        body, grid=(nt,), name="ffn_act_bwd",
        in_specs=[pl.BlockSpec((tt, c), rev),
                  _prev_spec(tt, c, lambda i: nt - 1 - i),
                  pl.BlockSpec((FFN_CONV, c), lambda i: (0, 0)), pl.BlockSpec((1, c), lambda i: (0, 0)),
                  pl.BlockSpec((tt, D_FF), rev)],
        out_specs=[pl.BlockSpec((tt, c), rev), pl.BlockSpec((FFN_CONV, c), lambda i: (0, 0)),
                   pl.BlockSpec((1, c), lambda i: (0, 0))],
        out_shape=[jax.ShapeDtypeStruct((t, c), BF16), jax.ShapeDtypeStruct((FFN_CONV, c), F32),
                   jax.ShapeDtypeStruct((1, c), F32)],
        scratch_shapes=[pltpu.VMEM((8, c), F32)],
        compiler_params=_params("arbitrary"),
    )(up, up, conv_w, conv_b, dact)


def _final_loss(h2, target, gain, seq):
    t, d = h2.shape
    tr = _pick(t, (320, 256, 128))

    def body(h_ref, t_ref, g_ref, loss_ref, dh_ref, dhb_ref, dg_ref):
        i = pl.program_id(0)
        x = h_ref[...]
        gain_v = g_ref[...]
        r = lax.rsqrt(jnp.mean(x * x, axis=-1, keepdims=True) + RMS_EPS)
        xr = x * r
        rows = i * tr + lax.broadcasted_iota(jnp.int32, (tr, 1), 0)
        real = (rows >= N_META) & (rows < N_META + seq)
        err = jnp.where(real, xr * gain_v - t_ref[...], 0.0)
        dy = err * (1.0 / d)
        gy = dy * gain_v
        dh = r * (gy - xr * jnp.mean(gy * xr, axis=-1, keepdims=True))
        dh_ref[...] = dh
        dhb_ref[...] = dh.astype(BF16)

        @pl.when(i == 0)
        def _():
            loss_ref[...] = jnp.zeros_like(loss_ref)
            dg_ref[...] = jnp.zeros_like(dg_ref)

        part = jnp.sum(jnp.sum(err * err, axis=-1, keepdims=True), axis=0, keepdims=True)
        loss_ref[...] += jnp.broadcast_to(part * (0.5 / d), loss_ref.shape)
        dg_ref[...] += jnp.sum(dy * xr, axis=0, keepdims=True)

    row = pl.BlockSpec((tr, d), lambda i: (i, 0))
    vec = pl.BlockSpec((1, d), lambda i: (0, 0))
    return pl.pallas_call(
        body, grid=(t // tr,), name="final_loss",
        in_specs=[row, row, vec],
        out_specs=[pl.BlockSpec((1, LANES), lambda i: (0, 0)), row, row, vec],
        out_shape=[jax.ShapeDtypeStruct((1, LANES), F32), jax.ShapeDtypeStruct((t, d), F32),
                   jax.ShapeDtypeStruct((t, d), BF16), jax.ShapeDtypeStruct((1, d), F32)],
        compiler_params=_params("arbitrary"),
    )(h2, target, gain)


ADAM_TILE_BYTES = 1 << 20


def _adamw(w, m, v, grecv, name):
    r, cols = w.shape
    tr = r
    if r * cols * 4 > ADAM_TILE_BYTES:
        tr = max(d for d in range(8, r + 1, 8) if r % d == 0 and d * cols * 4 <= ADAM_TILE_BYTES)

    def body(w_ref, m_ref, v_ref, g_ref, go_ref, d_ref, mo_ref, vo_ref):
        g = g_ref[0].astype(F32)
        for s in range(1, N_DEV):
            g = g + g_ref[s].astype(F32)
        wv = w_ref[...]
        mn = ADAM_B1 * m_ref[...] + (1.0 - ADAM_B1) * g
        vn = ADAM_B2 * v_ref[...] + (1.0 - ADAM_B2) * (g * g)
        m_hat = mn / (1.0 - ADAM_B1 ** ADAM_STEP)
        v_hat = vn / (1.0 - ADAM_B2 ** ADAM_STEP)
        go_ref[...] = g
        d_ref[...] = -ADAM_LR * (m_hat / (jnp.sqrt(v_hat) + ADAM_EPS) + ADAM_WD * wv)
        mo_ref[...] = mn
        vo_ref[...] = vn

    row = pl.BlockSpec((tr, cols), lambda i: (i, 0))
    shp = jax.ShapeDtypeStruct((r, cols), F32)
    return pl.pallas_call(
        body, grid=(r // tr,), name=name,
        in_specs=[row, row, row, pl.BlockSpec((N_DEV, tr, cols), lambda i: (0, i, 0))],
        out_specs=[row] * 4, out_shape=[shp] * 4,
        compiler_params=_params("parallel"),
    )(w, m, v, grecv)


def _mesh_pos():
    return lax.axis_index("x"), lax.axis_index("y"), lax.axis_index("c")


def _all_gather(shards):
    n = len(shards)

    def body(*refs):
        x_refs, out_refs = refs[:n], refs[n:2 * n]
        send_sems, recv_sems, local_sems = refs[2 * n:]
        x, y, c = _mesh_pos()
        me, sibling = (x, y, c), (x, y, 1 - c)
        chips = [(1 - x, y), (x, 1 - y), (1 - x, 1 - y)]

        def slot(a, px, py, pc):
            return out_refs[a].at[4 * px + 2 * py + pc]

        def copy(a, kk, block, to, src=None):
            return pltpu.make_async_remote_copy(
                src_ref=slot(a, *block) if src is None else src, dst_ref=slot(a, *block),
                send_sem=send_sems.at[7 * a + kk], recv_sem=recv_sems.at[7 * a + kk],
                device_id=to, device_id_type=MESH_ID)

        mine = [pltpu.make_async_copy(x_refs[a], slot(a, *me), local_sems.at[a]) for a in range(n)]
        first = []
        for a in range(n):
            first.append(copy(a, 0, me, sibling, src=x_refs[a]))
            first += [copy(a, 1 + j, me, (*chip, c), src=x_refs[a]) for j, chip in enumerate(chips)]
        for cp in mine + first:
            cp.start()
        passed = []
        for j, chip in enumerate(chips):
            for a in range(n):
                copy(a, 1 + j, (*chip, c), me).wait_recv()
                passed.append(copy(a, 4 + j, (*chip, c), sibling))
                passed[-1].start()
        for a in range(n):
            copy(a, 0, sibling, me).wait_recv()
        for j, chip in enumerate(chips):
            for a in range(n):
                copy(a, 4 + j, (*chip, 1 - c), me).wait_recv()
        for cp in first + passed:
            cp.wait_send()
        for cp in mine:
            cp.wait()

    hbm = pl.BlockSpec(memory_space=pl.ANY)
    return pl.pallas_call(
        body, name="weight_all_gather", in_specs=[hbm] * n, out_specs=[hbm] * n,
        out_shape=[jax.ShapeDtypeStruct((N_DEV,) + s.shape, s.dtype) for s in shards],
        scratch_shapes=[pltpu.SemaphoreType.DMA((7 * n,)), pltpu.SemaphoreType.DMA((7 * n,)),
                        pltpu.SemaphoreType.DMA((n,))],
    )(*shards)


def _grad_exchange(blocks, small):
    n = len(blocks)

    def body(*refs):
        src_refs, dst_refs = refs[:n + 1], refs[n + 1:2 * n + 2]
        send_sems, recv_sems, local_sems = refs[2 * n + 2:]
        x, y, c = _mesh_pos()
        me = 4 * x + 2 * y + c
        copies = []
        for kk in range(1, N_DEV):
            px = 1 - x if kk & 4 else x
            py = 1 - y if kk & 2 else y
            pc = 1 - c if kk & 1 else c
            peer = 4 * px + 2 * py + pc
            for a in range(n + 1):
                copies.append(pltpu.make_async_remote_copy(
                    src_ref=src_refs[a].at[peer] if a < n else src_refs[a], dst_ref=dst_refs[a].at[me],
                    send_sem=send_sems.at[7 * a + kk - 1], recv_sem=recv_sems.at[7 * a + kk - 1],
                    device_id=(px, py, pc), device_id_type=MESH_ID))
        own = [pltpu.make_async_copy(src_refs[a].at[me] if a < n else src_refs[a], dst_refs[a].at[me],
                                     local_sems.at[a]) for a in range(n + 1)]
        for cp in own + copies:
            cp.start()
        for cp in copies + own:
            cp.wait()

    hbm = pl.BlockSpec(memory_space=pl.ANY)
    return pl.pallas_call(
        body, name="grad_exchange", in_specs=[hbm] * (n + 1), out_specs=[hbm] * (n + 1),
        out_shape=[jax.ShapeDtypeStruct(b.shape, b.dtype) for b in blocks]
        + [jax.ShapeDtypeStruct((N_DEV,) + small.shape, small.dtype)],
        scratch_shapes=[pltpu.SemaphoreType.DMA((7 * (n + 1),)), pltpu.SemaphoreType.DMA((7 * (n + 1),)),
                        pltpu.SemaphoreType.DMA((n + 1,))],
    )(*blocks, small)


def _pad_flat(parts, rows):
    flat = jnp.concatenate([p.reshape(-1) for p in parts])
    return jnp.pad(flat, (0, rows * LANES - flat.shape[0])).reshape(rows, LANES)


def _rows_for(n_elems, mult=1024):
    rows = -(-n_elems // LANES)
    return -(-rows // mult) * mult


SHARDED = ("meta_tokens", "w_in", "gdn_conv_w", "w_branch_fox", "w_branch_gdn", "w_out", "ffn_w_up", "ffn_conv_w",
           "ffn_w_down")
MATMUL = ("w_in", "w_branch_fox", "w_branch_gdn", "w_out", "ffn_w_up", "ffn_w_down")
EXACT = ("meta_tokens", "gdn_conv_w", "ffn_conv_w")
REPLICATED = ("fgt_bias", "gdn_a_log", "gdn_dt_bias", "gdn_norm_w", "gate_bias", "norm_mix_w", "norm_ffn_w",
              "ffn_conv_b", "norm_final_w")
WEIGHTS = ("meta_tokens", "w_in", "fgt_bias", "gdn_conv_w", "gdn_a_log", "gdn_dt_bias", "gdn_norm_w", "gate_bias",
           "w_branch_fox", "w_branch_gdn", "w_out", "norm_mix_w", "norm_ffn_w", "ffn_w_up", "ffn_conv_w",
           "ffn_conv_b", "ffn_w_down", "norm_final_w")


def _unpack(buf, shapes):
    flat = buf.reshape(-1)
    out, off = [], 0
    for s in shapes:
        n = int(np.prod(s))
        out.append(flat[off:off + n].reshape(s))
        off += n
    return out


def _unpack_gathered(buf, shapes):
    flat = buf.reshape(N_DEV, -1)
    out, off = [], 0
    for s in shapes:
        n = int(np.prod(s))
        out.append(flat[:, off:off + n].reshape((N_DEV,) + tuple(s)))
        off += n
    return out


def _cat_cols(g):
    return g.transpose(1, 0, 2).reshape(g.shape[1], -1)


def _col_blocks(full, width):
    return full.reshape(full.shape[0], N_DEV, width).transpose(1, 0, 2)


def _local_step(x, target, w):
    seq = x.shape[0]
    t = _padded_tokens(seq)
    pad = t - N_META - seq
    seg = _seg_matrix()
    zrows = jnp.zeros((pad, D_MODEL), F32)
    h0 = jnp.concatenate([w["meta_tokens"], x, zrows], axis=0)
    tgt = jnp.concatenate([jnp.zeros((N_META, D_MODEL), F32), target, zrows], axis=0)

    w_in = w["w_in"]
    o_f, o_g, o_z, o_b, o_a, o_gate = 1536, 1544, 3080, 3592, 3600, 3608
    w_small = jnp.concatenate([w_in[:, o_f:o_f + 8], w_in[:, o_b:o_b + 8], w_in[:, o_a:o_a + 8],
                               jnp.zeros((D_MODEL, LANES - 24), BF16)], axis=1)
    w_r = jnp.concatenate([w_in[:, :1536], w_in[:, o_g:o_z], w_in[:, o_z:o_b], w_in[:, o_gate:], w_small], axis=1)

    a1 = _rmsnorm_fwd(h0, w["norm_mix_w"])
    fq = _mm(a1, w_r[:, :1536], BF16, "proj_fox")
    xg = _mm(a1, w_r[:, 1536:3584], BF16, "proj_gdn")
    gt = _mm(a1, w_r[:, 3584:5632], BF16, "proj_gates")
    sm = _mm(a1, w_r[:, 5632:], F32, "proj_small")

    lanes_pad = lambda a, lo: jnp.pad(a, ((0, 0), (lo, LANES - lo - a.shape[1])))
    neg_exp_a = -jnp.exp(w["gdn_a_log"])
    pbias = lanes_pad(w["fgt_bias"], 0) + lanes_pad(w["gdn_dt_bias"], 2 * HEADS)
    pscale = lanes_pad(neg_exp_a, 2 * HEADS)
    scal = _gate_fwd(sm, pbias, pscale)
    gct = scal[:, 2 * HEADS:3 * HEADS].T

    qa, ka, va, kat, vat = _fox_prep(fq, scal)
    oa, qb, qbt = _fox_fwd(qa, ka, vat)
    o_fox = _fox_post(oa)

    qh, kh, vh = _gdn_pre_fwd(xg, w["gdn_conv_w"], seg)
    og, s0s = _gdn_fwd(qh, kh, vh, scal, gct)
    norm_w = jnp.tile(w["gdn_norm_w"], (1, HEADS))
    ogn = _gdn_post_fwd(og, xg, norm_w, seg)

    yf = _mm(o_fox, w["w_branch_fox"], F32, "branch_fox")
    yg = _mm(ogn, w["w_branch_gdn"], F32, "branch_gdn")
    mix = _mix_fwd(yf, yg, gt, w["gate_bias"])
    h1 = _mm(mix, w["w_out"], F32, "out_proj", res=h0)
    a2 = _rmsnorm_fwd(h1, w["norm_ffn_w"])
    up = _mm(a2, w["ffn_w_up"], BF16, "ffn_up")
    act = _ffn_act_fwd(up, w["ffn_conv_w"], w["ffn_conv_b"])
    h2 = _mm(act, w["ffn_w_down"], F32, "ffn_down", res=h1)
    loss, dh2, dh2b, g_final = _final_loss(h2, tgt, w["norm_final_w"].reshape(1, D_MODEL), seq)

    grads = {"norm_final_w": g_final.reshape(D_MODEL)}
    grads["ffn_w_down"] = _mm_tn(act, dh2b, "wgrad_ffn_down")
    dact = _mm(dh2b, w["ffn_w_down"].T, F32, "dgrad_ffn_down")
    dup, g_cw, g_cb = _ffn_act_bwd(up, w["ffn_conv_w"], w["ffn_conv_b"], dact)
    grads["ffn_conv_w"], grads["ffn_conv_b"] = g_cw, g_cb
    grads["ffn_w_up"] = _mm_tn(a2, dup, "wgrad_ffn_up")
    da2 = _mm(dup, w["ffn_w_up"].T, F32, "dgrad_ffn_up")
    dh1, dh1b, grads["norm_ffn_w"] = _rmsnorm_bwd(h1, da2, w["norm_ffn_w"], dh2)
    grads["w_out"] = _mm_tn(mix, dh1b, "wgrad_out")
    dmix = _mm(dh1b, w["w_out"].T, F32, "dgrad_out")
    dyf, dyg, dgt, grads["gate_bias"] = _mix_bwd(dmix, yf, yg, gt, w["gate_bias"])
    grads["w_branch_fox"] = _mm_tn(o_fox, dyf, "wgrad_branch_fox")
    grads["w_branch_gdn"] = _mm_tn(ogn, dyg, "wgrad_branch_gdn")
    do_fox = _mm(dyf, w["w_branch_fox"].T, F32, "dgrad_branch_fox")
    dogn = _mm(dyg, w["w_branch_gdn"].T, F32, "dgrad_branch_gdn")

    dog, dz, g_nw = _gdn_post_bwd(og, xg, norm_w, seg, dogn)
    grads["gdn_norm_w"] = g_nw.reshape(HEADS, HEAD_DIM).sum(axis=0)[None]
    dqh, dkh, dvh, dscal_g, dgct = _gdn_bwd(qh, kh, vh, scal, gct, s0s, dog)
    dxg, grads["gdn_conv_w"] = _gdn_pre_bwd(xg, w["gdn_conv_w"], seg, dqh, dkh, dvh)

    doa, doat = _fox_bwd_prep(do_fox, oa)
    dfq, dscal_c = _fox_bwd_post(*_fox_bwd(qb, qbt, ka, kat, va, doa, doat))

    dscal = dscal_c + dscal_g + lanes_pad(dgct.T, 2 * HEADS)
    dsm, dpb, dps = _gate_bwd(sm, pbias, pscale, dscal)
    grads["fgt_bias"] = dpb[:, :HEADS]
    grads["gdn_dt_bias"] = dpb[:, 2 * HEADS:3 * HEADS]
    grads["gdn_a_log"] = dps[:, 2 * HEADS:3 * HEADS] * neg_exp_a

    dproj = jnp.concatenate([dfq, dxg, dz, dgt, dsm], axis=1)
    g_r = _mm_tn(a1, dproj, "wgrad_in")
    grads["w_in"] = jnp.concatenate([g_r[:, :1536], g_r[:, 5632:5640], g_r[:, 1536:3072], g_r[:, 3072:3584],
                                     g_r[:, 5640:5648], g_r[:, 5648:5656], g_r[:, 3584:5632]], axis=1)
    da1 = _mm(dproj, w_r.T, F32, "dgrad_in")
    dh0, _, grads["norm_mix_w"] = _rmsnorm_bwd(h0, da1, w["norm_mix_w"], dh1)
    grads["meta_tokens"] = dh0[:N_META]
    return loss, dh0[N_META:N_META + seq], grads


def _shard_pieces(arrs):
    return [arrs[n][0] if arrs[n].ndim == 3 else arrs[n] for n in SHARDED]


def _full_grad_blocks(grads):
    g = grads
    cols = lambda a, wd: _col_blocks(a, wd)
    rows = lambda a: a.reshape(N_DEV, a.shape[0] // N_DEV, a.shape[1])
    per = [cols(g["meta_tokens"], 128), cols(g["w_in"], IN_WIDTH // N_DEV), cols(g["gdn_conv_w"], 3 * WIDTH // N_DEV),
           cols(g["w_branch_fox"], D_MODEL // N_DEV), cols(g["w_branch_gdn"], D_MODEL // N_DEV), rows(g["w_out"]),
           cols(g["ffn_w_up"], 2 * D_FF // N_DEV), cols(g["ffn_conv_w"], 2 * D_FF // N_DEV), rows(g["ffn_w_down"])]
    return per


def kernel(x, meta_tokens, w_in, fgt_bias, gdn_conv_w, gdn_a_log, gdn_dt_bias, gdn_norm_w, gate_bias, w_branch_fox, w_branch_gdn, w_out, norm_mix_w, norm_ffn_w, ffn_w_up, ffn_conv_w, ffn_conv_b, ffn_w_down, norm_final_w, loss_target, m_meta_tokens, m_w_in, m_fgt_bias, m_gdn_conv_w, m_gdn_a_log, m_gdn_dt_bias, m_gdn_norm_w, m_gate_bias, m_w_branch_fox, m_w_branch_gdn, m_w_out, m_norm_mix_w, m_norm_ffn_w, m_ffn_w_up, m_ffn_conv_w, m_ffn_conv_b, m_ffn_w_down, m_norm_final_w, v_meta_tokens, v_w_in, v_fgt_bias, v_gdn_conv_w, v_gdn_a_log, v_gdn_dt_bias, v_gdn_norm_w, v_gate_bias, v_w_branch_fox, v_w_branch_gdn, v_w_out, v_norm_mix_w, v_norm_ffn_w, v_ffn_w_up, v_ffn_conv_w, v_ffn_conv_b, v_ffn_w_down, v_norm_final_w):
    wts = dict(meta_tokens=meta_tokens, w_in=w_in, fgt_bias=fgt_bias, gdn_conv_w=gdn_conv_w, gdn_a_log=gdn_a_log,
               gdn_dt_bias=gdn_dt_bias, gdn_norm_w=gdn_norm_w, gate_bias=gate_bias, w_branch_fox=w_branch_fox,
               w_branch_gdn=w_branch_gdn, w_out=w_out, norm_mix_w=norm_mix_w, norm_ffn_w=norm_ffn_w,
               ffn_w_up=ffn_w_up, ffn_conv_w=ffn_conv_w, ffn_conv_b=ffn_conv_b, ffn_w_down=ffn_w_down,
               norm_final_w=norm_final_w)
    mom = dict(meta_tokens=m_meta_tokens, w_in=m_w_in, fgt_bias=m_fgt_bias, gdn_conv_w=m_gdn_conv_w,
               gdn_a_log=m_gdn_a_log, gdn_dt_bias=m_gdn_dt_bias, gdn_norm_w=m_gdn_norm_w, gate_bias=m_gate_bias,
               w_branch_fox=m_w_branch_fox, w_branch_gdn=m_w_branch_gdn, w_out=m_w_out, norm_mix_w=m_norm_mix_w,
               norm_ffn_w=m_norm_ffn_w, ffn_w_up=m_ffn_w_up, ffn_conv_w=m_ffn_conv_w, ffn_conv_b=m_ffn_conv_b,
               ffn_w_down=m_ffn_w_down, norm_final_w=m_norm_final_w)
    var = dict(meta_tokens=v_meta_tokens, w_in=v_w_in, fgt_bias=v_fgt_bias, gdn_conv_w=v_gdn_conv_w,
               gdn_a_log=v_gdn_a_log, gdn_dt_bias=v_gdn_dt_bias, gdn_norm_w=v_gdn_norm_w, gate_bias=v_gate_bias,
               w_branch_fox=v_w_branch_fox, w_branch_gdn=v_w_branch_gdn, w_out=v_w_out, norm_mix_w=v_norm_mix_w,
               norm_ffn_w=v_norm_ffn_w, ffn_w_up=v_ffn_w_up, ffn_conv_w=v_ffn_conv_w, ffn_conv_b=v_ffn_conv_b,
               ffn_w_down=v_ffn_w_down, norm_final_w=v_norm_final_w)

    sh = dict(zip(SHARDED, _shard_pieces(wts)))
    exact_shapes = [sh[n].shape for n in EXACT]
    rows_exact = _rows_for(sum(int(np.prod(s)) for s in exact_shapes), 8)
    gathered = _all_gather([sh[n].astype(BF16) for n in MATMUL] + [_pad_flat([sh[n] for n in EXACT], rows_exact)])
    g_in, g_bf, g_bg, g_out, g_up, g_down = gathered[:6]
    meta_full, conv_full, fconv_full = (_cat_cols(a) for a in _unpack_gathered(gathered[6], exact_shapes))
    full = dict(
        meta_tokens=meta_full, w_in=_cat_cols(g_in), gdn_conv_w=conv_full,
        w_branch_fox=_cat_cols(g_bf), w_branch_gdn=_cat_cols(g_bg), w_out=g_out.reshape(D_MODEL, D_MODEL),
        ffn_w_up=_cat_cols(g_up), ffn_conv_w=fconv_full, ffn_w_down=g_down.reshape(D_FF, D_MODEL),
        fgt_bias=fgt_bias, gdn_a_log=gdn_a_log, gdn_dt_bias=gdn_dt_bias, gdn_norm_w=gdn_norm_w, gate_bias=gate_bias,
        norm_mix_w=norm_mix_w, norm_ffn_w=norm_ffn_w, ffn_conv_b=ffn_conv_b, norm_final_w=norm_final_w)

    loss, grad_x, grads = _local_step(x[0], loss_target[0], full)

    blocks = [b.astype(BF16) for b in _full_grad_blocks(grads)]
    rep_parts = [grads[n] for n in REPLICATED] + [loss[:, :1]]
    rep_shapes = [wts[n].shape for n in REPLICATED]
    rows_small = _rows_for(sum(int(np.prod(p.shape)) for p in rep_parts), 8)
    received = _grad_exchange(blocks, _pad_flat(rep_parts, rows_small))

    result = {}
    kinds = ("grad", "delta", "new_m", "new_v")
    for n, recv in zip(SHARDED, received[:-1]):
        outs = _adamw(sh[n], _shard_pieces(mom)[SHARDED.index(n)], _shard_pieces(var)[SHARDED.index(n)], recv,
                      "adamw_" + n)
        for kind, a in zip(kinds, outs):
            result[kind, n] = a.reshape(wts[n].shape)
    rep_w = _pad_flat([wts[n] for n in REPLICATED] + [jnp.zeros((1, 1), F32)], rows_small)
    rep_m = _pad_flat([mom[n] for n in REPLICATED] + [jnp.zeros((1, 1), F32)], rows_small)
    rep_v = _pad_flat([var[n] for n in REPLICATED] + [jnp.ones((1, 1), F32)], rows_small)
    outs_r = _adamw(rep_w, rep_m, rep_v, received[-1], "adamw_replicated")
    for kind, br in zip(kinds, outs_r):
        for n, a in zip(REPLICATED, _unpack(br, rep_shapes)):
            result[kind, n] = a
    n_rep = sum(int(np.prod(s)) for s in rep_shapes)
    total_loss = outs_r[0].reshape(-1)[n_rep]
    out = [total_loss, grad_x[None]]
    for kind in ("grad", "delta", "new_m", "new_v"):
        out += [result[kind, n] for n in WEIGHTS]
    return tuple(out)
```

```python
import functools

import jax
import jax.numpy as jnp
import numpy as np
from jax import lax
from jax.experimental import pallas as pl
from jax.experimental.pallas import tpu as pltpu

F32 = jnp.float32
BF16 = jnp.bfloat16

D_MODEL = 1024
N_META = 16
HEADS = 8
HEAD_DIM = 64
WIDTH = HEADS * HEAD_DIM
CHUNK = 64
GDN_CONV = 4
D_FF = 2816
FFN_CONV = 3
IN_WIDTH = 5656
IN_PAD = 5760
RMS_EPS = 1e-6
NEG = -1e30
AUG = 128
N_DEV = 8
LANES = 128

ADAM_LR = 0.001
ADAM_B1 = 0.9
ADAM_B2 = 0.999
ADAM_EPS = 1e-08
ADAM_WD = 0.01
ADAM_STEP = 10

VMEM_LIMIT = 56 * 1024 * 1024
MM_VMEM_BUDGET = 36 * 1024 * 1024
FFN_LANES = 128
HI = lax.Precision.HIGH
MESH_ID = pl.DeviceIdType.MESH


def _pick(n, cands):
    for c in cands:
        if n % c == 0:
            return c
    raise ValueError(f"no tile for {n} in {cands}")


def _params(*sem):
    return pltpu.CompilerParams(dimension_semantics=sem if sem else None, vmem_limit_bytes=VMEM_LIMIT)


def _padded_tokens(seq):
    t = -(-(N_META + seq) // 128) * 128
    if t > 1280 and t % 640:
        t = -(-t // 640) * 640
    return t


ROW_TILES = (640, 512, 384, 256, 128)


def _rmsnorm_fwd(h, gain):
    t, d = h.shape
    tr = _pick(t, ROW_TILES)

    def body(h_ref, g_ref, o_ref):
        x = h_ref[...]
        r = lax.rsqrt(jnp.mean(x * x, axis=-1, keepdims=True) + RMS_EPS)
        o_ref[...] = (x * r * g_ref[...]).astype(o_ref.dtype)

    return pl.pallas_call(
        body, grid=(t // tr,), name="rmsnorm_fwd",
        in_specs=[pl.BlockSpec((tr, d), lambda i: (i, 0)), pl.BlockSpec((1, d), lambda i: (0, 0))],
        out_specs=pl.BlockSpec((tr, d), lambda i: (i, 0)),
        out_shape=jax.ShapeDtypeStruct((t, d), BF16),
        compiler_params=_params("arbitrary"),
    )(h, gain)


def _rmsnorm_bwd(h, dy, gain, dres):
    t, d = h.shape
    tr = _pick(t, (320, 256, 128))

    def body(h_ref, dy_ref, g_ref, dres_ref, dh_ref, dhb_ref, dg_ref):
        x = h_ref[...]
        dyv = dy_ref[...]
        r = lax.rsqrt(jnp.mean(x * x, axis=-1, keepdims=True) + RMS_EPS)
        gy = dyv * g_ref[...]
        m = jnp.mean(gy * x, axis=-1, keepdims=True)
        dh = dres_ref[...] + r * gy - x * (r * r * r * m)
        dh_ref[...] = dh
        dhb_ref[...] = dh.astype(BF16)

        @pl.when(pl.program_id(0) == 0)
        def _():
            dg_ref[...] = jnp.zeros_like(dg_ref)

        dg_ref[...] += jnp.sum(dyv * x * r, axis=0, keepdims=True)

    row = pl.BlockSpec((tr, d), lambda i: (i, 0))
    vec = pl.BlockSpec((1, d), lambda i: (0, 0))
    return pl.pallas_call(
        body, grid=(t // tr,), name="rmsnorm_bwd",
        in_specs=[row, row, vec, row], out_specs=[row, row, vec],
        out_shape=[jax.ShapeDtypeStruct((t, d), F32), jax.ShapeDtypeStruct((t, d), BF16),
                   jax.ShapeDtypeStruct((1, d), F32)],
        compiler_params=_params("arbitrary"),
    )(h, dy, gain, dres)


def _mm(a, b, out_dtype, name, res=None):
    m, k = a.shape
    _, n = b.shape
    tm = _pick(m, ROW_TILES)
    out_bytes = jnp.dtype(out_dtype).itemsize + (4 if res is not None else 0)
    fits = lambda tn: 4 * tm * k + 4 * k * tn + 2 * tm * tn * out_bytes <= MM_VMEM_BUDGET
    tn = next(c for c in (n, 2816, 2048, 1536, 1408, 1024, 512, 384, 256, 128) if n % c == 0 and fits(c))

    def body(*refs):
        if res is None:
            a_ref, b_ref, o_ref = refs
        else:
            a_ref, b_ref, r_ref, o_ref = refs
        out = jnp.dot(a_ref[...], b_ref[...], preferred_element_type=F32)
        if res is not None:
            out = out + r_ref[...]
        o_ref[...] = out.astype(o_ref.dtype)

    in_specs = [pl.BlockSpec((tm, k), lambda i, j: (i, 0)), pl.BlockSpec((k, tn), lambda i, j: (0, j))]
    args = [a, b]
    if res is not None:
        in_specs.append(pl.BlockSpec((tm, tn), lambda i, j: (i, j)))
        args.append(res)
    return pl.pallas_call(
        body, grid=(m // tm, n // tn), name=name,
        in_specs=in_specs, out_specs=pl.BlockSpec((tm, tn), lambda i, j: (i, j)),
        out_shape=jax.ShapeDtypeStruct((m, n), out_dtype),
        compiler_params=_params("parallel", "parallel"),
    )(*args)


def _mm_tn(a, g, name):
    t, k = a.shape
    _, n = g.shape
    tk = _pick(k, (1024, 1408, 512))
    tn = _pick(n, (512, 640, 384, 256, 128))
    tt = next(c for c in (3328, 1280) + ROW_TILES
              if t % c == 0 and 4 * c * (tk + tn) + 8 * tk * tn <= MM_VMEM_BUDGET)
    nt = t // tt

    def body(a_ref, g_ref, o_ref):
        @pl.when(pl.program_id(2) == 0)
        def _():
            o_ref[...] = jnp.zeros_like(o_ref)

        o_ref[...] += lax.dot_general(a_ref[...], g_ref[...], (((0,), (0,)), ((), ())),
                                      preferred_element_type=F32)

    return pl.pallas_call(
        body, grid=(k // tk, n // tn, nt), name=name,
        in_specs=[pl.BlockSpec((tt, tk), lambda i, j, s: (s, i)), pl.BlockSpec((tt, tn), lambda i, j, s: (s, j))],
        out_specs=pl.BlockSpec((tk, tn), lambda i, j, s: (i, j)),
        out_shape=jax.ShapeDtypeStruct((k, n), F32),
        compiler_params=_params("parallel", "parallel", "arbitrary"),
    )(a, g)


def _split3_exact(x):
    def top(v):
        return lax.bitcast_convert_type(lax.bitcast_convert_type(v, jnp.int32) & jnp.int32(-65536), F32)

    hi = top(x)
    r1 = x - hi
    mid = top(r1)
    return hi, mid, r1 - mid


def _pair_head(ref, h, rows):
    x = ref[:, 128 * (h // 2):128 * (h // 2) + 128].astype(F32)
    return pltpu.roll(x, HEAD_DIM, axis=1) if h % 2 else x


def _lanes(rows):
    return lax.broadcasted_iota(jnp.int32, (rows, AUG), 1)


def _fox_prep(fq, scal):
    t = fq.shape[0]
    tt = _pick(t, (256, 128))

    def body(q_ref, k_ref, v_ref, s_ref, qa_ref, ka_ref, va_ref, kt_ref, vt_ref):
        lane = _lanes(tt)
        chi, cmid, clo = _split3_exact(s_ref[...])
        ones = lambda lo: jnp.where((lane >= lo) & (lane < lo + 3), 1.0, 0.0)
        for h in range(HEADS):
            col = lambda a: jnp.broadcast_to(a[:, h:h + 1], (tt, AUG))
            c1, c2, c3 = col(chi), col(cmid), col(clo)
            qx = jnp.where(lane == 64, c1, jnp.where(lane == 65, c2, jnp.where(lane == 66, c3, ones(67))))
            kx = jnp.where(lane == 67, -c1, jnp.where(lane == 68, -c2, jnp.where(lane == 69, -c3, ones(64) + ones(70))))
            qa_ref[h] = jnp.where(lane < HEAD_DIM, _pair_head(q_ref, h, tt) * (HEAD_DIM ** -0.5), qx).astype(BF16)
            k_aug = jnp.where(lane < HEAD_DIM, _pair_head(k_ref, h, tt), kx)
            ka_ref[h] = k_aug.astype(BF16)
            kt_ref[h] = k_aug.T.astype(BF16)
            v_aug = jnp.where(lane < HEAD_DIM, _pair_head(v_ref, h, tt), ones(64))
            va_ref[h] = v_aug.astype(BF16)
            vt_ref[h] = v_aug.T.astype(BF16)

    out = pl.BlockSpec((HEADS, tt, AUG), lambda i: (0, i, 0))
    out_t = pl.BlockSpec((HEADS, AUG, tt), lambda i: (0, 0, i))
    shp = jax.ShapeDtypeStruct((HEADS, t, AUG), BF16)
    shp_t = jax.ShapeDtypeStruct((HEADS, AUG, t), BF16)
    return pl.pallas_call(
        body, grid=(t // tt,), name="fox_prep",
        in_specs=[pl.BlockSpec((tt, WIDTH), lambda i: (i, 0)), pl.BlockSpec((tt, WIDTH), lambda i: (i, 1)),
                  pl.BlockSpec((tt, WIDTH), lambda i: (i, 2)), pl.BlockSpec((tt, LANES), lambda i: (i, 0))],
        out_specs=[out, out, out, out_t, out_t], out_shape=[shp, shp, shp, shp_t, shp_t],
        compiler_params=_params("parallel"),
    )(fq, fq, fq, scal)


def _fox_post(oa):
    t = oa.shape[1]
    tt = _pick(t, (256, 128))

    def body(o_ref, out_ref):
        out_ref[...] = jnp.concatenate([o_ref[h][:, :HEAD_DIM] for h in range(HEADS)], axis=1).astype(BF16)

    return pl.pallas_call(
        body, grid=(t // tt,), name="fox_post",
        in_specs=[pl.BlockSpec((HEADS, tt, AUG), lambda i: (0, i, 0))],
        out_specs=pl.BlockSpec((tt, WIDTH), lambda i: (i, 0)),
        out_shape=jax.ShapeDtypeStruct((t, WIDTH), BF16),
        compiler_params=_params("parallel"),
    )(oa)


def _fox_bwd_prep(do, oa):
    t = do.shape[0]
    tt = _pick(t, (256, 128))

    def body(d_ref, o_ref, out_ref, outt_ref):
        lane = _lanes(tt)
        for h in range(HEADS):
            x = _pair_head(d_ref, h, tt)
            delta = jnp.sum(jnp.where(lane < HEAD_DIM, x * o_ref[h], 0.0), axis=1, keepdims=True)
            hi, mid, lo = _split3_exact(jnp.broadcast_to(-delta, (tt, AUG)))
            ex = jnp.where(lane == 64, hi, jnp.where(lane == 65, mid, jnp.where(lane == 66, lo, 0.0)))
            do_aug = jnp.where(lane < HEAD_DIM, x, ex)
            out_ref[h] = do_aug.astype(BF16)
            outt_ref[h] = do_aug.T.astype(BF16)

    hm = pl.BlockSpec((HEADS, tt, AUG), lambda i: (0, i, 0))
    return pl.pallas_call(
        body, grid=(t // tt,), name="fox_bwd_prep",
        in_specs=[pl.BlockSpec((tt, WIDTH), lambda i: (i, 0)), hm],
        out_specs=[hm, pl.BlockSpec((HEADS, AUG, tt), lambda i: (0, 0, i))],
        out_shape=[jax.ShapeDtypeStruct((HEADS, t, AUG), BF16), jax.ShapeDtypeStruct((HEADS, AUG, t), BF16)],
        compiler_params=_params("parallel"),
    )(do, oa)


def _fox_bwd_post(dqt, dkt, dvt):
    t = dqt.shape[2]
    tt = _pick(t, (256, 128))

    def body(dq_ref, dk_ref, dv_ref, out_ref, dsc_ref):
        lane = _lanes(tt)
        dqs = [dq_ref[h].T for h in range(HEADS)]
        dks = [dk_ref[h].T for h in range(HEADS)]
        heads = lambda xs: jnp.concatenate([x[:, :HEAD_DIM] for x in xs], axis=1)
        out_ref[:, 0:WIDTH] = (heads(dqs) * (HEAD_DIM ** -0.5)).astype(BF16)
        out_ref[:, WIDTH:2 * WIDTH] = heads(dks).astype(BF16)
        out_ref[:, 2 * WIDTH:] = heads([dv_ref[h].T for h in range(HEADS)]).astype(BF16)
        dsc = jnp.zeros((tt, LANES), F32)
        for h in range(HEADS):
            both = jnp.where(lane == HEAD_DIM, dqs[h], 0.0) - jnp.where(lane == HEAD_DIM + 3, dks[h], 0.0)
            dsc = jnp.where(lane == h, jnp.sum(both, axis=1, keepdims=True), dsc)
        dsc_ref[...] = dsc

    hm = pl.BlockSpec((HEADS, AUG, tt), lambda i: (0, 0, i))
    return pl.pallas_call(
        body, grid=(t // tt,), name="fox_bwd_post",
        in_specs=[hm, hm, hm],
        out_specs=[pl.BlockSpec((tt, 3 * WIDTH), lambda i: (i, 0)), pl.BlockSpec((tt, LANES), lambda i: (i, 0))],
        out_shape=[jax.ShapeDtypeStruct((t, 3 * WIDTH), BF16), jax.ShapeDtypeStruct((t, LANES), F32)],
        compiler_params=_params("parallel"),
    )(dqt, dkt, dvt)


def _fox_fwd(qa, ka, vat, tq=None):
    h, t, _ = qa.shape
    tq = tq or _pick(t, ROW_TILES)

    def body(q_ref, k_ref, vt_ref, o_ref, qb_ref, qbt_ref, s_ref):
        i = pl.program_id(1)
        q = q_ref[...]
        krow = lax.broadcasted_iota(jnp.int32, (tq, tq), 0)
        qcol = lax.broadcasted_iota(jnp.int32, (tq, tq), 1)
        rows = lambda j: pl.ds(pl.multiple_of(j * tq, tq), tq)

        def scores(j, slot):
            s_ref[slot] = lax.dot_general(k_ref[rows(j), :], q, (((1,), (1,)), ((), ())), preferred_element_type=F32)

        def update(j, slot, carry, masked):
            m, acc = carry
            s = s_ref[slot]
            if masked:
                s = jnp.where(qcol >= krow, s, NEG)
            m_new = jnp.maximum(m, jnp.max(s, axis=0, keepdims=True))
            p = jnp.exp(s - m_new)
            alpha = jnp.exp(m - m_new)
            return m_new, acc * alpha + jnp.dot(vt_ref[:, rows(j)], p.astype(BF16), preferred_element_type=F32)

        def pair(jj, carry):
            j = 2 * jj
            scores(j + 1, 1)
            carry = update(j, 0, carry, False)
            scores(j + 2, 0)
            return update(j + 1, 1, carry, False)

        def odd_tail(carry):
            scores(i, 1)
            return update(i, 1, update(i - 1, 0, carry, False), True)

        scores(0, 0)
        carry = (jnp.full((1, tq), NEG, F32), jnp.zeros((AUG, tq), F32))
        carry = lax.fori_loop(0, i // 2, pair, carry)
        m, acc = lax.cond(i % 2 == 1, odd_tail, lambda c: update(i, 0, c, True), carry)
        sub = lax.broadcasted_iota(jnp.int32, (AUG, tq), 0)
        l = jnp.sum(jnp.where(sub == HEAD_DIM, acc, 0.0), axis=0, keepdims=True)
        out = jnp.where(sub < HEAD_DIM, acc / l, m + jnp.log(l)).T
        o_ref[...] = out
        lane = lax.broadcasted_iota(jnp.int32, (tq, AUG), 1)
        lse = jnp.broadcast_to(jnp.sum(jnp.where(lane == HEAD_DIM, out, 0.0), axis=1, keepdims=True), (tq, AUG))
        hi, mid, lo = _split3_exact(-lse)
        qb = jnp.where(lane == 70, hi, jnp.where(lane == 71, mid, jnp.where(lane == 72, lo, q.astype(F32))))
        qb_ref[...] = qb.astype(BF16)
        qbt_ref[...] = qb.T.astype(BF16)

    blk = pl.BlockSpec((None, tq, AUG), lambda hh, i: (hh, i, 0))
    return pl.pallas_call(
        body, grid=(h, t // tq), name="fox_fwd",
        in_specs=[blk, pl.BlockSpec((None, t, AUG), lambda hh, i: (hh, 0, 0)),
                  pl.BlockSpec((None, AUG, t), lambda hh, i: (hh, 0, 0))],
        out_specs=[blk, blk, pl.BlockSpec((None, AUG, tq), lambda hh, i: (hh, 0, i))],
        out_shape=[jax.ShapeDtypeStruct((h, t, AUG), F32), jax.ShapeDtypeStruct((h, t, AUG), BF16),
                   jax.ShapeDtypeStruct((h, AUG, t), BF16)],
        scratch_shapes=[pltpu.VMEM((2, tq, tq), F32)],
        compiler_params=_params("parallel", "arbitrary"),
    )(qa, ka, vat)


def _fox_bwd(qb, qbt, ka, kat, va, doa, doat, tq=None):
    h, t, _ = qb.shape
    tq = tq or _pick(t, ROW_TILES)
    nq = t // tq

    def body(q_ref, qt_ref, k_ref, kt_ref, v_ref, do_ref, dot_ref, dqt_ref, dkt_ref, dvt_ref, s_ref, dp_ref):
        j = pl.program_id(1)
        n = nq - j

        @pl.when(j == 0)
        def _():
            dqt_ref[...] = jnp.zeros_like(dqt_ref)

        dkt_ref[...] = jnp.zeros_like(dkt_ref)
        dvt_ref[...] = jnp.zeros_like(dvt_ref)
        kj = k_ref[...]
        ktj = kt_ref[...]
        vj = v_ref[...]
        qrow = lax.broadcasted_iota(jnp.int32, (tq, tq), 0)
        kcol = lax.broadcasted_iota(jnp.int32, (tq, tq), 1)
        rows = lambda i: pl.ds(pl.multiple_of(i * tq, tq), tq)
        nt_dims = (((1,), (1,)), ((), ()))

        def scores(i, slot):
            s_ref[slot] = lax.dot_general(q_ref[rows(i), :], kj, nt_dims, preferred_element_type=F32)
            dp_ref[slot] = lax.dot_general(do_ref[rows(i), :], vj, nt_dims, preferred_element_type=F32)

        def update(i, slot):
            p = jnp.exp(jnp.where((qrow >= kcol) | (i > j), s_ref[slot], NEG))
            ds = (p * dp_ref[slot]).astype(BF16)
            dvt_ref[...] += jnp.dot(dot_ref[:, rows(i)], p.astype(BF16), preferred_element_type=F32)
            dkt_ref[...] += jnp.dot(qt_ref[:, rows(i)], ds, preferred_element_type=F32)
            dqt_ref[:, rows(i)] += lax.dot_general(ktj, ds, nt_dims, preferred_element_type=F32)

        def pair(kk, carry):
            i0 = j + 2 * kk
            scores(i0 + 1, 1)
            update(i0, 0)
            scores(jnp.minimum(i0 + 2, nq - 1), 0)
            update(i0 + 1, 1)
            return carry

        scores(j, 0)
        lax.fori_loop(0, n // 2, pair, 0)

        @pl.when(n % 2 == 1)
        def _():
            update(nq - 1, 0)

    once = pl.Buffered(1)
    full = pl.BlockSpec((None, t, AUG), lambda hh, j: (hh, 0, 0), pipeline_mode=once)
    full_t = pl.BlockSpec((None, AUG, t), lambda hh, j: (hh, 0, 0), pipeline_mode=once)
    blk = pl.BlockSpec((None, tq, AUG), lambda hh, j: (hh, j, 0))
    blk_t = pl.BlockSpec((None, AUG, tq), lambda hh, j: (hh, 0, j))
    shp = jax.ShapeDtypeStruct((h, AUG, t), F32)
    return pl.pallas_call(
        body, grid=(h, nq), name="fox_bwd",
        in_specs=[full, full_t, blk, blk_t, blk, full, full_t],
        out_specs=[pl.BlockSpec((None, AUG, t), lambda hh, j: (hh, 0, 0)), blk_t, blk_t], out_shape=[shp, shp, shp],
        scratch_shapes=[pltpu.VMEM((2, tq, tq), F32), pltpu.VMEM((2, tq, tq), F32)],
        compiler_params=_params("parallel", "arbitrary"),
    )(qb, qbt, ka, kat, va, doa, doat)


def _seg_matrix():
    idx = np.arange(WIDTH) // HEAD_DIM
    return jnp.asarray((idx[:, None] == idx[None, :]).astype(np.float32))


def _segsum(x, e):
    return jnp.dot(x, e, precision=HI, preferred_element_type=F32)


def _silu(x):
    return x * jax.nn.sigmoid(x)


def _silu_grad(x):
    s = jax.nn.sigmoid(x)
    return s * (1.0 + x * (1.0 - s))


def _shift_down(x, prev8, k):
    r = pltpu.roll(x, k, axis=0)
    p = pltpu.roll(prev8, k, axis=0)
    row = lax.broadcasted_iota(jnp.int32, prev8.shape, 0)
    head = jnp.where(row < k, p, r[:8])
    return jnp.concatenate([head, r[8:]], axis=0)


def _shift_up(x, next8, k):
    n = x.shape[0]
    r = pltpu.roll(x, n - k, axis=0)
    p = pltpu.roll(next8, 8 - k, axis=0)
    row = lax.broadcasted_iota(jnp.int32, next8.shape, 0)
    tail = jnp.where(row >= 8 - k, p, r[n - 8:])
    return jnp.concatenate([r[:n - 8], tail], axis=0)


def _causal_conv(x, prev8, w_ref, width, cols=slice(None)):
    y = x * w_ref[width - 1:width, cols]
    for k in range(1, width):
        y = y + _shift_down(x, prev8, k) * w_ref[width - 1 - k:width - k, cols]
    return y


def _causal_conv_bwd(x, prev8, dy, dnext8, w_ref, dw_ref, width, cols=slice(None)):
    dx = dy * w_ref[width - 1:width, cols]
    dw_ref[width - 1:width, cols] += jnp.sum(dy * x, axis=0, keepdims=True)
    for k in range(1, width):
        dx = dx + _shift_up(dy, dnext8, k) * w_ref[width - 1 - k:width - k, cols]
        dw_ref[width - 1 - k:width - k, cols] += jnp.sum(dy * _shift_down(x, prev8, k), axis=0, keepdims=True)
    return dx


HALO = 16


def _prev_spec(tt, width, tile=lambda i: i):
    return pl.BlockSpec((HALO, width), lambda i: (jnp.maximum(tile(i) * (tt // HALO) - 1, 0), 0))


def _prev8(p_ref, cols=slice(None)):
    return p_ref[:, cols].astype(F32)[HALO - 8:]


def _store_heads(ref, x):
    for h in range(HEADS):
        ref[h] = x[:, HEAD_DIM * h:HEAD_DIM * (h + 1)]


def _load_heads(ref):
    return jnp.concatenate([ref[h] for h in range(HEADS)], axis=1)


def _softplus(z):
    return jnp.maximum(z, 0.0) + jnp.log1p(jnp.exp(-jnp.abs(z)))


def _tri_masks(tt):
    r = lax.broadcasted_iota(jnp.int32, (tt, tt), 0)
    c = lax.broadcasted_iota(jnp.int32, (tt, tt), 1)
    same_chunk = lax.shift_right_logical(r, 6) == lax.shift_right_logical(c, 6)
    return r, c, same_chunk


def _gate_fwd(small, pbias, pscale):
    t = small.shape[0]
    tt = _pick(t, (256, 128))

    def body(x_ref, pb_ref, ps_ref, o_ref, carry_ref):
        @pl.when(pl.program_id(0) == 0)
        def _():
            carry_ref[...] = jnp.zeros_like(carry_ref)

        lane = lax.broadcasted_iota(jnp.int32, (tt, LANES), 1)
        z = x_ref[...] + pb_ref[...]
        log_f = jnp.where(lane < HEADS, -_softplus(-z), 0.0)
        g = jnp.where((lane >= 2 * HEADS) & (lane < 3 * HEADS), ps_ref[...] * _softplus(z), 0.0)
        r, c, same_chunk = _tri_masks(tt)
        lower = jnp.where(r >= c, 1.0, 0.0)
        lower_chunk = jnp.where((r >= c) & same_chunk, 1.0, 0.0)
        csum = jnp.dot(lower, log_f, precision=lax.Precision.HIGHEST, preferred_element_type=F32) + carry_ref[...]
        gc = jnp.dot(lower_chunk, g, precision=lax.Precision.HIGHEST, preferred_element_type=F32)
        carry_ref[...] += jnp.sum(log_f, axis=0, keepdims=True)
        o_ref[...] = jnp.where(lane < HEADS, csum, jnp.where(lane < 2 * HEADS, jax.nn.sigmoid(z), gc))

    row = pl.BlockSpec((tt, LANES), lambda i: (i, 0))
    vec = pl.BlockSpec((1, LANES), lambda i: (0, 0))
    return pl.pallas_call(
        body, grid=(t // tt,), name="gate_fwd", in_specs=[row, vec, vec], out_specs=row,
        out_shape=jax.ShapeDtypeStruct((t, LANES), F32),
        scratch_shapes=[pltpu.VMEM((1, LANES), F32)],
        compiler_params=_params("arbitrary"),
    )(small, pbias, pscale)


def _gate_bwd(small, pbias, pscale, dscal):
    t = small.shape[0]
    tt = _pick(t, (256, 128))
    nt = t // tt

    def body(x_ref, pb_ref, ps_ref, d_ref, dx_ref, dpb_ref, dps_ref, carry_ref):
        @pl.when(pl.program_id(0) == 0)
        def _():
            carry_ref[...] = jnp.zeros_like(carry_ref)
            dpb_ref[...] = jnp.zeros_like(dpb_ref)
            dps_ref[...] = jnp.zeros_like(dps_ref)

        lane = lax.broadcasted_iota(jnp.int32, (tt, LANES), 1)
        z = x_ref[...] + pb_ref[...]
        d = d_ref[...]
        dc = jnp.where(lane < HEADS, d, 0.0)
        dbeta = jnp.where((lane >= HEADS) & (lane < 2 * HEADS), d, 0.0)
        dgc = jnp.where((lane >= 2 * HEADS) & (lane < 3 * HEADS), d, 0.0)
        r, c, same_chunk = _tri_masks(tt)
        upper = jnp.where(r <= c, 1.0, 0.0)
        upper_chunk = jnp.where((r <= c) & same_chunk, 1.0, 0.0)
        dlogf = jnp.dot(upper, dc, precision=lax.Precision.HIGHEST, preferred_element_type=F32) + carry_ref[...]
        dg = jnp.dot(upper_chunk, dgc, precision=lax.Precision.HIGHEST, preferred_element_type=F32)
        carry_ref[...] += jnp.sum(dc, axis=0, keepdims=True)
        sg = jax.nn.sigmoid(z)
        dz = dlogf * (1.0 - sg) + dbeta * sg * (1.0 - sg) + dg * ps_ref[...] * sg
        dx_ref[...] = dz.astype(dx_ref.dtype)
        dpb_ref[...] += jnp.sum(dz, axis=0, keepdims=True)
        dps_ref[...] += jnp.sum(dg * _softplus(z), axis=0, keepdims=True)

    row = pl.BlockSpec((tt, LANES), lambda i: (nt - 1 - i, 0))
    vec = pl.BlockSpec((1, LANES), lambda i: (0, 0))
    return pl.pallas_call(
        body, grid=(nt,), name="gate_bwd", in_specs=[row, vec, vec, row], out_specs=[row, vec, vec],
        out_shape=[jax.ShapeDtypeStruct((t, LANES), BF16), jax.ShapeDtypeStruct((1, LANES), F32),
                   jax.ShapeDtypeStruct((1, LANES), F32)],
        scratch_shapes=[pltpu.VMEM((1, LANES), F32)],
        compiler_params=_params("arbitrary"),
    )(small, pbias, pscale, dscal)


def _gdn_pre_fwd(xg, conv_w, seg):
    t = xg.shape[0]
    c3 = 3 * WIDTH
    tt = _pick(t, (320, 256, 128))

    def body(x_ref, p_ref, w_ref, e_ref, q_ref, k_ref, v_ref):
        x = x_ref[...].astype(F32)
        prev = jnp.where(pl.program_id(0) == 0, 0.0, _prev8(p_ref))
        s = _silu(_causal_conv(x, prev, w_ref, GDN_CONV))
        e = e_ref[...]
        q = s[:, :WIDTH]
        k = s[:, WIDTH:2 * WIDTH]
        _store_heads(q_ref, q * lax.rsqrt(_segsum(q * q, e) + RMS_EPS) * (HEAD_DIM ** -0.5))
        _store_heads(k_ref, k * lax.rsqrt(_segsum(k * k, e) + RMS_EPS))
        _store_heads(v_ref, s[:, 2 * WIDTH:])

    out = pl.BlockSpec((HEADS, tt, HEAD_DIM), lambda i: (0, i, 0))
    shp = jax.ShapeDtypeStruct((HEADS, t, HEAD_DIM), F32)
    return pl.pallas_call(
        body, grid=(t // tt,), name="gdn_pre_fwd",
        in_specs=[pl.BlockSpec((tt, c3), lambda i: (i, 0)), _prev_spec(tt, c3),
                  pl.BlockSpec((GDN_CONV, c3), lambda i: (0, 0)), pl.BlockSpec((WIDTH, WIDTH), lambda i: (0, 0))],
        out_specs=[out, out, out], out_shape=[shp, shp, shp],
        compiler_params=_params("arbitrary"),
    )(xg, xg, conv_w, seg)


def _gdn_pre_bwd(xg, conv_w, seg, dqn, dkn, dv):
    t = xg.shape[0]
    c3 = 3 * WIDTH
    tt = _pick(t, (320, 256, 128))
    nt = t // tt

    def body(x_ref, p_ref, w_ref, e_ref, dq_ref, dk_ref, dv_ref, dx_ref, dw_ref, carry_ref):
        step = pl.program_id(0)
        x = x_ref[...].astype(F32)
        e = e_ref[...]
        prev = jnp.where(step == nt - 1, 0.0, _prev8(p_ref))
        y = _causal_conv(x, prev, w_ref, GDN_CONV)
        s = _silu(y)
        q = s[:, :WIDTH]
        k = s[:, WIDTH:2 * WIDTH]
        rq = lax.rsqrt(_segsum(q * q, e) + RMS_EPS)
        rk = lax.rsqrt(_segsum(k * k, e) + RMS_EPS)
        gq = _load_heads(dq_ref) * (HEAD_DIM ** -0.5)
        gk = _load_heads(dk_ref)
        dq = rq * gq - q * (rq * rq * rq) * _segsum(gq * q, e)
        dk = rk * gk - k * (rk * rk * rk) * _segsum(gk * k, e)
        dy = jnp.concatenate([dq, dk, _load_heads(dv_ref)], axis=1) * _silu_grad(y)

        @pl.when(step == 0)
        def _():
            carry_ref[...] = jnp.zeros_like(carry_ref)
            dw_ref[...] = jnp.zeros_like(dw_ref)

        dx = _causal_conv_bwd(x, prev, dy, carry_ref[...], w_ref, dw_ref, GDN_CONV)
        dx_ref[...] = dx.astype(dx_ref.dtype)
        carry_ref[...] = dy[:8]

    rev = lambda i: (nt - 1 - i, 0)
    blk = pl.BlockSpec((HEADS, tt, HEAD_DIM), lambda i: (0, nt - 1 - i, 0))
    return pl.pallas_call(
        body, grid=(nt,), name="gdn_pre_bwd",
        in_specs=[pl.BlockSpec((tt, c3), rev), _prev_spec(tt, c3, lambda i: nt - 1 - i),
                  pl.BlockSpec((GDN_CONV, c3), lambda i: (0, 0)), pl.BlockSpec((WIDTH, WIDTH), lambda i: (0, 0)),
                  blk, blk, blk],
        out_specs=[pl.BlockSpec((tt, c3), rev), pl.BlockSpec((GDN_CONV, c3), lambda i: (0, 0))],
        out_shape=[jax.ShapeDtypeStruct((t, c3), BF16), jax.ShapeDtypeStruct((GDN_CONV, c3), F32)],
        scratch_shapes=[pltpu.VMEM((8, c3), F32)],
        compiler_params=_params("arbitrary"),
    )(xg, xg, conv_w, seg, dqn, dkn, dv)


def _bmm(a, b, ca, cb, precision=None):
    return lax.dot_general(a, b, (((ca,), (cb,)), ((0,), (0,))), precision=precision, preferred_element_type=F32)


def _bf(x):
    return x.astype(BF16)


def _tri_inverse(a, eye):
    x = -a
    tinv = eye + x
    pw = x
    for _ in range(5):
        pb = _bf(pw)
        pw = _bmm(pb, pb, 2, 1)
        tinv = tinv + _bmm(_bf(tinv), _bf(pw), 2, 1)
    resid = eye - _bmm(eye + a, tinv, 2, 1, precision=HI)
    return tinv + _bmm(_bf(tinv), _bf(resid), 2, 1)


def _gdn_intra(q, k, v, bc, gcc, gcr):
    ii = lax.broadcasted_iota(jnp.int32, (CHUNK, CHUNK), 0)
    jj = lax.broadcasted_iota(jnp.int32, (CHUNK, CHUNK), 1)
    tril = (ii >= jj)[None]
    strict = (ii > jj)[None]
    eye = jnp.where(ii == jj, 1.0, 0.0).astype(F32)[None]
    last = (ii == CHUNK - 1)[None]
    dm = jnp.exp(jnp.where(tril, gcc - gcr, NEG))
    gam = jnp.exp(gcc)
    kb = k * bc
    vb = v * bc
    kk = _bmm(_bf(kb), _bf(k), 2, 2)
    a = jnp.where(strict, kk * dm, 0.0)
    tinv = _tri_inverse(a, eye)
    kbg = kb * gam
    u = _bmm(tinv, vb, 2, 1, precision=HI)
    wk = _bmm(tinv, kbg, 2, 1, precision=HI)
    qk = _bmm(_bf(q), _bf(k), 2, 2)
    p = jnp.where(tril, qk * dm, 0.0)
    gl = jnp.sum(jnp.where(last, gcc, 0.0), axis=1, keepdims=True)
    edec = jnp.exp(gl - gcc)
    return dict(tril=tril, strict=strict, dm=dm, gam=gam, kb=kb, kk=kk, a=a, tinv=tinv, u=u, wk=wk, qk=qk, p=p,
                qg=q * gam, kt=k * edec, edec=edec, gaml=jnp.exp(gl), last=last)


def _gate_tiles(sc, gct, nb):
    rows = nb * CHUNK
    cols = lambda lane0: jnp.stack([jnp.broadcast_to(sc[:, lane0 + h:lane0 + h + 1], (rows, HEAD_DIM))
                                    for h in range(HEADS)], axis=0).reshape(HEADS * nb, CHUNK, HEAD_DIM)
    gcr = jnp.stack([jnp.broadcast_to(gct[h:h + 1, n * CHUNK:(n + 1) * CHUNK], (CHUNK, CHUNK))
                     for h in range(HEADS) for n in range(nb)], axis=0)
    return cols(HEADS), cols(2 * HEADS), gcr


def _gdn_fwd(q, k, v, scal, gct, nb=None):
    h, t, dh = q.shape
    nc = t // CHUNK
    nb = nb or _pick(nc, (4, 2))
    bsz = h * nb

    def body(q_ref, k_ref, v_ref, sc_ref, gt_ref, o_ref, s0_ref, state_ref):
        @pl.when(pl.program_id(0) == 0)
        def _():
            state_ref[...] = jnp.zeros_like(state_ref)

        ld = lambda r: r[...].reshape(bsz, CHUNK, dh)
        bc, gcc, gcr = _gate_tiles(sc_ref[...], gt_ref[...], nb)
        z = _gdn_intra(ld(q_ref), ld(k_ref), ld(v_ref), bc, gcc, gcr)
        per = lambda x: x.reshape((h, nb) + x.shape[1:])
        u, wk, p, qg, kt, gaml = (per(z[n]) for n in ("u", "wk", "p", "qg", "kt", "gaml"))
        s = state_ref[...]
        for n in range(nb):
            s0_ref[:, n] = s
            sb = _bf(s)
            vn = u[:, n] - _bmm(_bf(wk[:, n]), sb, 2, 1)
            o_ref[:, n * CHUNK:(n + 1) * CHUNK, :] = _bmm(_bf(qg[:, n]), sb, 2, 1) + _bmm(_bf(p[:, n]), _bf(vn), 2, 1)
            s = s * gaml[:, n] + _bmm(_bf(kt[:, n]), _bf(vn), 1, 1)
        state_ref[...] = s

    blk = pl.BlockSpec((h, nb * CHUNK, dh), lambda i: (0, i, 0))
    return pl.pallas_call(
        body, grid=(nc // nb,), name="gdn_fwd",
        in_specs=[blk] * 3 + [pl.BlockSpec((nb * CHUNK, LANES), lambda i: (i, 0)),
                              pl.BlockSpec((h, nb * CHUNK), lambda i: (0, i))],
        out_specs=[blk, pl.BlockSpec((h, nb, dh, dh), lambda i: (0, i, 0, 0))],
        out_shape=[jax.ShapeDtypeStruct((h, t, dh), F32), jax.ShapeDtypeStruct((h, nc, dh, dh), F32)],
        scratch_shapes=[pltpu.VMEM((h, dh, dh), F32)],
        compiler_params=_params("arbitrary"),
    )(q, k, v, scal, gct)


def _gdn_bwd(q, k, v, scal, gct, s0s, do, nb=None):
    h, t, dh = q.shape
    nc = t // CHUNK
    nb = nb or _pick(nc, (2,))
    bsz = h * nb
    ng = nc // nb
    rows = nb * CHUNK

    def body(q_ref, k_ref, v_ref, sc_ref, gt_ref, s0_ref, do_ref,
             dq_ref, dk_ref, dv_ref, dsc_ref, dgt_ref, ds_ref):
        @pl.when(pl.program_id(0) == 0)
        def _():
            ds_ref[...] = jnp.zeros_like(ds_ref)

        ld = lambda r: r[...].reshape(bsz, CHUNK, dh)
        q, k, v = ld(q_ref), ld(k_ref), ld(v_ref)
        bc, gcc, gcr = _gate_tiles(sc_ref[...], gt_ref[...], nb)
        z = _gdn_intra(q, k, v, bc, gcc, gcr)
        per = lambda x: x.reshape((h, nb) + x.shape[1:])
        u, wk, p, qg, kt, gaml = (per(z[n]) for n in ("u", "wk", "p", "qg", "kt", "gaml"))
        dout = per(ld(do_ref))
        ds = ds_ref[...]
        d_u, d_wk, d_p, d_qg, d_kt, d_gaml = ([None] * nb for _ in range(6))
        for n in reversed(range(nb)):
            s0 = s0_ref[:, n]
            s0b, dsb, dob = _bf(s0), _bf(ds), _bf(dout[:, n])
            wkb, qgb = _bf(wk[:, n]), _bf(qg[:, n])
            vn = u[:, n] - _bmm(wkb, s0b, 2, 1)
            dvn = _bmm(_bf(p[:, n]), dob, 1, 1) + _bmm(_bf(kt[:, n]), dsb, 2, 1)
            dvnb = _bf(dvn)
            d_u[n] = dvn
            d_p[n] = _bmm(dob, _bf(vn), 2, 2)
            d_qg[n] = _bmm(dob, s0b, 2, 2)
            d_kt[n] = _bmm(_bf(vn), dsb, 2, 2)
            d_gaml[n] = jnp.sum(s0 * ds, axis=1, keepdims=True)
            d_wk[n] = -_bmm(dvnb, s0b, 2, 2)
            ds = _bmm(qgb, dob, 1, 1) + gaml[:, n] * ds - _bmm(wkb, dvnb, 1, 1)
        ds_ref[...] = ds

        flat = lambda xs: jnp.stack(xs, axis=1).reshape((bsz,) + xs[0].shape[1:])
        d_u, d_wk, d_p, d_qg, d_kt, d_gaml = (flat(x) for x in (d_u, d_wk, d_p, d_qg, d_kt, d_gaml))
        tinv, gam, kb, dm = z["tinv"], z["gam"], z["kb"], z["dm"]
        drv = _bmm(tinv, d_u, 1, 1, precision=HI)
        drk = _bmm(tinv, d_wk, 1, 1, precision=HI)
        da = -(_bmm(_bf(drv), _bf(z["u"]), 2, 2) + _bmm(_bf(drk), _bf(z["wk"]), 2, 2))
        da = jnp.where(z["strict"], da, 0.0)
        d_p = jnp.where(z["tril"], d_p, 0.0)
        dkk = _bf(da * dm)
        dqk = _bf(d_p * dm)
        dkb = _bmm(dkk, _bf(k), 2, 1) + drk * gam
        dk = _bmm(dkk, _bf(kb), 1, 1) + _bmm(dqk, _bf(q), 1, 1) + dkb * bc + d_kt * z["edec"]
        dq = _bmm(dqk, _bf(k), 2, 1) + d_qg * gam
        mm = da * z["a"] + d_p * z["p"]
        dkt_kt = d_kt * z["kt"]
        dgl = jnp.sum(dkt_kt, axis=1, keepdims=True) + d_gaml * z["gaml"]
        dgc = mm + d_qg * z["qg"] + drk * kb * gam - dkt_kt + jnp.where(z["last"], dgl, 0.0)
        dq_ref[...] = dq.reshape(h, rows, dh)
        dk_ref[...] = dk.reshape(h, rows, dh)
        dv_ref[...] = (drv * bc).reshape(h, rows, dh)
        dbeta = (dkb * k + drv * v).reshape(h, rows, dh)
        dgc = dgc.reshape(h, rows, dh)
        lane = lax.broadcasted_iota(jnp.int32, (rows, LANES), 1)
        dsc = jnp.zeros((rows, LANES), F32)
        for hh in range(h):
            dsc = jnp.where(lane == HEADS + hh, jnp.sum(dbeta[hh], axis=1, keepdims=True), dsc)
            dsc = jnp.where(lane == 2 * HEADS + hh, jnp.sum(dgc[hh], axis=1, keepdims=True), dsc)
        dsc_ref[...] = dsc
        dgr = -jnp.sum(mm, axis=1, keepdims=True)
        for hh in range(h):
            for n in range(nb):
                dgt_ref[hh:hh + 1, n * CHUNK:(n + 1) * CHUNK] = dgr[hh * nb + n]

    blk = pl.BlockSpec((h, rows, dh), lambda i: (0, ng - 1 - i, 0))
    shp = jax.ShapeDtypeStruct((h, t, dh), F32)
    sc_spec = pl.BlockSpec((rows, LANES), lambda i: (ng - 1 - i, 0))
    gt_spec = pl.BlockSpec((h, rows), lambda i: (0, ng - 1 - i))
    return pl.pallas_call(
        body, grid=(ng,), name="gdn_bwd",
        in_specs=[blk] * 3 + [sc_spec, gt_spec, pl.BlockSpec((h, nb, dh, dh), lambda i: (0, ng - 1 - i, 0, 0)), blk],
        out_specs=[blk] * 3 + [sc_spec, gt_spec],
        out_shape=[shp] * 3 + [jax.ShapeDtypeStruct((t, LANES), F32), jax.ShapeDtypeStruct((h, t), F32)],
        scratch_shapes=[pltpu.VMEM((h, dh, dh), F32)],
        compiler_params=_params("arbitrary"),
    )(q, k, v, scal, gct, s0s, do)


def _gdn_post_fwd(o, xg, gain, seg):
    t = o.shape[1]
    tt = _pick(t, (320, 256, 128))

    def body(o_ref, z_ref, g_ref, e_ref, y_ref):
        x = _load_heads(o_ref)
        r = lax.rsqrt(_segsum(x * x, e_ref[...]) * (1.0 / HEAD_DIM) + RMS_EPS)
        y_ref[...] = (x * r * g_ref[...] * _silu(z_ref[...].astype(F32))).astype(y_ref.dtype)

    return pl.pallas_call(
        body, grid=(t // tt,), name="gdn_post_fwd",
        in_specs=[pl.BlockSpec((HEADS, tt, HEAD_DIM), lambda i: (0, i, 0)), pl.BlockSpec((tt, WIDTH), lambda i: (i, 3)),
                  pl.BlockSpec((1, WIDTH), lambda i: (0, 0)), pl.BlockSpec((WIDTH, WIDTH), lambda i: (0, 0))],
        out_specs=pl.BlockSpec((tt, WIDTH), lambda i: (i, 0)),
        out_shape=jax.ShapeDtypeStruct((t, WIDTH), BF16),
        compiler_params=_params("arbitrary"),
    )(o, xg, gain, seg)


def _gdn_post_bwd(o, xg, gain, seg, dy):
    t = o.shape[1]
    tt = _pick(t, (320, 256, 128))

    def body(o_ref, z_ref, g_ref, e_ref, dy_ref, do_ref, dz_ref, dg_ref):
        x = _load_heads(o_ref)
        zz = z_ref[...].astype(F32)
        e = e_ref[...]
        gain_v = g_ref[...]
        d = dy_ref[...]
        r = lax.rsqrt(_segsum(x * x, e) * (1.0 / HEAD_DIM) + RMS_EPS)
        xr = x * r
        don = d * _silu(zz)
        dz_ref[...] = (d * xr * gain_v * _silu_grad(zz)).astype(dz_ref.dtype)
        gy = don * gain_v
        _store_heads(do_ref, r * gy - xr * (r * r) * (_segsum(gy * x, e) * (1.0 / HEAD_DIM)))

        @pl.when(pl.program_id(0) == 0)
        def _():
            dg_ref[...] = jnp.zeros_like(dg_ref)

        dg_ref[...] += jnp.sum(don * xr, axis=0, keepdims=True)

    row = pl.BlockSpec((tt, WIDTH), lambda i: (i, 0))
    vec = pl.BlockSpec((1, WIDTH), lambda i: (0, 0))
    hm = pl.BlockSpec((HEADS, tt, HEAD_DIM), lambda i: (0, i, 0))
    return pl.pallas_call(
        body, grid=(t // tt,), name="gdn_post_bwd",
        in_specs=[hm, pl.BlockSpec((tt, WIDTH), lambda i: (i, 3)), vec,
                  pl.BlockSpec((WIDTH, WIDTH), lambda i: (0, 0)), row],
        out_specs=[hm, row, vec],
        out_shape=[jax.ShapeDtypeStruct((HEADS, t, HEAD_DIM), F32), jax.ShapeDtypeStruct((t, WIDTH), BF16),
                   jax.ShapeDtypeStruct((1, WIDTH), F32)],
        compiler_params=_params("arbitrary"),
    )(o, xg, gain, seg, dy)


def _mix_fwd(yf, yg, gates, bias):
    t, d = yf.shape
    tt = _pick(t, (320, 256, 128))

    def body(yf_ref, yg_ref, g1_ref, g2_ref, b1_ref, b2_ref, o_ref):
        g1 = jax.nn.sigmoid(g1_ref[...].astype(F32) + b1_ref[...])
        g2 = jax.nn.sigmoid(g2_ref[...].astype(F32) + b2_ref[...])
        o_ref[...] = (g1 * yf_ref[...] + g2 * yg_ref[...]).astype(o_ref.dtype)

    row = pl.BlockSpec((tt, d), lambda i: (i, 0))
    return pl.pallas_call(
        body, grid=(t // tt,), name="mix_fwd",
        in_specs=[row, row, row, pl.BlockSpec((tt, d), lambda i: (i, 1)),
                  pl.BlockSpec((1, d), lambda i: (0, 0)), pl.BlockSpec((1, d), lambda i: (0, 1))],
        out_specs=row, out_shape=jax.ShapeDtypeStruct((t, d), BF16),
        compiler_params=_params("arbitrary"),
    )(yf, yg, gates, gates, bias, bias)


def _mix_bwd(dmix, yf, yg, gates, bias):
    t, d = yf.shape
    tt = _pick(t, (320, 256, 128))

    def body(dm_ref, yf_ref, yg_ref, g1_ref, g2_ref, b1_ref, b2_ref, dyf_ref, dyg_ref, dg_ref, db_ref):
        dm = dm_ref[...]
        g1 = jax.nn.sigmoid(g1_ref[...].astype(F32) + b1_ref[...])
        g2 = jax.nn.sigmoid(g2_ref[...].astype(F32) + b2_ref[...])
        dyf_ref[...] = (dm * g1).astype(BF16)
        dyg_ref[...] = (dm * g2).astype(BF16)
        dgate = jnp.concatenate([dm * yf_ref[...] * g1 * (1.0 - g1), dm * yg_ref[...] * g2 * (1.0 - g2)], axis=1)
        dg_ref[...] = dgate.astype(BF16)

        @pl.when(pl.program_id(0) == 0)
        def _():
            db_ref[...] = jnp.zeros_like(db_ref)

        db_ref[...] += jnp.sum(dgate, axis=0, keepdims=True)

    row = pl.BlockSpec((tt, d), lambda i: (i, 0))
    wide = pl.BlockSpec((tt, 2 * d), lambda i: (i, 0))
    return pl.pallas_call(
        body, grid=(t // tt,), name="mix_bwd",
        in_specs=[row, row, row, row, pl.BlockSpec((tt, d), lambda i: (i, 1)),
                  pl.BlockSpec((1, d), lambda i: (0, 0)), pl.BlockSpec((1, d), lambda i: (0, 1))],
        out_specs=[row, row, wide, pl.BlockSpec((1, 2 * d), lambda i: (0, 0))],
        out_shape=[jax.ShapeDtypeStruct((t, d), BF16), jax.ShapeDtypeStruct((t, d), BF16),
                   jax.ShapeDtypeStruct((t, 2 * d), BF16), jax.ShapeDtypeStruct((1, 2 * d), F32)],
        compiler_params=_params("arbitrary"),
    )(dmix, yf, yg, gates, gates, bias, bias)


def _ffn_act_fwd(up, conv_w, conv_b):
    t, c = up.shape
    tt = 128

    def body(x_ref, p_ref, w_ref, b_ref, o_ref):
        first = pl.program_id(0) == 0

        def conv(cols):
            prev = jnp.where(first, 0.0, _prev8(p_ref, cols))
            return _causal_conv(x_ref[:, cols].astype(F32), prev, w_ref, FFN_CONV, cols) + b_ref[:, cols]

        for lo in range(0, D_FF, FFN_LANES):
            gate = conv(slice(lo, lo + FFN_LANES))
            val = conv(slice(D_FF + lo, D_FF + lo + FFN_LANES))
            o_ref[:, lo:lo + FFN_LANES] = (_silu(gate) * val).astype(o_ref.dtype)

    return pl.pallas_call(
        body, grid=(t // tt,), name="ffn_act_fwd",
        in_specs=[pl.BlockSpec((tt, c), lambda i: (i, 0)), _prev_spec(tt, c),
                  pl.BlockSpec((FFN_CONV, c), lambda i: (0, 0)), pl.BlockSpec((1, c), lambda i: (0, 0))],
        out_specs=pl.BlockSpec((tt, D_FF), lambda i: (i, 0)),
        out_shape=jax.ShapeDtypeStruct((t, D_FF), BF16),
        compiler_params=_params("arbitrary"),
    )(up, up, conv_w, conv_b)


def _ffn_act_bwd(up, conv_w, conv_b, dact):
    t, c = up.shape
    tt = 128
    nt = t // tt

    def body(x_ref, p_ref, w_ref, b_ref, da_ref, dx_ref, dw_ref, db_ref, carry_ref):
        step = pl.program_id(0)

        @pl.when(step == 0)
        def _():
            carry_ref[...] = jnp.zeros_like(carry_ref)
            dw_ref[...] = jnp.zeros_like(dw_ref)
            db_ref[...] = jnp.zeros_like(db_ref)

        def conv(cols):
            x = x_ref[:, cols].astype(F32)
            prev = jnp.where(step == nt - 1, 0.0, _prev8(p_ref, cols))
            return x, prev, _causal_conv(x, prev, w_ref, FFN_CONV, cols) + b_ref[:, cols]

        def back(cols, x, prev, du):
            dx = _causal_conv_bwd(x, prev, du, carry_ref[:, cols], w_ref, dw_ref, FFN_CONV, cols)
            dx_ref[:, cols] = dx.astype(dx_ref.dtype)
            db_ref[:, cols] += jnp.sum(du, axis=0, keepdims=True)
            carry_ref[:, cols] = du[:8]

        for lo in range(0, D_FF, FFN_LANES):
            gcols, vcols = slice(lo, lo + FFN_LANES), slice(D_FF + lo, D_FF + lo + FFN_LANES)
            xg, pg, gate = conv(gcols)
            xv, pv, val = conv(vcols)
            da = da_ref[:, gcols]
            back(gcols, xg, pg, da * val * _silu_grad(gate))
            back(vcols, xv, pv, da * _silu(gate))

    rev = lambda i: (nt - 1 - i, 0)
    return pl.pallas_call(
        body, grid=(nt,), name="ffn_act_bwd",
        in_specs=[pl.BlockSpec((tt, c), rev),
                  _prev_spec(tt, c, lambda i: nt - 1 - i),
                  pl.BlockSpec((FFN_CONV, c), lambda i: (0, 0)), pl.BlockSpec((1, c), lambda i: (0, 0)),
                  pl.BlockSpec((tt, D_FF), rev)],
        out_specs=[pl.BlockSpec((tt, c), rev), pl.BlockSpec((FFN_CONV, c), lambda i: (0, 0)),
                   pl.BlockSpec((1, c), lambda i: (0, 0))],
        out_shape=[jax.ShapeDtypeStruct((t, c), BF16), jax.ShapeDtypeStruct((FFN_CONV, c), F32),
                   jax.ShapeDtypeStruct((1, c), F32)],
        scratch_shapes=[pltpu.VMEM((8, c), F32)],
        compiler_params=_params("arbitrary"),
    )(up, up, conv_w, conv_b, dact)


def _final_loss(h2, target, gain, seq):
    t, d = h2.shape
    tr = _pick(t, (320, 256, 128))

    def body(h_ref, t_ref, g_ref, loss_ref, dh_ref, dhb_ref, dg_ref):
        i = pl.program_id(0)
        x = h_ref[...]
        gain_v = g_ref[...]
        r = lax.rsqrt(jnp.mean(x * x, axis=-1, keepdims=True) + RMS_EPS)
        xr = x * r
        rows = i * tr + lax.broadcasted_iota(jnp.int32, (tr, 1), 0)
        real = (rows >= N_META) & (rows < N_META + seq)
        err = jnp.where(real, xr * gain_v - t_ref[...], 0.0)
        dy = err * (1.0 / d)
        gy = dy * gain_v
        dh = r * (gy - xr * jnp.mean(gy * xr, axis=-1, keepdims=True))
        dh_ref[...] = dh
        dhb_ref[...] = dh.astype(BF16)

        @pl.when(i == 0)
        def _():
            loss_ref[...] = jnp.zeros_like(loss_ref)
            dg_ref[...] = jnp.zeros_like(dg_ref)

        part = jnp.sum(jnp.sum(err * err, axis=-1, keepdims=True), axis=0, keepdims=True)
        loss_ref[...] += jnp.broadcast_to(part * (0.5 / d), loss_ref.shape)
        dg_ref[...] += jnp.sum(dy * xr, axis=0, keepdims=True)

    row = pl.BlockSpec((tr, d), lambda i: (i, 0))
    vec = pl.BlockSpec((1, d), lambda i: (0, 0))
    return pl.pallas_call(
        body, grid=(t // tr,), name="final_loss",
        in_specs=[row, row, vec],
        out_specs=[pl.BlockSpec((1, LANES), lambda i: (0, 0)), row, row, vec],
        out_shape=[jax.ShapeDtypeStruct((1, LANES), F32), jax.ShapeDtypeStruct((t, d), F32),
                   jax.ShapeDtypeStruct((t, d), BF16), jax.ShapeDtypeStruct((1, d), F32)],
        compiler_params=_params("arbitrary"),
    )(h2, target, gain)


ADAM_TILE_BYTES = 1 << 20


def _adamw(w, m, v, grecv, name):
    r, cols = w.shape
    tr = r
    if r * cols * 4 > ADAM_TILE_BYTES:
        tr = max(d for d in range(8, r + 1, 8) if r % d == 0 and d * cols * 4 <= ADAM_TILE_BYTES)

    def body(w_ref, m_ref, v_ref, g_ref, go_ref, d_ref, mo_ref, vo_ref):
        g = g_ref[0].astype(F32)
        for s in range(1, N_DEV):
            g = g + g_ref[s].astype(F32)
        wv = w_ref[...]
        mn = ADAM_B1 * m_ref[...] + (1.0 - ADAM_B1) * g
        vn = ADAM_B2 * v_ref[...] + (1.0 - ADAM_B2) * (g * g)
        m_hat = mn / (1.0 - ADAM_B1 ** ADAM_STEP)
        v_hat = vn / (1.0 - ADAM_B2 ** ADAM_STEP)
        go_ref[...] = g
        d_ref[...] = -ADAM_LR * (m_hat / (jnp.sqrt(v_hat) + ADAM_EPS) + ADAM_WD * wv)
        mo_ref[...] = mn
        vo_ref[...] = vn

    row = pl.BlockSpec((tr, cols), lambda i: (i, 0))
    shp = jax.ShapeDtypeStruct((r, cols), F32)
    return pl.pallas_call(
        body, grid=(r // tr,), name=name,
        in_specs=[row, row, row, pl.BlockSpec((N_DEV, tr, cols), lambda i: (0, i, 0))],
        out_specs=[row] * 4, out_shape=[shp] * 4,
        compiler_params=_params("parallel"),
    )(w, m, v, grecv)


def _mesh_pos():
    return lax.axis_index("x"), lax.axis_index("y"), lax.axis_index("c")


def _all_gather(shards):
    n = len(shards)

    def body(*refs):
        x_refs, out_refs = refs[:n], refs[n:2 * n]
        send_sems, recv_sems, local_sems = refs[2 * n:]
        x, y, c = _mesh_pos()
        me, sibling = (x, y, c), (x, y, 1 - c)
        chips = [(1 - x, y), (x, 1 - y), (1 - x, 1 - y)]

        def slot(a, px, py, pc):
            return out_refs[a].at[4 * px + 2 * py + pc]

        def copy(a, kk, block, to, src=None):
            return pltpu.make_async_remote_copy(
                src_ref=slot(a, *block) if src is None else src, dst_ref=slot(a, *block),
                send_sem=send_sems.at[7 * a + kk], recv_sem=recv_sems.at[7 * a + kk],
                device_id=to, device_id_type=MESH_ID)

        mine = [pltpu.make_async_copy(x_refs[a], slot(a, *me), local_sems.at[a]) for a in range(n)]
        first = []
        for a in range(n):
            first.append(copy(a, 0, me, sibling, src=x_refs[a]))
            first += [copy(a, 1 + j, me, (*chip, c), src=x_refs[a]) for j, chip in enumerate(chips)]
        for cp in mine + first:
            cp.start()
        passed = []
        for j, chip in enumerate(chips):
            for a in range(n):
                copy(a, 1 + j, (*chip, c), me).wait_recv()
                passed.append(copy(a, 4 + j, (*chip, c), sibling))
                passed[-1].start()
        for a in range(n):
            copy(a, 0, sibling, me).wait_recv()
        for j, chip in enumerate(chips):
            for a in range(n):
                copy(a, 4 + j, (*chip, 1 - c), me).wait_recv()
        for cp in first + passed:
            cp.wait_send()
        for cp in mine:
            cp.wait()

    hbm = pl.BlockSpec(memory_space=pl.ANY)
    return pl.pallas_call(
        body, name="weight_all_gather", in_specs=[hbm] * n, out_specs=[hbm] * n,
        out_shape=[jax.ShapeDtypeStruct((N_DEV,) + s.shape, s.dtype) for s in shards],
        scratch_shapes=[pltpu.SemaphoreType.DMA((7 * n,)), pltpu.SemaphoreType.DMA((7 * n,)),
                        pltpu.SemaphoreType.DMA((n,))],
    )(*shards)


def _grad_exchange(blocks, small):
    n = len(blocks)

    def body(*refs):
        src_refs, dst_refs = refs[:n + 1], refs[n + 1:2 * n + 2]
        send_sems, recv_sems, local_sems = refs[2 * n + 2:]
        x, y, c = _mesh_pos()
        me = 4 * x + 2 * y + c
        copies = []
        for kk in range(1, N_DEV):
            px = 1 - x if kk & 4 else x
            py = 1 - y if kk & 2 else y
            pc = 1 - c if kk & 1 else c
            peer = 4 * px + 2 * py + pc
            for a in range(n + 1):
                copies.append(pltpu.make_async_remote_copy(
                    src_ref=src_refs[a].at[peer] if a < n else src_refs[a], dst_ref=dst_refs[a].at[me],
                    send_sem=send_sems.at[7 * a + kk - 1], recv_sem=recv_sems.at[7 * a + kk - 1],
                    device_id=(px, py, pc), device_id_type=MESH_ID))
        own = [pltpu.make_async_copy(src_refs[a].at[me] if a < n else src_refs[a], dst_refs[a].at[me],
                                     local_sems.at[a]) for a in range(n + 1)]
        for cp in own + copies:
            cp.start()
        for cp in copies + own:
            cp.wait()

    hbm = pl.BlockSpec(memory_space=pl.ANY)
    return pl.pallas_call(
        body, name="grad_exchange", in_specs=[hbm] * (n + 1), out_specs=[hbm] * (n + 1),
        out_shape=[jax.ShapeDtypeStruct(b.shape, b.dtype) for b in blocks]
        + [jax.ShapeDtypeStruct((N_DEV,) + small.shape, small.dtype)],
        scratch_shapes=[pltpu.SemaphoreType.DMA((7 * (n + 1),)), pltpu.SemaphoreType.DMA((7 * (n + 1),)),
                        pltpu.SemaphoreType.DMA((n + 1,))],
    )(*blocks, small)


def _exchange_copies(src_refs, land_refs, send_sems, recv_sems):
    x, y, c = _mesh_pos()
    me = 4 * x + 2 * y + c
    copies = []
    for kk in range(1, N_DEV):
        px = 1 - x if kk & 4 else x
        py = 1 - y if kk & 2 else y
        pc = 1 - c if kk & 1 else c
        for a, (src, land) in enumerate(zip(src_refs, land_refs)):
            copies.append(pltpu.make_async_remote_copy(
                src_ref=src.at[4 * px + 2 * py + pc], dst_ref=land.at[me],
                send_sem=send_sems.at[7 * a + kk - 1], recv_sem=recv_sems.at[7 * a + kk - 1],
                device_id=(px, py, pc), device_id_type=MESH_ID))
    return copies


_HBM = pl.BlockSpec(memory_space=pltpu.HBM)
_SEM = pl.BlockSpec(memory_space=pltpu.SEMAPHORE)
_DATAFLOW = pltpu.SideEffectType.DATAFLOW_SIDE_EFFECTING


def _exchange_start(blocks):
    n = len(blocks)

    def body(*refs):
        src_refs, land_refs, send_sems, recv_sems = refs[:n], refs[n:2 * n], refs[2 * n], refs[2 * n + 1]
        for cp in _exchange_copies(src_refs, land_refs, send_sems, recv_sems):
            cp.start()
        token = refs[-1]
        token[...] = jnp.zeros_like(token)

    in_hbm = lambda a: pltpu.with_memory_space_constraint(a, pltpu.HBM)
    outs = pl.pallas_call(
        body, name="grad_exchange_start", in_specs=[_HBM] * (2 * n),
        out_shape=(pltpu.SemaphoreType.DMA((7 * n,)), pltpu.SemaphoreType.DMA((7 * n,)),
                   *[pltpu.HBM(b.shape, b.dtype) for b in blocks], *[pltpu.HBM(b.shape, b.dtype) for b in blocks],
                   jax.ShapeDtypeStruct((8, LANES), F32)),
        out_specs=(_SEM, _SEM, *[_HBM] * (2 * n), pl.BlockSpec(memory_space=pltpu.VMEM)),
        input_output_aliases={a: 2 + a for a in range(2 * n)},
        compiler_params=pltpu.CompilerParams(has_side_effects=_DATAFLOW),
    )(*[in_hbm(b) for b in blocks], *[in_hbm(lax.empty(b.shape, b.dtype)) for b in blocks])
    return outs[0], outs[1], outs[2:2 + n], outs[2 + n:2 + 2 * n], outs[-1]


def _exchange_wait(send_sems, recv_sems, src_thru, land_thru, after):
    n = len(src_thru)

    def body(*refs):
        src_refs, land_refs, send_sems, recv_sems = refs[:n], refs[n:2 * n], refs[2 * n], refs[2 * n + 1]
        for cp in _exchange_copies(src_refs, land_refs, send_sems, recv_sems):
            cp.wait_send()
            cp.wait_recv()

    outs = pl.pallas_call(
        body, name="grad_exchange_wait",
        in_specs=[_HBM] * (2 * n) + [_SEM, _SEM, pl.BlockSpec(memory_space=pl.ANY)],
        out_shape=tuple(pltpu.HBM(b.shape, b.dtype) for b in list(src_thru) + list(land_thru)),
        out_specs=[_HBM] * (2 * n), input_output_aliases={a: a for a in range(2 * n)},
        compiler_params=pltpu.CompilerParams(has_side_effects=_DATAFLOW),
    )(*src_thru, *land_thru, send_sems, recv_sems, after)
    return outs[:n], outs[n:]


def _pad_flat(parts, rows):
    flat = jnp.concatenate([p.reshape(-1) for p in parts])
    return jnp.pad(flat, (0, rows * LANES - flat.shape[0])).reshape(rows, LANES)


def _rows_for(n_elems, mult=1024):
    rows = -(-n_elems // LANES)
    return -(-rows // mult) * mult


SHARDED = ("meta_tokens", "w_in", "gdn_conv_w", "w_branch_fox", "w_branch_gdn", "w_out", "ffn_w_up", "ffn_conv_w",
           "ffn_w_down")
MATMUL = ("w_in", "w_branch_fox", "w_branch_gdn", "w_out", "ffn_w_up", "ffn_w_down")
EXACT = ("meta_tokens", "gdn_conv_w", "ffn_conv_w")
REPLICATED = ("fgt_bias", "gdn_a_log", "gdn_dt_bias", "gdn_norm_w", "gate_bias", "norm_mix_w", "norm_ffn_w",
              "ffn_conv_b", "norm_final_w")
WEIGHTS = ("meta_tokens", "w_in", "fgt_bias", "gdn_conv_w", "gdn_a_log", "gdn_dt_bias", "gdn_norm_w", "gate_bias",
           "w_branch_fox", "w_branch_gdn", "w_out", "norm_mix_w", "norm_ffn_w", "ffn_w_up", "ffn_conv_w",
           "ffn_conv_b", "ffn_w_down", "norm_final_w")


def _unpack(buf, shapes):
    flat = buf.reshape(-1)
    out, off = [], 0
    for s in shapes:
        n = int(np.prod(s))
        out.append(flat[off:off + n].reshape(s))
        off += n
    return out


def _unpack_gathered(buf, shapes):
    flat = buf.reshape(N_DEV, -1)
    out, off = [], 0
    for s in shapes:
        n = int(np.prod(s))
        out.append(flat[:, off:off + n].reshape((N_DEV,) + tuple(s)))
        off += n
    return out


def _cat_cols(g):
    return g.transpose(1, 0, 2).reshape(g.shape[1], -1)


def _col_blocks(full, width):
    return full.reshape(full.shape[0], N_DEV, width).transpose(1, 0, 2)


def _local_step(x, target, w, early=None):
    seq = x.shape[0]
    t = _padded_tokens(seq)
    pad = t - N_META - seq
    seg = _seg_matrix()
    zrows = jnp.zeros((pad, D_MODEL), F32)
    h0 = jnp.concatenate([w["meta_tokens"], x, zrows], axis=0)
    tgt = jnp.concatenate([jnp.zeros((N_META, D_MODEL), F32), target, zrows], axis=0)

    w_in = w["w_in"]
    o_f, o_g, o_z, o_b, o_a, o_gate = 1536, 1544, 3080, 3592, 3600, 3608
    w_small = jnp.concatenate([w_in[:, o_f:o_f + 8], w_in[:, o_b:o_b + 8], w_in[:, o_a:o_a + 8],
                               jnp.zeros((D_MODEL, LANES - 24), BF16)], axis=1)
    w_r = jnp.concatenate([w_in[:, :1536], w_in[:, o_g:o_z], w_in[:, o_z:o_b], w_in[:, o_gate:], w_small], axis=1)

    a1 = _rmsnorm_fwd(h0, w["norm_mix_w"])
    fq = _mm(a1, w_r[:, :1536], BF16, "proj_fox")
    xg = _mm(a1, w_r[:, 1536:3584], BF16, "proj_gdn")
    gt = _mm(a1, w_r[:, 3584:5632], BF16, "proj_gates")
    sm = _mm(a1, w_r[:, 5632:], F32, "proj_small")

    lanes_pad = lambda a, lo: jnp.pad(a, ((0, 0), (lo, LANES - lo - a.shape[1])))
    neg_exp_a = -jnp.exp(w["gdn_a_log"])
    pbias = lanes_pad(w["fgt_bias"], 0) + lanes_pad(w["gdn_dt_bias"], 2 * HEADS)
    pscale = lanes_pad(neg_exp_a, 2 * HEADS)
    scal = _gate_fwd(sm, pbias, pscale)
    gct = scal[:, 2 * HEADS:3 * HEADS].T

    qa, ka, va, kat, vat = _fox_prep(fq, scal)
    oa, qb, qbt = _fox_fwd(qa, ka, vat)
    o_fox = _fox_post(oa)

    qh, kh, vh = _gdn_pre_fwd(xg, w["gdn_conv_w"], seg)
    og, s0s = _gdn_fwd(qh, kh, vh, scal, gct)
    norm_w = jnp.tile(w["gdn_norm_w"], (1, HEADS))
    ogn = _gdn_post_fwd(og, xg, norm_w, seg)

    yf = _mm(o_fox, w["w_branch_fox"], F32, "branch_fox")
    yg = _mm(ogn, w["w_branch_gdn"], F32, "branch_gdn")
    mix = _mix_fwd(yf, yg, gt, w["gate_bias"])
    h1 = _mm(mix, w["w_out"], F32, "out_proj", res=h0)
    a2 = _rmsnorm_fwd(h1, w["norm_ffn_w"])
    up = _mm(a2, w["ffn_w_up"], BF16, "ffn_up")
    act = _ffn_act_fwd(up, w["ffn_conv_w"], w["ffn_conv_b"])
    h2 = _mm(act, w["ffn_w_down"], F32, "ffn_down", res=h1)
    loss, dh2, dh2b, g_final = _final_loss(h2, tgt, w["norm_final_w"].reshape(1, D_MODEL), seq)

    grads = {"norm_final_w": g_final.reshape(D_MODEL)}
    grads["ffn_w_down"] = _mm_tn(act, dh2b, "wgrad_ffn_down")
    dact = _mm(dh2b, w["ffn_w_down"].T, F32, "dgrad_ffn_down")
    dup, g_cw, g_cb = _ffn_act_bwd(up, w["ffn_conv_w"], w["ffn_conv_b"], dact)
    grads["ffn_conv_w"], grads["ffn_conv_b"] = g_cw, g_cb
    grads["ffn_w_up"] = _mm_tn(a2, dup, "wgrad_ffn_up")
    da2 = _mm(dup, w["ffn_w_up"].T, F32, "dgrad_ffn_up")
    dh1, dh1b, grads["norm_ffn_w"] = _rmsnorm_bwd(h1, da2, w["norm_ffn_w"], dh2)
    grads["w_out"] = _mm_tn(mix, dh1b, "wgrad_out")
    dmix = _mm(dh1b, w["w_out"].T, F32, "dgrad_out")
    dyf, dyg, dgt, grads["gate_bias"] = _mix_bwd(dmix, yf, yg, gt, w["gate_bias"])
    grads["w_branch_fox"] = _mm_tn(o_fox, dyf, "wgrad_branch_fox")
    grads["w_branch_gdn"] = _mm_tn(ogn, dyg, "wgrad_branch_gdn")
    do_fox = _mm(dyf, w["w_branch_fox"].T, F32, "dgrad_branch_fox")
    dogn = _mm(dyg, w["w_branch_gdn"].T, F32, "dgrad_branch_gdn")

    dog, dz, g_nw = _gdn_post_bwd(og, xg, norm_w, seg, dogn)
    grads["gdn_norm_w"] = g_nw.reshape(HEADS, HEAD_DIM).sum(axis=0)[None]
    dqh, dkh, dvh, dscal_g, dgct = _gdn_bwd(qh, kh, vh, scal, gct, s0s, dog)
    dxg, grads["gdn_conv_w"] = _gdn_pre_bwd(xg, w["gdn_conv_w"], seg, dqh, dkh, dvh)

    doa, doat = _fox_bwd_prep(do_fox, oa)
    dfq, dscal_c = _fox_bwd_post(*_fox_bwd(qb, qbt, ka, kat, va, doa, doat))

    dscal = dscal_c + dscal_g + lanes_pad(dgct.T, 2 * HEADS)
    dsm, dpb, dps = _gate_bwd(sm, pbias, pscale, dscal)
    grads["fgt_bias"] = dpb[:, :HEADS]
    grads["gdn_dt_bias"] = dpb[:, 2 * HEADS:3 * HEADS]
    grads["gdn_a_log"] = dps[:, 2 * HEADS:3 * HEADS] * neg_exp_a

    dproj = jnp.concatenate([dfq, dxg, dz, dgt, dsm], axis=1)
    g_r = _mm_tn(a1, dproj, "wgrad_in")
    grads["w_in"] = jnp.concatenate([g_r[:, :1536], g_r[:, 5632:5640], g_r[:, 1536:3072], g_r[:, 3072:3584],
                                     g_r[:, 5640:5648], g_r[:, 5648:5656], g_r[:, 3584:5632]], axis=1)
    token, handle = early(grads) if early is not None else (jnp.zeros((8, LANES), F32), None)
    w_rt = w_r.T + token[0, 0].astype(BF16)
    da1 = _mm(dproj, w_rt, F32, "dgrad_in")
    dh0, _, grads["norm_mix_w"] = _rmsnorm_bwd(h0, da1, w["norm_mix_w"], dh1)
    grads["meta_tokens"] = dh0[:N_META]
    return loss, dh0[N_META:N_META + seq], grads, handle


def _shard_pieces(arrs):
    return [arrs[n][0] if arrs[n].ndim == 3 else arrs[n] for n in SHARDED]


def _full_grad_blocks(grads):
    g = grads
    cols = lambda a, wd: _col_blocks(a, wd)
    rows = lambda a: a.reshape(N_DEV, a.shape[0] // N_DEV, a.shape[1])
    return [cols(g["w_in"], IN_WIDTH // N_DEV), cols(g["gdn_conv_w"], 3 * WIDTH // N_DEV),
            cols(g["w_branch_fox"], D_MODEL // N_DEV), cols(g["w_branch_gdn"], D_MODEL // N_DEV), rows(g["w_out"]),
            cols(g["ffn_w_up"], 2 * D_FF // N_DEV), cols(g["ffn_conv_w"], 2 * D_FF // N_DEV), rows(g["ffn_w_down"])]


def kernel(x, meta_tokens, w_in, fgt_bias, gdn_conv_w, gdn_a_log, gdn_dt_bias, gdn_norm_w, gate_bias, w_branch_fox, w_branch_gdn, w_out, norm_mix_w, norm_ffn_w, ffn_w_up, ffn_conv_w, ffn_conv_b, ffn_w_down, norm_final_w, loss_target, m_meta_tokens, m_w_in, m_fgt_bias, m_gdn_conv_w, m_gdn_a_log, m_gdn_dt_bias, m_gdn_norm_w, m_gate_bias, m_w_branch_fox, m_w_branch_gdn, m_w_out, m_norm_mix_w, m_norm_ffn_w, m_ffn_w_up, m_ffn_conv_w, m_ffn_conv_b, m_ffn_w_down, m_norm_final_w, v_meta_tokens, v_w_in, v_fgt_bias, v_gdn_conv_w, v_gdn_a_log, v_gdn_dt_bias, v_gdn_norm_w, v_gate_bias, v_w_branch_fox, v_w_branch_gdn, v_w_out, v_norm_mix_w, v_norm_ffn_w, v_ffn_w_up, v_ffn_conv_w, v_ffn_conv_b, v_ffn_w_down, v_norm_final_w):
    wts = dict(meta_tokens=meta_tokens, w_in=w_in, fgt_bias=fgt_bias, gdn_conv_w=gdn_conv_w, gdn_a_log=gdn_a_log,
               gdn_dt_bias=gdn_dt_bias, gdn_norm_w=gdn_norm_w, gate_bias=gate_bias, w_branch_fox=w_branch_fox,
               w_branch_gdn=w_branch_gdn, w_out=w_out, norm_mix_w=norm_mix_w, norm_ffn_w=norm_ffn_w,
               ffn_w_up=ffn_w_up, ffn_conv_w=ffn_conv_w, ffn_conv_b=ffn_conv_b, ffn_w_down=ffn_w_down,
               norm_final_w=norm_final_w)
    mom = dict(meta_tokens=m_meta_tokens, w_in=m_w_in, fgt_bias=m_fgt_bias, gdn_conv_w=m_gdn_conv_w,
               gdn_a_log=m_gdn_a_log, gdn_dt_bias=m_gdn_dt_bias, gdn_norm_w=m_gdn_norm_w, gate_bias=m_gate_bias,
               w_branch_fox=m_w_branch_fox, w_branch_gdn=m_w_branch_gdn, w_out=m_w_out, norm_mix_w=m_norm_mix_w,
               norm_ffn_w=m_norm_ffn_w, ffn_w_up=m_ffn_w_up, ffn_conv_w=m_ffn_conv_w, ffn_conv_b=m_ffn_conv_b,
               ffn_w_down=m_ffn_w_down, norm_final_w=m_norm_final_w)
    var = dict(meta_tokens=v_meta_tokens, w_in=v_w_in, fgt_bias=v_fgt_bias, gdn_conv_w=v_gdn_conv_w,
               gdn_a_log=v_gdn_a_log, gdn_dt_bias=v_gdn_dt_bias, gdn_norm_w=v_gdn_norm_w, gate_bias=v_gate_bias,
               w_branch_fox=v_w_branch_fox, w_branch_gdn=v_w_branch_gdn, w_out=v_w_out, norm_mix_w=v_norm_mix_w,
               norm_ffn_w=v_norm_ffn_w, ffn_w_up=v_ffn_w_up, ffn_conv_w=v_ffn_conv_w, ffn_conv_b=v_ffn_conv_b,
               ffn_w_down=v_ffn_w_down, norm_final_w=v_norm_final_w)

    sh = dict(zip(SHARDED, _shard_pieces(wts)))
    exact_shapes = [sh[n].shape for n in EXACT]
    rows_exact = _rows_for(sum(int(np.prod(s)) for s in exact_shapes), 8)
    gathered = _all_gather([sh[n].astype(BF16) for n in MATMUL] + [_pad_flat([sh[n] for n in EXACT], rows_exact)])
    g_in, g_bf, g_bg, g_out, g_up, g_down = gathered[:6]
    meta_full, conv_full, fconv_full = (_cat_cols(a) for a in _unpack_gathered(gathered[6], exact_shapes))
    full = dict(
        meta_tokens=meta_full, w_in=_cat_cols(g_in), gdn_conv_w=conv_full,
        w_branch_fox=_cat_cols(g_bf), w_branch_gdn=_cat_cols(g_bg), w_out=g_out.reshape(D_MODEL, D_MODEL),
        ffn_w_up=_cat_cols(g_up), ffn_conv_w=fconv_full, ffn_w_down=g_down.reshape(D_FF, D_MODEL),
        fgt_bias=fgt_bias, gdn_a_log=gdn_a_log, gdn_dt_bias=gdn_dt_bias, gdn_norm_w=gdn_norm_w, gate_bias=gate_bias,
        norm_mix_w=norm_mix_w, norm_ffn_w=norm_ffn_w, ffn_conv_b=ffn_conv_b, norm_final_w=norm_final_w)

    def start_exchange(grads_so_far):
        blocks = [b.astype(BF16) for b in _full_grad_blocks(grads_so_far)]
        send_sems, recv_sems, src_thru, land_thru, token = _exchange_start(blocks)
        return token, (send_sems, recv_sems, src_thru, land_thru)

    loss, grad_x, grads, handle = _local_step(x[0], loss_target[0], full, early=start_exchange)
    sent, landed = _exchange_wait(*handle, after=grad_x)
    me = 4 * lax.axis_index("x") + 2 * lax.axis_index("y") + lax.axis_index("c")
    own = lambda src, land: lax.dynamic_update_slice_in_dim(land, lax.dynamic_slice_in_dim(src, me, 1, 0), me, 0)
    received = [own(src, land) for src, land in zip(sent, landed)]

    rep_parts = [grads[n] for n in REPLICATED] + [loss[:, :1]]
    rep_shapes = [wts[n].shape for n in REPLICATED]
    rows_small = _rows_for(sum(int(np.prod(p.shape)) for p in rep_parts), 8)
    meta_recv, small_recv = _grad_exchange([_col_blocks(grads["meta_tokens"], LANES).astype(BF16)],
                                           _pad_flat(rep_parts, rows_small))
    received = [meta_recv] + received + [small_recv]

    result = {}
    kinds = ("grad", "delta", "new_m", "new_v")
    for n, recv in zip(SHARDED, received[:-1]):
        outs = _adamw(sh[n], _shard_pieces(mom)[SHARDED.index(n)], _shard_pieces(var)[SHARDED.index(n)], recv,
                      "adamw_" + n)
        for kind, a in zip(kinds, outs):
            result[kind, n] = a.reshape(wts[n].shape)
    rep_w = _pad_flat([wts[n] for n in REPLICATED] + [jnp.zeros((1, 1), F32)], rows_small)
    rep_m = _pad_flat([mom[n] for n in REPLICATED] + [jnp.zeros((1, 1), F32)], rows_small)
    rep_v = _pad_flat([var[n] for n in REPLICATED] + [jnp.ones((1, 1), F32)], rows_small)
    outs_r = _adamw(rep_w, rep_m, rep_v, received[-1], "adamw_replicated")
    for kind, br in zip(kinds, outs_r):
        for n, a in zip(REPLICATED, _unpack(br, rep_shapes)):
            result[kind, n] = a
    n_rep = sum(int(np.prod(s)) for s in rep_shapes)
    total_loss = outs_r[0].reshape(-1)[n_rep]
    out = [total_loss, grad_x[None]]
    for kind in ("grad", "delta", "new_m", "new_v"):
        out += [result[kind, n] for n in WEIGHTS]
    return tuple(out)
```

```python
import functools

import jax
import jax.numpy as jnp
import numpy as np
from jax import lax
from jax.experimental import pallas as pl
from jax.experimental.pallas import tpu as pltpu

F32 = jnp.float32
BF16 = jnp.bfloat16

D_MODEL = 1024
N_META = 16
HEADS = 8
HEAD_DIM = 64
WIDTH = HEADS * HEAD_DIM
CHUNK = 64
GDN_CONV = 4
D_FF = 2816
FFN_CONV = 3
IN_WIDTH = 5656
IN_PAD = 5760
RMS_EPS = 1e-6
NEG = -1e30
AUG = 128
N_DEV = 8
LANES = 128

ADAM_LR = 0.001
ADAM_B1 = 0.9
ADAM_B2 = 0.999
ADAM_EPS = 1e-08
ADAM_WD = 0.01
ADAM_STEP = 10

VMEM_LIMIT = 56 * 1024 * 1024
MM_VMEM_BUDGET = 36 * 1024 * 1024
FFN_LANES = 128
HI = lax.Precision.HIGH
MESH_ID = pl.DeviceIdType.MESH


def _pick(n, cands):
    for c in cands:
        if n % c == 0:
            return c
    raise ValueError(f"no tile for {n} in {cands}")


def _params(*sem):
    return pltpu.CompilerParams(dimension_semantics=sem if sem else None, vmem_limit_bytes=VMEM_LIMIT)


def _padded_tokens(seq):
    t = -(-(N_META + seq) // 128) * 128
    if t > 1280 and t % 640:
        t = -(-t // 640) * 640
    return t


ROW_TILES = (640, 512, 384, 256, 128)


def _rmsnorm_fwd(h, gain):
    t, d = h.shape
    tr = _pick(t, ROW_TILES)

    def body(h_ref, g_ref, o_ref):
        x = h_ref[...]
        r = lax.rsqrt(jnp.mean(x * x, axis=-1, keepdims=True) + RMS_EPS)
        o_ref[...] = (x * r * g_ref[...]).astype(o_ref.dtype)

    return pl.pallas_call(
        body, grid=(t // tr,), name="rmsnorm_fwd",
        in_specs=[pl.BlockSpec((tr, d), lambda i: (i, 0)), pl.BlockSpec((1, d), lambda i: (0, 0))],
        out_specs=pl.BlockSpec((tr, d), lambda i: (i, 0)),
        out_shape=jax.ShapeDtypeStruct((t, d), BF16),
        compiler_params=_params("arbitrary"),
    )(h, gain)


def _rmsnorm_bwd(h, dy, gain, dres):
    t, d = h.shape
    tr = _pick(t, (320, 256, 128))

    def body(h_ref, dy_ref, g_ref, dres_ref, dh_ref, dhb_ref, dg_ref):
        x = h_ref[...]
        dyv = dy_ref[...]
        r = lax.rsqrt(jnp.mean(x * x, axis=-1, keepdims=True) + RMS_EPS)
        gy = dyv * g_ref[...]
        m = jnp.mean(gy * x, axis=-1, keepdims=True)
        dh = dres_ref[...] + r * gy - x * (r * r * r * m)
        dh_ref[...] = dh
        dhb_ref[...] = dh.astype(BF16)

        @pl.when(pl.program_id(0) == 0)
        def _():
            dg_ref[...] = jnp.zeros_like(dg_ref)

        dg_ref[...] += jnp.sum(dyv * x * r, axis=0, keepdims=True)

    row = pl.BlockSpec((tr, d), lambda i: (i, 0))
    vec = pl.BlockSpec((1, d), lambda i: (0, 0))
    return pl.pallas_call(
        body, grid=(t // tr,), name="rmsnorm_bwd",
        in_specs=[row, row, vec, row], out_specs=[row, row, vec],
        out_shape=[jax.ShapeDtypeStruct((t, d), F32), jax.ShapeDtypeStruct((t, d), BF16),
                   jax.ShapeDtypeStruct((1, d), F32)],
        compiler_params=_params("arbitrary"),
    )(h, dy, gain, dres)


def _mm(a, b, out_dtype, name, res=None):
    m, k = a.shape
    _, n = b.shape
    tm = _pick(m, ROW_TILES)
    out_bytes = jnp.dtype(out_dtype).itemsize + (4 if res is not None else 0)
    fits = lambda tn: 4 * tm * k + 4 * k * tn + 2 * tm * tn * out_bytes <= MM_VMEM_BUDGET
    tn = next(c for c in (n, 2816, 2048, 1536, 1408, 1024, 512, 384, 256, 128) if n % c == 0 and fits(c))

    def body(*refs):
        if res is None:
            a_ref, b_ref, o_ref = refs
        else:
            a_ref, b_ref, r_ref, o_ref = refs
        out = jnp.dot(a_ref[...], b_ref[...], preferred_element_type=F32)
        if res is not None:
            out = out + r_ref[...]
        o_ref[...] = out.astype(o_ref.dtype)

    in_specs = [pl.BlockSpec((tm, k), lambda i, j: (i, 0)), pl.BlockSpec((k, tn), lambda i, j: (0, j))]
    args = [a, b]
    if res is not None:
        in_specs.append(pl.BlockSpec((tm, tn), lambda i, j: (i, j)))
        args.append(res)
    return pl.pallas_call(
        body, grid=(m // tm, n // tn), name=name,
        in_specs=in_specs, out_specs=pl.BlockSpec((tm, tn), lambda i, j: (i, j)),
        out_shape=jax.ShapeDtypeStruct((m, n), out_dtype),
        compiler_params=_params("parallel", "parallel"),
    )(*args)


def _mm_tn(a, g, name):
    t, k = a.shape
    _, n = g.shape
    tk = _pick(k, (1024, 1408, 512))
    tn = _pick(n, (512, 640, 384, 256, 128))
    tt = next(c for c in (3328, 1280) + ROW_TILES
              if t % c == 0 and 4 * c * (tk + tn) + 8 * tk * tn <= MM_VMEM_BUDGET)
    nt = t // tt

    def body(a_ref, g_ref, o_ref):
        @pl.when(pl.program_id(2) == 0)
        def _():
            o_ref[...] = jnp.zeros_like(o_ref)

        o_ref[...] += lax.dot_general(a_ref[...], g_ref[...], (((0,), (0,)), ((), ())),
                                      preferred_element_type=F32)

    return pl.pallas_call(
        body, grid=(k // tk, n // tn, nt), name=name,
        in_specs=[pl.BlockSpec((tt, tk), lambda i, j, s: (s, i)), pl.BlockSpec((tt, tn), lambda i, j, s: (s, j))],
        out_specs=pl.BlockSpec((tk, tn), lambda i, j, s: (i, j)),
        out_shape=jax.ShapeDtypeStruct((k, n), F32),
        compiler_params=_params("parallel", "parallel", "arbitrary"),
    )(a, g)


def _split3_exact(x):
    def top(v):
        return lax.bitcast_convert_type(lax.bitcast_convert_type(v, jnp.int32) & jnp.int32(-65536), F32)

    hi = top(x)
    r1 = x - hi
    mid = top(r1)
    return hi, mid, r1 - mid


def _pair_head(ref, h, rows):
    x = ref[:, 128 * (h // 2):128 * (h // 2) + 128].astype(F32)
    return pltpu.roll(x, HEAD_DIM, axis=1) if h % 2 else x


def _lanes(rows):
    return lax.broadcasted_iota(jnp.int32, (rows, AUG), 1)


def _fox_prep(fq, scal):
    t = fq.shape[0]
    tt = _pick(t, (256, 128))

    def body(q_ref, k_ref, v_ref, s_ref, qa_ref, ka_ref, va_ref, kt_ref, vt_ref):
        lane = _lanes(tt)
        chi, cmid, clo = _split3_exact(s_ref[...])
        ones = lambda lo: jnp.where((lane >= lo) & (lane < lo + 3), 1.0, 0.0)
        for h in range(HEADS):
            col = lambda a: jnp.broadcast_to(a[:, h:h + 1], (tt, AUG))
            c1, c2, c3 = col(chi), col(cmid), col(clo)
            qx = jnp.where(lane == 64, c1, jnp.where(lane == 65, c2, jnp.where(lane == 66, c3, ones(67))))
            kx = jnp.where(lane == 67, -c1, jnp.where(lane == 68, -c2, jnp.where(lane == 69, -c3, ones(64) + ones(70))))
            qa_ref[h] = jnp.where(lane < HEAD_DIM, _pair_head(q_ref, h, tt) * (HEAD_DIM ** -0.5), qx).astype(BF16)
            k_aug = jnp.where(lane < HEAD_DIM, _pair_head(k_ref, h, tt), kx)
            ka_ref[h] = k_aug.astype(BF16)
            kt_ref[h] = k_aug.T.astype(BF16)
            v_aug = jnp.where(lane < HEAD_DIM, _pair_head(v_ref, h, tt), ones(64))
            va_ref[h] = v_aug.astype(BF16)
            vt_ref[h] = v_aug.T.astype(BF16)

    out = pl.BlockSpec((HEADS, tt, AUG), lambda i: (0, i, 0))
    out_t = pl.BlockSpec((HEADS, AUG, tt), lambda i: (0, 0, i))
    shp = jax.ShapeDtypeStruct((HEADS, t, AUG), BF16)
    shp_t = jax.ShapeDtypeStruct((HEADS, AUG, t), BF16)
    return pl.pallas_call(
        body, grid=(t // tt,), name="fox_prep",
        in_specs=[pl.BlockSpec((tt, WIDTH), lambda i: (i, 0)), pl.BlockSpec((tt, WIDTH), lambda i: (i, 1)),
                  pl.BlockSpec((tt, WIDTH), lambda i: (i, 2)), pl.BlockSpec((tt, LANES), lambda i: (i, 0))],
        out_specs=[out, out, out, out_t, out_t], out_shape=[shp, shp, shp, shp_t, shp_t],
        compiler_params=_params("parallel"),
    )(fq, fq, fq, scal)


def _fox_post(oa):
    t = oa.shape[1]
    tt = _pick(t, (256, 128))

    def body(o_ref, out_ref):
        out_ref[...] = jnp.concatenate([o_ref[h][:, :HEAD_DIM] for h in range(HEADS)], axis=1).astype(BF16)

    return pl.pallas_call(
        body, grid=(t // tt,), name="fox_post",
        in_specs=[pl.BlockSpec((HEADS, tt, AUG), lambda i: (0, i, 0))],
        out_specs=pl.BlockSpec((tt, WIDTH), lambda i: (i, 0)),
        out_shape=jax.ShapeDtypeStruct((t, WIDTH), BF16),
        compiler_params=_params("parallel"),
    )(oa)


def _fox_bwd_prep(do, oa):
    t = do.shape[0]
    tt = _pick(t, (256, 128))

    def body(d_ref, o_ref, out_ref, outt_ref):
        lane = _lanes(tt)
        for h in range(HEADS):
            x = _pair_head(d_ref, h, tt)
            delta = jnp.sum(jnp.where(lane < HEAD_DIM, x * o_ref[h], 0.0), axis=1, keepdims=True)
            hi, mid, lo = _split3_exact(jnp.broadcast_to(-delta, (tt, AUG)))
            ex = jnp.where(lane == 64, hi, jnp.where(lane == 65, mid, jnp.where(lane == 66, lo, 0.0)))
            do_aug = jnp.where(lane < HEAD_DIM, x, ex)
            out_ref[h] = do_aug.astype(BF16)
            outt_ref[h] = do_aug.T.astype(BF16)

    hm = pl.BlockSpec((HEADS, tt, AUG), lambda i: (0, i, 0))
    return pl.pallas_call(
        body, grid=(t // tt,), name="fox_bwd_prep",
        in_specs=[pl.BlockSpec((tt, WIDTH), lambda i: (i, 0)), hm],
        out_specs=[hm, pl.BlockSpec((HEADS, AUG, tt), lambda i: (0, 0, i))],
        out_shape=[jax.ShapeDtypeStruct((HEADS, t, AUG), BF16), jax.ShapeDtypeStruct((HEADS, AUG, t), BF16)],
        compiler_params=_params("parallel"),
    )(do, oa)


def _fox_bwd_post(dqt, dkt, dvt):
    t = dqt.shape[2]
    tt = _pick(t, (256, 128))

    def body(dq_ref, dk_ref, dv_ref, out_ref, dsc_ref):
        lane = _lanes(tt)
        dqs = [dq_ref[h].T for h in range(HEADS)]
        dks = [dk_ref[h].T for h in range(HEADS)]
        heads = lambda xs: jnp.concatenate([x[:, :HEAD_DIM] for x in xs], axis=1)
        out_ref[:, 0:WIDTH] = (heads(dqs) * (HEAD_DIM ** -0.5)).astype(BF16)
        out_ref[:, WIDTH:2 * WIDTH] = heads(dks).astype(BF16)
        out_ref[:, 2 * WIDTH:] = heads([dv_ref[h].T for h in range(HEADS)]).astype(BF16)
        dsc = jnp.zeros((tt, LANES), F32)
        for h in range(HEADS):
            both = jnp.where(lane == HEAD_DIM, dqs[h], 0.0) - jnp.where(lane == HEAD_DIM + 3, dks[h], 0.0)
            dsc = jnp.where(lane == h, jnp.sum(both, axis=1, keepdims=True), dsc)
        dsc_ref[...] = dsc

    hm = pl.BlockSpec((HEADS, AUG, tt), lambda i: (0, 0, i))
    return pl.pallas_call(
        body, grid=(t // tt,), name="fox_bwd_post",
        in_specs=[hm, hm, hm],
        out_specs=[pl.BlockSpec((tt, 3 * WIDTH), lambda i: (i, 0)), pl.BlockSpec((tt, LANES), lambda i: (i, 0))],
        out_shape=[jax.ShapeDtypeStruct((t, 3 * WIDTH), BF16), jax.ShapeDtypeStruct((t, LANES), F32)],
        compiler_params=_params("parallel"),
    )(dqt, dkt, dvt)


def _fox_fwd(qa, ka, vat, tq=None):
    h, t, _ = qa.shape
    tq = tq or _pick(t, ROW_TILES)

    def body(q_ref, k_ref, vt_ref, o_ref, qb_ref, qbt_ref, s_ref):
        i = pl.program_id(1)
        q = q_ref[...]
        krow = lax.broadcasted_iota(jnp.int32, (tq, tq), 0)
        qcol = lax.broadcasted_iota(jnp.int32, (tq, tq), 1)
        rows = lambda j: pl.ds(pl.multiple_of(j * tq, tq), tq)

        def scores(j, slot):
            s_ref[slot] = lax.dot_general(k_ref[rows(j), :], q, (((1,), (1,)), ((), ())), preferred_element_type=F32)

        def update(j, slot, carry, masked):
            m, acc = carry
            s = s_ref[slot]
            if masked:
                s = jnp.where(qcol >= krow, s, NEG)
            m_new = jnp.maximum(m, jnp.max(s, axis=0, keepdims=True))
            p = jnp.exp(s - m_new)
            alpha = jnp.exp(m - m_new)
            return m_new, acc * alpha + jnp.dot(vt_ref[:, rows(j)], p.astype(BF16), preferred_element_type=F32)

        def pair(jj, carry):
            j = 2 * jj
            scores(j + 1, 1)
            carry = update(j, 0, carry, False)
            scores(j + 2, 0)
            return update(j + 1, 1, carry, False)

        def odd_tail(carry):
            scores(i, 1)
            return update(i, 1, update(i - 1, 0, carry, False), True)

        scores(0, 0)
        carry = (jnp.full((1, tq), NEG, F32), jnp.zeros((AUG, tq), F32))
        carry = lax.fori_loop(0, i // 2, pair, carry)
        m, acc = lax.cond(i % 2 == 1, odd_tail, lambda c: update(i, 0, c, True), carry)
        sub = lax.broadcasted_iota(jnp.int32, (AUG, tq), 0)
        l = jnp.sum(jnp.where(sub == HEAD_DIM, acc, 0.0), axis=0, keepdims=True)
        out = jnp.where(sub < HEAD_DIM, acc / l, m + jnp.log(l)).T
        o_ref[...] = out
        lane = lax.broadcasted_iota(jnp.int32, (tq, AUG), 1)
        lse = jnp.broadcast_to(jnp.sum(jnp.where(lane == HEAD_DIM, out, 0.0), axis=1, keepdims=True), (tq, AUG))
        hi, mid, lo = _split3_exact(-lse)
        qb = jnp.where(lane == 70, hi, jnp.where(lane == 71, mid, jnp.where(lane == 72, lo, q.astype(F32))))
        qb_ref[...] = qb.astype(BF16)
        qbt_ref[...] = qb.T.astype(BF16)

    blk = pl.BlockSpec((None, tq, AUG), lambda hh, i: (hh, i, 0))
    return pl.pallas_call(
        body, grid=(h, t // tq), name="fox_fwd",
        in_specs=[blk, pl.BlockSpec((None, t, AUG), lambda hh, i: (hh, 0, 0)),
                  pl.BlockSpec((None, AUG, t), lambda hh, i: (hh, 0, 0))],
        out_specs=[blk, blk, pl.BlockSpec((None, AUG, tq), lambda hh, i: (hh, 0, i))],
        out_shape=[jax.ShapeDtypeStruct((h, t, AUG), F32), jax.ShapeDtypeStruct((h, t, AUG), BF16),
                   jax.ShapeDtypeStruct((h, AUG, t), BF16)],
        scratch_shapes=[pltpu.VMEM((2, tq, tq), F32)],
        compiler_params=_params("parallel", "arbitrary"),
    )(qa, ka, vat)


def _fox_bwd(qb, qbt, ka, kat, va, doa, doat, tq=None):
    h, t, _ = qb.shape
    tq = tq or _pick(t, ROW_TILES)
    nq = t // tq

    def body(q_ref, qt_ref, k_ref, kt_ref, v_ref, do_ref, dot_ref, dqt_ref, dkt_ref, dvt_ref, s_ref, dp_ref):
        j = pl.program_id(1)
        n = nq - j

        @pl.when(j == 0)
        def _():
            dqt_ref[...] = jnp.zeros_like(dqt_ref)

        dkt_ref[...] = jnp.zeros_like(dkt_ref)
        dvt_ref[...] = jnp.zeros_like(dvt_ref)
        kj = k_ref[...]
        ktj = kt_ref[...]
        vj = v_ref[...]
        qrow = lax.broadcasted_iota(jnp.int32, (tq, tq), 0)
        kcol = lax.broadcasted_iota(jnp.int32, (tq, tq), 1)
        rows = lambda i: pl.ds(pl.multiple_of(i * tq, tq), tq)
        nt_dims = (((1,), (1,)), ((), ()))

        def scores(i, slot):
            s_ref[slot] = lax.dot_general(q_ref[rows(i), :], kj, nt_dims, preferred_element_type=F32)
            dp_ref[slot] = lax.dot_general(do_ref[rows(i), :], vj, nt_dims, preferred_element_type=F32)

        def update(i, slot):
            p = jnp.exp(jnp.where((qrow >= kcol) | (i > j), s_ref[slot], NEG))
            ds = (p * dp_ref[slot]).astype(BF16)
            dvt_ref[...] += jnp.dot(dot_ref[:, rows(i)], p.astype(BF16), preferred_element_type=F32)
            dkt_ref[...] += jnp.dot(qt_ref[:, rows(i)], ds, preferred_element_type=F32)
            dqt_ref[:, rows(i)] += lax.dot_general(ktj, ds, nt_dims, preferred_element_type=F32)

        def pair(kk, carry):
            i0 = j + 2 * kk
            scores(i0 + 1, 1)
            update(i0, 0)
            scores(jnp.minimum(i0 + 2, nq - 1), 0)
            update(i0 + 1, 1)
            return carry

        scores(j, 0)
        lax.fori_loop(0, n // 2, pair, 0)

        @pl.when(n % 2 == 1)
        def _():
            update(nq - 1, 0)

    once = pl.Buffered(1)
    full = pl.BlockSpec((None, t, AUG), lambda hh, j: (hh, 0, 0), pipeline_mode=once)
    full_t = pl.BlockSpec((None, AUG, t), lambda hh, j: (hh, 0, 0), pipeline_mode=once)
    blk = pl.BlockSpec((None, tq, AUG), lambda hh, j: (hh, j, 0))
    blk_t = pl.BlockSpec((None, AUG, tq), lambda hh, j: (hh, 0, j))
    shp = jax.ShapeDtypeStruct((h, AUG, t), F32)
    return pl.pallas_call(
        body, grid=(h, nq), name="fox_bwd",
        in_specs=[full, full_t, blk, blk_t, blk, full, full_t],
        out_specs=[pl.BlockSpec((None, AUG, t), lambda hh, j: (hh, 0, 0)), blk_t, blk_t], out_shape=[shp, shp, shp],
        scratch_shapes=[pltpu.VMEM((2, tq, tq), F32), pltpu.VMEM((2, tq, tq), F32)],
        compiler_params=_params("parallel", "arbitrary"),
    )(qb, qbt, ka, kat, va, doa, doat)


def _seg_matrix():
    idx = np.arange(WIDTH) // HEAD_DIM
    return jnp.asarray((idx[:, None] == idx[None, :]).astype(np.float32))


def _segsum(x, e):
    return jnp.dot(x, e, precision=HI, preferred_element_type=F32)


def _silu(x):
    return x * jax.nn.sigmoid(x)


def _silu_grad(x):
    s = jax.nn.sigmoid(x)
    return s * (1.0 + x * (1.0 - s))


def _shift_down(x, prev8, k):
    r = pltpu.roll(x, k, axis=0)
    p = pltpu.roll(prev8, k, axis=0)
    row = lax.broadcasted_iota(jnp.int32, prev8.shape, 0)
    head = jnp.where(row < k, p, r[:8])
    return jnp.concatenate([head, r[8:]], axis=0)


def _shift_up(x, next8, k):
    n = x.shape[0]
    r = pltpu.roll(x, n - k, axis=0)
    p = pltpu.roll(next8, 8 - k, axis=0)
    row = lax.broadcasted_iota(jnp.int32, next8.shape, 0)
    tail = jnp.where(row >= 8 - k, p, r[n - 8:])
    return jnp.concatenate([r[:n - 8], tail], axis=0)


def _causal_conv(x, prev8, w_ref, width, cols=slice(None)):
    y = x * w_ref[width - 1:width, cols]
    for k in range(1, width):
        y = y + _shift_down(x, prev8, k) * w_ref[width - 1 - k:width - k, cols]
    return y


def _causal_conv_bwd(x, prev8, dy, dnext8, w_ref, dw_ref, width, cols=slice(None)):
    dx = dy * w_ref[width - 1:width, cols]
    dw_ref[width - 1:width, cols] += jnp.sum(dy * x, axis=0, keepdims=True)
    for k in range(1, width):
        dx = dx + _shift_up(dy, dnext8, k) * w_ref[width - 1 - k:width - k, cols]
        dw_ref[width - 1 - k:width - k, cols] += jnp.sum(dy * _shift_down(x, prev8, k), axis=0, keepdims=True)
    return dx


HALO = 16


def _prev_spec(tt, width, tile=lambda i: i):
    return pl.BlockSpec((HALO, width), lambda i: (jnp.maximum(tile(i) * (tt // HALO) - 1, 0), 0))


def _prev8(p_ref, cols=slice(None)):
    return p_ref[:, cols].astype(F32)[HALO - 8:]


def _store_heads(ref, x):
    for h in range(HEADS):
        ref[h] = x[:, HEAD_DIM * h:HEAD_DIM * (h + 1)]


def _load_heads(ref):
    return jnp.concatenate([ref[h] for h in range(HEADS)], axis=1)


def _softplus(z):
    return jnp.maximum(z, 0.0) + jnp.log1p(jnp.exp(-jnp.abs(z)))


def _tri_masks(tt):
    r = lax.broadcasted_iota(jnp.int32, (tt, tt), 0)
    c = lax.broadcasted_iota(jnp.int32, (tt, tt), 1)
    same_chunk = lax.shift_right_logical(r, 6) == lax.shift_right_logical(c, 6)
    return r, c, same_chunk


def _gate_fwd(small, pbias, pscale):
    t = small.shape[0]
    tt = _pick(t, (256, 128))

    def body(x_ref, pb_ref, ps_ref, o_ref, carry_ref):
        @pl.when(pl.program_id(0) == 0)
        def _():
            carry_ref[...] = jnp.zeros_like(carry_ref)

        lane = lax.broadcasted_iota(jnp.int32, (tt, LANES), 1)
        z = x_ref[...] + pb_ref[...]
        log_f = jnp.where(lane < HEADS, -_softplus(-z), 0.0)
        g = jnp.where((lane >= 2 * HEADS) & (lane < 3 * HEADS), ps_ref[...] * _softplus(z), 0.0)
        r, c, same_chunk = _tri_masks(tt)
        lower = jnp.where(r >= c, 1.0, 0.0)
        lower_chunk = jnp.where((r >= c) & same_chunk, 1.0, 0.0)
        csum = jnp.dot(lower, log_f, precision=lax.Precision.HIGHEST, preferred_element_type=F32) + carry_ref[...]
        gc = jnp.dot(lower_chunk, g, precision=lax.Precision.HIGHEST, preferred_element_type=F32)
        carry_ref[...] += jnp.sum(log_f, axis=0, keepdims=True)
        o_ref[...] = jnp.where(lane < HEADS, csum, jnp.where(lane < 2 * HEADS, jax.nn.sigmoid(z), gc))

    row = pl.BlockSpec((tt, LANES), lambda i: (i, 0))
    vec = pl.BlockSpec((1, LANES), lambda i: (0, 0))
    return pl.pallas_call(
        body, grid=(t // tt,), name="gate_fwd", in_specs=[row, vec, vec], out_specs=row,
        out_shape=jax.ShapeDtypeStruct((t, LANES), F32),
        scratch_shapes=[pltpu.VMEM((1, LANES), F32)],
        compiler_params=_params("arbitrary"),
    )(small, pbias, pscale)


def _gate_bwd(small, pbias, pscale, dscal):
    t = small.shape[0]
    tt = _pick(t, (256, 128))
    nt = t // tt

    def body(x_ref, pb_ref, ps_ref, d_ref, dx_ref, dpb_ref, dps_ref, carry_ref):
        @pl.when(pl.program_id(0) == 0)
        def _():
            carry_ref[...] = jnp.zeros_like(carry_ref)
            dpb_ref[...] = jnp.zeros_like(dpb_ref)
            dps_ref[...] = jnp.zeros_like(dps_ref)

        lane = lax.broadcasted_iota(jnp.int32, (tt, LANES), 1)
        z = x_ref[...] + pb_ref[...]
        d = d_ref[...]
        dc = jnp.where(lane < HEADS, d, 0.0)
        dbeta = jnp.where((lane >= HEADS) & (lane < 2 * HEADS), d, 0.0)
        dgc = jnp.where((lane >= 2 * HEADS) & (lane < 3 * HEADS), d, 0.0)
        r, c, same_chunk = _tri_masks(tt)
        upper = jnp.where(r <= c, 1.0, 0.0)
        upper_chunk = jnp.where((r <= c) & same_chunk, 1.0, 0.0)
        dlogf = jnp.dot(upper, dc, precision=lax.Precision.HIGHEST, preferred_element_type=F32) + carry_ref[...]
        dg = jnp.dot(upper_chunk, dgc, precision=lax.Precision.HIGHEST, preferred_element_type=F32)
        carry_ref[...] += jnp.sum(dc, axis=0, keepdims=True)
        sg = jax.nn.sigmoid(z)
        dz = dlogf * (1.0 - sg) + dbeta * sg * (1.0 - sg) + dg * ps_ref[...] * sg
        dx_ref[...] = dz.astype(dx_ref.dtype)
        dpb_ref[...] += jnp.sum(dz, axis=0, keepdims=True)
        dps_ref[...] += jnp.sum(dg * _softplus(z), axis=0, keepdims=True)

    row = pl.BlockSpec((tt, LANES), lambda i: (nt - 1 - i, 0))
    vec = pl.BlockSpec((1, LANES), lambda i: (0, 0))
    return pl.pallas_call(
        body, grid=(nt,), name="gate_bwd", in_specs=[row, vec, vec, row], out_specs=[row, vec, vec],
        out_shape=[jax.ShapeDtypeStruct((t, LANES), BF16), jax.ShapeDtypeStruct((1, LANES), F32),
                   jax.ShapeDtypeStruct((1, LANES), F32)],
        scratch_shapes=[pltpu.VMEM((1, LANES), F32)],
        compiler_params=_params("arbitrary"),
    )(small, pbias, pscale, dscal)


def _gdn_pre_fwd(xg, conv_w, seg):
    t = xg.shape[0]
    c3 = 3 * WIDTH
    tt = _pick(t, (320, 256, 128))

    def body(x_ref, p_ref, w_ref, e_ref, q_ref, k_ref, v_ref):
        x = x_ref[...].astype(F32)
        prev = jnp.where(pl.program_id(0) == 0, 0.0, _prev8(p_ref))
        s = _silu(_causal_conv(x, prev, w_ref, GDN_CONV))
        e = e_ref[...]
        q = s[:, :WIDTH]
        k = s[:, WIDTH:2 * WIDTH]
        _store_heads(q_ref, q * lax.rsqrt(_segsum(q * q, e) + RMS_EPS) * (HEAD_DIM ** -0.5))
        _store_heads(k_ref, k * lax.rsqrt(_segsum(k * k, e) + RMS_EPS))
        _store_heads(v_ref, s[:, 2 * WIDTH:])

    out = pl.BlockSpec((HEADS, tt, HEAD_DIM), lambda i: (0, i, 0))
    shp = jax.ShapeDtypeStruct((HEADS, t, HEAD_DIM), F32)
    return pl.pallas_call(
        body, grid=(t // tt,), name="gdn_pre_fwd",
        in_specs=[pl.BlockSpec((tt, c3), lambda i: (i, 0)), _prev_spec(tt, c3),
                  pl.BlockSpec((GDN_CONV, c3), lambda i: (0, 0)), pl.BlockSpec((WIDTH, WIDTH), lambda i: (0, 0))],
        out_specs=[out, out, out], out_shape=[shp, shp, shp],
        compiler_params=_params("arbitrary"),
    )(xg, xg, conv_w, seg)


def _gdn_pre_bwd(xg, conv_w, seg, dqn, dkn, dv):
    t = xg.shape[0]
    c3 = 3 * WIDTH
    tt = _pick(t, (320, 256, 128))
    nt = t // tt

    def body(x_ref, p_ref, w_ref, e_ref, dq_ref, dk_ref, dv_ref, dx_ref, dw_ref, carry_ref):
        step = pl.program_id(0)
        x = x_ref[...].astype(F32)
        e = e_ref[...]
        prev = jnp.where(step == nt - 1, 0.0, _prev8(p_ref))
        y = _causal_conv(x, prev, w_ref, GDN_CONV)
        s = _silu(y)
        q = s[:, :WIDTH]
        k = s[:, WIDTH:2 * WIDTH]
        rq = lax.rsqrt(_segsum(q * q, e) + RMS_EPS)
        rk = lax.rsqrt(_segsum(k * k, e) + RMS_EPS)
        gq = _load_heads(dq_ref) * (HEAD_DIM ** -0.5)
        gk = _load_heads(dk_ref)
        dq = rq * gq - q * (rq * rq * rq) * _segsum(gq * q, e)
        dk = rk * gk - k * (rk * rk * rk) * _segsum(gk * k, e)
        dy = jnp.concatenate([dq, dk, _load_heads(dv_ref)], axis=1) * _silu_grad(y)

        @pl.when(step == 0)
        def _():
            carry_ref[...] = jnp.zeros_like(carry_ref)
            dw_ref[...] = jnp.zeros_like(dw_ref)

        dx = _causal_conv_bwd(x, prev, dy, carry_ref[...], w_ref, dw_ref, GDN_CONV)
        dx_ref[...] = dx.astype(dx_ref.dtype)
        carry_ref[...] = dy[:8]

    rev = lambda i: (nt - 1 - i, 0)
    blk = pl.BlockSpec((HEADS, tt, HEAD_DIM), lambda i: (0, nt - 1 - i, 0))
    return pl.pallas_call(
        body, grid=(nt,), name="gdn_pre_bwd",
        in_specs=[pl.BlockSpec((tt, c3), rev), _prev_spec(tt, c3, lambda i: nt - 1 - i),
                  pl.BlockSpec((GDN_CONV, c3), lambda i: (0, 0)), pl.BlockSpec((WIDTH, WIDTH), lambda i: (0, 0)),
                  blk, blk, blk],
        out_specs=[pl.BlockSpec((tt, c3), rev), pl.BlockSpec((GDN_CONV, c3), lambda i: (0, 0))],
        out_shape=[jax.ShapeDtypeStruct((t, c3), BF16), jax.ShapeDtypeStruct((GDN_CONV, c3), F32)],
        scratch_shapes=[pltpu.VMEM((8, c3), F32)],
        compiler_params=_params("arbitrary"),
    )(xg, xg, conv_w, seg, dqn, dkn, dv)


def _bmm(a, b, ca, cb, precision=None):
    return lax.dot_general(a, b, (((ca,), (cb,)), ((0,), (0,))), precision=precision, preferred_element_type=F32)


def _bf(x):
    return x.astype(BF16)


def _tri_inverse(a, eye):
    x = -a
    tinv = eye + x
    pw = x
    for _ in range(5):
        pb = _bf(pw)
        pw = _bmm(pb, pb, 2, 1)
        tinv = tinv + _bmm(_bf(tinv), _bf(pw), 2, 1)
    resid = eye - _bmm(eye + a, tinv, 2, 1, precision=HI)
    return tinv + _bmm(_bf(tinv), _bf(resid), 2, 1)


def _gdn_intra(q, k, v, bc, gcc, gcr):
    ii = lax.broadcasted_iota(jnp.int32, (CHUNK, CHUNK), 0)
    jj = lax.broadcasted_iota(jnp.int32, (CHUNK, CHUNK), 1)
    tril = (ii >= jj)[None]
    strict = (ii > jj)[None]
    eye = jnp.where(ii == jj, 1.0, 0.0).astype(F32)[None]
    last = (ii == CHUNK - 1)[None]
    dm = jnp.exp(jnp.where(tril, gcc - gcr, NEG))
    gam = jnp.exp(gcc)
    kb = k * bc
    vb = v * bc
    kk = _bmm(_bf(kb), _bf(k), 2, 2)
    a = jnp.where(strict, kk * dm, 0.0)
    tinv = _tri_inverse(a, eye)
    kbg = kb * gam
    u = _bmm(tinv, vb, 2, 1, precision=HI)
    wk = _bmm(tinv, kbg, 2, 1, precision=HI)
    qk = _bmm(_bf(q), _bf(k), 2, 2)
    p = jnp.where(tril, qk * dm, 0.0)
    gl = jnp.sum(jnp.where(last, gcc, 0.0), axis=1, keepdims=True)
    edec = jnp.exp(gl - gcc)
    return dict(tril=tril, strict=strict, dm=dm, gam=gam, kb=kb, kk=kk, a=a, tinv=tinv, u=u, wk=wk, qk=qk, p=p,
                qg=q * gam, kt=k * edec, edec=edec, gaml=jnp.exp(gl), last=last)


def _gate_tiles(sc, gct, nb):
    rows = nb * CHUNK
    cols = lambda lane0: jnp.stack([jnp.broadcast_to(sc[:, lane0 + h:lane0 + h + 1], (rows, HEAD_DIM))
                                    for h in range(HEADS)], axis=0).reshape(HEADS * nb, CHUNK, HEAD_DIM)
    gcr = jnp.stack([jnp.broadcast_to(gct[h:h + 1, n * CHUNK:(n + 1) * CHUNK], (CHUNK, CHUNK))
                     for h in range(HEADS) for n in range(nb)], axis=0)
    return cols(HEADS), cols(2 * HEADS), gcr


def _gdn_fwd(q, k, v, scal, gct, nb=None):
    h, t, dh = q.shape
    nc = t // CHUNK
    nb = nb or _pick(nc, (4, 2))
    bsz = h * nb

    def body(q_ref, k_ref, v_ref, sc_ref, gt_ref, o_ref, s0_ref, state_ref):
        @pl.when(pl.program_id(0) == 0)
        def _():
            state_ref[...] = jnp.zeros_like(state_ref)

        ld = lambda r: r[...].reshape(bsz, CHUNK, dh)
        bc, gcc, gcr = _gate_tiles(sc_ref[...], gt_ref[...], nb)
        z = _gdn_intra(ld(q_ref), ld(k_ref), ld(v_ref), bc, gcc, gcr)
        per = lambda x: x.reshape((h, nb) + x.shape[1:])
        u, wk, p, qg, kt, gaml = (per(z[n]) for n in ("u", "wk", "p", "qg", "kt", "gaml"))
        s = state_ref[...]
        for n in range(nb):
            s0_ref[:, n] = s
            sb = _bf(s)
            vn = u[:, n] - _bmm(_bf(wk[:, n]), sb, 2, 1)
            o_ref[:, n * CHUNK:(n + 1) * CHUNK, :] = _bmm(_bf(qg[:, n]), sb, 2, 1) + _bmm(_bf(p[:, n]), _bf(vn), 2, 1)
            s = s * gaml[:, n] + _bmm(_bf(kt[:, n]), _bf(vn), 1, 1)
        state_ref[...] = s

    blk = pl.BlockSpec((h, nb * CHUNK, dh), lambda i: (0, i, 0))
    return pl.pallas_call(
        body, grid=(nc // nb,), name="gdn_fwd",
        in_specs=[blk] * 3 + [pl.BlockSpec((nb * CHUNK, LANES), lambda i: (i, 0)),
                              pl.BlockSpec((h, nb * CHUNK), lambda i: (0, i))],
        out_specs=[blk, pl.BlockSpec((h, nb, dh, dh), lambda i: (0, i, 0, 0))],
        out_shape=[jax.ShapeDtypeStruct((h, t, dh), F32), jax.ShapeDtypeStruct((h, nc, dh, dh), F32)],
        scratch_shapes=[pltpu.VMEM((h, dh, dh), F32)],
        compiler_params=_params("arbitrary"),
    )(q, k, v, scal, gct)


def _gdn_bwd(q, k, v, scal, gct, s0s, do, nb=None):
    h, t, dh = q.shape
    nc = t // CHUNK
    nb = nb or _pick(nc, (2,))
    bsz = h * nb
    ng = nc // nb
    rows = nb * CHUNK

    def body(q_ref, k_ref, v_ref, sc_ref, gt_ref, s0_ref, do_ref,
             dq_ref, dk_ref, dv_ref, dsc_ref, dgt_ref, ds_ref):
        @pl.when(pl.program_id(0) == 0)
        def _():
            ds_ref[...] = jnp.zeros_like(ds_ref)

        ld = lambda r: r[...].reshape(bsz, CHUNK, dh)
        q, k, v = ld(q_ref), ld(k_ref), ld(v_ref)
        bc, gcc, gcr = _gate_tiles(sc_ref[...], gt_ref[...], nb)
        z = _gdn_intra(q, k, v, bc, gcc, gcr)
        per = lambda x: x.reshape((h, nb) + x.shape[1:])
        u, wk, p, qg, kt, gaml = (per(z[n]) for n in ("u", "wk", "p", "qg", "kt", "gaml"))
        dout = per(ld(do_ref))
        ds = ds_ref[...]
        d_u, d_wk, d_p, d_qg, d_kt, d_gaml = ([None] * nb for _ in range(6))
        for n in reversed(range(nb)):
            s0 = s0_ref[:, n]
            s0b, dsb, dob = _bf(s0), _bf(ds), _bf(dout[:, n])
            wkb, qgb = _bf(wk[:, n]), _bf(qg[:, n])
            vn = u[:, n] - _bmm(wkb, s0b, 2, 1)
            dvn = _bmm(_bf(p[:, n]), dob, 1, 1) + _bmm(_bf(kt[:, n]), dsb, 2, 1)
            dvnb = _bf(dvn)
            d_u[n] = dvn
            d_p[n] = _bmm(dob, _bf(vn), 2, 2)
            d_qg[n] = _bmm(dob, s0b, 2, 2)
            d_kt[n] = _bmm(_bf(vn), dsb, 2, 2)
            d_gaml[n] = jnp.sum(s0 * ds, axis=1, keepdims=True)
            d_wk[n] = -_bmm(dvnb, s0b, 2, 2)
            ds = _bmm(qgb, dob, 1, 1) + gaml[:, n] * ds - _bmm(wkb, dvnb, 1, 1)
        ds_ref[...] = ds

        flat = lambda xs: jnp.stack(xs, axis=1).reshape((bsz,) + xs[0].shape[1:])
        d_u, d_wk, d_p, d_qg, d_kt, d_gaml = (flat(x) for x in (d_u, d_wk, d_p, d_qg, d_kt, d_gaml))
        tinv, gam, kb, dm = z["tinv"], z["gam"], z["kb"], z["dm"]
        drv = _bmm(tinv, d_u, 1, 1, precision=HI)
        drk = _bmm(tinv, d_wk, 1, 1, precision=HI)
        da = -(_bmm(_bf(drv), _bf(z["u"]), 2, 2) + _bmm(_bf(drk), _bf(z["wk"]), 2, 2))
        da = jnp.where(z["strict"], da, 0.0)
        d_p = jnp.where(z["tril"], d_p, 0.0)
        dkk = _bf(da * dm)
        dqk = _bf(d_p * dm)
        dkb = _bmm(dkk, _bf(k), 2, 1) + drk * gam
        dk = _bmm(dkk, _bf(kb), 1, 1) + _bmm(dqk, _bf(q), 1, 1) + dkb * bc + d_kt * z["edec"]
        dq = _bmm(dqk, _bf(k), 2, 1) + d_qg * gam
        mm = da * z["a"] + d_p * z["p"]
        dkt_kt = d_kt * z["kt"]
        dgl = jnp.sum(dkt_kt, axis=1, keepdims=True) + d_gaml * z["gaml"]
        dgc = mm + d_qg * z["qg"] + drk * kb * gam - dkt_kt + jnp.where(z["last"], dgl, 0.0)
        dq_ref[...] = dq.reshape(h, rows, dh)
        dk_ref[...] = dk.reshape(h, rows, dh)
        dv_ref[...] = (drv * bc).reshape(h, rows, dh)
        dbeta = (dkb * k + drv * v).reshape(h, rows, dh)
        dgc = dgc.reshape(h, rows, dh)
        lane = lax.broadcasted_iota(jnp.int32, (rows, LANES), 1)
        dsc = jnp.zeros((rows, LANES), F32)
        for hh in range(h):
            dsc = jnp.where(lane == HEADS + hh, jnp.sum(dbeta[hh], axis=1, keepdims=True), dsc)
            dsc = jnp.where(lane == 2 * HEADS + hh, jnp.sum(dgc[hh], axis=1, keepdims=True), dsc)
        dsc_ref[...] = dsc
        dgr = -jnp.sum(mm, axis=1, keepdims=True)
        for hh in range(h):
            for n in range(nb):
                dgt_ref[hh:hh + 1, n * CHUNK:(n + 1) * CHUNK] = dgr[hh * nb + n]

    blk = pl.BlockSpec((h, rows, dh), lambda i: (0, ng - 1 - i, 0))
    shp = jax.ShapeDtypeStruct((h, t, dh), F32)
    sc_spec = pl.BlockSpec((rows, LANES), lambda i: (ng - 1 - i, 0))
    gt_spec = pl.BlockSpec((h, rows), lambda i: (0, ng - 1 - i))
    return pl.pallas_call(
        body, grid=(ng,), name="gdn_bwd",
        in_specs=[blk] * 3 + [sc_spec, gt_spec, pl.BlockSpec((h, nb, dh, dh), lambda i: (0, ng - 1 - i, 0, 0)), blk],
        out_specs=[blk] * 3 + [sc_spec, gt_spec],
        out_shape=[shp] * 3 + [jax.ShapeDtypeStruct((t, LANES), F32), jax.ShapeDtypeStruct((h, t), F32)],
        scratch_shapes=[pltpu.VMEM((h, dh, dh), F32)],
        compiler_params=_params("arbitrary"),
    )(q, k, v, scal, gct, s0s, do)


def _gdn_post_fwd(o, xg, gain, seg):
    t = o.shape[1]
    tt = _pick(t, (320, 256, 128))

    def body(o_ref, z_ref, g_ref, e_ref, y_ref):
        x = _load_heads(o_ref)
        r = lax.rsqrt(_segsum(x * x, e_ref[...]) * (1.0 / HEAD_DIM) + RMS_EPS)
        y_ref[...] = (x * r * g_ref[...] * _silu(z_ref[...].astype(F32))).astype(y_ref.dtype)

    return pl.pallas_call(
        body, grid=(t // tt,), name="gdn_post_fwd",
        in_specs=[pl.BlockSpec((HEADS, tt, HEAD_DIM), lambda i: (0, i, 0)), pl.BlockSpec((tt, WIDTH), lambda i: (i, 3)),
                  pl.BlockSpec((1, WIDTH), lambda i: (0, 0)), pl.BlockSpec((WIDTH, WIDTH), lambda i: (0, 0))],
        out_specs=pl.BlockSpec((tt, WIDTH), lambda i: (i, 0)),
        out_shape=jax.ShapeDtypeStruct((t, WIDTH), BF16),
        compiler_params=_params("arbitrary"),
    )(o, xg, gain, seg)


def _gdn_post_bwd(o, xg, gain, seg, dy):
    t = o.shape[1]
    tt = _pick(t, (320, 256, 128))

    def body(o_ref, z_ref, g_ref, e_ref, dy_ref, do_ref, dz_ref, dg_ref):
        x = _load_heads(o_ref)
        zz = z_ref[...].astype(F32)
        e = e_ref[...]
        gain_v = g_ref[...]
        d = dy_ref[...]
        r = lax.rsqrt(_segsum(x * x, e) * (1.0 / HEAD_DIM) + RMS_EPS)
        xr = x * r
        don = d * _silu(zz)
        dz_ref[...] = (d * xr * gain_v * _silu_grad(zz)).astype(dz_ref.dtype)
        gy = don * gain_v
        _store_heads(do_ref, r * gy - xr * (r * r) * (_segsum(gy * x, e) * (1.0 / HEAD_DIM)))

        @pl.when(pl.program_id(0) == 0)
        def _():
            dg_ref[...] = jnp.zeros_like(dg_ref)

        dg_ref[...] += jnp.sum(don * xr, axis=0, keepdims=True)

    row = pl.BlockSpec((tt, WIDTH), lambda i: (i, 0))
    vec = pl.BlockSpec((1, WIDTH), lambda i: (0, 0))
    hm = pl.BlockSpec((HEADS, tt, HEAD_DIM), lambda i: (0, i, 0))
    return pl.pallas_call(
        body, grid=(t // tt,), name="gdn_post_bwd",
        in_specs=[hm, pl.BlockSpec((tt, WIDTH), lambda i: (i, 3)), vec,
                  pl.BlockSpec((WIDTH, WIDTH), lambda i: (0, 0)), row],
        out_specs=[hm, row, vec],
        out_shape=[jax.ShapeDtypeStruct((HEADS, t, HEAD_DIM), F32), jax.ShapeDtypeStruct((t, WIDTH), BF16),
                   jax.ShapeDtypeStruct((1, WIDTH), F32)],
        compiler_params=_params("arbitrary"),
    )(o, xg, gain, seg, dy)


def _mix_fwd(yf, yg, gates, bias):
    t, d = yf.shape
    tt = _pick(t, (320, 256, 128))

    def body(yf_ref, yg_ref, g1_ref, g2_ref, b1_ref, b2_ref, o_ref):
        g1 = jax.nn.sigmoid(g1_ref[...].astype(F32) + b1_ref[...])
        g2 = jax.nn.sigmoid(g2_ref[...].astype(F32) + b2_ref[...])
        o_ref[...] = (g1 * yf_ref[...] + g2 * yg_ref[...]).astype(o_ref.dtype)

    row = pl.BlockSpec((tt, d), lambda i: (i, 0))
    return pl.pallas_call(
        body, grid=(t // tt,), name="mix_fwd",
        in_specs=[row, row, row, pl.BlockSpec((tt, d), lambda i: (i, 1)),
                  pl.BlockSpec((1, d), lambda i: (0, 0)), pl.BlockSpec((1, d), lambda i: (0, 1))],
        out_specs=row, out_shape=jax.ShapeDtypeStruct((t, d), BF16),
        compiler_params=_params("arbitrary"),
    )(yf, yg, gates, gates, bias, bias)


def _mix_bwd(dmix, yf, yg, gates, bias):
    t, d = yf.shape
    tt = _pick(t, (320, 256, 128))

    def body(dm_ref, yf_ref, yg_ref, g1_ref, g2_ref, b1_ref, b2_ref, dyf_ref, dyg_ref, dg_ref, db_ref):
        dm = dm_ref[...]
        g1 = jax.nn.sigmoid(g1_ref[...].astype(F32) + b1_ref[...])
        g2 = jax.nn.sigmoid(g2_ref[...].astype(F32) + b2_ref[...])
        dyf_ref[...] = (dm * g1).astype(BF16)
        dyg_ref[...] = (dm * g2).astype(BF16)
        dgate = jnp.concatenate([dm * yf_ref[...] * g1 * (1.0 - g1), dm * yg_ref[...] * g2 * (1.0 - g2)], axis=1)
        dg_ref[...] = dgate.astype(BF16)

        @pl.when(pl.program_id(0) == 0)
        def _():
            db_ref[...] = jnp.zeros_like(db_ref)

        db_ref[...] += jnp.sum(dgate, axis=0, keepdims=True)

    row = pl.BlockSpec((tt, d), lambda i: (i, 0))
    wide = pl.BlockSpec((tt, 2 * d), lambda i: (i, 0))
    return pl.pallas_call(
        body, grid=(t // tt,), name="mix_bwd",
        in_specs=[row, row, row, row, pl.BlockSpec((tt, d), lambda i: (i, 1)),
                  pl.BlockSpec((1, d), lambda i: (0, 0)), pl.BlockSpec((1, d), lambda i: (0, 1))],
        out_specs=[row, row, wide, pl.BlockSpec((1, 2 * d), lambda i: (0, 0))],
        out_shape=[jax.ShapeDtypeStruct((t, d), BF16), jax.ShapeDtypeStruct((t, d), BF16),
                   jax.ShapeDtypeStruct((t, 2 * d), BF16), jax.ShapeDtypeStruct((1, 2 * d), F32)],
        compiler_params=_params("arbitrary"),
    )(dmix, yf, yg, gates, gates, bias, bias)


def _ffn_act_fwd(up, conv_w, conv_b):
    t, c = up.shape
    tt = 128

    def body(x_ref, p_ref, w_ref, b_ref, o_ref):
        first = pl.program_id(0) == 0

        def conv(cols):
            prev = jnp.where(first, 0.0, _prev8(p_ref, cols))
            return _causal_conv(x_ref[:, cols].astype(F32), prev, w_ref, FFN_CONV, cols) + b_ref[:, cols]

        for lo in range(0, D_FF, FFN_LANES):
            gate = conv(slice(lo, lo + FFN_LANES))
            val = conv(slice(D_FF + lo, D_FF + lo + FFN_LANES))
            o_ref[:, lo:lo + FFN_LANES] = (_silu(gate) * val).astype(o_ref.dtype)

    return pl.pallas_call(
        body, grid=(t // tt,), name="ffn_act_fwd",
        in_specs=[pl.BlockSpec((tt, c), lambda i: (i, 0)), _prev_spec(tt, c),
                  pl.BlockSpec((FFN_CONV, c), lambda i: (0, 0)), pl.BlockSpec((1, c), lambda i: (0, 0))],
        out_specs=pl.BlockSpec((tt, D_FF), lambda i: (i, 0)),
        out_shape=jax.ShapeDtypeStruct((t, D_FF), BF16),
        compiler_params=_params("arbitrary"),
    )(up, up, conv_w, conv_b)


def _ffn_act_bwd(up, conv_w, conv_b, dact):
    t, c = up.shape
    tt = 128
    nt = t // tt

    def body(x_ref, p_ref, w_ref, b_ref, da_ref, dx_ref, dw_ref, db_ref, carry_ref):
        step = pl.program_id(0)

        @pl.when(step == 0)
        def _():
            carry_ref[...] = jnp.zeros_like(carry_ref)
            dw_ref[...] = jnp.zeros_like(dw_ref)
            db_ref[...] = jnp.zeros_like(db_ref)

        def conv(cols):
            x = x_ref[:, cols].astype(F32)
            prev = jnp.where(step == nt - 1, 0.0, _prev8(p_ref, cols))
            return x, prev, _causal_conv(x, prev, w_ref, FFN_CONV, cols) + b_ref[:, cols]

        def back(cols, x, prev, du):
            dx = _causal_conv_bwd(x, prev, du, carry_ref[:, cols], w_ref, dw_ref, FFN_CONV, cols)
            dx_ref[:, cols] = dx.astype(dx_ref.dtype)
            db_ref[:, cols] += jnp.sum(du, axis=0, keepdims=True)
            carry_ref[:, cols] = du[:8]

        for lo in range(0, D_FF, FFN_LANES):
            gcols, vcols = slice(lo, lo + FFN_LANES), slice(D_FF + lo, D_FF + lo + FFN_LANES)
            xg, pg, gate = conv(gcols)
            xv, pv, val = conv(vcols)
            da = da_ref[:, gcols]
            back(gcols, xg, pg, da * val * _silu_grad(gate))
            back(vcols, xv, pv, da * _silu(gate))

    rev = lambda i: (nt - 1 - i, 0)
    return pl.pallas_call(
        body, grid=(nt,), name="ffn_act_bwd",
        in_specs=[pl.BlockSpec((tt, c), rev),
                  _prev_spec(tt, c, lambda i: nt - 1 - i),
                  pl.BlockSpec((FFN_CONV, c), lambda i: (0, 0)), pl.BlockSpec((1, c), lambda i: (0, 0)),
                  pl.BlockSpec((tt, D_FF), rev)],
        out_specs=[pl.BlockSpec((tt, c), rev), pl.BlockSpec((FFN_CONV, c), lambda i: (0, 0)),
                   pl.BlockSpec((1, c), lambda i: (0, 0))],
        out_shape=[jax.ShapeDtypeStruct((t, c), BF16), jax.ShapeDtypeStruct((FFN_CONV, c), F32),
                   jax.ShapeDtypeStruct((1, c), F32)],
        scratch_shapes=[pltpu.VMEM((8, c), F32)],
        compiler_params=_params("arbitrary"),
    )(up, up, conv_w, conv_b, dact)


def _final_loss(h2, target, gain, seq):
    t, d = h2.shape
    tr = _pick(t, (320, 256, 128))

    def body(h_ref, t_ref, g_ref, loss_ref, dh_ref, dhb_ref, dg_ref):
        i = pl.program_id(0)
        x = h_ref[...]
        gain_v = g_ref[...]
        r = lax.rsqrt(jnp.mean(x * x, axis=-1, keepdims=True) + RMS_EPS)
        xr = x * r
        rows = i * tr + lax.broadcasted_iota(jnp.int32, (tr, 1), 0)
        real = (rows >= N_META) & (rows < N_META + seq)
        err = jnp.where(real, xr * gain_v - t_ref[...], 0.0)
        dy = err * (1.0 / d)
        gy = dy * gain_v
        dh = r * (gy - xr * jnp.mean(gy * xr, axis=-1, keepdims=True))
        dh_ref[...] = dh
        dhb_ref[...] = dh.astype(BF16)

        @pl.when(i == 0)
        def _():
            loss_ref[...] = jnp.zeros_like(loss_ref)
            dg_ref[...] = jnp.zeros_like(dg_ref)

        part = jnp.sum(jnp.sum(err * err, axis=-1, keepdims=True), axis=0, keepdims=True)
        loss_ref[...] += jnp.broadcast_to(part * (0.5 / d), loss_ref.shape)
        dg_ref[...] += jnp.sum(dy * xr, axis=0, keepdims=True)

    row = pl.BlockSpec((tr, d), lambda i: (i, 0))
    vec = pl.BlockSpec((1, d), lambda i: (0, 0))
    return pl.pallas_call(
        body, grid=(t // tr,), name="final_loss",
        in_specs=[row, row, vec],
        out_specs=[pl.BlockSpec((1, LANES), lambda i: (0, 0)), row, row, vec],
        out_shape=[jax.ShapeDtypeStruct((1, LANES), F32), jax.ShapeDtypeStruct((t, d), F32),
                   jax.ShapeDtypeStruct((t, d), BF16), jax.ShapeDtypeStruct((1, d), F32)],
        compiler_params=_params("arbitrary"),
    )(h2, target, gain)


ADAM_TILE_BYTES = 1 << 20


def _adamw(w, m, v, grecv, name):
    r, cols = w.shape
    tr = r
    if r * cols * 4 > ADAM_TILE_BYTES:
        tr = max(d for d in range(8, r + 1, 8) if r % d == 0 and d * cols * 4 <= ADAM_TILE_BYTES)

    def body(w_ref, m_ref, v_ref, g_ref, go_ref, d_ref, mo_ref, vo_ref):
        g = g_ref[0].astype(F32)
        for s in range(1, N_DEV):
            g = g + g_ref[s].astype(F32)
        wv = w_ref[...]
        mn = ADAM_B1 * m_ref[...] + (1.0 - ADAM_B1) * g
        vn = ADAM_B2 * v_ref[...] + (1.0 - ADAM_B2) * (g * g)
        m_hat = mn / (1.0 - ADAM_B1 ** ADAM_STEP)
        v_hat = vn / (1.0 - ADAM_B2 ** ADAM_STEP)
        go_ref[...] = g
        d_ref[...] = -ADAM_LR * (m_hat / (jnp.sqrt(v_hat) + ADAM_EPS) + ADAM_WD * wv)
        mo_ref[...] = mn
        vo_ref[...] = vn

    row = pl.BlockSpec((tr, cols), lambda i: (i, 0))
    shp = jax.ShapeDtypeStruct((r, cols), F32)
    return pl.pallas_call(
        body, grid=(r // tr,), name=name,
        in_specs=[row, row, row, pl.BlockSpec((N_DEV, tr, cols), lambda i: (0, i, 0))],
        out_specs=[row] * 4, out_shape=[shp] * 4,
        compiler_params=_params("parallel"),
    )(w, m, v, grecv)


def _mesh_pos():
    return lax.axis_index("x"), lax.axis_index("y"), lax.axis_index("c")


def _all_gather(shards):
    n = len(shards)

    def body(*refs):
        x_refs, out_refs = refs[:n], refs[n:2 * n]
        send_sems, recv_sems, local_sems = refs[2 * n:]
        x, y, c = _mesh_pos()
        me, sibling = (x, y, c), (x, y, 1 - c)
        chips = [(1 - x, y), (x, 1 - y), (1 - x, 1 - y)]

        def slot(a, px, py, pc):
            return out_refs[a].at[4 * px + 2 * py + pc]

        def copy(a, kk, block, to, src=None):
            return pltpu.make_async_remote_copy(
                src_ref=slot(a, *block) if src is None else src, dst_ref=slot(a, *block),
                send_sem=send_sems.at[7 * a + kk], recv_sem=recv_sems.at[7 * a + kk],
                device_id=to, device_id_type=MESH_ID)

        mine = [pltpu.make_async_copy(x_refs[a], slot(a, *me), local_sems.at[a]) for a in range(n)]
        first = []
        for a in range(n):
            first.append(copy(a, 0, me, sibling, src=x_refs[a]))
            first += [copy(a, 1 + j, me, (*chip, c), src=x_refs[a]) for j, chip in enumerate(chips)]
        for cp in mine + first:
            cp.start()
        passed = []
        for j, chip in enumerate(chips):
            for a in range(n):
                copy(a, 1 + j, (*chip, c), me).wait_recv()
                passed.append(copy(a, 4 + j, (*chip, c), sibling))
                passed[-1].start()
        for a in range(n):
            copy(a, 0, sibling, me).wait_recv()
        for j, chip in enumerate(chips):
            for a in range(n):
                copy(a, 4 + j, (*chip, 1 - c), me).wait_recv()
        for cp in first + passed:
            cp.wait_send()
        for cp in mine:
            cp.wait()

    hbm = pl.BlockSpec(memory_space=pl.ANY)
    return pl.pallas_call(
        body, name="weight_all_gather", in_specs=[hbm] * n, out_specs=[hbm] * n,
        out_shape=[jax.ShapeDtypeStruct((N_DEV,) + s.shape, s.dtype) for s in shards],
        scratch_shapes=[pltpu.SemaphoreType.DMA((7 * n,)), pltpu.SemaphoreType.DMA((7 * n,)),
                        pltpu.SemaphoreType.DMA((n,))],
    )(*shards)


def _grad_exchange(blocks, small):
    n = len(blocks)

    def body(*refs):
        src_refs, dst_refs = refs[:n + 1], refs[n + 1:2 * n + 2]
        send_sems, recv_sems, local_sems = refs[2 * n + 2:]
        x, y, c = _mesh_pos()
        me = 4 * x + 2 * y + c
        copies = []
        for kk in range(1, N_DEV):
            px = 1 - x if kk & 4 else x
            py = 1 - y if kk & 2 else y
            pc = 1 - c if kk & 1 else c
            peer = 4 * px + 2 * py + pc
            for a in range(n + 1):
                copies.append(pltpu.make_async_remote_copy(
                    src_ref=src_refs[a].at[peer] if a < n else src_refs[a], dst_ref=dst_refs[a].at[me],
                    send_sem=send_sems.at[7 * a + kk - 1], recv_sem=recv_sems.at[7 * a + kk - 1],
                    device_id=(px, py, pc), device_id_type=MESH_ID))
        own = [pltpu.make_async_copy(src_refs[a].at[me] if a < n else src_refs[a], dst_refs[a].at[me],
                                     local_sems.at[a]) for a in range(n + 1)]
        for cp in own + copies:
            cp.start()
        for cp in copies + own:
            cp.wait()

    hbm = pl.BlockSpec(memory_space=pl.ANY)
    return pl.pallas_call(
        body, name="grad_exchange", in_specs=[hbm] * (n + 1), out_specs=[hbm] * (n + 1),
        out_shape=[jax.ShapeDtypeStruct(b.shape, b.dtype) for b in blocks]
        + [jax.ShapeDtypeStruct((N_DEV,) + small.shape, small.dtype)],
        scratch_shapes=[pltpu.SemaphoreType.DMA((7 * (n + 1),)), pltpu.SemaphoreType.DMA((7 * (n + 1),)),
                        pltpu.SemaphoreType.DMA((n + 1,))],
    )(*blocks, small)


def _exchange_copies(src_refs, land_refs, send_sems, recv_sems):
    x, y, c = _mesh_pos()
    me = 4 * x + 2 * y + c
    copies = []
    for kk in range(1, N_DEV):
        px = 1 - x if kk & 4 else x
        py = 1 - y if kk & 2 else y
        pc = 1 - c if kk & 1 else c
        for a, (src, land) in enumerate(zip(src_refs, land_refs)):
            copies.append(pltpu.make_async_remote_copy(
                src_ref=src.at[4 * px + 2 * py + pc], dst_ref=land.at[me],
                send_sem=send_sems.at[7 * a + kk - 1], recv_sem=recv_sems.at[7 * a + kk - 1],
                device_id=(px, py, pc), device_id_type=MESH_ID))
    return copies


def _gather_copies(src_refs, land_refs, send_sems, recv_sems):
    x, y, c = _mesh_pos()
    me = 4 * x + 2 * y + c
    copies = []
    for kk in range(1, N_DEV):
        px = 1 - x if kk & 4 else x
        py = 1 - y if kk & 2 else y
        pc = 1 - c if kk & 1 else c
        for a, (src, land) in enumerate(zip(src_refs, land_refs)):
            copies.append(pltpu.make_async_remote_copy(
                src_ref=src, dst_ref=land.at[me],
                send_sem=send_sems.at[7 * a + kk - 1], recv_sem=recv_sems.at[7 * a + kk - 1],
                device_id=(px, py, pc), device_id_type=MESH_ID))
    return copies


_HBM = pl.BlockSpec(memory_space=pltpu.HBM)
_SEM = pl.BlockSpec(memory_space=pltpu.SEMAPHORE)
_DATAFLOW = pltpu.SideEffectType.DATAFLOW_SIDE_EFFECTING


def _split_start(name, make_copies, sources, land_shapes):
    n = len(sources)

    def body(*refs):
        src_refs, land_refs, send_sems, recv_sems = refs[:n], refs[n:2 * n], refs[2 * n], refs[2 * n + 1]
        for cp in make_copies(src_refs, land_refs, send_sems, recv_sems):
            cp.start()
        token = refs[-1]
        token[...] = jnp.zeros_like(token)

    in_hbm = lambda a: pltpu.with_memory_space_constraint(a, pltpu.HBM)
    hbm_shapes = [pltpu.HBM(s.shape, s.dtype) for s in list(sources) + list(land_shapes)]
    outs = pl.pallas_call(
        body, name=name, in_specs=[_HBM] * (2 * n),
        out_shape=(pltpu.SemaphoreType.DMA((7 * n,)), pltpu.SemaphoreType.DMA((7 * n,)), *hbm_shapes,
                   jax.ShapeDtypeStruct((8, LANES), F32)),
        out_specs=(_SEM, _SEM, *[_HBM] * (2 * n), pl.BlockSpec(memory_space=pltpu.VMEM)),
        input_output_aliases={a: 2 + a for a in range(2 * n)},
        compiler_params=pltpu.CompilerParams(has_side_effects=_DATAFLOW),
    )(*[in_hbm(s) for s in sources], *[in_hbm(lax.empty(s.shape, s.dtype)) for s in land_shapes])
    return outs[0], outs[1], outs[2:2 + n], outs[2 + n:2 + 2 * n], outs[-1]


def _split_wait(name, make_copies, send_sems, recv_sems, src_thru, land_thru, after):
    n = len(src_thru)

    def body(*refs):
        src_refs, land_refs, send_sems, recv_sems = refs[:n], refs[n:2 * n], refs[2 * n], refs[2 * n + 1]
        for cp in make_copies(src_refs, land_refs, send_sems, recv_sems):
            cp.wait_send()
            cp.wait_recv()

    outs = pl.pallas_call(
        body, name=name,
        in_specs=[_HBM] * (2 * n) + [_SEM, _SEM, pl.BlockSpec(memory_space=pl.ANY)],
        out_shape=tuple(pltpu.HBM(b.shape, b.dtype) for b in list(src_thru) + list(land_thru)),
        out_specs=[_HBM] * (2 * n), input_output_aliases={a: a for a in range(2 * n)},
        compiler_params=pltpu.CompilerParams(has_side_effects=_DATAFLOW),
    )(*src_thru, *land_thru, send_sems, recv_sems, after)
    return outs[:n], outs[n:]


def _exchange_start(blocks):
    return _split_start("grad_exchange_start", _exchange_copies, blocks, blocks)


def _exchange_wait(send_sems, recv_sems, src_thru, land_thru, after):
    return _split_wait("grad_exchange_wait", _exchange_copies, send_sems, recv_sems, src_thru, land_thru, after)


def _gather_start(shards):
    lands = [jax.ShapeDtypeStruct((N_DEV,) + s.shape, s.dtype) for s in shards]
    return _split_start("weight_gather_start", _gather_copies, shards, lands)


def _gather_wait(send_sems, recv_sems, src_thru, land_thru, after):
    return _split_wait("weight_gather_wait", _gather_copies, send_sems, recv_sems, src_thru, land_thru, after)


def _pad_flat(parts, rows):
    flat = jnp.concatenate([p.reshape(-1) for p in parts])
    return jnp.pad(flat, (0, rows * LANES - flat.shape[0])).reshape(rows, LANES)


def _rows_for(n_elems, mult=1024):
    rows = -(-n_elems // LANES)
    return -(-rows // mult) * mult


SHARDED = ("meta_tokens", "w_in", "gdn_conv_w", "w_branch_fox", "w_branch_gdn", "w_out", "ffn_w_up", "ffn_conv_w",
           "ffn_w_down")
MATMUL = ("w_in", "w_branch_fox", "w_branch_gdn", "w_out", "ffn_w_up", "ffn_w_down")
EXACT = ("meta_tokens", "gdn_conv_w", "ffn_conv_w")
REPLICATED = ("fgt_bias", "gdn_a_log", "gdn_dt_bias", "gdn_norm_w", "gate_bias", "norm_mix_w", "norm_ffn_w",
              "ffn_conv_b", "norm_final_w")
WEIGHTS = ("meta_tokens", "w_in", "fgt_bias", "gdn_conv_w", "gdn_a_log", "gdn_dt_bias", "gdn_norm_w", "gate_bias",
           "w_branch_fox", "w_branch_gdn", "w_out", "norm_mix_w", "norm_ffn_w", "ffn_w_up", "ffn_conv_w",
           "ffn_conv_b", "ffn_w_down", "norm_final_w")


def _unpack(buf, shapes):
    flat = buf.reshape(-1)
    out, off = [], 0
    for s in shapes:
        n = int(np.prod(s))
        out.append(flat[off:off + n].reshape(s))
        off += n
    return out


def _unpack_gathered(buf, shapes):
    flat = buf.reshape(N_DEV, -1)
    out, off = [], 0
    for s in shapes:
        n = int(np.prod(s))
        out.append(flat[:, off:off + n].reshape((N_DEV,) + tuple(s)))
        off += n
    return out


def _cat_cols(g):
    return g.transpose(1, 0, 2).reshape(g.shape[1], -1)


def _col_blocks(full, width):
    return full.reshape(full.shape[0], N_DEV, width).transpose(1, 0, 2)


def _local_step(x, target, w, early=None, late_weights=None):
    seq = x.shape[0]
    t = _padded_tokens(seq)
    pad = t - N_META - seq
    seg = _seg_matrix()
    zrows = jnp.zeros((pad, D_MODEL), F32)
    h0 = jnp.concatenate([w["meta_tokens"], x, zrows], axis=0)
    tgt = jnp.concatenate([jnp.zeros((N_META, D_MODEL), F32), target, zrows], axis=0)

    w_in = w["w_in"]
    o_f, o_g, o_z, o_b, o_a, o_gate = 1536, 1544, 3080, 3592, 3600, 3608
    w_small = jnp.concatenate([w_in[:, o_f:o_f + 8], w_in[:, o_b:o_b + 8], w_in[:, o_a:o_a + 8],
                               jnp.zeros((D_MODEL, LANES - 24), BF16)], axis=1)
    w_r = jnp.concatenate([w_in[:, :1536], w_in[:, o_g:o_z], w_in[:, o_z:o_b], w_in[:, o_gate:], w_small], axis=1)

    a1 = _rmsnorm_fwd(h0, w["norm_mix_w"])
    fq = _mm(a1, w_r[:, :1536], BF16, "proj_fox")
    xg = _mm(a1, w_r[:, 1536:3584], BF16, "proj_gdn")
    gt = _mm(a1, w_r[:, 3584:5632], BF16, "proj_gates")
    sm = _mm(a1, w_r[:, 5632:], F32, "proj_small")

    lanes_pad = lambda a, lo: jnp.pad(a, ((0, 0), (lo, LANES - lo - a.shape[1])))
    neg_exp_a = -jnp.exp(w["gdn_a_log"])
    pbias = lanes_pad(w["fgt_bias"], 0) + lanes_pad(w["gdn_dt_bias"], 2 * HEADS)
    if late_weights is not None:
        pbias = pbias + late_weights[0][0, 0]
    pscale = lanes_pad(neg_exp_a, 2 * HEADS)
    scal = _gate_fwd(sm, pbias, pscale)
    gct = scal[:, 2 * HEADS:3 * HEADS].T

    qa, ka, va, kat, vat = _fox_prep(fq, scal)
    oa, qb, qbt = _fox_fwd(qa, ka, vat)
    o_fox = _fox_post(oa)

    qh, kh, vh = _gdn_pre_fwd(xg, w["gdn_conv_w"], seg)
    og, s0s = _gdn_fwd(qh, kh, vh, scal, gct)
    norm_w = jnp.tile(w["gdn_norm_w"], (1, HEADS))
    ogn = _gdn_post_fwd(og, xg, norm_w, seg)

    if late_weights is not None:
        w = {**w, **late_weights[1](ogn)}
    yf = _mm(o_fox, w["w_branch_fox"], F32, "branch_fox")
    yg = _mm(ogn, w["w_branch_gdn"], F32, "branch_gdn")
    mix = _mix_fwd(yf, yg, gt, w["gate_bias"])
    h1 = _mm(mix, w["w_out"], F32, "out_proj", res=h0)
    a2 = _rmsnorm_fwd(h1, w["norm_ffn_w"])
    up = _mm(a2, w["ffn_w_up"], BF16, "ffn_up")
    act = _ffn_act_fwd(up, w["ffn_conv_w"], w["ffn_conv_b"])
    h2 = _mm(act, w["ffn_w_down"], F32, "ffn_down", res=h1)
    loss, dh2, dh2b, g_final = _final_loss(h2, tgt, w["norm_final_w"].reshape(1, D_MODEL), seq)

    grads = {"norm_final_w": g_final.reshape(D_MODEL)}
    grads["ffn_w_down"] = _mm_tn(act, dh2b, "wgrad_ffn_down")
    dact = _mm(dh2b, w["ffn_w_down"].T, F32, "dgrad_ffn_down")
    dup, g_cw, g_cb = _ffn_act_bwd(up, w["ffn_conv_w"], w["ffn_conv_b"], dact)
    grads["ffn_conv_w"], grads["ffn_conv_b"] = g_cw, g_cb
    grads["ffn_w_up"] = _mm_tn(a2, dup, "wgrad_ffn_up")
    da2 = _mm(dup, w["ffn_w_up"].T, F32, "dgrad_ffn_up")
    dh1, dh1b, grads["norm_ffn_w"] = _rmsnorm_bwd(h1, da2, w["norm_ffn_w"], dh2)
    grads["w_out"] = _mm_tn(mix, dh1b, "wgrad_out")
    dmix = _mm(dh1b, w["w_out"].T, F32, "dgrad_out")
    dyf, dyg, dgt, grads["gate_bias"] = _mix_bwd(dmix, yf, yg, gt, w["gate_bias"])
    grads["w_branch_fox"] = _mm_tn(o_fox, dyf, "wgrad_branch_fox")
    grads["w_branch_gdn"] = _mm_tn(ogn, dyg, "wgrad_branch_gdn")
    do_fox = _mm(dyf, w["w_branch_fox"].T, F32, "dgrad_branch_fox")
    dogn = _mm(dyg, w["w_branch_gdn"].T, F32, "dgrad_branch_gdn")

    dog, dz, g_nw = _gdn_post_bwd(og, xg, norm_w, seg, dogn)
    grads["gdn_norm_w"] = g_nw.reshape(HEADS, HEAD_DIM).sum(axis=0)[None]
    dqh, dkh, dvh, dscal_g, dgct = _gdn_bwd(qh, kh, vh, scal, gct, s0s, dog)
    dxg, grads["gdn_conv_w"] = _gdn_pre_bwd(xg, w["gdn_conv_w"], seg, dqh, dkh, dvh)

    doa, doat = _fox_bwd_prep(do_fox, oa)
    dfq, dscal_c = _fox_bwd_post(*_fox_bwd(qb, qbt, ka, kat, va, doa, doat))

    dscal = dscal_c + dscal_g + lanes_pad(dgct.T, 2 * HEADS)
    dsm, dpb, dps = _gate_bwd(sm, pbias, pscale, dscal)
    grads["fgt_bias"] = dpb[:, :HEADS]
    grads["gdn_dt_bias"] = dpb[:, 2 * HEADS:3 * HEADS]
    grads["gdn_a_log"] = dps[:, 2 * HEADS:3 * HEADS] * neg_exp_a

    dproj = jnp.concatenate([dfq, dxg, dz, dgt, dsm], axis=1)
    g_r = _mm_tn(a1, dproj, "wgrad_in")
    grads["w_in"] = jnp.concatenate([g_r[:, :1536], g_r[:, 5632:5640], g_r[:, 1536:3072], g_r[:, 3072:3584],
                                     g_r[:, 5640:5648], g_r[:, 5648:5656], g_r[:, 3584:5632]], axis=1)
    token, handle = early(grads) if early is not None else (jnp.zeros((8, LANES), F32), None)
    w_rt = w_r.T + token[0, 0].astype(BF16)
    da1 = _mm(dproj, w_rt, F32, "dgrad_in")
    dh0, _, grads["norm_mix_w"] = _rmsnorm_bwd(h0, da1, w["norm_mix_w"], dh1)
    grads["meta_tokens"] = dh0[:N_META]
    return loss, dh0[N_META:N_META + seq], grads, handle


def _shard_pieces(arrs):
    return [arrs[n][0] if arrs[n].ndim == 3 else arrs[n] for n in SHARDED]


def _full_grad_blocks(grads):
    g = grads
    cols = lambda a, wd: _col_blocks(a, wd)
    rows = lambda a: a.reshape(N_DEV, a.shape[0] // N_DEV, a.shape[1])
    return [cols(g["w_in"], IN_WIDTH // N_DEV), cols(g["gdn_conv_w"], 3 * WIDTH // N_DEV),
            cols(g["w_branch_fox"], D_MODEL // N_DEV), cols(g["w_branch_gdn"], D_MODEL // N_DEV), rows(g["w_out"]),
            cols(g["ffn_w_up"], 2 * D_FF // N_DEV), cols(g["ffn_conv_w"], 2 * D_FF // N_DEV), rows(g["ffn_w_down"])]


def kernel(x, meta_tokens, w_in, fgt_bias, gdn_conv_w, gdn_a_log, gdn_dt_bias, gdn_norm_w, gate_bias, w_branch_fox, w_branch_gdn, w_out, norm_mix_w, norm_ffn_w, ffn_w_up, ffn_conv_w, ffn_conv_b, ffn_w_down, norm_final_w, loss_target, m_meta_tokens, m_w_in, m_fgt_bias, m_gdn_conv_w, m_gdn_a_log, m_gdn_dt_bias, m_gdn_norm_w, m_gate_bias, m_w_branch_fox, m_w_branch_gdn, m_w_out, m_norm_mix_w, m_norm_ffn_w, m_ffn_w_up, m_ffn_conv_w, m_ffn_conv_b, m_ffn_w_down, m_norm_final_w, v_meta_tokens, v_w_in, v_fgt_bias, v_gdn_conv_w, v_gdn_a_log, v_gdn_dt_bias, v_gdn_norm_w, v_gate_bias, v_w_branch_fox, v_w_branch_gdn, v_w_out, v_norm_mix_w, v_norm_ffn_w, v_ffn_w_up, v_ffn_conv_w, v_ffn_conv_b, v_ffn_w_down, v_norm_final_w):
    wts = dict(meta_tokens=meta_tokens, w_in=w_in, fgt_bias=fgt_bias, gdn_conv_w=gdn_conv_w, gdn_a_log=gdn_a_log,
               gdn_dt_bias=gdn_dt_bias, gdn_norm_w=gdn_norm_w, gate_bias=gate_bias, w_branch_fox=w_branch_fox,
               w_branch_gdn=w_branch_gdn, w_out=w_out, norm_mix_w=norm_mix_w, norm_ffn_w=norm_ffn_w,
               ffn_w_up=ffn_w_up, ffn_conv_w=ffn_conv_w, ffn_conv_b=ffn_conv_b, ffn_w_down=ffn_w_down,
               norm_final_w=norm_final_w)
    mom = dict(meta_tokens=m_meta_tokens, w_in=m_w_in, fgt_bias=m_fgt_bias, gdn_conv_w=m_gdn_conv_w,
               gdn_a_log=m_gdn_a_log, gdn_dt_bias=m_gdn_dt_bias, gdn_norm_w=m_gdn_norm_w, gate_bias=m_gate_bias,
               w_branch_fox=m_w_branch_fox, w_branch_gdn=m_w_branch_gdn, w_out=m_w_out, norm_mix_w=m_norm_mix_w,
               norm_ffn_w=m_norm_ffn_w, ffn_w_up=m_ffn_w_up, ffn_conv_w=m_ffn_conv_w, ffn_conv_b=m_ffn_conv_b,
               ffn_w_down=m_ffn_w_down, norm_final_w=m_norm_final_w)
    var = dict(meta_tokens=v_meta_tokens, w_in=v_w_in, fgt_bias=v_fgt_bias, gdn_conv_w=v_gdn_conv_w,
               gdn_a_log=v_gdn_a_log, gdn_dt_bias=v_gdn_dt_bias, gdn_norm_w=v_gdn_norm_w, gate_bias=v_gate_bias,
               w_branch_fox=v_w_branch_fox, w_branch_gdn=v_w_branch_gdn, w_out=v_w_out, norm_mix_w=v_norm_mix_w,
               norm_ffn_w=v_norm_ffn_w, ffn_w_up=v_ffn_w_up, ffn_conv_w=v_ffn_conv_w, ffn_conv_b=v_ffn_conv_b,
               ffn_w_down=v_ffn_w_down, norm_final_w=v_norm_final_w)

    sh = dict(zip(SHARDED, _shard_pieces(wts)))
    me = 4 * lax.axis_index("x") + 2 * lax.axis_index("y") + lax.axis_index("c")
    late_names = MATMUL[1:]
    late_sems_send, late_sems_recv, late_src, late_land, late_token = _gather_start(
        [sh[n].astype(BF16) for n in late_names])
    exact_shapes = [sh[n].shape for n in EXACT]
    rows_exact = _rows_for(sum(int(np.prod(s)) for s in exact_shapes), 8)
    g_in, g_exact = _all_gather([sh["w_in"].astype(BF16), _pad_flat([sh[n] for n in EXACT], rows_exact)])
    meta_full, conv_full, fconv_full = (_cat_cols(a) for a in _unpack_gathered(g_exact, exact_shapes))
    full = dict(
        meta_tokens=meta_full, w_in=_cat_cols(g_in), gdn_conv_w=conv_full, ffn_conv_w=fconv_full,
        fgt_bias=fgt_bias, gdn_a_log=gdn_a_log, gdn_dt_bias=gdn_dt_bias, gdn_norm_w=gdn_norm_w, gate_bias=gate_bias,
        norm_mix_w=norm_mix_w, norm_ffn_w=norm_ffn_w, ffn_conv_b=ffn_conv_b, norm_final_w=norm_final_w)

    def fetch_late_weights(after):
        shards, lands = _gather_wait(late_sems_send, late_sems_recv, late_src, late_land, after)
        g_bf, g_bg, g_out, g_up, g_down = (lax.dynamic_update_slice_in_dim(land, s[None], me, 0)
                                           for s, land in zip(shards, lands))
        return dict(w_branch_fox=_cat_cols(g_bf), w_branch_gdn=_cat_cols(g_bg), w_out=g_out.reshape(D_MODEL, D_MODEL),
                    ffn_w_up=_cat_cols(g_up), ffn_w_down=g_down.reshape(D_FF, D_MODEL))

    def start_exchange(grads_so_far):
        blocks = [b.astype(BF16) for b in _full_grad_blocks(grads_so_far)]
        send_sems, recv_sems, src_thru, land_thru, token = _exchange_start(blocks)
        return token, (send_sems, recv_sems, src_thru, land_thru)

    loss, grad_x, grads, handle = _local_step(x[0], loss_target[0], full, early=start_exchange,
                                              late_weights=(late_token, fetch_late_weights))
    sent, landed = _exchange_wait(*handle, after=grad_x)
    own = lambda src, land: lax.dynamic_update_slice_in_dim(land, lax.dynamic_slice_in_dim(src, me, 1, 0), me, 0)
    received = [own(src, land) for src, land in zip(sent, landed)]

    rep_parts = [grads[n] for n in REPLICATED] + [loss[:, :1]]
    rep_shapes = [wts[n].shape for n in REPLICATED]
    rows_small = _rows_for(sum(int(np.prod(p.shape)) for p in rep_parts), 8)
    meta_recv, small_recv = _grad_exchange([_col_blocks(grads["meta_tokens"], LANES).astype(BF16)],
                                           _pad_flat(rep_parts, rows_small))
    received = [meta_recv] + received + [small_recv]

    result = {}
    kinds = ("grad", "delta", "new_m", "new_v")
    for n, recv in zip(SHARDED, received[:-1]):
        outs = _adamw(sh[n], _shard_pieces(mom)[SHARDED.index(n)], _shard_pieces(var)[SHARDED.index(n)], recv,
                      "adamw_" + n)
        for kind, a in zip(kinds, outs):
            result[kind, n] = a.reshape(wts[n].shape)
    rep_w = _pad_flat([wts[n] for n in REPLICATED] + [jnp.zeros((1, 1), F32)], rows_small)
    rep_m = _pad_flat([mom[n] for n in REPLICATED] + [jnp.zeros((1, 1), F32)], rows_small)
    rep_v = _pad_flat([var[n] for n in REPLICATED] + [jnp.ones((1, 1), F32)], rows_small)
    outs_r = _adamw(rep_w, rep_m, rep_v, received[-1], "adamw_replicated")
    for kind, br in zip(kinds, outs_r):
        for n, a in zip(REPLICATED, _unpack(br, rep_shapes)):
            result[kind, n] = a
    n_rep = sum(int(np.prod(s)) for s in rep_shapes)
    total_loss = outs_r[0].reshape(-1)[n_rep]
    out = [total_loss, grad_x[None]]
    for kind in ("grad", "delta", "new_m", "new_v"):
        out += [result[kind, n] for n in WEIGHTS]
    return tuple(out)
```

```python
import functools

import jax
import jax.numpy as jnp
import numpy as np
from jax import lax
from jax.experimental import pallas as pl
from jax.experimental.pallas import tpu as pltpu

F32 = jnp.float32
BF16 = jnp.bfloat16

D_MODEL = 1024
N_META = 16
HEADS = 8
HEAD_DIM = 64
WIDTH = HEADS * HEAD_DIM
CHUNK = 64
GDN_CONV = 4
D_FF = 2816
FFN_CONV = 3
IN_WIDTH = 5656
IN_PAD = 5760
RMS_EPS = 1e-6
NEG = -1e30
AUG = 128
N_DEV = 8
LANES = 128

ADAM_LR = 0.001
ADAM_B1 = 0.9
ADAM_B2 = 0.999
ADAM_EPS = 1e-08
ADAM_WD = 0.01
ADAM_STEP = 10

VMEM_LIMIT = 56 * 1024 * 1024
MM_VMEM_BUDGET = 36 * 1024 * 1024
FFN_LANES = 128
HI = lax.Precision.HIGH
MESH_ID = pl.DeviceIdType.MESH


def _pick(n, cands):
    for c in cands:
        if n % c == 0:
            return c
    raise ValueError(f"no tile for {n} in {cands}")


def _params(*sem):
    return pltpu.CompilerParams(dimension_semantics=sem if sem else None, vmem_limit_bytes=VMEM_LIMIT)


def _padded_tokens(seq):
    t = -(-(N_META + seq) // 128) * 128
    if t > 1280 and t % 640:
        t = -(-t // 640) * 640
    return t


ROW_TILES = (640, 512, 384, 256, 128)


def _rmsnorm_fwd(h, gain):
    t, d = h.shape
    tr = _pick(t, ROW_TILES)

    def body(h_ref, g_ref, o_ref):
        x = h_ref[...]
        r = lax.rsqrt(jnp.mean(x * x, axis=-1, keepdims=True) + RMS_EPS)
        o_ref[...] = (x * r * g_ref[...]).astype(o_ref.dtype)

    return pl.pallas_call(
        body, grid=(t // tr,), name="rmsnorm_fwd",
        in_specs=[pl.BlockSpec((tr, d), lambda i: (i, 0)), pl.BlockSpec((1, d), lambda i: (0, 0))],
        out_specs=pl.BlockSpec((tr, d), lambda i: (i, 0)),
        out_shape=jax.ShapeDtypeStruct((t, d), BF16),
        compiler_params=_params("arbitrary"),
    )(h, gain)


def _rmsnorm_bwd(h, dy, gain, dres):
    t, d = h.shape
    tr = _pick(t, (320, 256, 128))

    def body(h_ref, dy_ref, g_ref, dres_ref, dh_ref, dhb_ref, dg_ref):
        x = h_ref[...]
        dyv = dy_ref[...]
        r = lax.rsqrt(jnp.mean(x * x, axis=-1, keepdims=True) + RMS_EPS)
        gy = dyv * g_ref[...]
        m = jnp.mean(gy * x, axis=-1, keepdims=True)
        dh = dres_ref[...] + r * gy - x * (r * r * r * m)
        dh_ref[...] = dh
        dhb_ref[...] = dh.astype(BF16)

        @pl.when(pl.program_id(0) == 0)
        def _():
            dg_ref[...] = jnp.zeros_like(dg_ref)

        dg_ref[...] += jnp.sum(dyv * x * r, axis=0, keepdims=True)

    row = pl.BlockSpec((tr, d), lambda i: (i, 0))
    vec = pl.BlockSpec((1, d), lambda i: (0, 0))
    return pl.pallas_call(
        body, grid=(t // tr,), name="rmsnorm_bwd",
        in_specs=[row, row, vec, row], out_specs=[row, row, vec],
        out_shape=[jax.ShapeDtypeStruct((t, d), F32), jax.ShapeDtypeStruct((t, d), BF16),
                   jax.ShapeDtypeStruct((1, d), F32)],
        compiler_params=_params("arbitrary"),
    )(h, dy, gain, dres)


def _mm(a, b, out_dtype, name, res=None):
    m, k = a.shape
    _, n = b.shape
    tm = _pick(m, ROW_TILES)
    out_bytes = jnp.dtype(out_dtype).itemsize + (4 if res is not None else 0)
    fits = lambda tn: 4 * tm * k + 4 * k * tn + 2 * tm * tn * out_bytes <= MM_VMEM_BUDGET
    tn = next(c for c in (n, 2816, 2048, 1536, 1408, 1024, 512, 384, 256, 128) if n % c == 0 and fits(c))

    def body(*refs):
        if res is None:
            a_ref, b_ref, o_ref = refs
        else:
            a_ref, b_ref, r_ref, o_ref = refs
        out = jnp.dot(a_ref[...], b_ref[...], preferred_element_type=F32)
        if res is not None:
            out = out + r_ref[...]
        o_ref[...] = out.astype(o_ref.dtype)

    in_specs = [pl.BlockSpec((tm, k), lambda i, j: (i, 0)), pl.BlockSpec((k, tn), lambda i, j: (0, j))]
    args = [a, b]
    if res is not None:
        in_specs.append(pl.BlockSpec((tm, tn), lambda i, j: (i, j)))
        args.append(res)
    return pl.pallas_call(
        body, grid=(m // tm, n // tn), name=name,
        in_specs=in_specs, out_specs=pl.BlockSpec((tm, tn), lambda i, j: (i, j)),
        out_shape=jax.ShapeDtypeStruct((m, n), out_dtype),
        compiler_params=_params("parallel", "parallel"),
    )(*args)


def _mm_tn(a, g, name):
    t, k = a.shape
    _, n = g.shape
    tk = _pick(k, (1024, 1408, 512))
    tn = _pick(n, (512, 640, 384, 256, 128))
    tt = next(c for c in (3328, 1280) + ROW_TILES
              if t % c == 0 and 4 * c * (tk + tn) + 8 * tk * tn <= MM_VMEM_BUDGET)
    nt = t // tt

    def body(a_ref, g_ref, o_ref):
        @pl.when(pl.program_id(2) == 0)
        def _():
            o_ref[...] = jnp.zeros_like(o_ref)

        o_ref[...] += lax.dot_general(a_ref[...], g_ref[...], (((0,), (0,)), ((), ())),
                                      preferred_element_type=F32)

    return pl.pallas_call(
        body, grid=(k // tk, n // tn, nt), name=name,
        in_specs=[pl.BlockSpec((tt, tk), lambda i, j, s: (s, i)), pl.BlockSpec((tt, tn), lambda i, j, s: (s, j))],
        out_specs=pl.BlockSpec((tk, tn), lambda i, j, s: (i, j)),
        out_shape=jax.ShapeDtypeStruct((k, n), F32),
        compiler_params=_params("parallel", "parallel", "arbitrary"),
    )(a, g)


def _split3_exact(x):
    def top(v):
        return lax.bitcast_convert_type(lax.bitcast_convert_type(v, jnp.int32) & jnp.int32(-65536), F32)

    hi = top(x)
    r1 = x - hi
    mid = top(r1)
    return hi, mid, r1 - mid


def _pair_head(ref, h, rows):
    x = ref[:, 128 * (h // 2):128 * (h // 2) + 128].astype(F32)
    return pltpu.roll(x, HEAD_DIM, axis=1) if h % 2 else x


def _lanes(rows):
    return lax.broadcasted_iota(jnp.int32, (rows, AUG), 1)


def _fox_prep(fq, scal):
    t = fq.shape[0]
    tt = _pick(t, (256, 128))

    def body(q_ref, k_ref, v_ref, s_ref, qa_ref, ka_ref, va_ref, kt_ref, vt_ref):
        lane = _lanes(tt)
        chi, cmid, clo = _split3_exact(s_ref[...])
        ones = lambda lo: jnp.where((lane >= lo) & (lane < lo + 3), 1.0, 0.0)
        for h in range(HEADS):
            col = lambda a: jnp.broadcast_to(a[:, h:h + 1], (tt, AUG))
            c1, c2, c3 = col(chi), col(cmid), col(clo)
            qx = jnp.where(lane == 64, c1, jnp.where(lane == 65, c2, jnp.where(lane == 66, c3, ones(67))))
            kx = jnp.where(lane == 67, -c1, jnp.where(lane == 68, -c2, jnp.where(lane == 69, -c3, ones(64) + ones(70))))
            qa_ref[h] = jnp.where(lane < HEAD_DIM, _pair_head(q_ref, h, tt) * (HEAD_DIM ** -0.5), qx).astype(BF16)
            k_aug = jnp.where(lane < HEAD_DIM, _pair_head(k_ref, h, tt), kx)
            ka_ref[h] = k_aug.astype(BF16)
            kt_ref[h] = k_aug.T.astype(BF16)
            v_aug = jnp.where(lane < HEAD_DIM, _pair_head(v_ref, h, tt), ones(64))
            va_ref[h] = v_aug.astype(BF16)
            vt_ref[h] = v_aug.T.astype(BF16)

    out = pl.BlockSpec((HEADS, tt, AUG), lambda i: (0, i, 0))
    out_t = pl.BlockSpec((HEADS, AUG, tt), lambda i: (0, 0, i))
    shp = jax.ShapeDtypeStruct((HEADS, t, AUG), BF16)
    shp_t = jax.ShapeDtypeStruct((HEADS, AUG, t), BF16)
    return pl.pallas_call(
        body, grid=(t // tt,), name="fox_prep",
        in_specs=[pl.BlockSpec((tt, WIDTH), lambda i: (i, 0)), pl.BlockSpec((tt, WIDTH), lambda i: (i, 1)),
                  pl.BlockSpec((tt, WIDTH), lambda i: (i, 2)), pl.BlockSpec((tt, LANES), lambda i: (i, 0))],
        out_specs=[out, out, out, out_t, out_t], out_shape=[shp, shp, shp, shp_t, shp_t],
        compiler_params=_params("parallel"),
    )(fq, fq, fq, scal)


def _fox_post(oa):
    t = oa.shape[1]
    tt = _pick(t, (256, 128))

    def body(o_ref, out_ref):
        out_ref[...] = jnp.concatenate([o_ref[h][:, :HEAD_DIM] for h in range(HEADS)], axis=1).astype(BF16)

    return pl.pallas_call(
        body, grid=(t // tt,), name="fox_post",
        in_specs=[pl.BlockSpec((HEADS, tt, AUG), lambda i: (0, i, 0))],
        out_specs=pl.BlockSpec((tt, WIDTH), lambda i: (i, 0)),
        out_shape=jax.ShapeDtypeStruct((t, WIDTH), BF16),
        compiler_params=_params("parallel"),
    )(oa)


def _fox_bwd_prep(do, oa):
    t = do.shape[0]
    tt = _pick(t, (256, 128))

    def body(d_ref, o_ref, out_ref, outt_ref):
        lane = _lanes(tt)
        for h in range(HEADS):
            x = _pair_head(d_ref, h, tt)
            delta = jnp.sum(jnp.where(lane < HEAD_DIM, x * o_ref[h], 0.0), axis=1, keepdims=True)
            hi, mid, lo = _split3_exact(jnp.broadcast_to(-delta, (tt, AUG)))
            ex = jnp.where(lane == 64, hi, jnp.where(lane == 65, mid, jnp.where(lane == 66, lo, 0.0)))
            do_aug = jnp.where(lane < HEAD_DIM, x, ex)
            out_ref[h] = do_aug.astype(BF16)
            outt_ref[h] = do_aug.T.astype(BF16)

    hm = pl.BlockSpec((HEADS, tt, AUG), lambda i: (0, i, 0))
    return pl.pallas_call(
        body, grid=(t // tt,), name="fox_bwd_prep",
        in_specs=[pl.BlockSpec((tt, WIDTH), lambda i: (i, 0)), hm],
        out_specs=[hm, pl.BlockSpec((HEADS, AUG, tt), lambda i: (0, 0, i))],
        out_shape=[jax.ShapeDtypeStruct((HEADS, t, AUG), BF16), jax.ShapeDtypeStruct((HEADS, AUG, t), BF16)],
        compiler_params=_params("parallel"),
    )(do, oa)


def _fox_bwd_post(dqt, dkt, dvt):
    t = dqt.shape[2]
    tt = _pick(t, (256, 128))

    def body(dq_ref, dk_ref, dv_ref, out_ref, dsc_ref):
        lane = _lanes(tt)
        dqs = [dq_ref[h].T for h in range(HEADS)]
        dks = [dk_ref[h].T for h in range(HEADS)]
        heads = lambda xs: jnp.concatenate([x[:, :HEAD_DIM] for x in xs], axis=1)
        out_ref[:, 0:WIDTH] = (heads(dqs) * (HEAD_DIM ** -0.5)).astype(BF16)
        out_ref[:, WIDTH:2 * WIDTH] = heads(dks).astype(BF16)
        out_ref[:, 2 * WIDTH:] = heads([dv_ref[h].T for h in range(HEADS)]).astype(BF16)
        dsc = jnp.zeros((tt, LANES), F32)
        for h in range(HEADS):
            both = jnp.where(lane == HEAD_DIM, dqs[h], 0.0) - jnp.where(lane == HEAD_DIM + 3, dks[h], 0.0)
            dsc = jnp.where(lane == h, jnp.sum(both, axis=1, keepdims=True), dsc)
        dsc_ref[...] = dsc

    hm = pl.BlockSpec((HEADS, AUG, tt), lambda i: (0, 0, i))
    return pl.pallas_call(
        body, grid=(t // tt,), name="fox_bwd_post",
        in_specs=[hm, hm, hm],
        out_specs=[pl.BlockSpec((tt, 3 * WIDTH), lambda i: (i, 0)), pl.BlockSpec((tt, LANES), lambda i: (i, 0))],
        out_shape=[jax.ShapeDtypeStruct((t, 3 * WIDTH), BF16), jax.ShapeDtypeStruct((t, LANES), F32)],
        compiler_params=_params("parallel"),
    )(dqt, dkt, dvt)


def _fox_fwd(qa, ka, vat, tq=None):
    h, t, _ = qa.shape
    tq = tq or _pick(t, ROW_TILES)

    def body(q_ref, k_ref, vt_ref, o_ref, qb_ref, qbt_ref, s_ref):
        i = pl.program_id(1)
        q = q_ref[...]
        krow = lax.broadcasted_iota(jnp.int32, (tq, tq), 0)
        qcol = lax.broadcasted_iota(jnp.int32, (tq, tq), 1)
        rows = lambda j: pl.ds(pl.multiple_of(j * tq, tq), tq)

        def scores(j, slot):
            s_ref[slot] = lax.dot_general(k_ref[rows(j), :], q, (((1,), (1,)), ((), ())), preferred_element_type=F32)

        def update(j, slot, carry, masked):
            m, acc = carry
            s = s_ref[slot]
            if masked:
                s = jnp.where(qcol >= krow, s, NEG)
            m_new = jnp.maximum(m, jnp.max(s, axis=0, keepdims=True))
            p = jnp.exp(s - m_new)
            alpha = jnp.exp(m - m_new)
            return m_new, acc * alpha + jnp.dot(vt_ref[:, rows(j)], p.astype(BF16), preferred_element_type=F32)

        def pair(j, carry):
            scores(j + 1, 1)
            carry = update(j, 0, carry, False)
            scores(j + 2, 0)
            return update(j + 1, 1, carry, False)

        def odd_tail(carry):
            scores(i, 1)
            return update(i, 1, update(i - 1, 0, carry, False), True)

        scores(0, 0)
        carry = (jnp.full((1, tq), NEG, F32), jnp.zeros((AUG, tq), F32))
        carry = lax.fori_loop(0, i // 4, lambda jj, c: pair(4 * jj + 2, pair(4 * jj, c)), carry)
        carry = lax.fori_loop(0, (i % 4) // 2, lambda jj, c: pair(4 * (i // 4), c), carry)
        m, acc = lax.cond(i % 2 == 1, odd_tail, lambda c: update(i, 0, c, True), carry)
        sub = lax.broadcasted_iota(jnp.int32, (AUG, tq), 0)
        l = jnp.sum(jnp.where(sub == HEAD_DIM, acc, 0.0), axis=0, keepdims=True)
        out = jnp.where(sub < HEAD_DIM, acc / l, m + jnp.log(l)).T
        o_ref[...] = out
        lane = lax.broadcasted_iota(jnp.int32, (tq, AUG), 1)
        lse = jnp.broadcast_to(jnp.sum(jnp.where(lane == HEAD_DIM, out, 0.0), axis=1, keepdims=True), (tq, AUG))
        hi, mid, lo = _split3_exact(-lse)
        qb = jnp.where(lane == 70, hi, jnp.where(lane == 71, mid, jnp.where(lane == 72, lo, q.astype(F32))))
        qb_ref[...] = qb.astype(BF16)
        qbt_ref[...] = qb.T.astype(BF16)

    blk = pl.BlockSpec((None, tq, AUG), lambda hh, i: (hh, i, 0))
    return pl.pallas_call(
        body, grid=(h, t // tq), name="fox_fwd",
        in_specs=[blk, pl.BlockSpec((None, t, AUG), lambda hh, i: (hh, 0, 0)),
                  pl.BlockSpec((None, AUG, t), lambda hh, i: (hh, 0, 0))],
        out_specs=[blk, blk, pl.BlockSpec((None, AUG, tq), lambda hh, i: (hh, 0, i))],
        out_shape=[jax.ShapeDtypeStruct((h, t, AUG), F32), jax.ShapeDtypeStruct((h, t, AUG), BF16),
                   jax.ShapeDtypeStruct((h, AUG, t), BF16)],
        scratch_shapes=[pltpu.VMEM((2, tq, tq), F32)],
        compiler_params=_params("parallel", "arbitrary"),
    )(qa, ka, vat)


def _fox_bwd(qb, qbt, ka, kat, va, doa, doat, tq=None):
    h, t, _ = qb.shape
    tq = tq or _pick(t, ROW_TILES)
    nq = t // tq

    def body(q_ref, qt_ref, k_ref, kt_ref, v_ref, do_ref, dot_ref, dqt_ref, dkt_ref, dvt_ref, s_ref, dp_ref):
        j = pl.program_id(1)
        n = nq - j

        @pl.when(j == 0)
        def _():
            dqt_ref[...] = jnp.zeros_like(dqt_ref)

        dkt_ref[...] = jnp.zeros_like(dkt_ref)
        dvt_ref[...] = jnp.zeros_like(dvt_ref)
        kj = k_ref[...]
        ktj = kt_ref[...]
        vj = v_ref[...]
        qrow = lax.broadcasted_iota(jnp.int32, (tq, tq), 0)
        kcol = lax.broadcasted_iota(jnp.int32, (tq, tq), 1)
        rows = lambda i: pl.ds(pl.multiple_of(i * tq, tq), tq)
        nt_dims = (((1,), (1,)), ((), ()))

        def scores(i, slot):
            s_ref[slot] = lax.dot_general(q_ref[rows(i), :], kj, nt_dims, preferred_element_type=F32)
            dp_ref[slot] = lax.dot_general(do_ref[rows(i), :], vj, nt_dims, preferred_element_type=F32)

        def update(i, slot):
            p = jnp.exp(jnp.where((qrow >= kcol) | (i > j), s_ref[slot], NEG))
            ds = (p * dp_ref[slot]).astype(BF16)
            dvt_ref[...] += jnp.dot(dot_ref[:, rows(i)], p.astype(BF16), preferred_element_type=F32)
            dkt_ref[...] += jnp.dot(qt_ref[:, rows(i)], ds, preferred_element_type=F32)
            dqt_ref[:, rows(i)] += lax.dot_general(ktj, ds, nt_dims, preferred_element_type=F32)

        def pair(i0):
            scores(i0 + 1, 1)
            update(i0, 0)
            scores(jnp.minimum(i0 + 2, nq - 1), 0)
            update(i0 + 1, 1)

        def quad(kk, carry):
            pair(j + 4 * kk)
            pair(j + 4 * kk + 2)
            return carry

        def last_pair(kk, carry):
            pair(j + 4 * (n // 4))
            return carry

        scores(j, 0)
        lax.fori_loop(0, n // 4, quad, 0)
        lax.fori_loop(0, (n % 4) // 2, last_pair, 0)

        @pl.when(n % 2 == 1)
        def _():
            update(nq - 1, 0)

    once = pl.Buffered(1)
    full = pl.BlockSpec((None, t, AUG), lambda hh, j: (hh, 0, 0), pipeline_mode=once)
    full_t = pl.BlockSpec((None, AUG, t), lambda hh, j: (hh, 0, 0), pipeline_mode=once)
    blk = pl.BlockSpec((None, tq, AUG), lambda hh, j: (hh, j, 0))
    blk_t = pl.BlockSpec((None, AUG, tq), lambda hh, j: (hh, 0, j))
    shp = jax.ShapeDtypeStruct((h, AUG, t), F32)
    return pl.pallas_call(
        body, grid=(h, nq), name="fox_bwd",
        in_specs=[full, full_t, blk, blk_t, blk, full, full_t],
        out_specs=[pl.BlockSpec((None, AUG, t), lambda hh, j: (hh, 0, 0)), blk_t, blk_t], out_shape=[shp, shp, shp],
        scratch_shapes=[pltpu.VMEM((2, tq, tq), F32), pltpu.VMEM((2, tq, tq), F32)],
        compiler_params=_params("parallel", "arbitrary"),
    )(qb, qbt, ka, kat, va, doa, doat)


def _seg_matrix():
    idx = np.arange(WIDTH) // HEAD_DIM
    return jnp.asarray((idx[:, None] == idx[None, :]).astype(np.float32))


def _segsum(x, e):
    return jnp.dot(x, e, precision=HI, preferred_element_type=F32)


def _silu(x):
    return x * jax.nn.sigmoid(x)


def _silu_grad(x):
    s = jax.nn.sigmoid(x)
    return s * (1.0 + x * (1.0 - s))


def _shift_down(x, prev8, k):
    r = pltpu.roll(x, k, axis=0)
    p = pltpu.roll(prev8, k, axis=0)
    row = lax.broadcasted_iota(jnp.int32, prev8.shape, 0)
    head = jnp.where(row < k, p, r[:8])
    return jnp.concatenate([head, r[8:]], axis=0)


def _shift_up(x, next8, k):
    n = x.shape[0]
    r = pltpu.roll(x, n - k, axis=0)
    p = pltpu.roll(next8, 8 - k, axis=0)
    row = lax.broadcasted_iota(jnp.int32, next8.shape, 0)
    tail = jnp.where(row >= 8 - k, p, r[n - 8:])
    return jnp.concatenate([r[:n - 8], tail], axis=0)


def _causal_conv(x, prev8, w_ref, width, cols=slice(None)):
    y = x * w_ref[width - 1:width, cols]
    for k in range(1, width):
        y = y + _shift_down(x, prev8, k) * w_ref[width - 1 - k:width - k, cols]
    return y


def _causal_conv_bwd(x, prev8, dy, dnext8, w_ref, dw_ref, width, cols=slice(None)):
    dx = dy * w_ref[width - 1:width, cols]
    dw_ref[width - 1:width, cols] += jnp.sum(dy * x, axis=0, keepdims=True)
    for k in range(1, width):
        dx = dx + _shift_up(dy, dnext8, k) * w_ref[width - 1 - k:width - k, cols]
        dw_ref[width - 1 - k:width - k, cols] += jnp.sum(dy * _shift_down(x, prev8, k), axis=0, keepdims=True)
    return dx


HALO = 16


def _prev_spec(tt, width, tile=lambda i: i):
    return pl.BlockSpec((HALO, width), lambda i: (jnp.maximum(tile(i) * (tt // HALO) - 1, 0), 0))


def _prev8(p_ref, cols=slice(None)):
    return p_ref[:, cols].astype(F32)[HALO - 8:]


def _store_heads(ref, x):
    for h in range(HEADS):
        ref[h] = x[:, HEAD_DIM * h:HEAD_DIM * (h + 1)]


def _load_heads(ref):
    return jnp.concatenate([ref[h] for h in range(HEADS)], axis=1)


def _softplus(z):
    return jnp.maximum(z, 0.0) + jnp.log1p(jnp.exp(-jnp.abs(z)))


def _tri_masks(tt):
    r = lax.broadcasted_iota(jnp.int32, (tt, tt), 0)
    c = lax.broadcasted_iota(jnp.int32, (tt, tt), 1)
    same_chunk = lax.shift_right_logical(r, 6) == lax.shift_right_logical(c, 6)
    return r, c, same_chunk


def _gate_fwd(small, pbias, pscale):
    t = small.shape[0]
    tt = _pick(t, (256, 128))

    def body(x_ref, pb_ref, ps_ref, o_ref, carry_ref):
        @pl.when(pl.program_id(0) == 0)
        def _():
            carry_ref[...] = jnp.zeros_like(carry_ref)

        lane = lax.broadcasted_iota(jnp.int32, (tt, LANES), 1)
        z = x_ref[...] + pb_ref[...]
        log_f = jnp.where(lane < HEADS, -_softplus(-z), 0.0)
        g = jnp.where((lane >= 2 * HEADS) & (lane < 3 * HEADS), ps_ref[...] * _softplus(z), 0.0)
        r, c, same_chunk = _tri_masks(tt)
        lower = jnp.where(r >= c, 1.0, 0.0)
        lower_chunk = jnp.where((r >= c) & same_chunk, 1.0, 0.0)
        csum = jnp.dot(lower, log_f, precision=lax.Precision.HIGHEST, preferred_element_type=F32) + carry_ref[...]
        gc = jnp.dot(lower_chunk, g, precision=lax.Precision.HIGHEST, preferred_element_type=F32)
        carry_ref[...] += jnp.sum(log_f, axis=0, keepdims=True)
        o_ref[...] = jnp.where(lane < HEADS, csum, jnp.where(lane < 2 * HEADS, jax.nn.sigmoid(z), gc))

    row = pl.BlockSpec((tt, LANES), lambda i: (i, 0))
    vec = pl.BlockSpec((1, LANES), lambda i: (0, 0))
    return pl.pallas_call(
        body, grid=(t // tt,), name="gate_fwd", in_specs=[row, vec, vec], out_specs=row,
        out_shape=jax.ShapeDtypeStruct((t, LANES), F32),
        scratch_shapes=[pltpu.VMEM((1, LANES), F32)],
        compiler_params=_params("arbitrary"),
    )(small, pbias, pscale)


def _gate_bwd(small, pbias, pscale, dscal):
    t = small.shape[0]
    tt = _pick(t, (256, 128))
    nt = t // tt

    def body(x_ref, pb_ref, ps_ref, d_ref, dx_ref, dpb_ref, dps_ref, carry_ref):
        @pl.when(pl.program_id(0) == 0)
        def _():
            carry_ref[...] = jnp.zeros_like(carry_ref)
            dpb_ref[...] = jnp.zeros_like(dpb_ref)
            dps_ref[...] = jnp.zeros_like(dps_ref)

        lane = lax.broadcasted_iota(jnp.int32, (tt, LANES), 1)
        z = x_ref[...] + pb_ref[...]
        d = d_ref[...]
        dc = jnp.where(lane < HEADS, d, 0.0)
        dbeta = jnp.where((lane >= HEADS) & (lane < 2 * HEADS), d, 0.0)
        dgc = jnp.where((lane >= 2 * HEADS) & (lane < 3 * HEADS), d, 0.0)
        r, c, same_chunk = _tri_masks(tt)
        upper = jnp.where(r <= c, 1.0, 0.0)
        upper_chunk = jnp.where((r <= c) & same_chunk, 1.0, 0.0)
        dlogf = jnp.dot(upper, dc, precision=lax.Precision.HIGHEST, preferred_element_type=F32) + carry_ref[...]
        dg = jnp.dot(upper_chunk, dgc, precision=lax.Precision.HIGHEST, preferred_element_type=F32)
        carry_ref[...] += jnp.sum(dc, axis=0, keepdims=True)
        sg = jax.nn.sigmoid(z)
        dz = dlogf * (1.0 - sg) + dbeta * sg * (1.0 - sg) + dg * ps_ref[...] * sg
        dx_ref[...] = dz.astype(dx_ref.dtype)
        dpb_ref[...] += jnp.sum(dz, axis=0, keepdims=True)
        dps_ref[...] += jnp.sum(dg * _softplus(z), axis=0, keepdims=True)

    row = pl.BlockSpec((tt, LANES), lambda i: (nt - 1 - i, 0))
    vec = pl.BlockSpec((1, LANES), lambda i: (0, 0))
    return pl.pallas_call(
        body, grid=(nt,), name="gate_bwd", in_specs=[row, vec, vec, row], out_specs=[row, vec, vec],
        out_shape=[jax.ShapeDtypeStruct((t, LANES), BF16), jax.ShapeDtypeStruct((1, LANES), F32),
                   jax.ShapeDtypeStruct((1, LANES), F32)],
        scratch_shapes=[pltpu.VMEM((1, LANES), F32)],
        compiler_params=_params("arbitrary"),
    )(small, pbias, pscale, dscal)


def _gdn_pre_fwd(xg, conv_w, seg):
    t = xg.shape[0]
    c3 = 3 * WIDTH
    tt = _pick(t, (320, 256, 128))

    def body(x_ref, p_ref, w_ref, e_ref, q_ref, k_ref, v_ref):
        x = x_ref[...].astype(F32)
        prev = jnp.where(pl.program_id(0) == 0, 0.0, _prev8(p_ref))
        s = _silu(_causal_conv(x, prev, w_ref, GDN_CONV))
        e = e_ref[...]
        q = s[:, :WIDTH]
        k = s[:, WIDTH:2 * WIDTH]
        _store_heads(q_ref, q * lax.rsqrt(_segsum(q * q, e) + RMS_EPS) * (HEAD_DIM ** -0.5))
        _store_heads(k_ref, k * lax.rsqrt(_segsum(k * k, e) + RMS_EPS))
        _store_heads(v_ref, s[:, 2 * WIDTH:])

    out = pl.BlockSpec((HEADS, tt, HEAD_DIM), lambda i: (0, i, 0))
    shp = jax.ShapeDtypeStruct((HEADS, t, HEAD_DIM), F32)
    return pl.pallas_call(
        body, grid=(t // tt,), name="gdn_pre_fwd",
        in_specs=[pl.BlockSpec((tt, c3), lambda i: (i, 0)), _prev_spec(tt, c3),
                  pl.BlockSpec((GDN_CONV, c3), lambda i: (0, 0)), pl.BlockSpec((WIDTH, WIDTH), lambda i: (0, 0))],
        out_specs=[out, out, out], out_shape=[shp, shp, shp],
        compiler_params=_params("arbitrary"),
    )(xg, xg, conv_w, seg)


def _gdn_pre_bwd(xg, conv_w, seg, dqn, dkn, dv):
    t = xg.shape[0]
    c3 = 3 * WIDTH
    tt = _pick(t, (320, 256, 128))
    nt = t // tt

    def body(x_ref, p_ref, w_ref, e_ref, dq_ref, dk_ref, dv_ref, dx_ref, dw_ref, carry_ref):
        step = pl.program_id(0)
        x = x_ref[...].astype(F32)
        e = e_ref[...]
        prev = jnp.where(step == nt - 1, 0.0, _prev8(p_ref))
        y = _causal_conv(x, prev, w_ref, GDN_CONV)
        s = _silu(y)
        q = s[:, :WIDTH]
        k = s[:, WIDTH:2 * WIDTH]
        rq = lax.rsqrt(_segsum(q * q, e) + RMS_EPS)
        rk = lax.rsqrt(_segsum(k * k, e) + RMS_EPS)
        gq = _load_heads(dq_ref) * (HEAD_DIM ** -0.5)
        gk = _load_heads(dk_ref)
        dq = rq * gq - q * (rq * rq * rq) * _segsum(gq * q, e)
        dk = rk * gk - k * (rk * rk * rk) * _segsum(gk * k, e)
        dy = jnp.concatenate([dq, dk, _load_heads(dv_ref)], axis=1) * _silu_grad(y)

        @pl.when(step == 0)
        def _():
            carry_ref[...] = jnp.zeros_like(carry_ref)
            dw_ref[...] = jnp.zeros_like(dw_ref)

        dx = _causal_conv_bwd(x, prev, dy, carry_ref[...], w_ref, dw_ref, GDN_CONV)
        dx_ref[...] = dx.astype(dx_ref.dtype)
        carry_ref[...] = dy[:8]

    rev = lambda i: (nt - 1 - i, 0)
    blk = pl.BlockSpec((HEADS, tt, HEAD_DIM), lambda i: (0, nt - 1 - i, 0))
    return pl.pallas_call(
        body, grid=(nt,), name="gdn_pre_bwd",
        in_specs=[pl.BlockSpec((tt, c3), rev), _prev_spec(tt, c3, lambda i: nt - 1 - i),
                  pl.BlockSpec((GDN_CONV, c3), lambda i: (0, 0)), pl.BlockSpec((WIDTH, WIDTH), lambda i: (0, 0)),
                  blk, blk, blk],
        out_specs=[pl.BlockSpec((tt, c3), rev), pl.BlockSpec((GDN_CONV, c3), lambda i: (0, 0))],
        out_shape=[jax.ShapeDtypeStruct((t, c3), BF16), jax.ShapeDtypeStruct((GDN_CONV, c3), F32)],
        scratch_shapes=[pltpu.VMEM((8, c3), F32)],
        compiler_params=_params("arbitrary"),
    )(xg, xg, conv_w, seg, dqn, dkn, dv)


def _bmm(a, b, ca, cb, precision=None):
    return lax.dot_general(a, b, (((ca,), (cb,)), ((0,), (0,))), precision=precision, preferred_element_type=F32)


def _bf(x):
    return x.astype(BF16)


def _tri_inverse(a, eye):
    x = -a
    tinv = eye + x
    pw = x
    for _ in range(5):
        pb = _bf(pw)
        pw = _bmm(pb, pb, 2, 1)
        tinv = tinv + _bmm(_bf(tinv), _bf(pw), 2, 1)
    resid = eye - _bmm(eye + a, tinv, 2, 1, precision=HI)
    return tinv + _bmm(_bf(tinv), _bf(resid), 2, 1)


def _gdn_intra(q, k, v, bc, gcc, gcr):
    ii = lax.broadcasted_iota(jnp.int32, (CHUNK, CHUNK), 0)
    jj = lax.broadcasted_iota(jnp.int32, (CHUNK, CHUNK), 1)
    tril = (ii >= jj)[None]
    strict = (ii > jj)[None]
    eye = jnp.where(ii == jj, 1.0, 0.0).astype(F32)[None]
    last = (ii == CHUNK - 1)[None]
    dm = jnp.exp(jnp.where(tril, gcc - gcr, NEG))
    gam = jnp.exp(gcc)
    kb = k * bc
    vb = v * bc
    kk = _bmm(_bf(kb), _bf(k), 2, 2)
    a = jnp.where(strict, kk * dm, 0.0)
    tinv = _tri_inverse(a, eye)
    kbg = kb * gam
    u = _bmm(tinv, vb, 2, 1, precision=HI)
    wk = _bmm(tinv, kbg, 2, 1, precision=HI)
    qk = _bmm(_bf(q), _bf(k), 2, 2)
    p = jnp.where(tril, qk * dm, 0.0)
    gl = jnp.sum(jnp.where(last, gcc, 0.0), axis=1, keepdims=True)
    edec = jnp.exp(gl - gcc)
    return dict(tril=tril, strict=strict, dm=dm, gam=gam, kb=kb, kk=kk, a=a, tinv=tinv, u=u, wk=wk, qk=qk, p=p,
                qg=q * gam, kt=k * edec, edec=edec, gaml=jnp.exp(gl), last=last)


def _gate_tiles(sc, gct, nb):
    rows = nb * CHUNK
    cols = lambda lane0: jnp.stack([jnp.broadcast_to(sc[:, lane0 + h:lane0 + h + 1], (rows, HEAD_DIM))
                                    for h in range(HEADS)], axis=0).reshape(HEADS * nb, CHUNK, HEAD_DIM)
    gcr = jnp.stack([jnp.broadcast_to(gct[h:h + 1, n * CHUNK:(n + 1) * CHUNK], (CHUNK, CHUNK))
                     for h in range(HEADS) for n in range(nb)], axis=0)
    return cols(HEADS), cols(2 * HEADS), gcr


def _gdn_fwd(q, k, v, scal, gct, nb=None):
    h, t, dh = q.shape
    nc = t // CHUNK
    nb = nb or _pick(nc, (4, 2))
    bsz = h * nb

    def body(q_ref, k_ref, v_ref, sc_ref, gt_ref, o_ref, s0_ref, state_ref):
        @pl.when(pl.program_id(0) == 0)
        def _():
            state_ref[...] = jnp.zeros_like(state_ref)

        ld = lambda r: r[...].reshape(bsz, CHUNK, dh)
        bc, gcc, gcr = _gate_tiles(sc_ref[...], gt_ref[...], nb)
        z = _gdn_intra(ld(q_ref), ld(k_ref), ld(v_ref), bc, gcc, gcr)
        per = lambda x: x.reshape((h, nb) + x.shape[1:])
        u, wk, p, qg, kt, gaml = (per(z[n]) for n in ("u", "wk", "p", "qg", "kt", "gaml"))
        s = state_ref[...]
        for n in range(nb):
            s0_ref[:, n] = s
            sb = _bf(s)
            vn = u[:, n] - _bmm(_bf(wk[:, n]), sb, 2, 1)
            o_ref[:, n * CHUNK:(n + 1) * CHUNK, :] = _bmm(_bf(qg[:, n]), sb, 2, 1) + _bmm(_bf(p[:, n]), _bf(vn), 2, 1)
            s = s * gaml[:, n] + _bmm(_bf(kt[:, n]), _bf(vn), 1, 1)
        state_ref[...] = s

    blk = pl.BlockSpec((h, nb * CHUNK, dh), lambda i: (0, i, 0))
    return pl.pallas_call(
        body, grid=(nc // nb,), name="gdn_fwd",
        in_specs=[blk] * 3 + [pl.BlockSpec((nb * CHUNK, LANES), lambda i: (i, 0)),
                              pl.BlockSpec((h, nb * CHUNK), lambda i: (0, i))],
        out_specs=[blk, pl.BlockSpec((h, nb, dh, dh), lambda i: (0, i, 0, 0))],
        out_shape=[jax.ShapeDtypeStruct((h, t, dh), F32), jax.ShapeDtypeStruct((h, nc, dh, dh), F32)],
        scratch_shapes=[pltpu.VMEM((h, dh, dh), F32)],
        compiler_params=_params("arbitrary"),
    )(q, k, v, scal, gct)


def _gdn_bwd(q, k, v, scal, gct, s0s, do, nb=None):
    h, t, dh = q.shape
    nc = t // CHUNK
    nb = nb or _pick(nc, (2,))
    bsz = h * nb
    ng = nc // nb
    rows = nb * CHUNK

    def body(q_ref, k_ref, v_ref, sc_ref, gt_ref, s0_ref, do_ref,
             dq_ref, dk_ref, dv_ref, dsc_ref, dgt_ref, ds_ref):
        @pl.when(pl.program_id(0) == 0)
        def _():
            ds_ref[...] = jnp.zeros_like(ds_ref)

        ld = lambda r: r[...].reshape(bsz, CHUNK, dh)
        q, k, v = ld(q_ref), ld(k_ref), ld(v_ref)
        bc, gcc, gcr = _gate_tiles(sc_ref[...], gt_ref[...], nb)
        z = _gdn_intra(q, k, v, bc, gcc, gcr)
        per = lambda x: x.reshape((h, nb) + x.shape[1:])
        u, wk, p, qg, kt, gaml = (per(z[n]) for n in ("u", "wk", "p", "qg", "kt", "gaml"))
        dout = per(ld(do_ref))
        ds = ds_ref[...]
        d_u, d_wk, d_p, d_qg, d_kt, d_gaml = ([None] * nb for _ in range(6))
        for n in reversed(range(nb)):
            s0 = s0_ref[:, n]
            s0b, dsb, dob = _bf(s0), _bf(ds), _bf(dout[:, n])
            wkb, qgb = _bf(wk[:, n]), _bf(qg[:, n])
            vn = u[:, n] - _bmm(wkb, s0b, 2, 1)
            dvn = _bmm(_bf(p[:, n]), dob, 1, 1) + _bmm(_bf(kt[:, n]), dsb, 2, 1)
            dvnb = _bf(dvn)
            d_u[n] = dvn
            d_p[n] = _bmm(dob, _bf(vn), 2, 2)
            d_qg[n] = _bmm(dob, s0b, 2, 2)
            d_kt[n] = _bmm(_bf(vn), dsb, 2, 2)
            d_gaml[n] = jnp.sum(s0 * ds, axis=1, keepdims=True)
            d_wk[n] = -_bmm(dvnb, s0b, 2, 2)
            ds = _bmm(qgb, dob, 1, 1) + gaml[:, n] * ds - _bmm(wkb, dvnb, 1, 1)
        ds_ref[...] = ds

        flat = lambda xs: jnp.stack(xs, axis=1).reshape((bsz,) + xs[0].shape[1:])
        d_u, d_wk, d_p, d_qg, d_kt, d_gaml = (flat(x) for x in (d_u, d_wk, d_p, d_qg, d_kt, d_gaml))
        tinv, gam, kb, dm = z["tinv"], z["gam"], z["kb"], z["dm"]
        drv = _bmm(_bf(tinv), _bf(d_u), 1, 1)
        drk = _bmm(_bf(tinv), _bf(d_wk), 1, 1)
        da = -(_bmm(_bf(drv), _bf(z["u"]), 2, 2) + _bmm(_bf(drk), _bf(z["wk"]), 2, 2))
        da = jnp.where(z["strict"], da, 0.0)
        d_p = jnp.where(z["tril"], d_p, 0.0)
        dkk = _bf(da * dm)
        dqk = _bf(d_p * dm)
        dkb = _bmm(dkk, _bf(k), 2, 1) + drk * gam
        dk = _bmm(dkk, _bf(kb), 1, 1) + _bmm(dqk, _bf(q), 1, 1) + dkb * bc + d_kt * z["edec"]
        dq = _bmm(dqk, _bf(k), 2, 1) + d_qg * gam
        mm = da * z["a"] + d_p * z["p"]
        dkt_kt = d_kt * z["kt"]
        dgl = jnp.sum(dkt_kt, axis=1, keepdims=True) + d_gaml * z["gaml"]
        dgc = mm + d_qg * z["qg"] + drk * kb * gam - dkt_kt + jnp.where(z["last"], dgl, 0.0)
        dq_ref[...] = dq.reshape(h, rows, dh)
        dk_ref[...] = dk.reshape(h, rows, dh)
        dv_ref[...] = (drv * bc).reshape(h, rows, dh)
        dbeta = (dkb * k + drv * v).reshape(h, rows, dh)
        dgc = dgc.reshape(h, rows, dh)
        lane = lax.broadcasted_iota(jnp.int32, (rows, LANES), 1)
        dsc = jnp.zeros((rows, LANES), F32)
        for hh in range(h):
            dsc = jnp.where(lane == HEADS + hh, jnp.sum(dbeta[hh], axis=1, keepdims=True), dsc)
            dsc = jnp.where(lane == 2 * HEADS + hh, jnp.sum(dgc[hh], axis=1, keepdims=True), dsc)
        dsc_ref[...] = dsc
        dgr = -jnp.sum(mm, axis=1, keepdims=True)
        for hh in range(h):
            for n in range(nb):
                dgt_ref[hh:hh + 1, n * CHUNK:(n + 1) * CHUNK] = dgr[hh * nb + n]

    blk = pl.BlockSpec((h, rows, dh), lambda i: (0, ng - 1 - i, 0))
    shp = jax.ShapeDtypeStruct((h, t, dh), F32)
    sc_spec = pl.BlockSpec((rows, LANES), lambda i: (ng - 1 - i, 0))
    gt_spec = pl.BlockSpec((h, rows), lambda i: (0, ng - 1 - i))
    return pl.pallas_call(
        body, grid=(ng,), name="gdn_bwd",
        in_specs=[blk] * 3 + [sc_spec, gt_spec, pl.BlockSpec((h, nb, dh, dh), lambda i: (0, ng - 1 - i, 0, 0)), blk],
        out_specs=[blk] * 3 + [sc_spec, gt_spec],
        out_shape=[shp] * 3 + [jax.ShapeDtypeStruct((t, LANES), F32), jax.ShapeDtypeStruct((h, t), F32)],
        scratch_shapes=[pltpu.VMEM((h, dh, dh), F32)],
        compiler_params=_params("arbitrary"),
    )(q, k, v, scal, gct, s0s, do)


def _gdn_post_fwd(o, xg, gain, seg):
    t = o.shape[1]
    tt = _pick(t, (320, 256, 128))

    def body(o_ref, z_ref, g_ref, e_ref, y_ref):
        x = _load_heads(o_ref)
        r = lax.rsqrt(_segsum(x * x, e_ref[...]) * (1.0 / HEAD_DIM) + RMS_EPS)
        y_ref[...] = (x * r * g_ref[...] * _silu(z_ref[...].astype(F32))).astype(y_ref.dtype)

    return pl.pallas_call(
        body, grid=(t // tt,), name="gdn_post_fwd",
        in_specs=[pl.BlockSpec((HEADS, tt, HEAD_DIM), lambda i: (0, i, 0)), pl.BlockSpec((tt, WIDTH), lambda i: (i, 3)),
                  pl.BlockSpec((1, WIDTH), lambda i: (0, 0)), pl.BlockSpec((WIDTH, WIDTH), lambda i: (0, 0))],
        out_specs=pl.BlockSpec((tt, WIDTH), lambda i: (i, 0)),
        out_shape=jax.ShapeDtypeStruct((t, WIDTH), BF16),
        compiler_params=_params("arbitrary"),
    )(o, xg, gain, seg)


def _gdn_post_bwd(o, xg, gain, seg, dy):
    t = o.shape[1]
    tt = _pick(t, (320, 256, 128))

    def body(o_ref, z_ref, g_ref, e_ref, dy_ref, do_ref, dz_ref, dg_ref):
        x = _load_heads(o_ref)
        zz = z_ref[...].astype(F32)
        e = e_ref[...]
        gain_v = g_ref[...]
        d = dy_ref[...]
        r = lax.rsqrt(_segsum(x * x, e) * (1.0 / HEAD_DIM) + RMS_EPS)
        xr = x * r
        don = d * _silu(zz)
        dz_ref[...] = (d * xr * gain_v * _silu_grad(zz)).astype(dz_ref.dtype)
        gy = don * gain_v
        _store_heads(do_ref, r * gy - xr * (r * r) * (_segsum(gy * x, e) * (1.0 / HEAD_DIM)))

        @pl.when(pl.program_id(0) == 0)
        def _():
            dg_ref[...] = jnp.zeros_like(dg_ref)

        dg_ref[...] += jnp.sum(don * xr, axis=0, keepdims=True)

    row = pl.BlockSpec((tt, WIDTH), lambda i: (i, 0))
    vec = pl.BlockSpec((1, WIDTH), lambda i: (0, 0))
    hm = pl.BlockSpec((HEADS, tt, HEAD_DIM), lambda i: (0, i, 0))
    return pl.pallas_call(
        body, grid=(t // tt,), name="gdn_post_bwd",
        in_specs=[hm, pl.BlockSpec((tt, WIDTH), lambda i: (i, 3)), vec,
                  pl.BlockSpec((WIDTH, WIDTH), lambda i: (0, 0)), row],
        out_specs=[hm, row, vec],
        out_shape=[jax.ShapeDtypeStruct((HEADS, t, HEAD_DIM), F32), jax.ShapeDtypeStruct((t, WIDTH), BF16),
                   jax.ShapeDtypeStruct((1, WIDTH), F32)],
        compiler_params=_params("arbitrary"),
    )(o, xg, gain, seg, dy)


def _mix_fwd(yf, yg, gates, bias):
    t, d = yf.shape
    tt = _pick(t, (320, 256, 128))

    def body(yf_ref, yg_ref, g1_ref, g2_ref, b1_ref, b2_ref, o_ref):
        g1 = jax.nn.sigmoid(g1_ref[...].astype(F32) + b1_ref[...])
        g2 = jax.nn.sigmoid(g2_ref[...].astype(F32) + b2_ref[...])
        o_ref[...] = (g1 * yf_ref[...] + g2 * yg_ref[...]).astype(o_ref.dtype)

    row = pl.BlockSpec((tt, d), lambda i: (i, 0))
    return pl.pallas_call(
        body, grid=(t // tt,), name="mix_fwd",
        in_specs=[row, row, row, pl.BlockSpec((tt, d), lambda i: (i, 1)),
                  pl.BlockSpec((1, d), lambda i: (0, 0)), pl.BlockSpec((1, d), lambda i: (0, 1))],
        out_specs=row, out_shape=jax.ShapeDtypeStruct((t, d), BF16),
        compiler_params=_params("arbitrary"),
    )(yf, yg, gates, gates, bias, bias)


def _mix_bwd(dmix, yf, yg, gates, bias):
    t, d = yf.shape
    tt = _pick(t, (320, 256, 128))

    def body(dm_ref, yf_ref, yg_ref, g1_ref, g2_ref, b1_ref, b2_ref, dyf_ref, dyg_ref, dg_ref, db_ref):
        dm = dm_ref[...]
        g1 = jax.nn.sigmoid(g1_ref[...].astype(F32) + b1_ref[...])
        g2 = jax.nn.sigmoid(g2_ref[...].astype(F32) + b2_ref[...])
        dyf_ref[...] = (dm * g1).astype(BF16)
        dyg_ref[...] = (dm * g2).astype(BF16)
        dgate = jnp.concatenate([dm * yf_ref[...] * g1 * (1.0 - g1), dm * yg_ref[...] * g2 * (1.0 - g2)], axis=1)
        dg_ref[...] = dgate.astype(BF16)

        @pl.when(pl.program_id(0) == 0)
        def _():
            db_ref[...] = jnp.zeros_like(db_ref)

        db_ref[...] += jnp.sum(dgate, axis=0, keepdims=True)

    row = pl.BlockSpec((tt, d), lambda i: (i, 0))
    wide = pl.BlockSpec((tt, 2 * d), lambda i: (i, 0))
    return pl.pallas_call(
        body, grid=(t // tt,), name="mix_bwd",
        in_specs=[row, row, row, row, pl.BlockSpec((tt, d), lambda i: (i, 1)),
                  pl.BlockSpec((1, d), lambda i: (0, 0)), pl.BlockSpec((1, d), lambda i: (0, 1))],
        out_specs=[row, row, wide, pl.BlockSpec((1, 2 * d), lambda i: (0, 0))],
        out_shape=[jax.ShapeDtypeStruct((t, d), BF16), jax.ShapeDtypeStruct((t, d), BF16),
                   jax.ShapeDtypeStruct((t, 2 * d), BF16), jax.ShapeDtypeStruct((1, 2 * d), F32)],
        compiler_params=_params("arbitrary"),
    )(dmix, yf, yg, gates, gates, bias, bias)


def _ffn_act_fwd(up, conv_w, conv_b):
    t, c = up.shape
    tt = 128

    def body(x_ref, p_ref, w_ref, b_ref, o_ref):
        first = pl.program_id(0) == 0

        def conv(cols):
            prev = jnp.where(first, 0.0, _prev8(p_ref, cols))
            return _causal_conv(x_ref[:, cols].astype(F32), prev, w_ref, FFN_CONV, cols) + b_ref[:, cols]

        for lo in range(0, D_FF, FFN_LANES):
            gate = conv(slice(lo, lo + FFN_LANES))
            val = conv(slice(D_FF + lo, D_FF + lo + FFN_LANES))
            o_ref[:, lo:lo + FFN_LANES] = (_silu(gate) * val).astype(o_ref.dtype)

    return pl.pallas_call(
        body, grid=(t // tt,), name="ffn_act_fwd",
        in_specs=[pl.BlockSpec((tt, c), lambda i: (i, 0)), _prev_spec(tt, c),
                  pl.BlockSpec((FFN_CONV, c), lambda i: (0, 0)), pl.BlockSpec((1, c), lambda i: (0, 0))],
        out_specs=pl.BlockSpec((tt, D_FF), lambda i: (i, 0)),
        out_shape=jax.ShapeDtypeStruct((t, D_FF), BF16),
        compiler_params=_params("arbitrary"),
    )(up, up, conv_w, conv_b)


def _ffn_act_bwd(up, conv_w, conv_b, dact):
    t, c = up.shape
    tt = 128
    nt = t // tt

    def body(x_ref, p_ref, w_ref, b_ref, da_ref, dx_ref, dw_ref, db_ref, carry_ref):
        step = pl.program_id(0)

        @pl.when(step == 0)
        def _():
            carry_ref[...] = jnp.zeros_like(carry_ref)
            dw_ref[...] = jnp.zeros_like(dw_ref)
            db_ref[...] = jnp.zeros_like(db_ref)

        def conv(cols):
            x = x_ref[:, cols].astype(F32)
            prev = jnp.where(step == nt - 1, 0.0, _prev8(p_ref, cols))
            return x, prev, _causal_conv(x, prev, w_ref, FFN_CONV, cols) + b_ref[:, cols]

        def back(cols, x, prev, du):
            dx = _causal_conv_bwd(x, prev, du, carry_ref[:, cols], w_ref, dw_ref, FFN_CONV, cols)
            dx_ref[:, cols] = dx.astype(dx_ref.dtype)
            db_ref[:, cols] += jnp.sum(du, axis=0, keepdims=True)
            carry_ref[:, cols] = du[:8]

        for lo in range(0, D_FF, FFN_LANES):
            gcols, vcols = slice(lo, lo + FFN_LANES), slice(D_FF + lo, D_FF + lo + FFN_LANES)
            xg, pg, gate = conv(gcols)
            xv, pv, val = conv(vcols)
            da = da_ref[:, gcols]
            back(gcols, xg, pg, da * val * _silu_grad(gate))
            back(vcols, xv, pv, da * _silu(gate))

    rev = lambda i: (nt - 1 - i, 0)
    return pl.pallas_call(
        body, grid=(nt,), name="ffn_act_bwd",
        in_specs=[pl.BlockSpec((tt, c), rev),
                  _prev_spec(tt, c, lambda i: nt - 1 - i),
                  pl.BlockSpec((FFN_CONV, c), lambda i: (0, 0)), pl.BlockSpec((1, c), lambda i: (0, 0)),
                  pl.BlockSpec((tt, D_FF), rev)],
        out_specs=[pl.BlockSpec((tt, c), rev), pl.BlockSpec((FFN_CONV, c), lambda i: (0, 0)),
                   pl.BlockSpec((1, c), lambda i: (0, 0))],
        out_shape=[jax.ShapeDtypeStruct((t, c), BF16), jax.ShapeDtypeStruct((FFN_CONV, c), F32),
                   jax.ShapeDtypeStruct((1, c), F32)],
        scratch_shapes=[pltpu.VMEM((8, c), F32)],
        compiler_params=_params("arbitrary"),
    )(up, up, conv_w, conv_b, dact)


def _final_loss(h2, target, gain, seq):
    t, d = h2.shape
    tr = _pick(t, (320, 256, 128))

    def body(h_ref, t_ref, g_ref, loss_ref, dh_ref, dhb_ref, dg_ref):
        i = pl.program_id(0)
        x = h_ref[...]
        gain_v = g_ref[...]
        r = lax.rsqrt(jnp.mean(x * x, axis=-1, keepdims=True) + RMS_EPS)
        xr = x * r
        rows = i * tr + lax.broadcasted_iota(jnp.int32, (tr, 1), 0)
        real = (rows >= N_META) & (rows < N_META + seq)
        err = jnp.where(real, xr * gain_v - t_ref[...], 0.0)
        dy = err * (1.0 / d)
        gy = dy * gain_v
        dh = r * (gy - xr * jnp.mean(gy * xr, axis=-1, keepdims=True))
        dh_ref[...] = dh
        dhb_ref[...] = dh.astype(BF16)

        @pl.when(i == 0)
        def _():
            loss_ref[...] = jnp.zeros_like(loss_ref)
            dg_ref[...] = jnp.zeros_like(dg_ref)

        part = jnp.sum(jnp.sum(err * err, axis=-1, keepdims=True), axis=0, keepdims=True)
        loss_ref[...] += jnp.broadcast_to(part * (0.5 / d), loss_ref.shape)
        dg_ref[...] += jnp.sum(dy * xr, axis=0, keepdims=True)

    row = pl.BlockSpec((tr, d), lambda i: (i, 0))
    vec = pl.BlockSpec((1, d), lambda i: (0, 0))
    return pl.pallas_call(
        body, grid=(t // tr,), name="final_loss",
        in_specs=[row, row, vec],
        out_specs=[pl.BlockSpec((1, LANES), lambda i: (0, 0)), row, row, vec],
        out_shape=[jax.ShapeDtypeStruct((1, LANES), F32), jax.ShapeDtypeStruct((t, d), F32),
                   jax.ShapeDtypeStruct((t, d), BF16), jax.ShapeDtypeStruct((1, d), F32)],
        compiler_params=_params("arbitrary"),
    )(h2, target, gain)


ADAM_TILE_BYTES = 1 << 20


def _adamw(w, m, v, grecv, name):
    r, cols = w.shape
    tr = r
    if r * cols * 4 > ADAM_TILE_BYTES:
        tr = max(d for d in range(8, r + 1, 8) if r % d == 0 and d * cols * 4 <= ADAM_TILE_BYTES)

    def body(w_ref, m_ref, v_ref, g_ref, go_ref, d_ref, mo_ref, vo_ref):
        g = g_ref[0].astype(F32)
        for s in range(1, N_DEV):
            g = g + g_ref[s].astype(F32)
        wv = w_ref[...]
        mn = ADAM_B1 * m_ref[...] + (1.0 - ADAM_B1) * g
        vn = ADAM_B2 * v_ref[...] + (1.0 - ADAM_B2) * (g * g)
        m_hat = mn / (1.0 - ADAM_B1 ** ADAM_STEP)
        v_hat = vn / (1.0 - ADAM_B2 ** ADAM_STEP)
        go_ref[...] = g
        d_ref[...] = -ADAM_LR * (m_hat / (jnp.sqrt(v_hat) + ADAM_EPS) + ADAM_WD * wv)
        mo_ref[...] = mn
        vo_ref[...] = vn

    row = pl.BlockSpec((tr, cols), lambda i: (i, 0))
    shp = jax.ShapeDtypeStruct((r, cols), F32)
    return pl.pallas_call(
        body, grid=(r // tr,), name=name,
        in_specs=[row, row, row, pl.BlockSpec((N_DEV, tr, cols), lambda i: (0, i, 0))],
        out_specs=[row] * 4, out_shape=[shp] * 4,
        compiler_params=_params("parallel"),
    )(w, m, v, grecv)


def _mesh_pos():
    return lax.axis_index("x"), lax.axis_index("y"), lax.axis_index("c")


def _all_gather(shards):
    n = len(shards)

    def body(*refs):
        x_refs, out_refs = refs[:n], refs[n:2 * n]
        send_sems, recv_sems, local_sems = refs[2 * n:]
        x, y, c = _mesh_pos()
        me, sibling = (x, y, c), (x, y, 1 - c)
        chips = [(1 - x, y), (x, 1 - y), (1 - x, 1 - y)]

        def slot(a, px, py, pc):
            return out_refs[a].at[4 * px + 2 * py + pc]

        def copy(a, kk, block, to, src=None):
            return pltpu.make_async_remote_copy(
                src_ref=slot(a, *block) if src is None else src, dst_ref=slot(a, *block),
                send_sem=send_sems.at[7 * a + kk], recv_sem=recv_sems.at[7 * a + kk],
                device_id=to, device_id_type=MESH_ID)

        mine = [pltpu.make_async_copy(x_refs[a], slot(a, *me), local_sems.at[a]) for a in range(n)]
        first = []
        for a in range(n):
            first.append(copy(a, 0, me, sibling, src=x_refs[a]))
            first += [copy(a, 1 + j, me, (*chip, c), src=x_refs[a]) for j, chip in enumerate(chips)]
        for cp in mine + first:
            cp.start()
        passed = []
        for j, chip in enumerate(chips):
            for a in range(n):
                copy(a, 1 + j, (*chip, c), me).wait_recv()
                passed.append(copy(a, 4 + j, (*chip, c), sibling))
                passed[-1].start()
        for a in range(n):
            copy(a, 0, sibling, me).wait_recv()
        for j, chip in enumerate(chips):
            for a in range(n):
                copy(a, 4 + j, (*chip, 1 - c), me).wait_recv()
        for cp in first + passed:
            cp.wait_send()
        for cp in mine:
            cp.wait()

    hbm = pl.BlockSpec(memory_space=pl.ANY)
    return pl.pallas_call(
        body, name="weight_all_gather", in_specs=[hbm] * n, out_specs=[hbm] * n,
        out_shape=[jax.ShapeDtypeStruct((N_DEV,) + s.shape, s.dtype) for s in shards],
        scratch_shapes=[pltpu.SemaphoreType.DMA((7 * n,)), pltpu.SemaphoreType.DMA((7 * n,)),
                        pltpu.SemaphoreType.DMA((n,))],
    )(*shards)


def _grad_exchange(blocks, small):
    n = len(blocks)

    def body(*refs):
        src_refs, dst_refs = refs[:n + 1], refs[n + 1:2 * n + 2]
        send_sems, recv_sems, local_sems = refs[2 * n + 2:]
        x, y, c = _mesh_pos()
        me = 4 * x + 2 * y + c
        copies = []
        for kk in range(1, N_DEV):
            px = 1 - x if kk & 4 else x
            py = 1 - y if kk & 2 else y
            pc = 1 - c if kk & 1 else c
            peer = 4 * px + 2 * py + pc
            for a in range(n + 1):
                copies.append(pltpu.make_async_remote_copy(
                    src_ref=src_refs[a].at[peer] if a < n else src_refs[a], dst_ref=dst_refs[a].at[me],
                    send_sem=send_sems.at[7 * a + kk - 1], recv_sem=recv_sems.at[7 * a + kk - 1],
                    device_id=(px, py, pc), device_id_type=MESH_ID))
        own = [pltpu.make_async_copy(src_refs[a].at[me] if a < n else src_refs[a], dst_refs[a].at[me],
                                     local_sems.at[a]) for a in range(n + 1)]
        for cp in own + copies:
            cp.start()
        for cp in copies + own:
            cp.wait()

    hbm = pl.BlockSpec(memory_space=pl.ANY)
    return pl.pallas_call(
        body, name="grad_exchange", in_specs=[hbm] * (n + 1), out_specs=[hbm] * (n + 1),
        out_shape=[jax.ShapeDtypeStruct(b.shape, b.dtype) for b in blocks]
        + [jax.ShapeDtypeStruct((N_DEV,) + small.shape, small.dtype)],
        scratch_shapes=[pltpu.SemaphoreType.DMA((7 * (n + 1),)), pltpu.SemaphoreType.DMA((7 * (n + 1),)),
                        pltpu.SemaphoreType.DMA((n + 1,))],
    )(*blocks, small)


def _exchange_copies(src_refs, land_refs, send_sems, recv_sems):
    x, y, c = _mesh_pos()
    me = 4 * x + 2 * y + c
    copies = []
    for kk in range(1, N_DEV):
        px = 1 - x if kk & 4 else x
        py = 1 - y if kk & 2 else y
        pc = 1 - c if kk & 1 else c
        for a, (src, land) in enumerate(zip(src_refs, land_refs)):
            copies.append(pltpu.make_async_remote_copy(
                src_ref=src.at[4 * px + 2 * py + pc], dst_ref=land.at[me],
                send_sem=send_sems.at[7 * a + kk - 1], recv_sem=recv_sems.at[7 * a + kk - 1],
                device_id=(px, py, pc), device_id_type=MESH_ID))
    return copies


def _gather_copies(src_refs, land_refs, send_sems, recv_sems):
    x, y, c = _mesh_pos()
    me = 4 * x + 2 * y + c
    copies = []
    for kk in range(1, N_DEV):
        px = 1 - x if kk & 4 else x
        py = 1 - y if kk & 2 else y
        pc = 1 - c if kk & 1 else c
        for a, (src, land) in enumerate(zip(src_refs, land_refs)):
            copies.append(pltpu.make_async_remote_copy(
                src_ref=src, dst_ref=land.at[me],
                send_sem=send_sems.at[7 * a + kk - 1], recv_sem=recv_sems.at[7 * a + kk - 1],
                device_id=(px, py, pc), device_id_type=MESH_ID))
    return copies


_HBM = pl.BlockSpec(memory_space=pltpu.HBM)
_SEM = pl.BlockSpec(memory_space=pltpu.SEMAPHORE)
_DATAFLOW = pltpu.SideEffectType.DATAFLOW_SIDE_EFFECTING


def _split_start(name, make_copies, sources, land_shapes):
    n = len(sources)

    def body(*refs):
        src_refs, land_refs, send_sems, recv_sems = refs[:n], refs[n:2 * n], refs[2 * n], refs[2 * n + 1]
        for cp in make_copies(src_refs, land_refs, send_sems, recv_sems):
            cp.start()
        token = refs[-1]
        token[...] = jnp.zeros_like(token)

    in_hbm = lambda a: pltpu.with_memory_space_constraint(a, pltpu.HBM)
    hbm_shapes = [pltpu.HBM(s.shape, s.dtype) for s in list(sources) + list(land_shapes)]
    outs = pl.pallas_call(
        body, name=name, in_specs=[_HBM] * (2 * n),
        out_shape=(pltpu.SemaphoreType.DMA((7 * n,)), pltpu.SemaphoreType.DMA((7 * n,)), *hbm_shapes,
                   jax.ShapeDtypeStruct((8, LANES), F32)),
        out_specs=(_SEM, _SEM, *[_HBM] * (2 * n), pl.BlockSpec(memory_space=pltpu.VMEM)),
        input_output_aliases={a: 2 + a for a in range(2 * n)},
        compiler_params=pltpu.CompilerParams(has_side_effects=_DATAFLOW),
    )(*[in_hbm(s) for s in sources], *[in_hbm(lax.empty(s.shape, s.dtype)) for s in land_shapes])
    return outs[0], outs[1], outs[2:2 + n], outs[2 + n:2 + 2 * n], outs[-1]


def _split_wait(name, make_copies, send_sems, recv_sems, src_thru, land_thru, after):
    n = len(src_thru)

    def body(*refs):
        src_refs, land_refs, send_sems, recv_sems = refs[:n], refs[n:2 * n], refs[2 * n], refs[2 * n + 1]
        for cp in make_copies(src_refs, land_refs, send_sems, recv_sems):
            cp.wait_send()
            cp.wait_recv()

    outs = pl.pallas_call(
        body, name=name,
        in_specs=[_HBM] * (2 * n) + [_SEM, _SEM, pl.BlockSpec(memory_space=pl.ANY)],
        out_shape=tuple(pltpu.HBM(b.shape, b.dtype) for b in list(src_thru) + list(land_thru)),
        out_specs=[_HBM] * (2 * n), input_output_aliases={a: a for a in range(2 * n)},
        compiler_params=pltpu.CompilerParams(has_side_effects=_DATAFLOW),
    )(*src_thru, *land_thru, send_sems, recv_sems, after)
    return outs[:n], outs[n:]


def _exchange_start(blocks):
    return _split_start("grad_exchange_start", _exchange_copies, blocks, blocks)


def _exchange_wait(send_sems, recv_sems, src_thru, land_thru, after):
    return _split_wait("grad_exchange_wait", _exchange_copies, send_sems, recv_sems, src_thru, land_thru, after)


def _gather_start(shards):
    lands = [jax.ShapeDtypeStruct((N_DEV,) + s.shape, s.dtype) for s in shards]
    return _split_start("weight_gather_start", _gather_copies, shards, lands)


def _gather_wait(send_sems, recv_sems, src_thru, land_thru, after):
    return _split_wait("weight_gather_wait", _gather_copies, send_sems, recv_sems, src_thru, land_thru, after)


def _pad_flat(parts, rows):
    flat = jnp.concatenate([p.reshape(-1) for p in parts])
    return jnp.pad(flat, (0, rows * LANES - flat.shape[0])).reshape(rows, LANES)


def _rows_for(n_elems, mult=1024):
    rows = -(-n_elems // LANES)
    return -(-rows // mult) * mult


SHARDED = ("meta_tokens", "w_in", "gdn_conv_w", "w_branch_fox", "w_branch_gdn", "w_out", "ffn_w_up", "ffn_conv_w",
           "ffn_w_down")
MATMUL = ("w_in", "w_branch_fox", "w_branch_gdn", "w_out", "ffn_w_up", "ffn_w_down")
EXACT = ("meta_tokens", "gdn_conv_w", "ffn_conv_w")
REPLICATED = ("fgt_bias", "gdn_a_log", "gdn_dt_bias", "gdn_norm_w", "gate_bias", "norm_mix_w", "norm_ffn_w",
              "ffn_conv_b", "norm_final_w")
WEIGHTS = ("meta_tokens", "w_in", "fgt_bias", "gdn_conv_w", "gdn_a_log", "gdn_dt_bias", "gdn_norm_w", "gate_bias",
           "w_branch_fox", "w_branch_gdn", "w_out", "norm_mix_w", "norm_ffn_w", "ffn_w_up", "ffn_conv_w",
           "ffn_conv_b", "ffn_w_down", "norm_final_w")


def _unpack(buf, shapes):
    flat = buf.reshape(-1)
    out, off = [], 0
    for s in shapes:
        n = int(np.prod(s))
        out.append(flat[off:off + n].reshape(s))
        off += n
    return out


def _unpack_gathered(buf, shapes):
    flat = buf.reshape(N_DEV, -1)
    out, off = [], 0
    for s in shapes:
        n = int(np.prod(s))
        out.append(flat[:, off:off + n].reshape((N_DEV,) + tuple(s)))
        off += n
    return out


def _cat_cols(g):
    return g.transpose(1, 0, 2).reshape(g.shape[1], -1)


def _col_blocks(full, width):
    return full.reshape(full.shape[0], N_DEV, width).transpose(1, 0, 2)


def _local_step(x, target, w, early=None, late_weights=None):
    seq = x.shape[0]
    t = _padded_tokens(seq)
    pad = t - N_META - seq
    seg = _seg_matrix()
    zrows = jnp.zeros((pad, D_MODEL), F32)
    h0 = jnp.concatenate([w["meta_tokens"], x, zrows], axis=0)
    tgt = jnp.concatenate([jnp.zeros((N_META, D_MODEL), F32), target, zrows], axis=0)

    w_in = w["w_in"]
    o_f, o_g, o_z, o_b, o_a, o_gate = 1536, 1544, 3080, 3592, 3600, 3608
    w_small = jnp.concatenate([w_in[:, o_f:o_f + 8], w_in[:, o_b:o_b + 8], w_in[:, o_a:o_a + 8],
                               jnp.zeros((D_MODEL, LANES - 24), BF16)], axis=1)
    w_r = jnp.concatenate([w_in[:, :1536], w_in[:, o_g:o_z], w_in[:, o_z:o_b], w_in[:, o_gate:], w_small], axis=1)

    a1 = _rmsnorm_fwd(h0, w["norm_mix_w"])
    fq = _mm(a1, w_r[:, :1536], BF16, "proj_fox")
    xg = _mm(a1, w_r[:, 1536:3584], BF16, "proj_gdn")
    gt = _mm(a1, w_r[:, 3584:5632], BF16, "proj_gates")
    sm = _mm(a1, w_r[:, 5632:], F32, "proj_small")

    lanes_pad = lambda a, lo: jnp.pad(a, ((0, 0), (lo, LANES - lo - a.shape[1])))
    neg_exp_a = -jnp.exp(w["gdn_a_log"])
    pbias = lanes_pad(w["fgt_bias"], 0) + lanes_pad(w["gdn_dt_bias"], 2 * HEADS)
    if late_weights is not None:
        pbias = pbias + late_weights[0][0, 0]
    pscale = lanes_pad(neg_exp_a, 2 * HEADS)
    scal = _gate_fwd(sm, pbias, pscale)
    gct = scal[:, 2 * HEADS:3 * HEADS].T

    qa, ka, va, kat, vat = _fox_prep(fq, scal)
    oa, qb, qbt = _fox_fwd(qa, ka, vat)
    o_fox = _fox_post(oa)

    qh, kh, vh = _gdn_pre_fwd(xg, w["gdn_conv_w"], seg)
    og, s0s = _gdn_fwd(qh, kh, vh, scal, gct)
    norm_w = jnp.tile(w["gdn_norm_w"], (1, HEADS))
    ogn = _gdn_post_fwd(og, xg, norm_w, seg)

    if late_weights is not None:
        w = {**w, **late_weights[1](ogn)}
    yf = _mm(o_fox, w["w_branch_fox"], F32, "branch_fox")
    yg = _mm(ogn, w["w_branch_gdn"], F32, "branch_gdn")
    mix = _mix_fwd(yf, yg, gt, w["gate_bias"])
    h1 = _mm(mix, w["w_out"], F32, "out_proj", res=h0)
    a2 = _rmsnorm_fwd(h1, w["norm_ffn_w"])
    up = _mm(a2, w["ffn_w_up"], BF16, "ffn_up")
    act = _ffn_act_fwd(up, w["ffn_conv_w"], w["ffn_conv_b"])
    h2 = _mm(act, w["ffn_w_down"], F32, "ffn_down", res=h1)
    loss, dh2, dh2b, g_final = _final_loss(h2, tgt, w["norm_final_w"].reshape(1, D_MODEL), seq)

    grads = {"norm_final_w": g_final.reshape(D_MODEL)}
    grads["ffn_w_down"] = _mm_tn(act, dh2b, "wgrad_ffn_down")
    dact = _mm(dh2b, w["ffn_w_down"].T, F32, "dgrad_ffn_down")
    dup, g_cw, g_cb = _ffn_act_bwd(up, w["ffn_conv_w"], w["ffn_conv_b"], dact)
    grads["ffn_conv_w"], grads["ffn_conv_b"] = g_cw, g_cb
    grads["ffn_w_up"] = _mm_tn(a2, dup, "wgrad_ffn_up")
    da2 = _mm(dup, w["ffn_w_up"].T, F32, "dgrad_ffn_up")
    dh1, dh1b, grads["norm_ffn_w"] = _rmsnorm_bwd(h1, da2, w["norm_ffn_w"], dh2)
    grads["w_out"] = _mm_tn(mix, dh1b, "wgrad_out")
    dmix = _mm(dh1b, w["w_out"].T, F32, "dgrad_out")
    dyf, dyg, dgt, grads["gate_bias"] = _mix_bwd(dmix, yf, yg, gt, w["gate_bias"])
    grads["w_branch_fox"] = _mm_tn(o_fox, dyf, "wgrad_branch_fox")
    grads["w_branch_gdn"] = _mm_tn(ogn, dyg, "wgrad_branch_gdn")
    do_fox = _mm(dyf, w["w_branch_fox"].T, F32, "dgrad_branch_fox")
    dogn = _mm(dyg, w["w_branch_gdn"].T, F32, "dgrad_branch_gdn")

    dog, dz, g_nw = _gdn_post_bwd(og, xg, norm_w, seg, dogn)
    grads["gdn_norm_w"] = g_nw.reshape(HEADS, HEAD_DIM).sum(axis=0)[None]
    dqh, dkh, dvh, dscal_g, dgct = _gdn_bwd(qh, kh, vh, scal, gct, s0s, dog)
    dxg, grads["gdn_conv_w"] = _gdn_pre_bwd(xg, w["gdn_conv_w"], seg, dqh, dkh, dvh)

    doa, doat = _fox_bwd_prep(do_fox, oa)
    dfq, dscal_c = _fox_bwd_post(*_fox_bwd(qb, qbt, ka, kat, va, doa, doat))

    dscal = dscal_c + dscal_g + lanes_pad(dgct.T, 2 * HEADS)
    dsm, dpb, dps = _gate_bwd(sm, pbias, pscale, dscal)
    grads["fgt_bias"] = dpb[:, :HEADS]
    grads["gdn_dt_bias"] = dpb[:, 2 * HEADS:3 * HEADS]
    grads["gdn_a_log"] = dps[:, 2 * HEADS:3 * HEADS] * neg_exp_a

    dproj = jnp.concatenate([dfq, dxg, dz, dgt, dsm], axis=1)
    g_r = _mm_tn(a1, dproj, "wgrad_in")
    grads["w_in"] = jnp.concatenate([g_r[:, :1536], g_r[:, 5632:5640], g_r[:, 1536:3072], g_r[:, 3072:3584],
                                     g_r[:, 5640:5648], g_r[:, 5648:5656], g_r[:, 3584:5632]], axis=1)
    token, handle = early(grads) if early is not None else (jnp.zeros((8, LANES), F32), None)
    w_rt = w_r.T + token[0, 0].astype(BF16)
    da1 = _mm(dproj, w_rt, F32, "dgrad_in")
    dh0, _, grads["norm_mix_w"] = _rmsnorm_bwd(h0, da1, w["norm_mix_w"], dh1)
    grads["meta_tokens"] = dh0[:N_META]
    return loss, dh0[N_META:N_META + seq], grads, handle


def _shard_pieces(arrs):
    return [arrs[n][0] if arrs[n].ndim == 3 else arrs[n] for n in SHARDED]


def _full_grad_blocks(grads):
    g = grads
    cols = lambda a, wd: _col_blocks(a, wd)
    rows = lambda a: a.reshape(N_DEV, a.shape[0] // N_DEV, a.shape[1])
    return [cols(g["w_in"], IN_WIDTH // N_DEV), cols(g["gdn_conv_w"], 3 * WIDTH // N_DEV),
            cols(g["w_branch_fox"], D_MODEL // N_DEV), cols(g["w_branch_gdn"], D_MODEL // N_DEV), rows(g["w_out"]),
            cols(g["ffn_w_up"], 2 * D_FF // N_DEV), cols(g["ffn_conv_w"], 2 * D_FF // N_DEV), rows(g["ffn_w_down"])]


def kernel(x, meta_tokens, w_in, fgt_bias, gdn_conv_w, gdn_a_log, gdn_dt_bias, gdn_norm_w, gate_bias, w_branch_fox, w_branch_gdn, w_out, norm_mix_w, norm_ffn_w, ffn_w_up, ffn_conv_w, ffn_conv_b, ffn_w_down, norm_final_w, loss_target, m_meta_tokens, m_w_in, m_fgt_bias, m_gdn_conv_w, m_gdn_a_log, m_gdn_dt_bias, m_gdn_norm_w, m_gate_bias, m_w_branch_fox, m_w_branch_gdn, m_w_out, m_norm_mix_w, m_norm_ffn_w, m_ffn_w_up, m_ffn_conv_w, m_ffn_conv_b, m_ffn_w_down, m_norm_final_w, v_meta_tokens, v_w_in, v_fgt_bias, v_gdn_conv_w, v_gdn_a_log, v_gdn_dt_bias, v_gdn_norm_w, v_gate_bias, v_w_branch_fox, v_w_branch_gdn, v_w_out, v_norm_mix_w, v_norm_ffn_w, v_ffn_w_up, v_ffn_conv_w, v_ffn_conv_b, v_ffn_w_down, v_norm_final_w):
    wts = dict(meta_tokens=meta_tokens, w_in=w_in, fgt_bias=fgt_bias, gdn_conv_w=gdn_conv_w, gdn_a_log=gdn_a_log,
               gdn_dt_bias=gdn_dt_bias, gdn_norm_w=gdn_norm_w, gate_bias=gate_bias, w_branch_fox=w_branch_fox,
               w_branch_gdn=w_branch_gdn, w_out=w_out, norm_mix_w=norm_mix_w, norm_ffn_w=norm_ffn_w,
               ffn_w_up=ffn_w_up, ffn_conv_w=ffn_conv_w, ffn_conv_b=ffn_conv_b, ffn_w_down=ffn_w_down,
               norm_final_w=norm_final_w)
    mom = dict(meta_tokens=m_meta_tokens, w_in=m_w_in, fgt_bias=m_fgt_bias, gdn_conv_w=m_gdn_conv_w,
               gdn_a_log=m_gdn_a_log, gdn_dt_bias=m_gdn_dt_bias, gdn_norm_w=m_gdn_norm_w, gate_bias=m_gate_bias,
               w_branch_fox=m_w_branch_fox, w_branch_gdn=m_w_branch_gdn, w_out=m_w_out, norm_mix_w=m_norm_mix_w,
               norm_ffn_w=m_norm_ffn_w, ffn_w_up=m_ffn_w_up, ffn_conv_w=m_ffn_conv_w, ffn_conv_b=m_ffn_conv_b,
               ffn_w_down=m_ffn_w_down, norm_final_w=m_norm_final_w)
    var = dict(meta_tokens=v_meta_tokens, w_in=v_w_in, fgt_bias=v_fgt_bias, gdn_conv_w=v_gdn_conv_w,
               gdn_a_log=v_gdn_a_log, gdn_dt_bias=v_gdn_dt_bias, gdn_norm_w=v_gdn_norm_w, gate_bias=v_gate_bias,
               w_branch_fox=v_w_branch_fox, w_branch_gdn=v_w_branch_gdn, w_out=v_w_out, norm_mix_w=v_norm_mix_w,
               norm_ffn_w=v_norm_ffn_w, ffn_w_up=v_ffn_w_up, ffn_conv_w=v_ffn_conv_w, ffn_conv_b=v_ffn_conv_b,
               ffn_w_down=v_ffn_w_down, norm_final_w=v_norm_final_w)

    sh = dict(zip(SHARDED, _shard_pieces(wts)))
    me = 4 * lax.axis_index("x") + 2 * lax.axis_index("y") + lax.axis_index("c")
    late_names = MATMUL[1:]
    late_sems_send, late_sems_recv, late_src, late_land, late_token = _gather_start(
        [sh[n].astype(BF16) for n in late_names])
    exact_shapes = [sh[n].shape for n in EXACT]
    rows_exact = _rows_for(sum(int(np.prod(s)) for s in exact_shapes), 8)
    g_in, g_exact = _all_gather([sh["w_in"].astype(BF16), _pad_flat([sh[n] for n in EXACT], rows_exact)])
    meta_full, conv_full, fconv_full = (_cat_cols(a) for a in _unpack_gathered(g_exact, exact_shapes))
    full = dict(
        meta_tokens=meta_full, w_in=_cat_cols(g_in), gdn_conv_w=conv_full, ffn_conv_w=fconv_full,
        fgt_bias=fgt_bias, gdn_a_log=gdn_a_log, gdn_dt_bias=gdn_dt_bias, gdn_norm_w=gdn_norm_w, gate_bias=gate_bias,
        norm_mix_w=norm_mix_w, norm_ffn_w=norm_ffn_w, ffn_conv_b=ffn_conv_b, norm_final_w=norm_final_w)

    def fetch_late_weights(after):
        shards, lands = _gather_wait(late_sems_send, late_sems_recv, late_src, late_land, after)
        g_bf, g_bg, g_out, g_up, g_down = (lax.dynamic_update_slice_in_dim(land, s[None], me, 0)
                                           for s, land in zip(shards, lands))
        return dict(w_branch_fox=_cat_cols(g_bf), w_branch_gdn=_cat_cols(g_bg), w_out=g_out.reshape(D_MODEL, D_MODEL),
                    ffn_w_up=_cat_cols(g_up), ffn_w_down=g_down.reshape(D_FF, D_MODEL))

    def start_exchange(grads_so_far):
        blocks = [b.astype(BF16) for b in _full_grad_blocks(grads_so_far)]
        send_sems, recv_sems, src_thru, land_thru, token = _exchange_start(blocks)
        return token, (send_sems, recv_sems, src_thru, land_thru)

    loss, grad_x, grads, handle = _local_step(x[0], loss_target[0], full, early=start_exchange,
                                              late_weights=(late_token, fetch_late_weights))
    sent, landed = _exchange_wait(*handle, after=grad_x)
    own = lambda src, land: lax.dynamic_update_slice_in_dim(land, lax.dynamic_slice_in_dim(src, me, 1, 0), me, 0)
    received = [own(src, land) for src, land in zip(sent, landed)]

    rep_parts = [grads[n] for n in REPLICATED] + [loss[:, :1]]
    rep_shapes = [wts[n].shape for n in REPLICATED]
    rows_small = _rows_for(sum(int(np.prod(p.shape)) for p in rep_parts), 8)
    meta_recv, small_recv = _grad_exchange([_col_blocks(grads["meta_tokens"], LANES).astype(BF16)],
                                           _pad_flat(rep_parts, rows_small))
    received = [meta_recv] + received + [small_recv]

    result = {}
    kinds = ("grad", "delta", "new_m", "new_v")
    for n, recv in zip(SHARDED, received[:-1]):
        outs = _adamw(sh[n], _shard_pieces(mom)[SHARDED.index(n)], _shard_pieces(var)[SHARDED.index(n)], recv,
                      "adamw_" + n)
        for kind, a in zip(kinds, outs):
            result[kind, n] = a.reshape(wts[n].shape)
    rep_w = _pad_flat([wts[n] for n in REPLICATED] + [jnp.zeros((1, 1), F32)], rows_small)
    rep_m = _pad_flat([mom[n] for n in REPLICATED] + [jnp.zeros((1, 1), F32)], rows_small)
    rep_v = _pad_flat([var[n] for n in REPLICATED] + [jnp.ones((1, 1), F32)], rows_small)
    outs_r = _adamw(rep_w, rep_m, rep_v, received[-1], "adamw_replicated")
    for kind, br in zip(kinds, outs_r):
        for n, a in zip(REPLICATED, _unpack(br, rep_shapes)):
            result[kind, n] = a
    n_rep = sum(int(np.prod(s)) for s in rep_shapes)
    total_loss = outs_r[0].reshape(-1)[n_rep]
    out = [total_loss, grad_x[None]]
    for kind in ("grad", "delta", "new_m", "new_v"):
        out += [result[kind, n] for n in WEIGHTS]
    return tuple(out)
```

```python
import functools

import jax
import jax.numpy as jnp
import numpy as np
from jax import lax
from jax.experimental import pallas as pl
from jax.experimental.pallas import tpu as pltpu

F32 = jnp.float32
BF16 = jnp.bfloat16

D_MODEL = 1024
N_META = 16
HEADS = 8
HEAD_DIM = 64
WIDTH = HEADS * HEAD_DIM
CHUNK = 64
GDN_CONV = 4
D_FF = 2816
FFN_CONV = 3
IN_WIDTH = 5656
IN_PAD = 5760
RMS_EPS = 1e-6
NEG = -1e30
AUG = 128
N_DEV = 8
LANES = 128

ADAM_LR = 0.001
ADAM_B1 = 0.9
ADAM_B2 = 0.999
ADAM_EPS = 1e-08
ADAM_WD = 0.01
ADAM_STEP = 10

VMEM_LIMIT = 56 * 1024 * 1024
MM_VMEM_BUDGET = 36 * 1024 * 1024
FFN_LANES = 128
HI = lax.Precision.HIGH
MESH_ID = pl.DeviceIdType.MESH


def _pick(n, cands):
    for c in cands:
        if n % c == 0:
            return c
    raise ValueError(f"no tile for {n} in {cands}")


def _params(*sem):
    return pltpu.CompilerParams(dimension_semantics=sem if sem else None, vmem_limit_bytes=VMEM_LIMIT)


def _padded_tokens(seq):
    t = -(-(N_META + seq) // 128) * 128
    if t > 1280 and t % 640:
        t = -(-t // 640) * 640
    return t


ROW_TILES = (640, 512, 384, 256, 128)


def _rmsnorm_fwd(h, gain):
    t, d = h.shape
    tr = _pick(t, ROW_TILES)

    def body(h_ref, g_ref, o_ref):
        x = h_ref[...]
        r = lax.rsqrt(jnp.mean(x * x, axis=-1, keepdims=True) + RMS_EPS)
        o_ref[...] = (x * r * g_ref[...]).astype(o_ref.dtype)

    return pl.pallas_call(
        body, grid=(t // tr,), name="rmsnorm_fwd",
        in_specs=[pl.BlockSpec((tr, d), lambda i: (i, 0)), pl.BlockSpec((1, d), lambda i: (0, 0))],
        out_specs=pl.BlockSpec((tr, d), lambda i: (i, 0)),
        out_shape=jax.ShapeDtypeStruct((t, d), BF16),
        compiler_params=_params("arbitrary"),
    )(h, gain)


def _rmsnorm_bwd(h, dy, gain, dres):
    t, d = h.shape
    tr = _pick(t, (320, 256, 128))

    def body(h_ref, dy_ref, g_ref, dres_ref, dh_ref, dhb_ref, dg_ref):
        x = h_ref[...]
        dyv = dy_ref[...]
        r = lax.rsqrt(jnp.mean(x * x, axis=-1, keepdims=True) + RMS_EPS)
        gy = dyv * g_ref[...]
        m = jnp.mean(gy * x, axis=-1, keepdims=True)
        dh = dres_ref[...] + r * gy - x * (r * r * r * m)
        dh_ref[...] = dh
        dhb_ref[...] = dh.astype(BF16)

        @pl.when(pl.program_id(0) == 0)
        def _():
            dg_ref[...] = jnp.zeros_like(dg_ref)

        dg_ref[...] += jnp.sum(dyv * x * r, axis=0, keepdims=True)

    row = pl.BlockSpec((tr, d), lambda i: (i, 0))
    vec = pl.BlockSpec((1, d), lambda i: (0, 0))
    return pl.pallas_call(
        body, grid=(t // tr,), name="rmsnorm_bwd",
        in_specs=[row, row, vec, row], out_specs=[row, row, vec],
        out_shape=[jax.ShapeDtypeStruct((t, d), F32), jax.ShapeDtypeStruct((t, d), BF16),
                   jax.ShapeDtypeStruct((1, d), F32)],
        compiler_params=_params("arbitrary"),
    )(h, dy, gain, dres)


def _mm(a, b, out_dtype, name, res=None):
    m, k = a.shape
    _, n = b.shape
    tm = _pick(m, ROW_TILES)
    out_bytes = jnp.dtype(out_dtype).itemsize + (4 if res is not None else 0)
    fits = lambda tn: 4 * tm * k + 4 * k * tn + 2 * tm * tn * out_bytes <= MM_VMEM_BUDGET
    tn = next(c for c in (n, 2816, 2048, 1536, 1408, 1024, 512, 384, 256, 128) if n % c == 0 and fits(c))

    def body(*refs):
        if res is None:
            a_ref, b_ref, o_ref = refs
        else:
            a_ref, b_ref, r_ref, o_ref = refs
        out = jnp.dot(a_ref[...], b_ref[...], preferred_element_type=F32)
        if res is not None:
            out = out + r_ref[...]
        o_ref[...] = out.astype(o_ref.dtype)

    in_specs = [pl.BlockSpec((tm, k), lambda i, j: (i, 0)), pl.BlockSpec((k, tn), lambda i, j: (0, j))]
    args = [a, b]
    if res is not None:
        in_specs.append(pl.BlockSpec((tm, tn), lambda i, j: (i, j)))
        args.append(res)
    return pl.pallas_call(
        body, grid=(m // tm, n // tn), name=name,
        in_specs=in_specs, out_specs=pl.BlockSpec((tm, tn), lambda i, j: (i, j)),
        out_shape=jax.ShapeDtypeStruct((m, n), out_dtype),
        compiler_params=_params("parallel", "parallel"),
    )(*args)


def _mm_tn(a, g, name):
    t, k = a.shape
    _, n = g.shape
    tk = _pick(k, (1024, 1408, 512))
    tn = _pick(n, (512, 640, 384, 256, 128))
    tt = next(c for c in (3328, 1280) + ROW_TILES
              if t % c == 0 and 4 * c * (tk + tn) + 8 * tk * tn <= MM_VMEM_BUDGET)
    nt = t // tt

    def body(a_ref, g_ref, o_ref):
        @pl.when(pl.program_id(2) == 0)
        def _():
            o_ref[...] = jnp.zeros_like(o_ref)

        o_ref[...] += lax.dot_general(a_ref[...], g_ref[...], (((0,), (0,)), ((), ())),
                                      preferred_element_type=F32)

    return pl.pallas_call(
        body, grid=(k // tk, n // tn, nt), name=name,
        in_specs=[pl.BlockSpec((tt, tk), lambda i, j, s: (s, i)), pl.BlockSpec((tt, tn), lambda i, j, s: (s, j))],
        out_specs=pl.BlockSpec((tk, tn), lambda i, j, s: (i, j)),
        out_shape=jax.ShapeDtypeStruct((k, n), F32),
        compiler_params=_params("parallel", "parallel", "arbitrary"),
    )(a, g)


def _split3_exact(x):
    def top(v):
        return lax.bitcast_convert_type(lax.bitcast_convert_type(v, jnp.int32) & jnp.int32(-65536), F32)

    hi = top(x)
    r1 = x - hi
    mid = top(r1)
    return hi, mid, r1 - mid


def _pair_head(ref, h, rows):
    x = ref[:, 128 * (h // 2):128 * (h // 2) + 128].astype(F32)
    return pltpu.roll(x, HEAD_DIM, axis=1) if h % 2 else x


def _lanes(rows):
    return lax.broadcasted_iota(jnp.int32, (rows, AUG), 1)


def _fox_prep(fq, scal):
    t = fq.shape[0]
    tt = _pick(t, (256, 128))

    def body(q_ref, k_ref, v_ref, s_ref, qa_ref, ka_ref, va_ref, kt_ref, vt_ref):
        lane = _lanes(tt)
        chi, cmid, clo = _split3_exact(s_ref[...])
        ones = lambda lo: jnp.where((lane >= lo) & (lane < lo + 3), 1.0, 0.0)
        for h in range(HEADS):
            col = lambda a: jnp.broadcast_to(a[:, h:h + 1], (tt, AUG))
            c1, c2, c3 = col(chi), col(cmid), col(clo)
            qx = jnp.where(lane == 64, c1, jnp.where(lane == 65, c2, jnp.where(lane == 66, c3, ones(67))))
            kx = jnp.where(lane == 67, -c1, jnp.where(lane == 68, -c2, jnp.where(lane == 69, -c3, ones(64) + ones(70))))
            qa_ref[h] = jnp.where(lane < HEAD_DIM, _pair_head(q_ref, h, tt) * (HEAD_DIM ** -0.5), qx).astype(BF16)
            k_aug = jnp.where(lane < HEAD_DIM, _pair_head(k_ref, h, tt), kx)
            ka_ref[h] = k_aug.astype(BF16)
            kt_ref[h] = k_aug.T.astype(BF16)
            v_aug = jnp.where(lane < HEAD_DIM, _pair_head(v_ref, h, tt), ones(64))
            va_ref[h] = v_aug.astype(BF16)
            vt_ref[h] = v_aug.T.astype(BF16)

    out = pl.BlockSpec((HEADS, tt, AUG), lambda i: (0, i, 0))
    out_t = pl.BlockSpec((HEADS, AUG, tt), lambda i: (0, 0, i))
    shp = jax.ShapeDtypeStruct((HEADS, t, AUG), BF16)
    shp_t = jax.ShapeDtypeStruct((HEADS, AUG, t), BF16)
    return pl.pallas_call(
        body, grid=(t // tt,), name="fox_prep",
        in_specs=[pl.BlockSpec((tt, WIDTH), lambda i: (i, 0)), pl.BlockSpec((tt, WIDTH), lambda i: (i, 1)),
                  pl.BlockSpec((tt, WIDTH), lambda i: (i, 2)), pl.BlockSpec((tt, LANES), lambda i: (i, 0))],
        out_specs=[out, out, out, out_t, out_t], out_shape=[shp, shp, shp, shp_t, shp_t],
        compiler_params=_params("parallel"),
    )(fq, fq, fq, scal)


def _fox_post(oa):
    t = oa.shape[1]
    tt = _pick(t, (256, 128))

    def body(o_ref, out_ref):
        out_ref[...] = jnp.concatenate([o_ref[h][:, :HEAD_DIM] for h in range(HEADS)], axis=1).astype(BF16)

    return pl.pallas_call(
        body, grid=(t // tt,), name="fox_post",
        in_specs=[pl.BlockSpec((HEADS, tt, AUG), lambda i: (0, i, 0))],
        out_specs=pl.BlockSpec((tt, WIDTH), lambda i: (i, 0)),
        out_shape=jax.ShapeDtypeStruct((t, WIDTH), BF16),
        compiler_params=_params("parallel"),
    )(oa)


def _fox_bwd_prep(do, oa):
    t = do.shape[0]
    tt = _pick(t, (256, 128))

    def body(d_ref, o_ref, out_ref, outt_ref):
        lane = _lanes(tt)
        for h in range(HEADS):
            x = _pair_head(d_ref, h, tt)
            delta = jnp.sum(jnp.where(lane < HEAD_DIM, x * o_ref[h], 0.0), axis=1, keepdims=True)
            hi, mid, lo = _split3_exact(jnp.broadcast_to(-delta, (tt, AUG)))
            ex = jnp.where(lane == 64, hi, jnp.where(lane == 65, mid, jnp.where(lane == 66, lo, 0.0)))
            do_aug = jnp.where(lane < HEAD_DIM, x, ex)
            out_ref[h] = do_aug.astype(BF16)
            outt_ref[h] = do_aug.T.astype(BF16)

    hm = pl.BlockSpec((HEADS, tt, AUG), lambda i: (0, i, 0))
    return pl.pallas_call(
        body, grid=(t // tt,), name="fox_bwd_prep",
        in_specs=[pl.BlockSpec((tt, WIDTH), lambda i: (i, 0)), hm],
        out_specs=[hm, pl.BlockSpec((HEADS, AUG, tt), lambda i: (0, 0, i))],
        out_shape=[jax.ShapeDtypeStruct((HEADS, t, AUG), BF16), jax.ShapeDtypeStruct((HEADS, AUG, t), BF16)],
        compiler_params=_params("parallel"),
    )(do, oa)


def _fox_bwd_post(dqt, dkt, dvt):
    t = dqt.shape[2]
    tt = _pick(t, (256, 128))

    def body(dq_ref, dk_ref, dv_ref, out_ref, dsc_ref):
        lane = _lanes(tt)
        dqs = [dq_ref[h].T for h in range(HEADS)]
        dks = [dk_ref[h].T for h in range(HEADS)]
        heads = lambda xs: jnp.concatenate([x[:, :HEAD_DIM] for x in xs], axis=1)
        out_ref[:, 0:WIDTH] = (heads(dqs) * (HEAD_DIM ** -0.5)).astype(BF16)
        out_ref[:, WIDTH:2 * WIDTH] = heads(dks).astype(BF16)
        out_ref[:, 2 * WIDTH:] = heads([dv_ref[h].T for h in range(HEADS)]).astype(BF16)
        dsc = jnp.zeros((tt, LANES), F32)
        for h in range(HEADS):
            both = jnp.where(lane == HEAD_DIM, dqs[h], 0.0) - jnp.where(lane == HEAD_DIM + 3, dks[h], 0.0)
            dsc = jnp.where(lane == h, jnp.sum(both, axis=1, keepdims=True), dsc)
        dsc_ref[...] = dsc

    hm = pl.BlockSpec((HEADS, AUG, tt), lambda i: (0, 0, i))
    return pl.pallas_call(
        body, grid=(t // tt,), name="fox_bwd_post",
        in_specs=[hm, hm, hm],
        out_specs=[pl.BlockSpec((tt, 3 * WIDTH), lambda i: (i, 0)), pl.BlockSpec((tt, LANES), lambda i: (i, 0))],
        out_shape=[jax.ShapeDtypeStruct((t, 3 * WIDTH), BF16), jax.ShapeDtypeStruct((t, LANES), F32)],
        compiler_params=_params("parallel"),
    )(dqt, dkt, dvt)


def _fox_fwd(qa, ka, vat, tq=None):
    h, t, _ = qa.shape
    tq = tq or _pick(t, ROW_TILES)

    def body(q_ref, k_ref, vt_ref, o_ref, qb_ref, qbt_ref, s_ref):
        i = pl.program_id(1)
        q = q_ref[...]
        krow = lax.broadcasted_iota(jnp.int32, (tq, tq), 0)
        qcol = lax.broadcasted_iota(jnp.int32, (tq, tq), 1)
        rows = lambda j: pl.ds(pl.multiple_of(j * tq, tq), tq)

        def scores(j, slot):
            s_ref[slot] = lax.dot_general(k_ref[rows(j), :], q, (((1,), (1,)), ((), ())), preferred_element_type=F32)

        def update(j, slot, carry, masked):
            m, acc = carry
            s = s_ref[slot]
            if masked:
                s = jnp.where(qcol >= krow, s, NEG)
            m_new = jnp.maximum(m, jnp.max(s, axis=0, keepdims=True))
            p = jnp.exp(s - m_new)
            alpha = jnp.exp(m - m_new)
            return m_new, acc * alpha + jnp.dot(vt_ref[:, rows(j)], p.astype(BF16), preferred_element_type=F32)

        def pair(j, carry):
            scores(j + 1, 1)
            carry = update(j, 0, carry, False)
            scores(j + 2, 0)
            return update(j + 1, 1, carry, False)

        def odd_tail(carry):
            scores(i, 1)
            return update(i, 1, update(i - 1, 0, carry, False), True)

        scores(0, 0)
        carry = (jnp.full((1, tq), NEG, F32), jnp.zeros((AUG, tq), F32))
        carry = lax.fori_loop(0, i // 4, lambda jj, c: pair(4 * jj + 2, pair(4 * jj, c)), carry)
        carry = lax.fori_loop(0, (i % 4) // 2, lambda jj, c: pair(4 * (i // 4), c), carry)
        m, acc = lax.cond(i % 2 == 1, odd_tail, lambda c: update(i, 0, c, True), carry)
        sub = lax.broadcasted_iota(jnp.int32, (AUG, tq), 0)
        l = jnp.sum(jnp.where(sub == HEAD_DIM, acc, 0.0), axis=0, keepdims=True)
        out = jnp.where(sub < HEAD_DIM, acc / l, m + jnp.log(l)).T
        o_ref[...] = out
        lane = lax.broadcasted_iota(jnp.int32, (tq, AUG), 1)
        lse = jnp.broadcast_to(jnp.sum(jnp.where(lane == HEAD_DIM, out, 0.0), axis=1, keepdims=True), (tq, AUG))
        hi, mid, lo = _split3_exact(-lse)
        qb = jnp.where(lane == 70, hi, jnp.where(lane == 71, mid, jnp.where(lane == 72, lo, q.astype(F32))))
        qb_ref[...] = qb.astype(BF16)
        qbt_ref[...] = qb.T.astype(BF16)

    blk = pl.BlockSpec((None, tq, AUG), lambda hh, i: (hh, i, 0))
    return pl.pallas_call(
        body, grid=(h, t // tq), name="fox_fwd",
        in_specs=[blk, pl.BlockSpec((None, t, AUG), lambda hh, i: (hh, 0, 0)),
                  pl.BlockSpec((None, AUG, t), lambda hh, i: (hh, 0, 0))],
        out_specs=[blk, blk, pl.BlockSpec((None, AUG, tq), lambda hh, i: (hh, 0, i))],
        out_shape=[jax.ShapeDtypeStruct((h, t, AUG), F32), jax.ShapeDtypeStruct((h, t, AUG), BF16),
                   jax.ShapeDtypeStruct((h, AUG, t), BF16)],
        scratch_shapes=[pltpu.VMEM((2, tq, tq), F32)],
        compiler_params=_params("parallel", "arbitrary"),
    )(qa, ka, vat)


def _fox_bwd(qb, qbt, ka, kat, va, doa, doat, tq=None):
    h, t, _ = qb.shape
    tq = tq or _pick(t, ROW_TILES)
    nq = t // tq

    def body(q_ref, qt_ref, k_ref, kt_ref, v_ref, do_ref, dot_ref, dqt_ref, dkt_ref, dvt_ref, s_ref, dp_ref):
        j = pl.program_id(1)
        n = nq - j

        @pl.when(j == 0)
        def _():
            dqt_ref[...] = jnp.zeros_like(dqt_ref)

        dkt_ref[...] = jnp.zeros_like(dkt_ref)
        dvt_ref[...] = jnp.zeros_like(dvt_ref)
        kj = k_ref[...]
        ktj = kt_ref[...]
        vj = v_ref[...]
        qrow = lax.broadcasted_iota(jnp.int32, (tq, tq), 0)
        kcol = lax.broadcasted_iota(jnp.int32, (tq, tq), 1)
        rows = lambda i: pl.ds(pl.multiple_of(i * tq, tq), tq)
        nt_dims = (((1,), (1,)), ((), ()))

        def scores(i, slot):
            s_ref[slot] = lax.dot_general(q_ref[rows(i), :], kj, nt_dims, preferred_element_type=F32)
            dp_ref[slot] = lax.dot_general(do_ref[rows(i), :], vj, nt_dims, preferred_element_type=F32)

        def update(i, slot):
            p = jnp.exp(jnp.where((qrow >= kcol) | (i > j), s_ref[slot], NEG))
            ds = (p * dp_ref[slot]).astype(BF16)
            dvt_ref[...] += jnp.dot(dot_ref[:, rows(i)], p.astype(BF16), preferred_element_type=F32)
            dkt_ref[...] += jnp.dot(qt_ref[:, rows(i)], ds, preferred_element_type=F32)
            dqt_ref[:, rows(i)] += lax.dot_general(ktj, ds, nt_dims, preferred_element_type=F32)

        def pair(i0):
            scores(i0 + 1, 1)
            update(i0, 0)
            scores(jnp.minimum(i0 + 2, nq - 1), 0)
            update(i0 + 1, 1)

        def quad(kk, carry):
            pair(j + 4 * kk)
            pair(j + 4 * kk + 2)
            return carry

        def last_pair(kk, carry):
            pair(j + 4 * (n // 4))
            return carry

        scores(j, 0)
        lax.fori_loop(0, n // 4, quad, 0)
        lax.fori_loop(0, (n % 4) // 2, last_pair, 0)

        @pl.when(n % 2 == 1)
        def _():
            update(nq - 1, 0)

    once = pl.Buffered(1)
    full = pl.BlockSpec((None, t, AUG), lambda hh, j: (hh, 0, 0), pipeline_mode=once)
    full_t = pl.BlockSpec((None, AUG, t), lambda hh, j: (hh, 0, 0), pipeline_mode=once)
    blk = pl.BlockSpec((None, tq, AUG), lambda hh, j: (hh, j, 0))
    blk_t = pl.BlockSpec((None, AUG, tq), lambda hh, j: (hh, 0, j))
    shp = jax.ShapeDtypeStruct((h, AUG, t), F32)
    return pl.pallas_call(
        body, grid=(h, nq), name="fox_bwd",
        in_specs=[full, full_t, blk, blk_t, blk, full, full_t],
        out_specs=[pl.BlockSpec((None, AUG, t), lambda hh, j: (hh, 0, 0)), blk_t, blk_t], out_shape=[shp, shp, shp],
        scratch_shapes=[pltpu.VMEM((2, tq, tq), F32), pltpu.VMEM((2, tq, tq), F32)],
        compiler_params=_params("parallel", "arbitrary"),
    )(qb, qbt, ka, kat, va, doa, doat)


def _seg_matrix():
    idx = np.arange(WIDTH) // HEAD_DIM
    return jnp.asarray((idx[:, None] == idx[None, :]).astype(np.float32))


def _segsum(x, e):
    return jnp.dot(x, e, precision=HI, preferred_element_type=F32)


def _silu(x):
    return x * jax.nn.sigmoid(x)


def _silu_grad(x):
    s = jax.nn.sigmoid(x)
    return s * (1.0 + x * (1.0 - s))


def _shift_down(x, prev8, k):
    r = pltpu.roll(x, k, axis=0)
    p = pltpu.roll(prev8, k, axis=0)
    row = lax.broadcasted_iota(jnp.int32, prev8.shape, 0)
    head = jnp.where(row < k, p, r[:8])
    return jnp.concatenate([head, r[8:]], axis=0)


def _shift_up(x, next8, k):
    n = x.shape[0]
    r = pltpu.roll(x, n - k, axis=0)
    p = pltpu.roll(next8, 8 - k, axis=0)
    row = lax.broadcasted_iota(jnp.int32, next8.shape, 0)
    tail = jnp.where(row >= 8 - k, p, r[n - 8:])
    return jnp.concatenate([r[:n - 8], tail], axis=0)


def _causal_conv(x, prev8, w_ref, width, cols=slice(None)):
    y = x * w_ref[width - 1:width, cols]
    for k in range(1, width):
        y = y + _shift_down(x, prev8, k) * w_ref[width - 1 - k:width - k, cols]
    return y


def _causal_conv_bwd(x, prev8, dy, dnext8, w_ref, dw_ref, width, cols=slice(None)):
    dx = dy * w_ref[width - 1:width, cols]
    dw_ref[width - 1:width, cols] += jnp.sum(dy * x, axis=0, keepdims=True)
    for k in range(1, width):
        dx = dx + _shift_up(dy, dnext8, k) * w_ref[width - 1 - k:width - k, cols]
        dw_ref[width - 1 - k:width - k, cols] += jnp.sum(dy * _shift_down(x, prev8, k), axis=0, keepdims=True)
    return dx


HALO = 16


def _prev_spec(tt, width, tile=lambda i: i):
    return pl.BlockSpec((HALO, width), lambda i: (jnp.maximum(tile(i) * (tt // HALO) - 1, 0), 0))


def _prev8(p_ref, cols=slice(None)):
    return p_ref[:, cols].astype(F32)[HALO - 8:]


def _store_heads(ref, x):
    for h in range(HEADS):
        ref[h] = x[:, HEAD_DIM * h:HEAD_DIM * (h + 1)]


def _load_heads(ref):
    return jnp.concatenate([ref[h] for h in range(HEADS)], axis=1)


def _softplus(z):
    return jnp.maximum(z, 0.0) + jnp.log1p(jnp.exp(-jnp.abs(z)))


def _tri_masks(tt):
    r = lax.broadcasted_iota(jnp.int32, (tt, tt), 0)
    c = lax.broadcasted_iota(jnp.int32, (tt, tt), 1)
    same_chunk = lax.shift_right_logical(r, 6) == lax.shift_right_logical(c, 6)
    return r, c, same_chunk


def _gate_fwd(small, pbias, pscale):
    t = small.shape[0]
    tt = _pick(t, (256, 128))

    def body(x_ref, pb_ref, ps_ref, o_ref, carry_ref):
        @pl.when(pl.program_id(0) == 0)
        def _():
            carry_ref[...] = jnp.zeros_like(carry_ref)

        lane = lax.broadcasted_iota(jnp.int32, (tt, LANES), 1)
        z = x_ref[...] + pb_ref[...]
        log_f = jnp.where(lane < HEADS, -_softplus(-z), 0.0)
        g = jnp.where((lane >= 2 * HEADS) & (lane < 3 * HEADS), ps_ref[...] * _softplus(z), 0.0)
        r, c, same_chunk = _tri_masks(tt)
        lower = jnp.where(r >= c, 1.0, 0.0)
        lower_chunk = jnp.where((r >= c) & same_chunk, 1.0, 0.0)
        csum = jnp.dot(lower, log_f, precision=lax.Precision.HIGHEST, preferred_element_type=F32) + carry_ref[...]
        gc = jnp.dot(lower_chunk, g, precision=lax.Precision.HIGHEST, preferred_element_type=F32)
        carry_ref[...] += jnp.sum(log_f, axis=0, keepdims=True)
        o_ref[...] = jnp.where(lane < HEADS, csum, jnp.where(lane < 2 * HEADS, jax.nn.sigmoid(z), gc))

    row = pl.BlockSpec((tt, LANES), lambda i: (i, 0))
    vec = pl.BlockSpec((1, LANES), lambda i: (0, 0))
    return pl.pallas_call(
        body, grid=(t // tt,), name="gate_fwd", in_specs=[row, vec, vec], out_specs=row,
        out_shape=jax.ShapeDtypeStruct((t, LANES), F32),
        scratch_shapes=[pltpu.VMEM((1, LANES), F32)],
        compiler_params=_params("arbitrary"),
    )(small, pbias, pscale)


def _gate_bwd(small, pbias, pscale, dscal):
    t = small.shape[0]
    tt = _pick(t, (256, 128))
    nt = t // tt

    def body(x_ref, pb_ref, ps_ref, d_ref, dx_ref, dpb_ref, dps_ref, carry_ref):
        @pl.when(pl.program_id(0) == 0)
        def _():
            carry_ref[...] = jnp.zeros_like(carry_ref)
            dpb_ref[...] = jnp.zeros_like(dpb_ref)
            dps_ref[...] = jnp.zeros_like(dps_ref)

        lane = lax.broadcasted_iota(jnp.int32, (tt, LANES), 1)
        z = x_ref[...] + pb_ref[...]
        d = d_ref[...]
        dc = jnp.where(lane < HEADS, d, 0.0)
        dbeta = jnp.where((lane >= HEADS) & (lane < 2 * HEADS), d, 0.0)
        dgc = jnp.where((lane >= 2 * HEADS) & (lane < 3 * HEADS), d, 0.0)
        r, c, same_chunk = _tri_masks(tt)
        upper = jnp.where(r <= c, 1.0, 0.0)
        upper_chunk = jnp.where((r <= c) & same_chunk, 1.0, 0.0)
        dlogf = jnp.dot(upper, dc, precision=lax.Precision.HIGHEST, preferred_element_type=F32) + carry_ref[...]
        dg = jnp.dot(upper_chunk, dgc, precision=lax.Precision.HIGHEST, preferred_element_type=F32)
        carry_ref[...] += jnp.sum(dc, axis=0, keepdims=True)
        sg = jax.nn.sigmoid(z)
        dz = dlogf * (1.0 - sg) + dbeta * sg * (1.0 - sg) + dg * ps_ref[...] * sg
        dx_ref[...] = dz.astype(dx_ref.dtype)
        dpb_ref[...] += jnp.sum(dz, axis=0, keepdims=True)
        dps_ref[...] += jnp.sum(dg * _softplus(z), axis=0, keepdims=True)

    row = pl.BlockSpec((tt, LANES), lambda i: (nt - 1 - i, 0))
    vec = pl.BlockSpec((1, LANES), lambda i: (0, 0))
    return pl.pallas_call(
        body, grid=(nt,), name="gate_bwd", in_specs=[row, vec, vec, row], out_specs=[row, vec, vec],
        out_shape=[jax.ShapeDtypeStruct((t, LANES), BF16), jax.ShapeDtypeStruct((1, LANES), F32),
                   jax.ShapeDtypeStruct((1, LANES), F32)],
        scratch_shapes=[pltpu.VMEM((1, LANES), F32)],
        compiler_params=_params("arbitrary"),
    )(small, pbias, pscale, dscal)


def _gdn_pre_fwd(xg, conv_w, seg):
    t = xg.shape[0]
    c3 = 3 * WIDTH
    tt = _pick(t, (320, 256, 128))

    def body(x_ref, p_ref, w_ref, e_ref, q_ref, k_ref, v_ref):
        x = x_ref[...].astype(F32)
        prev = jnp.where(pl.program_id(0) == 0, 0.0, _prev8(p_ref))
        s = _silu(_causal_conv(x, prev, w_ref, GDN_CONV))
        e = e_ref[...]
        q = s[:, :WIDTH]
        k = s[:, WIDTH:2 * WIDTH]
        _store_heads(q_ref, q * lax.rsqrt(_segsum(q * q, e) + RMS_EPS) * (HEAD_DIM ** -0.5))
        _store_heads(k_ref, k * lax.rsqrt(_segsum(k * k, e) + RMS_EPS))
        _store_heads(v_ref, s[:, 2 * WIDTH:])

    out = pl.BlockSpec((HEADS, tt, HEAD_DIM), lambda i: (0, i, 0))
    shp = jax.ShapeDtypeStruct((HEADS, t, HEAD_DIM), F32)
    return pl.pallas_call(
        body, grid=(t // tt,), name="gdn_pre_fwd",
        in_specs=[pl.BlockSpec((tt, c3), lambda i: (i, 0)), _prev_spec(tt, c3),
                  pl.BlockSpec((GDN_CONV, c3), lambda i: (0, 0)), pl.BlockSpec((WIDTH, WIDTH), lambda i: (0, 0))],
        out_specs=[out, out, out], out_shape=[shp, shp, shp],
        compiler_params=_params("arbitrary"),
    )(xg, xg, conv_w, seg)


def _gdn_pre_bwd(xg, conv_w, seg, dqn, dkn, dv):
    t = xg.shape[0]
    c3 = 3 * WIDTH
    tt = _pick(t, (320, 256, 128))
    nt = t // tt

    def body(x_ref, p_ref, w_ref, e_ref, dq_ref, dk_ref, dv_ref, dx_ref, dw_ref, carry_ref):
        step = pl.program_id(0)
        x = x_ref[...].astype(F32)
        e = e_ref[...]
        prev = jnp.where(step == nt - 1, 0.0, _prev8(p_ref))
        y = _causal_conv(x, prev, w_ref, GDN_CONV)
        s = _silu(y)
        q = s[:, :WIDTH]
        k = s[:, WIDTH:2 * WIDTH]
        rq = lax.rsqrt(_segsum(q * q, e) + RMS_EPS)
        rk = lax.rsqrt(_segsum(k * k, e) + RMS_EPS)
        gq = _load_heads(dq_ref) * (HEAD_DIM ** -0.5)
        gk = _load_heads(dk_ref)
        dq = rq * gq - q * (rq * rq * rq) * _segsum(gq * q, e)
        dk = rk * gk - k * (rk * rk * rk) * _segsum(gk * k, e)
        dy = jnp.concatenate([dq, dk, _load_heads(dv_ref)], axis=1) * _silu_grad(y)

        @pl.when(step == 0)
        def _():
            carry_ref[...] = jnp.zeros_like(carry_ref)
            dw_ref[...] = jnp.zeros_like(dw_ref)

        dx = _causal_conv_bwd(x, prev, dy, carry_ref[...], w_ref, dw_ref, GDN_CONV)
        dx_ref[...] = dx.astype(dx_ref.dtype)
        carry_ref[...] = dy[:8]

    rev = lambda i: (nt - 1 - i, 0)
    blk = pl.BlockSpec((HEADS, tt, HEAD_DIM), lambda i: (0, nt - 1 - i, 0))
    return pl.pallas_call(
        body, grid=(nt,), name="gdn_pre_bwd",
        in_specs=[pl.BlockSpec((tt, c3), rev), _prev_spec(tt, c3, lambda i: nt - 1 - i),
                  pl.BlockSpec((GDN_CONV, c3), lambda i: (0, 0)), pl.BlockSpec((WIDTH, WIDTH), lambda i: (0, 0)),
                  blk, blk, blk],
        out_specs=[pl.BlockSpec((tt, c3), rev), pl.BlockSpec((GDN_CONV, c3), lambda i: (0, 0))],
        out_shape=[jax.ShapeDtypeStruct((t, c3), BF16), jax.ShapeDtypeStruct((GDN_CONV, c3), F32)],
        scratch_shapes=[pltpu.VMEM((8, c3), F32)],
        compiler_params=_params("arbitrary"),
    )(xg, xg, conv_w, seg, dqn, dkn, dv)


def _bmm(a, b, ca, cb, precision=None):
    return lax.dot_general(a, b, (((ca,), (cb,)), ((0,), (0,))), precision=precision, preferred_element_type=F32)


def _bf(x):
    return x.astype(BF16)


def _tri_inverse(a, eye):
    x = -a
    tinv = eye + x
    pw = x
    for _ in range(5):
        pb = _bf(pw)
        pw = _bmm(pb, pb, 2, 1)
        tinv = tinv + _bmm(_bf(tinv), _bf(pw), 2, 1)
    resid = eye - _bmm(eye + a, tinv, 2, 1, precision=HI)
    return tinv + _bmm(_bf(tinv), _bf(resid), 2, 1)


def _gdn_intra(q, k, v, bc, gcc, gcr):
    ii = lax.broadcasted_iota(jnp.int32, (CHUNK, CHUNK), 0)
    jj = lax.broadcasted_iota(jnp.int32, (CHUNK, CHUNK), 1)
    tril = (ii >= jj)[None]
    strict = (ii > jj)[None]
    eye = jnp.where(ii == jj, 1.0, 0.0).astype(F32)[None]
    last = (ii == CHUNK - 1)[None]
    dm = jnp.exp(jnp.where(tril, gcc - gcr, NEG))
    gam = jnp.exp(gcc)
    kb = k * bc
    vb = v * bc
    kk = _bmm(_bf(kb), _bf(k), 2, 2)
    a = jnp.where(strict, kk * dm, 0.0)
    tinv = _tri_inverse(a, eye)
    uw = _bmm(tinv, jnp.concatenate([vb, kb * gam], axis=2), 2, 1, precision=HI)
    u, wk = uw[:, :, :HEAD_DIM], uw[:, :, HEAD_DIM:]
    qk = _bmm(_bf(q), _bf(k), 2, 2)
    p = jnp.where(tril, qk * dm, 0.0)
    gl = jnp.sum(jnp.where(last, gcc, 0.0), axis=1, keepdims=True)
    edec = jnp.exp(gl - gcc)
    return dict(tril=tril, strict=strict, dm=dm, gam=gam, kb=kb, kk=kk, a=a, tinv=tinv, u=u, wk=wk, qk=qk, p=p,
                qg=q * gam, kt=k * edec, edec=edec, gaml=jnp.exp(gl), last=last)


def _gate_tiles(sc, gct, nb):
    rows = nb * CHUNK
    cols = lambda lane0: jnp.stack([jnp.broadcast_to(sc[:, lane0 + h:lane0 + h + 1], (rows, HEAD_DIM))
                                    for h in range(HEADS)], axis=0).reshape(HEADS * nb, CHUNK, HEAD_DIM)
    gcr = jnp.stack([jnp.broadcast_to(gct[h:h + 1, n * CHUNK:(n + 1) * CHUNK], (CHUNK, CHUNK))
                     for h in range(HEADS) for n in range(nb)], axis=0)
    return cols(HEADS), cols(2 * HEADS), gcr


def _gdn_fwd(q, k, v, scal, gct, nb=None):
    h, t, dh = q.shape
    nc = t // CHUNK
    nb = nb or _pick(nc, (4, 2))
    bsz = h * nb

    def body(q_ref, k_ref, v_ref, sc_ref, gt_ref, o_ref, s0_ref, state_ref):
        @pl.when(pl.program_id(0) == 0)
        def _():
            state_ref[...] = jnp.zeros_like(state_ref)

        ld = lambda r: r[...].reshape(bsz, CHUNK, dh)
        bc, gcc, gcr = _gate_tiles(sc_ref[...], gt_ref[...], nb)
        z = _gdn_intra(ld(q_ref), ld(k_ref), ld(v_ref), bc, gcc, gcr)
        per = lambda x: x.reshape((h, nb) + x.shape[1:])
        u, wk, p, qg, kt, gaml = (per(z[n]) for n in ("u", "wk", "p", "qg", "kt", "gaml"))
        s = state_ref[...]
        for n in range(nb):
            s0_ref[:, n] = s
            sb = _bf(s)
            vn = u[:, n] - _bmm(_bf(wk[:, n]), sb, 2, 1)
            vnb = _bf(vn)
            o_ref[:, n * CHUNK:(n + 1) * CHUNK, :] = _bmm(_bf(jnp.concatenate([qg[:, n], p[:, n]], axis=2)),
                                                          jnp.concatenate([sb, vnb], axis=1), 2, 1)
            s = s * gaml[:, n] + _bmm(_bf(kt[:, n]), vnb, 1, 1)
        state_ref[...] = s

    blk = pl.BlockSpec((h, nb * CHUNK, dh), lambda i: (0, i, 0))
    return pl.pallas_call(
        body, grid=(nc // nb,), name="gdn_fwd",
        in_specs=[blk] * 3 + [pl.BlockSpec((nb * CHUNK, LANES), lambda i: (i, 0)),
                              pl.BlockSpec((h, nb * CHUNK), lambda i: (0, i))],
        out_specs=[blk, pl.BlockSpec((h, nb, dh, dh), lambda i: (0, i, 0, 0))],
        out_shape=[jax.ShapeDtypeStruct((h, t, dh), F32), jax.ShapeDtypeStruct((h, nc, dh, dh), F32)],
        scratch_shapes=[pltpu.VMEM((h, dh, dh), F32)],
        compiler_params=_params("arbitrary"),
    )(q, k, v, scal, gct)


def _gdn_bwd(q, k, v, scal, gct, s0s, do, nb=None):
    h, t, dh = q.shape
    nc = t // CHUNK
    nb = nb or _pick(nc, (2,))
    bsz = h * nb
    ng = nc // nb
    rows = nb * CHUNK

    def body(q_ref, k_ref, v_ref, sc_ref, gt_ref, s0_ref, do_ref,
             dq_ref, dk_ref, dv_ref, dsc_ref, dgt_ref, ds_ref):
        @pl.when(pl.program_id(0) == 0)
        def _():
            ds_ref[...] = jnp.zeros_like(ds_ref)

        ld = lambda r: r[...].reshape(bsz, CHUNK, dh)
        q, k, v = ld(q_ref), ld(k_ref), ld(v_ref)
        bc, gcc, gcr = _gate_tiles(sc_ref[...], gt_ref[...], nb)
        z = _gdn_intra(q, k, v, bc, gcc, gcr)
        per = lambda x: x.reshape((h, nb) + x.shape[1:])
        u, wk, p, qg, kt, gaml = (per(z[n]) for n in ("u", "wk", "p", "qg", "kt", "gaml"))
        dout = per(ld(do_ref))
        ds = ds_ref[...]
        d_u, d_wk, d_p, d_qg, d_kt, d_gaml = ([None] * nb for _ in range(6))
        for n in reversed(range(nb)):
            s0 = s0_ref[:, n]
            s0b, dsb, dob = _bf(s0), _bf(ds), _bf(dout[:, n])
            wkb, qgb = _bf(wk[:, n]), _bf(qg[:, n])
            vn = u[:, n] - _bmm(wkb, s0b, 2, 1)
            dvn = _bmm(_bf(p[:, n]), dob, 1, 1) + _bmm(_bf(kt[:, n]), dsb, 2, 1)
            dvnb = _bf(dvn)
            d_u[n] = dvn
            vnb = _bf(vn)
            dpq = _bmm(dob, jnp.concatenate([vnb, s0b], axis=1), 2, 2)
            d_p[n], d_qg[n] = dpq[:, :, :CHUNK], dpq[:, :, CHUNK:]
            d_kt[n] = _bmm(vnb, dsb, 2, 2)
            d_gaml[n] = jnp.sum(s0 * ds, axis=1, keepdims=True)
            d_wk[n] = -_bmm(dvnb, s0b, 2, 2)
            ds = gaml[:, n] * ds + _bmm(jnp.concatenate([qgb, -wkb], axis=1), jnp.concatenate([dob, dvnb], axis=1), 1, 1)
        ds_ref[...] = ds

        flat = lambda xs: jnp.stack(xs, axis=1).reshape((bsz,) + xs[0].shape[1:])
        d_u, d_wk, d_p, d_qg, d_kt, d_gaml = (flat(x) for x in (d_u, d_wk, d_p, d_qg, d_kt, d_gaml))
        tinv, gam, kb, dm = z["tinv"], z["gam"], z["kb"], z["dm"]
        dr = _bmm(tinv, jnp.concatenate([d_u, d_wk], axis=2), 1, 1, precision=HI)
        drv, drk = dr[:, :, :HEAD_DIM], dr[:, :, HEAD_DIM:]
        da = -_bmm(_bf(dr), _bf(jnp.concatenate([z["u"], z["wk"]], axis=2)), 2, 2)
        da = jnp.where(z["strict"], da, 0.0)
        d_p = jnp.where(z["tril"], d_p, 0.0)
        dkk = _bf(da * dm)
        dqk = _bf(d_p * dm)
        dkb = _bmm(dkk, _bf(k), 2, 1) + drk * gam
        dk = (_bmm(jnp.concatenate([dkk, dqk], axis=1), _bf(jnp.concatenate([kb, q], axis=1)), 1, 1)
              + dkb * bc + d_kt * z["edec"])
        dq = _bmm(dqk, _bf(k), 2, 1) + d_qg * gam
        mm = da * z["a"] + d_p * z["p"]
        dkt_kt = d_kt * z["kt"]
        dgl = jnp.sum(dkt_kt, axis=1, keepdims=True) + d_gaml * z["gaml"]
        dgc = mm + d_qg * z["qg"] + drk * kb * gam - dkt_kt + jnp.where(z["last"], dgl, 0.0)
        dq_ref[...] = dq.reshape(h, rows, dh)
        dk_ref[...] = dk.reshape(h, rows, dh)
        dv_ref[...] = (drv * bc).reshape(h, rows, dh)
        dbeta = (dkb * k + drv * v).reshape(h, rows, dh)
        dgc = dgc.reshape(h, rows, dh)
        lane = lax.broadcasted_iota(jnp.int32, (rows, LANES), 1)
        dsc = jnp.zeros((rows, LANES), F32)
        for hh in range(h):
            dsc = jnp.where(lane == HEADS + hh, jnp.sum(dbeta[hh], axis=1, keepdims=True), dsc)
            dsc = jnp.where(lane == 2 * HEADS + hh, jnp.sum(dgc[hh], axis=1, keepdims=True), dsc)
        dsc_ref[...] = dsc
        dgr = -jnp.sum(mm, axis=1, keepdims=True)
        for hh in range(h):
            for n in range(nb):
                dgt_ref[hh:hh + 1, n * CHUNK:(n + 1) * CHUNK] = dgr[hh * nb + n]

    blk = pl.BlockSpec((h, rows, dh), lambda i: (0, ng - 1 - i, 0))
    shp = jax.ShapeDtypeStruct((h, t, dh), F32)
    sc_spec = pl.BlockSpec((rows, LANES), lambda i: (ng - 1 - i, 0))
    gt_spec = pl.BlockSpec((h, rows), lambda i: (0, ng - 1 - i))
    return pl.pallas_call(
        body, grid=(ng,), name="gdn_bwd",
        in_specs=[blk] * 3 + [sc_spec, gt_spec, pl.BlockSpec((h, nb, dh, dh), lambda i: (0, ng - 1 - i, 0, 0)), blk],
        out_specs=[blk] * 3 + [sc_spec, gt_spec],
        out_shape=[shp] * 3 + [jax.ShapeDtypeStruct((t, LANES), F32), jax.ShapeDtypeStruct((h, t), F32)],
        scratch_shapes=[pltpu.VMEM((h, dh, dh), F32)],
        compiler_params=_params("arbitrary"),
    )(q, k, v, scal, gct, s0s, do)


def _gdn_post_fwd(o, xg, gain, seg):
    t = o.shape[1]
    tt = _pick(t, (320, 256, 128))

    def body(o_ref, z_ref, g_ref, e_ref, y_ref):
        x = _load_heads(o_ref)
        r = lax.rsqrt(_segsum(x * x, e_ref[...]) * (1.0 / HEAD_DIM) + RMS_EPS)
        y_ref[...] = (x * r * g_ref[...] * _silu(z_ref[...].astype(F32))).astype(y_ref.dtype)

    return pl.pallas_call(
        body, grid=(t // tt,), name="gdn_post_fwd",
        in_specs=[pl.BlockSpec((HEADS, tt, HEAD_DIM), lambda i: (0, i, 0)), pl.BlockSpec((tt, WIDTH), lambda i: (i, 3)),
                  pl.BlockSpec((1, WIDTH), lambda i: (0, 0)), pl.BlockSpec((WIDTH, WIDTH), lambda i: (0, 0))],
        out_specs=pl.BlockSpec((tt, WIDTH), lambda i: (i, 0)),
        out_shape=jax.ShapeDtypeStruct((t, WIDTH), BF16),
        compiler_params=_params("arbitrary"),
    )(o, xg, gain, seg)


def _gdn_post_bwd(o, xg, gain, seg, dy):
    t = o.shape[1]
    tt = _pick(t, (320, 256, 128))

    def body(o_ref, z_ref, g_ref, e_ref, dy_ref, do_ref, dz_ref, dg_ref):
        x = _load_heads(o_ref)
        zz = z_ref[...].astype(F32)
        e = e_ref[...]
        gain_v = g_ref[...]
        d = dy_ref[...]
        r = lax.rsqrt(_segsum(x * x, e) * (1.0 / HEAD_DIM) + RMS_EPS)
        xr = x * r
        don = d * _silu(zz)
        dz_ref[...] = (d * xr * gain_v * _silu_grad(zz)).astype(dz_ref.dtype)
        gy = don * gain_v
        _store_heads(do_ref, r * gy - xr * (r * r) * (_segsum(gy * x, e) * (1.0 / HEAD_DIM)))

        @pl.when(pl.program_id(0) == 0)
        def _():
            dg_ref[...] = jnp.zeros_like(dg_ref)

        dg_ref[...] += jnp.sum(don * xr, axis=0, keepdims=True)

    row = pl.BlockSpec((tt, WIDTH), lambda i: (i, 0))
    vec = pl.BlockSpec((1, WIDTH), lambda i: (0, 0))
    hm = pl.BlockSpec((HEADS, tt, HEAD_DIM), lambda i: (0, i, 0))
    return pl.pallas_call(
        body, grid=(t // tt,), name="gdn_post_bwd",
        in_specs=[hm, pl.BlockSpec((tt, WIDTH), lambda i: (i, 3)), vec,
                  pl.BlockSpec((WIDTH, WIDTH), lambda i: (0, 0)), row],
        out_specs=[hm, row, vec],
        out_shape=[jax.ShapeDtypeStruct((HEADS, t, HEAD_DIM), F32), jax.ShapeDtypeStruct((t, WIDTH), BF16),
                   jax.ShapeDtypeStruct((1, WIDTH), F32)],
        compiler_params=_params("arbitrary"),
    )(o, xg, gain, seg, dy)


def _mix_fwd(yf, yg, gates, bias):
    t, d = yf.shape
    tt = _pick(t, (320, 256, 128))

    def body(yf_ref, yg_ref, g1_ref, g2_ref, b1_ref, b2_ref, o_ref):
        g1 = jax.nn.sigmoid(g1_ref[...].astype(F32) + b1_ref[...])
        g2 = jax.nn.sigmoid(g2_ref[...].astype(F32) + b2_ref[...])
        o_ref[...] = (g1 * yf_ref[...] + g2 * yg_ref[...]).astype(o_ref.dtype)

    row = pl.BlockSpec((tt, d), lambda i: (i, 0))
    return pl.pallas_call(
        body, grid=(t // tt,), name="mix_fwd",
        in_specs=[row, row, row, pl.BlockSpec((tt, d), lambda i: (i, 1)),
                  pl.BlockSpec((1, d), lambda i: (0, 0)), pl.BlockSpec((1, d), lambda i: (0, 1))],
        out_specs=row, out_shape=jax.ShapeDtypeStruct((t, d), BF16),
        compiler_params=_params("arbitrary"),
    )(yf, yg, gates, gates, bias, bias)


def _mix_bwd(dmix, yf, yg, gates, bias):
    t, d = yf.shape
    tt = _pick(t, (320, 256, 128))

    def body(dm_ref, yf_ref, yg_ref, g1_ref, g2_ref, b1_ref, b2_ref, dyf_ref, dyg_ref, dg_ref, db_ref):
        dm = dm_ref[...]
        g1 = jax.nn.sigmoid(g1_ref[...].astype(F32) + b1_ref[...])
        g2 = jax.nn.sigmoid(g2_ref[...].astype(F32) + b2_ref[...])
        dyf_ref[...] = (dm * g1).astype(BF16)
        dyg_ref[...] = (dm * g2).astype(BF16)
        dgate = jnp.concatenate([dm * yf_ref[...] * g1 * (1.0 - g1), dm * yg_ref[...] * g2 * (1.0 - g2)], axis=1)
        dg_ref[...] = dgate.astype(BF16)

        @pl.when(pl.program_id(0) == 0)
        def _():
            db_ref[...] = jnp.zeros_like(db_ref)

        db_ref[...] += jnp.sum(dgate, axis=0, keepdims=True)

    row = pl.BlockSpec((tt, d), lambda i: (i, 0))
    wide = pl.BlockSpec((tt, 2 * d), lambda i: (i, 0))
    return pl.pallas_call(
        body, grid=(t // tt,), name="mix_bwd",
        in_specs=[row, row, row, row, pl.BlockSpec((tt, d), lambda i: (i, 1)),
                  pl.BlockSpec((1, d), lambda i: (0, 0)), pl.BlockSpec((1, d), lambda i: (0, 1))],
        out_specs=[row, row, wide, pl.BlockSpec((1, 2 * d), lambda i: (0, 0))],
        out_shape=[jax.ShapeDtypeStruct((t, d), BF16), jax.ShapeDtypeStruct((t, d), BF16),
                   jax.ShapeDtypeStruct((t, 2 * d), BF16), jax.ShapeDtypeStruct((1, 2 * d), F32)],
        compiler_params=_params("arbitrary"),
    )(dmix, yf, yg, gates, gates, bias, bias)


def _ffn_act_fwd(up, conv_w, conv_b):
    t, c = up.shape
    tt = 128

    def body(x_ref, p_ref, w_ref, b_ref, o_ref):
        first = pl.program_id(0) == 0

        def conv(cols):
            prev = jnp.where(first, 0.0, _prev8(p_ref, cols))
            return _causal_conv(x_ref[:, cols].astype(F32), prev, w_ref, FFN_CONV, cols) + b_ref[:, cols]

        for lo in range(0, D_FF, FFN_LANES):
            gate = conv(slice(lo, lo + FFN_LANES))
            val = conv(slice(D_FF + lo, D_FF + lo + FFN_LANES))
            o_ref[:, lo:lo + FFN_LANES] = (_silu(gate) * val).astype(o_ref.dtype)

    return pl.pallas_call(
        body, grid=(t // tt,), name="ffn_act_fwd",
        in_specs=[pl.BlockSpec((tt, c), lambda i: (i, 0)), _prev_spec(tt, c),
                  pl.BlockSpec((FFN_CONV, c), lambda i: (0, 0)), pl.BlockSpec((1, c), lambda i: (0, 0))],
        out_specs=pl.BlockSpec((tt, D_FF), lambda i: (i, 0)),
        out_shape=jax.ShapeDtypeStruct((t, D_FF), BF16),
        compiler_params=_params("arbitrary"),
    )(up, up, conv_w, conv_b)


def _ffn_act_bwd(up, conv_w, conv_b, dact):
    t, c = up.shape
    tt = 128
    nt = t // tt

    def body(x_ref, p_ref, w_ref, b_ref, da_ref, dx_ref, dw_ref, db_ref, carry_ref):
        step = pl.program_id(0)

        @pl.when(step == 0)
        def _():
            carry_ref[...] = jnp.zeros_like(carry_ref)
            dw_ref[...] = jnp.zeros_like(dw_ref)
            db_ref[...] = jnp.zeros_like(db_ref)

        def conv(cols):
            x = x_ref[:, cols].astype(F32)
            prev = jnp.where(step == nt - 1, 0.0, _prev8(p_ref, cols))
            return x, prev, _causal_conv(x, prev, w_ref, FFN_CONV, cols) + b_ref[:, cols]

        def back(cols, x, prev, du):
            dx = _causal_conv_bwd(x, prev, du, carry_ref[:, cols], w_ref, dw_ref, FFN_CONV, cols)
            dx_ref[:, cols] = dx.astype(dx_ref.dtype)
            db_ref[:, cols] += jnp.sum(du, axis=0, keepdims=True)
            carry_ref[:, cols] = du[:8]

        for lo in range(0, D_FF, FFN_LANES):
            gcols, vcols = slice(lo, lo + FFN_LANES), slice(D_FF + lo, D_FF + lo + FFN_LANES)
            xg, pg, gate = conv(gcols)
            xv, pv, val = conv(vcols)
            da = da_ref[:, gcols]
            back(gcols, xg, pg, da * val * _silu_grad(gate))
            back(vcols, xv, pv, da * _silu(gate))

    rev = lambda i: (nt - 1 - i, 0)
    return pl.pallas_call(
        body, grid=(nt,), name="ffn_act_bwd",
        in_specs=[pl.BlockSpec((tt, c), rev),
                  _prev_spec(tt, c, lambda i: nt - 1 - i),
                  pl.BlockSpec((FFN_CONV, c), lambda i: (0, 0)), pl.BlockSpec((1, c), lambda i: (0, 0)),
                  pl.BlockSpec((tt, D_FF), rev)],
        out_specs=[pl.BlockSpec((tt, c), rev), pl.BlockSpec((FFN_CONV, c), lambda i: (0, 0)),
                   pl.BlockSpec((1, c), lambda i: (0, 0))],
        out_shape=[jax.ShapeDtypeStruct((t, c), BF16), jax.ShapeDtypeStruct((FFN_CONV, c), F32),
                   jax.ShapeDtypeStruct((1, c), F32)],
        scratch_shapes=[pltpu.VMEM((8, c), F32)],
        compiler_params=_params("arbitrary"),
    )(up, up, conv_w, conv_b, dact)


def _final_loss(h2, target, gain, seq):
    t, d = h2.shape
    tr = _pick(t, (320, 256, 128))

    def body(h_ref, t_ref, g_ref, loss_ref, dh_ref, dhb_ref, dg_ref):
        i = pl.program_id(0)
        x = h_ref[...]
        gain_v = g_ref[...]
        r = lax.rsqrt(jnp.mean(x * x, axis=-1, keepdims=True) + RMS_EPS)
        xr = x * r
        rows = i * tr + lax.broadcasted_iota(jnp.int32, (tr, 1), 0)
        real = (rows >= N_META) & (rows < N_META + seq)
        err = jnp.where(real, xr * gain_v - t_ref[...], 0.0)
        dy = err * (1.0 / d)
        gy = dy * gain_v
        dh = r * (gy - xr * jnp.mean(gy * xr, axis=-1, keepdims=True))
        dh_ref[...] = dh
        dhb_ref[...] = dh.astype(BF16)

        @pl.when(i == 0)
        def _():
            loss_ref[...] = jnp.zeros_like(loss_ref)
            dg_ref[...] = jnp.zeros_like(dg_ref)

        part = jnp.sum(jnp.sum(err * err, axis=-1, keepdims=True), axis=0, keepdims=True)
        loss_ref[...] += jnp.broadcast_to(part * (0.5 / d), loss_ref.shape)
        dg_ref[...] += jnp.sum(dy * xr, axis=0, keepdims=True)

    row = pl.BlockSpec((tr, d), lambda i: (i, 0))
    vec = pl.BlockSpec((1, d), lambda i: (0, 0))
    return pl.pallas_call(
        body, grid=(t // tr,), name="final_loss",
        in_specs=[row, row, vec],
        out_specs=[pl.BlockSpec((1, LANES), lambda i: (0, 0)), row, row, vec],
        out_shape=[jax.ShapeDtypeStruct((1, LANES), F32), jax.ShapeDtypeStruct((t, d), F32),
                   jax.ShapeDtypeStruct((t, d), BF16), jax.ShapeDtypeStruct((1, d), F32)],
        compiler_params=_params("arbitrary"),
    )(h2, target, gain)


ADAM_TILE_BYTES = 1 << 20


def _adamw(w, m, v, grecv, name):
    r, cols = w.shape
    tr = r
    if r * cols * 4 > ADAM_TILE_BYTES:
        tr = max(d for d in range(8, r + 1, 8) if r % d == 0 and d * cols * 4 <= ADAM_TILE_BYTES)

    def body(w_ref, m_ref, v_ref, g_ref, go_ref, d_ref, mo_ref, vo_ref):
        g = g_ref[0].astype(F32)
        for s in range(1, N_DEV):
            g = g + g_ref[s].astype(F32)
        wv = w_ref[...]
        mn = ADAM_B1 * m_ref[...] + (1.0 - ADAM_B1) * g
        vn = ADAM_B2 * v_ref[...] + (1.0 - ADAM_B2) * (g * g)
        m_hat = mn / (1.0 - ADAM_B1 ** ADAM_STEP)
        v_hat = vn / (1.0 - ADAM_B2 ** ADAM_STEP)
        go_ref[...] = g
        d_ref[...] = -ADAM_LR * (m_hat / (jnp.sqrt(v_hat) + ADAM_EPS) + ADAM_WD * wv)
        mo_ref[...] = mn
        vo_ref[...] = vn

    row = pl.BlockSpec((tr, cols), lambda i: (i, 0))
    shp = jax.ShapeDtypeStruct((r, cols), F32)
    return pl.pallas_call(
        body, grid=(r // tr,), name=name,
        in_specs=[row, row, row, pl.BlockSpec((N_DEV, tr, cols), lambda i: (0, i, 0))],
        out_specs=[row] * 4, out_shape=[shp] * 4,
        compiler_params=_params("parallel"),
    )(w, m, v, grecv)


def _mesh_pos():
    return lax.axis_index("x"), lax.axis_index("y"), lax.axis_index("c")


def _all_gather(shards):
    n = len(shards)

    def body(*refs):
        x_refs, out_refs = refs[:n], refs[n:2 * n]
        send_sems, recv_sems, local_sems = refs[2 * n:]
        x, y, c = _mesh_pos()
        me, sibling = (x, y, c), (x, y, 1 - c)
        chips = [(1 - x, y), (x, 1 - y), (1 - x, 1 - y)]

        def slot(a, px, py, pc):
            return out_refs[a].at[4 * px + 2 * py + pc]

        def copy(a, kk, block, to, src=None):
            return pltpu.make_async_remote_copy(
                src_ref=slot(a, *block) if src is None else src, dst_ref=slot(a, *block),
                send_sem=send_sems.at[7 * a + kk], recv_sem=recv_sems.at[7 * a + kk],
                device_id=to, device_id_type=MESH_ID)

        mine = [pltpu.make_async_copy(x_refs[a], slot(a, *me), local_sems.at[a]) for a in range(n)]
        first = []
        for a in range(n):
            first.append(copy(a, 0, me, sibling, src=x_refs[a]))
            first += [copy(a, 1 + j, me, (*chip, c), src=x_refs[a]) for j, chip in enumerate(chips)]
        for cp in mine + first:
            cp.start()
        passed = []
        for j, chip in enumerate(chips):
            for a in range(n):
                copy(a, 1 + j, (*chip, c), me).wait_recv()
                passed.append(copy(a, 4 + j, (*chip, c), sibling))
                passed[-1].start()
        for a in range(n):
            copy(a, 0, sibling, me).wait_recv()
        for j, chip in enumerate(chips):
            for a in range(n):
                copy(a, 4 + j, (*chip, 1 - c), me).wait_recv()
        for cp in first + passed:
            cp.wait_send()
        for cp in mine:
            cp.wait()

    hbm = pl.BlockSpec(memory_space=pl.ANY)
    return pl.pallas_call(
        body, name="weight_all_gather", in_specs=[hbm] * n, out_specs=[hbm] * n,
        out_shape=[jax.ShapeDtypeStruct((N_DEV,) + s.shape, s.dtype) for s in shards],
        scratch_shapes=[pltpu.SemaphoreType.DMA((7 * n,)), pltpu.SemaphoreType.DMA((7 * n,)),
                        pltpu.SemaphoreType.DMA((n,))],
    )(*shards)


def _grad_exchange(blocks, small):
    n = len(blocks)

    def body(*refs):
        src_refs, dst_refs = refs[:n + 1], refs[n + 1:2 * n + 2]
        send_sems, recv_sems, local_sems = refs[2 * n + 2:]
        x, y, c = _mesh_pos()
        me = 4 * x + 2 * y + c
        copies = []
        for kk in range(1, N_DEV):
            px = 1 - x if kk & 4 else x
            py = 1 - y if kk & 2 else y
            pc = 1 - c if kk & 1 else c
            peer = 4 * px + 2 * py + pc
            for a in range(n + 1):
                copies.append(pltpu.make_async_remote_copy(
                    src_ref=src_refs[a].at[peer] if a < n else src_refs[a], dst_ref=dst_refs[a].at[me],
                    send_sem=send_sems.at[7 * a + kk - 1], recv_sem=recv_sems.at[7 * a + kk - 1],
                    device_id=(px, py, pc), device_id_type=MESH_ID))
        own = [pltpu.make_async_copy(src_refs[a].at[me] if a < n else src_refs[a], dst_refs[a].at[me],
                                     local_sems.at[a]) for a in range(n + 1)]
        for cp in own + copies:
            cp.start()
        for cp in copies + own:
            cp.wait()

    hbm = pl.BlockSpec(memory_space=pl.ANY)
    return pl.pallas_call(
        body, name="grad_exchange", in_specs=[hbm] * (n + 1), out_specs=[hbm] * (n + 1),
        out_shape=[jax.ShapeDtypeStruct(b.shape, b.dtype) for b in blocks]
        + [jax.ShapeDtypeStruct((N_DEV,) + small.shape, small.dtype)],
        scratch_shapes=[pltpu.SemaphoreType.DMA((7 * (n + 1),)), pltpu.SemaphoreType.DMA((7 * (n + 1),)),
                        pltpu.SemaphoreType.DMA((n + 1,))],
    )(*blocks, small)


def _exchange_copies(src_refs, land_refs, send_sems, recv_sems):
    x, y, c = _mesh_pos()
    me = 4 * x + 2 * y + c
    copies = []
    for kk in range(1, N_DEV):
        px = 1 - x if kk & 4 else x
        py = 1 - y if kk & 2 else y
        pc = 1 - c if kk & 1 else c
        for a, (src, land) in enumerate(zip(src_refs, land_refs)):
            copies.append(pltpu.make_async_remote_copy(
                src_ref=src.at[4 * px + 2 * py + pc], dst_ref=land.at[me],
                send_sem=send_sems.at[7 * a + kk - 1], recv_sem=recv_sems.at[7 * a + kk - 1],
                device_id=(px, py, pc), device_id_type=MESH_ID))
    return copies


def _gather_copies(src_refs, land_refs, send_sems, recv_sems):
    x, y, c = _mesh_pos()
    me = 4 * x + 2 * y + c
    copies = []
    for kk in range(1, N_DEV):
        px = 1 - x if kk & 4 else x
        py = 1 - y if kk & 2 else y
        pc = 1 - c if kk & 1 else c
        for a, (src, land) in enumerate(zip(src_refs, land_refs)):
            copies.append(pltpu.make_async_remote_copy(
                src_ref=src, dst_ref=land.at[me],
                send_sem=send_sems.at[7 * a + kk - 1], recv_sem=recv_sems.at[7 * a + kk - 1],
                device_id=(px, py, pc), device_id_type=MESH_ID))
    return copies


_HBM = pl.BlockSpec(memory_space=pltpu.HBM)
_SEM = pl.BlockSpec(memory_space=pltpu.SEMAPHORE)
_DATAFLOW = pltpu.SideEffectType.DATAFLOW_SIDE_EFFECTING


def _split_start(name, make_copies, sources, land_shapes):
    n = len(sources)

    def body(*refs):
        src_refs, land_refs, send_sems, recv_sems = refs[:n], refs[n:2 * n], refs[2 * n], refs[2 * n + 1]
        for cp in make_copies(src_refs, land_refs, send_sems, recv_sems):
            cp.start()
        token = refs[-1]
        token[...] = jnp.zeros_like(token)

    in_hbm = lambda a: pltpu.with_memory_space_constraint(a, pltpu.HBM)
    hbm_shapes = [pltpu.HBM(s.shape, s.dtype) for s in list(sources) + list(land_shapes)]
    outs = pl.pallas_call(
        body, name=name, in_specs=[_HBM] * (2 * n),
        out_shape=(pltpu.SemaphoreType.DMA((7 * n,)), pltpu.SemaphoreType.DMA((7 * n,)), *hbm_shapes,
                   jax.ShapeDtypeStruct((8, LANES), F32)),
        out_specs=(_SEM, _SEM, *[_HBM] * (2 * n), pl.BlockSpec(memory_space=pltpu.VMEM)),
        input_output_aliases={a: 2 + a for a in range(2 * n)},
        compiler_params=pltpu.CompilerParams(has_side_effects=_DATAFLOW),
    )(*[in_hbm(s) for s in sources], *[in_hbm(lax.empty(s.shape, s.dtype)) for s in land_shapes])
    return outs[0], outs[1], outs[2:2 + n], outs[2 + n:2 + 2 * n], outs[-1]


def _split_wait(name, make_copies, send_sems, recv_sems, src_thru, land_thru, after):
    n = len(src_thru)

    def body(*refs):
        src_refs, land_refs, send_sems, recv_sems = refs[:n], refs[n:2 * n], refs[2 * n], refs[2 * n + 1]
        for cp in make_copies(src_refs, land_refs, send_sems, recv_sems):
            cp.wait_send()
            cp.wait_recv()

    outs = pl.pallas_call(
        body, name=name,
        in_specs=[_HBM] * (2 * n) + [_SEM, _SEM, pl.BlockSpec(memory_space=pl.ANY)],
        out_shape=tuple(pltpu.HBM(b.shape, b.dtype) for b in list(src_thru) + list(land_thru)),
        out_specs=[_HBM] * (2 * n), input_output_aliases={a: a for a in range(2 * n)},
        compiler_params=pltpu.CompilerParams(has_side_effects=_DATAFLOW),
    )(*src_thru, *land_thru, send_sems, recv_sems, after)
    return outs[:n], outs[n:]


def _exchange_start(blocks):
    return _split_start("grad_exchange_start", _exchange_copies, blocks, blocks)


def _exchange_wait(send_sems, recv_sems, src_thru, land_thru, after):
    return _split_wait("grad_exchange_wait", _exchange_copies, send_sems, recv_sems, src_thru, land_thru, after)


def _gather_start(shards):
    lands = [jax.ShapeDtypeStruct((N_DEV,) + s.shape, s.dtype) for s in shards]
    return _split_start("weight_gather_start", _gather_copies, shards, lands)


def _gather_wait(send_sems, recv_sems, src_thru, land_thru, after):
    return _split_wait("weight_gather_wait", _gather_copies, send_sems, recv_sems, src_thru, land_thru, after)


def _pad_flat(parts, rows):
    flat = jnp.concatenate([p.reshape(-1) for p in parts])
    return jnp.pad(flat, (0, rows * LANES - flat.shape[0])).reshape(rows, LANES)


def _rows_for(n_elems, mult=1024):
    rows = -(-n_elems // LANES)
    return -(-rows // mult) * mult


SHARDED = ("meta_tokens", "w_in", "gdn_conv_w", "w_branch_fox", "w_branch_gdn", "w_out", "ffn_w_up", "ffn_conv_w",
           "ffn_w_down")
MATMUL = ("w_in", "w_branch_fox", "w_branch_gdn", "w_out", "ffn_w_up", "ffn_w_down")
EXACT = ("meta_tokens", "gdn_conv_w", "ffn_conv_w")
REPLICATED = ("fgt_bias", "gdn_a_log", "gdn_dt_bias", "gdn_norm_w", "gate_bias", "norm_mix_w", "norm_ffn_w",
              "ffn_conv_b", "norm_final_w")
WEIGHTS = ("meta_tokens", "w_in", "fgt_bias", "gdn_conv_w", "gdn_a_log", "gdn_dt_bias", "gdn_norm_w", "gate_bias",
           "w_branch_fox", "w_branch_gdn", "w_out", "norm_mix_w", "norm_ffn_w", "ffn_w_up", "ffn_conv_w",
           "ffn_conv_b", "ffn_w_down", "norm_final_w")


def _unpack(buf, shapes):
    flat = buf.reshape(-1)
    out, off = [], 0
    for s in shapes:
        n = int(np.prod(s))
        out.append(flat[off:off + n].reshape(s))
        off += n
    return out


def _unpack_gathered(buf, shapes):
    flat = buf.reshape(N_DEV, -1)
    out, off = [], 0
    for s in shapes:
        n = int(np.prod(s))
        out.append(flat[:, off:off + n].reshape((N_DEV,) + tuple(s)))
        off += n
    return out


def _cat_cols(g):
    return g.transpose(1, 0, 2).reshape(g.shape[1], -1)


def _col_blocks(full, width):
    return full.reshape(full.shape[0], N_DEV, width).transpose(1, 0, 2)


def _local_step(x, target, w, early=None, late_weights=None):
    seq = x.shape[0]
    t = _padded_tokens(seq)
    pad = t - N_META - seq
    seg = _seg_matrix()
    zrows = jnp.zeros((pad, D_MODEL), F32)
    h0 = jnp.concatenate([w["meta_tokens"], x, zrows], axis=0)
    tgt = jnp.concatenate([jnp.zeros((N_META, D_MODEL), F32), target, zrows], axis=0)

    w_in = w["w_in"]
    o_f, o_g, o_z, o_b, o_a, o_gate = 1536, 1544, 3080, 3592, 3600, 3608
    w_small = jnp.concatenate([w_in[:, o_f:o_f + 8], w_in[:, o_b:o_b + 8], w_in[:, o_a:o_a + 8],
                               jnp.zeros((D_MODEL, LANES - 24), BF16)], axis=1)
    w_r = jnp.concatenate([w_in[:, :1536], w_in[:, o_g:o_z], w_in[:, o_z:o_b], w_in[:, o_gate:], w_small], axis=1)

    a1 = _rmsnorm_fwd(h0, w["norm_mix_w"])
    fq = _mm(a1, w_r[:, :1536], BF16, "proj_fox")
    xg = _mm(a1, w_r[:, 1536:3584], BF16, "proj_gdn")
    gt = _mm(a1, w_r[:, 3584:5632], BF16, "proj_gates")
    sm = _mm(a1, w_r[:, 5632:], F32, "proj_small")

    lanes_pad = lambda a, lo: jnp.pad(a, ((0, 0), (lo, LANES - lo - a.shape[1])))
    neg_exp_a = -jnp.exp(w["gdn_a_log"])
    pbias = lanes_pad(w["fgt_bias"], 0) + lanes_pad(w["gdn_dt_bias"], 2 * HEADS)
    if late_weights is not None:
        pbias = pbias + late_weights[0][0, 0]
    pscale = lanes_pad(neg_exp_a, 2 * HEADS)
    scal = _gate_fwd(sm, pbias, pscale)
    gct = scal[:, 2 * HEADS:3 * HEADS].T

    qa, ka, va, kat, vat = _fox_prep(fq, scal)
    oa, qb, qbt = _fox_fwd(qa, ka, vat)
    o_fox = _fox_post(oa)

    qh, kh, vh = _gdn_pre_fwd(xg, w["gdn_conv_w"], seg)
    og, s0s = _gdn_fwd(qh, kh, vh, scal, gct)
    norm_w = jnp.tile(w["gdn_norm_w"], (1, HEADS))
    ogn = _gdn_post_fwd(og, xg, norm_w, seg)

    if late_weights is not None:
        w = {**w, **late_weights[1](ogn)}
    yf = _mm(o_fox, w["w_branch_fox"], F32, "branch_fox")
    yg = _mm(ogn, w["w_branch_gdn"], F32, "branch_gdn")
    mix = _mix_fwd(yf, yg, gt, w["gate_bias"])
    h1 = _mm(mix, w["w_out"], F32, "out_proj", res=h0)
    a2 = _rmsnorm_fwd(h1, w["norm_ffn_w"])
    up = _mm(a2, w["ffn_w_up"], BF16, "ffn_up")
    act = _ffn_act_fwd(up, w["ffn_conv_w"], w["ffn_conv_b"])
    h2 = _mm(act, w["ffn_w_down"], F32, "ffn_down", res=h1)
    loss, dh2, dh2b, g_final = _final_loss(h2, tgt, w["norm_final_w"].reshape(1, D_MODEL), seq)

    grads = {"norm_final_w": g_final.reshape(D_MODEL)}
    grads["ffn_w_down"] = _mm_tn(act, dh2b, "wgrad_ffn_down")
    dact = _mm(dh2b, w["ffn_w_down"].T, F32, "dgrad_ffn_down")
    dup, g_cw, g_cb = _ffn_act_bwd(up, w["ffn_conv_w"], w["ffn_conv_b"], dact)
    grads["ffn_conv_w"], grads["ffn_conv_b"] = g_cw, g_cb
    grads["ffn_w_up"] = _mm_tn(a2, dup, "wgrad_ffn_up")
    da2 = _mm(dup, w["ffn_w_up"].T, F32, "dgrad_ffn_up")
    dh1, dh1b, grads["norm_ffn_w"] = _rmsnorm_bwd(h1, da2, w["norm_ffn_w"], dh2)
    grads["w_out"] = _mm_tn(mix, dh1b, "wgrad_out")
    dmix = _mm(dh1b, w["w_out"].T, F32, "dgrad_out")
    dyf, dyg, dgt, grads["gate_bias"] = _mix_bwd(dmix, yf, yg, gt, w["gate_bias"])
    grads["w_branch_fox"] = _mm_tn(o_fox, dyf, "wgrad_branch_fox")
    grads["w_branch_gdn"] = _mm_tn(ogn, dyg, "wgrad_branch_gdn")
    do_fox = _mm(dyf, w["w_branch_fox"].T, F32, "dgrad_branch_fox")
    dogn = _mm(dyg, w["w_branch_gdn"].T, F32, "dgrad_branch_gdn")

    dog, dz, g_nw = _gdn_post_bwd(og, xg, norm_w, seg, dogn)
    grads["gdn_norm_w"] = g_nw.reshape(HEADS, HEAD_DIM).sum(axis=0)[None]
    dqh, dkh, dvh, dscal_g, dgct = _gdn_bwd(qh, kh, vh, scal, gct, s0s, dog)
    dxg, grads["gdn_conv_w"] = _gdn_pre_bwd(xg, w["gdn_conv_w"], seg, dqh, dkh, dvh)

    doa, doat = _fox_bwd_prep(do_fox, oa)
    dfq, dscal_c = _fox_bwd_post(*_fox_bwd(qb, qbt, ka, kat, va, doa, doat))

    dscal = dscal_c + dscal_g + lanes_pad(dgct.T, 2 * HEADS)
    dsm, dpb, dps = _gate_bwd(sm, pbias, pscale, dscal)
    grads["fgt_bias"] = dpb[:, :HEADS]
    grads["gdn_dt_bias"] = dpb[:, 2 * HEADS:3 * HEADS]
    grads["gdn_a_log"] = dps[:, 2 * HEADS:3 * HEADS] * neg_exp_a

    dproj = jnp.concatenate([dfq, dxg, dz, dgt, dsm], axis=1)
    g_r = _mm_tn(a1, dproj, "wgrad_in")
    grads["w_in"] = jnp.concatenate([g_r[:, :1536], g_r[:, 5632:5640], g_r[:, 1536:3072], g_r[:, 3072:3584],
                                     g_r[:, 5640:5648], g_r[:, 5648:5656], g_r[:, 3584:5632]], axis=1)
    token, handle = early(grads) if early is not None else (jnp.zeros((8, LANES), F32), None)
    w_rt = w_r.T + token[0, 0].astype(BF16)
    da1 = _mm(dproj, w_rt, F32, "dgrad_in")
    dh0, _, grads["norm_mix_w"] = _rmsnorm_bwd(h0, da1, w["norm_mix_w"], dh1)
    grads["meta_tokens"] = dh0[:N_META]
    return loss, dh0[N_META:N_META + seq], grads, handle


def _shard_pieces(arrs):
    return [arrs[n][0] if arrs[n].ndim == 3 else arrs[n] for n in SHARDED]


def _full_grad_blocks(grads):
    g = grads
    cols = lambda a, wd: _col_blocks(a, wd)
    rows = lambda a: a.reshape(N_DEV, a.shape[0] // N_DEV, a.shape[1])
    return [cols(g["w_in"], IN_WIDTH // N_DEV), cols(g["gdn_conv_w"], 3 * WIDTH // N_DEV),
            cols(g["w_branch_fox"], D_MODEL // N_DEV), cols(g["w_branch_gdn"], D_MODEL // N_DEV), rows(g["w_out"]),
            cols(g["ffn_w_up"], 2 * D_FF // N_DEV), cols(g["ffn_conv_w"], 2 * D_FF // N_DEV), rows(g["ffn_w_down"])]


def kernel(x, meta_tokens, w_in, fgt_bias, gdn_conv_w, gdn_a_log, gdn_dt_bias, gdn_norm_w, gate_bias, w_branch_fox, w_branch_gdn, w_out, norm_mix_w, norm_ffn_w, ffn_w_up, ffn_conv_w, ffn_conv_b, ffn_w_down, norm_final_w, loss_target, m_meta_tokens, m_w_in, m_fgt_bias, m_gdn_conv_w, m_gdn_a_log, m_gdn_dt_bias, m_gdn_norm_w, m_gate_bias, m_w_branch_fox, m_w_branch_gdn, m_w_out, m_norm_mix_w, m_norm_ffn_w, m_ffn_w_up, m_ffn_conv_w, m_ffn_conv_b, m_ffn_w_down, m_norm_final_w, v_meta_tokens, v_w_in, v_fgt_bias, v_gdn_conv_w, v_gdn_a_log, v_gdn_dt_bias, v_gdn_norm_w, v_gate_bias, v_w_branch_fox, v_w_branch_gdn, v_w_out, v_norm_mix_w, v_norm_ffn_w, v_ffn_w_up, v_ffn_conv_w, v_ffn_conv_b, v_ffn_w_down, v_norm_final_w):
    wts = dict(meta_tokens=meta_tokens, w_in=w_in, fgt_bias=fgt_bias, gdn_conv_w=gdn_conv_w, gdn_a_log=gdn_a_log,
               gdn_dt_bias=gdn_dt_bias, gdn_norm_w=gdn_norm_w, gate_bias=gate_bias, w_branch_fox=w_branch_fox,
               w_branch_gdn=w_branch_gdn, w_out=w_out, norm_mix_w=norm_mix_w, norm_ffn_w=norm_ffn_w,
               ffn_w_up=ffn_w_up, ffn_conv_w=ffn_conv_w, ffn_conv_b=ffn_conv_b, ffn_w_down=ffn_w_down,
               norm_final_w=norm_final_w)
    mom = dict(meta_tokens=m_meta_tokens, w_in=m_w_in, fgt_bias=m_fgt_bias, gdn_conv_w=m_gdn_conv_w,
               gdn_a_log=m_gdn_a_log, gdn_dt_bias=m_gdn_dt_bias, gdn_norm_w=m_gdn_norm_w, gate_bias=m_gate_bias,
               w_branch_fox=m_w_branch_fox, w_branch_gdn=m_w_branch_gdn, w_out=m_w_out, norm_mix_w=m_norm_mix_w,
               norm_ffn_w=m_norm_ffn_w, ffn_w_up=m_ffn_w_up, ffn_conv_w=m_ffn_conv_w, ffn_conv_b=m_ffn_conv_b,
               ffn_w_down=m_ffn_w_down, norm_final_w=m_norm_final_w)
    var = dict(meta_tokens=v_meta_tokens, w_in=v_w_in, fgt_bias=v_fgt_bias, gdn_conv_w=v_gdn_conv_w,
               gdn_a_log=v_gdn_a_log, gdn_dt_bias=v_gdn_dt_bias, gdn_norm_w=v_gdn_norm_w, gate_bias=v_gate_bias,
               w_branch_fox=v_w_branch_fox, w_branch_gdn=v_w_branch_gdn, w_out=v_w_out, norm_mix_w=v_norm_mix_w,
               norm_ffn_w=v_norm_ffn_w, ffn_w_up=v_ffn_w_up, ffn_conv_w=v_ffn_conv_w, ffn_conv_b=v_ffn_conv_b,
               ffn_w_down=v_ffn_w_down, norm_final_w=v_norm_final_w)

    sh = dict(zip(SHARDED, _shard_pieces(wts)))
    me = 4 * lax.axis_index("x") + 2 * lax.axis_index("y") + lax.axis_index("c")
    late_names = MATMUL[1:]
    late_sems_send, late_sems_recv, late_src, late_land, late_token = _gather_start(
        [sh[n].astype(BF16) for n in late_names])
    exact_shapes = [sh[n].shape for n in EXACT]
    rows_exact = _rows_for(sum(int(np.prod(s)) for s in exact_shapes), 8)
    g_in, g_exact = _all_gather([sh["w_in"].astype(BF16), _pad_flat([sh[n] for n in EXACT], rows_exact)])
    meta_full, conv_full, fconv_full = (_cat_cols(a) for a in _unpack_gathered(g_exact, exact_shapes))
    full = dict(
        meta_tokens=meta_full, w_in=_cat_cols(g_in), gdn_conv_w=conv_full, ffn_conv_w=fconv_full,
        fgt_bias=fgt_bias, gdn_a_log=gdn_a_log, gdn_dt_bias=gdn_dt_bias, gdn_norm_w=gdn_norm_w, gate_bias=gate_bias,
        norm_mix_w=norm_mix_w, norm_ffn_w=norm_ffn_w, ffn_conv_b=ffn_conv_b, norm_final_w=norm_final_w)

    def fetch_late_weights(after):
        shards, lands = _gather_wait(late_sems_send, late_sems_recv, late_src, late_land, after)
        g_bf, g_bg, g_out, g_up, g_down = (lax.dynamic_update_slice_in_dim(land, s[None], me, 0)
                                           for s, land in zip(shards, lands))
        return dict(w_branch_fox=_cat_cols(g_bf), w_branch_gdn=_cat_cols(g_bg), w_out=g_out.reshape(D_MODEL, D_MODEL),
                    ffn_w_up=_cat_cols(g_up), ffn_w_down=g_down.reshape(D_FF, D_MODEL))

    def start_exchange(grads_so_far):
        blocks = [b.astype(BF16) for b in _full_grad_blocks(grads_so_far)]
        send_sems, recv_sems, src_thru, land_thru, token = _exchange_start(blocks)
        return token, (send_sems, recv_sems, src_thru, land_thru)

    loss, grad_x, grads, handle = _local_step(x[0], loss_target[0], full, early=start_exchange,
                                              late_weights=(late_token, fetch_late_weights))
    sent, landed = _exchange_wait(*handle, after=grad_x)
    own = lambda src, land: lax.dynamic_update_slice_in_dim(land, lax.dynamic_slice_in_dim(src, me, 1, 0), me, 0)
    received = [own(src, land) for src, land in zip(sent, landed)]

    rep_parts = [grads[n] for n in REPLICATED] + [loss[:, :1]]
    rep_shapes = [wts[n].shape for n in REPLICATED]
    rows_small = _rows_for(sum(int(np.prod(p.shape)) for p in rep_parts), 8)
    meta_recv, small_recv = _grad_exchange([_col_blocks(grads["meta_tokens"], LANES).astype(BF16)],
                                           _pad_flat(rep_parts, rows_small))
    received = [meta_recv] + received + [small_recv]

    result = {}
    kinds = ("grad", "delta", "new_m", "new_v")
    for n, recv in zip(SHARDED, received[:-1]):
        outs = _adamw(sh[n], _shard_pieces(mom)[SHARDED.index(n)], _shard_pieces(var)[SHARDED.index(n)], recv,
                      "adamw_" + n)
        for kind, a in zip(kinds, outs):
            result[kind, n] = a.reshape(wts[n].shape)
    rep_w = _pad_flat([wts[n] for n in REPLICATED] + [jnp.zeros((1, 1), F32)], rows_small)
    rep_m = _pad_flat([mom[n] for n in REPLICATED] + [jnp.zeros((1, 1), F32)], rows_small)
    rep_v = _pad_flat([var[n] for n in REPLICATED] + [jnp.ones((1, 1), F32)], rows_small)
    outs_r = _adamw(rep_w, rep_m, rep_v, received[-1], "adamw_replicated")
    for kind, br in zip(kinds, outs_r):
        for n, a in zip(REPLICATED, _unpack(br, rep_shapes)):
            result[kind, n] = a
    n_rep = sum(int(np.prod(s)) for s in rep_shapes)
    total_loss = outs_r[0].reshape(-1)[n_rep]
    out = [total_loss, grad_x[None]]
    for kind in ("grad", "delta", "new_m", "new_v"):
        out += [result[kind, n] for n in WEIGHTS]
    return tuple(out)
```

```python
import jax
import jax.numpy as jnp
import numpy as np
from jax import lax
from jax.experimental import pallas as pl
from jax.experimental.pallas import tpu as pltpu

F32 = jnp.float32
BF16 = jnp.bfloat16

D_MODEL = 1024
N_META = 16
HEADS = 8
HEAD_DIM = 64
WIDTH = HEADS * HEAD_DIM
CHUNK = 64
GDN_CONV = 4
D_FF = 2816
FFN_CONV = 3
IN_WIDTH = 5656
RMS_EPS = 1e-6
NEG = -1e30
AUG = 128
N_DEV = 8
LANES = 128

ADAM_LR = 0.001
ADAM_B1 = 0.9
ADAM_B2 = 0.999
ADAM_EPS = 1e-08
ADAM_WD = 0.01
ADAM_STEP = 10

VMEM_LIMIT = 56 * 1024 * 1024
MM_VMEM_BUDGET = 36 * 1024 * 1024
FFN_LANES = 128
HI = lax.Precision.HIGH
MESH_ID = pl.DeviceIdType.MESH


def _pick(n, cands):
    for c in cands:
        if n % c == 0:
            return c
    raise ValueError(f"no tile for {n} in {cands}")


def _params(*sem):
    return pltpu.CompilerParams(dimension_semantics=sem if sem else None, vmem_limit_bytes=VMEM_LIMIT)


def _padded_tokens(seq):
    t = -(-(N_META + seq) // 128) * 128
    if t > 1280 and t % 640:
        t = -(-t // 640) * 640
    return t


ROW_TILES = (640, 512, 384, 256, 128)


def _rmsnorm_fwd(h, gain):
    t, d = h.shape
    tr = _pick(t, ROW_TILES)

    def body(h_ref, g_ref, o_ref):
        x = h_ref[...]
        r = lax.rsqrt(jnp.mean(x * x, axis=-1, keepdims=True) + RMS_EPS)
        o_ref[...] = (x * r * g_ref[...]).astype(o_ref.dtype)

    return pl.pallas_call(
        body, grid=(t // tr,), name="rmsnorm_fwd",
        in_specs=[pl.BlockSpec((tr, d), lambda i: (i, 0)), pl.BlockSpec((1, d), lambda i: (0, 0))],
        out_specs=pl.BlockSpec((tr, d), lambda i: (i, 0)),
        out_shape=jax.ShapeDtypeStruct((t, d), BF16),
        compiler_params=_params("arbitrary"),
    )(h, gain)


def _rmsnorm_bwd(h, dy, gain, dres):
    t, d = h.shape
    tr = _pick(t, (320, 256, 128))

    def body(h_ref, dy_ref, g_ref, dres_ref, dh_ref, dhb_ref, dg_ref):
        x = h_ref[...]
        dyv = dy_ref[...].astype(F32)
        r = lax.rsqrt(jnp.mean(x * x, axis=-1, keepdims=True) + RMS_EPS)
        gy = dyv * g_ref[...]
        m = jnp.mean(gy * x, axis=-1, keepdims=True)
        dh = dres_ref[...] + r * gy - x * (r * r * r * m)
        dh_ref[...] = dh
        dhb_ref[...] = dh.astype(BF16)

        @pl.when(pl.program_id(0) == 0)
        def _():
            dg_ref[...] = jnp.zeros_like(dg_ref)

        dg_ref[...] += jnp.sum(dyv * x * r, axis=0, keepdims=True)

    row = pl.BlockSpec((tr, d), lambda i: (i, 0))
    vec = pl.BlockSpec((1, d), lambda i: (0, 0))
    return pl.pallas_call(
        body, grid=(t // tr,), name="rmsnorm_bwd",
        in_specs=[row, row, vec, row], out_specs=[row, row, vec],
        out_shape=[jax.ShapeDtypeStruct((t, d), F32), jax.ShapeDtypeStruct((t, d), BF16),
                   jax.ShapeDtypeStruct((1, d), F32)],
        compiler_params=_params("arbitrary"),
    )(h, dy, gain, dres)


def _mm(a, b, out_dtype, name, res=None):
    m, k = a.shape
    _, n = b.shape
    tm = _pick(m, ROW_TILES)
    out_bytes = jnp.dtype(out_dtype).itemsize + (4 if res is not None else 0)
    fits = lambda tn: 4 * tm * k + 4 * k * tn + 2 * tm * tn * out_bytes <= MM_VMEM_BUDGET
    tn = next(c for c in (n, 2816, 2048, 1536, 1408, 1024, 512, 384, 256, 128) if n % c == 0 and fits(c))

    def body(*refs):
        if res is None:
            a_ref, b_ref, o_ref = refs
        else:
            a_ref, b_ref, r_ref, o_ref = refs
        out = jnp.dot(a_ref[...], b_ref[...], preferred_element_type=F32)
        if res is not None:
            out = out + r_ref[...]
        o_ref[...] = out.astype(o_ref.dtype)

    in_specs = [pl.BlockSpec((tm, k), lambda i, j: (i, 0)), pl.BlockSpec((k, tn), lambda i, j: (0, j))]
    args = [a, b]
    if res is not None:
        in_specs.append(pl.BlockSpec((tm, tn), lambda i, j: (i, j)))
        args.append(res)
    return pl.pallas_call(
        body, grid=(m // tm, n // tn), name=name,
        in_specs=in_specs, out_specs=pl.BlockSpec((tm, tn), lambda i, j: (i, j)),
        out_shape=jax.ShapeDtypeStruct((m, n), out_dtype),
        compiler_params=_params("parallel", "parallel"),
    )(*args)


def _mm_tn(a, g, name):
    t, k = a.shape
    _, n = g.shape
    tk = _pick(k, (1024, 1408, 512))
    tn = _pick(n, (512, 640, 384, 256, 128))
    tt = next(c for c in (3328, 1280) + ROW_TILES
              if t % c == 0 and 4 * c * (tk + tn) + 8 * tk * tn <= MM_VMEM_BUDGET)
    nt = t // tt

    def body(a_ref, g_ref, o_ref):
        @pl.when(pl.program_id(2) == 0)
        def _():
            o_ref[...] = jnp.zeros_like(o_ref)

        o_ref[...] += lax.dot_general(a_ref[...], g_ref[...], (((0,), (0,)), ((), ())),
                                      preferred_element_type=F32)

    return pl.pallas_call(
        body, grid=(k // tk, n // tn, nt), name=name,
        in_specs=[pl.BlockSpec((tt, tk), lambda i, j, s: (s, i)), pl.BlockSpec((tt, tn), lambda i, j, s: (s, j))],
        out_specs=pl.BlockSpec((tk, tn), lambda i, j, s: (i, j)),
        out_shape=jax.ShapeDtypeStruct((k, n), F32),
        compiler_params=_params("parallel", "parallel", "arbitrary"),
    )(a, g)


def _split3_exact(x):
    def top(v):
        return lax.bitcast_convert_type(lax.bitcast_convert_type(v, jnp.int32) & jnp.int32(-65536), F32)

    hi = top(x)
    r1 = x - hi
    mid = top(r1)
    return hi, mid, r1 - mid


def _pair_head(ref, h, rows):
    x = ref[:, 128 * (h // 2):128 * (h // 2) + 128].astype(F32)
    return pltpu.roll(x, HEAD_DIM, axis=1) if h % 2 else x


def _lanes(rows):
    return lax.broadcasted_iota(jnp.int32, (rows, AUG), 1)


def _fox_prep(fq, scal):
    t = fq.shape[0]
    tt = _pick(t, (256, 128))

    def body(q_ref, k_ref, v_ref, s_ref, qa_ref, ka_ref, va_ref, kt_ref, vt_ref):
        lane = _lanes(tt)
        chi, cmid, clo = _split3_exact(s_ref[...])
        ones = lambda lo: jnp.where((lane >= lo) & (lane < lo + 3), 1.0, 0.0)
        for h in range(HEADS):
            col = lambda a: jnp.broadcast_to(a[:, h:h + 1], (tt, AUG))
            c1, c2, c3 = col(chi), col(cmid), col(clo)
            qx = jnp.where(lane == 64, c1, jnp.where(lane == 65, c2, jnp.where(lane == 66, c3, ones(67))))
            kx = jnp.where(lane == 67, -c1, jnp.where(lane == 68, -c2, jnp.where(lane == 69, -c3, ones(64) + ones(70))))
            qa_ref[h] = jnp.where(lane < HEAD_DIM, _pair_head(q_ref, h, tt) * (HEAD_DIM ** -0.5), qx).astype(BF16)
            k_aug = jnp.where(lane < HEAD_DIM, _pair_head(k_ref, h, tt), kx)
            ka_ref[h] = k_aug.astype(BF16)
            kt_ref[h] = k_aug.T.astype(BF16)
            v_aug = jnp.where(lane < HEAD_DIM, _pair_head(v_ref, h, tt), ones(64))
            va_ref[h] = v_aug.astype(BF16)
            vt_ref[h] = v_aug.T.astype(BF16)

    out = pl.BlockSpec((HEADS, tt, AUG), lambda i: (0, i, 0))
    out_t = pl.BlockSpec((HEADS, AUG, tt), lambda i: (0, 0, i))
    shp = jax.ShapeDtypeStruct((HEADS, t, AUG), BF16)
    shp_t = jax.ShapeDtypeStruct((HEADS, AUG, t), BF16)
    return pl.pallas_call(
        body, grid=(t // tt,), name="fox_prep",
        in_specs=[pl.BlockSpec((tt, WIDTH), lambda i: (i, 0)), pl.BlockSpec((tt, WIDTH), lambda i: (i, 1)),
                  pl.BlockSpec((tt, WIDTH), lambda i: (i, 2)), pl.BlockSpec((tt, LANES), lambda i: (i, 0))],
        out_specs=[out, out, out, out_t, out_t], out_shape=[shp, shp, shp, shp_t, shp_t],
        compiler_params=_params("parallel"),
    )(fq, fq, fq, scal)


def _fox_post(oa):
    t = oa.shape[1]
    tt = _pick(t, (256, 128))

    def body(o_ref, out_ref):
        out_ref[...] = jnp.concatenate([o_ref[h][:, :HEAD_DIM] for h in range(HEADS)], axis=1).astype(BF16)

    return pl.pallas_call(
        body, grid=(t // tt,), name="fox_post",
        in_specs=[pl.BlockSpec((HEADS, tt, AUG), lambda i: (0, i, 0))],
        out_specs=pl.BlockSpec((tt, WIDTH), lambda i: (i, 0)),
        out_shape=jax.ShapeDtypeStruct((t, WIDTH), BF16),
        compiler_params=_params("parallel"),
    )(oa)


def _fox_bwd_prep(do, oa):
    t = do.shape[0]
    tt = _pick(t, (256, 128))

    def body(d_ref, o_ref, out_ref, outt_ref):
        lane = _lanes(tt)
        for h in range(HEADS):
            x = _pair_head(d_ref, h, tt)
            delta = jnp.sum(jnp.where(lane < HEAD_DIM, x * o_ref[h], 0.0), axis=1, keepdims=True)
            hi, mid, lo = _split3_exact(jnp.broadcast_to(-delta, (tt, AUG)))
            ex = jnp.where(lane == 64, hi, jnp.where(lane == 65, mid, jnp.where(lane == 66, lo, 0.0)))
            do_aug = jnp.where(lane < HEAD_DIM, x, ex)
            out_ref[h] = do_aug.astype(BF16)
            outt_ref[h] = do_aug.T.astype(BF16)

    hm = pl.BlockSpec((HEADS, tt, AUG), lambda i: (0, i, 0))
    return pl.pallas_call(
        body, grid=(t // tt,), name="fox_bwd_prep",
        in_specs=[pl.BlockSpec((tt, WIDTH), lambda i: (i, 0)), hm],
        out_specs=[hm, pl.BlockSpec((HEADS, AUG, tt), lambda i: (0, 0, i))],
        out_shape=[jax.ShapeDtypeStruct((HEADS, t, AUG), BF16), jax.ShapeDtypeStruct((HEADS, AUG, t), BF16)],
        compiler_params=_params("parallel"),
    )(do, oa)


def _fox_bwd_post(dqt, dkt, dvt):
    t = dqt.shape[2]
    tt = _pick(t, (256, 128))

    def body(dq_ref, dk_ref, dv_ref, out_ref, dsc_ref):
        lane = _lanes(tt)
        dqs = [dq_ref[h].T for h in range(HEADS)]
        dks = [dk_ref[h].T for h in range(HEADS)]
        heads = lambda xs: jnp.concatenate([x[:, :HEAD_DIM] for x in xs], axis=1)
        out_ref[:, 0:WIDTH] = (heads(dqs) * (HEAD_DIM ** -0.5)).astype(BF16)
        out_ref[:, WIDTH:2 * WIDTH] = heads(dks).astype(BF16)
        out_ref[:, 2 * WIDTH:] = heads([dv_ref[h].T for h in range(HEADS)]).astype(BF16)
        dsc = jnp.zeros((tt, LANES), F32)
        for h in range(HEADS):
            both = jnp.where(lane == HEAD_DIM, dqs[h], 0.0) - jnp.where(lane == HEAD_DIM + 3, dks[h], 0.0)
            dsc = jnp.where(lane == h, jnp.sum(both, axis=1, keepdims=True), dsc)
        dsc_ref[...] = dsc

    hm = pl.BlockSpec((HEADS, AUG, tt), lambda i: (0, 0, i))
    return pl.pallas_call(
        body, grid=(t // tt,), name="fox_bwd_post",
        in_specs=[hm, hm, hm],
        out_specs=[pl.BlockSpec((tt, 3 * WIDTH), lambda i: (i, 0)), pl.BlockSpec((tt, LANES), lambda i: (i, 0))],
        out_shape=[jax.ShapeDtypeStruct((t, 3 * WIDTH), BF16), jax.ShapeDtypeStruct((t, LANES), F32)],
        compiler_params=_params("parallel"),
    )(dqt, dkt, dvt)


def _fox_fwd(qa, ka, vat, tq=None):
    h, t, _ = qa.shape
    tq = tq or _pick(t, ROW_TILES)

    def body(q_ref, k_ref, vt_ref, o_ref, qb_ref, qbt_ref, s_ref):
        i = pl.program_id(1)
        q = q_ref[...]
        krow = lax.broadcasted_iota(jnp.int32, (tq, tq), 0)
        qcol = lax.broadcasted_iota(jnp.int32, (tq, tq), 1)
        rows = lambda j: pl.ds(pl.multiple_of(j * tq, tq), tq)

        def scores(j, slot):
            s_ref[slot] = lax.dot_general(k_ref[rows(j), :], q, (((1,), (1,)), ((), ())), preferred_element_type=F32)

        def update(j, slot, carry, masked):
            m, acc = carry
            s = s_ref[slot]
            if masked:
                s = jnp.where(qcol >= krow, s, NEG)
            m_new = jnp.maximum(m, jnp.max(s, axis=0, keepdims=True))
            p = jnp.exp(s - m_new)
            alpha = jnp.exp(m - m_new)
            return m_new, acc * alpha + jnp.dot(vt_ref[:, rows(j)], p.astype(BF16), preferred_element_type=F32)

        def pair(j, carry):
            scores(j + 1, 1)
            carry = update(j, 0, carry, False)
            scores(j + 2, 0)
            return update(j + 1, 1, carry, False)

        def odd_tail(carry):
            scores(i, 1)
            return update(i, 1, update(i - 1, 0, carry, False), True)

        scores(0, 0)
        carry = (jnp.full((1, tq), NEG, F32), jnp.zeros((AUG, tq), F32))
        carry = lax.fori_loop(0, i // 4, lambda jj, c: pair(4 * jj + 2, pair(4 * jj, c)), carry)
        carry = lax.fori_loop(0, (i % 4) // 2, lambda jj, c: pair(4 * (i // 4), c), carry)
        m, acc = lax.cond(i % 2 == 1, odd_tail, lambda c: update(i, 0, c, True), carry)
        sub = lax.broadcasted_iota(jnp.int32, (AUG, tq), 0)
        l = jnp.sum(jnp.where(sub == HEAD_DIM, acc, 0.0), axis=0, keepdims=True)
        out = jnp.where(sub < HEAD_DIM, acc / l, m + jnp.log(l)).T
        o_ref[...] = out
        lane = lax.broadcasted_iota(jnp.int32, (tq, AUG), 1)
        lse = jnp.broadcast_to(jnp.sum(jnp.where(lane == HEAD_DIM, out, 0.0), axis=1, keepdims=True), (tq, AUG))
        hi, mid, lo = _split3_exact(-lse)
        qb = jnp.where(lane == 70, hi, jnp.where(lane == 71, mid, jnp.where(lane == 72, lo, q.astype(F32))))
        qb_ref[...] = qb.astype(BF16)
        qbt_ref[...] = qb.T.astype(BF16)

    blk = pl.BlockSpec((None, tq, AUG), lambda hh, i: (hh, i, 0))
    return pl.pallas_call(
        body, grid=(h, t // tq), name="fox_fwd",
        in_specs=[blk, pl.BlockSpec((None, t, AUG), lambda hh, i: (hh, 0, 0)),
                  pl.BlockSpec((None, AUG, t), lambda hh, i: (hh, 0, 0))],
        out_specs=[blk, blk, pl.BlockSpec((None, AUG, tq), lambda hh, i: (hh, 0, i))],
        out_shape=[jax.ShapeDtypeStruct((h, t, AUG), F32), jax.ShapeDtypeStruct((h, t, AUG), BF16),
                   jax.ShapeDtypeStruct((h, AUG, t), BF16)],
        scratch_shapes=[pltpu.VMEM((2, tq, tq), F32)],
        compiler_params=_params("parallel", "arbitrary"),
    )(qa, ka, vat)


def _fox_bwd(qb, qbt, ka, kat, va, doa, doat, tq=None):
    h, t, _ = qb.shape
    tq = tq or _pick(t, ROW_TILES)
    nq = t // tq

    def body(q_ref, qt_ref, k_ref, kt_ref, v_ref, do_ref, dot_ref, dqt_ref, dkt_ref, dvt_ref, s_ref, dp_ref):
        j = pl.program_id(1)
        n = nq - j

        @pl.when(j == 0)
        def _():
            dqt_ref[...] = jnp.zeros_like(dqt_ref)

        dkt_ref[...] = jnp.zeros_like(dkt_ref)
        dvt_ref[...] = jnp.zeros_like(dvt_ref)
        kj = k_ref[...]
        ktj = kt_ref[...]
        vj = v_ref[...]
        qrow = lax.broadcasted_iota(jnp.int32, (tq, tq), 0)
        kcol = lax.broadcasted_iota(jnp.int32, (tq, tq), 1)
        rows = lambda i: pl.ds(pl.multiple_of(i * tq, tq), tq)
        nt_dims = (((1,), (1,)), ((), ()))

        def scores(i, slot):
            s_ref[slot] = lax.dot_general(q_ref[rows(i), :], kj, nt_dims, preferred_element_type=F32)
            dp_ref[slot] = lax.dot_general(do_ref[rows(i), :], vj, nt_dims, preferred_element_type=F32)

        def update(i, slot):
            p = jnp.exp(jnp.where((qrow >= kcol) | (i > j), s_ref[slot], NEG))
            ds = (p * dp_ref[slot]).astype(BF16)
            dvt_ref[...] += jnp.dot(dot_ref[:, rows(i)], p.astype(BF16), preferred_element_type=F32)
            dkt_ref[...] += jnp.dot(qt_ref[:, rows(i)], ds, preferred_element_type=F32)
            dqt_ref[:, rows(i)] += lax.dot_general(ktj, ds, nt_dims, preferred_element_type=F32)

        def pair(i0):
            scores(i0 + 1, 1)
            update(i0, 0)
            scores(jnp.minimum(i0 + 2, nq - 1), 0)
            update(i0 + 1, 1)

        def quad(kk, carry):
            pair(j + 4 * kk)
            pair(j + 4 * kk + 2)
            return carry

        def last_pair(kk, carry):
            pair(j + 4 * (n // 4))
            return carry

        scores(j, 0)
        lax.fori_loop(0, n // 4, quad, 0)
        lax.fori_loop(0, (n % 4) // 2, last_pair, 0)

        @pl.when(n % 2 == 1)
        def _():
            update(nq - 1, 0)

    once = pl.Buffered(1)
    full = pl.BlockSpec((None, t, AUG), lambda hh, j: (hh, 0, 0), pipeline_mode=once)
    full_t = pl.BlockSpec((None, AUG, t), lambda hh, j: (hh, 0, 0), pipeline_mode=once)
    blk = pl.BlockSpec((None, tq, AUG), lambda hh, j: (hh, j, 0))
    blk_t = pl.BlockSpec((None, AUG, tq), lambda hh, j: (hh, 0, j))
    shp = jax.ShapeDtypeStruct((h, AUG, t), F32)
    return pl.pallas_call(
        body, grid=(h, nq), name="fox_bwd",
        in_specs=[full, full_t, blk, blk_t, blk, full, full_t],
        out_specs=[pl.BlockSpec((None, AUG, t), lambda hh, j: (hh, 0, 0)), blk_t, blk_t], out_shape=[shp, shp, shp],
        scratch_shapes=[pltpu.VMEM((2, tq, tq), F32), pltpu.VMEM((2, tq, tq), F32)],
        compiler_params=_params("parallel", "arbitrary"),
    )(qb, qbt, ka, kat, va, doa, doat)


def _seg_matrix():
    idx = np.arange(WIDTH) // HEAD_DIM
    return jnp.asarray((idx[:, None] == idx[None, :]).astype(np.float32))


def _segsum(x, e):
    return jnp.dot(x, e, precision=HI, preferred_element_type=F32)


def _silu(x):
    return x * jax.nn.sigmoid(x)


def _silu_grad(x):
    s = jax.nn.sigmoid(x)
    return s * (1.0 + x * (1.0 - s))


def _shift_down(x, prev8, k):
    r = pltpu.roll(x, k, axis=0)
    p = pltpu.roll(prev8, k, axis=0)
    row = lax.broadcasted_iota(jnp.int32, prev8.shape, 0)
    head = jnp.where(row < k, p, r[:8])
    return jnp.concatenate([head, r[8:]], axis=0)


def _shift_up(x, next8, k):
    n = x.shape[0]
    r = pltpu.roll(x, n - k, axis=0)
    p = pltpu.roll(next8, 8 - k, axis=0)
    row = lax.broadcasted_iota(jnp.int32, next8.shape, 0)
    tail = jnp.where(row >= 8 - k, p, r[n - 8:])
    return jnp.concatenate([r[:n - 8], tail], axis=0)


def _causal_conv(x, prev8, w_ref, width, cols=slice(None)):
    y = x * w_ref[width - 1:width, cols]
    for k in range(1, width):
        y = y + _shift_down(x, prev8, k) * w_ref[width - 1 - k:width - k, cols]
    return y


def _causal_conv_bwd(x, prev8, dy, dnext8, w_ref, dw_ref, width, cols=slice(None)):
    dx = dy * w_ref[width - 1:width, cols]
    dw_ref[width - 1:width, cols] += jnp.sum(dy * x, axis=0, keepdims=True)
    for k in range(1, width):
        dx = dx + _shift_up(dy, dnext8, k) * w_ref[width - 1 - k:width - k, cols]
        dw_ref[width - 1 - k:width - k, cols] += jnp.sum(dy * _shift_down(x, prev8, k), axis=0, keepdims=True)
    return dx


HALO = 16


def _prev_spec(tt, width, tile=lambda i: i):
    return pl.BlockSpec((HALO, width), lambda i: (jnp.maximum(tile(i) * (tt // HALO) - 1, 0), 0))


def _prev8(p_ref, cols=slice(None)):
    return p_ref[:, cols].astype(F32)[HALO - 8:]


def _store_heads(ref, x):
    for h in range(HEADS):
        ref[h] = x[:, HEAD_DIM * h:HEAD_DIM * (h + 1)]


def _load_heads(ref):
    return jnp.concatenate([ref[h] for h in range(HEADS)], axis=1)


def _softplus(z):
    return jnp.maximum(z, 0.0) + jnp.log1p(jnp.exp(-jnp.abs(z)))


def _tri_masks(tt):
    r = lax.broadcasted_iota(jnp.int32, (tt, tt), 0)
    c = lax.broadcasted_iota(jnp.int32, (tt, tt), 1)
    same_chunk = lax.shift_right_logical(r, 6) == lax.shift_right_logical(c, 6)
    return r, c, same_chunk


def _gate_fwd(small, pbias, pscale):
    t = small.shape[0]
    tt = _pick(t, (256, 128))

    def body(x_ref, pb_ref, ps_ref, o_ref, carry_ref):
        @pl.when(pl.program_id(0) == 0)
        def _():
            carry_ref[...] = jnp.zeros_like(carry_ref)

        lane = lax.broadcasted_iota(jnp.int32, (tt, LANES), 1)
        z = x_ref[...] + pb_ref[...]
        log_f = jnp.where(lane < HEADS, -_softplus(-z), 0.0)
        g = jnp.where((lane >= 2 * HEADS) & (lane < 3 * HEADS), ps_ref[...] * _softplus(z), 0.0)
        r, c, same_chunk = _tri_masks(tt)
        lower = jnp.where(r >= c, 1.0, 0.0)
        lower_chunk = jnp.where((r >= c) & same_chunk, 1.0, 0.0)
        csum = jnp.dot(lower, log_f, precision=lax.Precision.HIGHEST, preferred_element_type=F32) + carry_ref[...]
        gc = jnp.dot(lower_chunk, g, precision=lax.Precision.HIGHEST, preferred_element_type=F32)
        carry_ref[...] += jnp.sum(log_f, axis=0, keepdims=True)
        o_ref[...] = jnp.where(lane < HEADS, csum, jnp.where(lane < 2 * HEADS, jax.nn.sigmoid(z), gc))

    row = pl.BlockSpec((tt, LANES), lambda i: (i, 0))
    vec = pl.BlockSpec((1, LANES), lambda i: (0, 0))
    return pl.pallas_call(
        body, grid=(t // tt,), name="gate_fwd", in_specs=[row, vec, vec], out_specs=row,
        out_shape=jax.ShapeDtypeStruct((t, LANES), F32),
        scratch_shapes=[pltpu.VMEM((1, LANES), F32)],
        compiler_params=_params("arbitrary"),
    )(small, pbias, pscale)


def _gate_bwd(small, pbias, pscale, dscal):
    t = small.shape[0]
    tt = _pick(t, (256, 128))
    nt = t // tt

    def body(x_ref, pb_ref, ps_ref, d_ref, dx_ref, dpb_ref, dps_ref, carry_ref):
        @pl.when(pl.program_id(0) == 0)
        def _():
            carry_ref[...] = jnp.zeros_like(carry_ref)
            dpb_ref[...] = jnp.zeros_like(dpb_ref)
            dps_ref[...] = jnp.zeros_like(dps_ref)

        lane = lax.broadcasted_iota(jnp.int32, (tt, LANES), 1)
        z = x_ref[...] + pb_ref[...]
        d = d_ref[...]
        dc = jnp.where(lane < HEADS, d, 0.0)
        dbeta = jnp.where((lane >= HEADS) & (lane < 2 * HEADS), d, 0.0)
        dgc = jnp.where((lane >= 2 * HEADS) & (lane < 3 * HEADS), d, 0.0)
        r, c, same_chunk = _tri_masks(tt)
        upper = jnp.where(r <= c, 1.0, 0.0)
        upper_chunk = jnp.where((r <= c) & same_chunk, 1.0, 0.0)
        dlogf = jnp.dot(upper, dc, precision=lax.Precision.HIGHEST, preferred_element_type=F32) + carry_ref[...]
        dg = jnp.dot(upper_chunk, dgc, precision=lax.Precision.HIGHEST, preferred_element_type=F32)
        carry_ref[...] += jnp.sum(dc, axis=0, keepdims=True)
        sg = jax.nn.sigmoid(z)
        dz = dlogf * (1.0 - sg) + dbeta * sg * (1.0 - sg) + dg * ps_ref[...] * sg
        dx_ref[...] = dz.astype(dx_ref.dtype)
        dpb_ref[...] += jnp.sum(dz, axis=0, keepdims=True)
        dps_ref[...] += jnp.sum(dg * _softplus(z), axis=0, keepdims=True)

    row = pl.BlockSpec((tt, LANES), lambda i: (nt - 1 - i, 0))
    vec = pl.BlockSpec((1, LANES), lambda i: (0, 0))
    return pl.pallas_call(
        body, grid=(nt,), name="gate_bwd", in_specs=[row, vec, vec, row], out_specs=[row, vec, vec],
        out_shape=[jax.ShapeDtypeStruct((t, LANES), BF16), jax.ShapeDtypeStruct((1, LANES), F32),
                   jax.ShapeDtypeStruct((1, LANES), F32)],
        scratch_shapes=[pltpu.VMEM((1, LANES), F32)],
        compiler_params=_params("arbitrary"),
    )(small, pbias, pscale, dscal)


def _gdn_pre_fwd(xg, conv_w, seg):
    t = xg.shape[0]
    c3 = 3 * WIDTH
    tt = _pick(t, (320, 256, 128))

    def body(x_ref, p_ref, w_ref, e_ref, q_ref, k_ref, v_ref):
        x = x_ref[...].astype(F32)
        prev = jnp.where(pl.program_id(0) == 0, 0.0, _prev8(p_ref))
        s = _silu(_causal_conv(x, prev, w_ref, GDN_CONV))
        e = e_ref[...]
        q = s[:, :WIDTH]
        k = s[:, WIDTH:2 * WIDTH]
        _store_heads(q_ref, q * lax.rsqrt(_segsum(q * q, e) + RMS_EPS) * (HEAD_DIM ** -0.5))
        _store_heads(k_ref, k * lax.rsqrt(_segsum(k * k, e) + RMS_EPS))
        _store_heads(v_ref, s[:, 2 * WIDTH:])

    out = pl.BlockSpec((HEADS, tt, HEAD_DIM), lambda i: (0, i, 0))
    shp = jax.ShapeDtypeStruct((HEADS, t, HEAD_DIM), F32)
    return pl.pallas_call(
        body, grid=(t // tt,), name="gdn_pre_fwd",
        in_specs=[pl.BlockSpec((tt, c3), lambda i: (i, 0)), _prev_spec(tt, c3),
                  pl.BlockSpec((GDN_CONV, c3), lambda i: (0, 0)), pl.BlockSpec((WIDTH, WIDTH), lambda i: (0, 0))],
        out_specs=[out, out, out], out_shape=[shp, shp, shp],
        compiler_params=_params("arbitrary"),
    )(xg, xg, conv_w, seg)


def _gdn_pre_bwd(xg, conv_w, seg, dqn, dkn, dv):
    t = xg.shape[0]
    c3 = 3 * WIDTH
    tt = _pick(t, (320, 256, 128))
    nt = t // tt

    def body(x_ref, p_ref, w_ref, e_ref, dq_ref, dk_ref, dv_ref, dx_ref, dw_ref, carry_ref):
        step = pl.program_id(0)
        x = x_ref[...].astype(F32)
        e = e_ref[...]
        prev = jnp.where(step == nt - 1, 0.0, _prev8(p_ref))
        y = _causal_conv(x, prev, w_ref, GDN_CONV)
        s = _silu(y)
        q = s[:, :WIDTH]
        k = s[:, WIDTH:2 * WIDTH]
        rq = lax.rsqrt(_segsum(q * q, e) + RMS_EPS)
        rk = lax.rsqrt(_segsum(k * k, e) + RMS_EPS)
        gq = _load_heads(dq_ref) * (HEAD_DIM ** -0.5)
        gk = _load_heads(dk_ref)
        dq = rq * gq - q * (rq * rq * rq) * _segsum(gq * q, e)
        dk = rk * gk - k * (rk * rk * rk) * _segsum(gk * k, e)
        dy = jnp.concatenate([dq, dk, _load_heads(dv_ref)], axis=1) * _silu_grad(y)

        @pl.when(step == 0)
        def _():
            carry_ref[...] = jnp.zeros_like(carry_ref)
            dw_ref[...] = jnp.zeros_like(dw_ref)

        dx = _causal_conv_bwd(x, prev, dy, carry_ref[...], w_ref, dw_ref, GDN_CONV)
        dx_ref[...] = dx.astype(dx_ref.dtype)
        carry_ref[...] = dy[:8]

    rev = lambda i: (nt - 1 - i, 0)
    blk = pl.BlockSpec((HEADS, tt, HEAD_DIM), lambda i: (0, nt - 1 - i, 0))
    return pl.pallas_call(
        body, grid=(nt,), name="gdn_pre_bwd",
        in_specs=[pl.BlockSpec((tt, c3), rev), _prev_spec(tt, c3, lambda i: nt - 1 - i),
                  pl.BlockSpec((GDN_CONV, c3), lambda i: (0, 0)), pl.BlockSpec((WIDTH, WIDTH), lambda i: (0, 0)),
                  blk, blk, blk],
        out_specs=[pl.BlockSpec((tt, c3), rev), pl.BlockSpec((GDN_CONV, c3), lambda i: (0, 0))],
        out_shape=[jax.ShapeDtypeStruct((t, c3), BF16), jax.ShapeDtypeStruct((GDN_CONV, c3), F32)],
        scratch_shapes=[pltpu.VMEM((8, c3), F32)],
        compiler_params=_params("arbitrary"),
    )(xg, xg, conv_w, seg, dqn, dkn, dv)


def _bmm(a, b, ca, cb, precision=None):
    return lax.dot_general(a, b, (((ca,), (cb,)), ((0,), (0,))), precision=precision, preferred_element_type=F32)


def _bf(x):
    return x.astype(BF16)


def _tri_inverse(a, eye):
    x = -a
    tinv = eye + x
    pw = x
    for _ in range(5):
        pb = _bf(pw)
        pw = _bmm(pb, pb, 2, 1)
        tinv = tinv + _bmm(_bf(tinv), _bf(pw), 2, 1)
    resid = eye - _bmm(eye + a, tinv, 2, 1, precision=HI)
    return tinv + _bmm(_bf(tinv), _bf(resid), 2, 1)


def _gdn_intra(q, k, v, bc, gcc, gcr):
    ii = lax.broadcasted_iota(jnp.int32, (CHUNK, CHUNK), 0)
    jj = lax.broadcasted_iota(jnp.int32, (CHUNK, CHUNK), 1)
    tril = (ii >= jj)[None]
    strict = (ii > jj)[None]
    eye = jnp.where(ii == jj, 1.0, 0.0).astype(F32)[None]
    last = (ii == CHUNK - 1)[None]
    dm = jnp.exp(jnp.where(tril, gcc - gcr, NEG))
    gam = jnp.exp(gcc)
    kb = k * bc
    vb = v * bc
    kk = _bmm(_bf(kb), _bf(k), 2, 2)
    a = jnp.where(strict, kk * dm, 0.0)
    tinv = _tri_inverse(a, eye)
    uw = _bmm(tinv, jnp.concatenate([vb, kb * gam], axis=2), 2, 1, precision=HI)
    u, wk = uw[:, :, :HEAD_DIM], uw[:, :, HEAD_DIM:]
    qk = _bmm(_bf(q), _bf(k), 2, 2)
    p = jnp.where(tril, qk * dm, 0.0)
    gl = jnp.sum(jnp.where(last, gcc, 0.0), axis=1, keepdims=True)
    edec = jnp.exp(gl - gcc)
    return dict(tril=tril, strict=strict, dm=dm, gam=gam, kb=kb, kk=kk, a=a, tinv=tinv, u=u, wk=wk, qk=qk, p=p,
                qg=q * gam, kt=k * edec, edec=edec, gaml=jnp.exp(gl), last=last)


def _gate_tiles(sc, gct, nb):
    rows = nb * CHUNK
    cols = lambda lane0: jnp.stack([jnp.broadcast_to(sc[:, lane0 + h:lane0 + h + 1], (rows, HEAD_DIM))
                                    for h in range(HEADS)], axis=0).reshape(HEADS * nb, CHUNK, HEAD_DIM)
    gcr = jnp.stack([jnp.broadcast_to(gct[h:h + 1, n * CHUNK:(n + 1) * CHUNK], (CHUNK, CHUNK))
                     for h in range(HEADS) for n in range(nb)], axis=0)
    return cols(HEADS), cols(2 * HEADS), gcr


def _gdn_fwd(q, k, v, scal, gct, nb=None):
    h, t, dh = q.shape
    nc = t // CHUNK
    nb = nb or _pick(nc, (4, 2))
    bsz = h * nb

    def body(q_ref, k_ref, v_ref, sc_ref, gt_ref, o_ref, s0_ref, state_ref):
        @pl.when(pl.program_id(0) == 0)
        def _():
            state_ref[...] = jnp.zeros_like(state_ref)

        ld = lambda r: r[...].reshape(bsz, CHUNK, dh)
        bc, gcc, gcr = _gate_tiles(sc_ref[...], gt_ref[...], nb)
        z = _gdn_intra(ld(q_ref), ld(k_ref), ld(v_ref), bc, gcc, gcr)
        per = lambda x: x.reshape((h, nb) + x.shape[1:])
        u, wk, p, qg, kt, gaml = (per(z[n]) for n in ("u", "wk", "p", "qg", "kt", "gaml"))
        s = state_ref[...]
        for n in range(nb):
            s0_ref[:, n] = s
            sb = _bf(s)
            vn = u[:, n] - _bmm(_bf(wk[:, n]), sb, 2, 1)
            vnb = _bf(vn)
            o_ref[:, n * CHUNK:(n + 1) * CHUNK, :] = _bmm(_bf(jnp.concatenate([qg[:, n], p[:, n]], axis=2)),
                                                          jnp.concatenate([sb, vnb], axis=1), 2, 1)
            s = s * gaml[:, n] + _bmm(_bf(kt[:, n]), vnb, 1, 1)
        state_ref[...] = s

    blk = pl.BlockSpec((h, nb * CHUNK, dh), lambda i: (0, i, 0))
    return pl.pallas_call(
        body, grid=(nc // nb,), name="gdn_fwd",
        in_specs=[blk] * 3 + [pl.BlockSpec((nb * CHUNK, LANES), lambda i: (i, 0)),
                              pl.BlockSpec((h, nb * CHUNK), lambda i: (0, i))],
        out_specs=[blk, pl.BlockSpec((h, nb, dh, dh), lambda i: (0, i, 0, 0))],
        out_shape=[jax.ShapeDtypeStruct((h, t, dh), F32), jax.ShapeDtypeStruct((h, nc, dh, dh), F32)],
        scratch_shapes=[pltpu.VMEM((h, dh, dh), F32)],
        compiler_params=_params("arbitrary"),
    )(q, k, v, scal, gct)


def _gdn_bwd(q, k, v, scal, gct, s0s, do, nb=None):
    h, t, dh = q.shape
    nc = t // CHUNK
    nb = nb or _pick(nc, (2,))
    bsz = h * nb
    ng = nc // nb
    rows = nb * CHUNK

    def body(q_ref, k_ref, v_ref, sc_ref, gt_ref, s0_ref, do_ref,
             dq_ref, dk_ref, dv_ref, dsc_ref, dgt_ref, ds_ref):
        @pl.when(pl.program_id(0) == 0)
        def _():
            ds_ref[...] = jnp.zeros_like(ds_ref)

        ld = lambda r: r[...].reshape(bsz, CHUNK, dh)
        q, k, v = ld(q_ref), ld(k_ref), ld(v_ref)
        bc, gcc, gcr = _gate_tiles(sc_ref[...], gt_ref[...], nb)
        z = _gdn_intra(q, k, v, bc, gcc, gcr)
        per = lambda x: x.reshape((h, nb) + x.shape[1:])
        u, wk, p, qg, kt, gaml = (per(z[n]) for n in ("u", "wk", "p", "qg", "kt", "gaml"))
        dout = per(ld(do_ref))
        ds = ds_ref[...]
        d_u, d_wk, d_p, d_qg, d_kt, d_gaml = ([None] * nb for _ in range(6))
        for n in reversed(range(nb)):
            s0 = s0_ref[:, n]
            s0b, dsb, dob = _bf(s0), _bf(ds), _bf(dout[:, n])
            wkb, qgb = _bf(wk[:, n]), _bf(qg[:, n])
            vn = u[:, n] - _bmm(wkb, s0b, 2, 1)
            dvn = _bmm(_bf(p[:, n]), dob, 1, 1) + _bmm(_bf(kt[:, n]), dsb, 2, 1)
            dvnb = _bf(dvn)
            d_u[n] = dvn
            vnb = _bf(vn)
            dpq = _bmm(dob, jnp.concatenate([vnb, s0b], axis=1), 2, 2)
            d_p[n], d_qg[n] = dpq[:, :, :CHUNK], dpq[:, :, CHUNK:]
            d_kt[n] = _bmm(vnb, dsb, 2, 2)
            d_gaml[n] = jnp.sum(s0 * ds, axis=1, keepdims=True)
            d_wk[n] = -_bmm(dvnb, s0b, 2, 2)
            ds = gaml[:, n] * ds + _bmm(jnp.concatenate([qgb, -wkb], axis=1), jnp.concatenate([dob, dvnb], axis=1), 1, 1)
        ds_ref[...] = ds

        flat = lambda xs: jnp.stack(xs, axis=1).reshape((bsz,) + xs[0].shape[1:])
        d_u, d_wk, d_p, d_qg, d_kt, d_gaml = (flat(x) for x in (d_u, d_wk, d_p, d_qg, d_kt, d_gaml))
        tinv, gam, kb, dm = z["tinv"], z["gam"], z["kb"], z["dm"]
        dr = _bmm(tinv, jnp.concatenate([d_u, d_wk], axis=2), 1, 1, precision=HI)
        drv, drk = dr[:, :, :HEAD_DIM], dr[:, :, HEAD_DIM:]
        da = -_bmm(_bf(dr), _bf(jnp.concatenate([z["u"], z["wk"]], axis=2)), 2, 2)
        da = jnp.where(z["strict"], da, 0.0)
        d_p = jnp.where(z["tril"], d_p, 0.0)
        dkk = _bf(da * dm)
        dqk = _bf(d_p * dm)
        dkb = _bmm(dkk, _bf(k), 2, 1) + drk * gam
        dk = (_bmm(jnp.concatenate([dkk, dqk], axis=1), _bf(jnp.concatenate([kb, q], axis=1)), 1, 1)
              + dkb * bc + d_kt * z["edec"])
        dq = _bmm(dqk, _bf(k), 2, 1) + d_qg * gam
        mm = da * z["a"] + d_p * z["p"]
        dkt_kt = d_kt * z["kt"]
        dgl = jnp.sum(dkt_kt, axis=1, keepdims=True) + d_gaml * z["gaml"]
        dgc = mm + d_qg * z["qg"] + drk * kb * gam - dkt_kt + jnp.where(z["last"], dgl, 0.0)
        dq_ref[...] = dq.reshape(h, rows, dh)
        dk_ref[...] = dk.reshape(h, rows, dh)
        dv_ref[...] = (drv * bc).reshape(h, rows, dh)
        dbeta = (dkb * k + drv * v).reshape(h, rows, dh)
        dgc = dgc.reshape(h, rows, dh)
        lane = lax.broadcasted_iota(jnp.int32, (rows, LANES), 1)
        dsc = jnp.zeros((rows, LANES), F32)
        for hh in range(h):
            dsc = jnp.where(lane == HEADS + hh, jnp.sum(dbeta[hh], axis=1, keepdims=True), dsc)
            dsc = jnp.where(lane == 2 * HEADS + hh, jnp.sum(dgc[hh], axis=1, keepdims=True), dsc)
        dsc_ref[...] = dsc
        dgr = -jnp.sum(mm, axis=1, keepdims=True)
        for hh in range(h):
            for n in range(nb):
                dgt_ref[hh:hh + 1, n * CHUNK:(n + 1) * CHUNK] = dgr[hh * nb + n]

    blk = pl.BlockSpec((h, rows, dh), lambda i: (0, ng - 1 - i, 0))
    shp = jax.ShapeDtypeStruct((h, t, dh), F32)
    sc_spec = pl.BlockSpec((rows, LANES), lambda i: (ng - 1 - i, 0))
    gt_spec = pl.BlockSpec((h, rows), lambda i: (0, ng - 1 - i))
    return pl.pallas_call(
        body, grid=(ng,), name="gdn_bwd",
        in_specs=[blk] * 3 + [sc_spec, gt_spec, pl.BlockSpec((h, nb, dh, dh), lambda i: (0, ng - 1 - i, 0, 0)), blk],
        out_specs=[blk] * 3 + [sc_spec, gt_spec],
        out_shape=[shp] * 3 + [jax.ShapeDtypeStruct((t, LANES), F32), jax.ShapeDtypeStruct((h, t), F32)],
        scratch_shapes=[pltpu.VMEM((h, dh, dh), F32)],
        compiler_params=_params("arbitrary"),
    )(q, k, v, scal, gct, s0s, do)


def _gdn_post_fwd(o, xg, gain, seg):
    t = o.shape[1]
    tt = _pick(t, (320, 256, 128))

    def body(o_ref, z_ref, g_ref, e_ref, y_ref):
        x = _load_heads(o_ref)
        r = lax.rsqrt(_segsum(x * x, e_ref[...]) * (1.0 / HEAD_DIM) + RMS_EPS)
        y_ref[...] = (x * r * g_ref[...] * _silu(z_ref[...].astype(F32))).astype(y_ref.dtype)

    return pl.pallas_call(
        body, grid=(t // tt,), name="gdn_post_fwd",
        in_specs=[pl.BlockSpec((HEADS, tt, HEAD_DIM), lambda i: (0, i, 0)), pl.BlockSpec((tt, WIDTH), lambda i: (i, 3)),
                  pl.BlockSpec((1, WIDTH), lambda i: (0, 0)), pl.BlockSpec((WIDTH, WIDTH), lambda i: (0, 0))],
        out_specs=pl.BlockSpec((tt, WIDTH), lambda i: (i, 0)),
        out_shape=jax.ShapeDtypeStruct((t, WIDTH), BF16),
        compiler_params=_params("arbitrary"),
    )(o, xg, gain, seg)


def _gdn_post_bwd(o, xg, gain, seg, dy):
    t = o.shape[1]
    tt = _pick(t, (320, 256, 128))

    def body(o_ref, z_ref, g_ref, e_ref, dy_ref, do_ref, dz_ref, dg_ref):
        x = _load_heads(o_ref)
        zz = z_ref[...].astype(F32)
        e = e_ref[...]
        gain_v = g_ref[...]
        d = dy_ref[...]
        r = lax.rsqrt(_segsum(x * x, e) * (1.0 / HEAD_DIM) + RMS_EPS)
        xr = x * r
        don = d * _silu(zz)
        dz_ref[...] = (d * xr * gain_v * _silu_grad(zz)).astype(dz_ref.dtype)
        gy = don * gain_v
        _store_heads(do_ref, r * gy - xr * (r * r) * (_segsum(gy * x, e) * (1.0 / HEAD_DIM)))

        @pl.when(pl.program_id(0) == 0)
        def _():
            dg_ref[...] = jnp.zeros_like(dg_ref)

        dg_ref[...] += jnp.sum(don * xr, axis=0, keepdims=True)

    row = pl.BlockSpec((tt, WIDTH), lambda i: (i, 0))
    vec = pl.BlockSpec((1, WIDTH), lambda i: (0, 0))
    hm = pl.BlockSpec((HEADS, tt, HEAD_DIM), lambda i: (0, i, 0))
    return pl.pallas_call(
        body, grid=(t // tt,), name="gdn_post_bwd",
        in_specs=[hm, pl.BlockSpec((tt, WIDTH), lambda i: (i, 3)), vec,
                  pl.BlockSpec((WIDTH, WIDTH), lambda i: (0, 0)), row],
        out_specs=[hm, row, vec],
        out_shape=[jax.ShapeDtypeStruct((HEADS, t, HEAD_DIM), F32), jax.ShapeDtypeStruct((t, WIDTH), BF16),
                   jax.ShapeDtypeStruct((1, WIDTH), F32)],
        compiler_params=_params("arbitrary"),
    )(o, xg, gain, seg, dy)


def _mix_fwd(yf, yg, gates, bias):
    t, d = yf.shape
    tt = _pick(t, (320, 256, 128))

    def body(yf_ref, yg_ref, g1_ref, g2_ref, b1_ref, b2_ref, o_ref):
        g1 = jax.nn.sigmoid(g1_ref[...].astype(F32) + b1_ref[...])
        g2 = jax.nn.sigmoid(g2_ref[...].astype(F32) + b2_ref[...])
        o_ref[...] = (g1 * yf_ref[...].astype(F32) + g2 * yg_ref[...].astype(F32)).astype(o_ref.dtype)

    row = pl.BlockSpec((tt, d), lambda i: (i, 0))
    return pl.pallas_call(
        body, grid=(t // tt,), name="mix_fwd",
        in_specs=[row, row, row, pl.BlockSpec((tt, d), lambda i: (i, 1)),
                  pl.BlockSpec((1, d), lambda i: (0, 0)), pl.BlockSpec((1, d), lambda i: (0, 1))],
        out_specs=row, out_shape=jax.ShapeDtypeStruct((t, d), BF16),
        compiler_params=_params("arbitrary"),
    )(yf, yg, gates, gates, bias, bias)


def _mix_bwd(dmix, yf, yg, gates, bias):
    t, d = yf.shape
    tt = _pick(t, (320, 256, 128))

    def body(dm_ref, yf_ref, yg_ref, g1_ref, g2_ref, b1_ref, b2_ref, dyf_ref, dyg_ref, dg_ref, db_ref):
        dm = dm_ref[...].astype(F32)
        g1 = jax.nn.sigmoid(g1_ref[...].astype(F32) + b1_ref[...])
        g2 = jax.nn.sigmoid(g2_ref[...].astype(F32) + b2_ref[...])
        dyf_ref[...] = (dm * g1).astype(BF16)
        dyg_ref[...] = (dm * g2).astype(BF16)
        dgate = jnp.concatenate([dm * yf_ref[...].astype(F32) * g1 * (1.0 - g1),
                                 dm * yg_ref[...].astype(F32) * g2 * (1.0 - g2)], axis=1)
        dg_ref[...] = dgate.astype(BF16)

        @pl.when(pl.program_id(0) == 0)
        def _():
            db_ref[...] = jnp.zeros_like(db_ref)

        db_ref[...] += jnp.sum(dgate, axis=0, keepdims=True)

    row = pl.BlockSpec((tt, d), lambda i: (i, 0))
    wide = pl.BlockSpec((tt, 2 * d), lambda i: (i, 0))
    return pl.pallas_call(
        body, grid=(t // tt,), name="mix_bwd",
        in_specs=[row, row, row, row, pl.BlockSpec((tt, d), lambda i: (i, 1)),
                  pl.BlockSpec((1, d), lambda i: (0, 0)), pl.BlockSpec((1, d), lambda i: (0, 1))],
        out_specs=[row, row, wide, pl.BlockSpec((1, 2 * d), lambda i: (0, 0))],
        out_shape=[jax.ShapeDtypeStruct((t, d), BF16), jax.ShapeDtypeStruct((t, d), BF16),
                   jax.ShapeDtypeStruct((t, 2 * d), BF16), jax.ShapeDtypeStruct((1, 2 * d), F32)],
        compiler_params=_params("arbitrary"),
    )(dmix, yf, yg, gates, gates, bias, bias)


def _ffn_act_fwd(up, conv_w, conv_b):
    t, c = up.shape
    tt = 128

    def body(x_ref, p_ref, w_ref, b_ref, o_ref):
        first = pl.program_id(0) == 0

        def conv(cols):
            prev = jnp.where(first, 0.0, _prev8(p_ref, cols))
            return _causal_conv(x_ref[:, cols].astype(F32), prev, w_ref, FFN_CONV, cols) + b_ref[:, cols]

        for lo in range(0, D_FF, FFN_LANES):
            gate = conv(slice(lo, lo + FFN_LANES))
            val = conv(slice(D_FF + lo, D_FF + lo + FFN_LANES))
            o_ref[:, lo:lo + FFN_LANES] = (_silu(gate) * val).astype(o_ref.dtype)

    return pl.pallas_call(
        body, grid=(t // tt,), name="ffn_act_fwd",
        in_specs=[pl.BlockSpec((tt, c), lambda i: (i, 0)), _prev_spec(tt, c),
                  pl.BlockSpec((FFN_CONV, c), lambda i: (0, 0)), pl.BlockSpec((1, c), lambda i: (0, 0))],
        out_specs=pl.BlockSpec((tt, D_FF), lambda i: (i, 0)),
        out_shape=jax.ShapeDtypeStruct((t, D_FF), BF16),
        compiler_params=_params("arbitrary"),
    )(up, up, conv_w, conv_b)


def _ffn_act_bwd(up, conv_w, conv_b, dact):
    t, c = up.shape
    tt = 128
    nt = t // tt

    def body(x_ref, p_ref, w_ref, b_ref, da_ref, dx_ref, dw_ref, db_ref, carry_ref):
        step = pl.program_id(0)

        @pl.when(step == 0)
        def _():
            carry_ref[...] = jnp.zeros_like(carry_ref)
            dw_ref[...] = jnp.zeros_like(dw_ref)
            db_ref[...] = jnp.zeros_like(db_ref)

        def conv(cols):
            x = x_ref[:, cols].astype(F32)
            prev = jnp.where(step == nt - 1, 0.0, _prev8(p_ref, cols))
            return x, prev, _causal_conv(x, prev, w_ref, FFN_CONV, cols) + b_ref[:, cols]

        def back(cols, x, prev, du):
            dx = _causal_conv_bwd(x, prev, du, carry_ref[:, cols], w_ref, dw_ref, FFN_CONV, cols)
            dx_ref[:, cols] = dx.astype(dx_ref.dtype)
            db_ref[:, cols] += jnp.sum(du, axis=0, keepdims=True)
            carry_ref[:, cols] = du[:8]

        for lo in range(0, D_FF, FFN_LANES):
            gcols, vcols = slice(lo, lo + FFN_LANES), slice(D_FF + lo, D_FF + lo + FFN_LANES)
            xg, pg, gate = conv(gcols)
            xv, pv, val = conv(vcols)
            da = da_ref[:, gcols]
            back(gcols, xg, pg, da * val * _silu_grad(gate))
            back(vcols, xv, pv, da * _silu(gate))

    rev = lambda i: (nt - 1 - i, 0)
    return pl.pallas_call(
        body, grid=(nt,), name="ffn_act_bwd",
        in_specs=[pl.BlockSpec((tt, c), rev),
                  _prev_spec(tt, c, lambda i: nt - 1 - i),
                  pl.BlockSpec((FFN_CONV, c), lambda i: (0, 0)), pl.BlockSpec((1, c), lambda i: (0, 0)),
                  pl.BlockSpec((tt, D_FF), rev)],
        out_specs=[pl.BlockSpec((tt, c), rev), pl.BlockSpec((FFN_CONV, c), lambda i: (0, 0)),
                   pl.BlockSpec((1, c), lambda i: (0, 0))],
        out_shape=[jax.ShapeDtypeStruct((t, c), BF16), jax.ShapeDtypeStruct((FFN_CONV, c), F32),
                   jax.ShapeDtypeStruct((1, c), F32)],
        scratch_shapes=[pltpu.VMEM((8, c), F32)],
        compiler_params=_params("arbitrary"),
    )(up, up, conv_w, conv_b, dact)


def _final_loss(h2, target, gain, seq):
    t, d = h2.shape
    tr = _pick(t, (320, 256, 128))

    def body(h_ref, t_ref, g_ref, loss_ref, dh_ref, dhb_ref, dg_ref):
        i = pl.program_id(0)
        x = h_ref[...]
        gain_v = g_ref[...]
        r = lax.rsqrt(jnp.mean(x * x, axis=-1, keepdims=True) + RMS_EPS)
        xr = x * r
        rows = i * tr + lax.broadcasted_iota(jnp.int32, (tr, 1), 0)
        real = (rows >= N_META) & (rows < N_META + seq)
        err = jnp.where(real, xr * gain_v - t_ref[...], 0.0)
        dy = err * (1.0 / d)
        gy = dy * gain_v
        dh = r * (gy - xr * jnp.mean(gy * xr, axis=-1, keepdims=True))
        dh_ref[...] = dh
        dhb_ref[...] = dh.astype(BF16)

        @pl.when(i == 0)
        def _():
            loss_ref[...] = jnp.zeros_like(loss_ref)
            dg_ref[...] = jnp.zeros_like(dg_ref)

        part = jnp.sum(jnp.sum(err * err, axis=-1, keepdims=True), axis=0, keepdims=True)
        loss_ref[...] += jnp.broadcast_to(part * (0.5 / d), loss_ref.shape)
        dg_ref[...] += jnp.sum(dy * xr, axis=0, keepdims=True)

    row = pl.BlockSpec((tr, d), lambda i: (i, 0))
    vec = pl.BlockSpec((1, d), lambda i: (0, 0))
    return pl.pallas_call(
        body, grid=(t // tr,), name="final_loss",
        in_specs=[row, row, vec],
        out_specs=[pl.BlockSpec((1, LANES), lambda i: (0, 0)), row, row, vec],
        out_shape=[jax.ShapeDtypeStruct((1, LANES), F32), jax.ShapeDtypeStruct((t, d), F32),
                   jax.ShapeDtypeStruct((t, d), BF16), jax.ShapeDtypeStruct((1, d), F32)],
        compiler_params=_params("arbitrary"),
    )(h2, target, gain)


ADAM_TILE_BYTES = 1 << 20


def _adamw(w, m, v, grecv, name):
    r, cols = w.shape
    tr = r
    if r * cols * 4 > ADAM_TILE_BYTES:
        tr = max(d for d in range(8, r + 1, 8) if r % d == 0 and d * cols * 4 <= ADAM_TILE_BYTES)

    def body(w_ref, m_ref, v_ref, g_ref, go_ref, d_ref, mo_ref, vo_ref):
        g = g_ref[0].astype(F32)
        for s in range(1, N_DEV):
            g = g + g_ref[s].astype(F32)
        wv = w_ref[...]
        mn = ADAM_B1 * m_ref[...] + (1.0 - ADAM_B1) * g
        vn = ADAM_B2 * v_ref[...] + (1.0 - ADAM_B2) * (g * g)
        m_hat = mn / (1.0 - ADAM_B1 ** ADAM_STEP)
        v_hat = vn / (1.0 - ADAM_B2 ** ADAM_STEP)
        go_ref[...] = g
        d_ref[...] = -ADAM_LR * (m_hat / (jnp.sqrt(v_hat) + ADAM_EPS) + ADAM_WD * wv)
        mo_ref[...] = mn
        vo_ref[...] = vn

    row = pl.BlockSpec((tr, cols), lambda i: (i, 0))
    shp = jax.ShapeDtypeStruct((r, cols), F32)
    return pl.pallas_call(
        body, grid=(r // tr,), name=name,
        in_specs=[row, row, row, pl.BlockSpec((N_DEV, tr, cols), lambda i: (0, i, 0))],
        out_specs=[row] * 4, out_shape=[shp] * 4,
        compiler_params=_params("parallel"),
    )(w, m, v, grecv)


def _mesh_pos():
    return lax.axis_index("x"), lax.axis_index("y"), lax.axis_index("c")


def _all_gather(shards):
    n = len(shards)

    def body(*refs):
        x_refs, out_refs = refs[:n], refs[n:2 * n]
        send_sems, recv_sems, local_sems = refs[2 * n:]
        x, y, c = _mesh_pos()
        me, sibling = (x, y, c), (x, y, 1 - c)
        chips = [(1 - x, y), (x, 1 - y), (1 - x, 1 - y)]

        def slot(a, px, py, pc):
            return out_refs[a].at[4 * px + 2 * py + pc]

        def copy(a, kk, block, to, src=None):
            return pltpu.make_async_remote_copy(
                src_ref=slot(a, *block) if src is None else src, dst_ref=slot(a, *block),
                send_sem=send_sems.at[7 * a + kk], recv_sem=recv_sems.at[7 * a + kk],
                device_id=to, device_id_type=MESH_ID)

        mine = [pltpu.make_async_copy(x_refs[a], slot(a, *me), local_sems.at[a]) for a in range(n)]
        first = []
        for a in range(n):
            first.append(copy(a, 0, me, sibling, src=x_refs[a]))
            first += [copy(a, 1 + j, me, (*chip, c), src=x_refs[a]) for j, chip in enumerate(chips)]
        for cp in mine + first:
            cp.start()
        passed = []
        for j, chip in enumerate(chips):
            for a in range(n):
                copy(a, 1 + j, (*chip, c), me).wait_recv()
                passed.append(copy(a, 4 + j, (*chip, c), sibling))
                passed[-1].start()
        for a in range(n):
            copy(a, 0, sibling, me).wait_recv()
        for j, chip in enumerate(chips):
            for a in range(n):
                copy(a, 4 + j, (*chip, 1 - c), me).wait_recv()
        for cp in first + passed:
            cp.wait_send()
        for cp in mine:
            cp.wait()

    hbm = pl.BlockSpec(memory_space=pl.ANY)
    return pl.pallas_call(
        body, name="weight_all_gather", in_specs=[hbm] * n, out_specs=[hbm] * n,
        out_shape=[jax.ShapeDtypeStruct((N_DEV,) + s.shape, s.dtype) for s in shards],
        scratch_shapes=[pltpu.SemaphoreType.DMA((7 * n,)), pltpu.SemaphoreType.DMA((7 * n,)),
                        pltpu.SemaphoreType.DMA((n,))],
    )(*shards)


def _grad_exchange(blocks, small):
    n = len(blocks)

    def body(*refs):
        src_refs, dst_refs = refs[:n + 1], refs[n + 1:2 * n + 2]
        send_sems, recv_sems, local_sems = refs[2 * n + 2:]
        x, y, c = _mesh_pos()
        me = 4 * x + 2 * y + c
        copies = []
        for kk in range(1, N_DEV):
            px = 1 - x if kk & 4 else x
            py = 1 - y if kk & 2 else y
            pc = 1 - c if kk & 1 else c
            peer = 4 * px + 2 * py + pc
            for a in range(n + 1):
                copies.append(pltpu.make_async_remote_copy(
                    src_ref=src_refs[a].at[peer] if a < n else src_refs[a], dst_ref=dst_refs[a].at[me],
                    send_sem=send_sems.at[7 * a + kk - 1], recv_sem=recv_sems.at[7 * a + kk - 1],
                    device_id=(px, py, pc), device_id_type=MESH_ID))
        own = [pltpu.make_async_copy(src_refs[a].at[me] if a < n else src_refs[a], dst_refs[a].at[me],
                                     local_sems.at[a]) for a in range(n + 1)]
        for cp in own + copies:
            cp.start()
        for cp in copies + own:
            cp.wait()

    hbm = pl.BlockSpec(memory_space=pl.ANY)
    return pl.pallas_call(
        body, name="grad_exchange", in_specs=[hbm] * (n + 1), out_specs=[hbm] * (n + 1),
        out_shape=[jax.ShapeDtypeStruct(b.shape, b.dtype) for b in blocks]
        + [jax.ShapeDtypeStruct((N_DEV,) + small.shape, small.dtype)],
        scratch_shapes=[pltpu.SemaphoreType.DMA((7 * (n + 1),)), pltpu.SemaphoreType.DMA((7 * (n + 1),)),
                        pltpu.SemaphoreType.DMA((n + 1,))],
    )(*blocks, small)


def _exchange_copies(src_refs, land_refs, send_sems, recv_sems):
    x, y, c = _mesh_pos()
    me = 4 * x + 2 * y + c
    copies = []
    for kk in range(1, N_DEV):
        px = 1 - x if kk & 4 else x
        py = 1 - y if kk & 2 else y
        pc = 1 - c if kk & 1 else c
        for a, (src, land) in enumerate(zip(src_refs, land_refs)):
            copies.append(pltpu.make_async_remote_copy(
                src_ref=src.at[4 * px + 2 * py + pc], dst_ref=land.at[me],
                send_sem=send_sems.at[7 * a + kk - 1], recv_sem=recv_sems.at[7 * a + kk - 1],
                device_id=(px, py, pc), device_id_type=MESH_ID))
    return copies


def _gather_copies(src_refs, land_refs, send_sems, recv_sems):
    x, y, c = _mesh_pos()
    me = 4 * x + 2 * y + c
    copies = []
    for kk in range(1, N_DEV):
        px = 1 - x if kk & 4 else x
        py = 1 - y if kk & 2 else y
        pc = 1 - c if kk & 1 else c
        for a, (src, land) in enumerate(zip(src_refs, land_refs)):
            copies.append(pltpu.make_async_remote_copy(
                src_ref=src, dst_ref=land.at[me],
                send_sem=send_sems.at[7 * a + kk - 1], recv_sem=recv_sems.at[7 * a + kk - 1],
                device_id=(px, py, pc), device_id_type=MESH_ID))
    return copies


_HBM = pl.BlockSpec(memory_space=pltpu.HBM)
_SEM = pl.BlockSpec(memory_space=pltpu.SEMAPHORE)
_DATAFLOW = pltpu.SideEffectType.DATAFLOW_SIDE_EFFECTING


def _split_start(name, make_copies, sources, land_shapes):
    n = len(sources)

    def body(*refs):
        src_refs, land_refs, send_sems, recv_sems = refs[:n], refs[n:2 * n], refs[2 * n], refs[2 * n + 1]
        for cp in make_copies(src_refs, land_refs, send_sems, recv_sems):
            cp.start()
        token = refs[-1]
        token[...] = jnp.zeros_like(token)

    in_hbm = lambda a: pltpu.with_memory_space_constraint(a, pltpu.HBM)
    hbm_shapes = [pltpu.HBM(s.shape, s.dtype) for s in list(sources) + list(land_shapes)]
    outs = pl.pallas_call(
        body, name=name, in_specs=[_HBM] * (2 * n),
        out_shape=(pltpu.SemaphoreType.DMA((7 * n,)), pltpu.SemaphoreType.DMA((7 * n,)), *hbm_shapes,
                   jax.ShapeDtypeStruct((8, LANES), F32)),
        out_specs=(_SEM, _SEM, *[_HBM] * (2 * n), pl.BlockSpec(memory_space=pltpu.VMEM)),
        input_output_aliases={a: 2 + a for a in range(2 * n)},
        compiler_params=pltpu.CompilerParams(has_side_effects=_DATAFLOW),
    )(*[in_hbm(s) for s in sources], *[in_hbm(lax.empty(s.shape, s.dtype)) for s in land_shapes])
    return outs[0], outs[1], outs[2:2 + n], outs[2 + n:2 + 2 * n], outs[-1]


def _split_wait(name, make_copies, send_sems, recv_sems, src_thru, land_thru, after):
    n = len(src_thru)

    def body(*refs):
        src_refs, land_refs, send_sems, recv_sems = refs[:n], refs[n:2 * n], refs[2 * n], refs[2 * n + 1]
        for cp in make_copies(src_refs, land_refs, send_sems, recv_sems):
            cp.wait_send()
            cp.wait_recv()

    outs = pl.pallas_call(
        body, name=name,
        in_specs=[_HBM] * (2 * n) + [_SEM, _SEM, pl.BlockSpec(memory_space=pl.ANY)],
        out_shape=tuple(pltpu.HBM(b.shape, b.dtype) for b in list(src_thru) + list(land_thru)),
        out_specs=[_HBM] * (2 * n), input_output_aliases={a: a for a in range(2 * n)},
        compiler_params=pltpu.CompilerParams(has_side_effects=_DATAFLOW),
    )(*src_thru, *land_thru, send_sems, recv_sems, after)
    return outs[:n], outs[n:]


def _exchange_start(blocks):
    return _split_start("grad_exchange_start", _exchange_copies, blocks, blocks)


def _exchange_wait(send_sems, recv_sems, src_thru, land_thru, after):
    return _split_wait("grad_exchange_wait", _exchange_copies, send_sems, recv_sems, src_thru, land_thru, after)


def _gather_start(shards):
    lands = [jax.ShapeDtypeStruct((N_DEV,) + s.shape, s.dtype) for s in shards]
    return _split_start("weight_gather_start", _gather_copies, shards, lands)


def _gather_wait(send_sems, recv_sems, src_thru, land_thru, after):
    return _split_wait("weight_gather_wait", _gather_copies, send_sems, recv_sems, src_thru, land_thru, after)


def _pad_flat(parts, rows):
    flat = jnp.concatenate([p.reshape(-1) for p in parts])
    return jnp.pad(flat, (0, rows * LANES - flat.shape[0])).reshape(rows, LANES)


def _rows_for(n_elems, mult=1024):
    rows = -(-n_elems // LANES)
    return -(-rows // mult) * mult


SHARDED = ("meta_tokens", "w_in", "gdn_conv_w", "w_branch_fox", "w_branch_gdn", "w_out", "ffn_w_up", "ffn_conv_w",
           "ffn_w_down")
MATMUL = ("w_in", "w_branch_fox", "w_branch_gdn", "w_out", "ffn_w_up", "ffn_w_down")
EXACT = ("meta_tokens", "gdn_conv_w", "ffn_conv_w")
REPLICATED = ("fgt_bias", "gdn_a_log", "gdn_dt_bias", "gdn_norm_w", "gate_bias", "norm_mix_w", "norm_ffn_w",
              "ffn_conv_b", "norm_final_w")
WEIGHTS = ("meta_tokens", "w_in", "fgt_bias", "gdn_conv_w", "gdn_a_log", "gdn_dt_bias", "gdn_norm_w", "gate_bias",
           "w_branch_fox", "w_branch_gdn", "w_out", "norm_mix_w", "norm_ffn_w", "ffn_w_up", "ffn_conv_w",
           "ffn_conv_b", "ffn_w_down", "norm_final_w")


def _unpack(buf, shapes):
    flat = buf.reshape(-1)
    out, off = [], 0
    for s in shapes:
        n = int(np.prod(s))
        out.append(flat[off:off + n].reshape(s))
        off += n
    return out


def _unpack_gathered(buf, shapes):
    flat = buf.reshape(N_DEV, -1)
    out, off = [], 0
    for s in shapes:
        n = int(np.prod(s))
        out.append(flat[:, off:off + n].reshape((N_DEV,) + tuple(s)))
        off += n
    return out


def _cat_cols(g):
    return g.transpose(1, 0, 2).reshape(g.shape[1], -1)


def _col_blocks(full, width):
    return full.reshape(full.shape[0], N_DEV, width).transpose(1, 0, 2)


def _local_step(x, target, w, early=None, late_weights=None):
    seq = x.shape[0]
    t = _padded_tokens(seq)
    pad = t - N_META - seq
    seg = _seg_matrix()
    zrows = jnp.zeros((pad, D_MODEL), F32)
    h0 = jnp.concatenate([w["meta_tokens"], x, zrows], axis=0)
    tgt = jnp.concatenate([jnp.zeros((N_META, D_MODEL), F32), target, zrows], axis=0)

    w_in = w["w_in"]
    o_f, o_g, o_z, o_b, o_a, o_gate = 1536, 1544, 3080, 3592, 3600, 3608
    w_small = jnp.concatenate([w_in[:, o_f:o_f + 8], w_in[:, o_b:o_b + 8], w_in[:, o_a:o_a + 8],
                               jnp.zeros((D_MODEL, LANES - 24), BF16)], axis=1)
    w_r = jnp.concatenate([w_in[:, :1536], w_in[:, o_g:o_z], w_in[:, o_z:o_b], w_in[:, o_gate:], w_small], axis=1)

    a1 = _rmsnorm_fwd(h0, w["norm_mix_w"])
    fq = _mm(a1, w_r[:, :1536], BF16, "proj_fox")
    xg = _mm(a1, w_r[:, 1536:3584], BF16, "proj_gdn")
    gt = _mm(a1, w_r[:, 3584:5632], BF16, "proj_gates")
    sm = _mm(a1, w_r[:, 5632:], F32, "proj_small")

    lanes_pad = lambda a, lo: jnp.pad(a, ((0, 0), (lo, LANES - lo - a.shape[1])))
    neg_exp_a = -jnp.exp(w["gdn_a_log"])
    pbias = lanes_pad(w["fgt_bias"], 0) + lanes_pad(w["gdn_dt_bias"], 2 * HEADS)
    if late_weights is not None:
        pbias = pbias + late_weights[0][0, 0]
    pscale = lanes_pad(neg_exp_a, 2 * HEADS)
    scal = _gate_fwd(sm, pbias, pscale)
    gct = scal[:, 2 * HEADS:3 * HEADS].T

    qa, ka, va, kat, vat = _fox_prep(fq, scal)
    oa, qb, qbt = _fox_fwd(qa, ka, vat)
    o_fox = _fox_post(oa)

    qh, kh, vh = _gdn_pre_fwd(xg, w["gdn_conv_w"], seg)
    og, s0s = _gdn_fwd(qh, kh, vh, scal, gct)
    norm_w = jnp.tile(w["gdn_norm_w"], (1, HEADS))
    ogn = _gdn_post_fwd(og, xg, norm_w, seg)

    if late_weights is not None:
        w = {**w, **late_weights[1](ogn)}
    yf = _mm(o_fox, w["w_branch_fox"], BF16, "branch_fox")
    yg = _mm(ogn, w["w_branch_gdn"], BF16, "branch_gdn")
    mix = _mix_fwd(yf, yg, gt, w["gate_bias"])
    h1 = _mm(mix, w["w_out"], F32, "out_proj", res=h0)
    a2 = _rmsnorm_fwd(h1, w["norm_ffn_w"])
    up = _mm(a2, w["ffn_w_up"], BF16, "ffn_up")
    act = _ffn_act_fwd(up, w["ffn_conv_w"], w["ffn_conv_b"])
    h2 = _mm(act, w["ffn_w_down"], F32, "ffn_down", res=h1)
    loss, dh2, dh2b, g_final = _final_loss(h2, tgt, w["norm_final_w"].reshape(1, D_MODEL), seq)

    grads = {"norm_final_w": g_final.reshape(D_MODEL)}
    grads["ffn_w_down"] = _mm_tn(act, dh2b, "wgrad_ffn_down")
    dact = _mm(dh2b, w["ffn_w_down"].T, F32, "dgrad_ffn_down")
    dup, g_cw, g_cb = _ffn_act_bwd(up, w["ffn_conv_w"], w["ffn_conv_b"], dact)
    grads["ffn_conv_w"], grads["ffn_conv_b"] = g_cw, g_cb
    grads["ffn_w_up"] = _mm_tn(a2, dup, "wgrad_ffn_up")
    da2 = _mm(dup, w["ffn_w_up"].T, BF16, "dgrad_ffn_up")
    dh1, dh1b, grads["norm_ffn_w"] = _rmsnorm_bwd(h1, da2, w["norm_ffn_w"], dh2)
    grads["w_out"] = _mm_tn(mix, dh1b, "wgrad_out")
    dmix = _mm(dh1b, w["w_out"].T, BF16, "dgrad_out")
    dyf, dyg, dgt, grads["gate_bias"] = _mix_bwd(dmix, yf, yg, gt, w["gate_bias"])
    grads["w_branch_fox"] = _mm_tn(o_fox, dyf, "wgrad_branch_fox")
    grads["w_branch_gdn"] = _mm_tn(ogn, dyg, "wgrad_branch_gdn")
    do_fox = _mm(dyf, w["w_branch_fox"].T, F32, "dgrad_branch_fox")
    dogn = _mm(dyg, w["w_branch_gdn"].T, F32, "dgrad_branch_gdn")

    dog, dz, g_nw = _gdn_post_bwd(og, xg, norm_w, seg, dogn)
    grads["gdn_norm_w"] = g_nw.reshape(HEADS, HEAD_DIM).sum(axis=0)[None]
    dqh, dkh, dvh, dscal_g, dgct = _gdn_bwd(qh, kh, vh, scal, gct, s0s, dog)
    dxg, grads["gdn_conv_w"] = _gdn_pre_bwd(xg, w["gdn_conv_w"], seg, dqh, dkh, dvh)

    doa, doat = _fox_bwd_prep(do_fox, oa)
    dfq, dscal_c = _fox_bwd_post(*_fox_bwd(qb, qbt, ka, kat, va, doa, doat))

    dscal = dscal_c + dscal_g + lanes_pad(dgct.T, 2 * HEADS)
    dsm, dpb, dps = _gate_bwd(sm, pbias, pscale, dscal)
    grads["fgt_bias"] = dpb[:, :HEADS]
    grads["gdn_dt_bias"] = dpb[:, 2 * HEADS:3 * HEADS]
    grads["gdn_a_log"] = dps[:, 2 * HEADS:3 * HEADS] * neg_exp_a

    dproj = jnp.concatenate([dfq, dxg, dz, dgt, dsm], axis=1)
    g_r = _mm_tn(a1, dproj, "wgrad_in")
    grads["w_in"] = jnp.concatenate([g_r[:, :1536], g_r[:, 5632:5640], g_r[:, 1536:3072], g_r[:, 3072:3584],
                                     g_r[:, 5640:5648], g_r[:, 5648:5656], g_r[:, 3584:5632]], axis=1)
    token, handle = early(grads) if early is not None else (jnp.zeros((8, LANES), F32), None)
    w_rt = w_r.T + token[0, 0].astype(BF16)
    da1 = _mm(dproj, w_rt, BF16, "dgrad_in")
    dh0, _, grads["norm_mix_w"] = _rmsnorm_bwd(h0, da1, w["norm_mix_w"], dh1)
    grads["meta_tokens"] = dh0[:N_META]
    return loss, dh0[N_META:N_META + seq], grads, handle


def _shard_pieces(arrs):
    return [arrs[n][0] if arrs[n].ndim == 3 else arrs[n] for n in SHARDED]


def _full_grad_blocks(grads):
    g = grads
    cols = lambda a, wd: _col_blocks(a, wd)
    rows = lambda a: a.reshape(N_DEV, a.shape[0] // N_DEV, a.shape[1])
    return [cols(g["w_in"], IN_WIDTH // N_DEV), cols(g["gdn_conv_w"], 3 * WIDTH // N_DEV),
            cols(g["w_branch_fox"], D_MODEL // N_DEV), cols(g["w_branch_gdn"], D_MODEL // N_DEV), rows(g["w_out"]),
            cols(g["ffn_w_up"], 2 * D_FF // N_DEV), cols(g["ffn_conv_w"], 2 * D_FF // N_DEV), rows(g["ffn_w_down"])]


def kernel(x, meta_tokens, w_in, fgt_bias, gdn_conv_w, gdn_a_log, gdn_dt_bias, gdn_norm_w, gate_bias, w_branch_fox, w_branch_gdn, w_out, norm_mix_w, norm_ffn_w, ffn_w_up, ffn_conv_w, ffn_conv_b, ffn_w_down, norm_final_w, loss_target, m_meta_tokens, m_w_in, m_fgt_bias, m_gdn_conv_w, m_gdn_a_log, m_gdn_dt_bias, m_gdn_norm_w, m_gate_bias, m_w_branch_fox, m_w_branch_gdn, m_w_out, m_norm_mix_w, m_norm_ffn_w, m_ffn_w_up, m_ffn_conv_w, m_ffn_conv_b, m_ffn_w_down, m_norm_final_w, v_meta_tokens, v_w_in, v_fgt_bias, v_gdn_conv_w, v_gdn_a_log, v_gdn_dt_bias, v_gdn_norm_w, v_gate_bias, v_w_branch_fox, v_w_branch_gdn, v_w_out, v_norm_mix_w, v_norm_ffn_w, v_ffn_w_up, v_ffn_conv_w, v_ffn_conv_b, v_ffn_w_down, v_norm_final_w):
    wts = dict(meta_tokens=meta_tokens, w_in=w_in, fgt_bias=fgt_bias, gdn_conv_w=gdn_conv_w, gdn_a_log=gdn_a_log,
               gdn_dt_bias=gdn_dt_bias, gdn_norm_w=gdn_norm_w, gate_bias=gate_bias, w_branch_fox=w_branch_fox,
               w_branch_gdn=w_branch_gdn, w_out=w_out, norm_mix_w=norm_mix_w, norm_ffn_w=norm_ffn_w,
               ffn_w_up=ffn_w_up, ffn_conv_w=ffn_conv_w, ffn_conv_b=ffn_conv_b, ffn_w_down=ffn_w_down,
               norm_final_w=norm_final_w)
    mom = dict(meta_tokens=m_meta_tokens, w_in=m_w_in, fgt_bias=m_fgt_bias, gdn_conv_w=m_gdn_conv_w,
               gdn_a_log=m_gdn_a_log, gdn_dt_bias=m_gdn_dt_bias, gdn_norm_w=m_gdn_norm_w, gate_bias=m_gate_bias,
               w_branch_fox=m_w_branch_fox, w_branch_gdn=m_w_branch_gdn, w_out=m_w_out, norm_mix_w=m_norm_mix_w,
               norm_ffn_w=m_norm_ffn_w, ffn_w_up=m_ffn_w_up, ffn_conv_w=m_ffn_conv_w, ffn_conv_b=m_ffn_conv_b,
               ffn_w_down=m_ffn_w_down, norm_final_w=m_norm_final_w)
    var = dict(meta_tokens=v_meta_tokens, w_in=v_w_in, fgt_bias=v_fgt_bias, gdn_conv_w=v_gdn_conv_w,
               gdn_a_log=v_gdn_a_log, gdn_dt_bias=v_gdn_dt_bias, gdn_norm_w=v_gdn_norm_w, gate_bias=v_gate_bias,
               w_branch_fox=v_w_branch_fox, w_branch_gdn=v_w_branch_gdn, w_out=v_w_out, norm_mix_w=v_norm_mix_w,
               norm_ffn_w=v_norm_ffn_w, ffn_w_up=v_ffn_w_up, ffn_conv_w=v_ffn_conv_w, ffn_conv_b=v_ffn_conv_b,
               ffn_w_down=v_ffn_w_down, norm_final_w=v_norm_final_w)

    sh = dict(zip(SHARDED, _shard_pieces(wts)))
    me = 4 * lax.axis_index("x") + 2 * lax.axis_index("y") + lax.axis_index("c")
    late_names = MATMUL[1:]
    late_sems_send, late_sems_recv, late_src, late_land, late_token = _gather_start(
        [sh[n].astype(BF16) for n in late_names])
    exact_shapes = [sh[n].shape for n in EXACT]
    rows_exact = _rows_for(sum(int(np.prod(s)) for s in exact_shapes), 8)
    g_in, g_exact = _all_gather([sh["w_in"].astype(BF16), _pad_flat([sh[n] for n in EXACT], rows_exact)])
    meta_full, conv_full, fconv_full = (_cat_cols(a) for a in _unpack_gathered(g_exact, exact_shapes))
    full = dict(
        meta_tokens=meta_full, w_in=_cat_cols(g_in), gdn_conv_w=conv_full, ffn_conv_w=fconv_full,
        fgt_bias=fgt_bias, gdn_a_log=gdn_a_log, gdn_dt_bias=gdn_dt_bias, gdn_norm_w=gdn_norm_w, gate_bias=gate_bias,
        norm_mix_w=norm_mix_w, norm_ffn_w=norm_ffn_w, ffn_conv_b=ffn_conv_b, norm_final_w=norm_final_w)

    def fetch_late_weights(after):
        shards, lands = _gather_wait(late_sems_send, late_sems_recv, late_src, late_land, after)
        g_bf, g_bg, g_out, g_up, g_down = (lax.dynamic_update_slice_in_dim(land, s[None], me, 0)
                                           for s, land in zip(shards, lands))
        return dict(w_branch_fox=_cat_cols(g_bf), w_branch_gdn=_cat_cols(g_bg), w_out=g_out.reshape(D_MODEL, D_MODEL),
                    ffn_w_up=_cat_cols(g_up), ffn_w_down=g_down.reshape(D_FF, D_MODEL))

    def start_exchange(grads_so_far):
        blocks = [b.astype(BF16) for b in _full_grad_blocks(grads_so_far)]
        send_sems, recv_sems, src_thru, land_thru, token = _exchange_start(blocks)
        return token, (send_sems, recv_sems, src_thru, land_thru)

    loss, grad_x, grads, handle = _local_step(x[0], loss_target[0], full, early=start_exchange,
                                              late_weights=(late_token, fetch_late_weights))
    sent, landed = _exchange_wait(*handle, after=grad_x)
    own = lambda src, land: lax.dynamic_update_slice_in_dim(land, lax.dynamic_slice_in_dim(src, me, 1, 0), me, 0)
    received = [own(src, land) for src, land in zip(sent, landed)]

    rep_parts = [grads[n] for n in REPLICATED] + [loss[:, :1]]
    rep_shapes = [wts[n].shape for n in REPLICATED]
    rows_small = _rows_for(sum(int(np.prod(p.shape)) for p in rep_parts), 8)
    meta_recv, small_recv = _grad_exchange([_col_blocks(grads["meta_tokens"], LANES).astype(BF16)],
                                           _pad_flat(rep_parts, rows_small))
    received = [meta_recv] + received + [small_recv]

    result = {}
    kinds = ("grad", "delta", "new_m", "new_v")
    for n, recv in zip(SHARDED, received[:-1]):
        outs = _adamw(sh[n], _shard_pieces(mom)[SHARDED.index(n)], _shard_pieces(var)[SHARDED.index(n)], recv,
                      "adamw_" + n)
        for kind, a in zip(kinds, outs):
            result[kind, n] = a.reshape(wts[n].shape)
    rep_w = _pad_flat([wts[n] for n in REPLICATED] + [jnp.zeros((1, 1), F32)], rows_small)
    rep_m = _pad_flat([mom[n] for n in REPLICATED] + [jnp.zeros((1, 1), F32)], rows_small)
    rep_v = _pad_flat([var[n] for n in REPLICATED] + [jnp.ones((1, 1), F32)], rows_small)
    outs_r = _adamw(rep_w, rep_m, rep_v, received[-1], "adamw_replicated")
    for kind, br in zip(kinds, outs_r):
        for n, a in zip(REPLICATED, _unpack(br, rep_shapes)):
            result[kind, n] = a
    n_rep = sum(int(np.prod(s)) for s in rep_shapes)
    total_loss = outs_r[0].reshape(-1)[n_rep]
    out = [total_loss, grad_x[None]]
    for kind in ("grad", "delta", "new_m", "new_v"):
        out += [result[kind, n] for n in WEIGHTS]
    return tuple(out)
```

```python
import jax
import jax.numpy as jnp
import numpy as np
from jax import lax
from jax.experimental import pallas as pl
from jax.experimental.pallas import tpu as pltpu

F32 = jnp.float32
BF16 = jnp.bfloat16

D_MODEL = 1024
N_META = 16
HEADS = 8
HEAD_DIM = 64
WIDTH = HEADS * HEAD_DIM
CHUNK = 64
GDN_CONV = 4
D_FF = 2816
FFN_CONV = 3
IN_WIDTH = 5656
RMS_EPS = 1e-6
NEG = -1e30
AUG = 128
N_DEV = 8
LANES = 128

ADAM_LR = 0.001
ADAM_B1 = 0.9
ADAM_B2 = 0.999
ADAM_EPS = 1e-08
ADAM_WD = 0.01
ADAM_STEP = 10

VMEM_LIMIT = 56 * 1024 * 1024
MM_VMEM_BUDGET = 36 * 1024 * 1024
FFN_LANES = 128
HI = lax.Precision.HIGH
MESH_ID = pl.DeviceIdType.MESH


def _pick(n, cands):
    for c in cands:
        if n % c == 0:
            return c
    raise ValueError(f"no tile for {n} in {cands}")


def _params(*sem):
    return pltpu.CompilerParams(dimension_semantics=sem if sem else None, vmem_limit_bytes=VMEM_LIMIT)


def _padded_tokens(seq):
    t = -(-(N_META + seq) // 128) * 128
    if t > 1280 and t % 640:
        t = -(-t // 640) * 640
    return t


ROW_TILES = (640, 512, 384, 256, 128)


def _rmsnorm_fwd(h, gain):
    t, d = h.shape
    tr = _pick(t, ROW_TILES)

    def body(h_ref, g_ref, o_ref):
        x = h_ref[...]
        r = lax.rsqrt(jnp.mean(x * x, axis=-1, keepdims=True) + RMS_EPS)
        o_ref[...] = (x * r * g_ref[...]).astype(o_ref.dtype)

    return pl.pallas_call(
        body, grid=(t // tr,), name="rmsnorm_fwd",
        in_specs=[pl.BlockSpec((tr, d), lambda i: (i, 0)), pl.BlockSpec((1, d), lambda i: (0, 0))],
        out_specs=pl.BlockSpec((tr, d), lambda i: (i, 0)),
        out_shape=jax.ShapeDtypeStruct((t, d), BF16),
        compiler_params=_params("arbitrary"),
    )(h, gain)


def _rmsnorm_bwd(h, dy, gain, dres):
    t, d = h.shape
    tr = _pick(t, (320, 256, 128))

    def body(h_ref, dy_ref, g_ref, dres_ref, dh_ref, dhb_ref, dg_ref):
        x = h_ref[...]
        dyv = dy_ref[...].astype(F32)
        r = lax.rsqrt(jnp.mean(x * x, axis=-1, keepdims=True) + RMS_EPS)
        gy = dyv * g_ref[...]
        m = jnp.mean(gy * x, axis=-1, keepdims=True)
        dh = dres_ref[...] + r * gy - x * (r * r * r * m)
        dh_ref[...] = dh
        dhb_ref[...] = dh.astype(BF16)

        @pl.when(pl.program_id(0) == 0)
        def _():
            dg_ref[...] = jnp.zeros_like(dg_ref)

        dg_ref[...] += jnp.sum(dyv * x * r, axis=0, keepdims=True)

    row = pl.BlockSpec((tr, d), lambda i: (i, 0))
    vec = pl.BlockSpec((1, d), lambda i: (0, 0))
    return pl.pallas_call(
        body, grid=(t // tr,), name="rmsnorm_bwd",
        in_specs=[row, row, vec, row], out_specs=[row, row, vec],
        out_shape=[jax.ShapeDtypeStruct((t, d), F32), jax.ShapeDtypeStruct((t, d), BF16),
                   jax.ShapeDtypeStruct((1, d), F32)],
        compiler_params=_params("arbitrary"),
    )(h, dy, gain, dres)


def _mm(a, b, out_dtype, name, res=None, b_transposed=False):
    m, k = a.shape
    n = b.shape[0] if b_transposed else b.shape[1]
    tm = _pick(m, ROW_TILES)
    out_bytes = jnp.dtype(out_dtype).itemsize + (4 if res is not None else 0)
    fits = lambda tn: 4 * tm * k + 4 * k * tn + 2 * tm * tn * out_bytes <= MM_VMEM_BUDGET
    tn = next(c for c in (n, 2816, 2048, 1536, 1408, 1024, 512, 384, 256, 128) if n % c == 0 and fits(c))

    def body(*refs):
        if res is None:
            a_ref, b_ref, o_ref = refs
        else:
            a_ref, b_ref, r_ref, o_ref = refs
        out = lax.dot_general(a_ref[...], b_ref[...], (((1,), (1 if b_transposed else 0,)), ((), ())),
                              preferred_element_type=F32)
        if res is not None:
            out = out + r_ref[...]
        o_ref[...] = out.astype(o_ref.dtype)

    b_spec = pl.BlockSpec((tn, k), lambda i, j: (j, 0)) if b_transposed else pl.BlockSpec((k, tn), lambda i, j: (0, j))
    in_specs = [pl.BlockSpec((tm, k), lambda i, j: (i, 0)), b_spec]
    args = [a, b]
    if res is not None:
        in_specs.append(pl.BlockSpec((tm, tn), lambda i, j: (i, j)))
        args.append(res)
    return pl.pallas_call(
        body, grid=(m // tm, n // tn), name=name,
        in_specs=in_specs, out_specs=pl.BlockSpec((tm, tn), lambda i, j: (i, j)),
        out_shape=jax.ShapeDtypeStruct((m, n), out_dtype),
        compiler_params=_params("parallel", "parallel"),
    )(*args)


def _mm_tn(a, g, name):
    t, k = a.shape
    _, n = g.shape
    tk = _pick(k, (1024, 1408, 512))
    tn = _pick(n, (512, 640, 384, 256, 128))
    tt = next(c for c in (3328, 1280) + ROW_TILES
              if t % c == 0 and 4 * c * (tk + tn) + 8 * tk * tn <= MM_VMEM_BUDGET)
    nt = t // tt

    def body(a_ref, g_ref, o_ref):
        @pl.when(pl.program_id(2) == 0)
        def _():
            o_ref[...] = jnp.zeros_like(o_ref)

        o_ref[...] += lax.dot_general(a_ref[...], g_ref[...], (((0,), (0,)), ((), ())),
                                      preferred_element_type=F32)

    return pl.pallas_call(
        body, grid=(k // tk, n // tn, nt), name=name,
        in_specs=[pl.BlockSpec((tt, tk), lambda i, j, s: (s, i)), pl.BlockSpec((tt, tn), lambda i, j, s: (s, j))],
        out_specs=pl.BlockSpec((tk, tn), lambda i, j, s: (i, j)),
        out_shape=jax.ShapeDtypeStruct((k, n), F32),
        compiler_params=_params("parallel", "parallel", "arbitrary"),
    )(a, g)


def _split3_exact(x):
    def top(v):
        return lax.bitcast_convert_type(lax.bitcast_convert_type(v, jnp.int32) & jnp.int32(-65536), F32)

    hi = top(x)
    r1 = x - hi
    mid = top(r1)
    return hi, mid, r1 - mid


def _pair_head(ref, h, rows):
    x = ref[:, 128 * (h // 2):128 * (h // 2) + 128].astype(F32)
    return pltpu.roll(x, HEAD_DIM, axis=1) if h % 2 else x


def _lanes(rows):
    return lax.broadcasted_iota(jnp.int32, (rows, AUG), 1)


def _fox_prep(fq, scal):
    t = fq.shape[0]
    tt = _pick(t, (256, 128))

    def body(q_ref, k_ref, v_ref, s_ref, qa_ref, ka_ref, va_ref, kt_ref, vt_ref):
        lane = _lanes(tt)
        chi, cmid, clo = _split3_exact(s_ref[...])
        ones = lambda lo: jnp.where((lane >= lo) & (lane < lo + 3), 1.0, 0.0)
        for h in range(HEADS):
            col = lambda a: jnp.broadcast_to(a[:, h:h + 1], (tt, AUG))
            c1, c2, c3 = col(chi), col(cmid), col(clo)
            qx = jnp.where(lane == 64, c1, jnp.where(lane == 65, c2, jnp.where(lane == 66, c3, ones(67))))
            kx = jnp.where(lane == 67, -c1, jnp.where(lane == 68, -c2, jnp.where(lane == 69, -c3, ones(64) + ones(70))))
            qa_ref[h] = jnp.where(lane < HEAD_DIM, _pair_head(q_ref, h, tt) * (HEAD_DIM ** -0.5), qx).astype(BF16)
            k_aug = jnp.where(lane < HEAD_DIM, _pair_head(k_ref, h, tt), kx)
            ka_ref[h] = k_aug.astype(BF16)
            kt_ref[h] = k_aug.T.astype(BF16)
            v_aug = jnp.where(lane < HEAD_DIM, _pair_head(v_ref, h, tt), ones(64))
            va_ref[h] = v_aug.astype(BF16)
            vt_ref[h] = v_aug.T.astype(BF16)

    out = pl.BlockSpec((HEADS, tt, AUG), lambda i: (0, i, 0))
    out_t = pl.BlockSpec((HEADS, AUG, tt), lambda i: (0, 0, i))
    shp = jax.ShapeDtypeStruct((HEADS, t, AUG), BF16)
    shp_t = jax.ShapeDtypeStruct((HEADS, AUG, t), BF16)
    return pl.pallas_call(
        body, grid=(t // tt,), name="fox_prep",
        in_specs=[pl.BlockSpec((tt, WIDTH), lambda i: (i, 0)), pl.BlockSpec((tt, WIDTH), lambda i: (i, 1)),
                  pl.BlockSpec((tt, WIDTH), lambda i: (i, 2)), pl.BlockSpec((tt, LANES), lambda i: (i, 0))],
        out_specs=[out, out, out, out_t, out_t], out_shape=[shp, shp, shp, shp_t, shp_t],
        compiler_params=_params("parallel"),
    )(fq, fq, fq, scal)


def _fox_post(oa):
    t = oa.shape[1]
    tt = _pick(t, (256, 128))

    def body(o_ref, out_ref):
        out_ref[...] = jnp.concatenate([o_ref[h][:, :HEAD_DIM] for h in range(HEADS)], axis=1).astype(BF16)

    return pl.pallas_call(
        body, grid=(t // tt,), name="fox_post",
        in_specs=[pl.BlockSpec((HEADS, tt, AUG), lambda i: (0, i, 0))],
        out_specs=pl.BlockSpec((tt, WIDTH), lambda i: (i, 0)),
        out_shape=jax.ShapeDtypeStruct((t, WIDTH), BF16),
        compiler_params=_params("parallel"),
    )(oa)


def _fox_bwd_prep(do, oa):
    t = do.shape[0]
    tt = _pick(t, (256, 128))

    def body(d_ref, o_ref, out_ref, outt_ref):
        lane = _lanes(tt)
        for h in range(HEADS):
            x = _pair_head(d_ref, h, tt)
            delta = jnp.sum(jnp.where(lane < HEAD_DIM, x * o_ref[h], 0.0), axis=1, keepdims=True)
            hi, mid, lo = _split3_exact(jnp.broadcast_to(-delta, (tt, AUG)))
            ex = jnp.where(lane == 64, hi, jnp.where(lane == 65, mid, jnp.where(lane == 66, lo, 0.0)))
            do_aug = jnp.where(lane < HEAD_DIM, x, ex)
            out_ref[h] = do_aug.astype(BF16)
            outt_ref[h] = do_aug.T.astype(BF16)

    hm = pl.BlockSpec((HEADS, tt, AUG), lambda i: (0, i, 0))
    return pl.pallas_call(
        body, grid=(t // tt,), name="fox_bwd_prep",
        in_specs=[pl.BlockSpec((tt, WIDTH), lambda i: (i, 0)), hm],
        out_specs=[hm, pl.BlockSpec((HEADS, AUG, tt), lambda i: (0, 0, i))],
        out_shape=[jax.ShapeDtypeStruct((HEADS, t, AUG), BF16), jax.ShapeDtypeStruct((HEADS, AUG, t), BF16)],
        compiler_params=_params("parallel"),
    )(do, oa)


def _fox_bwd_post(dqt, dkt, dvt):
    t = dqt.shape[2]
    tt = _pick(t, (256, 128))

    def body(dq_ref, dk_ref, dv_ref, out_ref, dsc_ref):
        lane = _lanes(tt)
        dqs = [dq_ref[h].T for h in range(HEADS)]
        dks = [dk_ref[h].T for h in range(HEADS)]
        heads = lambda xs: jnp.concatenate([x[:, :HEAD_DIM] for x in xs], axis=1)
        out_ref[:, 0:WIDTH] = (heads(dqs) * (HEAD_DIM ** -0.5)).astype(BF16)
        out_ref[:, WIDTH:2 * WIDTH] = heads(dks).astype(BF16)
        out_ref[:, 2 * WIDTH:] = heads([dv_ref[h].T for h in range(HEADS)]).astype(BF16)
        dsc = jnp.zeros((tt, LANES), F32)
        for h in range(HEADS):
            both = jnp.where(lane == HEAD_DIM, dqs[h], 0.0) - jnp.where(lane == HEAD_DIM + 3, dks[h], 0.0)
            dsc = jnp.where(lane == h, jnp.sum(both, axis=1, keepdims=True), dsc)
        dsc_ref[...] = dsc

    hm = pl.BlockSpec((HEADS, AUG, tt), lambda i: (0, 0, i))
    return pl.pallas_call(
        body, grid=(t // tt,), name="fox_bwd_post",
        in_specs=[hm, hm, hm],
        out_specs=[pl.BlockSpec((tt, 3 * WIDTH), lambda i: (i, 0)), pl.BlockSpec((tt, LANES), lambda i: (i, 0))],
        out_shape=[jax.ShapeDtypeStruct((t, 3 * WIDTH), BF16), jax.ShapeDtypeStruct((t, LANES), F32)],
        compiler_params=_params("parallel"),
    )(dqt, dkt, dvt)


def _fox_fwd(qa, ka, vat, tq=None):
    h, t, _ = qa.shape
    tq = tq or _pick(t, ROW_TILES)

    def body(q_ref, k_ref, vt_ref, o_ref, qb_ref, qbt_ref, s_ref):
        i = pl.program_id(1)
        q = q_ref[...]
        krow = lax.broadcasted_iota(jnp.int32, (tq, tq), 0)
        qcol = lax.broadcasted_iota(jnp.int32, (tq, tq), 1)
        rows = lambda j: pl.ds(pl.multiple_of(j * tq, tq), tq)

        def scores(j, slot):
            s_ref[slot] = lax.dot_general(k_ref[rows(j), :], q, (((1,), (1,)), ((), ())), preferred_element_type=F32)

        def update(j, slot, carry, masked):
            m, acc = carry
            s = s_ref[slot]
            if masked:
                s = jnp.where(qcol >= krow, s, NEG)
            m_new = jnp.maximum(m, jnp.max(s, axis=0, keepdims=True))
            p = jnp.exp(s - m_new)
            alpha = jnp.exp(m - m_new)
            return m_new, acc * alpha + jnp.dot(vt_ref[:, rows(j)], p.astype(BF16), preferred_element_type=F32)

        def pair(j, carry):
            scores(j + 1, 1)
            carry = update(j, 0, carry, False)
            scores(j + 2, 0)
            return update(j + 1, 1, carry, False)

        def odd_tail(carry):
            scores(i, 1)
            return update(i, 1, update(i - 1, 0, carry, False), True)

        scores(0, 0)
        carry = (jnp.full((1, tq), NEG, F32), jnp.zeros((AUG, tq), F32))
        carry = lax.fori_loop(0, i // 4, lambda jj, c: pair(4 * jj + 2, pair(4 * jj, c)), carry)
        carry = lax.fori_loop(0, (i % 4) // 2, lambda jj, c: pair(4 * (i // 4), c), carry)
        m, acc = lax.cond(i % 2 == 1, odd_tail, lambda c: update(i, 0, c, True), carry)
        sub = lax.broadcasted_iota(jnp.int32, (AUG, tq), 0)
        l = jnp.sum(jnp.where(sub == HEAD_DIM, acc, 0.0), axis=0, keepdims=True)
        out = jnp.where(sub < HEAD_DIM, acc / l, m + jnp.log(l)).T
        o_ref[...] = out
        lane = lax.broadcasted_iota(jnp.int32, (tq, AUG), 1)
        lse = jnp.broadcast_to(jnp.sum(jnp.where(lane == HEAD_DIM, out, 0.0), axis=1, keepdims=True), (tq, AUG))
        hi, mid, lo = _split3_exact(-lse)
        qb = jnp.where(lane == 70, hi, jnp.where(lane == 71, mid, jnp.where(lane == 72, lo, q.astype(F32))))
        qb_ref[...] = qb.astype(BF16)
        qbt_ref[...] = qb.T.astype(BF16)

    blk = pl.BlockSpec((None, tq, AUG), lambda hh, i: (hh, i, 0))
    return pl.pallas_call(
        body, grid=(h, t // tq), name="fox_fwd",
        in_specs=[blk, pl.BlockSpec((None, t, AUG), lambda hh, i: (hh, 0, 0)),
                  pl.BlockSpec((None, AUG, t), lambda hh, i: (hh, 0, 0))],
        out_specs=[blk, blk, pl.BlockSpec((None, AUG, tq), lambda hh, i: (hh, 0, i))],
        out_shape=[jax.ShapeDtypeStruct((h, t, AUG), F32), jax.ShapeDtypeStruct((h, t, AUG), BF16),
                   jax.ShapeDtypeStruct((h, AUG, t), BF16)],
        scratch_shapes=[pltpu.VMEM((2, tq, tq), F32)],
        compiler_params=_params("parallel", "arbitrary"),
    )(qa, ka, vat)


def _fox_bwd(qb, qbt, ka, kat, va, doa, doat, tq=None):
    h, t, _ = qb.shape
    tq = tq or _pick(t, ROW_TILES)
    nq = t // tq

    def body(q_ref, qt_ref, k_ref, kt_ref, v_ref, do_ref, dot_ref, dqt_ref, dkt_ref, dvt_ref, s_ref, dp_ref):
        j = pl.program_id(1)
        n = nq - j

        @pl.when(j == 0)
        def _():
            dqt_ref[...] = jnp.zeros_like(dqt_ref)

        dkt_ref[...] = jnp.zeros_like(dkt_ref)
        dvt_ref[...] = jnp.zeros_like(dvt_ref)
        kj = k_ref[...]
        ktj = kt_ref[...]
        vj = v_ref[...]
        qrow = lax.broadcasted_iota(jnp.int32, (tq, tq), 0)
        kcol = lax.broadcasted_iota(jnp.int32, (tq, tq), 1)
        rows = lambda i: pl.ds(pl.multiple_of(i * tq, tq), tq)
        nt_dims = (((1,), (1,)), ((), ()))

        def scores(i, slot):
            s_ref[slot] = lax.dot_general(q_ref[rows(i), :], kj, nt_dims, preferred_element_type=F32)
            dp_ref[slot] = lax.dot_general(do_ref[rows(i), :], vj, nt_dims, preferred_element_type=F32)

        def update(i, slot):
            p = jnp.exp(jnp.where((qrow >= kcol) | (i > j), s_ref[slot], NEG))
            ds = (p * dp_ref[slot]).astype(BF16)
            dvt_ref[...] += jnp.dot(dot_ref[:, rows(i)], p.astype(BF16), preferred_element_type=F32)
            dkt_ref[...] += jnp.dot(qt_ref[:, rows(i)], ds, preferred_element_type=F32)
            dqt_ref[:, rows(i)] += lax.dot_general(ktj, ds, nt_dims, preferred_element_type=F32)

        def pair(i0):
            scores(i0 + 1, 1)
            update(i0, 0)
            scores(jnp.minimum(i0 + 2, nq - 1), 0)
            update(i0 + 1, 1)

        def quad(kk, carry):
            pair(j + 4 * kk)
            pair(j + 4 * kk + 2)
            return carry

        def last_pair(kk, carry):
            pair(j + 4 * (n // 4))
            return carry

        scores(j, 0)
        lax.fori_loop(0, n // 4, quad, 0)
        lax.fori_loop(0, (n % 4) // 2, last_pair, 0)

        @pl.when(n % 2 == 1)
        def _():
            update(nq - 1, 0)

    once = pl.Buffered(1)
    full = pl.BlockSpec((None, t, AUG), lambda hh, j: (hh, 0, 0), pipeline_mode=once)
    full_t = pl.BlockSpec((None, AUG, t), lambda hh, j: (hh, 0, 0), pipeline_mode=once)
    blk = pl.BlockSpec((None, tq, AUG), lambda hh, j: (hh, j, 0))
    blk_t = pl.BlockSpec((None, AUG, tq), lambda hh, j: (hh, 0, j))
    shp = jax.ShapeDtypeStruct((h, AUG, t), F32)
    return pl.pallas_call(
        body, grid=(h, nq), name="fox_bwd",
        in_specs=[full, full_t, blk, blk_t, blk, full, full_t],
        out_specs=[pl.BlockSpec((None, AUG, t), lambda hh, j: (hh, 0, 0)), blk_t, blk_t], out_shape=[shp, shp, shp],
        scratch_shapes=[pltpu.VMEM((2, tq, tq), F32), pltpu.VMEM((2, tq, tq), F32)],
        compiler_params=_params("parallel", "arbitrary"),
    )(qb, qbt, ka, kat, va, doa, doat)


def _seg_matrix():
    idx = np.arange(WIDTH) // HEAD_DIM
    return jnp.asarray((idx[:, None] == idx[None, :]).astype(np.float32))


def _segsum(x, e):
    return jnp.dot(x, e, precision=HI, preferred_element_type=F32)


def _silu(x):
    return x * jax.nn.sigmoid(x)


def _silu_grad(x):
    s = jax.nn.sigmoid(x)
    return s * (1.0 + x * (1.0 - s))


def _shift_down(x, prev8, k):
    r = pltpu.roll(x, k, axis=0)
    p = pltpu.roll(prev8, k, axis=0)
    row = lax.broadcasted_iota(jnp.int32, prev8.shape, 0)
    head = jnp.where(row < k, p, r[:8])
    return jnp.concatenate([head, r[8:]], axis=0)


def _shift_up(x, next8, k):
    n = x.shape[0]
    r = pltpu.roll(x, n - k, axis=0)
    p = pltpu.roll(next8, 8 - k, axis=0)
    row = lax.broadcasted_iota(jnp.int32, next8.shape, 0)
    tail = jnp.where(row >= 8 - k, p, r[n - 8:])
    return jnp.concatenate([r[:n - 8], tail], axis=0)


def _causal_conv(x, prev8, w_ref, width, cols=slice(None)):
    y = x * w_ref[width - 1:width, cols]
    for k in range(1, width):
        y = y + _shift_down(x, prev8, k) * w_ref[width - 1 - k:width - k, cols]
    return y


def _causal_conv_bwd(x, prev8, dy, dnext8, w_ref, dw_ref, width, cols=slice(None)):
    dx = dy * w_ref[width - 1:width, cols]
    dw_ref[width - 1:width, cols] += jnp.sum(dy * x, axis=0, keepdims=True)
    for k in range(1, width):
        dx = dx + _shift_up(dy, dnext8, k) * w_ref[width - 1 - k:width - k, cols]
        dw_ref[width - 1 - k:width - k, cols] += jnp.sum(dy * _shift_down(x, prev8, k), axis=0, keepdims=True)
    return dx


HALO = 16


def _prev_spec(tt, width, tile=lambda i: i):
    return pl.BlockSpec((HALO, width), lambda i: (jnp.maximum(tile(i) * (tt // HALO) - 1, 0), 0))


def _prev8(p_ref, cols=slice(None)):
    return p_ref[:, cols].astype(F32)[HALO - 8:]


def _store_heads(ref, x):
    for h in range(HEADS):
        ref[h] = x[:, HEAD_DIM * h:HEAD_DIM * (h + 1)]


def _load_heads(ref):
    return jnp.concatenate([ref[h] for h in range(HEADS)], axis=1)


def _softplus(z):
    return jnp.maximum(z, 0.0) + jnp.log1p(jnp.exp(-jnp.abs(z)))


def _tri_masks(tt):
    r = lax.broadcasted_iota(jnp.int32, (tt, tt), 0)
    c = lax.broadcasted_iota(jnp.int32, (tt, tt), 1)
    same_chunk = lax.shift_right_logical(r, 6) == lax.shift_right_logical(c, 6)
    return r, c, same_chunk


def _gate_fwd(small, pbias, pscale):
    t = small.shape[0]
    tt = _pick(t, (256, 128))

    def body(x_ref, pb_ref, ps_ref, o_ref, carry_ref):
        @pl.when(pl.program_id(0) == 0)
        def _():
            carry_ref[...] = jnp.zeros_like(carry_ref)

        lane = lax.broadcasted_iota(jnp.int32, (tt, LANES), 1)
        z = x_ref[...] + pb_ref[...]
        log_f = jnp.where(lane < HEADS, -_softplus(-z), 0.0)
        g = jnp.where((lane >= 2 * HEADS) & (lane < 3 * HEADS), ps_ref[...] * _softplus(z), 0.0)
        r, c, same_chunk = _tri_masks(tt)
        lower = jnp.where(r >= c, 1.0, 0.0)
        lower_chunk = jnp.where((r >= c) & same_chunk, 1.0, 0.0)
        csum = jnp.dot(lower, log_f, precision=lax.Precision.HIGHEST, preferred_element_type=F32) + carry_ref[...]
        gc = jnp.dot(lower_chunk, g, precision=lax.Precision.HIGHEST, preferred_element_type=F32)
        carry_ref[...] += jnp.sum(log_f, axis=0, keepdims=True)
        o_ref[...] = jnp.where(lane < HEADS, csum, jnp.where(lane < 2 * HEADS, jax.nn.sigmoid(z), gc))

    row = pl.BlockSpec((tt, LANES), lambda i: (i, 0))
    vec = pl.BlockSpec((1, LANES), lambda i: (0, 0))
    return pl.pallas_call(
        body, grid=(t // tt,), name="gate_fwd", in_specs=[row, vec, vec], out_specs=row,
        out_shape=jax.ShapeDtypeStruct((t, LANES), F32),
        scratch_shapes=[pltpu.VMEM((1, LANES), F32)],
        compiler_params=_params("arbitrary"),
    )(small, pbias, pscale)


def _gate_bwd(small, pbias, pscale, dscal):
    t = small.shape[0]
    tt = _pick(t, (256, 128))
    nt = t // tt

    def body(x_ref, pb_ref, ps_ref, d_ref, dx_ref, dpb_ref, dps_ref, carry_ref):
        @pl.when(pl.program_id(0) == 0)
        def _():
            carry_ref[...] = jnp.zeros_like(carry_ref)
            dpb_ref[...] = jnp.zeros_like(dpb_ref)
            dps_ref[...] = jnp.zeros_like(dps_ref)

        lane = lax.broadcasted_iota(jnp.int32, (tt, LANES), 1)
        z = x_ref[...] + pb_ref[...]
        d = d_ref[...]
        dc = jnp.where(lane < HEADS, d, 0.0)
        dbeta = jnp.where((lane >= HEADS) & (lane < 2 * HEADS), d, 0.0)
        dgc = jnp.where((lane >= 2 * HEADS) & (lane < 3 * HEADS), d, 0.0)
        r, c, same_chunk = _tri_masks(tt)
        upper = jnp.where(r <= c, 1.0, 0.0)
        upper_chunk = jnp.where((r <= c) & same_chunk, 1.0, 0.0)
        dlogf = jnp.dot(upper, dc, precision=lax.Precision.HIGHEST, preferred_element_type=F32) + carry_ref[...]
        dg = jnp.dot(upper_chunk, dgc, precision=lax.Precision.HIGHEST, preferred_element_type=F32)
        carry_ref[...] += jnp.sum(dc, axis=0, keepdims=True)
        sg = jax.nn.sigmoid(z)
        dz = dlogf * (1.0 - sg) + dbeta * sg * (1.0 - sg) + dg * ps_ref[...] * sg
        dx_ref[...] = dz.astype(dx_ref.dtype)
        dpb_ref[...] += jnp.sum(dz, axis=0, keepdims=True)
        dps_ref[...] += jnp.sum(dg * _softplus(z), axis=0, keepdims=True)

    row = pl.BlockSpec((tt, LANES), lambda i: (nt - 1 - i, 0))
    vec = pl.BlockSpec((1, LANES), lambda i: (0, 0))
    return pl.pallas_call(
        body, grid=(nt,), name="gate_bwd", in_specs=[row, vec, vec, row], out_specs=[row, vec, vec],
        out_shape=[jax.ShapeDtypeStruct((t, LANES), BF16), jax.ShapeDtypeStruct((1, LANES), F32),
                   jax.ShapeDtypeStruct((1, LANES), F32)],
        scratch_shapes=[pltpu.VMEM((1, LANES), F32)],
        compiler_params=_params("arbitrary"),
    )(small, pbias, pscale, dscal)


def _gdn_pre_fwd(xg, conv_w, seg):
    t = xg.shape[0]
    c3 = 3 * WIDTH
    tt = _pick(t, (320, 256, 128))

    def body(x_ref, p_ref, w_ref, e_ref, q_ref, k_ref, v_ref):
        x = x_ref[...].astype(F32)
        prev = jnp.where(pl.program_id(0) == 0, 0.0, _prev8(p_ref))
        s = _silu(_causal_conv(x, prev, w_ref, GDN_CONV))
        e = e_ref[...]
        q = s[:, :WIDTH]
        k = s[:, WIDTH:2 * WIDTH]
        _store_heads(q_ref, q * lax.rsqrt(_segsum(q * q, e) + RMS_EPS) * (HEAD_DIM ** -0.5))
        _store_heads(k_ref, k * lax.rsqrt(_segsum(k * k, e) + RMS_EPS))
        _store_heads(v_ref, s[:, 2 * WIDTH:])

    out = pl.BlockSpec((HEADS, tt, HEAD_DIM), lambda i: (0, i, 0))
    shp = jax.ShapeDtypeStruct((HEADS, t, HEAD_DIM), F32)
    return pl.pallas_call(
        body, grid=(t // tt,), name="gdn_pre_fwd",
        in_specs=[pl.BlockSpec((tt, c3), lambda i: (i, 0)), _prev_spec(tt, c3),
                  pl.BlockSpec((GDN_CONV, c3), lambda i: (0, 0)), pl.BlockSpec((WIDTH, WIDTH), lambda i: (0, 0))],
        out_specs=[out, out, out], out_shape=[shp, shp, shp],
        compiler_params=_params("arbitrary"),
    )(xg, xg, conv_w, seg)


def _gdn_pre_bwd(xg, conv_w, seg, dqn, dkn, dv):
    t = xg.shape[0]
    c3 = 3 * WIDTH
    tt = _pick(t, (320, 256, 128))
    nt = t // tt

    def body(x_ref, p_ref, w_ref, e_ref, dq_ref, dk_ref, dv_ref, dx_ref, dw_ref, carry_ref):
        step = pl.program_id(0)
        x = x_ref[...].astype(F32)
        e = e_ref[...]
        prev = jnp.where(step == nt - 1, 0.0, _prev8(p_ref))
        y = _causal_conv(x, prev, w_ref, GDN_CONV)
        s = _silu(y)
        q = s[:, :WIDTH]
        k = s[:, WIDTH:2 * WIDTH]
        rq = lax.rsqrt(_segsum(q * q, e) + RMS_EPS)
        rk = lax.rsqrt(_segsum(k * k, e) + RMS_EPS)
        gq = _load_heads(dq_ref) * (HEAD_DIM ** -0.5)
        gk = _load_heads(dk_ref)
        dq = rq * gq - q * (rq * rq * rq) * _segsum(gq * q, e)
        dk = rk * gk - k * (rk * rk * rk) * _segsum(gk * k, e)
        dy = jnp.concatenate([dq, dk, _load_heads(dv_ref)], axis=1) * _silu_grad(y)

        @pl.when(step == 0)
        def _():
            carry_ref[...] = jnp.zeros_like(carry_ref)
            dw_ref[...] = jnp.zeros_like(dw_ref)

        dx = _causal_conv_bwd(x, prev, dy, carry_ref[...], w_ref, dw_ref, GDN_CONV)
        dx_ref[...] = dx.astype(dx_ref.dtype)
        carry_ref[...] = dy[:8]

    rev = lambda i: (nt - 1 - i, 0)
    blk = pl.BlockSpec((HEADS, tt, HEAD_DIM), lambda i: (0, nt - 1 - i, 0))
    return pl.pallas_call(
        body, grid=(nt,), name="gdn_pre_bwd",
        in_specs=[pl.BlockSpec((tt, c3), rev), _prev_spec(tt, c3, lambda i: nt - 1 - i),
                  pl.BlockSpec((GDN_CONV, c3), lambda i: (0, 0)), pl.BlockSpec((WIDTH, WIDTH), lambda i: (0, 0)),
                  blk, blk, blk],
        out_specs=[pl.BlockSpec((tt, c3), rev), pl.BlockSpec((GDN_CONV, c3), lambda i: (0, 0))],
        out_shape=[jax.ShapeDtypeStruct((t, c3), BF16), jax.ShapeDtypeStruct((GDN_CONV, c3), F32)],
        scratch_shapes=[pltpu.VMEM((8, c3), F32)],
        compiler_params=_params("arbitrary"),
    )(xg, xg, conv_w, seg, dqn, dkn, dv)


def _bmm(a, b, ca, cb, precision=None):
    return lax.dot_general(a, b, (((ca,), (cb,)), ((0,), (0,))), precision=precision, preferred_element_type=F32)


def _bf(x):
    return x.astype(BF16)


def _tri_inverse(a, eye):
    x = -a
    tinv = eye + x
    pw = x
    for _ in range(5):
        pb = _bf(pw)
        pw = _bmm(pb, pb, 2, 1)
        tinv = tinv + _bmm(_bf(tinv), _bf(pw), 2, 1)
    resid = eye - _bmm(eye + a, tinv, 2, 1, precision=HI)
    return tinv + _bmm(_bf(tinv), _bf(resid), 2, 1)


def _gdn_intra(q, k, v, bc, gcc, gcr):
    ii = lax.broadcasted_iota(jnp.int32, (CHUNK, CHUNK), 0)
    jj = lax.broadcasted_iota(jnp.int32, (CHUNK, CHUNK), 1)
    tril = (ii >= jj)[None]
    strict = (ii > jj)[None]
    eye = jnp.where(ii == jj, 1.0, 0.0).astype(F32)[None]
    last = (ii == CHUNK - 1)[None]
    dm = jnp.exp(jnp.where(tril, gcc - gcr, NEG))
    gam = jnp.exp(gcc)
    kb = k * bc
    vb = v * bc
    kk = _bmm(_bf(kb), _bf(k), 2, 2)
    a = jnp.where(strict, kk * dm, 0.0)
    tinv = _tri_inverse(a, eye)
    uw = _bmm(tinv, jnp.concatenate([vb, kb * gam], axis=2), 2, 1, precision=HI)
    u, wk = uw[:, :, :HEAD_DIM], uw[:, :, HEAD_DIM:]
    qk = _bmm(_bf(q), _bf(k), 2, 2)
    p = jnp.where(tril, qk * dm, 0.0)
    gl = jnp.sum(jnp.where(last, gcc, 0.0), axis=1, keepdims=True)
    edec = jnp.exp(gl - gcc)
    return dict(tril=tril, strict=strict, dm=dm, gam=gam, kb=kb, kk=kk, a=a, tinv=tinv, u=u, wk=wk, qk=qk, p=p,
                qg=q * gam, kt=k * edec, edec=edec, gaml=jnp.exp(gl), last=last)


def _gate_tiles(sc, gct, nb):
    rows = nb * CHUNK
    cols = lambda lane0: jnp.stack([jnp.broadcast_to(sc[:, lane0 + h:lane0 + h + 1], (rows, HEAD_DIM))
                                    for h in range(HEADS)], axis=0).reshape(HEADS * nb, CHUNK, HEAD_DIM)
    gcr = jnp.stack([jnp.broadcast_to(gct[h:h + 1, n * CHUNK:(n + 1) * CHUNK], (CHUNK, CHUNK))
                     for h in range(HEADS) for n in range(nb)], axis=0)
    return cols(HEADS), cols(2 * HEADS), gcr


def _gdn_fwd(q, k, v, scal, gct, nb=None):
    h, t, dh = q.shape
    nc = t // CHUNK
    nb = nb or _pick(nc, (4, 2))
    bsz = h * nb

    def body(q_ref, k_ref, v_ref, sc_ref, gt_ref, o_ref, s0_ref, state_ref):
        @pl.when(pl.program_id(0) == 0)
        def _():
            state_ref[...] = jnp.zeros_like(state_ref)

        ld = lambda r: r[...].reshape(bsz, CHUNK, dh)
        bc, gcc, gcr = _gate_tiles(sc_ref[...], gt_ref[...], nb)
        z = _gdn_intra(ld(q_ref), ld(k_ref), ld(v_ref), bc, gcc, gcr)
        per = lambda x: x.reshape((h, nb) + x.shape[1:])
        u, wk, p, qg, kt, gaml = (per(z[n]) for n in ("u", "wk", "p", "qg", "kt", "gaml"))
        s = state_ref[...]
        for n in range(nb):
            s0_ref[:, n] = s
            sb = _bf(s)
            vn = u[:, n] - _bmm(_bf(wk[:, n]), sb, 2, 1)
            vnb = _bf(vn)
            o_ref[:, n * CHUNK:(n + 1) * CHUNK, :] = _bmm(_bf(jnp.concatenate([qg[:, n], p[:, n]], axis=2)),
                                                          jnp.concatenate([sb, vnb], axis=1), 2, 1)
            s = s * gaml[:, n] + _bmm(_bf(kt[:, n]), vnb, 1, 1)
        state_ref[...] = s

    blk = pl.BlockSpec((h, nb * CHUNK, dh), lambda i: (0, i, 0))
    return pl.pallas_call(
        body, grid=(nc // nb,), name="gdn_fwd",
        in_specs=[blk] * 3 + [pl.BlockSpec((nb * CHUNK, LANES), lambda i: (i, 0)),
                              pl.BlockSpec((h, nb * CHUNK), lambda i: (0, i))],
        out_specs=[blk, pl.BlockSpec((h, nb, dh, dh), lambda i: (0, i, 0, 0))],
        out_shape=[jax.ShapeDtypeStruct((h, t, dh), F32), jax.ShapeDtypeStruct((h, nc, dh, dh), F32)],
        scratch_shapes=[pltpu.VMEM((h, dh, dh), F32)],
        compiler_params=_params("arbitrary"),
    )(q, k, v, scal, gct)


def _gdn_bwd(q, k, v, scal, gct, s0s, do, nb=None):
    h, t, dh = q.shape
    nc = t // CHUNK
    nb = nb or _pick(nc, (2,))
    bsz = h * nb
    ng = nc // nb
    rows = nb * CHUNK

    def body(q_ref, k_ref, v_ref, sc_ref, gt_ref, s0_ref, do_ref,
             dq_ref, dk_ref, dv_ref, dsc_ref, dgt_ref, ds_ref):
        @pl.when(pl.program_id(0) == 0)
        def _():
            ds_ref[...] = jnp.zeros_like(ds_ref)

        ld = lambda r: r[...].reshape(bsz, CHUNK, dh)
        q, k, v = ld(q_ref), ld(k_ref), ld(v_ref)
        bc, gcc, gcr = _gate_tiles(sc_ref[...], gt_ref[...], nb)
        z = _gdn_intra(q, k, v, bc, gcc, gcr)
        per = lambda x: x.reshape((h, nb) + x.shape[1:])
        u, wk, p, qg, kt, gaml = (per(z[n]) for n in ("u", "wk", "p", "qg", "kt", "gaml"))
        dout = per(ld(do_ref))
        ds = ds_ref[...]
        d_u, d_wk, d_p, d_qg, d_kt, d_gaml = ([None] * nb for _ in range(6))
        for n in reversed(range(nb)):
            s0 = s0_ref[:, n]
            s0b, dsb, dob = _bf(s0), _bf(ds), _bf(dout[:, n])
            wkb, qgb = _bf(wk[:, n]), _bf(qg[:, n])
            vn = u[:, n] - _bmm(wkb, s0b, 2, 1)
            dvn = _bmm(_bf(p[:, n]), dob, 1, 1) + _bmm(_bf(kt[:, n]), dsb, 2, 1)
            dvnb = _bf(dvn)
            d_u[n] = dvn
            vnb = _bf(vn)
            dpq = _bmm(dob, jnp.concatenate([vnb, s0b], axis=1), 2, 2)
            d_p[n], d_qg[n] = dpq[:, :, :CHUNK], dpq[:, :, CHUNK:]
            d_kt[n] = _bmm(vnb, dsb, 2, 2)
            d_gaml[n] = jnp.sum(s0 * ds, axis=1, keepdims=True)
            d_wk[n] = -_bmm(dvnb, s0b, 2, 2)
            ds = gaml[:, n] * ds + _bmm(jnp.concatenate([qgb, -wkb], axis=1), jnp.concatenate([dob, dvnb], axis=1), 1, 1)
        ds_ref[...] = ds

        flat = lambda xs: jnp.stack(xs, axis=1).reshape((bsz,) + xs[0].shape[1:])
        d_u, d_wk, d_p, d_qg, d_kt, d_gaml = (flat(x) for x in (d_u, d_wk, d_p, d_qg, d_kt, d_gaml))
        tinv, gam, kb, dm = z["tinv"], z["gam"], z["kb"], z["dm"]
        dr = _bmm(tinv, jnp.concatenate([d_u, d_wk], axis=2), 1, 1, precision=HI)
        drv, drk = dr[:, :, :HEAD_DIM], dr[:, :, HEAD_DIM:]
        da = -_bmm(_bf(dr), _bf(jnp.concatenate([z["u"], z["wk"]], axis=2)), 2, 2)
        da = jnp.where(z["strict"], da, 0.0)
        d_p = jnp.where(z["tril"], d_p, 0.0)
        dkk = _bf(da * dm)
        dqk = _bf(d_p * dm)
        dkb = _bmm(dkk, _bf(k), 2, 1) + drk * gam
        dk = (_bmm(jnp.concatenate([dkk, dqk], axis=1), _bf(jnp.concatenate([kb, q], axis=1)), 1, 1)
              + dkb * bc + d_kt * z["edec"])
        dq = _bmm(dqk, _bf(k), 2, 1) + d_qg * gam
        mm = da * z["a"] + d_p * z["p"]
        dkt_kt = d_kt * z["kt"]
        dgl = jnp.sum(dkt_kt, axis=1, keepdims=True) + d_gaml * z["gaml"]
        dgc = mm + d_qg * z["qg"] + drk * kb * gam - dkt_kt + jnp.where(z["last"], dgl, 0.0)
        dq_ref[...] = dq.reshape(h, rows, dh)
        dk_ref[...] = dk.reshape(h, rows, dh)
        dv_ref[...] = (drv * bc).reshape(h, rows, dh)
        dbeta = (dkb * k + drv * v).reshape(h, rows, dh)
        dgc = dgc.reshape(h, rows, dh)
        lane = lax.broadcasted_iota(jnp.int32, (rows, LANES), 1)
        dsc = jnp.zeros((rows, LANES), F32)
        for hh in range(h):
            dsc = jnp.where(lane == HEADS + hh, jnp.sum(dbeta[hh], axis=1, keepdims=True), dsc)
            dsc = jnp.where(lane == 2 * HEADS + hh, jnp.sum(dgc[hh], axis=1, keepdims=True), dsc)
        dsc_ref[...] = dsc
        dgr = -jnp.sum(mm, axis=1, keepdims=True)
        for hh in range(h):
            for n in range(nb):
                dgt_ref[hh:hh + 1, n * CHUNK:(n + 1) * CHUNK] = dgr[hh * nb + n]

    blk = pl.BlockSpec((h, rows, dh), lambda i: (0, ng - 1 - i, 0))
    shp = jax.ShapeDtypeStruct((h, t, dh), F32)
    sc_spec = pl.BlockSpec((rows, LANES), lambda i: (ng - 1 - i, 0))
    gt_spec = pl.BlockSpec((h, rows), lambda i: (0, ng - 1 - i))
    return pl.pallas_call(
        body, grid=(ng,), name="gdn_bwd",
        in_specs=[blk] * 3 + [sc_spec, gt_spec, pl.BlockSpec((h, nb, dh, dh), lambda i: (0, ng - 1 - i, 0, 0)), blk],
        out_specs=[blk] * 3 + [sc_spec, gt_spec],
        out_shape=[shp] * 3 + [jax.ShapeDtypeStruct((t, LANES), F32), jax.ShapeDtypeStruct((h, t), F32)],
        scratch_shapes=[pltpu.VMEM((h, dh, dh), F32)],
        compiler_params=_params("arbitrary"),
    )(q, k, v, scal, gct, s0s, do)


def _gdn_post_fwd(o, xg, gain, seg):
    t = o.shape[1]
    tt = _pick(t, (320, 256, 128))

    def body(o_ref, z_ref, g_ref, e_ref, y_ref):
        x = _load_heads(o_ref)
        r = lax.rsqrt(_segsum(x * x, e_ref[...]) * (1.0 / HEAD_DIM) + RMS_EPS)
        y_ref[...] = (x * r * g_ref[...] * _silu(z_ref[...].astype(F32))).astype(y_ref.dtype)

    return pl.pallas_call(
        body, grid=(t // tt,), name="gdn_post_fwd",
        in_specs=[pl.BlockSpec((HEADS, tt, HEAD_DIM), lambda i: (0, i, 0)), pl.BlockSpec((tt, WIDTH), lambda i: (i, 3)),
                  pl.BlockSpec((1, WIDTH), lambda i: (0, 0)), pl.BlockSpec((WIDTH, WIDTH), lambda i: (0, 0))],
        out_specs=pl.BlockSpec((tt, WIDTH), lambda i: (i, 0)),
        out_shape=jax.ShapeDtypeStruct((t, WIDTH), BF16),
        compiler_params=_params("arbitrary"),
    )(o, xg, gain, seg)


def _gdn_post_bwd(o, xg, gain, seg, dy):
    t = o.shape[1]
    tt = _pick(t, (320, 256, 128))

    def body(o_ref, z_ref, g_ref, e_ref, dy_ref, do_ref, dz_ref, dg_ref):
        x = _load_heads(o_ref)
        zz = z_ref[...].astype(F32)
        e = e_ref[...]
        gain_v = g_ref[...]
        d = dy_ref[...]
        r = lax.rsqrt(_segsum(x * x, e) * (1.0 / HEAD_DIM) + RMS_EPS)
        xr = x * r
        don = d * _silu(zz)
        dz_ref[...] = (d * xr * gain_v * _silu_grad(zz)).astype(dz_ref.dtype)
        gy = don * gain_v
        _store_heads(do_ref, r * gy - xr * (r * r) * (_segsum(gy * x, e) * (1.0 / HEAD_DIM)))

        @pl.when(pl.program_id(0) == 0)
        def _():
            dg_ref[...] = jnp.zeros_like(dg_ref)

        dg_ref[...] += jnp.sum(don * xr, axis=0, keepdims=True)

    row = pl.BlockSpec((tt, WIDTH), lambda i: (i, 0))
    vec = pl.BlockSpec((1, WIDTH), lambda i: (0, 0))
    hm = pl.BlockSpec((HEADS, tt, HEAD_DIM), lambda i: (0, i, 0))
    return pl.pallas_call(
        body, grid=(t // tt,), name="gdn_post_bwd",
        in_specs=[hm, pl.BlockSpec((tt, WIDTH), lambda i: (i, 3)), vec,
                  pl.BlockSpec((WIDTH, WIDTH), lambda i: (0, 0)), row],
        out_specs=[hm, row, vec],
        out_shape=[jax.ShapeDtypeStruct((HEADS, t, HEAD_DIM), F32), jax.ShapeDtypeStruct((t, WIDTH), BF16),
                   jax.ShapeDtypeStruct((1, WIDTH), F32)],
        compiler_params=_params("arbitrary"),
    )(o, xg, gain, seg, dy)


def _mix_fwd(yf, yg, gates, bias):
    t, d = yf.shape
    tt = _pick(t, (320, 256, 128))

    def body(yf_ref, yg_ref, g1_ref, g2_ref, b1_ref, b2_ref, o_ref):
        g1 = jax.nn.sigmoid(g1_ref[...].astype(F32) + b1_ref[...])
        g2 = jax.nn.sigmoid(g2_ref[...].astype(F32) + b2_ref[...])
        o_ref[...] = (g1 * yf_ref[...].astype(F32) + g2 * yg_ref[...].astype(F32)).astype(o_ref.dtype)

    row = pl.BlockSpec((tt, d), lambda i: (i, 0))
    return pl.pallas_call(
        body, grid=(t // tt,), name="mix_fwd",
        in_specs=[row, row, row, pl.BlockSpec((tt, d), lambda i: (i, 1)),
                  pl.BlockSpec((1, d), lambda i: (0, 0)), pl.BlockSpec((1, d), lambda i: (0, 1))],
        out_specs=row, out_shape=jax.ShapeDtypeStruct((t, d), BF16),
        compiler_params=_params("arbitrary"),
    )(yf, yg, gates, gates, bias, bias)


def _mix_bwd(dmix, yf, yg, gates, bias):
    t, d = yf.shape
    tt = _pick(t, (320, 256, 128))

    def body(dm_ref, yf_ref, yg_ref, g1_ref, g2_ref, b1_ref, b2_ref, dyf_ref, dyg_ref, dg_ref, db_ref):
        dm = dm_ref[...].astype(F32)
        g1 = jax.nn.sigmoid(g1_ref[...].astype(F32) + b1_ref[...])
        g2 = jax.nn.sigmoid(g2_ref[...].astype(F32) + b2_ref[...])
        dyf_ref[...] = (dm * g1).astype(BF16)
        dyg_ref[...] = (dm * g2).astype(BF16)
        dgate = jnp.concatenate([dm * yf_ref[...].astype(F32) * g1 * (1.0 - g1),
                                 dm * yg_ref[...].astype(F32) * g2 * (1.0 - g2)], axis=1)
        dg_ref[...] = dgate.astype(BF16)

        @pl.when(pl.program_id(0) == 0)
        def _():
            db_ref[...] = jnp.zeros_like(db_ref)

        db_ref[...] += jnp.sum(dgate, axis=0, keepdims=True)

    row = pl.BlockSpec((tt, d), lambda i: (i, 0))
    wide = pl.BlockSpec((tt, 2 * d), lambda i: (i, 0))
    return pl.pallas_call(
        body, grid=(t // tt,), name="mix_bwd",
        in_specs=[row, row, row, row, pl.BlockSpec((tt, d), lambda i: (i, 1)),
                  pl.BlockSpec((1, d), lambda i: (0, 0)), pl.BlockSpec((1, d), lambda i: (0, 1))],
        out_specs=[row, row, wide, pl.BlockSpec((1, 2 * d), lambda i: (0, 0))],
        out_shape=[jax.ShapeDtypeStruct((t, d), BF16), jax.ShapeDtypeStruct((t, d), BF16),
                   jax.ShapeDtypeStruct((t, 2 * d), BF16), jax.ShapeDtypeStruct((1, 2 * d), F32)],
        compiler_params=_params("arbitrary"),
    )(dmix, yf, yg, gates, gates, bias, bias)


def _ffn_act_fwd(up, conv_w, conv_b):
    t, c = up.shape
    tt = 128

    def body(x_ref, p_ref, w_ref, b_ref, o_ref):
        first = pl.program_id(0) == 0

        def conv(cols):
            prev = jnp.where(first, 0.0, _prev8(p_ref, cols))
            return _causal_conv(x_ref[:, cols].astype(F32), prev, w_ref, FFN_CONV, cols) + b_ref[:, cols]

        for lo in range(0, D_FF, FFN_LANES):
            gate = conv(slice(lo, lo + FFN_LANES))
            val = conv(slice(D_FF + lo, D_FF + lo + FFN_LANES))
            o_ref[:, lo:lo + FFN_LANES] = (_silu(gate) * val).astype(o_ref.dtype)

    return pl.pallas_call(
        body, grid=(t // tt,), name="ffn_act_fwd",
        in_specs=[pl.BlockSpec((tt, c), lambda i: (i, 0)), _prev_spec(tt, c),
                  pl.BlockSpec((FFN_CONV, c), lambda i: (0, 0)), pl.BlockSpec((1, c), lambda i: (0, 0))],
        out_specs=pl.BlockSpec((tt, D_FF), lambda i: (i, 0)),
        out_shape=jax.ShapeDtypeStruct((t, D_FF), BF16),
        compiler_params=_params("arbitrary"),
    )(up, up, conv_w, conv_b)


def _ffn_act_bwd(up, conv_w, conv_b, dact):
    t, c = up.shape
    tt = 128
    nt = t // tt

    def body(x_ref, p_ref, w_ref, b_ref, da_ref, dx_ref, dw_ref, db_ref, carry_ref):
        step = pl.program_id(0)

        @pl.when(step == 0)
        def _():
            carry_ref[...] = jnp.zeros_like(carry_ref)
            dw_ref[...] = jnp.zeros_like(dw_ref)
            db_ref[...] = jnp.zeros_like(db_ref)

        def conv(cols):
            x = x_ref[:, cols].astype(F32)
            prev = jnp.where(step == nt - 1, 0.0, _prev8(p_ref, cols))
            return x, prev, _causal_conv(x, prev, w_ref, FFN_CONV, cols) + b_ref[:, cols]

        def back(cols, x, prev, du):
            dx = _causal_conv_bwd(x, prev, du, carry_ref[:, cols], w_ref, dw_ref, FFN_CONV, cols)
            dx_ref[:, cols] = dx.astype(dx_ref.dtype)
            db_ref[:, cols] += jnp.sum(du, axis=0, keepdims=True)
            carry_ref[:, cols] = du[:8]

        for lo in range(0, D_FF, FFN_LANES):
            gcols, vcols = slice(lo, lo + FFN_LANES), slice(D_FF + lo, D_FF + lo + FFN_LANES)
            xg, pg, gate = conv(gcols)
            xv, pv, val = conv(vcols)
            da = da_ref[:, gcols]
            back(gcols, xg, pg, da * val * _silu_grad(gate))
            back(vcols, xv, pv, da * _silu(gate))

    rev = lambda i: (nt - 1 - i, 0)
    return pl.pallas_call(
        body, grid=(nt,), name="ffn_act_bwd",
        in_specs=[pl.BlockSpec((tt, c), rev),
                  _prev_spec(tt, c, lambda i: nt - 1 - i),
                  pl.BlockSpec((FFN_CONV, c), lambda i: (0, 0)), pl.BlockSpec((1, c), lambda i: (0, 0)),
                  pl.BlockSpec((tt, D_FF), rev)],
        out_specs=[pl.BlockSpec((tt, c), rev), pl.BlockSpec((FFN_CONV, c), lambda i: (0, 0)),
                   pl.BlockSpec((1, c), lambda i: (0, 0))],
        out_shape=[jax.ShapeDtypeStruct((t, c), BF16), jax.ShapeDtypeStruct((FFN_CONV, c), F32),
                   jax.ShapeDtypeStruct((1, c), F32)],
        scratch_shapes=[pltpu.VMEM((8, c), F32)],
        compiler_params=_params("arbitrary"),
    )(up, up, conv_w, conv_b, dact)


def _final_loss(h2, target, gain, seq):
    t, d = h2.shape
    tr = _pick(t, (320, 256, 128))

    def body(h_ref, t_ref, g_ref, loss_ref, dh_ref, dhb_ref, dg_ref):
        i = pl.program_id(0)
        x = h_ref[...]
        gain_v = g_ref[...]
        r = lax.rsqrt(jnp.mean(x * x, axis=-1, keepdims=True) + RMS_EPS)
        xr = x * r
        rows = i * tr + lax.broadcasted_iota(jnp.int32, (tr, 1), 0)
        real = (rows >= N_META) & (rows < N_META + seq)
        err = jnp.where(real, xr * gain_v - t_ref[...], 0.0)
        dy = err * (1.0 / d)
        gy = dy * gain_v
        dh = r * (gy - xr * jnp.mean(gy * xr, axis=-1, keepdims=True))
        dh_ref[...] = dh
        dhb_ref[...] = dh.astype(BF16)

        @pl.when(i == 0)
        def _():
            loss_ref[...] = jnp.zeros_like(loss_ref)
            dg_ref[...] = jnp.zeros_like(dg_ref)

        part = jnp.sum(jnp.sum(err * err, axis=-1, keepdims=True), axis=0, keepdims=True)
        loss_ref[...] += jnp.broadcast_to(part * (0.5 / d), loss_ref.shape)
        dg_ref[...] += jnp.sum(dy * xr, axis=0, keepdims=True)

    row = pl.BlockSpec((tr, d), lambda i: (i, 0))
    vec = pl.BlockSpec((1, d), lambda i: (0, 0))
    return pl.pallas_call(
        body, grid=(t // tr,), name="final_loss",
        in_specs=[row, row, vec],
        out_specs=[pl.BlockSpec((1, LANES), lambda i: (0, 0)), row, row, vec],
        out_shape=[jax.ShapeDtypeStruct((1, LANES), F32), jax.ShapeDtypeStruct((t, d), F32),
                   jax.ShapeDtypeStruct((t, d), BF16), jax.ShapeDtypeStruct((1, d), F32)],
        compiler_params=_params("arbitrary"),
    )(h2, target, gain)


ADAM_TILE_BYTES = 1 << 20


def _adamw(w, m, v, grecv, name):
    r, cols = w.shape
    tr = r
    if r * cols * 4 > ADAM_TILE_BYTES:
        tr = max(d for d in range(8, r + 1, 8) if r % d == 0 and d * cols * 4 <= ADAM_TILE_BYTES)

    def body(w_ref, m_ref, v_ref, g_ref, go_ref, d_ref, mo_ref, vo_ref):
        g = g_ref[0].astype(F32)
        for s in range(1, N_DEV):
            g = g + g_ref[s].astype(F32)
        wv = w_ref[...]
        mn = ADAM_B1 * m_ref[...] + (1.0 - ADAM_B1) * g
        vn = ADAM_B2 * v_ref[...] + (1.0 - ADAM_B2) * (g * g)
        m_hat = mn / (1.0 - ADAM_B1 ** ADAM_STEP)
        v_hat = vn / (1.0 - ADAM_B2 ** ADAM_STEP)
        go_ref[...] = g
        d_ref[...] = -ADAM_LR * (m_hat / (jnp.sqrt(v_hat) + ADAM_EPS) + ADAM_WD * wv)
        mo_ref[...] = mn
        vo_ref[...] = vn

    row = pl.BlockSpec((tr, cols), lambda i: (i, 0))
    shp = jax.ShapeDtypeStruct((r, cols), F32)
    return pl.pallas_call(
        body, grid=(r // tr,), name=name,
        in_specs=[row, row, row, pl.BlockSpec((N_DEV, tr, cols), lambda i: (0, i, 0))],
        out_specs=[row] * 4, out_shape=[shp] * 4,
        compiler_params=_params("parallel"),
    )(w, m, v, grecv)


def _mesh_pos():
    return lax.axis_index("x"), lax.axis_index("y"), lax.axis_index("c")


def _all_gather(shards):
    n = len(shards)

    def body(*refs):
        x_refs, out_refs = refs[:n], refs[n:2 * n]
        send_sems, recv_sems, local_sems = refs[2 * n:]
        x, y, c = _mesh_pos()
        me, sibling = (x, y, c), (x, y, 1 - c)
        chips = [(1 - x, y), (x, 1 - y), (1 - x, 1 - y)]

        def slot(a, px, py, pc):
            return out_refs[a].at[4 * px + 2 * py + pc]

        def copy(a, kk, block, to, src=None):
            return pltpu.make_async_remote_copy(
                src_ref=slot(a, *block) if src is None else src, dst_ref=slot(a, *block),
                send_sem=send_sems.at[7 * a + kk], recv_sem=recv_sems.at[7 * a + kk],
                device_id=to, device_id_type=MESH_ID)

        mine = [pltpu.make_async_copy(x_refs[a], slot(a, *me), local_sems.at[a]) for a in range(n)]
        first = []
        for a in range(n):
            first.append(copy(a, 0, me, sibling, src=x_refs[a]))
            first += [copy(a, 1 + j, me, (*chip, c), src=x_refs[a]) for j, chip in enumerate(chips)]
        for cp in mine + first:
            cp.start()
        passed = []
        for j, chip in enumerate(chips):
            for a in range(n):
                copy(a, 1 + j, (*chip, c), me).wait_recv()
                passed.append(copy(a, 4 + j, (*chip, c), sibling))
                passed[-1].start()
        for a in range(n):
            copy(a, 0, sibling, me).wait_recv()
        for j, chip in enumerate(chips):
            for a in range(n):
                copy(a, 4 + j, (*chip, 1 - c), me).wait_recv()
        for cp in first + passed:
            cp.wait_send()
        for cp in mine:
            cp.wait()

    hbm = pl.BlockSpec(memory_space=pl.ANY)
    return pl.pallas_call(
        body, name="weight_all_gather", in_specs=[hbm] * n, out_specs=[hbm] * n,
        out_shape=[jax.ShapeDtypeStruct((N_DEV,) + s.shape, s.dtype) for s in shards],
        scratch_shapes=[pltpu.SemaphoreType.DMA((7 * n,)), pltpu.SemaphoreType.DMA((7 * n,)),
                        pltpu.SemaphoreType.DMA((n,))],
    )(*shards)


def _grad_exchange(blocks, small):
    n = len(blocks)

    def body(*refs):
        src_refs, dst_refs = refs[:n + 1], refs[n + 1:2 * n + 2]
        send_sems, recv_sems, local_sems = refs[2 * n + 2:]
        x, y, c = _mesh_pos()
        me = 4 * x + 2 * y + c
        copies = []
        for kk in range(1, N_DEV):
            px = 1 - x if kk & 4 else x
            py = 1 - y if kk & 2 else y
            pc = 1 - c if kk & 1 else c
            peer = 4 * px + 2 * py + pc
            for a in range(n + 1):
                copies.append(pltpu.make_async_remote_copy(
                    src_ref=src_refs[a].at[peer] if a < n else src_refs[a], dst_ref=dst_refs[a].at[me],
                    send_sem=send_sems.at[7 * a + kk - 1], recv_sem=recv_sems.at[7 * a + kk - 1],
                    device_id=(px, py, pc), device_id_type=MESH_ID))
        own = [pltpu.make_async_copy(src_refs[a].at[me] if a < n else src_refs[a], dst_refs[a].at[me],
                                     local_sems.at[a]) for a in range(n + 1)]
        for cp in own + copies:
            cp.start()
        for cp in copies + own:
            cp.wait()

    hbm = pl.BlockSpec(memory_space=pl.ANY)
    return pl.pallas_call(
        body, name="grad_exchange", in_specs=[hbm] * (n + 1), out_specs=[hbm] * (n + 1),
        out_shape=[jax.ShapeDtypeStruct(b.shape, b.dtype) for b in blocks]
        + [jax.ShapeDtypeStruct((N_DEV,) + small.shape, small.dtype)],
        scratch_shapes=[pltpu.SemaphoreType.DMA((7 * (n + 1),)), pltpu.SemaphoreType.DMA((7 * (n + 1),)),
                        pltpu.SemaphoreType.DMA((n + 1,))],
    )(*blocks, small)


def _exchange_copies(src_refs, land_refs, send_sems, recv_sems):
    x, y, c = _mesh_pos()
    me = 4 * x + 2 * y + c
    copies = []
    for kk in range(1, N_DEV):
        px = 1 - x if kk & 4 else x
        py = 1 - y if kk & 2 else y
        pc = 1 - c if kk & 1 else c
        for a, (src, land) in enumerate(zip(src_refs, land_refs)):
            copies.append(pltpu.make_async_remote_copy(
                src_ref=src.at[4 * px + 2 * py + pc], dst_ref=land.at[me],
                send_sem=send_sems.at[7 * a + kk - 1], recv_sem=recv_sems.at[7 * a + kk - 1],
                device_id=(px, py, pc), device_id_type=MESH_ID))
    return copies


def _gather_copies(src_refs, land_refs, send_sems, recv_sems):
    x, y, c = _mesh_pos()
    me = 4 * x + 2 * y + c
    copies = []
    for kk in range(1, N_DEV):
        px = 1 - x if kk & 4 else x
        py = 1 - y if kk & 2 else y
        pc = 1 - c if kk & 1 else c
        for a, (src, land) in enumerate(zip(src_refs, land_refs)):
            copies.append(pltpu.make_async_remote_copy(
                src_ref=src, dst_ref=land.at[me],
                send_sem=send_sems.at[7 * a + kk - 1], recv_sem=recv_sems.at[7 * a + kk - 1],
                device_id=(px, py, pc), device_id_type=MESH_ID))
    return copies


_HBM = pl.BlockSpec(memory_space=pltpu.HBM)
_SEM = pl.BlockSpec(memory_space=pltpu.SEMAPHORE)
_DATAFLOW = pltpu.SideEffectType.DATAFLOW_SIDE_EFFECTING


def _split_start(name, make_copies, sources, land_shapes):
    n = len(sources)

    def body(*refs):
        src_refs, land_refs, send_sems, recv_sems = refs[:n], refs[n:2 * n], refs[2 * n], refs[2 * n + 1]
        for cp in make_copies(src_refs, land_refs, send_sems, recv_sems):
            cp.start()
        token = refs[-1]
        token[...] = jnp.zeros_like(token)

    in_hbm = lambda a: pltpu.with_memory_space_constraint(a, pltpu.HBM)
    hbm_shapes = [pltpu.HBM(s.shape, s.dtype) for s in list(sources) + list(land_shapes)]
    outs = pl.pallas_call(
        body, name=name, in_specs=[_HBM] * (2 * n),
        out_shape=(pltpu.SemaphoreType.DMA((7 * n,)), pltpu.SemaphoreType.DMA((7 * n,)), *hbm_shapes,
                   jax.ShapeDtypeStruct((8, LANES), F32)),
        out_specs=(_SEM, _SEM, *[_HBM] * (2 * n), pl.BlockSpec(memory_space=pltpu.VMEM)),
        input_output_aliases={a: 2 + a for a in range(2 * n)},
        compiler_params=pltpu.CompilerParams(has_side_effects=_DATAFLOW),
    )(*[in_hbm(s) for s in sources], *[in_hbm(lax.empty(s.shape, s.dtype)) for s in land_shapes])
    return outs[0], outs[1], outs[2:2 + n], outs[2 + n:2 + 2 * n], outs[-1]


def _split_wait(name, make_copies, send_sems, recv_sems, src_thru, land_thru, after):
    n = len(src_thru)

    def body(*refs):
        src_refs, land_refs, send_sems, recv_sems = refs[:n], refs[n:2 * n], refs[2 * n], refs[2 * n + 1]
        for cp in make_copies(src_refs, land_refs, send_sems, recv_sems):
            cp.wait_send()
            cp.wait_recv()

    outs = pl.pallas_call(
        body, name=name,
        in_specs=[_HBM] * (2 * n) + [_SEM, _SEM, pl.BlockSpec(memory_space=pl.ANY)],
        out_shape=tuple(pltpu.HBM(b.shape, b.dtype) for b in list(src_thru) + list(land_thru)),
        out_specs=[_HBM] * (2 * n), input_output_aliases={a: a for a in range(2 * n)},
        compiler_params=pltpu.CompilerParams(has_side_effects=_DATAFLOW),
    )(*src_thru, *land_thru, send_sems, recv_sems, after)
    return outs[:n], outs[n:]


def _exchange_start(blocks):
    return _split_start("grad_exchange_start", _exchange_copies, blocks, blocks)


def _exchange_wait(send_sems, recv_sems, src_thru, land_thru, after):
    return _split_wait("grad_exchange_wait", _exchange_copies, send_sems, recv_sems, src_thru, land_thru, after)


def _gather_start(shards):
    lands = [jax.ShapeDtypeStruct((N_DEV,) + s.shape, s.dtype) for s in shards]
    return _split_start("weight_gather_start", _gather_copies, shards, lands)


def _gather_wait(send_sems, recv_sems, src_thru, land_thru, after):
    return _split_wait("weight_gather_wait", _gather_copies, send_sems, recv_sems, src_thru, land_thru, after)


def _pad_flat(parts, rows):
    flat = jnp.concatenate([p.reshape(-1) for p in parts])
    return jnp.pad(flat, (0, rows * LANES - flat.shape[0])).reshape(rows, LANES)


def _rows_for(n_elems, mult=1024):
    rows = -(-n_elems // LANES)
    return -(-rows // mult) * mult


SHARDED = ("meta_tokens", "w_in", "gdn_conv_w", "w_branch_fox", "w_branch_gdn", "w_out", "ffn_w_up", "ffn_conv_w",
           "ffn_w_down")
MATMUL = ("w_in", "w_branch_fox", "w_branch_gdn", "w_out", "ffn_w_up", "ffn_w_down")
EXACT = ("meta_tokens", "gdn_conv_w", "ffn_conv_w")
REPLICATED = ("fgt_bias", "gdn_a_log", "gdn_dt_bias", "gdn_norm_w", "gate_bias", "norm_mix_w", "norm_ffn_w",
              "ffn_conv_b", "norm_final_w")
WEIGHTS = ("meta_tokens", "w_in", "fgt_bias", "gdn_conv_w", "gdn_a_log", "gdn_dt_bias", "gdn_norm_w", "gate_bias",
           "w_branch_fox", "w_branch_gdn", "w_out", "norm_mix_w", "norm_ffn_w", "ffn_w_up", "ffn_conv_w",
           "ffn_conv_b", "ffn_w_down", "norm_final_w")


def _unpack(buf, shapes):
    flat = buf.reshape(-1)
    out, off = [], 0
    for s in shapes:
        n = int(np.prod(s))
        out.append(flat[off:off + n].reshape(s))
        off += n
    return out


def _unpack_gathered(buf, shapes):
    flat = buf.reshape(N_DEV, -1)
    out, off = [], 0
    for s in shapes:
        n = int(np.prod(s))
        out.append(flat[:, off:off + n].reshape((N_DEV,) + tuple(s)))
        off += n
    return out


def _cat_cols(g):
    return g.transpose(1, 0, 2).reshape(g.shape[1], -1)


def _col_blocks(full, width):
    return full.reshape(full.shape[0], N_DEV, width).transpose(1, 0, 2)


def _local_step(x, target, w, early=None, late_weights=None):
    seq = x.shape[0]
    t = _padded_tokens(seq)
    pad = t - N_META - seq
    seg = _seg_matrix()
    zrows = jnp.zeros((pad, D_MODEL), F32)
    h0 = jnp.concatenate([w["meta_tokens"], x, zrows], axis=0)
    tgt = jnp.concatenate([jnp.zeros((N_META, D_MODEL), F32), target, zrows], axis=0)

    w_in = w["w_in"]
    o_f, o_g, o_z, o_b, o_a, o_gate = 1536, 1544, 3080, 3592, 3600, 3608
    w_small = jnp.concatenate([w_in[:, o_f:o_f + 8], w_in[:, o_b:o_b + 8], w_in[:, o_a:o_a + 8],
                               jnp.zeros((D_MODEL, LANES - 24), BF16)], axis=1)
    w_r = jnp.concatenate([w_in[:, :1536], w_in[:, o_g:o_z], w_in[:, o_z:o_b], w_in[:, o_gate:], w_small], axis=1)

    a1 = _rmsnorm_fwd(h0, w["norm_mix_w"])
    fq = _mm(a1, w_r[:, :1536], BF16, "proj_fox")
    xg = _mm(a1, w_r[:, 1536:3584], BF16, "proj_gdn")
    gt = _mm(a1, w_r[:, 3584:5632], BF16, "proj_gates")
    sm = _mm(a1, w_r[:, 5632:], F32, "proj_small")

    lanes_pad = lambda a, lo: jnp.pad(a, ((0, 0), (lo, LANES - lo - a.shape[1])))
    neg_exp_a = -jnp.exp(w["gdn_a_log"])
    pbias = lanes_pad(w["fgt_bias"], 0) + lanes_pad(w["gdn_dt_bias"], 2 * HEADS)
    if late_weights is not None:
        pbias = pbias + late_weights[0][0, 0]
    pscale = lanes_pad(neg_exp_a, 2 * HEADS)
    scal = _gate_fwd(sm, pbias, pscale)
    gct = scal[:, 2 * HEADS:3 * HEADS].T

    qa, ka, va, kat, vat = _fox_prep(fq, scal)
    oa, qb, qbt = _fox_fwd(qa, ka, vat)
    o_fox = _fox_post(oa)

    qh, kh, vh = _gdn_pre_fwd(xg, w["gdn_conv_w"], seg)
    og, s0s = _gdn_fwd(qh, kh, vh, scal, gct)
    norm_w = jnp.tile(w["gdn_norm_w"], (1, HEADS))
    ogn = _gdn_post_fwd(og, xg, norm_w, seg)

    if late_weights is not None:
        w = {**w, **late_weights[1](ogn)}
    yf = _mm(o_fox, w["w_branch_fox"], BF16, "branch_fox")
    yg = _mm(ogn, w["w_branch_gdn"], BF16, "branch_gdn")
    mix = _mix_fwd(yf, yg, gt, w["gate_bias"])
    h1 = _mm(mix, w["w_out"], F32, "out_proj", res=h0)
    a2 = _rmsnorm_fwd(h1, w["norm_ffn_w"])
    up = _mm(a2, w["ffn_w_up"], BF16, "ffn_up")
    act = _ffn_act_fwd(up, w["ffn_conv_w"], w["ffn_conv_b"])
    h2 = _mm(act, w["ffn_w_down"], F32, "ffn_down", res=h1)
    loss, dh2, dh2b, g_final = _final_loss(h2, tgt, w["norm_final_w"].reshape(1, D_MODEL), seq)

    grads = {"norm_final_w": g_final.reshape(D_MODEL)}
    grads["ffn_w_down"] = _mm_tn(act, dh2b, "wgrad_ffn_down")
    dact = _mm(dh2b, w["ffn_w_down"], F32, "dgrad_ffn_down", b_transposed=True)
    dup, g_cw, g_cb = _ffn_act_bwd(up, w["ffn_conv_w"], w["ffn_conv_b"], dact)
    grads["ffn_conv_w"], grads["ffn_conv_b"] = g_cw, g_cb
    grads["ffn_w_up"] = _mm_tn(a2, dup, "wgrad_ffn_up")
    da2 = _mm(dup, w["ffn_w_up"], BF16, "dgrad_ffn_up", b_transposed=True)
    dh1, dh1b, grads["norm_ffn_w"] = _rmsnorm_bwd(h1, da2, w["norm_ffn_w"], dh2)
    grads["w_out"] = _mm_tn(mix, dh1b, "wgrad_out")
    dmix = _mm(dh1b, w["w_out"], BF16, "dgrad_out", b_transposed=True)
    dyf, dyg, dgt, grads["gate_bias"] = _mix_bwd(dmix, yf, yg, gt, w["gate_bias"])
    grads["w_branch_fox"] = _mm_tn(o_fox, dyf, "wgrad_branch_fox")
    grads["w_branch_gdn"] = _mm_tn(ogn, dyg, "wgrad_branch_gdn")
    do_fox = _mm(dyf, w["w_branch_fox"], F32, "dgrad_branch_fox", b_transposed=True)
    dogn = _mm(dyg, w["w_branch_gdn"], F32, "dgrad_branch_gdn", b_transposed=True)

    dog, dz, g_nw = _gdn_post_bwd(og, xg, norm_w, seg, dogn)
    grads["gdn_norm_w"] = g_nw.reshape(HEADS, HEAD_DIM).sum(axis=0)[None]
    dqh, dkh, dvh, dscal_g, dgct = _gdn_bwd(qh, kh, vh, scal, gct, s0s, dog)
    dxg, grads["gdn_conv_w"] = _gdn_pre_bwd(xg, w["gdn_conv_w"], seg, dqh, dkh, dvh)

    doa, doat = _fox_bwd_prep(do_fox, oa)
    dfq, dscal_c = _fox_bwd_post(*_fox_bwd(qb, qbt, ka, kat, va, doa, doat))

    dscal = dscal_c + dscal_g + lanes_pad(dgct.T, 2 * HEADS)
    dsm, dpb, dps = _gate_bwd(sm, pbias, pscale, dscal)
    grads["fgt_bias"] = dpb[:, :HEADS]
    grads["gdn_dt_bias"] = dpb[:, 2 * HEADS:3 * HEADS]
    grads["gdn_a_log"] = dps[:, 2 * HEADS:3 * HEADS] * neg_exp_a

    dproj = jnp.concatenate([dfq, dxg, dz, dgt, dsm], axis=1)
    g_r = _mm_tn(a1, dproj, "wgrad_in")
    grads["w_in"] = jnp.concatenate([g_r[:, :1536], g_r[:, 5632:5640], g_r[:, 1536:3072], g_r[:, 3072:3584],
                                     g_r[:, 5640:5648], g_r[:, 5648:5656], g_r[:, 3584:5632]], axis=1)
    token, handle = early(grads) if early is not None else (jnp.zeros((8, LANES), F32), None)
    w_late = w_r + token[0, 0].astype(BF16)
    da1 = _mm(dproj, w_late, BF16, "dgrad_in", b_transposed=True)
    dh0, _, grads["norm_mix_w"] = _rmsnorm_bwd(h0, da1, w["norm_mix_w"], dh1)
    grads["meta_tokens"] = dh0[:N_META]
    return loss, dh0[N_META:N_META + seq], grads, handle


def _shard_pieces(arrs):
    return [arrs[n][0] if arrs[n].ndim == 3 else arrs[n] for n in SHARDED]


def _full_grad_blocks(grads):
    g = grads
    cols = lambda a, wd: _col_blocks(a, wd)
    rows = lambda a: a.reshape(N_DEV, a.shape[0] // N_DEV, a.shape[1])
    return [cols(g["w_in"], IN_WIDTH // N_DEV), cols(g["gdn_conv_w"], 3 * WIDTH // N_DEV),
            cols(g["w_branch_fox"], D_MODEL // N_DEV), cols(g["w_branch_gdn"], D_MODEL // N_DEV), rows(g["w_out"]),
            cols(g["ffn_w_up"], 2 * D_FF // N_DEV), cols(g["ffn_conv_w"], 2 * D_FF // N_DEV), rows(g["ffn_w_down"])]


def kernel(x, meta_tokens, w_in, fgt_bias, gdn_conv_w, gdn_a_log, gdn_dt_bias, gdn_norm_w, gate_bias, w_branch_fox, w_branch_gdn, w_out, norm_mix_w, norm_ffn_w, ffn_w_up, ffn_conv_w, ffn_conv_b, ffn_w_down, norm_final_w, loss_target, m_meta_tokens, m_w_in, m_fgt_bias, m_gdn_conv_w, m_gdn_a_log, m_gdn_dt_bias, m_gdn_norm_w, m_gate_bias, m_w_branch_fox, m_w_branch_gdn, m_w_out, m_norm_mix_w, m_norm_ffn_w, m_ffn_w_up, m_ffn_conv_w, m_ffn_conv_b, m_ffn_w_down, m_norm_final_w, v_meta_tokens, v_w_in, v_fgt_bias, v_gdn_conv_w, v_gdn_a_log, v_gdn_dt_bias, v_gdn_norm_w, v_gate_bias, v_w_branch_fox, v_w_branch_gdn, v_w_out, v_norm_mix_w, v_norm_ffn_w, v_ffn_w_up, v_ffn_conv_w, v_ffn_conv_b, v_ffn_w_down, v_norm_final_w):
    wts = dict(meta_tokens=meta_tokens, w_in=w_in, fgt_bias=fgt_bias, gdn_conv_w=gdn_conv_w, gdn_a_log=gdn_a_log,
               gdn_dt_bias=gdn_dt_bias, gdn_norm_w=gdn_norm_w, gate_bias=gate_bias, w_branch_fox=w_branch_fox,
               w_branch_gdn=w_branch_gdn, w_out=w_out, norm_mix_w=norm_mix_w, norm_ffn_w=norm_ffn_w,
               ffn_w_up=ffn_w_up, ffn_conv_w=ffn_conv_w, ffn_conv_b=ffn_conv_b, ffn_w_down=ffn_w_down,
               norm_final_w=norm_final_w)
    mom = dict(meta_tokens=m_meta_tokens, w_in=m_w_in, fgt_bias=m_fgt_bias, gdn_conv_w=m_gdn_conv_w,
               gdn_a_log=m_gdn_a_log, gdn_dt_bias=m_gdn_dt_bias, gdn_norm_w=m_gdn_norm_w, gate_bias=m_gate_bias,
               w_branch_fox=m_w_branch_fox, w_branch_gdn=m_w_branch_gdn, w_out=m_w_out, norm_mix_w=m_norm_mix_w,
               norm_ffn_w=m_norm_ffn_w, ffn_w_up=m_ffn_w_up, ffn_conv_w=m_ffn_conv_w, ffn_conv_b=m_ffn_conv_b,
               ffn_w_down=m_ffn_w_down, norm_final_w=m_norm_final_w)
    var = dict(meta_tokens=v_meta_tokens, w_in=v_w_in, fgt_bias=v_fgt_bias, gdn_conv_w=v_gdn_conv_w,
               gdn_a_log=v_gdn_a_log, gdn_dt_bias=v_gdn_dt_bias, gdn_norm_w=v_gdn_norm_w, gate_bias=v_gate_bias,
               w_branch_fox=v_w_branch_fox, w_branch_gdn=v_w_branch_gdn, w_out=v_w_out, norm_mix_w=v_norm_mix_w,
               norm_ffn_w=v_norm_ffn_w, ffn_w_up=v_ffn_w_up, ffn_conv_w=v_ffn_conv_w, ffn_conv_b=v_ffn_conv_b,
               ffn_w_down=v_ffn_w_down, norm_final_w=v_norm_final_w)

    sh = dict(zip(SHARDED, _shard_pieces(wts)))
    me = 4 * lax.axis_index("x") + 2 * lax.axis_index("y") + lax.axis_index("c")
    late_names = MATMUL[1:]
    late_sems_send, late_sems_recv, late_src, late_land, late_token = _gather_start(
        [sh[n].astype(BF16) for n in late_names])
    exact_shapes = [sh[n].shape for n in EXACT]
    rows_exact = _rows_for(sum(int(np.prod(s)) for s in exact_shapes), 8)
    g_in, g_exact = _all_gather([sh["w_in"].astype(BF16), _pad_flat([sh[n] for n in EXACT], rows_exact)])
    meta_full, conv_full, fconv_full = (_cat_cols(a) for a in _unpack_gathered(g_exact, exact_shapes))
    full = dict(
        meta_tokens=meta_full, w_in=_cat_cols(g_in), gdn_conv_w=conv_full, ffn_conv_w=fconv_full,
        fgt_bias=fgt_bias, gdn_a_log=gdn_a_log, gdn_dt_bias=gdn_dt_bias, gdn_norm_w=gdn_norm_w, gate_bias=gate_bias,
        norm_mix_w=norm_mix_w, norm_ffn_w=norm_ffn_w, ffn_conv_b=ffn_conv_b, norm_final_w=norm_final_w)

    def fetch_late_weights(after):
        shards, lands = _gather_wait(late_sems_send, late_sems_recv, late_src, late_land, after)
        g_bf, g_bg, g_out, g_up, g_down = (lax.dynamic_update_slice_in_dim(land, s[None], me, 0)
                                           for s, land in zip(shards, lands))
        return dict(w_branch_fox=_cat_cols(g_bf), w_branch_gdn=_cat_cols(g_bg), w_out=g_out.reshape(D_MODEL, D_MODEL),
                    ffn_w_up=_cat_cols(g_up), ffn_w_down=g_down.reshape(D_FF, D_MODEL))

    def start_exchange(grads_so_far):
        blocks = [b.astype(BF16) for b in _full_grad_blocks(grads_so_far)]
        send_sems, recv_sems, src_thru, land_thru, token = _exchange_start(blocks)
        return token, (send_sems, recv_sems, src_thru, land_thru)

    loss, grad_x, grads, handle = _local_step(x[0], loss_target[0], full, early=start_exchange,
                                              late_weights=(late_token, fetch_late_weights))
    sent, landed = _exchange_wait(*handle, after=grad_x)
    own = lambda src, land: lax.dynamic_update_slice_in_dim(land, lax.dynamic_slice_in_dim(src, me, 1, 0), me, 0)
    received = [own(src, land) for src, land in zip(sent, landed)]

    rep_parts = [grads[n] for n in REPLICATED] + [loss[:, :1]]
    rep_shapes = [wts[n].shape for n in REPLICATED]
    rows_small = _rows_for(sum(int(np.prod(p.shape)) for p in rep_parts), 8)
    meta_recv, small_recv = _grad_exchange([_col_blocks(grads["meta_tokens"], LANES).astype(BF16)],
                                           _pad_flat(rep_parts, rows_small))
    received = [meta_recv] + received + [small_recv]

    result = {}
    kinds = ("grad", "delta", "new_m", "new_v")
    for n, recv in zip(SHARDED, received[:-1]):
        outs = _adamw(sh[n], _shard_pieces(mom)[SHARDED.index(n)], _shard_pieces(var)[SHARDED.index(n)], recv,
                      "adamw_" + n)
        for kind, a in zip(kinds, outs):
            result[kind, n] = a.reshape(wts[n].shape)
    rep_w = _pad_flat([wts[n] for n in REPLICATED] + [jnp.zeros((1, 1), F32)], rows_small)
    rep_m = _pad_flat([mom[n] for n in REPLICATED] + [jnp.zeros((1, 1), F32)], rows_small)
    rep_v = _pad_flat([var[n] for n in REPLICATED] + [jnp.ones((1, 1), F32)], rows_small)
    outs_r = _adamw(rep_w, rep_m, rep_v, received[-1], "adamw_replicated")
    for kind, br in zip(kinds, outs_r):
        for n, a in zip(REPLICATED, _unpack(br, rep_shapes)):
            result[kind, n] = a
    n_rep = sum(int(np.prod(s)) for s in rep_shapes)
    total_loss = outs_r[0].reshape(-1)[n_rep]
    out = [total_loss, grad_x[None]]
    for kind in ("grad", "delta", "new_m", "new_v"):
        out += [result[kind, n] for n in WEIGHTS]
    return tuple(out)
```

```python
import jax
import jax.numpy as jnp
import numpy as np
from jax import lax
from jax.experimental import pallas as pl
from jax.experimental.pallas import tpu as pltpu

F32 = jnp.float32
BF16 = jnp.bfloat16

D_MODEL = 1024
N_META = 16
HEADS = 8
HEAD_DIM = 64
WIDTH = HEADS * HEAD_DIM
CHUNK = 64
GDN_CONV = 4
D_FF = 2816
FFN_CONV = 3
IN_WIDTH = 5656
RMS_EPS = 1e-6
NEG = -1e30
AUG = 128
N_DEV = 8
LANES = 128

ADAM_LR = 0.001
ADAM_B1 = 0.9
ADAM_B2 = 0.999
ADAM_EPS = 1e-08
ADAM_WD = 0.01
ADAM_STEP = 10

VMEM_LIMIT = 56 * 1024 * 1024
MM_VMEM_BUDGET = 36 * 1024 * 1024
FFN_LANES = 128
HI = lax.Precision.HIGH
MESH_ID = pl.DeviceIdType.MESH


def _pick(n, cands):
    for c in cands:
        if n % c == 0:
            return c
    raise ValueError(f"no tile for {n} in {cands}")


def _params(*sem):
    return pltpu.CompilerParams(dimension_semantics=sem if sem else None, vmem_limit_bytes=VMEM_LIMIT)


def _padded_tokens(seq):
    t = -(-(N_META + seq) // 128) * 128
    if t > 1280 and t % 640:
        t = -(-t // 640) * 640
    return t


ROW_TILES = (640, 512, 384, 256, 128)


def _rmsnorm_fwd(h, gain):
    t, d = h.shape
    tr = _pick(t, ROW_TILES)

    def body(h_ref, g_ref, o_ref):
        x = h_ref[...]
        r = lax.rsqrt(jnp.mean(x * x, axis=-1, keepdims=True) + RMS_EPS)
        o_ref[...] = (x * r * g_ref[...]).astype(o_ref.dtype)

    return pl.pallas_call(
        body, grid=(t // tr,), name="rmsnorm_fwd",
        in_specs=[pl.BlockSpec((tr, d), lambda i: (i, 0)), pl.BlockSpec((1, d), lambda i: (0, 0))],
        out_specs=pl.BlockSpec((tr, d), lambda i: (i, 0)),
        out_shape=jax.ShapeDtypeStruct((t, d), BF16),
        compiler_params=_params("arbitrary"),
    )(h, gain)


def _rmsnorm_bwd(h, dy, gain, dres):
    t, d = h.shape
    tr = _pick(t, (320, 256, 128))

    def body(h_ref, dy_ref, g_ref, dres_ref, dh_ref, dhb_ref, dg_ref):
        x = h_ref[...]
        dyv = dy_ref[...].astype(F32)
        r = lax.rsqrt(jnp.mean(x * x, axis=-1, keepdims=True) + RMS_EPS)
        gy = dyv * g_ref[...]
        m = jnp.mean(gy * x, axis=-1, keepdims=True)
        dh = dres_ref[...] + r * gy - x * (r * r * r * m)
        dh_ref[...] = dh
        dhb_ref[...] = dh.astype(BF16)

        @pl.when(pl.program_id(0) == 0)
        def _():
            dg_ref[...] = jnp.zeros_like(dg_ref)

        dg_ref[...] += jnp.sum(dyv * x * r, axis=0, keepdims=True)

    row = pl.BlockSpec((tr, d), lambda i: (i, 0))
    vec = pl.BlockSpec((1, d), lambda i: (0, 0))
    return pl.pallas_call(
        body, grid=(t // tr,), name="rmsnorm_bwd",
        in_specs=[row, row, vec, row], out_specs=[row, row, vec],
        out_shape=[jax.ShapeDtypeStruct((t, d), F32), jax.ShapeDtypeStruct((t, d), BF16),
                   jax.ShapeDtypeStruct((1, d), F32)],
        compiler_params=_params("arbitrary"),
    )(h, dy, gain, dres)


def _mm(a, b, out_dtype, name, res=None):
    m, k = a.shape
    _, n = b.shape
    tm = _pick(m, ROW_TILES)
    out_bytes = jnp.dtype(out_dtype).itemsize + (4 if res is not None else 0)
    fits = lambda tn: 4 * tm * k + 4 * k * tn + 2 * tm * tn * out_bytes <= MM_VMEM_BUDGET
    tn = next(c for c in (n, 2816, 2048, 1536, 1408, 1024, 512, 384, 256, 128) if n % c == 0 and fits(c))

    def body(*refs):
        if res is None:
            a_ref, b_ref, o_ref = refs
        else:
            a_ref, b_ref, r_ref, o_ref = refs
        out = jnp.dot(a_ref[...], b_ref[...], preferred_element_type=F32)
        if res is not None:
            out = out + r_ref[...]
        o_ref[...] = out.astype(o_ref.dtype)

    in_specs = [pl.BlockSpec((tm, k), lambda i, j: (i, 0)), pl.BlockSpec((k, tn), lambda i, j: (0, j))]
    args = [a, b]
    if res is not None:
        in_specs.append(pl.BlockSpec((tm, tn), lambda i, j: (i, j)))
        args.append(res)
    return pl.pallas_call(
        body, grid=(m // tm, n // tn), name=name,
        in_specs=in_specs, out_specs=pl.BlockSpec((tm, tn), lambda i, j: (i, j)),
        out_shape=jax.ShapeDtypeStruct((m, n), out_dtype),
        compiler_params=_params("parallel", "parallel"),
    )(*args)


def _mm_tn(a, g, name):
    t, k = a.shape
    _, n = g.shape
    tk = _pick(k, (1024, 1408, 512))
    tn = _pick(n, (512, 640, 384, 256, 128))
    tt = next(c for c in (3328, 1280) + ROW_TILES
              if t % c == 0 and 4 * c * (tk + tn) + 8 * tk * tn <= MM_VMEM_BUDGET)
    nt = t // tt

    def body(a_ref, g_ref, o_ref):
        @pl.when(pl.program_id(2) == 0)
        def _():
            o_ref[...] = jnp.zeros_like(o_ref)

        o_ref[...] += lax.dot_general(a_ref[...], g_ref[...], (((0,), (0,)), ((), ())),
                                      preferred_element_type=F32)

    return pl.pallas_call(
        body, grid=(k // tk, n // tn, nt), name=name,
        in_specs=[pl.BlockSpec((tt, tk), lambda i, j, s: (s, i)), pl.BlockSpec((tt, tn), lambda i, j, s: (s, j))],
        out_specs=pl.BlockSpec((tk, tn), lambda i, j, s: (i, j)),
        out_shape=jax.ShapeDtypeStruct((k, n), F32),
        compiler_params=_params("parallel", "parallel", "arbitrary"),
    )(a, g)


def _split3_exact(x):
    def top(v):
        return lax.bitcast_convert_type(lax.bitcast_convert_type(v, jnp.int32) & jnp.int32(-65536), F32)

    hi = top(x)
    r1 = x - hi
    mid = top(r1)
    return hi, mid, r1 - mid


def _pair_head(ref, h, rows):
    x = ref[:, 128 * (h // 2):128 * (h // 2) + 128].astype(F32)
    return pltpu.roll(x, HEAD_DIM, axis=1) if h % 2 else x


def _lanes(rows):
    return lax.broadcasted_iota(jnp.int32, (rows, AUG), 1)


def _fox_prep(fq, scal):
    t = fq.shape[0]
    tt = _pick(t, (256, 128))

    def body(q_ref, k_ref, v_ref, s_ref, qa_ref, ka_ref, va_ref, kt_ref, vt_ref):
        lane = _lanes(tt)
        chi, cmid, clo = _split3_exact(s_ref[...])
        ones = lambda lo: jnp.where((lane >= lo) & (lane < lo + 3), 1.0, 0.0)
        for h in range(HEADS):
            col = lambda a: jnp.broadcast_to(a[:, h:h + 1], (tt, AUG))
            c1, c2, c3 = col(chi), col(cmid), col(clo)
            qx = jnp.where(lane == 64, c1, jnp.where(lane == 65, c2, jnp.where(lane == 66, c3, ones(67))))
            kx = jnp.where(lane == 67, -c1, jnp.where(lane == 68, -c2, jnp.where(lane == 69, -c3, ones(64) + ones(70))))
            qa_ref[h] = jnp.where(lane < HEAD_DIM, _pair_head(q_ref, h, tt) * (HEAD_DIM ** -0.5), qx).astype(BF16)
            k_aug = jnp.where(lane < HEAD_DIM, _pair_head(k_ref, h, tt), kx)
            ka_ref[h] = k_aug.astype(BF16)
            kt_ref[h] = k_aug.T.astype(BF16)
            v_aug = jnp.where(lane < HEAD_DIM, _pair_head(v_ref, h, tt), ones(64))
            va_ref[h] = v_aug.astype(BF16)
            vt_ref[h] = v_aug.T.astype(BF16)

    out = pl.BlockSpec((HEADS, tt, AUG), lambda i: (0, i, 0))
    out_t = pl.BlockSpec((HEADS, AUG, tt), lambda i: (0, 0, i))
    shp = jax.ShapeDtypeStruct((HEADS, t, AUG), BF16)
    shp_t = jax.ShapeDtypeStruct((HEADS, AUG, t), BF16)
    return pl.pallas_call(
        body, grid=(t // tt,), name="fox_prep",
        in_specs=[pl.BlockSpec((tt, WIDTH), lambda i: (i, 0)), pl.BlockSpec((tt, WIDTH), lambda i: (i, 1)),
                  pl.BlockSpec((tt, WIDTH), lambda i: (i, 2)), pl.BlockSpec((tt, LANES), lambda i: (i, 0))],
        out_specs=[out, out, out, out_t, out_t], out_shape=[shp, shp, shp, shp_t, shp_t],
        compiler_params=_params("parallel"),
    )(fq, fq, fq, scal)


def _fox_post(oa):
    t = oa.shape[1]
    tt = _pick(t, (256, 128))

    def body(o_ref, out_ref):
        out_ref[...] = jnp.concatenate([o_ref[h][:, :HEAD_DIM] for h in range(HEADS)], axis=1).astype(BF16)

    return pl.pallas_call(
        body, grid=(t // tt,), name="fox_post",
        in_specs=[pl.BlockSpec((HEADS, tt, AUG), lambda i: (0, i, 0))],
        out_specs=pl.BlockSpec((tt, WIDTH), lambda i: (i, 0)),
        out_shape=jax.ShapeDtypeStruct((t, WIDTH), BF16),
        compiler_params=_params("parallel"),
    )(oa)


def _fox_bwd_prep(do, oa):
    t = do.shape[0]
    tt = _pick(t, (256, 128))

    def body(d_ref, o_ref, out_ref, outt_ref):
        lane = _lanes(tt)
        for h in range(HEADS):
            x = _pair_head(d_ref, h, tt)
            delta = jnp.sum(jnp.where(lane < HEAD_DIM, x * o_ref[h], 0.0), axis=1, keepdims=True)
            hi, mid, lo = _split3_exact(jnp.broadcast_to(-delta, (tt, AUG)))
            ex = jnp.where(lane == 64, hi, jnp.where(lane == 65, mid, jnp.where(lane == 66, lo, 0.0)))
            do_aug = jnp.where(lane < HEAD_DIM, x, ex)
            out_ref[h] = do_aug.astype(BF16)
            outt_ref[h] = do_aug.T.astype(BF16)

    hm = pl.BlockSpec((HEADS, tt, AUG), lambda i: (0, i, 0))
    return pl.pallas_call(
        body, grid=(t // tt,), name="fox_bwd_prep",
        in_specs=[pl.BlockSpec((tt, WIDTH), lambda i: (i, 0)), hm],
        out_specs=[hm, pl.BlockSpec((HEADS, AUG, tt), lambda i: (0, 0, i))],
        out_shape=[jax.ShapeDtypeStruct((HEADS, t, AUG), BF16), jax.ShapeDtypeStruct((HEADS, AUG, t), BF16)],
        compiler_params=_params("parallel"),
    )(do, oa)


def _fox_bwd_post(dqt, dkt, dvt):
    t = dqt.shape[2]
    tt = _pick(t, (256, 128))

    def body(dq_ref, dk_ref, dv_ref, out_ref, dsc_ref):
        lane = _lanes(tt)
        dqs = [dq_ref[h].T for h in range(HEADS)]
        dks = [dk_ref[h].T for h in range(HEADS)]
        heads = lambda xs: jnp.concatenate([x[:, :HEAD_DIM] for x in xs], axis=1)
        out_ref[:, 0:WIDTH] = (heads(dqs) * (HEAD_DIM ** -0.5)).astype(BF16)
        out_ref[:, WIDTH:2 * WIDTH] = heads(dks).astype(BF16)
        out_ref[:, 2 * WIDTH:] = heads([dv_ref[h].T for h in range(HEADS)]).astype(BF16)
        dsc = jnp.zeros((tt, LANES), F32)
        for h in range(HEADS):
            both = jnp.where(lane == HEAD_DIM, dqs[h], 0.0) - jnp.where(lane == HEAD_DIM + 3, dks[h], 0.0)
            dsc = jnp.where(lane == h, jnp.sum(both, axis=1, keepdims=True), dsc)
        dsc_ref[...] = dsc

    hm = pl.BlockSpec((HEADS, AUG, tt), lambda i: (0, 0, i))
    return pl.pallas_call(
        body, grid=(t // tt,), name="fox_bwd_post",
        in_specs=[hm, hm, hm],
        out_specs=[pl.BlockSpec((tt, 3 * WIDTH), lambda i: (i, 0)), pl.BlockSpec((tt, LANES), lambda i: (i, 0))],
        out_shape=[jax.ShapeDtypeStruct((t, 3 * WIDTH), BF16), jax.ShapeDtypeStruct((t, LANES), F32)],
        compiler_params=_params("parallel"),
    )(dqt, dkt, dvt)


def _fox_fwd(qa, ka, vat, tq=None):
    h, t, _ = qa.shape
    tq = tq or _pick(t, ROW_TILES)

    def body(q_ref, k_ref, vt_ref, o_ref, qb_ref, qbt_ref, s_ref):
        i = pl.program_id(1)
        q = q_ref[...]
        krow = lax.broadcasted_iota(jnp.int32, (tq, tq), 0)
        qcol = lax.broadcasted_iota(jnp.int32, (tq, tq), 1)
        rows = lambda j: pl.ds(pl.multiple_of(j * tq, tq), tq)

        def scores(j, slot):
            s_ref[slot] = lax.dot_general(k_ref[rows(j), :], q, (((1,), (1,)), ((), ())), preferred_element_type=F32)

        def update(j, slot, carry, masked):
            m, acc = carry
            s = s_ref[slot]
            if masked:
                s = jnp.where(qcol >= krow, s, NEG)
            m_new = jnp.maximum(m, jnp.max(s, axis=0, keepdims=True))
            p = jnp.exp(s - m_new)
            alpha = jnp.exp(m - m_new)
            return m_new, acc * alpha + jnp.dot(vt_ref[:, rows(j)], p.astype(BF16), preferred_element_type=F32)

        def pair(j, carry):
            scores(j + 1, 1)
            carry = update(j, 0, carry, False)
            scores(j + 2, 0)
            return update(j + 1, 1, carry, False)

        def odd_tail(carry):
            scores(i, 1)
            return update(i, 1, update(i - 1, 0, carry, False), True)

        scores(0, 0)
        carry = (jnp.full((1, tq), NEG, F32), jnp.zeros((AUG, tq), F32))
        carry = lax.fori_loop(0, i // 4, lambda jj, c: pair(4 * jj + 2, pair(4 * jj, c)), carry)
        carry = lax.fori_loop(0, (i % 4) // 2, lambda jj, c: pair(4 * (i // 4), c), carry)
        m, acc = lax.cond(i % 2 == 1, odd_tail, lambda c: update(i, 0, c, True), carry)
        sub = lax.broadcasted_iota(jnp.int32, (AUG, tq), 0)
        l = jnp.sum(jnp.where(sub == HEAD_DIM, acc, 0.0), axis=0, keepdims=True)
        out = jnp.where(sub < HEAD_DIM, acc / l, m + jnp.log(l)).T
        o_ref[...] = out
        lane = lax.broadcasted_iota(jnp.int32, (tq, AUG), 1)
        lse = jnp.broadcast_to(jnp.sum(jnp.where(lane == HEAD_DIM, out, 0.0), axis=1, keepdims=True), (tq, AUG))
        hi, mid, lo = _split3_exact(-lse)
        qb = jnp.where(lane == 70, hi, jnp.where(lane == 71, mid, jnp.where(lane == 72, lo, q.astype(F32))))
        qb_ref[...] = qb.astype(BF16)
        qbt_ref[...] = qb.T.astype(BF16)

    blk = pl.BlockSpec((None, tq, AUG), lambda hh, i: (hh, i, 0))
    return pl.pallas_call(
        body, grid=(h, t // tq), name="fox_fwd",
        in_specs=[blk, pl.BlockSpec((None, t, AUG), lambda hh, i: (hh, 0, 0)),
                  pl.BlockSpec((None, AUG, t), lambda hh, i: (hh, 0, 0))],
        out_specs=[blk, blk, pl.BlockSpec((None, AUG, tq), lambda hh, i: (hh, 0, i))],
        out_shape=[jax.ShapeDtypeStruct((h, t, AUG), F32), jax.ShapeDtypeStruct((h, t, AUG), BF16),
                   jax.ShapeDtypeStruct((h, AUG, t), BF16)],
        scratch_shapes=[pltpu.VMEM((2, tq, tq), F32)],
        compiler_params=_params("parallel", "arbitrary"),
    )(qa, ka, vat)


def _fox_bwd(qb, qbt, ka, kat, va, doa, doat, tq=None):
    h, t, _ = qb.shape
    tq = tq or _pick(t, ROW_TILES)
    nq = t // tq

    def body(q_ref, qt_ref, k_ref, kt_ref, v_ref, do_ref, dot_ref, dqt_ref, dkt_ref, dvt_ref, s_ref, dp_ref):
        j = pl.program_id(1)
        n = nq - j

        @pl.when(j == 0)
        def _():
            dqt_ref[...] = jnp.zeros_like(dqt_ref)

        dkt_ref[...] = jnp.zeros_like(dkt_ref)
        dvt_ref[...] = jnp.zeros_like(dvt_ref)
        kj = k_ref[...]
        ktj = kt_ref[...]
        vj = v_ref[...]
        qrow = lax.broadcasted_iota(jnp.int32, (tq, tq), 0)
        kcol = lax.broadcasted_iota(jnp.int32, (tq, tq), 1)
        rows = lambda i: pl.ds(pl.multiple_of(i * tq, tq), tq)
        nt_dims = (((1,), (1,)), ((), ()))

        def scores(i, slot):
            s_ref[slot] = lax.dot_general(q_ref[rows(i), :], kj, nt_dims, preferred_element_type=F32)
            dp_ref[slot] = lax.dot_general(do_ref[rows(i), :], vj, nt_dims, preferred_element_type=F32)

        def update(i, slot):
            p = jnp.exp(jnp.where((qrow >= kcol) | (i > j), s_ref[slot], NEG))
            ds = (p * dp_ref[slot]).astype(BF16)
            dvt_ref[...] += jnp.dot(dot_ref[:, rows(i)], p.astype(BF16), preferred_element_type=F32)
            dkt_ref[...] += jnp.dot(qt_ref[:, rows(i)], ds, preferred_element_type=F32)
            dqt_ref[:, rows(i)] += lax.dot_general(ktj, ds, nt_dims, preferred_element_type=F32)

        def pair(i0):
            scores(i0 + 1, 1)
            update(i0, 0)
            scores(jnp.minimum(i0 + 2, nq - 1), 0)
            update(i0 + 1, 1)

        def quad(kk, carry):
            pair(j + 4 * kk)
            pair(j + 4 * kk + 2)
            return carry

        def last_pair(kk, carry):
            pair(j + 4 * (n // 4))
            return carry

        scores(j, 0)
        lax.fori_loop(0, n // 4, quad, 0)
        lax.fori_loop(0, (n % 4) // 2, last_pair, 0)

        @pl.when(n % 2 == 1)
        def _():
            update(nq - 1, 0)

    once = pl.Buffered(1)
    full = pl.BlockSpec((None, t, AUG), lambda hh, j: (hh, 0, 0), pipeline_mode=once)
    full_t = pl.BlockSpec((None, AUG, t), lambda hh, j: (hh, 0, 0), pipeline_mode=once)
    blk = pl.BlockSpec((None, tq, AUG), lambda hh, j: (hh, j, 0))
    blk_t = pl.BlockSpec((None, AUG, tq), lambda hh, j: (hh, 0, j))
    shp = jax.ShapeDtypeStruct((h, AUG, t), F32)
    return pl.pallas_call(
        body, grid=(h, nq), name="fox_bwd",
        in_specs=[full, full_t, blk, blk_t, blk, full, full_t],
        out_specs=[pl.BlockSpec((None, AUG, t), lambda hh, j: (hh, 0, 0)), blk_t, blk_t], out_shape=[shp, shp, shp],
        scratch_shapes=[pltpu.VMEM((2, tq, tq), F32), pltpu.VMEM((2, tq, tq), F32)],
        compiler_params=_params("parallel", "arbitrary"),
    )(qb, qbt, ka, kat, va, doa, doat)


def _seg_matrix():
    idx = np.arange(WIDTH) // HEAD_DIM
    return jnp.asarray((idx[:, None] == idx[None, :]).astype(np.float32))


def _segsum(x, e):
    return jnp.dot(x, e, precision=HI, preferred_element_type=F32)


def _silu(x):
    return x * jax.nn.sigmoid(x)


def _silu_grad(x):
    s = jax.nn.sigmoid(x)
    return s * (1.0 + x * (1.0 - s))


def _shift_down(x, prev8, k):
    r = pltpu.roll(x, k, axis=0)
    p = pltpu.roll(prev8, k, axis=0)
    row = lax.broadcasted_iota(jnp.int32, prev8.shape, 0)
    head = jnp.where(row < k, p, r[:8])
    return jnp.concatenate([head, r[8:]], axis=0)


def _shift_up(x, next8, k):
    n = x.shape[0]
    r = pltpu.roll(x, n - k, axis=0)
    p = pltpu.roll(next8, 8 - k, axis=0)
    row = lax.broadcasted_iota(jnp.int32, next8.shape, 0)
    tail = jnp.where(row >= 8 - k, p, r[n - 8:])
    return jnp.concatenate([r[:n - 8], tail], axis=0)


def _causal_conv(x, prev8, w_ref, width, cols=slice(None)):
    y = x * w_ref[width - 1:width, cols]
    for k in range(1, width):
        y = y + _shift_down(x, prev8, k) * w_ref[width - 1 - k:width - k, cols]
    return y


def _causal_conv_bwd(x, prev8, dy, dnext8, w_ref, dw_ref, width, cols=slice(None)):
    dx = dy * w_ref[width - 1:width, cols]
    dw_ref[width - 1:width, cols] += jnp.sum(dy * x, axis=0, keepdims=True)
    for k in range(1, width):
        dx = dx + _shift_up(dy, dnext8, k) * w_ref[width - 1 - k:width - k, cols]
        dw_ref[width - 1 - k:width - k, cols] += jnp.sum(dy * _shift_down(x, prev8, k), axis=0, keepdims=True)
    return dx


HALO = 16


def _prev_spec(tt, width, tile=lambda i: i):
    return pl.BlockSpec((HALO, width), lambda i: (jnp.maximum(tile(i) * (tt // HALO) - 1, 0), 0))


def _prev8(p_ref, cols=slice(None)):
    return p_ref[:, cols].astype(F32)[HALO - 8:]


def _store_heads(ref, x):
    for h in range(HEADS):
        ref[h] = x[:, HEAD_DIM * h:HEAD_DIM * (h + 1)]


def _load_heads(ref):
    return jnp.concatenate([ref[h] for h in range(HEADS)], axis=1)


def _softplus(z):
    return jnp.maximum(z, 0.0) + jnp.log1p(jnp.exp(-jnp.abs(z)))


def _tri_masks(tt):
    r = lax.broadcasted_iota(jnp.int32, (tt, tt), 0)
    c = lax.broadcasted_iota(jnp.int32, (tt, tt), 1)
    same_chunk = lax.shift_right_logical(r, 6) == lax.shift_right_logical(c, 6)
    return r, c, same_chunk


def _gate_fwd(small, pbias, pscale):
    t = small.shape[0]
    tt = _pick(t, (256, 128))

    def body(x_ref, pb_ref, ps_ref, o_ref, carry_ref):
        @pl.when(pl.program_id(0) == 0)
        def _():
            carry_ref[...] = jnp.zeros_like(carry_ref)

        lane = lax.broadcasted_iota(jnp.int32, (tt, LANES), 1)
        z = x_ref[...] + pb_ref[...]
        log_f = jnp.where(lane < HEADS, -_softplus(-z), 0.0)
        g = jnp.where((lane >= 2 * HEADS) & (lane < 3 * HEADS), ps_ref[...] * _softplus(z), 0.0)
        r, c, same_chunk = _tri_masks(tt)
        lower = jnp.where(r >= c, 1.0, 0.0)
        lower_chunk = jnp.where((r >= c) & same_chunk, 1.0, 0.0)
        csum = jnp.dot(lower, log_f, precision=lax.Precision.HIGHEST, preferred_element_type=F32) + carry_ref[...]
        gc = jnp.dot(lower_chunk, g, precision=lax.Precision.HIGHEST, preferred_element_type=F32)
        carry_ref[...] += jnp.sum(log_f, axis=0, keepdims=True)
        o_ref[...] = jnp.where(lane < HEADS, csum, jnp.where(lane < 2 * HEADS, jax.nn.sigmoid(z), gc))

    row = pl.BlockSpec((tt, LANES), lambda i: (i, 0))
    vec = pl.BlockSpec((1, LANES), lambda i: (0, 0))
    return pl.pallas_call(
        body, grid=(t // tt,), name="gate_fwd", in_specs=[row, vec, vec], out_specs=row,
        out_shape=jax.ShapeDtypeStruct((t, LANES), F32),
        scratch_shapes=[pltpu.VMEM((1, LANES), F32)],
        compiler_params=_params("arbitrary"),
    )(small, pbias, pscale)


def _gate_bwd(small, pbias, pscale, dscal):
    t = small.shape[0]
    tt = _pick(t, (256, 128))
    nt = t // tt

    def body(x_ref, pb_ref, ps_ref, d_ref, dx_ref, dpb_ref, dps_ref, carry_ref):
        @pl.when(pl.program_id(0) == 0)
        def _():
            carry_ref[...] = jnp.zeros_like(carry_ref)
            dpb_ref[...] = jnp.zeros_like(dpb_ref)
            dps_ref[...] = jnp.zeros_like(dps_ref)

        lane = lax.broadcasted_iota(jnp.int32, (tt, LANES), 1)
        z = x_ref[...] + pb_ref[...]
        d = d_ref[...]
        dc = jnp.where(lane < HEADS, d, 0.0)
        dbeta = jnp.where((lane >= HEADS) & (lane < 2 * HEADS), d, 0.0)
        dgc = jnp.where((lane >= 2 * HEADS) & (lane < 3 * HEADS), d, 0.0)
        r, c, same_chunk = _tri_masks(tt)
        upper = jnp.where(r <= c, 1.0, 0.0)
        upper_chunk = jnp.where((r <= c) & same_chunk, 1.0, 0.0)
        dlogf = jnp.dot(upper, dc, precision=lax.Precision.HIGHEST, preferred_element_type=F32) + carry_ref[...]
        dg = jnp.dot(upper_chunk, dgc, precision=lax.Precision.HIGHEST, preferred_element_type=F32)
        carry_ref[...] += jnp.sum(dc, axis=0, keepdims=True)
        sg = jax.nn.sigmoid(z)
        dz = dlogf * (1.0 - sg) + dbeta * sg * (1.0 - sg) + dg * ps_ref[...] * sg
        dx_ref[...] = dz.astype(dx_ref.dtype)
        dpb_ref[...] += jnp.sum(dz, axis=0, keepdims=True)
        dps_ref[...] += jnp.sum(dg * _softplus(z), axis=0, keepdims=True)

    row = pl.BlockSpec((tt, LANES), lambda i: (nt - 1 - i, 0))
    vec = pl.BlockSpec((1, LANES), lambda i: (0, 0))
    return pl.pallas_call(
        body, grid=(nt,), name="gate_bwd", in_specs=[row, vec, vec, row], out_specs=[row, vec, vec],
        out_shape=[jax.ShapeDtypeStruct((t, LANES), BF16), jax.ShapeDtypeStruct((1, LANES), F32),
                   jax.ShapeDtypeStruct((1, LANES), F32)],
        scratch_shapes=[pltpu.VMEM((1, LANES), F32)],
        compiler_params=_params("arbitrary"),
    )(small, pbias, pscale, dscal)


def _gdn_pre_fwd(xg, conv_w, seg):
    t = xg.shape[0]
    c3 = 3 * WIDTH
    tt = _pick(t, (320, 256, 128))

    def body(x_ref, p_ref, w_ref, e_ref, q_ref, k_ref, v_ref):
        x = x_ref[...].astype(F32)
        prev = jnp.where(pl.program_id(0) == 0, 0.0, _prev8(p_ref))
        s = _silu(_causal_conv(x, prev, w_ref, GDN_CONV))
        e = e_ref[...]
        q = s[:, :WIDTH]
        k = s[:, WIDTH:2 * WIDTH]
        _store_heads(q_ref, q * lax.rsqrt(_segsum(q * q, e) + RMS_EPS) * (HEAD_DIM ** -0.5))
        _store_heads(k_ref, k * lax.rsqrt(_segsum(k * k, e) + RMS_EPS))
        _store_heads(v_ref, s[:, 2 * WIDTH:])

    out = pl.BlockSpec((HEADS, tt, HEAD_DIM), lambda i: (0, i, 0))
    shp = jax.ShapeDtypeStruct((HEADS, t, HEAD_DIM), F32)
    return pl.pallas_call(
        body, grid=(t // tt,), name="gdn_pre_fwd",
        in_specs=[pl.BlockSpec((tt, c3), lambda i: (i, 0)), _prev_spec(tt, c3),
                  pl.BlockSpec((GDN_CONV, c3), lambda i: (0, 0)), pl.BlockSpec((WIDTH, WIDTH), lambda i: (0, 0))],
        out_specs=[out, out, out], out_shape=[shp, shp, shp],
        compiler_params=_params("arbitrary"),
    )(xg, xg, conv_w, seg)


def _gdn_pre_bwd(xg, conv_w, seg, dqn, dkn, dv):
    t = xg.shape[0]
    c3 = 3 * WIDTH
    tt = _pick(t, (320, 256, 128))
    nt = t // tt

    def body(x_ref, p_ref, w_ref, e_ref, dq_ref, dk_ref, dv_ref, dx_ref, dw_ref, carry_ref):
        step = pl.program_id(0)
        x = x_ref[...].astype(F32)
        e = e_ref[...]
        prev = jnp.where(step == nt - 1, 0.0, _prev8(p_ref))
        y = _causal_conv(x, prev, w_ref, GDN_CONV)
        s = _silu(y)
        q = s[:, :WIDTH]
        k = s[:, WIDTH:2 * WIDTH]
        rq = lax.rsqrt(_segsum(q * q, e) + RMS_EPS)
        rk = lax.rsqrt(_segsum(k * k, e) + RMS_EPS)
        gq = _load_heads(dq_ref) * (HEAD_DIM ** -0.5)
        gk = _load_heads(dk_ref)
        dq = rq * gq - q * (rq * rq * rq) * _segsum(gq * q, e)
        dk = rk * gk - k * (rk * rk * rk) * _segsum(gk * k, e)
        dy = jnp.concatenate([dq, dk, _load_heads(dv_ref)], axis=1) * _silu_grad(y)

        @pl.when(step == 0)
        def _():
            carry_ref[...] = jnp.zeros_like(carry_ref)
            dw_ref[...] = jnp.zeros_like(dw_ref)

        dx = _causal_conv_bwd(x, prev, dy, carry_ref[...], w_ref, dw_ref, GDN_CONV)
        dx_ref[...] = dx.astype(dx_ref.dtype)
        carry_ref[...] = dy[:8]

    rev = lambda i: (nt - 1 - i, 0)
    blk = pl.BlockSpec((HEADS, tt, HEAD_DIM), lambda i: (0, nt - 1 - i, 0))
    return pl.pallas_call(
        body, grid=(nt,), name="gdn_pre_bwd",
        in_specs=[pl.BlockSpec((tt, c3), rev), _prev_spec(tt, c3, lambda i: nt - 1 - i),
                  pl.BlockSpec((GDN_CONV, c3), lambda i: (0, 0)), pl.BlockSpec((WIDTH, WIDTH), lambda i: (0, 0)),
                  blk, blk, blk],
        out_specs=[pl.BlockSpec((tt, c3), rev), pl.BlockSpec((GDN_CONV, c3), lambda i: (0, 0))],
        out_shape=[jax.ShapeDtypeStruct((t, c3), BF16), jax.ShapeDtypeStruct((GDN_CONV, c3), F32)],
        scratch_shapes=[pltpu.VMEM((8, c3), F32)],
        compiler_params=_params("arbitrary"),
    )(xg, xg, conv_w, seg, dqn, dkn, dv)


def _bmm(a, b, ca, cb, precision=None):
    return lax.dot_general(a, b, (((ca,), (cb,)), ((0,), (0,))), precision=precision, preferred_element_type=F32)


def _bf(x):
    return x.astype(BF16)


def _tri_inverse(a, eye):
    x = -a
    tinv = eye + x
    pw = x
    for _ in range(5):
        pb = _bf(pw)
        pw = _bmm(pb, pb, 2, 1)
        tinv = tinv + _bmm(_bf(tinv), _bf(pw), 2, 1)
    resid = eye - _bmm(eye + a, tinv, 2, 1, precision=HI)
    return tinv + _bmm(_bf(tinv), _bf(resid), 2, 1)


def _gdn_intra(q, k, v, bc, gcc, gcr):
    ii = lax.broadcasted_iota(jnp.int32, (CHUNK, CHUNK), 0)
    jj = lax.broadcasted_iota(jnp.int32, (CHUNK, CHUNK), 1)
    tril = (ii >= jj)[None]
    strict = (ii > jj)[None]
    eye = jnp.where(ii == jj, 1.0, 0.0).astype(F32)[None]
    last = (ii == CHUNK - 1)[None]
    dm = jnp.exp(jnp.where(tril, gcc - gcr, NEG))
    gam = jnp.exp(gcc)
    kb = k * bc
    vb = v * bc
    kk = _bmm(_bf(kb), _bf(k), 2, 2)
    a = jnp.where(strict, kk * dm, 0.0)
    tinv = _tri_inverse(a, eye)
    uw = _bmm(tinv, jnp.concatenate([vb, kb * gam], axis=2), 2, 1, precision=HI)
    u, wk = uw[:, :, :HEAD_DIM], uw[:, :, HEAD_DIM:]
    qk = _bmm(_bf(q), _bf(k), 2, 2)
    p = jnp.where(tril, qk * dm, 0.0)
    gl = jnp.sum(jnp.where(last, gcc, 0.0), axis=1, keepdims=True)
    edec = jnp.exp(gl - gcc)
    return dict(tril=tril, strict=strict, dm=dm, gam=gam, kb=kb, kk=kk, a=a, tinv=tinv, u=u, wk=wk, qk=qk, p=p,
                qg=q * gam, kt=k * edec, edec=edec, gaml=jnp.exp(gl), last=last)


def _gate_tiles(sc, gct, nb):
    rows = nb * CHUNK
    cols = lambda lane0: jnp.stack([jnp.broadcast_to(sc[:, lane0 + h:lane0 + h + 1], (rows, HEAD_DIM))
                                    for h in range(HEADS)], axis=0).reshape(HEADS * nb, CHUNK, HEAD_DIM)
    gcr = jnp.stack([jnp.broadcast_to(gct[h:h + 1, n * CHUNK:(n + 1) * CHUNK], (CHUNK, CHUNK))
                     for h in range(HEADS) for n in range(nb)], axis=0)
    return cols(HEADS), cols(2 * HEADS), gcr


def _gdn_fwd(q, k, v, scal, gct, nb=None):
    h, t, dh = q.shape
    nc = t // CHUNK
    nb = nb or _pick(nc, (4, 2))
    bsz = h * nb

    def body(q_ref, k_ref, v_ref, sc_ref, gt_ref, o_ref, s0_ref, state_ref):
        @pl.when(pl.program_id(0) == 0)
        def _():
            state_ref[...] = jnp.zeros_like(state_ref)

        ld = lambda r: r[...].reshape(bsz, CHUNK, dh)
        bc, gcc, gcr = _gate_tiles(sc_ref[...], gt_ref[...], nb)
        z = _gdn_intra(ld(q_ref), ld(k_ref), ld(v_ref), bc, gcc, gcr)
        per = lambda x: x.reshape((h, nb) + x.shape[1:])
        u, wk, p, qg, kt, gaml = (per(z[n]) for n in ("u", "wk", "p", "qg", "kt", "gaml"))
        s = state_ref[...]
        for n in range(nb):
            s0_ref[:, n] = s
            sb = _bf(s)
            vn = u[:, n] - _bmm(_bf(wk[:, n]), sb, 2, 1)
            vnb = _bf(vn)
            o_ref[:, n * CHUNK:(n + 1) * CHUNK, :] = _bmm(_bf(jnp.concatenate([qg[:, n], p[:, n]], axis=2)),
                                                          jnp.concatenate([sb, vnb], axis=1), 2, 1)
            s = s * gaml[:, n] + _bmm(_bf(kt[:, n]), vnb, 1, 1)
        state_ref[...] = s

    blk = pl.BlockSpec((h, nb * CHUNK, dh), lambda i: (0, i, 0))
    return pl.pallas_call(
        body, grid=(nc // nb,), name="gdn_fwd",
        in_specs=[blk] * 3 + [pl.BlockSpec((nb * CHUNK, LANES), lambda i: (i, 0)),
                              pl.BlockSpec((h, nb * CHUNK), lambda i: (0, i))],
        out_specs=[blk, pl.BlockSpec((h, nb, dh, dh), lambda i: (0, i, 0, 0))],
        out_shape=[jax.ShapeDtypeStruct((h, t, dh), F32), jax.ShapeDtypeStruct((h, nc, dh, dh), F32)],
        scratch_shapes=[pltpu.VMEM((h, dh, dh), F32)],
        compiler_params=_params("arbitrary"),
    )(q, k, v, scal, gct)


def _gdn_bwd(q, k, v, scal, gct, s0s, do, nb=None):
    h, t, dh = q.shape
    nc = t // CHUNK
    nb = nb or _pick(nc, (4, 2))
    bsz = h * nb
    ng = nc // nb
    rows = nb * CHUNK

    def body(q_ref, k_ref, v_ref, sc_ref, gt_ref, s0_ref, do_ref,
             dq_ref, dk_ref, dv_ref, dsc_ref, dgt_ref, ds_ref):
        @pl.when(pl.program_id(0) == 0)
        def _():
            ds_ref[...] = jnp.zeros_like(ds_ref)

        ld = lambda r: r[...].reshape(bsz, CHUNK, dh)
        q, k, v = ld(q_ref), ld(k_ref), ld(v_ref)
        bc, gcc, gcr = _gate_tiles(sc_ref[...], gt_ref[...], nb)
        z = _gdn_intra(q, k, v, bc, gcc, gcr)
        per = lambda x: x.reshape((h, nb) + x.shape[1:])
        u, wk, p, qg, kt, gaml = (per(z[n]) for n in ("u", "wk", "p", "qg", "kt", "gaml"))
        dout = per(ld(do_ref))
        ds = ds_ref[...]
        d_u, d_wk, d_p, d_qg, d_kt, d_gaml = ([None] * nb for _ in range(6))
        for n in reversed(range(nb)):
            s0 = s0_ref[:, n]
            s0b, dsb, dob = _bf(s0), _bf(ds), _bf(dout[:, n])
            wkb, qgb = _bf(wk[:, n]), _bf(qg[:, n])
            vn = u[:, n] - _bmm(wkb, s0b, 2, 1)
            dvn = _bmm(_bf(p[:, n]), dob, 1, 1) + _bmm(_bf(kt[:, n]), dsb, 2, 1)
            dvnb = _bf(dvn)
            d_u[n] = dvn
            vnb = _bf(vn)
            dpq = _bmm(dob, jnp.concatenate([vnb, s0b], axis=1), 2, 2)
            d_p[n], d_qg[n] = dpq[:, :, :CHUNK], dpq[:, :, CHUNK:]
            d_kt[n] = _bmm(vnb, dsb, 2, 2)
            d_gaml[n] = jnp.sum(s0 * ds, axis=1, keepdims=True)
            d_wk[n] = -_bmm(dvnb, s0b, 2, 2)
            ds = gaml[:, n] * ds + _bmm(jnp.concatenate([qgb, -wkb], axis=1), jnp.concatenate([dob, dvnb], axis=1), 1, 1)
        ds_ref[...] = ds

        flat = lambda xs: jnp.stack(xs, axis=1).reshape((bsz,) + xs[0].shape[1:])
        d_u, d_wk, d_p, d_qg, d_kt, d_gaml = (flat(x) for x in (d_u, d_wk, d_p, d_qg, d_kt, d_gaml))
        tinv, gam, kb, dm = z["tinv"], z["gam"], z["kb"], z["dm"]
        dr = _bmm(tinv, jnp.concatenate([d_u, d_wk], axis=2), 1, 1, precision=HI)
        drv, drk = dr[:, :, :HEAD_DIM], dr[:, :, HEAD_DIM:]
        da = -_bmm(_bf(dr), _bf(jnp.concatenate([z["u"], z["wk"]], axis=2)), 2, 2)
        da = jnp.where(z["strict"], da, 0.0)
        d_p = jnp.where(z["tril"], d_p, 0.0)
        dkk = _bf(da * dm)
        dqk = _bf(d_p * dm)
        dkb = _bmm(dkk, _bf(k), 2, 1) + drk * gam
        dk = (_bmm(jnp.concatenate([dkk, dqk], axis=1), _bf(jnp.concatenate([kb, q], axis=1)), 1, 1)
              + dkb * bc + d_kt * z["edec"])
        dq = _bmm(dqk, _bf(k), 2, 1) + d_qg * gam
        mm = da * z["a"] + d_p * z["p"]
        dkt_kt = d_kt * z["kt"]
        dgl = jnp.sum(dkt_kt, axis=1, keepdims=True) + d_gaml * z["gaml"]
        dgc = mm + d_qg * z["qg"] + drk * kb * gam - dkt_kt + jnp.where(z["last"], dgl, 0.0)
        dq_ref[...] = dq.reshape(h, rows, dh)
        dk_ref[...] = dk.reshape(h, rows, dh)
        dv_ref[...] = (drv * bc).reshape(h, rows, dh)
        dbeta = (dkb * k + drv * v).reshape(h, rows, dh)
        dgc = dgc.reshape(h, rows, dh)
        lane = lax.broadcasted_iota(jnp.int32, (rows, LANES), 1)
        dsc = jnp.zeros((rows, LANES), F32)
        for hh in range(h):
            dsc = jnp.where(lane == HEADS + hh, jnp.sum(dbeta[hh], axis=1, keepdims=True), dsc)
            dsc = jnp.where(lane == 2 * HEADS + hh, jnp.sum(dgc[hh], axis=1, keepdims=True), dsc)
        dsc_ref[...] = dsc
        dgr = -jnp.sum(mm, axis=1, keepdims=True)
        for hh in range(h):
            for n in range(nb):
                dgt_ref[hh:hh + 1, n * CHUNK:(n + 1) * CHUNK] = dgr[hh * nb + n]

    blk = pl.BlockSpec((h, rows, dh), lambda i: (0, ng - 1 - i, 0))
    shp = jax.ShapeDtypeStruct((h, t, dh), F32)
    sc_spec = pl.BlockSpec((rows, LANES), lambda i: (ng - 1 - i, 0))
    gt_spec = pl.BlockSpec((h, rows), lambda i: (0, ng - 1 - i))
    return pl.pallas_call(
        body, grid=(ng,), name="gdn_bwd",
        in_specs=[blk] * 3 + [sc_spec, gt_spec, pl.BlockSpec((h, nb, dh, dh), lambda i: (0, ng - 1 - i, 0, 0)), blk],
        out_specs=[blk] * 3 + [sc_spec, gt_spec],
        out_shape=[shp] * 3 + [jax.ShapeDtypeStruct((t, LANES), F32), jax.ShapeDtypeStruct((h, t), F32)],
        scratch_shapes=[pltpu.VMEM((h, dh, dh), F32)],
        compiler_params=_params("arbitrary"),
    )(q, k, v, scal, gct, s0s, do)


def _gdn_post_fwd(o, xg, gain, seg):
    t = o.shape[1]
    tt = _pick(t, (320, 256, 128))

    def body(o_ref, z_ref, g_ref, e_ref, y_ref):
        x = _load_heads(o_ref)
        r = lax.rsqrt(_segsum(x * x, e_ref[...]) * (1.0 / HEAD_DIM) + RMS_EPS)
        y_ref[...] = (x * r * g_ref[...] * _silu(z_ref[...].astype(F32))).astype(y_ref.dtype)

    return pl.pallas_call(
        body, grid=(t // tt,), name="gdn_post_fwd",
        in_specs=[pl.BlockSpec((HEADS, tt, HEAD_DIM), lambda i: (0, i, 0)), pl.BlockSpec((tt, WIDTH), lambda i: (i, 3)),
                  pl.BlockSpec((1, WIDTH), lambda i: (0, 0)), pl.BlockSpec((WIDTH, WIDTH), lambda i: (0, 0))],
        out_specs=pl.BlockSpec((tt, WIDTH), lambda i: (i, 0)),
        out_shape=jax.ShapeDtypeStruct((t, WIDTH), BF16),
        compiler_params=_params("arbitrary"),
    )(o, xg, gain, seg)


def _gdn_post_bwd(o, xg, gain, seg, dy):
    t = o.shape[1]
    tt = _pick(t, (320, 256, 128))

    def body(o_ref, z_ref, g_ref, e_ref, dy_ref, do_ref, dz_ref, dg_ref):
        x = _load_heads(o_ref)
        zz = z_ref[...].astype(F32)
        e = e_ref[...]
        gain_v = g_ref[...]
        d = dy_ref[...]
        r = lax.rsqrt(_segsum(x * x, e) * (1.0 / HEAD_DIM) + RMS_EPS)
        xr = x * r
        don = d * _silu(zz)
        dz_ref[...] = (d * xr * gain_v * _silu_grad(zz)).astype(dz_ref.dtype)
        gy = don * gain_v
        _store_heads(do_ref, r * gy - xr * (r * r) * (_segsum(gy * x, e) * (1.0 / HEAD_DIM)))

        @pl.when(pl.program_id(0) == 0)
        def _():
            dg_ref[...] = jnp.zeros_like(dg_ref)

        dg_ref[...] += jnp.sum(don * xr, axis=0, keepdims=True)

    row = pl.BlockSpec((tt, WIDTH), lambda i: (i, 0))
    vec = pl.BlockSpec((1, WIDTH), lambda i: (0, 0))
    hm = pl.BlockSpec((HEADS, tt, HEAD_DIM), lambda i: (0, i, 0))
    return pl.pallas_call(
        body, grid=(t // tt,), name="gdn_post_bwd",
        in_specs=[hm, pl.BlockSpec((tt, WIDTH), lambda i: (i, 3)), vec,
                  pl.BlockSpec((WIDTH, WIDTH), lambda i: (0, 0)), row],
        out_specs=[hm, row, vec],
        out_shape=[jax.ShapeDtypeStruct((HEADS, t, HEAD_DIM), F32), jax.ShapeDtypeStruct((t, WIDTH), BF16),
                   jax.ShapeDtypeStruct((1, WIDTH), F32)],
        compiler_params=_params("arbitrary"),
    )(o, xg, gain, seg, dy)


def _mix_fwd(yf, yg, gates, bias):
    t, d = yf.shape
    tt = _pick(t, (320, 256, 128))

    def body(yf_ref, yg_ref, g1_ref, g2_ref, b1_ref, b2_ref, o_ref):
        g1 = jax.nn.sigmoid(g1_ref[...].astype(F32) + b1_ref[...])
        g2 = jax.nn.sigmoid(g2_ref[...].astype(F32) + b2_ref[...])
        o_ref[...] = (g1 * yf_ref[...].astype(F32) + g2 * yg_ref[...].astype(F32)).astype(o_ref.dtype)

    row = pl.BlockSpec((tt, d), lambda i: (i, 0))
    return pl.pallas_call(
        body, grid=(t // tt,), name="mix_fwd",
        in_specs=[row, row, row, pl.BlockSpec((tt, d), lambda i: (i, 1)),
                  pl.BlockSpec((1, d), lambda i: (0, 0)), pl.BlockSpec((1, d), lambda i: (0, 1))],
        out_specs=row, out_shape=jax.ShapeDtypeStruct((t, d), BF16),
        compiler_params=_params("arbitrary"),
    )(yf, yg, gates, gates, bias, bias)


def _mix_bwd(dmix, yf, yg, gates, bias):
    t, d = yf.shape
    tt = _pick(t, (320, 256, 128))

    def body(dm_ref, yf_ref, yg_ref, g1_ref, g2_ref, b1_ref, b2_ref, dyf_ref, dyg_ref, dg_ref, db_ref):
        dm = dm_ref[...].astype(F32)
        g1 = jax.nn.sigmoid(g1_ref[...].astype(F32) + b1_ref[...])
        g2 = jax.nn.sigmoid(g2_ref[...].astype(F32) + b2_ref[...])
        dyf_ref[...] = (dm * g1).astype(BF16)
        dyg_ref[...] = (dm * g2).astype(BF16)
        dgate = jnp.concatenate([dm * yf_ref[...].astype(F32) * g1 * (1.0 - g1),
                                 dm * yg_ref[...].astype(F32) * g2 * (1.0 - g2)], axis=1)
        dg_ref[...] = dgate.astype(BF16)

        @pl.when(pl.program_id(0) == 0)
        def _():
            db_ref[...] = jnp.zeros_like(db_ref)

        db_ref[...] += jnp.sum(dgate, axis=0, keepdims=True)

    row = pl.BlockSpec((tt, d), lambda i: (i, 0))
    wide = pl.BlockSpec((tt, 2 * d), lambda i: (i, 0))
    return pl.pallas_call(
        body, grid=(t // tt,), name="mix_bwd",
        in_specs=[row, row, row, row, pl.BlockSpec((tt, d), lambda i: (i, 1)),
                  pl.BlockSpec((1, d), lambda i: (0, 0)), pl.BlockSpec((1, d), lambda i: (0, 1))],
        out_specs=[row, row, wide, pl.BlockSpec((1, 2 * d), lambda i: (0, 0))],
        out_shape=[jax.ShapeDtypeStruct((t, d), BF16), jax.ShapeDtypeStruct((t, d), BF16),
                   jax.ShapeDtypeStruct((t, 2 * d), BF16), jax.ShapeDtypeStruct((1, 2 * d), F32)],
        compiler_params=_params("arbitrary"),
    )(dmix, yf, yg, gates, gates, bias, bias)


def _ffn_act_fwd(up, conv_w, conv_b):
    t, c = up.shape
    tt = 128

    def body(x_ref, p_ref, w_ref, b_ref, o_ref):
        first = pl.program_id(0) == 0

        def conv(cols):
            prev = jnp.where(first, 0.0, _prev8(p_ref, cols))
            return _causal_conv(x_ref[:, cols].astype(F32), prev, w_ref, FFN_CONV, cols) + b_ref[:, cols]

        for lo in range(0, D_FF, FFN_LANES):
            gate = conv(slice(lo, lo + FFN_LANES))
            val = conv(slice(D_FF + lo, D_FF + lo + FFN_LANES))
            o_ref[:, lo:lo + FFN_LANES] = (_silu(gate) * val).astype(o_ref.dtype)

    return pl.pallas_call(
        body, grid=(t // tt,), name="ffn_act_fwd",
        in_specs=[pl.BlockSpec((tt, c), lambda i: (i, 0)), _prev_spec(tt, c),
                  pl.BlockSpec((FFN_CONV, c), lambda i: (0, 0)), pl.BlockSpec((1, c), lambda i: (0, 0))],
        out_specs=pl.BlockSpec((tt, D_FF), lambda i: (i, 0)),
        out_shape=jax.ShapeDtypeStruct((t, D_FF), BF16),
        compiler_params=_params("arbitrary"),
    )(up, up, conv_w, conv_b)


def _ffn_act_bwd(up, conv_w, conv_b, dact):
    t, c = up.shape
    tt = 128
    nt = t // tt

    def body(x_ref, p_ref, w_ref, b_ref, da_ref, dx_ref, dw_ref, db_ref, carry_ref):
        step = pl.program_id(0)

        @pl.when(step == 0)
        def _():
            carry_ref[...] = jnp.zeros_like(carry_ref)
            dw_ref[...] = jnp.zeros_like(dw_ref)
            db_ref[...] = jnp.zeros_like(db_ref)

        def conv(cols):
            x = x_ref[:, cols].astype(F32)
            prev = jnp.where(step == nt - 1, 0.0, _prev8(p_ref, cols))
            return x, prev, _causal_conv(x, prev, w_ref, FFN_CONV, cols) + b_ref[:, cols]

        def back(cols, x, prev, du):
            dx = _causal_conv_bwd(x, prev, du, carry_ref[:, cols], w_ref, dw_ref, FFN_CONV, cols)
            dx_ref[:, cols] = dx.astype(dx_ref.dtype)
            db_ref[:, cols] += jnp.sum(du, axis=0, keepdims=True)
            carry_ref[:, cols] = du[:8]

        for lo in range(0, D_FF, FFN_LANES):
            gcols, vcols = slice(lo, lo + FFN_LANES), slice(D_FF + lo, D_FF + lo + FFN_LANES)
            xg, pg, gate = conv(gcols)
            xv, pv, val = conv(vcols)
            da = da_ref[:, gcols]
            back(gcols, xg, pg, da * val * _silu_grad(gate))
            back(vcols, xv, pv, da * _silu(gate))

    rev = lambda i: (nt - 1 - i, 0)
    return pl.pallas_call(
        body, grid=(nt,), name="ffn_act_bwd",
        in_specs=[pl.BlockSpec((tt, c), rev),
                  _prev_spec(tt, c, lambda i: nt - 1 - i),
                  pl.BlockSpec((FFN_CONV, c), lambda i: (0, 0)), pl.BlockSpec((1, c), lambda i: (0, 0)),
                  pl.BlockSpec((tt, D_FF), rev)],
        out_specs=[pl.BlockSpec((tt, c), rev), pl.BlockSpec((FFN_CONV, c), lambda i: (0, 0)),
                   pl.BlockSpec((1, c), lambda i: (0, 0))],
        out_shape=[jax.ShapeDtypeStruct((t, c), BF16), jax.ShapeDtypeStruct((FFN_CONV, c), F32),
                   jax.ShapeDtypeStruct((1, c), F32)],
        scratch_shapes=[pltpu.VMEM((8, c), F32)],
        compiler_params=_params("arbitrary"),
    )(up, up, conv_w, conv_b, dact)


def _final_loss(h2, target, gain, seq):
    t, d = h2.shape
    tr = _pick(t, (320, 256, 128))

    def body(h_ref, t_ref, g_ref, loss_ref, dh_ref, dhb_ref, dg_ref):
        i = pl.program_id(0)
        x = h_ref[...]
        gain_v = g_ref[...]
        r = lax.rsqrt(jnp.mean(x * x, axis=-1, keepdims=True) + RMS_EPS)
        xr = x * r
        rows = i * tr + lax.broadcasted_iota(jnp.int32, (tr, 1), 0)
        real = (rows >= N_META) & (rows < N_META + seq)
        err = jnp.where(real, xr * gain_v - t_ref[...], 0.0)
        dy = err * (1.0 / d)
        gy = dy * gain_v
        dh = r * (gy - xr * jnp.mean(gy * xr, axis=-1, keepdims=True))
        dh_ref[...] = dh
        dhb_ref[...] = dh.astype(BF16)

        @pl.when(i == 0)
        def _():
            loss_ref[...] = jnp.zeros_like(loss_ref)
            dg_ref[...] = jnp.zeros_like(dg_ref)

        part = jnp.sum(jnp.sum(err * err, axis=-1, keepdims=True), axis=0, keepdims=True)
        loss_ref[...] += jnp.broadcast_to(part * (0.5 / d), loss_ref.shape)
        dg_ref[...] += jnp.sum(dy * xr, axis=0, keepdims=True)

    row = pl.BlockSpec((tr, d), lambda i: (i, 0))
    vec = pl.BlockSpec((1, d), lambda i: (0, 0))
    return pl.pallas_call(
        body, grid=(t // tr,), name="final_loss",
        in_specs=[row, row, vec],
        out_specs=[pl.BlockSpec((1, LANES), lambda i: (0, 0)), row, row, vec],
        out_shape=[jax.ShapeDtypeStruct((1, LANES), F32), jax.ShapeDtypeStruct((t, d), F32),
                   jax.ShapeDtypeStruct((t, d), BF16), jax.ShapeDtypeStruct((1, d), F32)],
        compiler_params=_params("arbitrary"),
    )(h2, target, gain)


ADAM_TILE_BYTES = 1 << 20


def _adamw(w, m, v, grecv, name):
    r, cols = w.shape
    tr = r
    if r * cols * 4 > ADAM_TILE_BYTES:
        tr = max(d for d in range(8, r + 1, 8) if r % d == 0 and d * cols * 4 <= ADAM_TILE_BYTES)

    def body(w_ref, m_ref, v_ref, g_ref, go_ref, d_ref, mo_ref, vo_ref):
        g = g_ref[0].astype(F32)
        for s in range(1, N_DEV):
            g = g + g_ref[s].astype(F32)
        wv = w_ref[...]
        mn = ADAM_B1 * m_ref[...] + (1.0 - ADAM_B1) * g
        vn = ADAM_B2 * v_ref[...] + (1.0 - ADAM_B2) * (g * g)
        m_hat = mn / (1.0 - ADAM_B1 ** ADAM_STEP)
        v_hat = vn / (1.0 - ADAM_B2 ** ADAM_STEP)
        go_ref[...] = g
        d_ref[...] = -ADAM_LR * (m_hat / (jnp.sqrt(v_hat) + ADAM_EPS) + ADAM_WD * wv)
        mo_ref[...] = mn
        vo_ref[...] = vn

    row = pl.BlockSpec((tr, cols), lambda i: (i, 0))
    shp = jax.ShapeDtypeStruct((r, cols), F32)
    return pl.pallas_call(
        body, grid=(r // tr,), name=name,
        in_specs=[row, row, row, pl.BlockSpec((N_DEV, tr, cols), lambda i: (0, i, 0))],
        out_specs=[row] * 4, out_shape=[shp] * 4,
        compiler_params=_params("parallel"),
    )(w, m, v, grecv)


def _mesh_pos():
    return lax.axis_index("x"), lax.axis_index("y"), lax.axis_index("c")


def _all_gather(shards):
    n = len(shards)

    def body(*refs):
        x_refs, out_refs = refs[:n], refs[n:2 * n]
        send_sems, recv_sems, local_sems = refs[2 * n:]
        x, y, c = _mesh_pos()
        me, sibling = (x, y, c), (x, y, 1 - c)
        chips = [(1 - x, y), (x, 1 - y), (1 - x, 1 - y)]

        def slot(a, px, py, pc):
            return out_refs[a].at[4 * px + 2 * py + pc]

        def copy(a, kk, block, to, src=None):
            return pltpu.make_async_remote_copy(
                src_ref=slot(a, *block) if src is None else src, dst_ref=slot(a, *block),
                send_sem=send_sems.at[7 * a + kk], recv_sem=recv_sems.at[7 * a + kk],
                device_id=to, device_id_type=MESH_ID)

        mine = [pltpu.make_async_copy(x_refs[a], slot(a, *me), local_sems.at[a]) for a in range(n)]
        first = []
        for a in range(n):
            first.append(copy(a, 0, me, sibling, src=x_refs[a]))
            first += [copy(a, 1 + j, me, (*chip, c), src=x_refs[a]) for j, chip in enumerate(chips)]
        for cp in mine + first:
            cp.start()
        passed = []
        for j, chip in enumerate(chips):
            for a in range(n):
                copy(a, 1 + j, (*chip, c), me).wait_recv()
                passed.append(copy(a, 4 + j, (*chip, c), sibling))
                passed[-1].start()
        for a in range(n):
            copy(a, 0, sibling, me).wait_recv()
        for j, chip in enumerate(chips):
            for a in range(n):
                copy(a, 4 + j, (*chip, 1 - c), me).wait_recv()
        for cp in first + passed:
            cp.wait_send()
        for cp in mine:
            cp.wait()

    hbm = pl.BlockSpec(memory_space=pl.ANY)
    return pl.pallas_call(
        body, name="weight_all_gather", in_specs=[hbm] * n, out_specs=[hbm] * n,
        out_shape=[jax.ShapeDtypeStruct((N_DEV,) + s.shape, s.dtype) for s in shards],
        scratch_shapes=[pltpu.SemaphoreType.DMA((7 * n,)), pltpu.SemaphoreType.DMA((7 * n,)),
                        pltpu.SemaphoreType.DMA((n,))],
    )(*shards)


def _grad_exchange(blocks, small):
    n = len(blocks)

    def body(*refs):
        src_refs, dst_refs = refs[:n + 1], refs[n + 1:2 * n + 2]
        send_sems, recv_sems, local_sems = refs[2 * n + 2:]
        x, y, c = _mesh_pos()
        me = 4 * x + 2 * y + c
        copies = []
        for kk in range(1, N_DEV):
            px = 1 - x if kk & 4 else x
            py = 1 - y if kk & 2 else y
            pc = 1 - c if kk & 1 else c
            peer = 4 * px + 2 * py + pc
            for a in range(n + 1):
                copies.append(pltpu.make_async_remote_copy(
                    src_ref=src_refs[a].at[peer] if a < n else src_refs[a], dst_ref=dst_refs[a].at[me],
                    send_sem=send_sems.at[7 * a + kk - 1], recv_sem=recv_sems.at[7 * a + kk - 1],
                    device_id=(px, py, pc), device_id_type=MESH_ID))
        own = [pltpu.make_async_copy(src_refs[a].at[me] if a < n else src_refs[a], dst_refs[a].at[me],
                                     local_sems.at[a]) for a in range(n + 1)]
        for cp in own + copies:
            cp.start()
        for cp in copies + own:
            cp.wait()

    hbm = pl.BlockSpec(memory_space=pl.ANY)
    return pl.pallas_call(
        body, name="grad_exchange", in_specs=[hbm] * (n + 1), out_specs=[hbm] * (n + 1),
        out_shape=[jax.ShapeDtypeStruct(b.shape, b.dtype) for b in blocks]
        + [jax.ShapeDtypeStruct((N_DEV,) + small.shape, small.dtype)],
        scratch_shapes=[pltpu.SemaphoreType.DMA((7 * (n + 1),)), pltpu.SemaphoreType.DMA((7 * (n + 1),)),
                        pltpu.SemaphoreType.DMA((n + 1,))],
    )(*blocks, small)


def _exchange_copies(src_refs, land_refs, send_sems, recv_sems):
    x, y, c = _mesh_pos()
    me = 4 * x + 2 * y + c
    copies = []
    for kk in range(1, N_DEV):
        px = 1 - x if kk & 4 else x
        py = 1 - y if kk & 2 else y
        pc = 1 - c if kk & 1 else c
        for a, (src, land) in enumerate(zip(src_refs, land_refs)):
            copies.append(pltpu.make_async_remote_copy(
                src_ref=src.at[4 * px + 2 * py + pc], dst_ref=land.at[me],
                send_sem=send_sems.at[7 * a + kk - 1], recv_sem=recv_sems.at[7 * a + kk - 1],
                device_id=(px, py, pc), device_id_type=MESH_ID))
    return copies


def _gather_copies(src_refs, land_refs, send_sems, recv_sems):
    x, y, c = _mesh_pos()
    me = 4 * x + 2 * y + c
    copies = []
    for kk in range(1, N_DEV):
        px = 1 - x if kk & 4 else x
        py = 1 - y if kk & 2 else y
        pc = 1 - c if kk & 1 else c
        for a, (src, land) in enumerate(zip(src_refs, land_refs)):
            copies.append(pltpu.make_async_remote_copy(
                src_ref=src, dst_ref=land.at[me],
                send_sem=send_sems.at[7 * a + kk - 1], recv_sem=recv_sems.at[7 * a + kk - 1],
                device_id=(px, py, pc), device_id_type=MESH_ID))
    return copies


_HBM = pl.BlockSpec(memory_space=pltpu.HBM)
_SEM = pl.BlockSpec(memory_space=pltpu.SEMAPHORE)
_DATAFLOW = pltpu.SideEffectType.DATAFLOW_SIDE_EFFECTING


def _split_start(name, make_copies, sources, land_shapes):
    n = len(sources)

    def body(*refs):
        src_refs, land_refs, send_sems, recv_sems = refs[:n], refs[n:2 * n], refs[2 * n], refs[2 * n + 1]
        for cp in make_copies(src_refs, land_refs, send_sems, recv_sems):
            cp.start()
        token = refs[-1]
        token[...] = jnp.zeros_like(token)

    in_hbm = lambda a: pltpu.with_memory_space_constraint(a, pltpu.HBM)
    hbm_shapes = [pltpu.HBM(s.shape, s.dtype) for s in list(sources) + list(land_shapes)]
    outs = pl.pallas_call(
        body, name=name, in_specs=[_HBM] * (2 * n),
        out_shape=(pltpu.SemaphoreType.DMA((7 * n,)), pltpu.SemaphoreType.DMA((7 * n,)), *hbm_shapes,
                   jax.ShapeDtypeStruct((8, LANES), F32)),
        out_specs=(_SEM, _SEM, *[_HBM] * (2 * n), pl.BlockSpec(memory_space=pltpu.VMEM)),
        input_output_aliases={a: 2 + a for a in range(2 * n)},
        compiler_params=pltpu.CompilerParams(has_side_effects=_DATAFLOW),
    )(*[in_hbm(s) for s in sources], *[in_hbm(lax.empty(s.shape, s.dtype)) for s in land_shapes])
    return outs[0], outs[1], outs[2:2 + n], outs[2 + n:2 + 2 * n], outs[-1]


def _split_wait(name, make_copies, send_sems, recv_sems, src_thru, land_thru, after):
    n = len(src_thru)

    def body(*refs):
        src_refs, land_refs, send_sems, recv_sems = refs[:n], refs[n:2 * n], refs[2 * n], refs[2 * n + 1]
        for cp in make_copies(src_refs, land_refs, send_sems, recv_sems):
            cp.wait_send()
            cp.wait_recv()

    outs = pl.pallas_call(
        body, name=name,
        in_specs=[_HBM] * (2 * n) + [_SEM, _SEM, pl.BlockSpec(memory_space=pl.ANY)],
        out_shape=tuple(pltpu.HBM(b.shape, b.dtype) for b in list(src_thru) + list(land_thru)),
        out_specs=[_HBM] * (2 * n), input_output_aliases={a: a for a in range(2 * n)},
        compiler_params=pltpu.CompilerParams(has_side_effects=_DATAFLOW),
    )(*src_thru, *land_thru, send_sems, recv_sems, after)
    return outs[:n], outs[n:]


def _exchange_start(blocks):
    return _split_start("grad_exchange_start", _exchange_copies, blocks, blocks)


def _exchange_wait(send_sems, recv_sems, src_thru, land_thru, after):
    return _split_wait("grad_exchange_wait", _exchange_copies, send_sems, recv_sems, src_thru, land_thru, after)


def _gather_start(shards):
    lands = [jax.ShapeDtypeStruct((N_DEV,) + s.shape, s.dtype) for s in shards]
    return _split_start("weight_gather_start", _gather_copies, shards, lands)


def _gather_wait(send_sems, recv_sems, src_thru, land_thru, after):
    return _split_wait("weight_gather_wait", _gather_copies, send_sems, recv_sems, src_thru, land_thru, after)


def _pad_flat(parts, rows):
    flat = jnp.concatenate([p.reshape(-1) for p in parts])
    return jnp.pad(flat, (0, rows * LANES - flat.shape[0])).reshape(rows, LANES)


def _rows_for(n_elems, mult=1024):
    rows = -(-n_elems // LANES)
    return -(-rows // mult) * mult


SHARDED = ("meta_tokens", "w_in", "gdn_conv_w", "w_branch_fox", "w_branch_gdn", "w_out", "ffn_w_up", "ffn_conv_w",
           "ffn_w_down")
MATMUL = ("w_in", "w_branch_fox", "w_branch_gdn", "w_out", "ffn_w_up", "ffn_w_down")
EXACT = ("meta_tokens", "gdn_conv_w", "ffn_conv_w")
REPLICATED = ("fgt_bias", "gdn_a_log", "gdn_dt_bias", "gdn_norm_w", "gate_bias", "norm_mix_w", "norm_ffn_w",
              "ffn_conv_b", "norm_final_w")
WEIGHTS = ("meta_tokens", "w_in", "fgt_bias", "gdn_conv_w", "gdn_a_log", "gdn_dt_bias", "gdn_norm_w", "gate_bias",
           "w_branch_fox", "w_branch_gdn", "w_out", "norm_mix_w", "norm_ffn_w", "ffn_w_up", "ffn_conv_w",
           "ffn_conv_b", "ffn_w_down", "norm_final_w")


def _unpack(buf, shapes):
    flat = buf.reshape(-1)
    out, off = [], 0
    for s in shapes:
        n = int(np.prod(s))
        out.append(flat[off:off + n].reshape(s))
        off += n
    return out


def _unpack_gathered(buf, shapes):
    flat = buf.reshape(N_DEV, -1)
    out, off = [], 0
    for s in shapes:
        n = int(np.prod(s))
        out.append(flat[:, off:off + n].reshape((N_DEV,) + tuple(s)))
        off += n
    return out


def _cat_cols(g):
    return g.transpose(1, 0, 2).reshape(g.shape[1], -1)


def _col_blocks(full, width):
    return full.reshape(full.shape[0], N_DEV, width).transpose(1, 0, 2)


def _local_step(x, target, w, early=None, late_weights=None):
    seq = x.shape[0]
    t = _padded_tokens(seq)
    pad = t - N_META - seq
    seg = _seg_matrix()
    zrows = jnp.zeros((pad, D_MODEL), F32)
    h0 = jnp.concatenate([w["meta_tokens"], x, zrows], axis=0)
    tgt = jnp.concatenate([jnp.zeros((N_META, D_MODEL), F32), target, zrows], axis=0)

    w_in = w["w_in"]
    o_f, o_g, o_z, o_b, o_a, o_gate = 1536, 1544, 3080, 3592, 3600, 3608
    w_small = jnp.concatenate([w_in[:, o_f:o_f + 8], w_in[:, o_b:o_b + 8], w_in[:, o_a:o_a + 8],
                               jnp.zeros((D_MODEL, LANES - 24), BF16)], axis=1)
    w_r = jnp.concatenate([w_in[:, :1536], w_in[:, o_g:o_z], w_in[:, o_z:o_b], w_in[:, o_gate:], w_small], axis=1)

    a1 = _rmsnorm_fwd(h0, w["norm_mix_w"])
    fq = _mm(a1, w_r[:, :1536], BF16, "proj_fox")
    xg = _mm(a1, w_r[:, 1536:3584], BF16, "proj_gdn")
    gt = _mm(a1, w_r[:, 3584:5632], BF16, "proj_gates")
    sm = _mm(a1, w_r[:, 5632:], F32, "proj_small")

    lanes_pad = lambda a, lo: jnp.pad(a, ((0, 0), (lo, LANES - lo - a.shape[1])))
    neg_exp_a = -jnp.exp(w["gdn_a_log"])
    pbias = lanes_pad(w["fgt_bias"], 0) + lanes_pad(w["gdn_dt_bias"], 2 * HEADS)
    if late_weights is not None:
        pbias = pbias + late_weights[0][0, 0]
    pscale = lanes_pad(neg_exp_a, 2 * HEADS)
    scal = _gate_fwd(sm, pbias, pscale)
    gct = scal[:, 2 * HEADS:3 * HEADS].T

    qa, ka, va, kat, vat = _fox_prep(fq, scal)
    oa, qb, qbt = _fox_fwd(qa, ka, vat)
    o_fox = _fox_post(oa)

    qh, kh, vh = _gdn_pre_fwd(xg, w["gdn_conv_w"], seg)
    og, s0s = _gdn_fwd(qh, kh, vh, scal, gct)
    norm_w = jnp.tile(w["gdn_norm_w"], (1, HEADS))
    ogn = _gdn_post_fwd(og, xg, norm_w, seg)

    if late_weights is not None:
        w = {**w, **late_weights[1](ogn)}
    yf = _mm(o_fox, w["w_branch_fox"], BF16, "branch_fox")
    yg = _mm(ogn, w["w_branch_gdn"], BF16, "branch_gdn")
    mix = _mix_fwd(yf, yg, gt, w["gate_bias"])
    h1 = _mm(mix, w["w_out"], F32, "out_proj", res=h0)
    a2 = _rmsnorm_fwd(h1, w["norm_ffn_w"])
    up = _mm(a2, w["ffn_w_up"], BF16, "ffn_up")
    act = _ffn_act_fwd(up, w["ffn_conv_w"], w["ffn_conv_b"])
    h2 = _mm(act, w["ffn_w_down"], F32, "ffn_down", res=h1)
    loss, dh2, dh2b, g_final = _final_loss(h2, tgt, w["norm_final_w"].reshape(1, D_MODEL), seq)

    grads = {"norm_final_w": g_final.reshape(D_MODEL)}
    grads["ffn_w_down"] = _mm_tn(act, dh2b, "wgrad_ffn_down")
    dact = _mm(dh2b, w["ffn_w_down"].T, F32, "dgrad_ffn_down")
    dup, g_cw, g_cb = _ffn_act_bwd(up, w["ffn_conv_w"], w["ffn_conv_b"], dact)
    grads["ffn_conv_w"], grads["ffn_conv_b"] = g_cw, g_cb
    grads["ffn_w_up"] = _mm_tn(a2, dup, "wgrad_ffn_up")
    da2 = _mm(dup, w["ffn_w_up"].T, BF16, "dgrad_ffn_up")
    dh1, dh1b, grads["norm_ffn_w"] = _rmsnorm_bwd(h1, da2, w["norm_ffn_w"], dh2)
    grads["w_out"] = _mm_tn(mix, dh1b, "wgrad_out")
    dmix = _mm(dh1b, w["w_out"].T, BF16, "dgrad_out")
    dyf, dyg, dgt, grads["gate_bias"] = _mix_bwd(dmix, yf, yg, gt, w["gate_bias"])
    grads["w_branch_fox"] = _mm_tn(o_fox, dyf, "wgrad_branch_fox")
    grads["w_branch_gdn"] = _mm_tn(ogn, dyg, "wgrad_branch_gdn")
    do_fox = _mm(dyf, w["w_branch_fox"].T, F32, "dgrad_branch_fox")
    dogn = _mm(dyg, w["w_branch_gdn"].T, F32, "dgrad_branch_gdn")

    dog, dz, g_nw = _gdn_post_bwd(og, xg, norm_w, seg, dogn)
    grads["gdn_norm_w"] = g_nw.reshape(HEADS, HEAD_DIM).sum(axis=0)[None]
    dqh, dkh, dvh, dscal_g, dgct = _gdn_bwd(qh, kh, vh, scal, gct, s0s, dog)
    dxg, grads["gdn_conv_w"] = _gdn_pre_bwd(xg, w["gdn_conv_w"], seg, dqh, dkh, dvh)

    doa, doat = _fox_bwd_prep(do_fox, oa)
    dfq, dscal_c = _fox_bwd_post(*_fox_bwd(qb, qbt, ka, kat, va, doa, doat))

    dscal = dscal_c + dscal_g + lanes_pad(dgct.T, 2 * HEADS)
    dsm, dpb, dps = _gate_bwd(sm, pbias, pscale, dscal)
    grads["fgt_bias"] = dpb[:, :HEADS]
    grads["gdn_dt_bias"] = dpb[:, 2 * HEADS:3 * HEADS]
    grads["gdn_a_log"] = dps[:, 2 * HEADS:3 * HEADS] * neg_exp_a

    dproj = jnp.concatenate([dfq, dxg, dz, dgt, dsm], axis=1)
    g_r = _mm_tn(a1, dproj, "wgrad_in")
    grads["w_in"] = jnp.concatenate([g_r[:, :1536], g_r[:, 5632:5640], g_r[:, 1536:3072], g_r[:, 3072:3584],
                                     g_r[:, 5640:5648], g_r[:, 5648:5656], g_r[:, 3584:5632]], axis=1)
    token, handle = early(grads) if early is not None else (jnp.zeros((8, LANES), F32), None)
    w_rt = w_r.T + token[0, 0].astype(BF16)
    da1 = _mm(dproj, w_rt, BF16, "dgrad_in")
    dh0, _, grads["norm_mix_w"] = _rmsnorm_bwd(h0, da1, w["norm_mix_w"], dh1)
    grads["meta_tokens"] = dh0[:N_META]
    return loss, dh0[N_META:N_META + seq], grads, handle


def _shard_pieces(arrs):
    return [arrs[n][0] if arrs[n].ndim == 3 else arrs[n] for n in SHARDED]


def _full_grad_blocks(grads):
    g = grads
    cols = lambda a, wd: _col_blocks(a, wd)
    rows = lambda a: a.reshape(N_DEV, a.shape[0] // N_DEV, a.shape[1])
    return [cols(g["w_in"], IN_WIDTH // N_DEV), cols(g["gdn_conv_w"], 3 * WIDTH // N_DEV),
            cols(g["w_branch_fox"], D_MODEL // N_DEV), cols(g["w_branch_gdn"], D_MODEL // N_DEV), rows(g["w_out"]),
            cols(g["ffn_w_up"], 2 * D_FF // N_DEV), cols(g["ffn_conv_w"], 2 * D_FF // N_DEV), rows(g["ffn_w_down"])]


def kernel(x, meta_tokens, w_in, fgt_bias, gdn_conv_w, gdn_a_log, gdn_dt_bias, gdn_norm_w, gate_bias, w_branch_fox, w_branch_gdn, w_out, norm_mix_w, norm_ffn_w, ffn_w_up, ffn_conv_w, ffn_conv_b, ffn_w_down, norm_final_w, loss_target, m_meta_tokens, m_w_in, m_fgt_bias, m_gdn_conv_w, m_gdn_a_log, m_gdn_dt_bias, m_gdn_norm_w, m_gate_bias, m_w_branch_fox, m_w_branch_gdn, m_w_out, m_norm_mix_w, m_norm_ffn_w, m_ffn_w_up, m_ffn_conv_w, m_ffn_conv_b, m_ffn_w_down, m_norm_final_w, v_meta_tokens, v_w_in, v_fgt_bias, v_gdn_conv_w, v_gdn_a_log, v_gdn_dt_bias, v_gdn_norm_w, v_gate_bias, v_w_branch_fox, v_w_branch_gdn, v_w_out, v_norm_mix_w, v_norm_ffn_w, v_ffn_w_up, v_ffn_conv_w, v_ffn_conv_b, v_ffn_w_down, v_norm_final_w):
    wts = dict(meta_tokens=meta_tokens, w_in=w_in, fgt_bias=fgt_bias, gdn_conv_w=gdn_conv_w, gdn_a_log=gdn_a_log,
               gdn_dt_bias=gdn_dt_bias, gdn_norm_w=gdn_norm_w, gate_bias=gate_bias, w_branch_fox=w_branch_fox,
               w_branch_gdn=w_branch_gdn, w_out=w_out, norm_mix_w=norm_mix_w, norm_ffn_w=norm_ffn_w,
               ffn_w_up=ffn_w_up, ffn_conv_w=ffn_conv_w, ffn_conv_b=ffn_conv_b, ffn_w_down=ffn_w_down,
               norm_final_w=norm_final_w)
    mom = dict(meta_tokens=m_meta_tokens, w_in=m_w_in, fgt_bias=m_fgt_bias, gdn_conv_w=m_gdn_conv_w,
               gdn_a_log=m_gdn_a_log, gdn_dt_bias=m_gdn_dt_bias, gdn_norm_w=m_gdn_norm_w, gate_bias=m_gate_bias,
               w_branch_fox=m_w_branch_fox, w_branch_gdn=m_w_branch_gdn, w_out=m_w_out, norm_mix_w=m_norm_mix_w,
               norm_ffn_w=m_norm_ffn_w, ffn_w_up=m_ffn_w_up, ffn_conv_w=m_ffn_conv_w, ffn_conv_b=m_ffn_conv_b,
               ffn_w_down=m_ffn_w_down, norm_final_w=m_norm_final_w)
    var = dict(meta_tokens=v_meta_tokens, w_in=v_w_in, fgt_bias=v_fgt_bias, gdn_conv_w=v_gdn_conv_w,
               gdn_a_log=v_gdn_a_log, gdn_dt_bias=v_gdn_dt_bias, gdn_norm_w=v_gdn_norm_w, gate_bias=v_gate_bias,
               w_branch_fox=v_w_branch_fox, w_branch_gdn=v_w_branch_gdn, w_out=v_w_out, norm_mix_w=v_norm_mix_w,
               norm_ffn_w=v_norm_ffn_w, ffn_w_up=v_ffn_w_up, ffn_conv_w=v_ffn_conv_w, ffn_conv_b=v_ffn_conv_b,
               ffn_w_down=v_ffn_w_down, norm_final_w=v_norm_final_w)

    sh = dict(zip(SHARDED, _shard_pieces(wts)))
    me = 4 * lax.axis_index("x") + 2 * lax.axis_index("y") + lax.axis_index("c")
    late_names = MATMUL[1:]
    late_sems_send, late_sems_recv, late_src, late_land, late_token = _gather_start(
        [sh[n].astype(BF16) for n in late_names])
    exact_shapes = [sh[n].shape for n in EXACT]
    rows_exact = _rows_for(sum(int(np.prod(s)) for s in exact_shapes), 8)
    g_in, g_exact = _all_gather([sh["w_in"].astype(BF16), _pad_flat([sh[n] for n in EXACT], rows_exact)])
    meta_full, conv_full, fconv_full = (_cat_cols(a) for a in _unpack_gathered(g_exact, exact_shapes))
    full = dict(
        meta_tokens=meta_full, w_in=_cat_cols(g_in), gdn_conv_w=conv_full, ffn_conv_w=fconv_full,
        fgt_bias=fgt_bias, gdn_a_log=gdn_a_log, gdn_dt_bias=gdn_dt_bias, gdn_norm_w=gdn_norm_w, gate_bias=gate_bias,
        norm_mix_w=norm_mix_w, norm_ffn_w=norm_ffn_w, ffn_conv_b=ffn_conv_b, norm_final_w=norm_final_w)

    def fetch_late_weights(after):
        shards, lands = _gather_wait(late_sems_send, late_sems_recv, late_src, late_land, after)
        g_bf, g_bg, g_out, g_up, g_down = (lax.dynamic_update_slice_in_dim(land, s[None], me, 0)
                                           for s, land in zip(shards, lands))
        return dict(w_branch_fox=_cat_cols(g_bf), w_branch_gdn=_cat_cols(g_bg), w_out=g_out.reshape(D_MODEL, D_MODEL),
                    ffn_w_up=_cat_cols(g_up), ffn_w_down=g_down.reshape(D_FF, D_MODEL))

    def start_exchange(grads_so_far):
        blocks = [b.astype(BF16) for b in _full_grad_blocks(grads_so_far)]
        send_sems, recv_sems, src_thru, land_thru, token = _exchange_start(blocks)
        return token, (send_sems, recv_sems, src_thru, land_thru)

    loss, grad_x, grads, handle = _local_step(x[0], loss_target[0], full, early=start_exchange,
                                              late_weights=(late_token, fetch_late_weights))
    sent, landed = _exchange_wait(*handle, after=grad_x)
    own = lambda src, land: lax.dynamic_update_slice_in_dim(land, lax.dynamic_slice_in_dim(src, me, 1, 0), me, 0)
    received = [own(src, land) for src, land in zip(sent, landed)]

    rep_parts = [grads[n] for n in REPLICATED] + [loss[:, :1]]
    rep_shapes = [wts[n].shape for n in REPLICATED]
    rows_small = _rows_for(sum(int(np.prod(p.shape)) for p in rep_parts), 8)
    meta_recv, small_recv = _grad_exchange([_col_blocks(grads["meta_tokens"], LANES).astype(BF16)],
                                           _pad_flat(rep_parts, rows_small))
    received = [meta_recv] + received + [small_recv]

    result = {}
    kinds = ("grad", "delta", "new_m", "new_v")
    for n, recv in zip(SHARDED, received[:-1]):
        outs = _adamw(sh[n], _shard_pieces(mom)[SHARDED.index(n)], _shard_pieces(var)[SHARDED.index(n)], recv,
                      "adamw_" + n)
        for kind, a in zip(kinds, outs):
            result[kind, n] = a.reshape(wts[n].shape)
    rep_w = _pad_flat([wts[n] for n in REPLICATED] + [jnp.zeros((1, 1), F32)], rows_small)
    rep_m = _pad_flat([mom[n] for n in REPLICATED] + [jnp.zeros((1, 1), F32)], rows_small)
    rep_v = _pad_flat([var[n] for n in REPLICATED] + [jnp.ones((1, 1), F32)], rows_small)
    outs_r = _adamw(rep_w, rep_m, rep_v, received[-1], "adamw_replicated")
    for kind, br in zip(kinds, outs_r):
        for n, a in zip(REPLICATED, _unpack(br, rep_shapes)):
            result[kind, n] = a
    n_rep = sum(int(np.prod(s)) for s in rep_shapes)
    total_loss = outs_r[0].reshape(-1)[n_rep]
    out = [total_loss, grad_x[None]]
    for kind in ("grad", "delta", "new_m", "new_v"):
        out += [result[kind, n] for n in WEIGHTS]
    return tuple(out)
```

```python
import jax
import jax.numpy as jnp
import numpy as np
from jax import lax
from jax.experimental import pallas as pl
from jax.experimental.pallas import tpu as pltpu

F32 = jnp.float32
BF16 = jnp.bfloat16

D_MODEL = 1024
N_META = 16
HEADS = 8
HEAD_DIM = 64
WIDTH = HEADS * HEAD_DIM
CHUNK = 64
GDN_CONV = 4
D_FF = 2816
FFN_CONV = 3
IN_WIDTH = 5656
RMS_EPS = 1e-6
NEG = -1e30
AUG = 128
N_DEV = 8
LANES = 128

ADAM_LR = 0.001
ADAM_B1 = 0.9
ADAM_B2 = 0.999
ADAM_EPS = 1e-08
ADAM_WD = 0.01
ADAM_STEP = 10

VMEM_LIMIT = 56 * 1024 * 1024
MM_VMEM_BUDGET = 36 * 1024 * 1024
FFN_LANES = 128
FFN_ROWS = 256
HI = lax.Precision.HIGH
MESH_ID = pl.DeviceIdType.MESH


def _pick(n, cands):
    for c in cands:
        if n % c == 0:
            return c
    raise ValueError(f"no tile for {n} in {cands}")


def _params(*sem):
    return pltpu.CompilerParams(dimension_semantics=sem if sem else None, vmem_limit_bytes=VMEM_LIMIT)


def _padded_tokens(seq):
    t = -(-(N_META + seq) // 128) * 128
    if t > 1280 and t % 640:
        t = -(-t // 640) * 640
    return t


ROW_TILES = (640, 512, 384, 256, 128)


def _rmsnorm_fwd(h, gain):
    t, d = h.shape
    tr = _pick(t, ROW_TILES)

    def body(h_ref, g_ref, o_ref):
        x = h_ref[...]
        r = lax.rsqrt(jnp.mean(x * x, axis=-1, keepdims=True) + RMS_EPS)
        o_ref[...] = (x * r * g_ref[...]).astype(o_ref.dtype)

    return pl.pallas_call(
        body, grid=(t // tr,), name="rmsnorm_fwd",
        in_specs=[pl.BlockSpec((tr, d), lambda i: (i, 0)), pl.BlockSpec((1, d), lambda i: (0, 0))],
        out_specs=pl.BlockSpec((tr, d), lambda i: (i, 0)),
        out_shape=jax.ShapeDtypeStruct((t, d), BF16),
        compiler_params=_params("arbitrary"),
    )(h, gain)


def _rmsnorm_bwd(h, dy, gain, dres):
    t, d = h.shape
    tr = _pick(t, (320, 256, 128))

    def body(h_ref, dy_ref, g_ref, dres_ref, dh_ref, dhb_ref, dg_ref):
        x = h_ref[...]
        dyv = dy_ref[...].astype(F32)
        r = lax.rsqrt(jnp.mean(x * x, axis=-1, keepdims=True) + RMS_EPS)
        gy = dyv * g_ref[...]
        m = jnp.mean(gy * x, axis=-1, keepdims=True)
        dh = dres_ref[...] + r * gy - x * (r * r * r * m)
        dh_ref[...] = dh
        dhb_ref[...] = dh.astype(BF16)

        @pl.when(pl.program_id(0) == 0)
        def _():
            dg_ref[...] = jnp.zeros_like(dg_ref)

        dg_ref[...] += jnp.sum(dyv * x * r, axis=0, keepdims=True)

    row = pl.BlockSpec((tr, d), lambda i: (i, 0))
    vec = pl.BlockSpec((1, d), lambda i: (0, 0))
    return pl.pallas_call(
        body, grid=(t // tr,), name="rmsnorm_bwd",
        in_specs=[row, row, vec, row], out_specs=[row, row, vec],
        out_shape=[jax.ShapeDtypeStruct((t, d), F32), jax.ShapeDtypeStruct((t, d), BF16),
                   jax.ShapeDtypeStruct((1, d), F32)],
        compiler_params=_params("arbitrary"),
    )(h, dy, gain, dres)


def _mm(a, b, out_dtype, name, res=None):
    m, k = a.shape
    _, n = b.shape
    tm = _pick(m, ROW_TILES)
    out_bytes = jnp.dtype(out_dtype).itemsize + (4 if res is not None else 0)
    fits = lambda tn: 4 * tm * k + 4 * k * tn + 2 * tm * tn * out_bytes <= MM_VMEM_BUDGET
    tn = next(c for c in (n, 2816, 2048, 1536, 1408, 1024, 512, 384, 256, 128) if n % c == 0 and fits(c))

    def body(*refs):
        if res is None:
            a_ref, b_ref, o_ref = refs
        else:
            a_ref, b_ref, r_ref, o_ref = refs
        out = jnp.dot(a_ref[...], b_ref[...], preferred_element_type=F32)
        if res is not None:
            out = out + r_ref[...]
        o_ref[...] = out.astype(o_ref.dtype)

    in_specs = [pl.BlockSpec((tm, k), lambda i, j: (i, 0)), pl.BlockSpec((k, tn), lambda i, j: (0, j))]
    args = [a, b]
    if res is not None:
        in_specs.append(pl.BlockSpec((tm, tn), lambda i, j: (i, j)))
        args.append(res)
    return pl.pallas_call(
        body, grid=(m // tm, n // tn), name=name,
        in_specs=in_specs, out_specs=pl.BlockSpec((tm, tn), lambda i, j: (i, j)),
        out_shape=jax.ShapeDtypeStruct((m, n), out_dtype),
        compiler_params=_params("parallel", "parallel"),
    )(*args)


def _mm_tn(a, g, name):
    t, k = a.shape
    _, n = g.shape
    tk = _pick(k, (1024, 1408, 512))
    tn = _pick(n, (512, 640, 384, 256, 128))
    tt = next(c for c in (3328, 1280) + ROW_TILES
              if t % c == 0 and 4 * c * (tk + tn) + 8 * tk * tn <= MM_VMEM_BUDGET)
    nt = t // tt

    def body(a_ref, g_ref, o_ref):
        @pl.when(pl.program_id(2) == 0)
        def _():
            o_ref[...] = jnp.zeros_like(o_ref)

        o_ref[...] += lax.dot_general(a_ref[...], g_ref[...], (((0,), (0,)), ((), ())),
                                      preferred_element_type=F32)

    return pl.pallas_call(
        body, grid=(k // tk, n // tn, nt), name=name,
        in_specs=[pl.BlockSpec((tt, tk), lambda i, j, s: (s, i)), pl.BlockSpec((tt, tn), lambda i, j, s: (s, j))],
        out_specs=pl.BlockSpec((tk, tn), lambda i, j, s: (i, j)),
        out_shape=jax.ShapeDtypeStruct((k, n), F32),
        compiler_params=_params("parallel", "parallel", "arbitrary"),
    )(a, g)


def _split3_exact(x):
    def top(v):
        return lax.bitcast_convert_type(lax.bitcast_convert_type(v, jnp.int32) & jnp.int32(-65536), F32)

    hi = top(x)
    r1 = x - hi
    mid = top(r1)
    return hi, mid, r1 - mid


def _pair_head(ref, h, rows):
    x = ref[:, 128 * (h // 2):128 * (h // 2) + 128].astype(F32)
    return pltpu.roll(x, HEAD_DIM, axis=1) if h % 2 else x


def _lanes(rows):
    return lax.broadcasted_iota(jnp.int32, (rows, AUG), 1)


def _fox_prep(fq, scal):
    t = fq.shape[0]
    tt = _pick(t, (256, 128))

    def body(q_ref, k_ref, v_ref, s_ref, qa_ref, ka_ref, va_ref, kt_ref, vt_ref):
        lane = _lanes(tt)
        chi, cmid, clo = _split3_exact(s_ref[...])
        ones = lambda lo: jnp.where((lane >= lo) & (lane < lo + 3), 1.0, 0.0)
        for h in range(HEADS):
            col = lambda a: jnp.broadcast_to(a[:, h:h + 1], (tt, AUG))
            c1, c2, c3 = col(chi), col(cmid), col(clo)
            qx = jnp.where(lane == 64, c1, jnp.where(lane == 65, c2, jnp.where(lane == 66, c3, ones(67))))
            kx = jnp.where(lane == 67, -c1, jnp.where(lane == 68, -c2, jnp.where(lane == 69, -c3, ones(64) + ones(70))))
            qa_ref[h] = jnp.where(lane < HEAD_DIM, _pair_head(q_ref, h, tt) * (HEAD_DIM ** -0.5), qx).astype(BF16)
            k_aug = jnp.where(lane < HEAD_DIM, _pair_head(k_ref, h, tt), kx)
            ka_ref[h] = k_aug.astype(BF16)
            kt_ref[h] = k_aug.T.astype(BF16)
            v_aug = jnp.where(lane < HEAD_DIM, _pair_head(v_ref, h, tt), ones(64))
            va_ref[h] = v_aug.astype(BF16)
            vt_ref[h] = v_aug.T.astype(BF16)

    out = pl.BlockSpec((HEADS, tt, AUG), lambda i: (0, i, 0))
    out_t = pl.BlockSpec((HEADS, AUG, tt), lambda i: (0, 0, i))
    shp = jax.ShapeDtypeStruct((HEADS, t, AUG), BF16)
    shp_t = jax.ShapeDtypeStruct((HEADS, AUG, t), BF16)
    return pl.pallas_call(
        body, grid=(t // tt,), name="fox_prep",
        in_specs=[pl.BlockSpec((tt, WIDTH), lambda i: (i, 0)), pl.BlockSpec((tt, WIDTH), lambda i: (i, 1)),
                  pl.BlockSpec((tt, WIDTH), lambda i: (i, 2)), pl.BlockSpec((tt, LANES), lambda i: (i, 0))],
        out_specs=[out, out, out, out_t, out_t], out_shape=[shp, shp, shp, shp_t, shp_t],
        compiler_params=_params("parallel"),
    )(fq, fq, fq, scal)


def _fox_post(oa):
    t = oa.shape[1]
    tt = _pick(t, (256, 128))

    def body(o_ref, out_ref):
        out_ref[...] = jnp.concatenate([o_ref[h][:, :HEAD_DIM] for h in range(HEADS)], axis=1).astype(BF16)

    return pl.pallas_call(
        body, grid=(t // tt,), name="fox_post",
        in_specs=[pl.BlockSpec((HEADS, tt, AUG), lambda i: (0, i, 0))],
        out_specs=pl.BlockSpec((tt, WIDTH), lambda i: (i, 0)),
        out_shape=jax.ShapeDtypeStruct((t, WIDTH), BF16),
        compiler_params=_params("parallel"),
    )(oa)


def _fox_bwd_prep(do, oa):
    t = do.shape[0]
    tt = _pick(t, (256, 128))

    def body(d_ref, o_ref, out_ref, outt_ref):
        lane = _lanes(tt)
        for h in range(HEADS):
            x = _pair_head(d_ref, h, tt)
            delta = jnp.sum(jnp.where(lane < HEAD_DIM, x * o_ref[h], 0.0), axis=1, keepdims=True)
            hi, mid, lo = _split3_exact(jnp.broadcast_to(-delta, (tt, AUG)))
            ex = jnp.where(lane == 64, hi, jnp.where(lane == 65, mid, jnp.where(lane == 66, lo, 0.0)))
            do_aug = jnp.where(lane < HEAD_DIM, x, ex)
            out_ref[h] = do_aug.astype(BF16)
            outt_ref[h] = do_aug.T.astype(BF16)

    hm = pl.BlockSpec((HEADS, tt, AUG), lambda i: (0, i, 0))
    return pl.pallas_call(
        body, grid=(t // tt,), name="fox_bwd_prep",
        in_specs=[pl.BlockSpec((tt, WIDTH), lambda i: (i, 0)), hm],
        out_specs=[hm, pl.BlockSpec((HEADS, AUG, tt), lambda i: (0, 0, i))],
        out_shape=[jax.ShapeDtypeStruct((HEADS, t, AUG), BF16), jax.ShapeDtypeStruct((HEADS, AUG, t), BF16)],
        compiler_params=_params("parallel"),
    )(do, oa)


def _fox_bwd_post(dqt, dkt, dvt):
    t = dqt.shape[2]
    tt = _pick(t, (256, 128))

    def body(dq_ref, dk_ref, dv_ref, out_ref, dsc_ref):
        lane = _lanes(tt)
        dqs = [dq_ref[h].T for h in range(HEADS)]
        dks = [dk_ref[h].T for h in range(HEADS)]
        heads = lambda xs: jnp.concatenate([x[:, :HEAD_DIM] for x in xs], axis=1)
        out_ref[:, 0:WIDTH] = (heads(dqs) * (HEAD_DIM ** -0.5)).astype(BF16)
        out_ref[:, WIDTH:2 * WIDTH] = heads(dks).astype(BF16)
        out_ref[:, 2 * WIDTH:] = heads([dv_ref[h].T for h in range(HEADS)]).astype(BF16)
        dsc = jnp.zeros((tt, LANES), F32)
        for h in range(HEADS):
            both = jnp.where(lane == HEAD_DIM, dqs[h], 0.0) - jnp.where(lane == HEAD_DIM + 3, dks[h], 0.0)
            dsc = jnp.where(lane == h, jnp.sum(both, axis=1, keepdims=True), dsc)
        dsc_ref[...] = dsc

    hm = pl.BlockSpec((HEADS, AUG, tt), lambda i: (0, 0, i))
    return pl.pallas_call(
        body, grid=(t // tt,), name="fox_bwd_post",
        in_specs=[hm, hm, hm],
        out_specs=[pl.BlockSpec((tt, 3 * WIDTH), lambda i: (i, 0)), pl.BlockSpec((tt, LANES), lambda i: (i, 0))],
        out_shape=[jax.ShapeDtypeStruct((t, 3 * WIDTH), BF16), jax.ShapeDtypeStruct((t, LANES), F32)],
        compiler_params=_params("parallel"),
    )(dqt, dkt, dvt)


def _fox_fwd(qa, ka, vat, tq=None):
    h, t, _ = qa.shape
    tq = tq or _pick(t, ROW_TILES)

    def body(q_ref, k_ref, vt_ref, o_ref, qb_ref, qbt_ref, s_ref):
        i = pl.program_id(1)
        q = q_ref[...]
        krow = lax.broadcasted_iota(jnp.int32, (tq, tq), 0)
        qcol = lax.broadcasted_iota(jnp.int32, (tq, tq), 1)
        rows = lambda j: pl.ds(pl.multiple_of(j * tq, tq), tq)

        def scores(j, slot):
            s_ref[slot] = lax.dot_general(k_ref[rows(j), :], q, (((1,), (1,)), ((), ())), preferred_element_type=F32)

        def update(j, slot, carry, masked):
            m, acc = carry
            s = s_ref[slot]
            if masked:
                s = jnp.where(qcol >= krow, s, NEG)
            m_new = jnp.maximum(m, jnp.max(s, axis=0, keepdims=True))
            p = jnp.exp(s - m_new)
            alpha = jnp.exp(m - m_new)
            return m_new, acc * alpha + jnp.dot(vt_ref[:, rows(j)], p.astype(BF16), preferred_element_type=F32)

        def pair(j, carry):
            scores(j + 1, 1)
            carry = update(j, 0, carry, False)
            scores(j + 2, 0)
            return update(j + 1, 1, carry, False)

        def odd_tail(carry):
            scores(i, 1)
            return update(i, 1, update(i - 1, 0, carry, False), True)

        scores(0, 0)
        carry = (jnp.full((1, tq), NEG, F32), jnp.zeros((AUG, tq), F32))
        carry = lax.fori_loop(0, i // 4, lambda jj, c: pair(4 * jj + 2, pair(4 * jj, c)), carry)
        carry = lax.fori_loop(0, (i % 4) // 2, lambda jj, c: pair(4 * (i // 4), c), carry)
        m, acc = lax.cond(i % 2 == 1, odd_tail, lambda c: update(i, 0, c, True), carry)
        sub = lax.broadcasted_iota(jnp.int32, (AUG, tq), 0)
        l = jnp.sum(jnp.where(sub == HEAD_DIM, acc, 0.0), axis=0, keepdims=True)
        out = jnp.where(sub < HEAD_DIM, acc / l, m + jnp.log(l)).T
        o_ref[...] = out
        lane = lax.broadcasted_iota(jnp.int32, (tq, AUG), 1)
        lse = jnp.broadcast_to(jnp.sum(jnp.where(lane == HEAD_DIM, out, 0.0), axis=1, keepdims=True), (tq, AUG))
        hi, mid, lo = _split3_exact(-lse)
        qb = jnp.where(lane == 70, hi, jnp.where(lane == 71, mid, jnp.where(lane == 72, lo, q.astype(F32))))
        qb_ref[...] = qb.astype(BF16)
        qbt_ref[...] = qb.T.astype(BF16)

    blk = pl.BlockSpec((None, tq, AUG), lambda hh, i: (hh, i, 0))
    return pl.pallas_call(
        body, grid=(h, t // tq), name="fox_fwd",
        in_specs=[blk, pl.BlockSpec((None, t, AUG), lambda hh, i: (hh, 0, 0)),
                  pl.BlockSpec((None, AUG, t), lambda hh, i: (hh, 0, 0))],
        out_specs=[blk, blk, pl.BlockSpec((None, AUG, tq), lambda hh, i: (hh, 0, i))],
        out_shape=[jax.ShapeDtypeStruct((h, t, AUG), F32), jax.ShapeDtypeStruct((h, t, AUG), BF16),
                   jax.ShapeDtypeStruct((h, AUG, t), BF16)],
        scratch_shapes=[pltpu.VMEM((2, tq, tq), F32)],
        compiler_params=_params("parallel", "arbitrary"),
    )(qa, ka, vat)


def _fox_bwd(qb, qbt, ka, kat, va, doa, doat, tq=None):
    h, t, _ = qb.shape
    tq = tq or _pick(t, ROW_TILES)
    nq = t // tq

    def body(q_ref, qt_ref, k_ref, kt_ref, v_ref, do_ref, dot_ref, dqt_ref, dkt_ref, dvt_ref, s_ref, dp_ref):
        j = pl.program_id(1)
        n = nq - j

        @pl.when(j == 0)
        def _():
            dqt_ref[...] = jnp.zeros_like(dqt_ref)

        dkt_ref[...] = jnp.zeros_like(dkt_ref)
        dvt_ref[...] = jnp.zeros_like(dvt_ref)
        kj = k_ref[...]
        ktj = kt_ref[...]
        vj = v_ref[...]
        qrow = lax.broadcasted_iota(jnp.int32, (tq, tq), 0)
        kcol = lax.broadcasted_iota(jnp.int32, (tq, tq), 1)
        rows = lambda i: pl.ds(pl.multiple_of(i * tq, tq), tq)
        nt_dims = (((1,), (1,)), ((), ()))

        def scores(i, slot):
            s_ref[slot] = lax.dot_general(q_ref[rows(i), :], kj, nt_dims, preferred_element_type=F32)
            dp_ref[slot] = lax.dot_general(do_ref[rows(i), :], vj, nt_dims, preferred_element_type=F32)

        def update(i, slot):
            p = jnp.exp(jnp.where((qrow >= kcol) | (i > j), s_ref[slot], NEG))
            ds = (p * dp_ref[slot]).astype(BF16)
            dvt_ref[...] += jnp.dot(dot_ref[:, rows(i)], p.astype(BF16), preferred_element_type=F32)
            dkt_ref[...] += jnp.dot(qt_ref[:, rows(i)], ds, preferred_element_type=F32)
            dqt_ref[:, rows(i)] += lax.dot_general(ktj, ds, nt_dims, preferred_element_type=F32)

        def pair(i0):
            scores(i0 + 1, 1)
            update(i0, 0)
            scores(jnp.minimum(i0 + 2, nq - 1), 0)
            update(i0 + 1, 1)

        def quad(kk, carry):
            pair(j + 4 * kk)
            pair(j + 4 * kk + 2)
            return carry

        def last_pair(kk, carry):
            pair(j + 4 * (n // 4))
            return carry

        scores(j, 0)
        lax.fori_loop(0, n // 4, quad, 0)
        lax.fori_loop(0, (n % 4) // 2, last_pair, 0)

        @pl.when(n % 2 == 1)
        def _():
            update(nq - 1, 0)

    once = pl.Buffered(1)
    full = pl.BlockSpec((None, t, AUG), lambda hh, j: (hh, 0, 0), pipeline_mode=once)
    full_t = pl.BlockSpec((None, AUG, t), lambda hh, j: (hh, 0, 0), pipeline_mode=once)
    blk = pl.BlockSpec((None, tq, AUG), lambda hh, j: (hh, j, 0))
    blk_t = pl.BlockSpec((None, AUG, tq), lambda hh, j: (hh, 0, j))
    shp = jax.ShapeDtypeStruct((h, AUG, t), F32)
    return pl.pallas_call(
        body, grid=(h, nq), name="fox_bwd",
        in_specs=[full, full_t, blk, blk_t, blk, full, full_t],
        out_specs=[pl.BlockSpec((None, AUG, t), lambda hh, j: (hh, 0, 0)), blk_t, blk_t], out_shape=[shp, shp, shp],
        scratch_shapes=[pltpu.VMEM((2, tq, tq), F32), pltpu.VMEM((2, tq, tq), F32)],
        compiler_params=_params("parallel", "arbitrary"),
    )(qb, qbt, ka, kat, va, doa, doat)


def _seg_matrix():
    idx = np.arange(WIDTH) // HEAD_DIM
    return jnp.asarray((idx[:, None] == idx[None, :]).astype(np.float32))


def _segsum(x, e):
    return jnp.dot(x, e, precision=HI, preferred_element_type=F32)


def _silu(x):
    return x * jax.nn.sigmoid(x)


def _silu_grad(x):
    s = jax.nn.sigmoid(x)
    return s * (1.0 + x * (1.0 - s))


def _shift_down(x, prev8, k):
    r = pltpu.roll(x, k, axis=0)
    p = pltpu.roll(prev8, k, axis=0)
    row = lax.broadcasted_iota(jnp.int32, prev8.shape, 0)
    head = jnp.where(row < k, p, r[:8])
    return jnp.concatenate([head, r[8:]], axis=0)


def _shift_up(x, next8, k):
    n = x.shape[0]
    r = pltpu.roll(x, n - k, axis=0)
    p = pltpu.roll(next8, 8 - k, axis=0)
    row = lax.broadcasted_iota(jnp.int32, next8.shape, 0)
    tail = jnp.where(row >= 8 - k, p, r[n - 8:])
    return jnp.concatenate([r[:n - 8], tail], axis=0)


def _causal_conv(x, prev8, w_ref, width, cols=slice(None)):
    y = x * w_ref[width - 1:width, cols]
    for k in range(1, width):
        y = y + _shift_down(x, prev8, k) * w_ref[width - 1 - k:width - k, cols]
    return y


def _causal_conv_bwd(x, prev8, dy, dnext8, w_ref, dw_ref, width, cols=slice(None)):
    dx = dy * w_ref[width - 1:width, cols]
    dw_ref[width - 1:width, cols] += jnp.sum(dy * x, axis=0, keepdims=True)
    for k in range(1, width):
        dx = dx + _shift_up(dy, dnext8, k) * w_ref[width - 1 - k:width - k, cols]
        dw_ref[width - 1 - k:width - k, cols] += jnp.sum(dy * _shift_down(x, prev8, k), axis=0, keepdims=True)
    return dx


HALO = 16


def _prev_spec(tt, width, tile=lambda i: i):
    return pl.BlockSpec((HALO, width), lambda i: (jnp.maximum(tile(i) * (tt // HALO) - 1, 0), 0))


def _prev8(p_ref, cols=slice(None)):
    return p_ref[:, cols].astype(F32)[HALO - 8:]


def _store_heads(ref, x):
    for h in range(HEADS):
        ref[h] = x[:, HEAD_DIM * h:HEAD_DIM * (h + 1)]


def _load_heads(ref):
    return jnp.concatenate([ref[h] for h in range(HEADS)], axis=1)


def _softplus(z):
    return jnp.maximum(z, 0.0) + jnp.log1p(jnp.exp(-jnp.abs(z)))


def _tri_masks(tt):
    r = lax.broadcasted_iota(jnp.int32, (tt, tt), 0)
    c = lax.broadcasted_iota(jnp.int32, (tt, tt), 1)
    same_chunk = lax.shift_right_logical(r, 6) == lax.shift_right_logical(c, 6)
    return r, c, same_chunk


def _gate_fwd(small, pbias, pscale):
    t = small.shape[0]
    tt = _pick(t, (256, 128))

    def body(x_ref, pb_ref, ps_ref, o_ref, carry_ref):
        @pl.when(pl.program_id(0) == 0)
        def _():
            carry_ref[...] = jnp.zeros_like(carry_ref)

        lane = lax.broadcasted_iota(jnp.int32, (tt, LANES), 1)
        z = x_ref[...] + pb_ref[...]
        log_f = jnp.where(lane < HEADS, -_softplus(-z), 0.0)
        g = jnp.where((lane >= 2 * HEADS) & (lane < 3 * HEADS), ps_ref[...] * _softplus(z), 0.0)
        r, c, same_chunk = _tri_masks(tt)
        lower = jnp.where(r >= c, 1.0, 0.0)
        lower_chunk = jnp.where((r >= c) & same_chunk, 1.0, 0.0)
        csum = jnp.dot(lower, log_f, precision=lax.Precision.HIGHEST, preferred_element_type=F32) + carry_ref[...]
        gc = jnp.dot(lower_chunk, g, precision=lax.Precision.HIGHEST, preferred_element_type=F32)
        carry_ref[...] += jnp.sum(log_f, axis=0, keepdims=True)
        o_ref[...] = jnp.where(lane < HEADS, csum, jnp.where(lane < 2 * HEADS, jax.nn.sigmoid(z), gc))

    row = pl.BlockSpec((tt, LANES), lambda i: (i, 0))
    vec = pl.BlockSpec((1, LANES), lambda i: (0, 0))
    return pl.pallas_call(
        body, grid=(t // tt,), name="gate_fwd", in_specs=[row, vec, vec], out_specs=row,
        out_shape=jax.ShapeDtypeStruct((t, LANES), F32),
        scratch_shapes=[pltpu.VMEM((1, LANES), F32)],
        compiler_params=_params("arbitrary"),
    )(small, pbias, pscale)


def _gate_bwd(small, pbias, pscale, dscal):
    t = small.shape[0]
    tt = _pick(t, (256, 128))
    nt = t // tt

    def body(x_ref, pb_ref, ps_ref, d_ref, dx_ref, dpb_ref, dps_ref, carry_ref):
        @pl.when(pl.program_id(0) == 0)
        def _():
            carry_ref[...] = jnp.zeros_like(carry_ref)
            dpb_ref[...] = jnp.zeros_like(dpb_ref)
            dps_ref[...] = jnp.zeros_like(dps_ref)

        lane = lax.broadcasted_iota(jnp.int32, (tt, LANES), 1)
        z = x_ref[...] + pb_ref[...]
        d = d_ref[...]
        dc = jnp.where(lane < HEADS, d, 0.0)
        dbeta = jnp.where((lane >= HEADS) & (lane < 2 * HEADS), d, 0.0)
        dgc = jnp.where((lane >= 2 * HEADS) & (lane < 3 * HEADS), d, 0.0)
        r, c, same_chunk = _tri_masks(tt)
        upper = jnp.where(r <= c, 1.0, 0.0)
        upper_chunk = jnp.where((r <= c) & same_chunk, 1.0, 0.0)
        dlogf = jnp.dot(upper, dc, precision=lax.Precision.HIGHEST, preferred_element_type=F32) + carry_ref[...]
        dg = jnp.dot(upper_chunk, dgc, precision=lax.Precision.HIGHEST, preferred_element_type=F32)
        carry_ref[...] += jnp.sum(dc, axis=0, keepdims=True)
        sg = jax.nn.sigmoid(z)
        dz = dlogf * (1.0 - sg) + dbeta * sg * (1.0 - sg) + dg * ps_ref[...] * sg
        dx_ref[...] = dz.astype(dx_ref.dtype)
        dpb_ref[...] += jnp.sum(dz, axis=0, keepdims=True)
        dps_ref[...] += jnp.sum(dg * _softplus(z), axis=0, keepdims=True)

    row = pl.BlockSpec((tt, LANES), lambda i: (nt - 1 - i, 0))
    vec = pl.BlockSpec((1, LANES), lambda i: (0, 0))
    return pl.pallas_call(
        body, grid=(nt,), name="gate_bwd", in_specs=[row, vec, vec, row], out_specs=[row, vec, vec],
        out_shape=[jax.ShapeDtypeStruct((t, LANES), BF16), jax.ShapeDtypeStruct((1, LANES), F32),
                   jax.ShapeDtypeStruct((1, LANES), F32)],
        scratch_shapes=[pltpu.VMEM((1, LANES), F32)],
        compiler_params=_params("arbitrary"),
    )(small, pbias, pscale, dscal)


def _gdn_pre_fwd(xg, conv_w, seg):
    t = xg.shape[0]
    c3 = 3 * WIDTH
    tt = _pick(t, (640, 256, 128))

    def body(x_ref, p_ref, w_ref, e_ref, q_ref, k_ref, v_ref):
        x = x_ref[...].astype(F32)
        prev = jnp.where(pl.program_id(0) == 0, 0.0, _prev8(p_ref))
        s = _silu(_causal_conv(x, prev, w_ref, GDN_CONV))
        e = e_ref[...]
        q = s[:, :WIDTH]
        k = s[:, WIDTH:2 * WIDTH]
        _store_heads(q_ref, q * lax.rsqrt(_segsum(q * q, e) + RMS_EPS) * (HEAD_DIM ** -0.5))
        _store_heads(k_ref, k * lax.rsqrt(_segsum(k * k, e) + RMS_EPS))
        _store_heads(v_ref, s[:, 2 * WIDTH:])

    out = pl.BlockSpec((HEADS, tt, HEAD_DIM), lambda i: (0, i, 0))
    shp = jax.ShapeDtypeStruct((HEADS, t, HEAD_DIM), F32)
    return pl.pallas_call(
        body, grid=(t // tt,), name="gdn_pre_fwd",
        in_specs=[pl.BlockSpec((tt, c3), lambda i: (i, 0)), _prev_spec(tt, c3),
                  pl.BlockSpec((GDN_CONV, c3), lambda i: (0, 0)), pl.BlockSpec((WIDTH, WIDTH), lambda i: (0, 0))],
        out_specs=[out, out, out], out_shape=[shp, shp, shp],
        compiler_params=_params("arbitrary"),
    )(xg, xg, conv_w, seg)


def _gdn_pre_bwd(xg, conv_w, seg, dqn, dkn, dv):
    t = xg.shape[0]
    c3 = 3 * WIDTH
    tt = _pick(t, (640, 256, 128))
    nt = t // tt

    def body(x_ref, p_ref, w_ref, e_ref, dq_ref, dk_ref, dv_ref, dx_ref, dw_ref, carry_ref):
        step = pl.program_id(0)
        x = x_ref[...].astype(F32)
        e = e_ref[...]
        prev = jnp.where(step == nt - 1, 0.0, _prev8(p_ref))
        y = _causal_conv(x, prev, w_ref, GDN_CONV)
        s = _silu(y)
        q = s[:, :WIDTH]
        k = s[:, WIDTH:2 * WIDTH]
        rq = lax.rsqrt(_segsum(q * q, e) + RMS_EPS)
        rk = lax.rsqrt(_segsum(k * k, e) + RMS_EPS)
        gq = _load_heads(dq_ref) * (HEAD_DIM ** -0.5)
        gk = _load_heads(dk_ref)
        dq = rq * gq - q * (rq * rq * rq) * _segsum(gq * q, e)
        dk = rk * gk - k * (rk * rk * rk) * _segsum(gk * k, e)
        dy = jnp.concatenate([dq, dk, _load_heads(dv_ref)], axis=1) * _silu_grad(y)

        @pl.when(step == 0)
        def _():
            carry_ref[...] = jnp.zeros_like(carry_ref)
            dw_ref[...] = jnp.zeros_like(dw_ref)

        dx = _causal_conv_bwd(x, prev, dy, carry_ref[...], w_ref, dw_ref, GDN_CONV)
        dx_ref[...] = dx.astype(dx_ref.dtype)
        carry_ref[...] = dy[:8]

    rev = lambda i: (nt - 1 - i, 0)
    blk = pl.BlockSpec((HEADS, tt, HEAD_DIM), lambda i: (0, nt - 1 - i, 0))
    return pl.pallas_call(
        body, grid=(nt,), name="gdn_pre_bwd",
        in_specs=[pl.BlockSpec((tt, c3), rev), _prev_spec(tt, c3, lambda i: nt - 1 - i),
                  pl.BlockSpec((GDN_CONV, c3), lambda i: (0, 0)), pl.BlockSpec((WIDTH, WIDTH), lambda i: (0, 0)),
                  blk, blk, blk],
        out_specs=[pl.BlockSpec((tt, c3), rev), pl.BlockSpec((GDN_CONV, c3), lambda i: (0, 0))],
        out_shape=[jax.ShapeDtypeStruct((t, c3), BF16), jax.ShapeDtypeStruct((GDN_CONV, c3), F32)],
        scratch_shapes=[pltpu.VMEM((8, c3), F32)],
        compiler_params=_params("arbitrary"),
    )(xg, xg, conv_w, seg, dqn, dkn, dv)


def _bmm(a, b, ca, cb, precision=None):
    return lax.dot_general(a, b, (((ca,), (cb,)), ((0,), (0,))), precision=precision, preferred_element_type=F32)


def _bf(x):
    return x.astype(BF16)


def _tri_inverse(a, eye):
    x = -a
    tinv = eye + x
    pw = x
    for _ in range(5):
        pb = _bf(pw)
        pw = _bmm(pb, pb, 2, 1)
        tinv = tinv + _bmm(_bf(tinv), _bf(pw), 2, 1)
    resid = eye - _bmm(eye + a, tinv, 2, 1, precision=HI)
    return tinv + _bmm(_bf(tinv), _bf(resid), 2, 1)


def _gdn_intra(q, k, v, bc, gcc, gcr):
    ii = lax.broadcasted_iota(jnp.int32, (CHUNK, CHUNK), 0)
    jj = lax.broadcasted_iota(jnp.int32, (CHUNK, CHUNK), 1)
    tril = (ii >= jj)[None]
    strict = (ii > jj)[None]
    eye = jnp.where(ii == jj, 1.0, 0.0).astype(F32)[None]
    last = (ii == CHUNK - 1)[None]
    dm = jnp.exp(jnp.where(tril, gcc - gcr, NEG))
    gam = jnp.exp(gcc)
    kb = k * bc
    vb = v * bc
    kk = _bmm(_bf(kb), _bf(k), 2, 2)
    a = jnp.where(strict, kk * dm, 0.0)
    tinv = _tri_inverse(a, eye)
    uw = _bmm(tinv, jnp.concatenate([vb, kb * gam], axis=2), 2, 1, precision=HI)
    u, wk = uw[:, :, :HEAD_DIM], uw[:, :, HEAD_DIM:]
    qk = _bmm(_bf(q), _bf(k), 2, 2)
    p = jnp.where(tril, qk * dm, 0.0)
    gl = jnp.sum(jnp.where(last, gcc, 0.0), axis=1, keepdims=True)
    edec = jnp.exp(gl - gcc)
    return dict(tril=tril, strict=strict, dm=dm, gam=gam, kb=kb, kk=kk, a=a, tinv=tinv, u=u, wk=wk, qk=qk, p=p,
                qg=q * gam, kt=k * edec, edec=edec, gaml=jnp.exp(gl), last=last)


def _gate_tiles(sc, gct, nb):
    rows = nb * CHUNK
    cols = lambda lane0: jnp.stack([jnp.broadcast_to(sc[:, lane0 + h:lane0 + h + 1], (rows, HEAD_DIM))
                                    for h in range(HEADS)], axis=0).reshape(HEADS * nb, CHUNK, HEAD_DIM)
    gcr = jnp.stack([jnp.broadcast_to(gct[h:h + 1, n * CHUNK:(n + 1) * CHUNK], (CHUNK, CHUNK))
                     for h in range(HEADS) for n in range(nb)], axis=0)
    return cols(HEADS), cols(2 * HEADS), gcr


def _gdn_fwd(q, k, v, scal, gct, nb=None):
    h, t, dh = q.shape
    nc = t // CHUNK
    nb = nb or _pick(nc, (4, 2))
    bsz = h * nb

    def body(q_ref, k_ref, v_ref, sc_ref, gt_ref, o_ref, s0_ref, state_ref):
        @pl.when(pl.program_id(0) == 0)
        def _():
            state_ref[...] = jnp.zeros_like(state_ref)

        ld = lambda r: r[...].reshape(bsz, CHUNK, dh)
        bc, gcc, gcr = _gate_tiles(sc_ref[...], gt_ref[...], nb)
        z = _gdn_intra(ld(q_ref), ld(k_ref), ld(v_ref), bc, gcc, gcr)
        per = lambda x: x.reshape((h, nb) + x.shape[1:])
        u, wk, p, qg, kt, gaml = (per(z[n]) for n in ("u", "wk", "p", "qg", "kt", "gaml"))
        s = state_ref[...]
        for n in range(nb):
            s0_ref[:, n] = s
            sb = _bf(s)
            vn = u[:, n] - _bmm(_bf(wk[:, n]), sb, 2, 1)
            vnb = _bf(vn)
            o_ref[:, n * CHUNK:(n + 1) * CHUNK, :] = _bmm(_bf(jnp.concatenate([qg[:, n], p[:, n]], axis=2)),
                                                          jnp.concatenate([sb, vnb], axis=1), 2, 1)
            s = s * gaml[:, n] + _bmm(_bf(kt[:, n]), vnb, 1, 1)
        state_ref[...] = s

    blk = pl.BlockSpec((h, nb * CHUNK, dh), lambda i: (0, i, 0))
    return pl.pallas_call(
        body, grid=(nc // nb,), name="gdn_fwd",
        in_specs=[blk] * 3 + [pl.BlockSpec((nb * CHUNK, LANES), lambda i: (i, 0)),
                              pl.BlockSpec((h, nb * CHUNK), lambda i: (0, i))],
        out_specs=[blk, pl.BlockSpec((h, nb, dh, dh), lambda i: (0, i, 0, 0))],
        out_shape=[jax.ShapeDtypeStruct((h, t, dh), F32), jax.ShapeDtypeStruct((h, nc, dh, dh), F32)],
        scratch_shapes=[pltpu.VMEM((h, dh, dh), F32)],
        compiler_params=_params("arbitrary"),
    )(q, k, v, scal, gct)


def _gdn_bwd(q, k, v, scal, gct, s0s, do, nb=None):
    h, t, dh = q.shape
    nc = t // CHUNK
    nb = nb or _pick(nc, (4, 2))
    bsz = h * nb
    ng = nc // nb
    rows = nb * CHUNK

    def body(q_ref, k_ref, v_ref, sc_ref, gt_ref, s0_ref, do_ref,
             dq_ref, dk_ref, dv_ref, dsc_ref, dgt_ref, ds_ref):
        @pl.when(pl.program_id(0) == 0)
        def _():
            ds_ref[...] = jnp.zeros_like(ds_ref)

        ld = lambda r: r[...].reshape(bsz, CHUNK, dh)
        q, k, v = ld(q_ref), ld(k_ref), ld(v_ref)
        bc, gcc, gcr = _gate_tiles(sc_ref[...], gt_ref[...], nb)
        z = _gdn_intra(q, k, v, bc, gcc, gcr)
        per = lambda x: x.reshape((h, nb) + x.shape[1:])
        u, wk, p, qg, kt, gaml = (per(z[n]) for n in ("u", "wk", "p", "qg", "kt", "gaml"))
        dout = per(ld(do_ref))
        ds = ds_ref[...]
        d_u, d_wk, d_p, d_qg, d_kt, d_gaml = ([None] * nb for _ in range(6))
        for n in reversed(range(nb)):
            s0 = s0_ref[:, n]
            s0b, dsb, dob = _bf(s0), _bf(ds), _bf(dout[:, n])
            wkb, qgb = _bf(wk[:, n]), _bf(qg[:, n])
            vn = u[:, n] - _bmm(wkb, s0b, 2, 1)
            dvn = _bmm(_bf(p[:, n]), dob, 1, 1) + _bmm(_bf(kt[:, n]), dsb, 2, 1)
            dvnb = _bf(dvn)
            d_u[n] = dvn
            vnb = _bf(vn)
            dpq = _bmm(dob, jnp.concatenate([vnb, s0b], axis=1), 2, 2)
            d_p[n], d_qg[n] = dpq[:, :, :CHUNK], dpq[:, :, CHUNK:]
            d_kt[n] = _bmm(vnb, dsb, 2, 2)
            d_gaml[n] = jnp.sum(s0 * ds, axis=1, keepdims=True)
            d_wk[n] = -_bmm(dvnb, s0b, 2, 2)
            ds = gaml[:, n] * ds + _bmm(jnp.concatenate([qgb, -wkb], axis=1), jnp.concatenate([dob, dvnb], axis=1), 1, 1)
        ds_ref[...] = ds

        flat = lambda xs: jnp.stack(xs, axis=1).reshape((bsz,) + xs[0].shape[1:])
        d_u, d_wk, d_p, d_qg, d_kt, d_gaml = (flat(x) for x in (d_u, d_wk, d_p, d_qg, d_kt, d_gaml))
        tinv, gam, kb, dm = z["tinv"], z["gam"], z["kb"], z["dm"]
        dr = _bmm(tinv, jnp.concatenate([d_u, d_wk], axis=2), 1, 1, precision=HI)
        drv, drk = dr[:, :, :HEAD_DIM], dr[:, :, HEAD_DIM:]
        da = -_bmm(_bf(dr), _bf(jnp.concatenate([z["u"], z["wk"]], axis=2)), 2, 2)
        da = jnp.where(z["strict"], da, 0.0)
        d_p = jnp.where(z["tril"], d_p, 0.0)
        dkk = _bf(da * dm)
        dqk = _bf(d_p * dm)
        dkb = _bmm(dkk, _bf(k), 2, 1) + drk * gam
        dk = (_bmm(jnp.concatenate([dkk, dqk], axis=1), _bf(jnp.concatenate([kb, q], axis=1)), 1, 1)
              + dkb * bc + d_kt * z["edec"])
        dq = _bmm(dqk, _bf(k), 2, 1) + d_qg * gam
        mm = da * z["a"] + d_p * z["p"]
        dkt_kt = d_kt * z["kt"]
        dgl = jnp.sum(dkt_kt, axis=1, keepdims=True) + d_gaml * z["gaml"]
        dgc = mm + d_qg * z["qg"] + drk * kb * gam - dkt_kt + jnp.where(z["last"], dgl, 0.0)
        dq_ref[...] = dq.reshape(h, rows, dh)
        dk_ref[...] = dk.reshape(h, rows, dh)
        dv_ref[...] = (drv * bc).reshape(h, rows, dh)
        dbeta = (dkb * k + drv * v).reshape(h, rows, dh)
        dgc = dgc.reshape(h, rows, dh)
        lane = lax.broadcasted_iota(jnp.int32, (rows, LANES), 1)
        dsc = jnp.zeros((rows, LANES), F32)
        for hh in range(h):
            dsc = jnp.where(lane == HEADS + hh, jnp.sum(dbeta[hh], axis=1, keepdims=True), dsc)
            dsc = jnp.where(lane == 2 * HEADS + hh, jnp.sum(dgc[hh], axis=1, keepdims=True), dsc)
        dsc_ref[...] = dsc
        dgr = -jnp.sum(mm, axis=1, keepdims=True)
        for hh in range(h):
            for n in range(nb):
                dgt_ref[hh:hh + 1, n * CHUNK:(n + 1) * CHUNK] = dgr[hh * nb + n]

    blk = pl.BlockSpec((h, rows, dh), lambda i: (0, ng - 1 - i, 0))
    shp = jax.ShapeDtypeStruct((h, t, dh), F32)
    sc_spec = pl.BlockSpec((rows, LANES), lambda i: (ng - 1 - i, 0))
    gt_spec = pl.BlockSpec((h, rows), lambda i: (0, ng - 1 - i))
    return pl.pallas_call(
        body, grid=(ng,), name="gdn_bwd",
        in_specs=[blk] * 3 + [sc_spec, gt_spec, pl.BlockSpec((h, nb, dh, dh), lambda i: (0, ng - 1 - i, 0, 0)), blk],
        out_specs=[blk] * 3 + [sc_spec, gt_spec],
        out_shape=[shp] * 3 + [jax.ShapeDtypeStruct((t, LANES), F32), jax.ShapeDtypeStruct((h, t), F32)],
        scratch_shapes=[pltpu.VMEM((h, dh, dh), F32)],
        compiler_params=_params("arbitrary"),
    )(q, k, v, scal, gct, s0s, do)


def _gdn_post_fwd(o, xg, gain, seg):
    t = o.shape[1]
    tt = _pick(t, (320, 256, 128))

    def body(o_ref, z_ref, g_ref, e_ref, y_ref):
        x = _load_heads(o_ref)
        r = lax.rsqrt(_segsum(x * x, e_ref[...]) * (1.0 / HEAD_DIM) + RMS_EPS)
        y_ref[...] = (x * r * g_ref[...] * _silu(z_ref[...].astype(F32))).astype(y_ref.dtype)

    return pl.pallas_call(
        body, grid=(t // tt,), name="gdn_post_fwd",
        in_specs=[pl.BlockSpec((HEADS, tt, HEAD_DIM), lambda i: (0, i, 0)), pl.BlockSpec((tt, WIDTH), lambda i: (i, 3)),
                  pl.BlockSpec((1, WIDTH), lambda i: (0, 0)), pl.BlockSpec((WIDTH, WIDTH), lambda i: (0, 0))],
        out_specs=pl.BlockSpec((tt, WIDTH), lambda i: (i, 0)),
        out_shape=jax.ShapeDtypeStruct((t, WIDTH), BF16),
        compiler_params=_params("arbitrary"),
    )(o, xg, gain, seg)


def _gdn_post_bwd(o, xg, gain, seg, dy):
    t = o.shape[1]
    tt = _pick(t, (320, 256, 128))

    def body(o_ref, z_ref, g_ref, e_ref, dy_ref, do_ref, dz_ref, dg_ref):
        x = _load_heads(o_ref)
        zz = z_ref[...].astype(F32)
        e = e_ref[...]
        gain_v = g_ref[...]
        d = dy_ref[...]
        r = lax.rsqrt(_segsum(x * x, e) * (1.0 / HEAD_DIM) + RMS_EPS)
        xr = x * r
        don = d * _silu(zz)
        dz_ref[...] = (d * xr * gain_v * _silu_grad(zz)).astype(dz_ref.dtype)
        gy = don * gain_v
        _store_heads(do_ref, r * gy - xr * (r * r) * (_segsum(gy * x, e) * (1.0 / HEAD_DIM)))

        @pl.when(pl.program_id(0) == 0)
        def _():
            dg_ref[...] = jnp.zeros_like(dg_ref)

        dg_ref[...] += jnp.sum(don * xr, axis=0, keepdims=True)

    row = pl.BlockSpec((tt, WIDTH), lambda i: (i, 0))
    vec = pl.BlockSpec((1, WIDTH), lambda i: (0, 0))
    hm = pl.BlockSpec((HEADS, tt, HEAD_DIM), lambda i: (0, i, 0))
    return pl.pallas_call(
        body, grid=(t // tt,), name="gdn_post_bwd",
        in_specs=[hm, pl.BlockSpec((tt, WIDTH), lambda i: (i, 3)), vec,
                  pl.BlockSpec((WIDTH, WIDTH), lambda i: (0, 0)), row],
        out_specs=[hm, row, vec],
        out_shape=[jax.ShapeDtypeStruct((HEADS, t, HEAD_DIM), F32), jax.ShapeDtypeStruct((t, WIDTH), BF16),
                   jax.ShapeDtypeStruct((1, WIDTH), F32)],
        compiler_params=_params("arbitrary"),
    )(o, xg, gain, seg, dy)


def _mix_fwd(yf, yg, gates, bias):
    t, d = yf.shape
    tt = _pick(t, (320, 256, 128))

    def body(yf_ref, yg_ref, g1_ref, g2_ref, b1_ref, b2_ref, o_ref):
        g1 = jax.nn.sigmoid(g1_ref[...].astype(F32) + b1_ref[...])
        g2 = jax.nn.sigmoid(g2_ref[...].astype(F32) + b2_ref[...])
        o_ref[...] = (g1 * yf_ref[...].astype(F32) + g2 * yg_ref[...].astype(F32)).astype(o_ref.dtype)

    row = pl.BlockSpec((tt, d), lambda i: (i, 0))
    return pl.pallas_call(
        body, grid=(t // tt,), name="mix_fwd",
        in_specs=[row, row, row, pl.BlockSpec((tt, d), lambda i: (i, 1)),
                  pl.BlockSpec((1, d), lambda i: (0, 0)), pl.BlockSpec((1, d), lambda i: (0, 1))],
        out_specs=row, out_shape=jax.ShapeDtypeStruct((t, d), BF16),
        compiler_params=_params("arbitrary"),
    )(yf, yg, gates, gates, bias, bias)


def _mix_bwd(dmix, yf, yg, gates, bias):
    t, d = yf.shape
    tt = _pick(t, (320, 256, 128))

    def body(dm_ref, yf_ref, yg_ref, g1_ref, g2_ref, b1_ref, b2_ref, dyf_ref, dyg_ref, dg_ref, db_ref):
        dm = dm_ref[...].astype(F32)
        g1 = jax.nn.sigmoid(g1_ref[...].astype(F32) + b1_ref[...])
        g2 = jax.nn.sigmoid(g2_ref[...].astype(F32) + b2_ref[...])
        dyf_ref[...] = (dm * g1).astype(BF16)
        dyg_ref[...] = (dm * g2).astype(BF16)
        dgate = jnp.concatenate([dm * yf_ref[...].astype(F32) * g1 * (1.0 - g1),
                                 dm * yg_ref[...].astype(F32) * g2 * (1.0 - g2)], axis=1)
        dg_ref[...] = dgate.astype(BF16)

        @pl.when(pl.program_id(0) == 0)
        def _():
            db_ref[...] = jnp.zeros_like(db_ref)

        db_ref[...] += jnp.sum(dgate, axis=0, keepdims=True)

    row = pl.BlockSpec((tt, d), lambda i: (i, 0))
    wide = pl.BlockSpec((tt, 2 * d), lambda i: (i, 0))
    return pl.pallas_call(
        body, grid=(t // tt,), name="mix_bwd",
        in_specs=[row, row, row, row, pl.BlockSpec((tt, d), lambda i: (i, 1)),
                  pl.BlockSpec((1, d), lambda i: (0, 0)), pl.BlockSpec((1, d), lambda i: (0, 1))],
        out_specs=[row, row, wide, pl.BlockSpec((1, 2 * d), lambda i: (0, 0))],
        out_shape=[jax.ShapeDtypeStruct((t, d), BF16), jax.ShapeDtypeStruct((t, d), BF16),
                   jax.ShapeDtypeStruct((t, 2 * d), BF16), jax.ShapeDtypeStruct((1, 2 * d), F32)],
        compiler_params=_params("arbitrary"),
    )(dmix, yf, yg, gates, gates, bias, bias)


def _ffn_act_fwd(up, conv_w, conv_b):
    t, c = up.shape
    tt = FFN_ROWS

    def body(x_ref, p_ref, w_ref, b_ref, o_ref):
        first = pl.program_id(0) == 0

        def conv(cols):
            prev = jnp.where(first, 0.0, _prev8(p_ref, cols))
            return _causal_conv(x_ref[:, cols].astype(F32), prev, w_ref, FFN_CONV, cols) + b_ref[:, cols]

        for lo in range(0, D_FF, FFN_LANES):
            gate = conv(slice(lo, lo + FFN_LANES))
            val = conv(slice(D_FF + lo, D_FF + lo + FFN_LANES))
            o_ref[:, lo:lo + FFN_LANES] = (_silu(gate) * val).astype(o_ref.dtype)

    return pl.pallas_call(
        body, grid=(t // tt,), name="ffn_act_fwd",
        in_specs=[pl.BlockSpec((tt, c), lambda i: (i, 0)), _prev_spec(tt, c),
                  pl.BlockSpec((FFN_CONV, c), lambda i: (0, 0)), pl.BlockSpec((1, c), lambda i: (0, 0))],
        out_specs=pl.BlockSpec((tt, D_FF), lambda i: (i, 0)),
        out_shape=jax.ShapeDtypeStruct((t, D_FF), BF16),
        compiler_params=_params("arbitrary"),
    )(up, up, conv_w, conv_b)


def _ffn_act_bwd(up, conv_w, conv_b, dact):
    t, c = up.shape
    tt = FFN_ROWS
    nt = t // tt

    def body(x_ref, p_ref, w_ref, b_ref, da_ref, dx_ref, dw_ref, db_ref, carry_ref):
        step = pl.program_id(0)

        @pl.when(step == 0)
        def _():
            carry_ref[...] = jnp.zeros_like(carry_ref)
            dw_ref[...] = jnp.zeros_like(dw_ref)
            db_ref[...] = jnp.zeros_like(db_ref)

        def conv(cols):
            x = x_ref[:, cols].astype(F32)
            prev = jnp.where(step == nt - 1, 0.0, _prev8(p_ref, cols))
            return x, prev, _causal_conv(x, prev, w_ref, FFN_CONV, cols) + b_ref[:, cols]

        def back(cols, x, prev, du):
            dx = _causal_conv_bwd(x, prev, du, carry_ref[:, cols], w_ref, dw_ref, FFN_CONV, cols)
            dx_ref[:, cols] = dx.astype(dx_ref.dtype)
            db_ref[:, cols] += jnp.sum(du, axis=0, keepdims=True)
            carry_ref[:, cols] = du[:8]

        for lo in range(0, D_FF, FFN_LANES):
            gcols, vcols = slice(lo, lo + FFN_LANES), slice(D_FF + lo, D_FF + lo + FFN_LANES)
            xg, pg, gate = conv(gcols)
            xv, pv, val = conv(vcols)
            da = da_ref[:, gcols]
            back(gcols, xg, pg, da * val * _silu_grad(gate))
            back(vcols, xv, pv, da * _silu(gate))

    rev = lambda i: (nt - 1 - i, 0)
    return pl.pallas_call(
        body, grid=(nt,), name="ffn_act_bwd",
        in_specs=[pl.BlockSpec((tt, c), rev),
                  _prev_spec(tt, c, lambda i: nt - 1 - i),
                  pl.BlockSpec((FFN_CONV, c), lambda i: (0, 0)), pl.BlockSpec((1, c), lambda i: (0, 0)),
                  pl.BlockSpec((tt, D_FF), rev)],
        out_specs=[pl.BlockSpec((tt, c), rev), pl.BlockSpec((FFN_CONV, c), lambda i: (0, 0)),
                   pl.BlockSpec((1, c), lambda i: (0, 0))],
        out_shape=[jax.ShapeDtypeStruct((t, c), BF16), jax.ShapeDtypeStruct((FFN_CONV, c), F32),
                   jax.ShapeDtypeStruct((1, c), F32)],
        scratch_shapes=[pltpu.VMEM((8, c), F32)],
        compiler_params=_params("arbitrary"),
    )(up, up, conv_w, conv_b, dact)


def _final_loss(h2, target, gain, seq):
    t, d = h2.shape
    tr = _pick(t, (320, 256, 128))

    def body(h_ref, t_ref, g_ref, loss_ref, dh_ref, dhb_ref, dg_ref):
        i = pl.program_id(0)
        x = h_ref[...]
        gain_v = g_ref[...]
        r = lax.rsqrt(jnp.mean(x * x, axis=-1, keepdims=True) + RMS_EPS)
        xr = x * r
        rows = i * tr + lax.broadcasted_iota(jnp.int32, (tr, 1), 0)
        real = (rows >= N_META) & (rows < N_META + seq)
        err = jnp.where(real, xr * gain_v - t_ref[...], 0.0)
        dy = err * (1.0 / d)
        gy = dy * gain_v
        dh = r * (gy - xr * jnp.mean(gy * xr, axis=-1, keepdims=True))
        dh_ref[...] = dh
        dhb_ref[...] = dh.astype(BF16)

        @pl.when(i == 0)
        def _():
            loss_ref[...] = jnp.zeros_like(loss_ref)
            dg_ref[...] = jnp.zeros_like(dg_ref)

        part = jnp.sum(jnp.sum(err * err, axis=-1, keepdims=True), axis=0, keepdims=True)
        loss_ref[...] += jnp.broadcast_to(part * (0.5 / d), loss_ref.shape)
        dg_ref[...] += jnp.sum(dy * xr, axis=0, keepdims=True)

    row = pl.BlockSpec((tr, d), lambda i: (i, 0))
    vec = pl.BlockSpec((1, d), lambda i: (0, 0))
    return pl.pallas_call(
        body, grid=(t // tr,), name="final_loss",
        in_specs=[row, row, vec],
        out_specs=[pl.BlockSpec((1, LANES), lambda i: (0, 0)), row, row, vec],
        out_shape=[jax.ShapeDtypeStruct((1, LANES), F32), jax.ShapeDtypeStruct((t, d), F32),
                   jax.ShapeDtypeStruct((t, d), BF16), jax.ShapeDtypeStruct((1, d), F32)],
        compiler_params=_params("arbitrary"),
    )(h2, target, gain)


ADAM_TILE_BYTES = 1 << 20


def _adamw(w, m, v, grecv, name):
    r, cols = w.shape
    tr = r
    if r * cols * 4 > ADAM_TILE_BYTES:
        tr = max(d for d in range(8, r + 1, 8) if r % d == 0 and d * cols * 4 <= ADAM_TILE_BYTES)

    def body(w_ref, m_ref, v_ref, g_ref, go_ref, d_ref, mo_ref, vo_ref):
        g = g_ref[0].astype(F32)
        for s in range(1, N_DEV):
            g = g + g_ref[s].astype(F32)
        wv = w_ref[...]
        mn = ADAM_B1 * m_ref[...] + (1.0 - ADAM_B1) * g
        vn = ADAM_B2 * v_ref[...] + (1.0 - ADAM_B2) * (g * g)
        m_hat = mn / (1.0 - ADAM_B1 ** ADAM_STEP)
        v_hat = vn / (1.0 - ADAM_B2 ** ADAM_STEP)
        go_ref[...] = g
        d_ref[...] = -ADAM_LR * (m_hat / (jnp.sqrt(v_hat) + ADAM_EPS) + ADAM_WD * wv)
        mo_ref[...] = mn
        vo_ref[...] = vn

    row = pl.BlockSpec((tr, cols), lambda i: (i, 0))
    shp = jax.ShapeDtypeStruct((r, cols), F32)
    return pl.pallas_call(
        body, grid=(r // tr,), name=name,
        in_specs=[row, row, row, pl.BlockSpec((N_DEV, tr, cols), lambda i: (0, i, 0))],
        out_specs=[row] * 4, out_shape=[shp] * 4,
        compiler_params=_params("parallel"),
    )(w, m, v, grecv)


def _mesh_pos():
    return lax.axis_index("x"), lax.axis_index("y"), lax.axis_index("c")


def _all_gather(shards):
    n = len(shards)

    def body(*refs):
        x_refs, out_refs = refs[:n], refs[n:2 * n]
        send_sems, recv_sems, local_sems = refs[2 * n:]
        x, y, c = _mesh_pos()
        me, sibling = (x, y, c), (x, y, 1 - c)
        chips = [(1 - x, y), (x, 1 - y), (1 - x, 1 - y)]

        def slot(a, px, py, pc):
            return out_refs[a].at[4 * px + 2 * py + pc]

        def copy(a, kk, block, to, src=None):
            return pltpu.make_async_remote_copy(
                src_ref=slot(a, *block) if src is None else src, dst_ref=slot(a, *block),
                send_sem=send_sems.at[7 * a + kk], recv_sem=recv_sems.at[7 * a + kk],
                device_id=to, device_id_type=MESH_ID)

        mine = [pltpu.make_async_copy(x_refs[a], slot(a, *me), local_sems.at[a]) for a in range(n)]
        first = []
        for a in range(n):
            first.append(copy(a, 0, me, sibling, src=x_refs[a]))
            first += [copy(a, 1 + j, me, (*chip, c), src=x_refs[a]) for j, chip in enumerate(chips)]
        for cp in mine + first:
            cp.start()
        passed = []
        for j, chip in enumerate(chips):
            for a in range(n):
                copy(a, 1 + j, (*chip, c), me).wait_recv()
                passed.append(copy(a, 4 + j, (*chip, c), sibling))
                passed[-1].start()
        for a in range(n):
            copy(a, 0, sibling, me).wait_recv()
        for j, chip in enumerate(chips):
            for a in range(n):
                copy(a, 4 + j, (*chip, 1 - c), me).wait_recv()
        for cp in first + passed:
            cp.wait_send()
        for cp in mine:
            cp.wait()

    hbm = pl.BlockSpec(memory_space=pl.ANY)
    return pl.pallas_call(
        body, name="weight_all_gather", in_specs=[hbm] * n, out_specs=[hbm] * n,
        out_shape=[jax.ShapeDtypeStruct((N_DEV,) + s.shape, s.dtype) for s in shards],
        scratch_shapes=[pltpu.SemaphoreType.DMA((7 * n,)), pltpu.SemaphoreType.DMA((7 * n,)),
                        pltpu.SemaphoreType.DMA((n,))],
    )(*shards)


def _grad_exchange(blocks, small):
    n = len(blocks)

    def body(*refs):
        src_refs, dst_refs = refs[:n + 1], refs[n + 1:2 * n + 2]
        send_sems, recv_sems, local_sems = refs[2 * n + 2:]
        x, y, c = _mesh_pos()
        me = 4 * x + 2 * y + c
        copies = []
        for kk in range(1, N_DEV):
            px = 1 - x if kk & 4 else x
            py = 1 - y if kk & 2 else y
            pc = 1 - c if kk & 1 else c
            peer = 4 * px + 2 * py + pc
            for a in range(n + 1):
                copies.append(pltpu.make_async_remote_copy(
                    src_ref=src_refs[a].at[peer] if a < n else src_refs[a], dst_ref=dst_refs[a].at[me],
                    send_sem=send_sems.at[7 * a + kk - 1], recv_sem=recv_sems.at[7 * a + kk - 1],
                    device_id=(px, py, pc), device_id_type=MESH_ID))
        own = [pltpu.make_async_copy(src_refs[a].at[me] if a < n else src_refs[a], dst_refs[a].at[me],
                                     local_sems.at[a]) for a in range(n + 1)]
        for cp in own + copies:
            cp.start()
        for cp in copies + own:
            cp.wait()

    hbm = pl.BlockSpec(memory_space=pl.ANY)
    return pl.pallas_call(
        body, name="grad_exchange", in_specs=[hbm] * (n + 1), out_specs=[hbm] * (n + 1),
        out_shape=[jax.ShapeDtypeStruct(b.shape, b.dtype) for b in blocks]
        + [jax.ShapeDtypeStruct((N_DEV,) + small.shape, small.dtype)],
        scratch_shapes=[pltpu.SemaphoreType.DMA((7 * (n + 1),)), pltpu.SemaphoreType.DMA((7 * (n + 1),)),
                        pltpu.SemaphoreType.DMA((n + 1,))],
    )(*blocks, small)


def _exchange_copies(src_refs, land_refs, send_sems, recv_sems):
    x, y, c = _mesh_pos()
    me = 4 * x + 2 * y + c
    copies = []
    for kk in range(1, N_DEV):
        px = 1 - x if kk & 4 else x
        py = 1 - y if kk & 2 else y
        pc = 1 - c if kk & 1 else c
        for a, (src, land) in enumerate(zip(src_refs, land_refs)):
            copies.append(pltpu.make_async_remote_copy(
                src_ref=src.at[4 * px + 2 * py + pc], dst_ref=land.at[me],
                send_sem=send_sems.at[7 * a + kk - 1], recv_sem=recv_sems.at[7 * a + kk - 1],
                device_id=(px, py, pc), device_id_type=MESH_ID))
    return copies


def _gather_copies(src_refs, land_refs, send_sems, recv_sems):
    x, y, c = _mesh_pos()
    me = 4 * x + 2 * y + c
    copies = []
    for kk in range(1, N_DEV):
        px = 1 - x if kk & 4 else x
        py = 1 - y if kk & 2 else y
        pc = 1 - c if kk & 1 else c
        for a, (src, land) in enumerate(zip(src_refs, land_refs)):
            copies.append(pltpu.make_async_remote_copy(
                src_ref=src, dst_ref=land.at[me],
                send_sem=send_sems.at[7 * a + kk - 1], recv_sem=recv_sems.at[7 * a + kk - 1],
                device_id=(px, py, pc), device_id_type=MESH_ID))
    return copies


_HBM = pl.BlockSpec(memory_space=pltpu.HBM)
_SEM = pl.BlockSpec(memory_space=pltpu.SEMAPHORE)
_DATAFLOW = pltpu.SideEffectType.DATAFLOW_SIDE_EFFECTING


def _split_start(name, make_copies, sources, land_shapes):
    n = len(sources)

    def body(*refs):
        src_refs, land_refs, send_sems, recv_sems = refs[:n], refs[n:2 * n], refs[2 * n], refs[2 * n + 1]
        for cp in make_copies(src_refs, land_refs, send_sems, recv_sems):
            cp.start()
        token = refs[-1]
        token[...] = jnp.zeros_like(token)

    in_hbm = lambda a: pltpu.with_memory_space_constraint(a, pltpu.HBM)
    hbm_shapes = [pltpu.HBM(s.shape, s.dtype) for s in list(sources) + list(land_shapes)]
    outs = pl.pallas_call(
        body, name=name, in_specs=[_HBM] * (2 * n),
        out_shape=(pltpu.SemaphoreType.DMA((7 * n,)), pltpu.SemaphoreType.DMA((7 * n,)), *hbm_shapes,
                   jax.ShapeDtypeStruct((8, LANES), F32)),
        out_specs=(_SEM, _SEM, *[_HBM] * (2 * n), pl.BlockSpec(memory_space=pltpu.VMEM)),
        input_output_aliases={a: 2 + a for a in range(2 * n)},
        compiler_params=pltpu.CompilerParams(has_side_effects=_DATAFLOW),
    )(*[in_hbm(s) for s in sources], *[in_hbm(lax.empty(s.shape, s.dtype)) for s in land_shapes])
    return outs[0], outs[1], outs[2:2 + n], outs[2 + n:2 + 2 * n], outs[-1]


def _split_wait(name, make_copies, send_sems, recv_sems, src_thru, land_thru, after):
    n = len(src_thru)

    def body(*refs):
        src_refs, land_refs, send_sems, recv_sems = refs[:n], refs[n:2 * n], refs[2 * n], refs[2 * n + 1]
        for cp in make_copies(src_refs, land_refs, send_sems, recv_sems):
            cp.wait_send()
            cp.wait_recv()

    outs = pl.pallas_call(
        body, name=name,
        in_specs=[_HBM] * (2 * n) + [_SEM, _SEM, pl.BlockSpec(memory_space=pl.ANY)],
        out_shape=tuple(pltpu.HBM(b.shape, b.dtype) for b in list(src_thru) + list(land_thru)),
        out_specs=[_HBM] * (2 * n), input_output_aliases={a: a for a in range(2 * n)},
        compiler_params=pltpu.CompilerParams(has_side_effects=_DATAFLOW),
    )(*src_thru, *land_thru, send_sems, recv_sems, after)
    return outs[:n], outs[n:]


def _exchange_start(blocks):
    return _split_start("grad_exchange_start", _exchange_copies, blocks, blocks)


def _exchange_wait(send_sems, recv_sems, src_thru, land_thru, after):
    return _split_wait("grad_exchange_wait", _exchange_copies, send_sems, recv_sems, src_thru, land_thru, after)


def _gather_start(shards):
    lands = [jax.ShapeDtypeStruct((N_DEV,) + s.shape, s.dtype) for s in shards]
    return _split_start("weight_gather_start", _gather_copies, shards, lands)


def _gather_wait(send_sems, recv_sems, src_thru, land_thru, after):
    return _split_wait("weight_gather_wait", _gather_copies, send_sems, recv_sems, src_thru, land_thru, after)


def _pad_flat(parts, rows):
    flat = jnp.concatenate([p.reshape(-1) for p in parts])
    return jnp.pad(flat, (0, rows * LANES - flat.shape[0])).reshape(rows, LANES)


def _rows_for(n_elems, mult=1024):
    rows = -(-n_elems // LANES)
    return -(-rows // mult) * mult


SHARDED = ("meta_tokens", "w_in", "gdn_conv_w", "w_branch_fox", "w_branch_gdn", "w_out", "ffn_w_up", "ffn_conv_w",
           "ffn_w_down")
MATMUL = ("w_in", "w_branch_fox", "w_branch_gdn", "w_out", "ffn_w_up", "ffn_w_down")
EXACT = ("meta_tokens", "gdn_conv_w", "ffn_conv_w")
REPLICATED = ("fgt_bias", "gdn_a_log", "gdn_dt_bias", "gdn_norm_w", "gate_bias", "norm_mix_w", "norm_ffn_w",
              "ffn_conv_b", "norm_final_w")
WEIGHTS = ("meta_tokens", "w_in", "fgt_bias", "gdn_conv_w", "gdn_a_log", "gdn_dt_bias", "gdn_norm_w", "gate_bias",
           "w_branch_fox", "w_branch_gdn", "w_out", "norm_mix_w", "norm_ffn_w", "ffn_w_up", "ffn_conv_w",
           "ffn_conv_b", "ffn_w_down", "norm_final_w")


def _unpack(buf, shapes):
    flat = buf.reshape(-1)
    out, off = [], 0
    for s in shapes:
        n = int(np.prod(s))
        out.append(flat[off:off + n].reshape(s))
        off += n
    return out


def _unpack_gathered(buf, shapes):
    flat = buf.reshape(N_DEV, -1)
    out, off = [], 0
    for s in shapes:
        n = int(np.prod(s))
        out.append(flat[:, off:off + n].reshape((N_DEV,) + tuple(s)))
        off += n
    return out


def _cat_cols(g):
    return g.transpose(1, 0, 2).reshape(g.shape[1], -1)


def _col_blocks(full, width):
    return full.reshape(full.shape[0], N_DEV, width).transpose(1, 0, 2)


def _local_step(x, target, w, early=None, late_weights=None):
    seq = x.shape[0]
    t = _padded_tokens(seq)
    pad = t - N_META - seq
    seg = _seg_matrix()
    zrows = jnp.zeros((pad, D_MODEL), F32)
    h0 = jnp.concatenate([w["meta_tokens"], x, zrows], axis=0)
    tgt = jnp.concatenate([jnp.zeros((N_META, D_MODEL), F32), target, zrows], axis=0)

    w_in = w["w_in"]
    o_f, o_g, o_z, o_b, o_a, o_gate = 1536, 1544, 3080, 3592, 3600, 3608
    w_small = jnp.concatenate([w_in[:, o_f:o_f + 8], w_in[:, o_b:o_b + 8], w_in[:, o_a:o_a + 8],
                               jnp.zeros((D_MODEL, LANES - 24), BF16)], axis=1)
    w_r = jnp.concatenate([w_in[:, :1536], w_in[:, o_g:o_z], w_in[:, o_z:o_b], w_in[:, o_gate:], w_small], axis=1)

    a1 = _rmsnorm_fwd(h0, w["norm_mix_w"])
    fq = _mm(a1, w_r[:, :1536], BF16, "proj_fox")
    xg = _mm(a1, w_r[:, 1536:3584], BF16, "proj_gdn")
    gt = _mm(a1, w_r[:, 3584:5632], BF16, "proj_gates")
    sm = _mm(a1, w_r[:, 5632:], F32, "proj_small")

    lanes_pad = lambda a, lo: jnp.pad(a, ((0, 0), (lo, LANES - lo - a.shape[1])))
    neg_exp_a = -jnp.exp(w["gdn_a_log"])
    pbias = lanes_pad(w["fgt_bias"], 0) + lanes_pad(w["gdn_dt_bias"], 2 * HEADS)
    if late_weights is not None:
        pbias = pbias + late_weights[0][0, 0]
    pscale = lanes_pad(neg_exp_a, 2 * HEADS)
    scal = _gate_fwd(sm, pbias, pscale)
    gct = scal[:, 2 * HEADS:3 * HEADS].T

    qa, ka, va, kat, vat = _fox_prep(fq, scal)
    oa, qb, qbt = _fox_fwd(qa, ka, vat)
    o_fox = _fox_post(oa)

    qh, kh, vh = _gdn_pre_fwd(xg, w["gdn_conv_w"], seg)
    og, s0s = _gdn_fwd(qh, kh, vh, scal, gct)
    norm_w = jnp.tile(w["gdn_norm_w"], (1, HEADS))
    ogn = _gdn_post_fwd(og, xg, norm_w, seg)

    if late_weights is not None:
        w = {**w, **late_weights[1](ogn)}
    yf = _mm(o_fox, w["w_branch_fox"], BF16, "branch_fox")
    yg = _mm(ogn, w["w_branch_gdn"], BF16, "branch_gdn")
    mix = _mix_fwd(yf, yg, gt, w["gate_bias"])
    h1 = _mm(mix, w["w_out"], F32, "out_proj", res=h0)
    a2 = _rmsnorm_fwd(h1, w["norm_ffn_w"])
    up = _mm(a2, w["ffn_w_up"], BF16, "ffn_up")
    act = _ffn_act_fwd(up, w["ffn_conv_w"], w["ffn_conv_b"])
    h2 = _mm(act, w["ffn_w_down"], F32, "ffn_down", res=h1)
    loss, dh2, dh2b, g_final = _final_loss(h2, tgt, w["norm_final_w"].reshape(1, D_MODEL), seq)

    grads = {"norm_final_w": g_final.reshape(D_MODEL)}
    grads["ffn_w_down"] = _mm_tn(act, dh2b, "wgrad_ffn_down")
    dact = _mm(dh2b, w["ffn_w_down"].T, F32, "dgrad_ffn_down")
    dup, g_cw, g_cb = _ffn_act_bwd(up, w["ffn_conv_w"], w["ffn_conv_b"], dact)
    grads["ffn_conv_w"], grads["ffn_conv_b"] = g_cw, g_cb
    grads["ffn_w_up"] = _mm_tn(a2, dup, "wgrad_ffn_up")
    da2 = _mm(dup, w["ffn_w_up"].T, BF16, "dgrad_ffn_up")
    dh1, dh1b, grads["norm_ffn_w"] = _rmsnorm_bwd(h1, da2, w["norm_ffn_w"], dh2)
    grads["w_out"] = _mm_tn(mix, dh1b, "wgrad_out")
    dmix = _mm(dh1b, w["w_out"].T, BF16, "dgrad_out")
    dyf, dyg, dgt, grads["gate_bias"] = _mix_bwd(dmix, yf, yg, gt, w["gate_bias"])
    grads["w_branch_fox"] = _mm_tn(o_fox, dyf, "wgrad_branch_fox")
    grads["w_branch_gdn"] = _mm_tn(ogn, dyg, "wgrad_branch_gdn")
    do_fox = _mm(dyf, w["w_branch_fox"].T, F32, "dgrad_branch_fox")
    dogn = _mm(dyg, w["w_branch_gdn"].T, F32, "dgrad_branch_gdn")

    dog, dz, g_nw = _gdn_post_bwd(og, xg, norm_w, seg, dogn)
    grads["gdn_norm_w"] = g_nw.reshape(HEADS, HEAD_DIM).sum(axis=0)[None]
    dqh, dkh, dvh, dscal_g, dgct = _gdn_bwd(qh, kh, vh, scal, gct, s0s, dog)
    dxg, grads["gdn_conv_w"] = _gdn_pre_bwd(xg, w["gdn_conv_w"], seg, dqh, dkh, dvh)

    doa, doat = _fox_bwd_prep(do_fox, oa)
    dfq, dscal_c = _fox_bwd_post(*_fox_bwd(qb, qbt, ka, kat, va, doa, doat))

    dscal = dscal_c + dscal_g + lanes_pad(dgct.T, 2 * HEADS)
    dsm, dpb, dps = _gate_bwd(sm, pbias, pscale, dscal)
    grads["fgt_bias"] = dpb[:, :HEADS]
    grads["gdn_dt_bias"] = dpb[:, 2 * HEADS:3 * HEADS]
    grads["gdn_a_log"] = dps[:, 2 * HEADS:3 * HEADS] * neg_exp_a

    dproj = jnp.concatenate([dfq, dxg, dz, dgt, dsm], axis=1)
    g_r = _mm_tn(a1, dproj, "wgrad_in")
    grads["w_in"] = jnp.concatenate([g_r[:, :1536], g_r[:, 5632:5640], g_r[:, 1536:3072], g_r[:, 3072:3584],
                                     g_r[:, 5640:5648], g_r[:, 5648:5656], g_r[:, 3584:5632]], axis=1)
    token, handle = early(grads) if early is not None else (jnp.zeros((8, LANES), F32), None)
    w_rt = w_r.T + token[0, 0].astype(BF16)
    da1 = _mm(dproj, w_rt, BF16, "dgrad_in")
    dh0, _, grads["norm_mix_w"] = _rmsnorm_bwd(h0, da1, w["norm_mix_w"], dh1)
    grads["meta_tokens"] = dh0[:N_META]
    return loss, dh0[N_META:N_META + seq], grads, handle


def _shard_pieces(arrs):
    return [arrs[n][0] if arrs[n].ndim == 3 else arrs[n] for n in SHARDED]


def _full_grad_blocks(grads):
    g = grads
    cols = lambda a, wd: _col_blocks(a, wd)
    rows = lambda a: a.reshape(N_DEV, a.shape[0] // N_DEV, a.shape[1])
    return [cols(g["w_in"], IN_WIDTH // N_DEV), cols(g["gdn_conv_w"], 3 * WIDTH // N_DEV),
            cols(g["w_branch_fox"], D_MODEL // N_DEV), cols(g["w_branch_gdn"], D_MODEL // N_DEV), rows(g["w_out"]),
            cols(g["ffn_w_up"], 2 * D_FF // N_DEV), cols(g["ffn_conv_w"], 2 * D_FF // N_DEV), rows(g["ffn_w_down"])]


def kernel(x, meta_tokens, w_in, fgt_bias, gdn_conv_w, gdn_a_log, gdn_dt_bias, gdn_norm_w, gate_bias, w_branch_fox, w_branch_gdn, w_out, norm_mix_w, norm_ffn_w, ffn_w_up, ffn_conv_w, ffn_conv_b, ffn_w_down, norm_final_w, loss_target, m_meta_tokens, m_w_in, m_fgt_bias, m_gdn_conv_w, m_gdn_a_log, m_gdn_dt_bias, m_gdn_norm_w, m_gate_bias, m_w_branch_fox, m_w_branch_gdn, m_w_out, m_norm_mix_w, m_norm_ffn_w, m_ffn_w_up, m_ffn_conv_w, m_ffn_conv_b, m_ffn_w_down, m_norm_final_w, v_meta_tokens, v_w_in, v_fgt_bias, v_gdn_conv_w, v_gdn_a_log, v_gdn_dt_bias, v_gdn_norm_w, v_gate_bias, v_w_branch_fox, v_w_branch_gdn, v_w_out, v_norm_mix_w, v_norm_ffn_w, v_ffn_w_up, v_ffn_conv_w, v_ffn_conv_b, v_ffn_w_down, v_norm_final_w):
    wts = dict(meta_tokens=meta_tokens, w_in=w_in, fgt_bias=fgt_bias, gdn_conv_w=gdn_conv_w, gdn_a_log=gdn_a_log,
               gdn_dt_bias=gdn_dt_bias, gdn_norm_w=gdn_norm_w, gate_bias=gate_bias, w_branch_fox=w_branch_fox,
               w_branch_gdn=w_branch_gdn, w_out=w_out, norm_mix_w=norm_mix_w, norm_ffn_w=norm_ffn_w,
               ffn_w_up=ffn_w_up, ffn_conv_w=ffn_conv_w, ffn_conv_b=ffn_conv_b, ffn_w_down=ffn_w_down,
               norm_final_w=norm_final_w)
    mom = dict(meta_tokens=m_meta_tokens, w_in=m_w_in, fgt_bias=m_fgt_bias, gdn_conv_w=m_gdn_conv_w,
               gdn_a_log=m_gdn_a_log, gdn_dt_bias=m_gdn_dt_bias, gdn_norm_w=m_gdn_norm_w, gate_bias=m_gate_bias,
               w_branch_fox=m_w_branch_fox, w_branch_gdn=m_w_branch_gdn, w_out=m_w_out, norm_mix_w=m_norm_mix_w,
               norm_ffn_w=m_norm_ffn_w, ffn_w_up=m_ffn_w_up, ffn_conv_w=m_ffn_conv_w, ffn_conv_b=m_ffn_conv_b,
               ffn_w_down=m_ffn_w_down, norm_final_w=m_norm_final_w)
    var = dict(meta_tokens=v_meta_tokens, w_in=v_w_in, fgt_bias=v_fgt_bias, gdn_conv_w=v_gdn_conv_w,
               gdn_a_log=v_gdn_a_log, gdn_dt_bias=v_gdn_dt_bias, gdn_norm_w=v_gdn_norm_w, gate_bias=v_gate_bias,
               w_branch_fox=v_w_branch_fox, w_branch_gdn=v_w_branch_gdn, w_out=v_w_out, norm_mix_w=v_norm_mix_w,
               norm_ffn_w=v_norm_ffn_w, ffn_w_up=v_ffn_w_up, ffn_conv_w=v_ffn_conv_w, ffn_conv_b=v_ffn_conv_b,
               ffn_w_down=v_ffn_w_down, norm_final_w=v_norm_final_w)

    sh = dict(zip(SHARDED, _shard_pieces(wts)))
    me = 4 * lax.axis_index("x") + 2 * lax.axis_index("y") + lax.axis_index("c")
    late_names = MATMUL[1:]
    late_sems_send, late_sems_recv, late_src, late_land, late_token = _gather_start(
        [sh[n].astype(BF16) for n in late_names])
    exact_shapes = [sh[n].shape for n in EXACT]
    rows_exact = _rows_for(sum(int(np.prod(s)) for s in exact_shapes), 8)
    g_in, g_exact = _all_gather([sh["w_in"].astype(BF16), _pad_flat([sh[n] for n in EXACT], rows_exact)])
    meta_full, conv_full, fconv_full = (_cat_cols(a) for a in _unpack_gathered(g_exact, exact_shapes))
    full = dict(
        meta_tokens=meta_full, w_in=_cat_cols(g_in), gdn_conv_w=conv_full, ffn_conv_w=fconv_full,
        fgt_bias=fgt_bias, gdn_a_log=gdn_a_log, gdn_dt_bias=gdn_dt_bias, gdn_norm_w=gdn_norm_w, gate_bias=gate_bias,
        norm_mix_w=norm_mix_w, norm_ffn_w=norm_ffn_w, ffn_conv_b=ffn_conv_b, norm_final_w=norm_final_w)

    def fetch_late_weights(after):
        shards, lands = _gather_wait(late_sems_send, late_sems_recv, late_src, late_land, after)
        g_bf, g_bg, g_out, g_up, g_down = (lax.dynamic_update_slice_in_dim(land, s[None], me, 0)
                                           for s, land in zip(shards, lands))
        return dict(w_branch_fox=_cat_cols(g_bf), w_branch_gdn=_cat_cols(g_bg), w_out=g_out.reshape(D_MODEL, D_MODEL),
                    ffn_w_up=_cat_cols(g_up), ffn_w_down=g_down.reshape(D_FF, D_MODEL))

    def start_exchange(grads_so_far):
        blocks = [b.astype(BF16) for b in _full_grad_blocks(grads_so_far)]
        send_sems, recv_sems, src_thru, land_thru, token = _exchange_start(blocks)
        return token, (send_sems, recv_sems, src_thru, land_thru)

    loss, grad_x, grads, handle = _local_step(x[0], loss_target[0], full, early=start_exchange,
                                              late_weights=(late_token, fetch_late_weights))
    sent, landed = _exchange_wait(*handle, after=grad_x)
    own = lambda src, land: lax.dynamic_update_slice_in_dim(land, lax.dynamic_slice_in_dim(src, me, 1, 0), me, 0)
    received = [own(src, land) for src, land in zip(sent, landed)]

    rep_parts = [grads[n] for n in REPLICATED] + [loss[:, :1]]
    rep_shapes = [wts[n].shape for n in REPLICATED]
    rows_small = _rows_for(sum(int(np.prod(p.shape)) for p in rep_parts), 8)
    meta_recv, small_recv = _grad_exchange([_col_blocks(grads["meta_tokens"], LANES).astype(BF16)],
                                           _pad_flat(rep_parts, rows_small))
    received = [meta_recv] + received + [small_recv]

    result = {}
    kinds = ("grad", "delta", "new_m", "new_v")
    for n, recv in zip(SHARDED, received[:-1]):
        outs = _adamw(sh[n], _shard_pieces(mom)[SHARDED.index(n)], _shard_pieces(var)[SHARDED.index(n)], recv,
                      "adamw_" + n)
        for kind, a in zip(kinds, outs):
            result[kind, n] = a.reshape(wts[n].shape)
    rep_w = _pad_flat([wts[n] for n in REPLICATED] + [jnp.zeros((1, 1), F32)], rows_small)
    rep_m = _pad_flat([mom[n] for n in REPLICATED] + [jnp.zeros((1, 1), F32)], rows_small)
    rep_v = _pad_flat([var[n] for n in REPLICATED] + [jnp.ones((1, 1), F32)], rows_small)
    outs_r = _adamw(rep_w, rep_m, rep_v, received[-1], "adamw_replicated")
    for kind, br in zip(kinds, outs_r):
        for n, a in zip(REPLICATED, _unpack(br, rep_shapes)):
            result[kind, n] = a
    n_rep = sum(int(np.prod(s)) for s in rep_shapes)
    total_loss = outs_r[0].reshape(-1)[n_rep]
    out = [total_loss, grad_x[None]]
    for kind in ("grad", "delta", "new_m", "new_v"):
        out += [result[kind, n] for n in WEIGHTS]
    return tuple(out)
```

```python
import jax
import jax.numpy as jnp
import numpy as np
from jax import lax
from jax.experimental import pallas as pl
from jax.experimental.pallas import tpu as pltpu

F32 = jnp.float32
BF16 = jnp.bfloat16

D_MODEL = 1024
N_META = 16
HEADS = 8
HEAD_DIM = 64
WIDTH = HEADS * HEAD_DIM
CHUNK = 64
GDN_CONV = 4
D_FF = 2816
FFN_CONV = 3
IN_WIDTH = 5656
RMS_EPS = 1e-6
NEG = -1e30
AUG = 128
N_DEV = 8
LANES = 128

ADAM_LR = 0.001
ADAM_B1 = 0.9
ADAM_B2 = 0.999
ADAM_EPS = 1e-08
ADAM_WD = 0.01
ADAM_STEP = 10

VMEM_LIMIT = 56 * 1024 * 1024
MM_VMEM_BUDGET = 36 * 1024 * 1024
FFN_LANES = 128
FFN_ROWS = 256
HI = lax.Precision.HIGH
MESH_ID = pl.DeviceIdType.MESH


def _pick(n, cands):
    for c in cands:
        if n % c == 0:
            return c
    raise ValueError(f"no tile for {n} in {cands}")


def _params(*sem):
    return pltpu.CompilerParams(dimension_semantics=sem if sem else None, vmem_limit_bytes=VMEM_LIMIT)


def _padded_tokens(seq):
    t = -(-(N_META + seq) // 128) * 128
    if t > 1280 and t % 640:
        t = -(-t // 640) * 640
    return t


ROW_TILES = (640, 512, 384, 256, 128)


def _rmsnorm_fwd(h, gain):
    t, d = h.shape
    tr = _pick(t, ROW_TILES)

    def body(h_ref, g_ref, o_ref):
        x = h_ref[...]
        r = lax.rsqrt(jnp.mean(x * x, axis=-1, keepdims=True) + RMS_EPS)
        o_ref[...] = (x * r * g_ref[...]).astype(o_ref.dtype)

    return pl.pallas_call(
        body, grid=(t // tr,), name="rmsnorm_fwd",
        in_specs=[pl.BlockSpec((tr, d), lambda i: (i, 0)), pl.BlockSpec((1, d), lambda i: (0, 0))],
        out_specs=pl.BlockSpec((tr, d), lambda i: (i, 0)),
        out_shape=jax.ShapeDtypeStruct((t, d), BF16),
        compiler_params=_params("arbitrary"),
    )(h, gain)


def _rmsnorm_bwd(h, dy, gain, dres):
    t, d = h.shape
    tr = _pick(t, (640, 320, 256, 128))

    def body(h_ref, dy_ref, g_ref, dres_ref, dh_ref, dhb_ref, dg_ref):
        x = h_ref[...]
        dyv = dy_ref[...].astype(F32)
        r = lax.rsqrt(jnp.mean(x * x, axis=-1, keepdims=True) + RMS_EPS)
        gy = dyv * g_ref[...]
        m = jnp.mean(gy * x, axis=-1, keepdims=True)
        dh = dres_ref[...] + r * gy - x * (r * r * r * m)
        dh_ref[...] = dh
        dhb_ref[...] = dh.astype(BF16)

        @pl.when(pl.program_id(0) == 0)
        def _():
            dg_ref[...] = jnp.zeros_like(dg_ref)

        dg_ref[...] += jnp.sum(dyv * x * r, axis=0, keepdims=True)

    row = pl.BlockSpec((tr, d), lambda i: (i, 0))
    vec = pl.BlockSpec((1, d), lambda i: (0, 0))
    return pl.pallas_call(
        body, grid=(t // tr,), name="rmsnorm_bwd",
        in_specs=[row, row, vec, row], out_specs=[row, row, vec],
        out_shape=[jax.ShapeDtypeStruct((t, d), F32), jax.ShapeDtypeStruct((t, d), BF16),
                   jax.ShapeDtypeStruct((1, d), F32)],
        compiler_params=_params("arbitrary"),
    )(h, dy, gain, dres)


def _mm(a, b, out_dtype, name, res=None):
    m, k = a.shape
    _, n = b.shape
    tm = _pick(m, ROW_TILES)
    out_bytes = jnp.dtype(out_dtype).itemsize + (4 if res is not None else 0)
    fits = lambda tn: 4 * tm * k + 4 * k * tn + 2 * tm * tn * out_bytes <= MM_VMEM_BUDGET
    tn = next(c for c in (n, 2816, 2048, 1536, 1408, 1024, 512, 384, 256, 128) if n % c == 0 and fits(c))

    def body(*refs):
        if res is None:
            a_ref, b_ref, o_ref = refs
        else:
            a_ref, b_ref, r_ref, o_ref = refs
        out = jnp.dot(a_ref[...], b_ref[...], preferred_element_type=F32)
        if res is not None:
            out = out + r_ref[...]
        o_ref[...] = out.astype(o_ref.dtype)

    in_specs = [pl.BlockSpec((tm, k), lambda i, j: (i, 0)), pl.BlockSpec((k, tn), lambda i, j: (0, j))]
    args = [a, b]
    if res is not None:
        in_specs.append(pl.BlockSpec((tm, tn), lambda i, j: (i, j)))
        args.append(res)
    return pl.pallas_call(
        body, grid=(m // tm, n // tn), name=name,
        in_specs=in_specs, out_specs=pl.BlockSpec((tm, tn), lambda i, j: (i, j)),
        out_shape=jax.ShapeDtypeStruct((m, n), out_dtype),
        compiler_params=_params("parallel", "parallel"),
    )(*args)


def _mm_tn(a, g, name):
    t, k = a.shape
    _, n = g.shape
    tk = _pick(k, (1024, 1408, 512))
    tn = _pick(n, (512, 640, 384, 256, 128))
    tt = next(c for c in (3328, 1280) + ROW_TILES
              if t % c == 0 and 4 * c * (tk + tn) + 8 * tk * tn <= MM_VMEM_BUDGET)
    nt = t // tt

    def body(a_ref, g_ref, o_ref):
        @pl.when(pl.program_id(2) == 0)
        def _():
            o_ref[...] = jnp.zeros_like(o_ref)

        o_ref[...] += lax.dot_general(a_ref[...], g_ref[...], (((0,), (0,)), ((), ())),
                                      preferred_element_type=F32)

    return pl.pallas_call(
        body, grid=(k // tk, n // tn, nt), name=name,
        in_specs=[pl.BlockSpec((tt, tk), lambda i, j, s: (s, i)), pl.BlockSpec((tt, tn), lambda i, j, s: (s, j))],
        out_specs=pl.BlockSpec((tk, tn), lambda i, j, s: (i, j)),
        out_shape=jax.ShapeDtypeStruct((k, n), F32),
        compiler_params=_params("parallel", "parallel", "arbitrary"),
    )(a, g)


def _split3_exact(x):
    def top(v):
        return lax.bitcast_convert_type(lax.bitcast_convert_type(v, jnp.int32) & jnp.int32(-65536), F32)

    hi = top(x)
    r1 = x - hi
    mid = top(r1)
    return hi, mid, r1 - mid


def _pair_head(ref, h, rows):
    x = ref[:, 128 * (h // 2):128 * (h // 2) + 128].astype(F32)
    return pltpu.roll(x, HEAD_DIM, axis=1) if h % 2 else x


def _lanes(rows):
    return lax.broadcasted_iota(jnp.int32, (rows, AUG), 1)


def _fox_prep(fq, scal):
    t = fq.shape[0]
    tt = _pick(t, (640, 256, 128))

    def body(q_ref, k_ref, v_ref, s_ref, qa_ref, ka_ref, va_ref, kt_ref, vt_ref):
        lane = _lanes(tt)
        chi, cmid, clo = _split3_exact(s_ref[...])
        ones = lambda lo: jnp.where((lane >= lo) & (lane < lo + 3), 1.0, 0.0)
        for h in range(HEADS):
            col = lambda a: jnp.broadcast_to(a[:, h:h + 1], (tt, AUG))
            c1, c2, c3 = col(chi), col(cmid), col(clo)
            qx = jnp.where(lane == 64, c1, jnp.where(lane == 65, c2, jnp.where(lane == 66, c3, ones(67))))
            kx = jnp.where(lane == 67, -c1, jnp.where(lane == 68, -c2, jnp.where(lane == 69, -c3, ones(64) + ones(70))))
            qa_ref[h] = jnp.where(lane < HEAD_DIM, _pair_head(q_ref, h, tt) * (HEAD_DIM ** -0.5), qx).astype(BF16)
            k_aug = jnp.where(lane < HEAD_DIM, _pair_head(k_ref, h, tt), kx)
            ka_ref[h] = k_aug.astype(BF16)
            kt_ref[h] = k_aug.T.astype(BF16)
            v_aug = jnp.where(lane < HEAD_DIM, _pair_head(v_ref, h, tt), ones(64))
            va_ref[h] = v_aug.astype(BF16)
            vt_ref[h] = v_aug.T.astype(BF16)

    out = pl.BlockSpec((HEADS, tt, AUG), lambda i: (0, i, 0))
    out_t = pl.BlockSpec((HEADS, AUG, tt), lambda i: (0, 0, i))
    shp = jax.ShapeDtypeStruct((HEADS, t, AUG), BF16)
    shp_t = jax.ShapeDtypeStruct((HEADS, AUG, t), BF16)
    return pl.pallas_call(
        body, grid=(t // tt,), name="fox_prep",
        in_specs=[pl.BlockSpec((tt, WIDTH), lambda i: (i, 0)), pl.BlockSpec((tt, WIDTH), lambda i: (i, 1)),
                  pl.BlockSpec((tt, WIDTH), lambda i: (i, 2)), pl.BlockSpec((tt, LANES), lambda i: (i, 0))],
        out_specs=[out, out, out, out_t, out_t], out_shape=[shp, shp, shp, shp_t, shp_t],
        compiler_params=_params("parallel"),
    )(fq, fq, fq, scal)


def _fox_post(oa):
    t = oa.shape[1]
    tt = _pick(t, (640, 256, 128))

    def body(o_ref, out_ref):
        out_ref[...] = jnp.concatenate([o_ref[h][:, :HEAD_DIM] for h in range(HEADS)], axis=1).astype(BF16)

    return pl.pallas_call(
        body, grid=(t // tt,), name="fox_post",
        in_specs=[pl.BlockSpec((HEADS, tt, AUG), lambda i: (0, i, 0))],
        out_specs=pl.BlockSpec((tt, WIDTH), lambda i: (i, 0)),
        out_shape=jax.ShapeDtypeStruct((t, WIDTH), BF16),
        compiler_params=_params("parallel"),
    )(oa)


def _fox_bwd_prep(do, oa):
    t = do.shape[0]
    tt = _pick(t, (640, 256, 128))

    def body(d_ref, o_ref, out_ref, outt_ref):
        lane = _lanes(tt)
        for h in range(HEADS):
            x = _pair_head(d_ref, h, tt)
            delta = jnp.sum(jnp.where(lane < HEAD_DIM, x * o_ref[h], 0.0), axis=1, keepdims=True)
            hi, mid, lo = _split3_exact(jnp.broadcast_to(-delta, (tt, AUG)))
            ex = jnp.where(lane == 64, hi, jnp.where(lane == 65, mid, jnp.where(lane == 66, lo, 0.0)))
            do_aug = jnp.where(lane < HEAD_DIM, x, ex)
            out_ref[h] = do_aug.astype(BF16)
            outt_ref[h] = do_aug.T.astype(BF16)

    hm = pl.BlockSpec((HEADS, tt, AUG), lambda i: (0, i, 0))
    return pl.pallas_call(
        body, grid=(t // tt,), name="fox_bwd_prep",
        in_specs=[pl.BlockSpec((tt, WIDTH), lambda i: (i, 0)), hm],
        out_specs=[hm, pl.BlockSpec((HEADS, AUG, tt), lambda i: (0, 0, i))],
        out_shape=[jax.ShapeDtypeStruct((HEADS, t, AUG), BF16), jax.ShapeDtypeStruct((HEADS, AUG, t), BF16)],
        compiler_params=_params("parallel"),
    )(do, oa)


def _fox_bwd_post(dqt, dkt, dvt):
    t = dqt.shape[2]
    tt = _pick(t, (640, 256, 128))

    def body(dq_ref, dk_ref, dv_ref, out_ref, dsc_ref):
        lane = _lanes(tt)
        dqs = [dq_ref[h].T for h in range(HEADS)]
        dks = [dk_ref[h].T for h in range(HEADS)]
        heads = lambda xs: jnp.concatenate([x[:, :HEAD_DIM] for x in xs], axis=1)
        out_ref[:, 0:WIDTH] = (heads(dqs) * (HEAD_DIM ** -0.5)).astype(BF16)
        out_ref[:, WIDTH:2 * WIDTH] = heads(dks).astype(BF16)
        out_ref[:, 2 * WIDTH:] = heads([dv_ref[h].T for h in range(HEADS)]).astype(BF16)
        dsc = jnp.zeros((tt, LANES), F32)
        for h in range(HEADS):
            both = jnp.where(lane == HEAD_DIM, dqs[h], 0.0) - jnp.where(lane == HEAD_DIM + 3, dks[h], 0.0)
            dsc = jnp.where(lane == h, jnp.sum(both, axis=1, keepdims=True), dsc)
        dsc_ref[...] = dsc

    hm = pl.BlockSpec((HEADS, AUG, tt), lambda i: (0, 0, i))
    return pl.pallas_call(
        body, grid=(t // tt,), name="fox_bwd_post",
        in_specs=[hm, hm, hm],
        out_specs=[pl.BlockSpec((tt, 3 * WIDTH), lambda i: (i, 0)), pl.BlockSpec((tt, LANES), lambda i: (i, 0))],
        out_shape=[jax.ShapeDtypeStruct((t, 3 * WIDTH), BF16), jax.ShapeDtypeStruct((t, LANES), F32)],
        compiler_params=_params("parallel"),
    )(dqt, dkt, dvt)


def _fox_fwd(qa, ka, vat, tq=None):
    h, t, _ = qa.shape
    tq = tq or _pick(t, ROW_TILES)

    def body(q_ref, k_ref, vt_ref, o_ref, qb_ref, qbt_ref, s_ref):
        i = pl.program_id(1)
        q = q_ref[...]
        krow = lax.broadcasted_iota(jnp.int32, (tq, tq), 0)
        qcol = lax.broadcasted_iota(jnp.int32, (tq, tq), 1)
        rows = lambda j: pl.ds(pl.multiple_of(j * tq, tq), tq)

        def scores(j, slot):
            s_ref[slot] = lax.dot_general(k_ref[rows(j), :], q, (((1,), (1,)), ((), ())), preferred_element_type=F32)

        def update(j, slot, carry, masked):
            m, acc = carry
            s = s_ref[slot]
            if masked:
                s = jnp.where(qcol >= krow, s, NEG)
            m_new = jnp.maximum(m, jnp.max(s, axis=0, keepdims=True))
            p = jnp.exp(s - m_new)
            alpha = jnp.exp(m - m_new)
            return m_new, acc * alpha + jnp.dot(vt_ref[:, rows(j)], p.astype(BF16), preferred_element_type=F32)

        def pair(j, carry):
            scores(j + 1, 1)
            carry = update(j, 0, carry, False)
            scores(j + 2, 0)
            return update(j + 1, 1, carry, False)

        def odd_tail(carry):
            scores(i, 1)
            return update(i, 1, update(i - 1, 0, carry, False), True)

        scores(0, 0)
        carry = (jnp.full((1, tq), NEG, F32), jnp.zeros((AUG, tq), F32))
        carry = lax.fori_loop(0, i // 4, lambda jj, c: pair(4 * jj + 2, pair(4 * jj, c)), carry)
        carry = lax.fori_loop(0, (i % 4) // 2, lambda jj, c: pair(4 * (i // 4), c), carry)
        m, acc = lax.cond(i % 2 == 1, odd_tail, lambda c: update(i, 0, c, True), carry)
        sub = lax.broadcasted_iota(jnp.int32, (AUG, tq), 0)
        l = jnp.sum(jnp.where(sub == HEAD_DIM, acc, 0.0), axis=0, keepdims=True)
        out = jnp.where(sub < HEAD_DIM, acc / l, m + jnp.log(l)).T
        o_ref[...] = out
        lane = lax.broadcasted_iota(jnp.int32, (tq, AUG), 1)
        lse = jnp.broadcast_to(jnp.sum(jnp.where(lane == HEAD_DIM, out, 0.0), axis=1, keepdims=True), (tq, AUG))
        hi, mid, lo = _split3_exact(-lse)
        qb = jnp.where(lane == 70, hi, jnp.where(lane == 71, mid, jnp.where(lane == 72, lo, q.astype(F32))))
        qb_ref[...] = qb.astype(BF16)
        qbt_ref[...] = qb.T.astype(BF16)

    blk = pl.BlockSpec((None, tq, AUG), lambda hh, i: (hh, i, 0))
    return pl.pallas_call(
        body, grid=(h, t // tq), name="fox_fwd",
        in_specs=[blk, pl.BlockSpec((None, t, AUG), lambda hh, i: (hh, 0, 0)),
                  pl.BlockSpec((None, AUG, t), lambda hh, i: (hh, 0, 0))],
        out_specs=[blk, blk, pl.BlockSpec((None, AUG, tq), lambda hh, i: (hh, 0, i))],
        out_shape=[jax.ShapeDtypeStruct((h, t, AUG), F32), jax.ShapeDtypeStruct((h, t, AUG), BF16),
                   jax.ShapeDtypeStruct((h, AUG, t), BF16)],
        scratch_shapes=[pltpu.VMEM((2, tq, tq), F32)],
        compiler_params=_params("parallel", "arbitrary"),
    )(qa, ka, vat)


def _fox_bwd(qb, qbt, ka, kat, va, doa, doat, tq=None):
    h, t, _ = qb.shape
    tq = tq or _pick(t, ROW_TILES)
    nq = t // tq

    def body(q_ref, qt_ref, k_ref, kt_ref, v_ref, do_ref, dot_ref, dqt_ref, dkt_ref, dvt_ref, s_ref, dp_ref):
        j = pl.program_id(1)
        n = nq - j

        @pl.when(j == 0)
        def _():
            dqt_ref[...] = jnp.zeros_like(dqt_ref)

        dkt_ref[...] = jnp.zeros_like(dkt_ref)
        dvt_ref[...] = jnp.zeros_like(dvt_ref)
        kj = k_ref[...]
        ktj = kt_ref[...]
        vj = v_ref[...]
        qrow = lax.broadcasted_iota(jnp.int32, (tq, tq), 0)
        kcol = lax.broadcasted_iota(jnp.int32, (tq, tq), 1)
        rows = lambda i: pl.ds(pl.multiple_of(i * tq, tq), tq)
        nt_dims = (((1,), (1,)), ((), ()))

        def scores(i, slot):
            s_ref[slot] = lax.dot_general(q_ref[rows(i), :], kj, nt_dims, preferred_element_type=F32)
            dp_ref[slot] = lax.dot_general(do_ref[rows(i), :], vj, nt_dims, preferred_element_type=F32)

        def update(i, slot):
            p = jnp.exp(jnp.where((qrow >= kcol) | (i > j), s_ref[slot], NEG))
            ds = (p * dp_ref[slot]).astype(BF16)
            dvt_ref[...] += jnp.dot(dot_ref[:, rows(i)], p.astype(BF16), preferred_element_type=F32)
            dkt_ref[...] += jnp.dot(qt_ref[:, rows(i)], ds, preferred_element_type=F32)
            dqt_ref[:, rows(i)] += lax.dot_general(ktj, ds, nt_dims, preferred_element_type=F32)

        def pair(i0):
            scores(i0 + 1, 1)
            update(i0, 0)
            scores(jnp.minimum(i0 + 2, nq - 1), 0)
            update(i0 + 1, 1)

        def quad(kk, carry):
            pair(j + 4 * kk)
            pair(j + 4 * kk + 2)
            return carry

        def last_pair(kk, carry):
            pair(j + 4 * (n // 4))
            return carry

        scores(j, 0)
        lax.fori_loop(0, n // 4, quad, 0)
        lax.fori_loop(0, (n % 4) // 2, last_pair, 0)

        @pl.when(n % 2 == 1)
        def _():
            update(nq - 1, 0)

    once = pl.Buffered(1)
    full = pl.BlockSpec((None, t, AUG), lambda hh, j: (hh, 0, 0), pipeline_mode=once)
    full_t = pl.BlockSpec((None, AUG, t), lambda hh, j: (hh, 0, 0), pipeline_mode=once)
    blk = pl.BlockSpec((None, tq, AUG), lambda hh, j: (hh, j, 0))
    blk_t = pl.BlockSpec((None, AUG, tq), lambda hh, j: (hh, 0, j))
    shp = jax.ShapeDtypeStruct((h, AUG, t), F32)
    return pl.pallas_call(
        body, grid=(h, nq), name="fox_bwd",
        in_specs=[full, full_t, blk, blk_t, blk, full, full_t],
        out_specs=[pl.BlockSpec((None, AUG, t), lambda hh, j: (hh, 0, 0)), blk_t, blk_t], out_shape=[shp, shp, shp],
        scratch_shapes=[pltpu.VMEM((2, tq, tq), F32), pltpu.VMEM((2, tq, tq), F32)],
        compiler_params=_params("parallel", "arbitrary"),
    )(qb, qbt, ka, kat, va, doa, doat)


def _seg_matrix():
    idx = np.arange(WIDTH) // HEAD_DIM
    return jnp.asarray((idx[:, None] == idx[None, :]).astype(np.float32))


def _segsum(x, e):
    return jnp.dot(x, e, precision=HI, preferred_element_type=F32)


def _silu(x):
    return x * jax.nn.sigmoid(x)


def _silu_grad(x):
    s = jax.nn.sigmoid(x)
    return s * (1.0 + x * (1.0 - s))


def _shift_down(x, prev8, k):
    r = pltpu.roll(x, k, axis=0)
    p = pltpu.roll(prev8, k, axis=0)
    row = lax.broadcasted_iota(jnp.int32, prev8.shape, 0)
    head = jnp.where(row < k, p, r[:8])
    return jnp.concatenate([head, r[8:]], axis=0)


def _shift_up(x, next8, k):
    n = x.shape[0]
    r = pltpu.roll(x, n - k, axis=0)
    p = pltpu.roll(next8, 8 - k, axis=0)
    row = lax.broadcasted_iota(jnp.int32, next8.shape, 0)
    tail = jnp.where(row >= 8 - k, p, r[n - 8:])
    return jnp.concatenate([r[:n - 8], tail], axis=0)


def _causal_conv(x, prev8, w_ref, width, cols=slice(None)):
    y = x * w_ref[width - 1:width, cols]
    for k in range(1, width):
        y = y + _shift_down(x, prev8, k) * w_ref[width - 1 - k:width - k, cols]
    return y


def _causal_conv_bwd(x, prev8, dy, dnext8, w_ref, dw_ref, width, cols=slice(None)):
    dx = dy * w_ref[width - 1:width, cols]
    dw_ref[width - 1:width, cols] += jnp.sum(dy * x, axis=0, keepdims=True)
    for k in range(1, width):
        dx = dx + _shift_up(dy, dnext8, k) * w_ref[width - 1 - k:width - k, cols]
        dw_ref[width - 1 - k:width - k, cols] += jnp.sum(dy * _shift_down(x, prev8, k), axis=0, keepdims=True)
    return dx


HALO = 16


def _prev_spec(tt, width, tile=lambda i: i):
    return pl.BlockSpec((HALO, width), lambda i: (jnp.maximum(tile(i) * (tt // HALO) - 1, 0), 0))


def _prev8(p_ref, cols=slice(None)):
    return p_ref[:, cols].astype(F32)[HALO - 8:]


def _store_heads(ref, x):
    for h in range(HEADS):
        ref[h] = x[:, HEAD_DIM * h:HEAD_DIM * (h + 1)]


def _load_heads(ref):
    return jnp.concatenate([ref[h] for h in range(HEADS)], axis=1)


def _softplus(z):
    return jnp.maximum(z, 0.0) + jnp.log1p(jnp.exp(-jnp.abs(z)))


def _tri_masks(tt):
    r = lax.broadcasted_iota(jnp.int32, (tt, tt), 0)
    c = lax.broadcasted_iota(jnp.int32, (tt, tt), 1)
    same_chunk = lax.shift_right_logical(r, 6) == lax.shift_right_logical(c, 6)
    return r, c, same_chunk


def _gate_fwd(small, pbias, pscale):
    t = small.shape[0]
    tt = _pick(t, (256, 128))

    def body(x_ref, pb_ref, ps_ref, o_ref, carry_ref):
        @pl.when(pl.program_id(0) == 0)
        def _():
            carry_ref[...] = jnp.zeros_like(carry_ref)

        lane = lax.broadcasted_iota(jnp.int32, (tt, LANES), 1)
        z = x_ref[...] + pb_ref[...]
        log_f = jnp.where(lane < HEADS, -_softplus(-z), 0.0)
        g = jnp.where((lane >= 2 * HEADS) & (lane < 3 * HEADS), ps_ref[...] * _softplus(z), 0.0)
        r, c, same_chunk = _tri_masks(tt)
        lower = jnp.where(r >= c, 1.0, 0.0)
        lower_chunk = jnp.where((r >= c) & same_chunk, 1.0, 0.0)
        csum = jnp.dot(lower, log_f, precision=lax.Precision.HIGHEST, preferred_element_type=F32) + carry_ref[...]
        gc = jnp.dot(lower_chunk, g, precision=lax.Precision.HIGHEST, preferred_element_type=F32)
        carry_ref[...] += jnp.sum(log_f, axis=0, keepdims=True)
        o_ref[...] = jnp.where(lane < HEADS, csum, jnp.where(lane < 2 * HEADS, jax.nn.sigmoid(z), gc))

    row = pl.BlockSpec((tt, LANES), lambda i: (i, 0))
    vec = pl.BlockSpec((1, LANES), lambda i: (0, 0))
    return pl.pallas_call(
        body, grid=(t // tt,), name="gate_fwd", in_specs=[row, vec, vec], out_specs=row,
        out_shape=jax.ShapeDtypeStruct((t, LANES), F32),
        scratch_shapes=[pltpu.VMEM((1, LANES), F32)],
        compiler_params=_params("arbitrary"),
    )(small, pbias, pscale)


def _gate_bwd(small, pbias, pscale, dscal):
    t = small.shape[0]
    tt = _pick(t, (256, 128))
    nt = t // tt

    def body(x_ref, pb_ref, ps_ref, d_ref, dx_ref, dpb_ref, dps_ref, carry_ref):
        @pl.when(pl.program_id(0) == 0)
        def _():
            carry_ref[...] = jnp.zeros_like(carry_ref)
            dpb_ref[...] = jnp.zeros_like(dpb_ref)
            dps_ref[...] = jnp.zeros_like(dps_ref)

        lane = lax.broadcasted_iota(jnp.int32, (tt, LANES), 1)
        z = x_ref[...] + pb_ref[...]
        d = d_ref[...]
        dc = jnp.where(lane < HEADS, d, 0.0)
        dbeta = jnp.where((lane >= HEADS) & (lane < 2 * HEADS), d, 0.0)
        dgc = jnp.where((lane >= 2 * HEADS) & (lane < 3 * HEADS), d, 0.0)
        r, c, same_chunk = _tri_masks(tt)
        upper = jnp.where(r <= c, 1.0, 0.0)
        upper_chunk = jnp.where((r <= c) & same_chunk, 1.0, 0.0)
        dlogf = jnp.dot(upper, dc, precision=lax.Precision.HIGHEST, preferred_element_type=F32) + carry_ref[...]
        dg = jnp.dot(upper_chunk, dgc, precision=lax.Precision.HIGHEST, preferred_element_type=F32)
        carry_ref[...] += jnp.sum(dc, axis=0, keepdims=True)
        sg = jax.nn.sigmoid(z)
        dz = dlogf * (1.0 - sg) + dbeta * sg * (1.0 - sg) + dg * ps_ref[...] * sg
        dx_ref[...] = dz.astype(dx_ref.dtype)
        dpb_ref[...] += jnp.sum(dz, axis=0, keepdims=True)
        dps_ref[...] += jnp.sum(dg * _softplus(z), axis=0, keepdims=True)

    row = pl.BlockSpec((tt, LANES), lambda i: (nt - 1 - i, 0))
    vec = pl.BlockSpec((1, LANES), lambda i: (0, 0))
    return pl.pallas_call(
        body, grid=(nt,), name="gate_bwd", in_specs=[row, vec, vec, row], out_specs=[row, vec, vec],
        out_shape=[jax.ShapeDtypeStruct((t, LANES), BF16), jax.ShapeDtypeStruct((1, LANES), F32),
                   jax.ShapeDtypeStruct((1, LANES), F32)],
        scratch_shapes=[pltpu.VMEM((1, LANES), F32)],
        compiler_params=_params("arbitrary"),
    )(small, pbias, pscale, dscal)


def _gdn_pre_fwd(xg, conv_w, seg):
    t = xg.shape[0]
    c3 = 3 * WIDTH
    tt = _pick(t, (640, 256, 128))

    def body(x_ref, p_ref, w_ref, e_ref, q_ref, k_ref, v_ref):
        x = x_ref[...].astype(F32)
        prev = jnp.where(pl.program_id(0) == 0, 0.0, _prev8(p_ref))
        s = _silu(_causal_conv(x, prev, w_ref, GDN_CONV))
        e = e_ref[...]
        q = s[:, :WIDTH]
        k = s[:, WIDTH:2 * WIDTH]
        _store_heads(q_ref, q * lax.rsqrt(_segsum(q * q, e) + RMS_EPS) * (HEAD_DIM ** -0.5))
        _store_heads(k_ref, k * lax.rsqrt(_segsum(k * k, e) + RMS_EPS))
        _store_heads(v_ref, s[:, 2 * WIDTH:])

    out = pl.BlockSpec((HEADS, tt, HEAD_DIM), lambda i: (0, i, 0))
    shp = jax.ShapeDtypeStruct((HEADS, t, HEAD_DIM), F32)
    return pl.pallas_call(
        body, grid=(t // tt,), name="gdn_pre_fwd",
        in_specs=[pl.BlockSpec((tt, c3), lambda i: (i, 0)), _prev_spec(tt, c3),
                  pl.BlockSpec((GDN_CONV, c3), lambda i: (0, 0)), pl.BlockSpec((WIDTH, WIDTH), lambda i: (0, 0))],
        out_specs=[out, out, out], out_shape=[shp, shp, shp],
        compiler_params=_params("arbitrary"),
    )(xg, xg, conv_w, seg)


def _gdn_pre_bwd(xg, conv_w, seg, dqn, dkn, dv):
    t = xg.shape[0]
    c3 = 3 * WIDTH
    tt = _pick(t, (640, 256, 128))
    nt = t // tt

    def body(x_ref, p_ref, w_ref, e_ref, dq_ref, dk_ref, dv_ref, dx_ref, dw_ref, carry_ref):
        step = pl.program_id(0)
        x = x_ref[...].astype(F32)
        e = e_ref[...]
        prev = jnp.where(step == nt - 1, 0.0, _prev8(p_ref))
        y = _causal_conv(x, prev, w_ref, GDN_CONV)
        s = _silu(y)
        q = s[:, :WIDTH]
        k = s[:, WIDTH:2 * WIDTH]
        rq = lax.rsqrt(_segsum(q * q, e) + RMS_EPS)
        rk = lax.rsqrt(_segsum(k * k, e) + RMS_EPS)
        gq = _load_heads(dq_ref) * (HEAD_DIM ** -0.5)
        gk = _load_heads(dk_ref)
        dq = rq * gq - q * (rq * rq * rq) * _segsum(gq * q, e)
        dk = rk * gk - k * (rk * rk * rk) * _segsum(gk * k, e)
        dy = jnp.concatenate([dq, dk, _load_heads(dv_ref)], axis=1) * _silu_grad(y)

        @pl.when(step == 0)
        def _():
            carry_ref[...] = jnp.zeros_like(carry_ref)
            dw_ref[...] = jnp.zeros_like(dw_ref)

        dx = _causal_conv_bwd(x, prev, dy, carry_ref[...], w_ref, dw_ref, GDN_CONV)
        dx_ref[...] = dx.astype(dx_ref.dtype)
        carry_ref[...] = dy[:8]

    rev = lambda i: (nt - 1 - i, 0)
    blk = pl.BlockSpec((HEADS, tt, HEAD_DIM), lambda i: (0, nt - 1 - i, 0))
    return pl.pallas_call(
        body, grid=(nt,), name="gdn_pre_bwd",
        in_specs=[pl.BlockSpec((tt, c3), rev), _prev_spec(tt, c3, lambda i: nt - 1 - i),
                  pl.BlockSpec((GDN_CONV, c3), lambda i: (0, 0)), pl.BlockSpec((WIDTH, WIDTH), lambda i: (0, 0)),
                  blk, blk, blk],
        out_specs=[pl.BlockSpec((tt, c3), rev), pl.BlockSpec((GDN_CONV, c3), lambda i: (0, 0))],
        out_shape=[jax.ShapeDtypeStruct((t, c3), BF16), jax.ShapeDtypeStruct((GDN_CONV, c3), F32)],
        scratch_shapes=[pltpu.VMEM((8, c3), F32)],
        compiler_params=_params("arbitrary"),
    )(xg, xg, conv_w, seg, dqn, dkn, dv)


def _bmm(a, b, ca, cb, precision=None):
    return lax.dot_general(a, b, (((ca,), (cb,)), ((0,), (0,))), precision=precision, preferred_element_type=F32)


def _bf(x):
    return x.astype(BF16)


def _tri_inverse(a, eye):
    x = -a
    tinv = eye + x
    pw = x
    for _ in range(5):
        pb = _bf(pw)
        pw = _bmm(pb, pb, 2, 1)
        tinv = tinv + _bmm(_bf(tinv), _bf(pw), 2, 1)
    resid = eye - _bmm(eye + a, tinv, 2, 1, precision=HI)
    return tinv + _bmm(_bf(tinv), _bf(resid), 2, 1)


def _gdn_intra(q, k, v, bc, gcc, gcr):
    ii = lax.broadcasted_iota(jnp.int32, (CHUNK, CHUNK), 0)
    jj = lax.broadcasted_iota(jnp.int32, (CHUNK, CHUNK), 1)
    tril = (ii >= jj)[None]
    strict = (ii > jj)[None]
    eye = jnp.where(ii == jj, 1.0, 0.0).astype(F32)[None]
    last = (ii == CHUNK - 1)[None]
    dm = jnp.exp(jnp.where(tril, gcc - gcr, NEG))
    gam = jnp.exp(gcc)
    kb = k * bc
    vb = v * bc
    kk = _bmm(_bf(kb), _bf(k), 2, 2)
    a = jnp.where(strict, kk * dm, 0.0)
    tinv = _tri_inverse(a, eye)
    uw = _bmm(tinv, jnp.concatenate([vb, kb * gam], axis=2), 2, 1, precision=HI)
    u, wk = uw[:, :, :HEAD_DIM], uw[:, :, HEAD_DIM:]
    qk = _bmm(_bf(q), _bf(k), 2, 2)
    p = jnp.where(tril, qk * dm, 0.0)
    gl = jnp.sum(jnp.where(last, gcc, 0.0), axis=1, keepdims=True)
    edec = jnp.exp(gl - gcc)
    return dict(tril=tril, strict=strict, dm=dm, gam=gam, kb=kb, kk=kk, a=a, tinv=tinv, u=u, wk=wk, qk=qk, p=p,
                qg=q * gam, kt=k * edec, edec=edec, gaml=jnp.exp(gl), last=last)


def _gate_tiles(sc, gct, nb):
    rows = nb * CHUNK
    cols = lambda lane0: jnp.stack([jnp.broadcast_to(sc[:, lane0 + h:lane0 + h + 1], (rows, HEAD_DIM))
                                    for h in range(HEADS)], axis=0).reshape(HEADS * nb, CHUNK, HEAD_DIM)
    gcr = jnp.stack([jnp.broadcast_to(gct[h:h + 1, n * CHUNK:(n + 1) * CHUNK], (CHUNK, CHUNK))
                     for h in range(HEADS) for n in range(nb)], axis=0)
    return cols(HEADS), cols(2 * HEADS), gcr


def _gdn_fwd(q, k, v, scal, gct, nb=None):
    h, t, dh = q.shape
    nc = t // CHUNK
    nb = nb or _pick(nc, (4, 2))
    bsz = h * nb

    def body(q_ref, k_ref, v_ref, sc_ref, gt_ref, o_ref, s0_ref, state_ref):
        @pl.when(pl.program_id(0) == 0)
        def _():
            state_ref[...] = jnp.zeros_like(state_ref)

        ld = lambda r: r[...].reshape(bsz, CHUNK, dh)
        bc, gcc, gcr = _gate_tiles(sc_ref[...], gt_ref[...], nb)
        z = _gdn_intra(ld(q_ref), ld(k_ref), ld(v_ref), bc, gcc, gcr)
        per = lambda x: x.reshape((h, nb) + x.shape[1:])
        u, wk, p, qg, kt, gaml = (per(z[n]) for n in ("u", "wk", "p", "qg", "kt", "gaml"))
        s = state_ref[...]
        for n in range(nb):
            s0_ref[:, n] = s
            sb = _bf(s)
            vn = u[:, n] - _bmm(_bf(wk[:, n]), sb, 2, 1)
            vnb = _bf(vn)
            o_ref[:, n * CHUNK:(n + 1) * CHUNK, :] = _bmm(_bf(jnp.concatenate([qg[:, n], p[:, n]], axis=2)),
                                                          jnp.concatenate([sb, vnb], axis=1), 2, 1)
            s = s * gaml[:, n] + _bmm(_bf(kt[:, n]), vnb, 1, 1)
        state_ref[...] = s

    blk = pl.BlockSpec((h, nb * CHUNK, dh), lambda i: (0, i, 0))
    return pl.pallas_call(
        body, grid=(nc // nb,), name="gdn_fwd",
        in_specs=[blk] * 3 + [pl.BlockSpec((nb * CHUNK, LANES), lambda i: (i, 0)),
                              pl.BlockSpec((h, nb * CHUNK), lambda i: (0, i))],
        out_specs=[blk, pl.BlockSpec((h, nb, dh, dh), lambda i: (0, i, 0, 0))],
        out_shape=[jax.ShapeDtypeStruct((h, t, dh), F32), jax.ShapeDtypeStruct((h, nc, dh, dh), F32)],
        scratch_shapes=[pltpu.VMEM((h, dh, dh), F32)],
        compiler_params=_params("arbitrary"),
    )(q, k, v, scal, gct)


def _gdn_bwd(q, k, v, scal, gct, s0s, do, nb=None):
    h, t, dh = q.shape
    nc = t // CHUNK
    nb = nb or _pick(nc, (4, 2))
    bsz = h * nb
    ng = nc // nb
    rows = nb * CHUNK

    def body(q_ref, k_ref, v_ref, sc_ref, gt_ref, s0_ref, do_ref,
             dq_ref, dk_ref, dv_ref, dsc_ref, dgt_ref, ds_ref):
        @pl.when(pl.program_id(0) == 0)
        def _():
            ds_ref[...] = jnp.zeros_like(ds_ref)

        ld = lambda r: r[...].reshape(bsz, CHUNK, dh)
        q, k, v = ld(q_ref), ld(k_ref), ld(v_ref)
        bc, gcc, gcr = _gate_tiles(sc_ref[...], gt_ref[...], nb)
        z = _gdn_intra(q, k, v, bc, gcc, gcr)
        per = lambda x: x.reshape((h, nb) + x.shape[1:])
        u, wk, p, qg, kt, gaml = (per(z[n]) for n in ("u", "wk", "p", "qg", "kt", "gaml"))
        dout = per(ld(do_ref))
        ds = ds_ref[...]
        d_u, d_wk, d_p, d_qg, d_kt, d_gaml = ([None] * nb for _ in range(6))
        for n in reversed(range(nb)):
            s0 = s0_ref[:, n]
            s0b, dsb, dob = _bf(s0), _bf(ds), _bf(dout[:, n])
            wkb, qgb = _bf(wk[:, n]), _bf(qg[:, n])
            vn = u[:, n] - _bmm(wkb, s0b, 2, 1)
            dvn = _bmm(_bf(p[:, n]), dob, 1, 1) + _bmm(_bf(kt[:, n]), dsb, 2, 1)
            dvnb = _bf(dvn)
            d_u[n] = dvn
            vnb = _bf(vn)
            dpq = _bmm(dob, jnp.concatenate([vnb, s0b], axis=1), 2, 2)
            d_p[n], d_qg[n] = dpq[:, :, :CHUNK], dpq[:, :, CHUNK:]
            d_kt[n] = _bmm(vnb, dsb, 2, 2)
            d_gaml[n] = jnp.sum(s0 * ds, axis=1, keepdims=True)
            d_wk[n] = -_bmm(dvnb, s0b, 2, 2)
            ds = gaml[:, n] * ds + _bmm(jnp.concatenate([qgb, -wkb], axis=1), jnp.concatenate([dob, dvnb], axis=1), 1, 1)
        ds_ref[...] = ds

        flat = lambda xs: jnp.stack(xs, axis=1).reshape((bsz,) + xs[0].shape[1:])
        d_u, d_wk, d_p, d_qg, d_kt, d_gaml = (flat(x) for x in (d_u, d_wk, d_p, d_qg, d_kt, d_gaml))
        tinv, gam, kb, dm = z["tinv"], z["gam"], z["kb"], z["dm"]
        dr = _bmm(tinv, jnp.concatenate([d_u, d_wk], axis=2), 1, 1, precision=HI)
        drv, drk = dr[:, :, :HEAD_DIM], dr[:, :, HEAD_DIM:]
        da = -_bmm(_bf(dr), _bf(jnp.concatenate([z["u"], z["wk"]], axis=2)), 2, 2)
        da = jnp.where(z["strict"], da, 0.0)
        d_p = jnp.where(z["tril"], d_p, 0.0)
        dkk = _bf(da * dm)
        dqk = _bf(d_p * dm)
        dkb = _bmm(dkk, _bf(k), 2, 1) + drk * gam
        dk = (_bmm(jnp.concatenate([dkk, dqk], axis=1), _bf(jnp.concatenate([kb, q], axis=1)), 1, 1)
              + dkb * bc + d_kt * z["edec"])
        dq = _bmm(dqk, _bf(k), 2, 1) + d_qg * gam
        mm = da * z["a"] + d_p * z["p"]
        dkt_kt = d_kt * z["kt"]
        dgl = jnp.sum(dkt_kt, axis=1, keepdims=True) + d_gaml * z["gaml"]
        dgc = mm + d_qg * z["qg"] + drk * kb * gam - dkt_kt + jnp.where(z["last"], dgl, 0.0)
        dq_ref[...] = dq.reshape(h, rows, dh)
        dk_ref[...] = dk.reshape(h, rows, dh)
        dv_ref[...] = (drv * bc).reshape(h, rows, dh)
        dbeta = (dkb * k + drv * v).reshape(h, rows, dh)
        dgc = dgc.reshape(h, rows, dh)
        lane = lax.broadcasted_iota(jnp.int32, (rows, LANES), 1)
        dsc = jnp.zeros((rows, LANES), F32)
        for hh in range(h):
            dsc = jnp.where(lane == HEADS + hh, jnp.sum(dbeta[hh], axis=1, keepdims=True), dsc)
            dsc = jnp.where(lane == 2 * HEADS + hh, jnp.sum(dgc[hh], axis=1, keepdims=True), dsc)
        dsc_ref[...] = dsc
        dgr = -jnp.sum(mm, axis=1, keepdims=True)
        for hh in range(h):
            for n in range(nb):
                dgt_ref[hh:hh + 1, n * CHUNK:(n + 1) * CHUNK] = dgr[hh * nb + n]

    blk = pl.BlockSpec((h, rows, dh), lambda i: (0, ng - 1 - i, 0))
    shp = jax.ShapeDtypeStruct((h, t, dh), F32)
    sc_spec = pl.BlockSpec((rows, LANES), lambda i: (ng - 1 - i, 0))
    gt_spec = pl.BlockSpec((h, rows), lambda i: (0, ng - 1 - i))
    return pl.pallas_call(
        body, grid=(ng,), name="gdn_bwd",
        in_specs=[blk] * 3 + [sc_spec, gt_spec, pl.BlockSpec((h, nb, dh, dh), lambda i: (0, ng - 1 - i, 0, 0)), blk],
        out_specs=[blk] * 3 + [sc_spec, gt_spec],
        out_shape=[shp] * 3 + [jax.ShapeDtypeStruct((t, LANES), F32), jax.ShapeDtypeStruct((h, t), F32)],
        scratch_shapes=[pltpu.VMEM((h, dh, dh), F32)],
        compiler_params=_params("arbitrary"),
    )(q, k, v, scal, gct, s0s, do)


def _gdn_post_fwd(o, xg, gain, seg):
    t = o.shape[1]
    tt = _pick(t, (640, 320, 256, 128))

    def body(o_ref, z_ref, g_ref, e_ref, y_ref):
        x = _load_heads(o_ref)
        r = lax.rsqrt(_segsum(x * x, e_ref[...]) * (1.0 / HEAD_DIM) + RMS_EPS)
        y_ref[...] = (x * r * g_ref[...] * _silu(z_ref[...].astype(F32))).astype(y_ref.dtype)

    return pl.pallas_call(
        body, grid=(t // tt,), name="gdn_post_fwd",
        in_specs=[pl.BlockSpec((HEADS, tt, HEAD_DIM), lambda i: (0, i, 0)), pl.BlockSpec((tt, WIDTH), lambda i: (i, 3)),
                  pl.BlockSpec((1, WIDTH), lambda i: (0, 0)), pl.BlockSpec((WIDTH, WIDTH), lambda i: (0, 0))],
        out_specs=pl.BlockSpec((tt, WIDTH), lambda i: (i, 0)),
        out_shape=jax.ShapeDtypeStruct((t, WIDTH), BF16),
        compiler_params=_params("arbitrary"),
    )(o, xg, gain, seg)


def _gdn_post_bwd(o, xg, gain, seg, dy):
    t = o.shape[1]
    tt = _pick(t, (640, 320, 256, 128))

    def body(o_ref, z_ref, g_ref, e_ref, dy_ref, do_ref, dz_ref, dg_ref):
        x = _load_heads(o_ref)
        zz = z_ref[...].astype(F32)
        e = e_ref[...]
        gain_v = g_ref[...]
        d = dy_ref[...]
        r = lax.rsqrt(_segsum(x * x, e) * (1.0 / HEAD_DIM) + RMS_EPS)
        xr = x * r
        don = d * _silu(zz)
        dz_ref[...] = (d * xr * gain_v * _silu_grad(zz)).astype(dz_ref.dtype)
        gy = don * gain_v
        _store_heads(do_ref, r * gy - xr * (r * r) * (_segsum(gy * x, e) * (1.0 / HEAD_DIM)))

        @pl.when(pl.program_id(0) == 0)
        def _():
            dg_ref[...] = jnp.zeros_like(dg_ref)

        dg_ref[...] += jnp.sum(don * xr, axis=0, keepdims=True)

    row = pl.BlockSpec((tt, WIDTH), lambda i: (i, 0))
    vec = pl.BlockSpec((1, WIDTH), lambda i: (0, 0))
    hm = pl.BlockSpec((HEADS, tt, HEAD_DIM), lambda i: (0, i, 0))
    return pl.pallas_call(
        body, grid=(t // tt,), name="gdn_post_bwd",
        in_specs=[hm, pl.BlockSpec((tt, WIDTH), lambda i: (i, 3)), vec,
                  pl.BlockSpec((WIDTH, WIDTH), lambda i: (0, 0)), row],
        out_specs=[hm, row, vec],
        out_shape=[jax.ShapeDtypeStruct((HEADS, t, HEAD_DIM), F32), jax.ShapeDtypeStruct((t, WIDTH), BF16),
                   jax.ShapeDtypeStruct((1, WIDTH), F32)],
        compiler_params=_params("arbitrary"),
    )(o, xg, gain, seg, dy)


def _mix_fwd(yf, yg, gates, bias):
    t, d = yf.shape
    tt = _pick(t, (640, 320, 256, 128))

    def body(yf_ref, yg_ref, g1_ref, g2_ref, b1_ref, b2_ref, o_ref):
        g1 = jax.nn.sigmoid(g1_ref[...].astype(F32) + b1_ref[...])
        g2 = jax.nn.sigmoid(g2_ref[...].astype(F32) + b2_ref[...])
        o_ref[...] = (g1 * yf_ref[...].astype(F32) + g2 * yg_ref[...].astype(F32)).astype(o_ref.dtype)

    row = pl.BlockSpec((tt, d), lambda i: (i, 0))
    return pl.pallas_call(
        body, grid=(t // tt,), name="mix_fwd",
        in_specs=[row, row, row, pl.BlockSpec((tt, d), lambda i: (i, 1)),
                  pl.BlockSpec((1, d), lambda i: (0, 0)), pl.BlockSpec((1, d), lambda i: (0, 1))],
        out_specs=row, out_shape=jax.ShapeDtypeStruct((t, d), BF16),
        compiler_params=_params("arbitrary"),
    )(yf, yg, gates, gates, bias, bias)


def _mix_bwd(dmix, yf, yg, gates, bias):
    t, d = yf.shape
    tt = _pick(t, (640, 320, 256, 128))

    def body(dm_ref, yf_ref, yg_ref, g1_ref, g2_ref, b1_ref, b2_ref, dyf_ref, dyg_ref, dg_ref, db_ref):
        dm = dm_ref[...].astype(F32)
        g1 = jax.nn.sigmoid(g1_ref[...].astype(F32) + b1_ref[...])
        g2 = jax.nn.sigmoid(g2_ref[...].astype(F32) + b2_ref[...])
        dyf_ref[...] = (dm * g1).astype(BF16)
        dyg_ref[...] = (dm * g2).astype(BF16)
        dgate = jnp.concatenate([dm * yf_ref[...].astype(F32) * g1 * (1.0 - g1),
                                 dm * yg_ref[...].astype(F32) * g2 * (1.0 - g2)], axis=1)
        dg_ref[...] = dgate.astype(BF16)

        @pl.when(pl.program_id(0) == 0)
        def _():
            db_ref[...] = jnp.zeros_like(db_ref)

        db_ref[...] += jnp.sum(dgate, axis=0, keepdims=True)

    row = pl.BlockSpec((tt, d), lambda i: (i, 0))
    wide = pl.BlockSpec((tt, 2 * d), lambda i: (i, 0))
    return pl.pallas_call(
        body, grid=(t // tt,), name="mix_bwd",
        in_specs=[row, row, row, row, pl.BlockSpec((tt, d), lambda i: (i, 1)),
                  pl.BlockSpec((1, d), lambda i: (0, 0)), pl.BlockSpec((1, d), lambda i: (0, 1))],
        out_specs=[row, row, wide, pl.BlockSpec((1, 2 * d), lambda i: (0, 0))],
        out_shape=[jax.ShapeDtypeStruct((t, d), BF16), jax.ShapeDtypeStruct((t, d), BF16),
                   jax.ShapeDtypeStruct((t, 2 * d), BF16), jax.ShapeDtypeStruct((1, 2 * d), F32)],
        compiler_params=_params("arbitrary"),
    )(dmix, yf, yg, gates, gates, bias, bias)


def _ffn_act_fwd(up, conv_w, conv_b):
    t, c = up.shape
    tt = FFN_ROWS

    def body(x_ref, p_ref, w_ref, b_ref, o_ref):
        first = pl.program_id(0) == 0

        def conv(cols):
            prev = jnp.where(first, 0.0, _prev8(p_ref, cols))
            return _causal_conv(x_ref[:, cols].astype(F32), prev, w_ref, FFN_CONV, cols) + b_ref[:, cols]

        for lo in range(0, D_FF, FFN_LANES):
            gate = conv(slice(lo, lo + FFN_LANES))
            val = conv(slice(D_FF + lo, D_FF + lo + FFN_LANES))
            o_ref[:, lo:lo + FFN_LANES] = (_silu(gate) * val).astype(o_ref.dtype)

    return pl.pallas_call(
        body, grid=(t // tt,), name="ffn_act_fwd",
        in_specs=[pl.BlockSpec((tt, c), lambda i: (i, 0)), _prev_spec(tt, c),
                  pl.BlockSpec((FFN_CONV, c), lambda i: (0, 0)), pl.BlockSpec((1, c), lambda i: (0, 0))],
        out_specs=pl.BlockSpec((tt, D_FF), lambda i: (i, 0)),
        out_shape=jax.ShapeDtypeStruct((t, D_FF), BF16),
        compiler_params=_params("arbitrary"),
    )(up, up, conv_w, conv_b)


def _ffn_act_bwd(up, conv_w, conv_b, dact):
    t, c = up.shape
    tt = FFN_ROWS
    nt = t // tt

    def body(x_ref, p_ref, w_ref, b_ref, da_ref, dx_ref, dw_ref, db_ref, carry_ref):
        step = pl.program_id(0)

        @pl.when(step == 0)
        def _():
            carry_ref[...] = jnp.zeros_like(carry_ref)
            dw_ref[...] = jnp.zeros_like(dw_ref)
            db_ref[...] = jnp.zeros_like(db_ref)

        def conv(cols):
            x = x_ref[:, cols].astype(F32)
            prev = jnp.where(step == nt - 1, 0.0, _prev8(p_ref, cols))
            return x, prev, _causal_conv(x, prev, w_ref, FFN_CONV, cols) + b_ref[:, cols]

        def back(cols, x, prev, du):
            dx = _causal_conv_bwd(x, prev, du, carry_ref[:, cols], w_ref, dw_ref, FFN_CONV, cols)
            dx_ref[:, cols] = dx.astype(dx_ref.dtype)
            db_ref[:, cols] += jnp.sum(du, axis=0, keepdims=True)
            carry_ref[:, cols] = du[:8]

        for lo in range(0, D_FF, FFN_LANES):
            gcols, vcols = slice(lo, lo + FFN_LANES), slice(D_FF + lo, D_FF + lo + FFN_LANES)
            xg, pg, gate = conv(gcols)
            xv, pv, val = conv(vcols)
            da = da_ref[:, gcols]
            back(gcols, xg, pg, da * val * _silu_grad(gate))
            back(vcols, xv, pv, da * _silu(gate))

    rev = lambda i: (nt - 1 - i, 0)
    return pl.pallas_call(
        body, grid=(nt,), name="ffn_act_bwd",
        in_specs=[pl.BlockSpec((tt, c), rev),
                  _prev_spec(tt, c, lambda i: nt - 1 - i),
                  pl.BlockSpec((FFN_CONV, c), lambda i: (0, 0)), pl.BlockSpec((1, c), lambda i: (0, 0)),
                  pl.BlockSpec((tt, D_FF), rev)],
        out_specs=[pl.BlockSpec((tt, c), rev), pl.BlockSpec((FFN_CONV, c), lambda i: (0, 0)),
                   pl.BlockSpec((1, c), lambda i: (0, 0))],
        out_shape=[jax.ShapeDtypeStruct((t, c), BF16), jax.ShapeDtypeStruct((FFN_CONV, c), F32),
                   jax.ShapeDtypeStruct((1, c), F32)],
        scratch_shapes=[pltpu.VMEM((8, c), F32)],
        compiler_params=_params("arbitrary"),
    )(up, up, conv_w, conv_b, dact)


def _final_loss(h2, target, gain, seq):
    t, d = h2.shape
    tr = _pick(t, (640, 320, 256, 128))

    def body(h_ref, t_ref, g_ref, loss_ref, dh_ref, dhb_ref, dg_ref):
        i = pl.program_id(0)
        x = h_ref[...]
        gain_v = g_ref[...]
        r = lax.rsqrt(jnp.mean(x * x, axis=-1, keepdims=True) + RMS_EPS)
        xr = x * r
        rows = i * tr + lax.broadcasted_iota(jnp.int32, (tr, 1), 0)
        real = (rows >= N_META) & (rows < N_META + seq)
        err = jnp.where(real, xr * gain_v - t_ref[...], 0.0)
        dy = err * (1.0 / d)
        gy = dy * gain_v
        dh = r * (gy - xr * jnp.mean(gy * xr, axis=-1, keepdims=True))
        dh_ref[...] = dh
        dhb_ref[...] = dh.astype(BF16)

        @pl.when(i == 0)
        def _():
            loss_ref[...] = jnp.zeros_like(loss_ref)
            dg_ref[...] = jnp.zeros_like(dg_ref)

        part = jnp.sum(jnp.sum(err * err, axis=-1, keepdims=True), axis=0, keepdims=True)
        loss_ref[...] += jnp.broadcast_to(part * (0.5 / d), loss_ref.shape)
        dg_ref[...] += jnp.sum(dy * xr, axis=0, keepdims=True)

    row = pl.BlockSpec((tr, d), lambda i: (i, 0))
    vec = pl.BlockSpec((1, d), lambda i: (0, 0))
    return pl.pallas_call(
        body, grid=(t // tr,), name="final_loss",
        in_specs=[row, row, vec],
        out_specs=[pl.BlockSpec((1, LANES), lambda i: (0, 0)), row, row, vec],
        out_shape=[jax.ShapeDtypeStruct((1, LANES), F32), jax.ShapeDtypeStruct((t, d), F32),
                   jax.ShapeDtypeStruct((t, d), BF16), jax.ShapeDtypeStruct((1, d), F32)],
        compiler_params=_params("arbitrary"),
    )(h2, target, gain)


ADAM_TILE_BYTES = 1 << 20


def _adamw(w, m, v, grecv, name):
    r, cols = w.shape
    tr = r
    if r * cols * 4 > ADAM_TILE_BYTES:
        tr = max(d for d in range(8, r + 1, 8) if r % d == 0 and d * cols * 4 <= ADAM_TILE_BYTES)

    def body(w_ref, m_ref, v_ref, g_ref, go_ref, d_ref, mo_ref, vo_ref):
        g = g_ref[0].astype(F32)
        for s in range(1, N_DEV):
            g = g + g_ref[s].astype(F32)
        wv = w_ref[...]
        mn = ADAM_B1 * m_ref[...] + (1.0 - ADAM_B1) * g
        vn = ADAM_B2 * v_ref[...] + (1.0 - ADAM_B2) * (g * g)
        m_hat = mn / (1.0 - ADAM_B1 ** ADAM_STEP)
        v_hat = vn / (1.0 - ADAM_B2 ** ADAM_STEP)
        go_ref[...] = g
        d_ref[...] = -ADAM_LR * (m_hat / (jnp.sqrt(v_hat) + ADAM_EPS) + ADAM_WD * wv)
        mo_ref[...] = mn
        vo_ref[...] = vn

    row = pl.BlockSpec((tr, cols), lambda i: (i, 0))
    shp = jax.ShapeDtypeStruct((r, cols), F32)
    return pl.pallas_call(
        body, grid=(r // tr,), name=name,
        in_specs=[row, row, row, pl.BlockSpec((N_DEV, tr, cols), lambda i: (0, i, 0))],
        out_specs=[row] * 4, out_shape=[shp] * 4,
        compiler_params=_params("parallel"),
    )(w, m, v, grecv)


def _mesh_pos():
    return lax.axis_index("x"), lax.axis_index("y"), lax.axis_index("c")


def _all_gather(shards):
    n = len(shards)

    def body(*refs):
        x_refs, out_refs = refs[:n], refs[n:2 * n]
        send_sems, recv_sems, local_sems = refs[2 * n:]
        x, y, c = _mesh_pos()
        me, sibling = (x, y, c), (x, y, 1 - c)
        chips = [(1 - x, y), (x, 1 - y), (1 - x, 1 - y)]

        def slot(a, px, py, pc):
            return out_refs[a].at[4 * px + 2 * py + pc]

        def copy(a, kk, block, to, src=None):
            return pltpu.make_async_remote_copy(
                src_ref=slot(a, *block) if src is None else src, dst_ref=slot(a, *block),
                send_sem=send_sems.at[7 * a + kk], recv_sem=recv_sems.at[7 * a + kk],
                device_id=to, device_id_type=MESH_ID)

        mine = [pltpu.make_async_copy(x_refs[a], slot(a, *me), local_sems.at[a]) for a in range(n)]
        first = []
        for a in range(n):
            first.append(copy(a, 0, me, sibling, src=x_refs[a]))
            first += [copy(a, 1 + j, me, (*chip, c), src=x_refs[a]) for j, chip in enumerate(chips)]
        for cp in mine + first:
            cp.start()
        passed = []
        for j, chip in enumerate(chips):
            for a in range(n):
                copy(a, 1 + j, (*chip, c), me).wait_recv()
                passed.append(copy(a, 4 + j, (*chip, c), sibling))
                passed[-1].start()
        for a in range(n):
            copy(a, 0, sibling, me).wait_recv()
        for j, chip in enumerate(chips):
            for a in range(n):
                copy(a, 4 + j, (*chip, 1 - c), me).wait_recv()
        for cp in first + passed:
            cp.wait_send()
        for cp in mine:
            cp.wait()

    hbm = pl.BlockSpec(memory_space=pl.ANY)
    return pl.pallas_call(
        body, name="weight_all_gather", in_specs=[hbm] * n, out_specs=[hbm] * n,
        out_shape=[jax.ShapeDtypeStruct((N_DEV,) + s.shape, s.dtype) for s in shards],
        scratch_shapes=[pltpu.SemaphoreType.DMA((7 * n,)), pltpu.SemaphoreType.DMA((7 * n,)),
                        pltpu.SemaphoreType.DMA((n,))],
    )(*shards)


def _grad_exchange(blocks, small):
    n = len(blocks)

    def body(*refs):
        src_refs, dst_refs = refs[:n + 1], refs[n + 1:2 * n + 2]
        send_sems, recv_sems, local_sems = refs[2 * n + 2:]
        x, y, c = _mesh_pos()
        me = 4 * x + 2 * y + c
        copies = []
        for kk in range(1, N_DEV):
            px = 1 - x if kk & 4 else x
            py = 1 - y if kk & 2 else y
            pc = 1 - c if kk & 1 else c
            peer = 4 * px + 2 * py + pc
            for a in range(n + 1):
                copies.append(pltpu.make_async_remote_copy(
                    src_ref=src_refs[a].at[peer] if a < n else src_refs[a], dst_ref=dst_refs[a].at[me],
                    send_sem=send_sems.at[7 * a + kk - 1], recv_sem=recv_sems.at[7 * a + kk - 1],
                    device_id=(px, py, pc), device_id_type=MESH_ID))
        own = [pltpu.make_async_copy(src_refs[a].at[me] if a < n else src_refs[a], dst_refs[a].at[me],
                                     local_sems.at[a]) for a in range(n + 1)]
        for cp in own + copies:
            cp.start()
        for cp in copies + own:
            cp.wait()

    hbm = pl.BlockSpec(memory_space=pl.ANY)
    return pl.pallas_call(
        body, name="grad_exchange", in_specs=[hbm] * (n + 1), out_specs=[hbm] * (n + 1),
        out_shape=[jax.ShapeDtypeStruct(b.shape, b.dtype) for b in blocks]
        + [jax.ShapeDtypeStruct((N_DEV,) + small.shape, small.dtype)],
        scratch_shapes=[pltpu.SemaphoreType.DMA((7 * (n + 1),)), pltpu.SemaphoreType.DMA((7 * (n + 1),)),
                        pltpu.SemaphoreType.DMA((n + 1,))],
    )(*blocks, small)


def _exchange_copies(src_refs, land_refs, send_sems, recv_sems):
    x, y, c = _mesh_pos()
    me = 4 * x + 2 * y + c
    copies = []
    for kk in range(1, N_DEV):
        px = 1 - x if kk & 4 else x
        py = 1 - y if kk & 2 else y
        pc = 1 - c if kk & 1 else c
        for a, (src, land) in enumerate(zip(src_refs, land_refs)):
            copies.append(pltpu.make_async_remote_copy(
                src_ref=src.at[4 * px + 2 * py + pc], dst_ref=land.at[me],
                send_sem=send_sems.at[7 * a + kk - 1], recv_sem=recv_sems.at[7 * a + kk - 1],
                device_id=(px, py, pc), device_id_type=MESH_ID))
    return copies


def _gather_copies(src_refs, land_refs, send_sems, recv_sems):
    x, y, c = _mesh_pos()
    me = 4 * x + 2 * y + c
    copies = []
    for kk in range(1, N_DEV):
        px = 1 - x if kk & 4 else x
        py = 1 - y if kk & 2 else y
        pc = 1 - c if kk & 1 else c
        for a, (src, land) in enumerate(zip(src_refs, land_refs)):
            copies.append(pltpu.make_async_remote_copy(
                src_ref=src, dst_ref=land.at[me],
                send_sem=send_sems.at[7 * a + kk - 1], recv_sem=recv_sems.at[7 * a + kk - 1],
                device_id=(px, py, pc), device_id_type=MESH_ID))
    return copies


_HBM = pl.BlockSpec(memory_space=pltpu.HBM)
_SEM = pl.BlockSpec(memory_space=pltpu.SEMAPHORE)
_DATAFLOW = pltpu.SideEffectType.DATAFLOW_SIDE_EFFECTING


def _split_start(name, make_copies, sources, land_shapes):
    n = len(sources)

    def body(*refs):
        src_refs, land_refs, send_sems, recv_sems = refs[:n], refs[n:2 * n], refs[2 * n], refs[2 * n + 1]
        for cp in make_copies(src_refs, land_refs, send_sems, recv_sems):
            cp.start()
        token = refs[-1]
        token[...] = jnp.zeros_like(token)

    in_hbm = lambda a: pltpu.with_memory_space_constraint(a, pltpu.HBM)
    hbm_shapes = [pltpu.HBM(s.shape, s.dtype) for s in list(sources) + list(land_shapes)]
    outs = pl.pallas_call(
        body, name=name, in_specs=[_HBM] * (2 * n),
        out_shape=(pltpu.SemaphoreType.DMA((7 * n,)), pltpu.SemaphoreType.DMA((7 * n,)), *hbm_shapes,
                   jax.ShapeDtypeStruct((8, LANES), F32)),
        out_specs=(_SEM, _SEM, *[_HBM] * (2 * n), pl.BlockSpec(memory_space=pltpu.VMEM)),
        input_output_aliases={a: 2 + a for a in range(2 * n)},
        compiler_params=pltpu.CompilerParams(has_side_effects=_DATAFLOW),
    )(*[in_hbm(s) for s in sources], *[in_hbm(lax.empty(s.shape, s.dtype)) for s in land_shapes])
    return outs[0], outs[1], outs[2:2 + n], outs[2 + n:2 + 2 * n], outs[-1]


def _split_wait(name, make_copies, send_sems, recv_sems, src_thru, land_thru, after):
    n = len(src_thru)

    def body(*refs):
        src_refs, land_refs, send_sems, recv_sems = refs[:n], refs[n:2 * n], refs[2 * n], refs[2 * n + 1]
        for cp in make_copies(src_refs, land_refs, send_sems, recv_sems):
            cp.wait_send()
            cp.wait_recv()

    outs = pl.pallas_call(
        body, name=name,
        in_specs=[_HBM] * (2 * n) + [_SEM, _SEM, pl.BlockSpec(memory_space=pl.ANY)],
        out_shape=tuple(pltpu.HBM(b.shape, b.dtype) for b in list(src_thru) + list(land_thru)),
        out_specs=[_HBM] * (2 * n), input_output_aliases={a: a for a in range(2 * n)},
        compiler_params=pltpu.CompilerParams(has_side_effects=_DATAFLOW),
    )(*src_thru, *land_thru, send_sems, recv_sems, after)
    return outs[:n], outs[n:]


def _exchange_start(blocks):
    return _split_start("grad_exchange_start", _exchange_copies, blocks, blocks)


def _exchange_wait(send_sems, recv_sems, src_thru, land_thru, after):
    return _split_wait("grad_exchange_wait", _exchange_copies, send_sems, recv_sems, src_thru, land_thru, after)


def _gather_start(shards):
    lands = [jax.ShapeDtypeStruct((N_DEV,) + s.shape, s.dtype) for s in shards]
    return _split_start("weight_gather_start", _gather_copies, shards, lands)


def _gather_wait(send_sems, recv_sems, src_thru, land_thru, after):
    return _split_wait("weight_gather_wait", _gather_copies, send_sems, recv_sems, src_thru, land_thru, after)


def _pad_flat(parts, rows):
    flat = jnp.concatenate([p.reshape(-1) for p in parts])
    return jnp.pad(flat, (0, rows * LANES - flat.shape[0])).reshape(rows, LANES)


def _rows_for(n_elems, mult=1024):
    rows = -(-n_elems // LANES)
    return -(-rows // mult) * mult


SHARDED = ("meta_tokens", "w_in", "gdn_conv_w", "w_branch_fox", "w_branch_gdn", "w_out", "ffn_w_up", "ffn_conv_w",
           "ffn_w_down")
MATMUL = ("w_in", "w_branch_fox", "w_branch_gdn", "w_out", "ffn_w_up", "ffn_w_down")
EXACT = ("meta_tokens", "gdn_conv_w", "ffn_conv_w")
REPLICATED = ("fgt_bias", "gdn_a_log", "gdn_dt_bias", "gdn_norm_w", "gate_bias", "norm_mix_w", "norm_ffn_w",
              "ffn_conv_b", "norm_final_w")
WEIGHTS = ("meta_tokens", "w_in", "fgt_bias", "gdn_conv_w", "gdn_a_log", "gdn_dt_bias", "gdn_norm_w", "gate_bias",
           "w_branch_fox", "w_branch_gdn", "w_out", "norm_mix_w", "norm_ffn_w", "ffn_w_up", "ffn_conv_w",
           "ffn_conv_b", "ffn_w_down", "norm_final_w")


def _unpack(buf, shapes):
    flat = buf.reshape(-1)
    out, off = [], 0
    for s in shapes:
        n = int(np.prod(s))
        out.append(flat[off:off + n].reshape(s))
        off += n
    return out


def _unpack_gathered(buf, shapes):
    flat = buf.reshape(N_DEV, -1)
    out, off = [], 0
    for s in shapes:
        n = int(np.prod(s))
        out.append(flat[:, off:off + n].reshape((N_DEV,) + tuple(s)))
        off += n
    return out


def _cat_cols(g):
    return g.transpose(1, 0, 2).reshape(g.shape[1], -1)


def _col_blocks(full, width):
    return full.reshape(full.shape[0], N_DEV, width).transpose(1, 0, 2)


def _local_step(x, target, w, early=None, late_weights=None):
    seq = x.shape[0]
    t = _padded_tokens(seq)
    pad = t - N_META - seq
    seg = _seg_matrix()
    zrows = jnp.zeros((pad, D_MODEL), F32)
    h0 = jnp.concatenate([w["meta_tokens"], x, zrows], axis=0)
    tgt = jnp.concatenate([jnp.zeros((N_META, D_MODEL), F32), target, zrows], axis=0)

    w_in = w["w_in"]
    o_f, o_g, o_z, o_b, o_a, o_gate = 1536, 1544, 3080, 3592, 3600, 3608
    w_small = jnp.concatenate([w_in[:, o_f:o_f + 8], w_in[:, o_b:o_b + 8], w_in[:, o_a:o_a + 8],
                               jnp.zeros((D_MODEL, LANES - 24), BF16)], axis=1)
    w_r = jnp.concatenate([w_in[:, :1536], w_in[:, o_g:o_z], w_in[:, o_z:o_b], w_in[:, o_gate:], w_small], axis=1)

    a1 = _rmsnorm_fwd(h0, w["norm_mix_w"])
    fq = _mm(a1, w_r[:, :1536], BF16, "proj_fox")
    xg = _mm(a1, w_r[:, 1536:3584], BF16, "proj_gdn")
    gt = _mm(a1, w_r[:, 3584:5632], BF16, "proj_gates")
    sm = _mm(a1, w_r[:, 5632:], F32, "proj_small")

    lanes_pad = lambda a, lo: jnp.pad(a, ((0, 0), (lo, LANES - lo - a.shape[1])))
    neg_exp_a = -jnp.exp(w["gdn_a_log"])
    pbias = lanes_pad(w["fgt_bias"], 0) + lanes_pad(w["gdn_dt_bias"], 2 * HEADS)
    if late_weights is not None:
        pbias = pbias + late_weights[0][0, 0]
    pscale = lanes_pad(neg_exp_a, 2 * HEADS)
    scal = _gate_fwd(sm, pbias, pscale)
    gct = scal[:, 2 * HEADS:3 * HEADS].T

    qa, ka, va, kat, vat = _fox_prep(fq, scal)
    oa, qb, qbt = _fox_fwd(qa, ka, vat)
    o_fox = _fox_post(oa)

    qh, kh, vh = _gdn_pre_fwd(xg, w["gdn_conv_w"], seg)
    og, s0s = _gdn_fwd(qh, kh, vh, scal, gct)
    norm_w = jnp.tile(w["gdn_norm_w"], (1, HEADS))
    ogn = _gdn_post_fwd(og, xg, norm_w, seg)

    if late_weights is not None:
        w = {**w, **late_weights[1](ogn)}
    yf = _mm(o_fox, w["w_branch_fox"], BF16, "branch_fox")
    yg = _mm(ogn, w["w_branch_gdn"], BF16, "branch_gdn")
    mix = _mix_fwd(yf, yg, gt, w["gate_bias"])
    h1 = _mm(mix, w["w_out"], F32, "out_proj", res=h0)
    a2 = _rmsnorm_fwd(h1, w["norm_ffn_w"])
    up = _mm(a2, w["ffn_w_up"], BF16, "ffn_up")
    act = _ffn_act_fwd(up, w["ffn_conv_w"], w["ffn_conv_b"])
    h2 = _mm(act, w["ffn_w_down"], F32, "ffn_down", res=h1)
    loss, dh2, dh2b, g_final = _final_loss(h2, tgt, w["norm_final_w"].reshape(1, D_MODEL), seq)

    grads = {"norm_final_w": g_final.reshape(D_MODEL)}
    grads["ffn_w_down"] = _mm_tn(act, dh2b, "wgrad_ffn_down")
    dact = _mm(dh2b, w["ffn_w_down"].T, F32, "dgrad_ffn_down")
    dup, g_cw, g_cb = _ffn_act_bwd(up, w["ffn_conv_w"], w["ffn_conv_b"], dact)
    grads["ffn_conv_w"], grads["ffn_conv_b"] = g_cw, g_cb
    grads["ffn_w_up"] = _mm_tn(a2, dup, "wgrad_ffn_up")
    da2 = _mm(dup, w["ffn_w_up"].T, BF16, "dgrad_ffn_up")
    dh1, dh1b, grads["norm_ffn_w"] = _rmsnorm_bwd(h1, da2, w["norm_ffn_w"], dh2)
    grads["w_out"] = _mm_tn(mix, dh1b, "wgrad_out")
    dmix = _mm(dh1b, w["w_out"].T, BF16, "dgrad_out")
    dyf, dyg, dgt, grads["gate_bias"] = _mix_bwd(dmix, yf, yg, gt, w["gate_bias"])
    grads["w_branch_fox"] = _mm_tn(o_fox, dyf, "wgrad_branch_fox")
    grads["w_branch_gdn"] = _mm_tn(ogn, dyg, "wgrad_branch_gdn")
    do_fox = _mm(dyf, w["w_branch_fox"].T, F32, "dgrad_branch_fox")
    dogn = _mm(dyg, w["w_branch_gdn"].T, F32, "dgrad_branch_gdn")

    dog, dz, g_nw = _gdn_post_bwd(og, xg, norm_w, seg, dogn)
    grads["gdn_norm_w"] = g_nw.reshape(HEADS, HEAD_DIM).sum(axis=0)[None]
    dqh, dkh, dvh, dscal_g, dgct = _gdn_bwd(qh, kh, vh, scal, gct, s0s, dog)
    dxg, grads["gdn_conv_w"] = _gdn_pre_bwd(xg, w["gdn_conv_w"], seg, dqh, dkh, dvh)

    doa, doat = _fox_bwd_prep(do_fox, oa)
    dfq, dscal_c = _fox_bwd_post(*_fox_bwd(qb, qbt, ka, kat, va, doa, doat))

    dscal = dscal_c + dscal_g + lanes_pad(dgct.T, 2 * HEADS)
    dsm, dpb, dps = _gate_bwd(sm, pbias, pscale, dscal)
    grads["fgt_bias"] = dpb[:, :HEADS]
    grads["gdn_dt_bias"] = dpb[:, 2 * HEADS:3 * HEADS]
    grads["gdn_a_log"] = dps[:, 2 * HEADS:3 * HEADS] * neg_exp_a

    dproj = jnp.concatenate([dfq, dxg, dz, dgt, dsm], axis=1)
    g_r = _mm_tn(a1, dproj, "wgrad_in")
    grads["w_in"] = jnp.concatenate([g_r[:, :1536], g_r[:, 5632:5640], g_r[:, 1536:3072], g_r[:, 3072:3584],
                                     g_r[:, 5640:5648], g_r[:, 5648:5656], g_r[:, 3584:5632]], axis=1)
    token, handle = early(grads) if early is not None else (jnp.zeros((8, LANES), F32), None)
    w_rt = w_r.T + token[0, 0].astype(BF16)
    da1 = _mm(dproj, w_rt, BF16, "dgrad_in")
    dh0, _, grads["norm_mix_w"] = _rmsnorm_bwd(h0, da1, w["norm_mix_w"], dh1)
    grads["meta_tokens"] = dh0[:N_META]
    return loss, dh0[N_META:N_META + seq], grads, handle


def _shard_pieces(arrs):
    return [arrs[n][0] if arrs[n].ndim == 3 else arrs[n] for n in SHARDED]


def _full_grad_blocks(grads):
    g = grads
    cols = lambda a, wd: _col_blocks(a, wd)
    rows = lambda a: a.reshape(N_DEV, a.shape[0] // N_DEV, a.shape[1])
    return [cols(g["w_in"], IN_WIDTH // N_DEV), cols(g["gdn_conv_w"], 3 * WIDTH // N_DEV),
            cols(g["w_branch_fox"], D_MODEL // N_DEV), cols(g["w_branch_gdn"], D_MODEL // N_DEV), rows(g["w_out"]),
            cols(g["ffn_w_up"], 2 * D_FF // N_DEV), cols(g["ffn_conv_w"], 2 * D_FF // N_DEV), rows(g["ffn_w_down"])]


def kernel(x, meta_tokens, w_in, fgt_bias, gdn_conv_w, gdn_a_log, gdn_dt_bias, gdn_norm_w, gate_bias, w_branch_fox, w_branch_gdn, w_out, norm_mix_w, norm_ffn_w, ffn_w_up, ffn_conv_w, ffn_conv_b, ffn_w_down, norm_final_w, loss_target, m_meta_tokens, m_w_in, m_fgt_bias, m_gdn_conv_w, m_gdn_a_log, m_gdn_dt_bias, m_gdn_norm_w, m_gate_bias, m_w_branch_fox, m_w_branch_gdn, m_w_out, m_norm_mix_w, m_norm_ffn_w, m_ffn_w_up, m_ffn_conv_w, m_ffn_conv_b, m_ffn_w_down, m_norm_final_w, v_meta_tokens, v_w_in, v_fgt_bias, v_gdn_conv_w, v_gdn_a_log, v_gdn_dt_bias, v_gdn_norm_w, v_gate_bias, v_w_branch_fox, v_w_branch_gdn, v_w_out, v_norm_mix_w, v_norm_ffn_w, v_ffn_w_up, v_ffn_conv_w, v_ffn_conv_b, v_ffn_w_down, v_norm_final_w):
    wts = dict(meta_tokens=meta_tokens, w_in=w_in, fgt_bias=fgt_bias, gdn_conv_w=gdn_conv_w, gdn_a_log=gdn_a_log,
               gdn_dt_bias=gdn_dt_bias, gdn_norm_w=gdn_norm_w, gate_bias=gate_bias, w_branch_fox=w_branch_fox,
               w_branch_gdn=w_branch_gdn, w_out=w_out, norm_mix_w=norm_mix_w, norm_ffn_w=norm_ffn_w,
               ffn_w_up=ffn_w_up, ffn_conv_w=ffn_conv_w, ffn_conv_b=ffn_conv_b, ffn_w_down=ffn_w_down,
               norm_final_w=norm_final_w)
    mom = dict(meta_tokens=m_meta_tokens, w_in=m_w_in, fgt_bias=m_fgt_bias, gdn_conv_w=m_gdn_conv_w,
               gdn_a_log=m_gdn_a_log, gdn_dt_bias=m_gdn_dt_bias, gdn_norm_w=m_gdn_norm_w, gate_bias=m_gate_bias,
               w_branch_fox=m_w_branch_fox, w_branch_gdn=m_w_branch_gdn, w_out=m_w_out, norm_mix_w=m_norm_mix_w,
               norm_ffn_w=m_norm_ffn_w, ffn_w_up=m_ffn_w_up, ffn_conv_w=m_ffn_conv_w, ffn_conv_b=m_ffn_conv_b,
               ffn_w_down=m_ffn_w_down, norm_final_w=m_norm_final_w)
    var = dict(meta_tokens=v_meta_tokens, w_in=v_w_in, fgt_bias=v_fgt_bias, gdn_conv_w=v_gdn_conv_w,
               gdn_a_log=v_gdn_a_log, gdn_dt_bias=v_gdn_dt_bias, gdn_norm_w=v_gdn_norm_w, gate_bias=v_gate_bias,
               w_branch_fox=v_w_branch_fox, w_branch_gdn=v_w_branch_gdn, w_out=v_w_out, norm_mix_w=v_norm_mix_w,
               norm_ffn_w=v_norm_ffn_w, ffn_w_up=v_ffn_w_up, ffn_conv_w=v_ffn_conv_w, ffn_conv_b=v_ffn_conv_b,
               ffn_w_down=v_ffn_w_down, norm_final_w=v_norm_final_w)

    sh = dict(zip(SHARDED, _shard_pieces(wts)))
    me = 4 * lax.axis_index("x") + 2 * lax.axis_index("y") + lax.axis_index("c")
    late_names = MATMUL[1:]
    late_sems_send, late_sems_recv, late_src, late_land, late_token = _gather_start(
        [sh[n].astype(BF16) for n in late_names])
    exact_shapes = [sh[n].shape for n in EXACT]
    rows_exact = _rows_for(sum(int(np.prod(s)) for s in exact_shapes), 8)
    g_in, g_exact = _all_gather([sh["w_in"].astype(BF16), _pad_flat([sh[n] for n in EXACT], rows_exact)])
    meta_full, conv_full, fconv_full = (_cat_cols(a) for a in _unpack_gathered(g_exact, exact_shapes))
    full = dict(
        meta_tokens=meta_full, w_in=_cat_cols(g_in), gdn_conv_w=conv_full, ffn_conv_w=fconv_full,
        fgt_bias=fgt_bias, gdn_a_log=gdn_a_log, gdn_dt_bias=gdn_dt_bias, gdn_norm_w=gdn_norm_w, gate_bias=gate_bias,
        norm_mix_w=norm_mix_w, norm_ffn_w=norm_ffn_w, ffn_conv_b=ffn_conv_b, norm_final_w=norm_final_w)

    def fetch_late_weights(after):
        shards, lands = _gather_wait(late_sems_send, late_sems_recv, late_src, late_land, after)
        g_bf, g_bg, g_out, g_up, g_down = (lax.dynamic_update_slice_in_dim(land, s[None], me, 0)
                                           for s, land in zip(shards, lands))
        return dict(w_branch_fox=_cat_cols(g_bf), w_branch_gdn=_cat_cols(g_bg), w_out=g_out.reshape(D_MODEL, D_MODEL),
                    ffn_w_up=_cat_cols(g_up), ffn_w_down=g_down.reshape(D_FF, D_MODEL))

    def start_exchange(grads_so_far):
        blocks = [b.astype(BF16) for b in _full_grad_blocks(grads_so_far)]
        send_sems, recv_sems, src_thru, land_thru, token = _exchange_start(blocks)
        return token, (send_sems, recv_sems, src_thru, land_thru)

    loss, grad_x, grads, handle = _local_step(x[0], loss_target[0], full, early=start_exchange,
                                              late_weights=(late_token, fetch_late_weights))
    sent, landed = _exchange_wait(*handle, after=grad_x)
    own = lambda src, land: lax.dynamic_update_slice_in_dim(land, lax.dynamic_slice_in_dim(src, me, 1, 0), me, 0)
    received = [own(src, land) for src, land in zip(sent, landed)]

    rep_parts = [grads[n] for n in REPLICATED] + [loss[:, :1]]
    rep_shapes = [wts[n].shape for n in REPLICATED]
    rows_small = _rows_for(sum(int(np.prod(p.shape)) for p in rep_parts), 8)
    meta_recv, small_recv = _grad_exchange([_col_blocks(grads["meta_tokens"], LANES).astype(BF16)],
                                           _pad_flat(rep_parts, rows_small))
    received = [meta_recv] + received + [small_recv]

    result = {}
    kinds = ("grad", "delta", "new_m", "new_v")
    for n, recv in zip(SHARDED, received[:-1]):
        outs = _adamw(sh[n], _shard_pieces(mom)[SHARDED.index(n)], _shard_pieces(var)[SHARDED.index(n)], recv,
                      "adamw_" + n)
        for kind, a in zip(kinds, outs):
            result[kind, n] = a.reshape(wts[n].shape)
    rep_w = _pad_flat([wts[n] for n in REPLICATED] + [jnp.zeros((1, 1), F32)], rows_small)
    rep_m = _pad_flat([mom[n] for n in REPLICATED] + [jnp.zeros((1, 1), F32)], rows_small)
    rep_v = _pad_flat([var[n] for n in REPLICATED] + [jnp.ones((1, 1), F32)], rows_small)
    outs_r = _adamw(rep_w, rep_m, rep_v, received[-1], "adamw_replicated")
    for kind, br in zip(kinds, outs_r):
        for n, a in zip(REPLICATED, _unpack(br, rep_shapes)):
            result[kind, n] = a
    n_rep = sum(int(np.prod(s)) for s in rep_shapes)
    total_loss = outs_r[0].reshape(-1)[n_rep]
    out = [total_loss, grad_x[None]]
    for kind in ("grad", "delta", "new_m", "new_v"):
        out += [result[kind, n] for n in WEIGHTS]
    return tuple(out)
```

```python
import jax
import jax.numpy as jnp
import numpy as np
from jax import lax
from jax.experimental import pallas as pl
from jax.experimental.pallas import tpu as pltpu

F32 = jnp.float32
BF16 = jnp.bfloat16

D_MODEL = 1024
N_META = 16
HEADS = 8
HEAD_DIM = 64
WIDTH = HEADS * HEAD_DIM
CHUNK = 64
GDN_CONV = 4
D_FF = 2816
FFN_CONV = 3
IN_WIDTH = 5656
RMS_EPS = 1e-6
NEG = -1e30
AUG = 128
N_DEV = 8
LANES = 128

ADAM_LR = 0.001
ADAM_B1 = 0.9
ADAM_B2 = 0.999
ADAM_EPS = 1e-08
ADAM_WD = 0.01
ADAM_STEP = 10

VMEM_LIMIT = 56 * 1024 * 1024
MM_VMEM_BUDGET = 36 * 1024 * 1024
FFN_LANES = 128
FFN_ROWS = 256
HI = lax.Precision.HIGH
MESH_ID = pl.DeviceIdType.MESH


def _pick(n, cands):
    for c in cands:
        if n % c == 0:
            return c
    raise ValueError(f"no tile for {n} in {cands}")


def _params(*sem):
    return pltpu.CompilerParams(dimension_semantics=sem if sem else None, vmem_limit_bytes=VMEM_LIMIT)


def _padded_tokens(seq):
    t = -(-(N_META + seq) // 128) * 128
    if t > 1280 and t % 640:
        t = -(-t // 640) * 640
    return t


ROW_TILES = (640, 512, 384, 256, 128)


def _rmsnorm_fwd(h, gain):
    t, d = h.shape
    tr = _pick(t, ROW_TILES)

    def body(h_ref, g_ref, o_ref):
        x = h_ref[...]
        r = lax.rsqrt(jnp.mean(x * x, axis=-1, keepdims=True) + RMS_EPS)
        o_ref[...] = (x * r * g_ref[...]).astype(o_ref.dtype)

    return pl.pallas_call(
        body, grid=(t // tr,), name="rmsnorm_fwd",
        in_specs=[pl.BlockSpec((tr, d), lambda i: (i, 0)), pl.BlockSpec((1, d), lambda i: (0, 0))],
        out_specs=pl.BlockSpec((tr, d), lambda i: (i, 0)),
        out_shape=jax.ShapeDtypeStruct((t, d), BF16),
        compiler_params=_params("arbitrary"),
    )(h, gain)


def _rmsnorm_bwd(h, dy, gain, dres):
    t, d = h.shape
    tr = _pick(t, (640, 320, 256, 128))

    def body(h_ref, dy_ref, g_ref, dres_ref, dh_ref, dhb_ref, dg_ref):
        x = h_ref[...]
        dyv = dy_ref[...].astype(F32)
        r = lax.rsqrt(jnp.mean(x * x, axis=-1, keepdims=True) + RMS_EPS)
        gy = dyv * g_ref[...]
        m = jnp.mean(gy * x, axis=-1, keepdims=True)
        dh = dres_ref[...] + r * gy - x * (r * r * r * m)
        dh_ref[...] = dh
        dhb_ref[...] = dh.astype(BF16)

        @pl.when(pl.program_id(0) == 0)
        def _():
            dg_ref[...] = jnp.zeros_like(dg_ref)

        dg_ref[...] += jnp.sum(dyv * x * r, axis=0, keepdims=True)

    row = pl.BlockSpec((tr, d), lambda i: (i, 0))
    vec = pl.BlockSpec((1, d), lambda i: (0, 0))
    return pl.pallas_call(
        body, grid=(t // tr,), name="rmsnorm_bwd",
        in_specs=[row, row, vec, row], out_specs=[row, row, vec],
        out_shape=[jax.ShapeDtypeStruct((t, d), F32), jax.ShapeDtypeStruct((t, d), BF16),
                   jax.ShapeDtypeStruct((1, d), F32)],
        compiler_params=_params("arbitrary"),
    )(h, dy, gain, dres)


def _mm(a, b, out_dtype, name, res=None, norm_gain=None):
    m, k = a.shape
    _, n = b.shape
    tm = _pick(m, ROW_TILES)
    out_bytes = jnp.dtype(out_dtype).itemsize + (4 if res is not None else 0) + (2 if norm_gain is not None else 0)
    fits = lambda tn: 4 * tm * k + 4 * k * tn + 2 * tm * tn * out_bytes <= MM_VMEM_BUDGET
    tn = next(c for c in (n, 2816, 2048, 1536, 1408, 1024, 512, 384, 256, 128) if n % c == 0 and fits(c))
    assert norm_gain is None or tn == n

    def body(*refs):
        refs = list(refs)
        a_ref, b_ref = refs[:2]
        r_ref = refs[2] if res is not None else None
        g_ref = refs[2 + (res is not None)] if norm_gain is not None else None
        o_ref = refs[2 + (res is not None) + (norm_gain is not None)]
        out = jnp.dot(a_ref[...], b_ref[...], preferred_element_type=F32)
        if res is not None:
            out = out + r_ref[...]
        o_ref[...] = out.astype(o_ref.dtype)
        if norm_gain is not None:
            r = lax.rsqrt(jnp.mean(out * out, axis=-1, keepdims=True) + RMS_EPS)
            refs[-1][...] = (out * r * g_ref[...]).astype(BF16)

    in_specs = [pl.BlockSpec((tm, k), lambda i, j: (i, 0)), pl.BlockSpec((k, tn), lambda i, j: (0, j))]
    args = [a, b]
    tile = pl.BlockSpec((tm, tn), lambda i, j: (i, j))
    if res is not None:
        in_specs.append(tile)
        args.append(res)
    out_specs, out_shape = tile, jax.ShapeDtypeStruct((m, n), out_dtype)
    if norm_gain is not None:
        in_specs.append(pl.BlockSpec((1, tn), lambda i, j: (0, j)))
        args.append(norm_gain)
        out_specs, out_shape = [tile, tile], [out_shape, jax.ShapeDtypeStruct((m, n), BF16)]
    return pl.pallas_call(
        body, grid=(m // tm, n // tn), name=name,
        in_specs=in_specs, out_specs=out_specs, out_shape=out_shape,
        compiler_params=_params("parallel", "parallel"),
    )(*args)


def _mm_tn(a, g, name):
    t, k = a.shape
    _, n = g.shape
    tk = _pick(k, (1024, 1408, 512))
    tn = _pick(n, (512, 640, 384, 256, 128))
    tt = next(c for c in (3328, 1280) + ROW_TILES
              if t % c == 0 and 4 * c * (tk + tn) + 8 * tk * tn <= MM_VMEM_BUDGET)
    nt = t // tt

    def body(a_ref, g_ref, o_ref):
        @pl.when(pl.program_id(2) == 0)
        def _():
            o_ref[...] = jnp.zeros_like(o_ref)

        o_ref[...] += lax.dot_general(a_ref[...], g_ref[...], (((0,), (0,)), ((), ())),
                                      preferred_element_type=F32)

    return pl.pallas_call(
        body, grid=(k // tk, n // tn, nt), name=name,
        in_specs=[pl.BlockSpec((tt, tk), lambda i, j, s: (s, i)), pl.BlockSpec((tt, tn), lambda i, j, s: (s, j))],
        out_specs=pl.BlockSpec((tk, tn), lambda i, j, s: (i, j)),
        out_shape=jax.ShapeDtypeStruct((k, n), F32),
        compiler_params=_params("parallel", "parallel", "arbitrary"),
    )(a, g)


def _split3_exact(x):
    def top(v):
        return lax.bitcast_convert_type(lax.bitcast_convert_type(v, jnp.int32) & jnp.int32(-65536), F32)

    hi = top(x)
    r1 = x - hi
    mid = top(r1)
    return hi, mid, r1 - mid


def _pair_head(ref, h, rows):
    x = ref[:, 128 * (h // 2):128 * (h // 2) + 128].astype(F32)
    return pltpu.roll(x, HEAD_DIM, axis=1) if h % 2 else x


def _lanes(rows):
    return lax.broadcasted_iota(jnp.int32, (rows, AUG), 1)


def _fox_prep(fq, scal):
    t = fq.shape[0]
    tt = _pick(t, (640, 256, 128))

    def body(q_ref, k_ref, v_ref, s_ref, qa_ref, ka_ref, va_ref, kt_ref, vt_ref):
        lane = _lanes(tt)
        chi, cmid, clo = _split3_exact(s_ref[...])
        ones = lambda lo: jnp.where((lane >= lo) & (lane < lo + 3), 1.0, 0.0)
        for h in range(HEADS):
            col = lambda a: jnp.broadcast_to(a[:, h:h + 1], (tt, AUG))
            c1, c2, c3 = col(chi), col(cmid), col(clo)
            qx = jnp.where(lane == 64, c1, jnp.where(lane == 65, c2, jnp.where(lane == 66, c3, ones(67))))
            kx = jnp.where(lane == 67, -c1, jnp.where(lane == 68, -c2, jnp.where(lane == 69, -c3, ones(64) + ones(70))))
            qa_ref[h] = jnp.where(lane < HEAD_DIM, _pair_head(q_ref, h, tt) * (HEAD_DIM ** -0.5), qx).astype(BF16)
            k_aug = jnp.where(lane < HEAD_DIM, _pair_head(k_ref, h, tt), kx)
            ka_ref[h] = k_aug.astype(BF16)
            kt_ref[h] = k_aug.T.astype(BF16)
            v_aug = jnp.where(lane < HEAD_DIM, _pair_head(v_ref, h, tt), ones(64))
            va_ref[h] = v_aug.astype(BF16)
            vt_ref[h] = v_aug.T.astype(BF16)

    out = pl.BlockSpec((HEADS, tt, AUG), lambda i: (0, i, 0))
    out_t = pl.BlockSpec((HEADS, AUG, tt), lambda i: (0, 0, i))
    shp = jax.ShapeDtypeStruct((HEADS, t, AUG), BF16)
    shp_t = jax.ShapeDtypeStruct((HEADS, AUG, t), BF16)
    return pl.pallas_call(
        body, grid=(t // tt,), name="fox_prep",
        in_specs=[pl.BlockSpec((tt, WIDTH), lambda i: (i, 0)), pl.BlockSpec((tt, WIDTH), lambda i: (i, 1)),
                  pl.BlockSpec((tt, WIDTH), lambda i: (i, 2)), pl.BlockSpec((tt, LANES), lambda i: (i, 0))],
        out_specs=[out, out, out, out_t, out_t], out_shape=[shp, shp, shp, shp_t, shp_t],
        compiler_params=_params("parallel"),
    )(fq, fq, fq, scal)


def _fox_post(oa):
    t = oa.shape[1]
    tt = _pick(t, (640, 256, 128))

    def body(o_ref, out_ref):
        out_ref[...] = jnp.concatenate([o_ref[h][:, :HEAD_DIM] for h in range(HEADS)], axis=1).astype(BF16)

    return pl.pallas_call(
        body, grid=(t // tt,), name="fox_post",
        in_specs=[pl.BlockSpec((HEADS, tt, AUG), lambda i: (0, i, 0))],
        out_specs=pl.BlockSpec((tt, WIDTH), lambda i: (i, 0)),
        out_shape=jax.ShapeDtypeStruct((t, WIDTH), BF16),
        compiler_params=_params("parallel"),
    )(oa)


def _fox_bwd_prep(do, oa):
    t = do.shape[0]
    tt = _pick(t, (640, 256, 128))

    def body(d_ref, o_ref, out_ref, outt_ref):
        lane = _lanes(tt)
        for h in range(HEADS):
            x = _pair_head(d_ref, h, tt)
            delta = jnp.sum(jnp.where(lane < HEAD_DIM, x * o_ref[h], 0.0), axis=1, keepdims=True)
            hi, mid, lo = _split3_exact(jnp.broadcast_to(-delta, (tt, AUG)))
            ex = jnp.where(lane == 64, hi, jnp.where(lane == 65, mid, jnp.where(lane == 66, lo, 0.0)))
            do_aug = jnp.where(lane < HEAD_DIM, x, ex)
            out_ref[h] = do_aug.astype(BF16)
            outt_ref[h] = do_aug.T.astype(BF16)

    hm = pl.BlockSpec((HEADS, tt, AUG), lambda i: (0, i, 0))
    return pl.pallas_call(
        body, grid=(t // tt,), name="fox_bwd_prep",
        in_specs=[pl.BlockSpec((tt, WIDTH), lambda i: (i, 0)), hm],
        out_specs=[hm, pl.BlockSpec((HEADS, AUG, tt), lambda i: (0, 0, i))],
        out_shape=[jax.ShapeDtypeStruct((HEADS, t, AUG), BF16), jax.ShapeDtypeStruct((HEADS, AUG, t), BF16)],
        compiler_params=_params("parallel"),
    )(do, oa)


def _fox_bwd_post(dqt, dkt, dvt):
    t = dqt.shape[2]
    tt = _pick(t, (640, 256, 128))

    def body(dq_ref, dk_ref, dv_ref, out_ref, dsc_ref):
        lane = _lanes(tt)
        dqs = [dq_ref[h].T for h in range(HEADS)]
        dks = [dk_ref[h].T for h in range(HEADS)]
        heads = lambda xs: jnp.concatenate([x[:, :HEAD_DIM] for x in xs], axis=1)
        out_ref[:, 0:WIDTH] = (heads(dqs) * (HEAD_DIM ** -0.5)).astype(BF16)
        out_ref[:, WIDTH:2 * WIDTH] = heads(dks).astype(BF16)
        out_ref[:, 2 * WIDTH:] = heads([dv_ref[h].T for h in range(HEADS)]).astype(BF16)
        dsc = jnp.zeros((tt, LANES), F32)
        for h in range(HEADS):
            both = jnp.where(lane == HEAD_DIM, dqs[h], 0.0) - jnp.where(lane == HEAD_DIM + 3, dks[h], 0.0)
            dsc = jnp.where(lane == h, jnp.sum(both, axis=1, keepdims=True), dsc)
        dsc_ref[...] = dsc

    hm = pl.BlockSpec((HEADS, AUG, tt), lambda i: (0, 0, i))
    return pl.pallas_call(
        body, grid=(t // tt,), name="fox_bwd_post",
        in_specs=[hm, hm, hm],
        out_specs=[pl.BlockSpec((tt, 3 * WIDTH), lambda i: (i, 0)), pl.BlockSpec((tt, LANES), lambda i: (i, 0))],
        out_shape=[jax.ShapeDtypeStruct((t, 3 * WIDTH), BF16), jax.ShapeDtypeStruct((t, LANES), F32)],
        compiler_params=_params("parallel"),
    )(dqt, dkt, dvt)


def _fox_fwd(qa, ka, vat, tq=None):
    h, t, _ = qa.shape
    tq = tq or _pick(t, ROW_TILES)

    def body(q_ref, k_ref, vt_ref, o_ref, qb_ref, qbt_ref, s_ref):
        i = pl.program_id(1)
        q = q_ref[...]
        krow = lax.broadcasted_iota(jnp.int32, (tq, tq), 0)
        qcol = lax.broadcasted_iota(jnp.int32, (tq, tq), 1)
        rows = lambda j: pl.ds(pl.multiple_of(j * tq, tq), tq)

        def scores(j, slot):
            s_ref[slot] = lax.dot_general(k_ref[rows(j), :], q, (((1,), (1,)), ((), ())), preferred_element_type=F32)

        def update(j, slot, carry, masked):
            m, acc = carry
            s = s_ref[slot]
            if masked:
                s = jnp.where(qcol >= krow, s, NEG)
            m_new = jnp.maximum(m, jnp.max(s, axis=0, keepdims=True))
            p = jnp.exp(s - m_new)
            alpha = jnp.exp(m - m_new)
            return m_new, acc * alpha + jnp.dot(vt_ref[:, rows(j)], p.astype(BF16), preferred_element_type=F32)

        def pair(j, carry):
            scores(j + 1, 1)
            carry = update(j, 0, carry, False)
            scores(j + 2, 0)
            return update(j + 1, 1, carry, False)

        def odd_tail(carry):
            scores(i, 1)
            return update(i, 1, update(i - 1, 0, carry, False), True)

        scores(0, 0)
        carry = (jnp.full((1, tq), NEG, F32), jnp.zeros((AUG, tq), F32))
        carry = lax.fori_loop(0, i // 4, lambda jj, c: pair(4 * jj + 2, pair(4 * jj, c)), carry)
        carry = lax.fori_loop(0, (i % 4) // 2, lambda jj, c: pair(4 * (i // 4), c), carry)
        m, acc = lax.cond(i % 2 == 1, odd_tail, lambda c: update(i, 0, c, True), carry)
        sub = lax.broadcasted_iota(jnp.int32, (AUG, tq), 0)
        l = jnp.sum(jnp.where(sub == HEAD_DIM, acc, 0.0), axis=0, keepdims=True)
        out = jnp.where(sub < HEAD_DIM, acc / l, m + jnp.log(l)).T
        o_ref[...] = out
        lane = lax.broadcasted_iota(jnp.int32, (tq, AUG), 1)
        lse = jnp.broadcast_to(jnp.sum(jnp.where(lane == HEAD_DIM, out, 0.0), axis=1, keepdims=True), (tq, AUG))
        hi, mid, lo = _split3_exact(-lse)
        qb = jnp.where(lane == 70, hi, jnp.where(lane == 71, mid, jnp.where(lane == 72, lo, q.astype(F32))))
        qb_ref[...] = qb.astype(BF16)
        qbt_ref[...] = qb.T.astype(BF16)

    blk = pl.BlockSpec((None, tq, AUG), lambda hh, i: (hh, i, 0))
    return pl.pallas_call(
        body, grid=(h, t // tq), name="fox_fwd",
        in_specs=[blk, pl.BlockSpec((None, t, AUG), lambda hh, i: (hh, 0, 0)),
                  pl.BlockSpec((None, AUG, t), lambda hh, i: (hh, 0, 0))],
        out_specs=[blk, blk, pl.BlockSpec((None, AUG, tq), lambda hh, i: (hh, 0, i))],
        out_shape=[jax.ShapeDtypeStruct((h, t, AUG), F32), jax.ShapeDtypeStruct((h, t, AUG), BF16),
                   jax.ShapeDtypeStruct((h, AUG, t), BF16)],
        scratch_shapes=[pltpu.VMEM((2, tq, tq), F32)],
        compiler_params=_params("parallel", "arbitrary"),
    )(qa, ka, vat)


def _fox_bwd(qb, qbt, ka, kat, va, doa, doat, tq=None):
    h, t, _ = qb.shape
    tq = tq or _pick(t, ROW_TILES)
    nq = t // tq

    def body(q_ref, qt_ref, k_ref, kt_ref, v_ref, do_ref, dot_ref, dqt_ref, dkt_ref, dvt_ref, s_ref, dp_ref):
        j = pl.program_id(1)
        n = nq - j

        @pl.when(j == 0)
        def _():
            dqt_ref[...] = jnp.zeros_like(dqt_ref)

        dkt_ref[...] = jnp.zeros_like(dkt_ref)
        dvt_ref[...] = jnp.zeros_like(dvt_ref)
        kj = k_ref[...]
        ktj = kt_ref[...]
        vj = v_ref[...]
        qrow = lax.broadcasted_iota(jnp.int32, (tq, tq), 0)
        kcol = lax.broadcasted_iota(jnp.int32, (tq, tq), 1)
        rows = lambda i: pl.ds(pl.multiple_of(i * tq, tq), tq)
        nt_dims = (((1,), (1,)), ((), ()))

        def scores(i, slot):
            s_ref[slot] = lax.dot_general(q_ref[rows(i), :], kj, nt_dims, preferred_element_type=F32)
            dp_ref[slot] = lax.dot_general(do_ref[rows(i), :], vj, nt_dims, preferred_element_type=F32)

        def update(i, slot):
            p = jnp.exp(jnp.where((qrow >= kcol) | (i > j), s_ref[slot], NEG))
            ds = (p * dp_ref[slot]).astype(BF16)
            dvt_ref[...] += jnp.dot(dot_ref[:, rows(i)], p.astype(BF16), preferred_element_type=F32)
            dkt_ref[...] += jnp.dot(qt_ref[:, rows(i)], ds, preferred_element_type=F32)
            dqt_ref[:, rows(i)] += lax.dot_general(ktj, ds, nt_dims, preferred_element_type=F32)

        def pair(i0):
            scores(i0 + 1, 1)
            update(i0, 0)
            scores(jnp.minimum(i0 + 2, nq - 1), 0)
            update(i0 + 1, 1)

        def quad(kk, carry):
            pair(j + 4 * kk)
            pair(j + 4 * kk + 2)
            return carry

        def last_pair(kk, carry):
            pair(j + 4 * (n // 4))
            return carry

        scores(j, 0)
        lax.fori_loop(0, n // 4, quad, 0)
        lax.fori_loop(0, (n % 4) // 2, last_pair, 0)

        @pl.when(n % 2 == 1)
        def _():
            update(nq - 1, 0)

    once = pl.Buffered(1)
    full = pl.BlockSpec((None, t, AUG), lambda hh, j: (hh, 0, 0), pipeline_mode=once)
    full_t = pl.BlockSpec((None, AUG, t), lambda hh, j: (hh, 0, 0), pipeline_mode=once)
    blk = pl.BlockSpec((None, tq, AUG), lambda hh, j: (hh, j, 0))
    blk_t = pl.BlockSpec((None, AUG, tq), lambda hh, j: (hh, 0, j))
    shp = jax.ShapeDtypeStruct((h, AUG, t), F32)
    return pl.pallas_call(
        body, grid=(h, nq), name="fox_bwd",
        in_specs=[full, full_t, blk, blk_t, blk, full, full_t],
        out_specs=[pl.BlockSpec((None, AUG, t), lambda hh, j: (hh, 0, 0)), blk_t, blk_t], out_shape=[shp, shp, shp],
        scratch_shapes=[pltpu.VMEM((2, tq, tq), F32), pltpu.VMEM((2, tq, tq), F32)],
        compiler_params=_params("parallel", "arbitrary"),
    )(qb, qbt, ka, kat, va, doa, doat)


def _seg_matrix():
    idx = np.arange(WIDTH) // HEAD_DIM
    return jnp.asarray((idx[:, None] == idx[None, :]).astype(np.float32))


def _segsum(x, e):
    return jnp.dot(x, e, precision=HI, preferred_element_type=F32)


def _silu(x):
    return x * jax.nn.sigmoid(x)


def _silu_grad(x):
    s = jax.nn.sigmoid(x)
    return s * (1.0 + x * (1.0 - s))


def _shift_down(x, prev8, k):
    r = pltpu.roll(x, k, axis=0)
    p = pltpu.roll(prev8, k, axis=0)
    row = lax.broadcasted_iota(jnp.int32, prev8.shape, 0)
    head = jnp.where(row < k, p, r[:8])
    return jnp.concatenate([head, r[8:]], axis=0)


def _shift_up(x, next8, k):
    n = x.shape[0]
    r = pltpu.roll(x, n - k, axis=0)
    p = pltpu.roll(next8, 8 - k, axis=0)
    row = lax.broadcasted_iota(jnp.int32, next8.shape, 0)
    tail = jnp.where(row >= 8 - k, p, r[n - 8:])
    return jnp.concatenate([r[:n - 8], tail], axis=0)


def _causal_conv(x, prev8, w_ref, width, cols=slice(None)):
    y = x * w_ref[width - 1:width, cols]
    for k in range(1, width):
        y = y + _shift_down(x, prev8, k) * w_ref[width - 1 - k:width - k, cols]
    return y


def _causal_conv_bwd(x, prev8, dy, dnext8, w_ref, dw_ref, width, cols=slice(None)):
    dx = dy * w_ref[width - 1:width, cols]
    dw_ref[width - 1:width, cols] += jnp.sum(dy * x, axis=0, keepdims=True)
    for k in range(1, width):
        dx = dx + _shift_up(dy, dnext8, k) * w_ref[width - 1 - k:width - k, cols]
        dw_ref[width - 1 - k:width - k, cols] += jnp.sum(dy * _shift_down(x, prev8, k), axis=0, keepdims=True)
    return dx


HALO = 16


def _prev_spec(tt, width, tile=lambda i: i):
    return pl.BlockSpec((HALO, width), lambda i: (jnp.maximum(tile(i) * (tt // HALO) - 1, 0), 0))


def _prev8(p_ref, cols=slice(None)):
    return p_ref[:, cols].astype(F32)[HALO - 8:]


def _store_heads(ref, x):
    for h in range(HEADS):
        ref[h] = x[:, HEAD_DIM * h:HEAD_DIM * (h + 1)]


def _load_heads(ref):
    return jnp.concatenate([ref[h] for h in range(HEADS)], axis=1)


def _softplus(z):
    return jnp.maximum(z, 0.0) + jnp.log1p(jnp.exp(-jnp.abs(z)))


def _tri_masks(tt):
    r = lax.broadcasted_iota(jnp.int32, (tt, tt), 0)
    c = lax.broadcasted_iota(jnp.int32, (tt, tt), 1)
    same_chunk = lax.shift_right_logical(r, 6) == lax.shift_right_logical(c, 6)
    return r, c, same_chunk


def _gate_fwd(small, pbias, pscale):
    t = small.shape[0]
    tt = _pick(t, (256, 128))

    def body(x_ref, pb_ref, ps_ref, o_ref, carry_ref):
        @pl.when(pl.program_id(0) == 0)
        def _():
            carry_ref[...] = jnp.zeros_like(carry_ref)

        lane = lax.broadcasted_iota(jnp.int32, (tt, LANES), 1)
        z = x_ref[...] + pb_ref[...]
        log_f = jnp.where(lane < HEADS, -_softplus(-z), 0.0)
        g = jnp.where((lane >= 2 * HEADS) & (lane < 3 * HEADS), ps_ref[...] * _softplus(z), 0.0)
        r, c, same_chunk = _tri_masks(tt)
        lower = jnp.where(r >= c, 1.0, 0.0)
        lower_chunk = jnp.where((r >= c) & same_chunk, 1.0, 0.0)
        csum = jnp.dot(lower, log_f, precision=lax.Precision.HIGHEST, preferred_element_type=F32) + carry_ref[...]
        gc = jnp.dot(lower_chunk, g, precision=lax.Precision.HIGHEST, preferred_element_type=F32)
        carry_ref[...] += jnp.sum(log_f, axis=0, keepdims=True)
        o_ref[...] = jnp.where(lane < HEADS, csum, jnp.where(lane < 2 * HEADS, jax.nn.sigmoid(z), gc))

    row = pl.BlockSpec((tt, LANES), lambda i: (i, 0))
    vec = pl.BlockSpec((1, LANES), lambda i: (0, 0))
    return pl.pallas_call(
        body, grid=(t // tt,), name="gate_fwd", in_specs=[row, vec, vec], out_specs=row,
        out_shape=jax.ShapeDtypeStruct((t, LANES), F32),
        scratch_shapes=[pltpu.VMEM((1, LANES), F32)],
        compiler_params=_params("arbitrary"),
    )(small, pbias, pscale)


def _gate_bwd(small, pbias, pscale, dscal):
    t = small.shape[0]
    tt = _pick(t, (256, 128))
    nt = t // tt

    def body(x_ref, pb_ref, ps_ref, d_ref, dx_ref, dpb_ref, dps_ref, carry_ref):
        @pl.when(pl.program_id(0) == 0)
        def _():
            carry_ref[...] = jnp.zeros_like(carry_ref)
            dpb_ref[...] = jnp.zeros_like(dpb_ref)
            dps_ref[...] = jnp.zeros_like(dps_ref)

        lane = lax.broadcasted_iota(jnp.int32, (tt, LANES), 1)
        z = x_ref[...] + pb_ref[...]
        d = d_ref[...]
        dc = jnp.where(lane < HEADS, d, 0.0)
        dbeta = jnp.where((lane >= HEADS) & (lane < 2 * HEADS), d, 0.0)
        dgc = jnp.where((lane >= 2 * HEADS) & (lane < 3 * HEADS), d, 0.0)
        r, c, same_chunk = _tri_masks(tt)
        upper = jnp.where(r <= c, 1.0, 0.0)
        upper_chunk = jnp.where((r <= c) & same_chunk, 1.0, 0.0)
        dlogf = jnp.dot(upper, dc, precision=lax.Precision.HIGHEST, preferred_element_type=F32) + carry_ref[...]
        dg = jnp.dot(upper_chunk, dgc, precision=lax.Precision.HIGHEST, preferred_element_type=F32)
        carry_ref[...] += jnp.sum(dc, axis=0, keepdims=True)
        sg = jax.nn.sigmoid(z)
        dz = dlogf * (1.0 - sg) + dbeta * sg * (1.0 - sg) + dg * ps_ref[...] * sg
        dx_ref[...] = dz.astype(dx_ref.dtype)
        dpb_ref[...] += jnp.sum(dz, axis=0, keepdims=True)
        dps_ref[...] += jnp.sum(dg * _softplus(z), axis=0, keepdims=True)

    row = pl.BlockSpec((tt, LANES), lambda i: (nt - 1 - i, 0))
    vec = pl.BlockSpec((1, LANES), lambda i: (0, 0))
    return pl.pallas_call(
        body, grid=(nt,), name="gate_bwd", in_specs=[row, vec, vec, row], out_specs=[row, vec, vec],
        out_shape=[jax.ShapeDtypeStruct((t, LANES), BF16), jax.ShapeDtypeStruct((1, LANES), F32),
                   jax.ShapeDtypeStruct((1, LANES), F32)],
        scratch_shapes=[pltpu.VMEM((1, LANES), F32)],
        compiler_params=_params("arbitrary"),
    )(small, pbias, pscale, dscal)


def _gdn_pre_fwd(xg, conv_w, seg):
    t = xg.shape[0]
    c3 = 3 * WIDTH
    tt = _pick(t, (640, 256, 128))

    def body(x_ref, p_ref, w_ref, e_ref, q_ref, k_ref, v_ref):
        x = x_ref[...].astype(F32)
        prev = jnp.where(pl.program_id(0) == 0, 0.0, _prev8(p_ref))
        s = _silu(_causal_conv(x, prev, w_ref, GDN_CONV))
        e = e_ref[...]
        q = s[:, :WIDTH]
        k = s[:, WIDTH:2 * WIDTH]
        _store_heads(q_ref, q * lax.rsqrt(_segsum(q * q, e) + RMS_EPS) * (HEAD_DIM ** -0.5))
        _store_heads(k_ref, k * lax.rsqrt(_segsum(k * k, e) + RMS_EPS))
        _store_heads(v_ref, s[:, 2 * WIDTH:])

    out = pl.BlockSpec((HEADS, tt, HEAD_DIM), lambda i: (0, i, 0))
    shp = jax.ShapeDtypeStruct((HEADS, t, HEAD_DIM), F32)
    return pl.pallas_call(
        body, grid=(t // tt,), name="gdn_pre_fwd",
        in_specs=[pl.BlockSpec((tt, c3), lambda i: (i, 0)), _prev_spec(tt, c3),
                  pl.BlockSpec((GDN_CONV, c3), lambda i: (0, 0)), pl.BlockSpec((WIDTH, WIDTH), lambda i: (0, 0))],
        out_specs=[out, out, out], out_shape=[shp, shp, shp],
        compiler_params=_params("arbitrary"),
    )(xg, xg, conv_w, seg)


def _gdn_pre_bwd(xg, conv_w, seg, dqn, dkn, dv):
    t = xg.shape[0]
    c3 = 3 * WIDTH
    tt = _pick(t, (640, 256, 128))
    nt = t // tt

    def body(x_ref, p_ref, w_ref, e_ref, dq_ref, dk_ref, dv_ref, dx_ref, dw_ref, carry_ref):
        step = pl.program_id(0)
        x = x_ref[...].astype(F32)
        e = e_ref[...]
        prev = jnp.where(step == nt - 1, 0.0, _prev8(p_ref))
        y = _causal_conv(x, prev, w_ref, GDN_CONV)
        s = _silu(y)
        q = s[:, :WIDTH]
        k = s[:, WIDTH:2 * WIDTH]
        rq = lax.rsqrt(_segsum(q * q, e) + RMS_EPS)
        rk = lax.rsqrt(_segsum(k * k, e) + RMS_EPS)
        gq = _load_heads(dq_ref) * (HEAD_DIM ** -0.5)
        gk = _load_heads(dk_ref)
        dq = rq * gq - q * (rq * rq * rq) * _segsum(gq * q, e)
        dk = rk * gk - k * (rk * rk * rk) * _segsum(gk * k, e)
        dy = jnp.concatenate([dq, dk, _load_heads(dv_ref)], axis=1) * _silu_grad(y)

        @pl.when(step == 0)
        def _():
            carry_ref[...] = jnp.zeros_like(carry_ref)
            dw_ref[...] = jnp.zeros_like(dw_ref)

        dx = _causal_conv_bwd(x, prev, dy, carry_ref[...], w_ref, dw_ref, GDN_CONV)
        dx_ref[...] = dx.astype(dx_ref.dtype)
        carry_ref[...] = dy[:8]

    rev = lambda i: (nt - 1 - i, 0)
    blk = pl.BlockSpec((HEADS, tt, HEAD_DIM), lambda i: (0, nt - 1 - i, 0))
    return pl.pallas_call(
        body, grid=(nt,), name="gdn_pre_bwd",
        in_specs=[pl.BlockSpec((tt, c3), rev), _prev_spec(tt, c3, lambda i: nt - 1 - i),
                  pl.BlockSpec((GDN_CONV, c3), lambda i: (0, 0)), pl.BlockSpec((WIDTH, WIDTH), lambda i: (0, 0)),
                  blk, blk, blk],
        out_specs=[pl.BlockSpec((tt, c3), rev), pl.BlockSpec((GDN_CONV, c3), lambda i: (0, 0))],
        out_shape=[jax.ShapeDtypeStruct((t, c3), BF16), jax.ShapeDtypeStruct((GDN_CONV, c3), F32)],
        scratch_shapes=[pltpu.VMEM((8, c3), F32)],
        compiler_params=_params("arbitrary"),
    )(xg, xg, conv_w, seg, dqn, dkn, dv)


def _bmm(a, b, ca, cb, precision=None):
    return lax.dot_general(a, b, (((ca,), (cb,)), ((0,), (0,))), precision=precision, preferred_element_type=F32)


def _bf(x):
    return x.astype(BF16)


def _tri_inverse(a, eye):
    x = -a
    tinv = eye + x
    pw = x
    for _ in range(5):
        pb = _bf(pw)
        pw = _bmm(pb, pb, 2, 1)
        tinv = tinv + _bmm(_bf(tinv), _bf(pw), 2, 1)
    resid = eye - _bmm(eye + a, tinv, 2, 1, precision=HI)
    return tinv + _bmm(_bf(tinv), _bf(resid), 2, 1)


def _gdn_intra(q, k, v, bc, gcc, gcr):
    ii = lax.broadcasted_iota(jnp.int32, (CHUNK, CHUNK), 0)
    jj = lax.broadcasted_iota(jnp.int32, (CHUNK, CHUNK), 1)
    tril = (ii >= jj)[None]
    strict = (ii > jj)[None]
    eye = jnp.where(ii == jj, 1.0, 0.0).astype(F32)[None]
    last = (ii == CHUNK - 1)[None]
    dm = jnp.exp(jnp.where(tril, gcc - gcr, NEG))
    gam = jnp.exp(gcc)
    kb = k * bc
    vb = v * bc
    kk = _bmm(_bf(kb), _bf(k), 2, 2)
    a = jnp.where(strict, kk * dm, 0.0)
    tinv = _tri_inverse(a, eye)
    uw = _bmm(tinv, jnp.concatenate([vb, kb * gam], axis=2), 2, 1, precision=HI)
    u, wk = uw[:, :, :HEAD_DIM], uw[:, :, HEAD_DIM:]
    qk = _bmm(_bf(q), _bf(k), 2, 2)
    p = jnp.where(tril, qk * dm, 0.0)
    gl = jnp.sum(jnp.where(last, gcc, 0.0), axis=1, keepdims=True)
    edec = jnp.exp(gl - gcc)
    return dict(tril=tril, strict=strict, dm=dm, gam=gam, kb=kb, kk=kk, a=a, tinv=tinv, u=u, wk=wk, qk=qk, p=p,
                qg=q * gam, kt=k * edec, edec=edec, gaml=jnp.exp(gl), last=last)


def _gate_tiles(sc, gct, nb):
    rows = nb * CHUNK
    cols = lambda lane0: jnp.stack([jnp.broadcast_to(sc[:, lane0 + h:lane0 + h + 1], (rows, HEAD_DIM))
                                    for h in range(HEADS)], axis=0).reshape(HEADS * nb, CHUNK, HEAD_DIM)
    gcr = jnp.stack([jnp.broadcast_to(gct[h:h + 1, n * CHUNK:(n + 1) * CHUNK], (CHUNK, CHUNK))
                     for h in range(HEADS) for n in range(nb)], axis=0)
    return cols(HEADS), cols(2 * HEADS), gcr


def _gdn_fwd(q, k, v, scal, gct, nb=None):
    h, t, dh = q.shape
    nc = t // CHUNK
    nb = nb or _pick(nc, (4, 2))
    bsz = h * nb

    def body(q_ref, k_ref, v_ref, sc_ref, gt_ref, o_ref, s0_ref, state_ref):
        @pl.when(pl.program_id(0) == 0)
        def _():
            state_ref[...] = jnp.zeros_like(state_ref)

        ld = lambda r: r[...].reshape(bsz, CHUNK, dh)
        bc, gcc, gcr = _gate_tiles(sc_ref[...], gt_ref[...], nb)
        z = _gdn_intra(ld(q_ref), ld(k_ref), ld(v_ref), bc, gcc, gcr)
        per = lambda x: x.reshape((h, nb) + x.shape[1:])
        u, wk, p, qg, kt, gaml = (per(z[n]) for n in ("u", "wk", "p", "qg", "kt", "gaml"))
        s = state_ref[...]
        for n in range(nb):
            s0_ref[:, n] = s
            sb = _bf(s)
            vn = u[:, n] - _bmm(_bf(wk[:, n]), sb, 2, 1)
            vnb = _bf(vn)
            o_ref[:, n * CHUNK:(n + 1) * CHUNK, :] = _bmm(_bf(jnp.concatenate([qg[:, n], p[:, n]], axis=2)),
                                                          jnp.concatenate([sb, vnb], axis=1), 2, 1)
            s = s * gaml[:, n] + _bmm(_bf(kt[:, n]), vnb, 1, 1)
        state_ref[...] = s

    blk = pl.BlockSpec((h, nb * CHUNK, dh), lambda i: (0, i, 0))
    return pl.pallas_call(
        body, grid=(nc // nb,), name="gdn_fwd",
        in_specs=[blk] * 3 + [pl.BlockSpec((nb * CHUNK, LANES), lambda i: (i, 0)),
                              pl.BlockSpec((h, nb * CHUNK), lambda i: (0, i))],
        out_specs=[blk, pl.BlockSpec((h, nb, dh, dh), lambda i: (0, i, 0, 0))],
        out_shape=[jax.ShapeDtypeStruct((h, t, dh), F32), jax.ShapeDtypeStruct((h, nc, dh, dh), F32)],
        scratch_shapes=[pltpu.VMEM((h, dh, dh), F32)],
        compiler_params=_params("arbitrary"),
    )(q, k, v, scal, gct)


def _gdn_bwd(q, k, v, scal, gct, s0s, do, nb=None):
    h, t, dh = q.shape
    nc = t // CHUNK
    nb = nb or _pick(nc, (4, 2))
    bsz = h * nb
    ng = nc // nb
    rows = nb * CHUNK

    def body(q_ref, k_ref, v_ref, sc_ref, gt_ref, s0_ref, do_ref,
             dq_ref, dk_ref, dv_ref, dsc_ref, dgt_ref, ds_ref):
        @pl.when(pl.program_id(0) == 0)
        def _():
            ds_ref[...] = jnp.zeros_like(ds_ref)

        ld = lambda r: r[...].reshape(bsz, CHUNK, dh)
        q, k, v = ld(q_ref), ld(k_ref), ld(v_ref)
        bc, gcc, gcr = _gate_tiles(sc_ref[...], gt_ref[...], nb)
        z = _gdn_intra(q, k, v, bc, gcc, gcr)
        per = lambda x: x.reshape((h, nb) + x.shape[1:])
        u, wk, p, qg, kt, gaml = (per(z[n]) for n in ("u", "wk", "p", "qg", "kt", "gaml"))
        dout = per(ld(do_ref))
        ds = ds_ref[...]
        d_u, d_wk, d_p, d_qg, d_kt, d_gaml = ([None] * nb for _ in range(6))
        for n in reversed(range(nb)):
            s0 = s0_ref[:, n]
            s0b, dsb, dob = _bf(s0), _bf(ds), _bf(dout[:, n])
            wkb, qgb = _bf(wk[:, n]), _bf(qg[:, n])
            vn = u[:, n] - _bmm(wkb, s0b, 2, 1)
            dvn = _bmm(_bf(p[:, n]), dob, 1, 1) + _bmm(_bf(kt[:, n]), dsb, 2, 1)
            dvnb = _bf(dvn)
            d_u[n] = dvn
            vnb = _bf(vn)
            dpq = _bmm(dob, jnp.concatenate([vnb, s0b], axis=1), 2, 2)
            d_p[n], d_qg[n] = dpq[:, :, :CHUNK], dpq[:, :, CHUNK:]
            d_kt[n] = _bmm(vnb, dsb, 2, 2)
            d_gaml[n] = jnp.sum(s0 * ds, axis=1, keepdims=True)
            d_wk[n] = -_bmm(dvnb, s0b, 2, 2)
            ds = gaml[:, n] * ds + _bmm(jnp.concatenate([qgb, -wkb], axis=1), jnp.concatenate([dob, dvnb], axis=1), 1, 1)
        ds_ref[...] = ds

        flat = lambda xs: jnp.stack(xs, axis=1).reshape((bsz,) + xs[0].shape[1:])
        d_u, d_wk, d_p, d_qg, d_kt, d_gaml = (flat(x) for x in (d_u, d_wk, d_p, d_qg, d_kt, d_gaml))
        tinv, gam, kb, dm = z["tinv"], z["gam"], z["kb"], z["dm"]
        dr = _bmm(tinv, jnp.concatenate([d_u, d_wk], axis=2), 1, 1, precision=HI)
        drv, drk = dr[:, :, :HEAD_DIM], dr[:, :, HEAD_DIM:]
        da = -_bmm(_bf(dr), _bf(jnp.concatenate([z["u"], z["wk"]], axis=2)), 2, 2)
        da = jnp.where(z["strict"], da, 0.0)
        d_p = jnp.where(z["tril"], d_p, 0.0)
        dkk = _bf(da * dm)
        dqk = _bf(d_p * dm)
        dkb = _bmm(dkk, _bf(k), 2, 1) + drk * gam
        dk = (_bmm(jnp.concatenate([dkk, dqk], axis=1), _bf(jnp.concatenate([kb, q], axis=1)), 1, 1)
              + dkb * bc + d_kt * z["edec"])
        dq = _bmm(dqk, _bf(k), 2, 1) + d_qg * gam
        mm = da * z["a"] + d_p * z["p"]
        dkt_kt = d_kt * z["kt"]
        dgl = jnp.sum(dkt_kt, axis=1, keepdims=True) + d_gaml * z["gaml"]
        dgc = mm + d_qg * z["qg"] + drk * kb * gam - dkt_kt + jnp.where(z["last"], dgl, 0.0)
        dq_ref[...] = dq.reshape(h, rows, dh)
        dk_ref[...] = dk.reshape(h, rows, dh)
        dv_ref[...] = (drv * bc).reshape(h, rows, dh)
        dbeta = (dkb * k + drv * v).reshape(h, rows, dh)
        dgc = dgc.reshape(h, rows, dh)
        lane = lax.broadcasted_iota(jnp.int32, (rows, LANES), 1)
        dsc = jnp.zeros((rows, LANES), F32)
        for hh in range(h):
            dsc = jnp.where(lane == HEADS + hh, jnp.sum(dbeta[hh], axis=1, keepdims=True), dsc)
            dsc = jnp.where(lane == 2 * HEADS + hh, jnp.sum(dgc[hh], axis=1, keepdims=True), dsc)
        dsc_ref[...] = dsc
        dgr = -jnp.sum(mm, axis=1, keepdims=True)
        for hh in range(h):
            for n in range(nb):
                dgt_ref[hh:hh + 1, n * CHUNK:(n + 1) * CHUNK] = dgr[hh * nb + n]

    blk = pl.BlockSpec((h, rows, dh), lambda i: (0, ng - 1 - i, 0))
    shp = jax.ShapeDtypeStruct((h, t, dh), F32)
    sc_spec = pl.BlockSpec((rows, LANES), lambda i: (ng - 1 - i, 0))
    gt_spec = pl.BlockSpec((h, rows), lambda i: (0, ng - 1 - i))
    return pl.pallas_call(
        body, grid=(ng,), name="gdn_bwd",
        in_specs=[blk] * 3 + [sc_spec, gt_spec, pl.BlockSpec((h, nb, dh, dh), lambda i: (0, ng - 1 - i, 0, 0)), blk],
        out_specs=[blk] * 3 + [sc_spec, gt_spec],
        out_shape=[shp] * 3 + [jax.ShapeDtypeStruct((t, LANES), F32), jax.ShapeDtypeStruct((h, t), F32)],
        scratch_shapes=[pltpu.VMEM((h, dh, dh), F32)],
        compiler_params=_params("arbitrary"),
    )(q, k, v, scal, gct, s0s, do)


def _gdn_post_fwd(o, xg, gain, seg):
    t = o.shape[1]
    tt = _pick(t, (640, 320, 256, 128))

    def body(o_ref, z_ref, g_ref, e_ref, y_ref):
        x = _load_heads(o_ref)
        r = lax.rsqrt(_segsum(x * x, e_ref[...]) * (1.0 / HEAD_DIM) + RMS_EPS)
        y_ref[...] = (x * r * g_ref[...] * _silu(z_ref[...].astype(F32))).astype(y_ref.dtype)

    return pl.pallas_call(
        body, grid=(t // tt,), name="gdn_post_fwd",
        in_specs=[pl.BlockSpec((HEADS, tt, HEAD_DIM), lambda i: (0, i, 0)), pl.BlockSpec((tt, WIDTH), lambda i: (i, 3)),
                  pl.BlockSpec((1, WIDTH), lambda i: (0, 0)), pl.BlockSpec((WIDTH, WIDTH), lambda i: (0, 0))],
        out_specs=pl.BlockSpec((tt, WIDTH), lambda i: (i, 0)),
        out_shape=jax.ShapeDtypeStruct((t, WIDTH), BF16),
        compiler_params=_params("arbitrary"),
    )(o, xg, gain, seg)


def _gdn_post_bwd(o, xg, gain, seg, dy):
    t = o.shape[1]
    tt = _pick(t, (640, 320, 256, 128))

    def body(o_ref, z_ref, g_ref, e_ref, dy_ref, do_ref, dz_ref, dg_ref):
        x = _load_heads(o_ref)
        zz = z_ref[...].astype(F32)
        e = e_ref[...]
        gain_v = g_ref[...]
        d = dy_ref[...]
        r = lax.rsqrt(_segsum(x * x, e) * (1.0 / HEAD_DIM) + RMS_EPS)
        xr = x * r
        don = d * _silu(zz)
        dz_ref[...] = (d * xr * gain_v * _silu_grad(zz)).astype(dz_ref.dtype)
        gy = don * gain_v
        _store_heads(do_ref, r * gy - xr * (r * r) * (_segsum(gy * x, e) * (1.0 / HEAD_DIM)))

        @pl.when(pl.program_id(0) == 0)
        def _():
            dg_ref[...] = jnp.zeros_like(dg_ref)

        dg_ref[...] += jnp.sum(don * xr, axis=0, keepdims=True)

    row = pl.BlockSpec((tt, WIDTH), lambda i: (i, 0))
    vec = pl.BlockSpec((1, WIDTH), lambda i: (0, 0))
    hm = pl.BlockSpec((HEADS, tt, HEAD_DIM), lambda i: (0, i, 0))
    return pl.pallas_call(
        body, grid=(t // tt,), name="gdn_post_bwd",
        in_specs=[hm, pl.BlockSpec((tt, WIDTH), lambda i: (i, 3)), vec,
                  pl.BlockSpec((WIDTH, WIDTH), lambda i: (0, 0)), row],
        out_specs=[hm, row, vec],
        out_shape=[jax.ShapeDtypeStruct((HEADS, t, HEAD_DIM), F32), jax.ShapeDtypeStruct((t, WIDTH), BF16),
                   jax.ShapeDtypeStruct((1, WIDTH), F32)],
        compiler_params=_params("arbitrary"),
    )(o, xg, gain, seg, dy)


def _mix_fwd(yf, yg, gates, bias):
    t, d = yf.shape
    tt = _pick(t, (640, 320, 256, 128))

    def body(yf_ref, yg_ref, g1_ref, g2_ref, b1_ref, b2_ref, o_ref):
        g1 = jax.nn.sigmoid(g1_ref[...].astype(F32) + b1_ref[...])
        g2 = jax.nn.sigmoid(g2_ref[...].astype(F32) + b2_ref[...])
        o_ref[...] = (g1 * yf_ref[...].astype(F32) + g2 * yg_ref[...].astype(F32)).astype(o_ref.dtype)

    row = pl.BlockSpec((tt, d), lambda i: (i, 0))
    return pl.pallas_call(
        body, grid=(t // tt,), name="mix_fwd",
        in_specs=[row, row, row, pl.BlockSpec((tt, d), lambda i: (i, 1)),
                  pl.BlockSpec((1, d), lambda i: (0, 0)), pl.BlockSpec((1, d), lambda i: (0, 1))],
        out_specs=row, out_shape=jax.ShapeDtypeStruct((t, d), BF16),
        compiler_params=_params("arbitrary"),
    )(yf, yg, gates, gates, bias, bias)


def _mix_bwd(dmix, yf, yg, gates, bias):
    t, d = yf.shape
    tt = _pick(t, (640, 320, 256, 128))

    def body(dm_ref, yf_ref, yg_ref, g1_ref, g2_ref, b1_ref, b2_ref, dyf_ref, dyg_ref, dg_ref, db_ref):
        dm = dm_ref[...].astype(F32)
        g1 = jax.nn.sigmoid(g1_ref[...].astype(F32) + b1_ref[...])
        g2 = jax.nn.sigmoid(g2_ref[...].astype(F32) + b2_ref[...])
        dyf_ref[...] = (dm * g1).astype(BF16)
        dyg_ref[...] = (dm * g2).astype(BF16)
        dgate = jnp.concatenate([dm * yf_ref[...].astype(F32) * g1 * (1.0 - g1),
                                 dm * yg_ref[...].astype(F32) * g2 * (1.0 - g2)], axis=1)
        dg_ref[...] = dgate.astype(BF16)

        @pl.when(pl.program_id(0) == 0)
        def _():
            db_ref[...] = jnp.zeros_like(db_ref)

        db_ref[...] += jnp.sum(dgate, axis=0, keepdims=True)

    row = pl.BlockSpec((tt, d), lambda i: (i, 0))
    wide = pl.BlockSpec((tt, 2 * d), lambda i: (i, 0))
    return pl.pallas_call(
        body, grid=(t // tt,), name="mix_bwd",
        in_specs=[row, row, row, row, pl.BlockSpec((tt, d), lambda i: (i, 1)),
                  pl.BlockSpec((1, d), lambda i: (0, 0)), pl.BlockSpec((1, d), lambda i: (0, 1))],
        out_specs=[row, row, wide, pl.BlockSpec((1, 2 * d), lambda i: (0, 0))],
        out_shape=[jax.ShapeDtypeStruct((t, d), BF16), jax.ShapeDtypeStruct((t, d), BF16),
                   jax.ShapeDtypeStruct((t, 2 * d), BF16), jax.ShapeDtypeStruct((1, 2 * d), F32)],
        compiler_params=_params("arbitrary"),
    )(dmix, yf, yg, gates, gates, bias, bias)


def _ffn_act_fwd(up, conv_w, conv_b):
    t, c = up.shape
    tt = FFN_ROWS

    def body(x_ref, p_ref, w_ref, b_ref, o_ref):
        first = pl.program_id(0) == 0

        def conv(cols):
            prev = jnp.where(first, 0.0, _prev8(p_ref, cols))
            return _causal_conv(x_ref[:, cols].astype(F32), prev, w_ref, FFN_CONV, cols) + b_ref[:, cols]

        for lo in range(0, D_FF, FFN_LANES):
            gate = conv(slice(lo, lo + FFN_LANES))
            val = conv(slice(D_FF + lo, D_FF + lo + FFN_LANES))
            o_ref[:, lo:lo + FFN_LANES] = (_silu(gate) * val).astype(o_ref.dtype)

    return pl.pallas_call(
        body, grid=(t // tt,), name="ffn_act_fwd",
        in_specs=[pl.BlockSpec((tt, c), lambda i: (i, 0)), _prev_spec(tt, c),
                  pl.BlockSpec((FFN_CONV, c), lambda i: (0, 0)), pl.BlockSpec((1, c), lambda i: (0, 0))],
        out_specs=pl.BlockSpec((tt, D_FF), lambda i: (i, 0)),
        out_shape=jax.ShapeDtypeStruct((t, D_FF), BF16),
        compiler_params=_params("arbitrary"),
    )(up, up, conv_w, conv_b)


def _ffn_act_bwd(up, conv_w, conv_b, dact):
    t, c = up.shape
    tt = FFN_ROWS
    nt = t // tt

    def body(x_ref, p_ref, w_ref, b_ref, da_ref, dx_ref, dw_ref, db_ref, carry_ref):
        step = pl.program_id(0)

        @pl.when(step == 0)
        def _():
            carry_ref[...] = jnp.zeros_like(carry_ref)
            dw_ref[...] = jnp.zeros_like(dw_ref)
            db_ref[...] = jnp.zeros_like(db_ref)

        def conv(cols):
            x = x_ref[:, cols].astype(F32)
            prev = jnp.where(step == nt - 1, 0.0, _prev8(p_ref, cols))
            return x, prev, _causal_conv(x, prev, w_ref, FFN_CONV, cols) + b_ref[:, cols]

        def back(cols, x, prev, du):
            dx = _causal_conv_bwd(x, prev, du, carry_ref[:, cols], w_ref, dw_ref, FFN_CONV, cols)
            dx_ref[:, cols] = dx.astype(dx_ref.dtype)
            db_ref[:, cols] += jnp.sum(du, axis=0, keepdims=True)
            carry_ref[:, cols] = du[:8]

        for lo in range(0, D_FF, FFN_LANES):
            gcols, vcols = slice(lo, lo + FFN_LANES), slice(D_FF + lo, D_FF + lo + FFN_LANES)
            xg, pg, gate = conv(gcols)
            xv, pv, val = conv(vcols)
            da = da_ref[:, gcols]
            back(gcols, xg, pg, da * val * _silu_grad(gate))
            back(vcols, xv, pv, da * _silu(gate))

    rev = lambda i: (nt - 1 - i, 0)
    return pl.pallas_call(
        body, grid=(nt,), name="ffn_act_bwd",
        in_specs=[pl.BlockSpec((tt, c), rev),
                  _prev_spec(tt, c, lambda i: nt - 1 - i),
                  pl.BlockSpec((FFN_CONV, c), lambda i: (0, 0)), pl.BlockSpec((1, c), lambda i: (0, 0)),
                  pl.BlockSpec((tt, D_FF), rev)],
        out_specs=[pl.BlockSpec((tt, c), rev), pl.BlockSpec((FFN_CONV, c), lambda i: (0, 0)),
                   pl.BlockSpec((1, c), lambda i: (0, 0))],
        out_shape=[jax.ShapeDtypeStruct((t, c), BF16), jax.ShapeDtypeStruct((FFN_CONV, c), F32),
                   jax.ShapeDtypeStruct((1, c), F32)],
        scratch_shapes=[pltpu.VMEM((8, c), F32)],
        compiler_params=_params("arbitrary"),
    )(up, up, conv_w, conv_b, dact)


def _final_loss(h2, target, gain, seq):
    t, d = h2.shape
    tr = _pick(t, (640, 320, 256, 128))

    def body(h_ref, t_ref, g_ref, loss_ref, dh_ref, dhb_ref, dg_ref):
        i = pl.program_id(0)
        x = h_ref[...]
        gain_v = g_ref[...]
        r = lax.rsqrt(jnp.mean(x * x, axis=-1, keepdims=True) + RMS_EPS)
        xr = x * r
        rows = i * tr + lax.broadcasted_iota(jnp.int32, (tr, 1), 0)
        real = (rows >= N_META) & (rows < N_META + seq)
        err = jnp.where(real, xr * gain_v - t_ref[...], 0.0)
        dy = err * (1.0 / d)
        gy = dy * gain_v
        dh = r * (gy - xr * jnp.mean(gy * xr, axis=-1, keepdims=True))
        dh_ref[...] = dh
        dhb_ref[...] = dh.astype(BF16)

        @pl.when(i == 0)
        def _():
            loss_ref[...] = jnp.zeros_like(loss_ref)
            dg_ref[...] = jnp.zeros_like(dg_ref)

        part = jnp.sum(jnp.sum(err * err, axis=-1, keepdims=True), axis=0, keepdims=True)
        loss_ref[...] += jnp.broadcast_to(part * (0.5 / d), loss_ref.shape)
        dg_ref[...] += jnp.sum(dy * xr, axis=0, keepdims=True)

    row = pl.BlockSpec((tr, d), lambda i: (i, 0))
    vec = pl.BlockSpec((1, d), lambda i: (0, 0))
    return pl.pallas_call(
        body, grid=(t // tr,), name="final_loss",
        in_specs=[row, row, vec],
        out_specs=[pl.BlockSpec((1, LANES), lambda i: (0, 0)), row, row, vec],
        out_shape=[jax.ShapeDtypeStruct((1, LANES), F32), jax.ShapeDtypeStruct((t, d), F32),
                   jax.ShapeDtypeStruct((t, d), BF16), jax.ShapeDtypeStruct((1, d), F32)],
        compiler_params=_params("arbitrary"),
    )(h2, target, gain)


ADAM_TILE_BYTES = 1 << 20


def _adamw(w, m, v, grecv, name):
    r, cols = w.shape
    tr = r
    if r * cols * 4 > ADAM_TILE_BYTES:
        tr = max(d for d in range(8, r + 1, 8) if r % d == 0 and d * cols * 4 <= ADAM_TILE_BYTES)

    def body(w_ref, m_ref, v_ref, g_ref, go_ref, d_ref, mo_ref, vo_ref):
        g = g_ref[0].astype(F32)
        for s in range(1, N_DEV):
            g = g + g_ref[s].astype(F32)
        wv = w_ref[...]
        mn = ADAM_B1 * m_ref[...] + (1.0 - ADAM_B1) * g
        vn = ADAM_B2 * v_ref[...] + (1.0 - ADAM_B2) * (g * g)
        m_hat = mn / (1.0 - ADAM_B1 ** ADAM_STEP)
        v_hat = vn / (1.0 - ADAM_B2 ** ADAM_STEP)
        go_ref[...] = g
        d_ref[...] = -ADAM_LR * (m_hat / (jnp.sqrt(v_hat) + ADAM_EPS) + ADAM_WD * wv)
        mo_ref[...] = mn
        vo_ref[...] = vn

    row = pl.BlockSpec((tr, cols), lambda i: (i, 0))
    shp = jax.ShapeDtypeStruct((r, cols), F32)
    return pl.pallas_call(
        body, grid=(r // tr,), name=name,
        in_specs=[row, row, row, pl.BlockSpec((N_DEV, tr, cols), lambda i: (0, i, 0))],
        out_specs=[row] * 4, out_shape=[shp] * 4,
        compiler_params=_params("parallel"),
    )(w, m, v, grecv)


def _mesh_pos():
    return lax.axis_index("x"), lax.axis_index("y"), lax.axis_index("c")


def _all_gather(shards):
    n = len(shards)

    def body(*refs):
        x_refs, out_refs = refs[:n], refs[n:2 * n]
        send_sems, recv_sems, local_sems = refs[2 * n:]
        x, y, c = _mesh_pos()
        me, sibling = (x, y, c), (x, y, 1 - c)
        chips = [(1 - x, y), (x, 1 - y), (1 - x, 1 - y)]

        def slot(a, px, py, pc):
            return out_refs[a].at[4 * px + 2 * py + pc]

        def copy(a, kk, block, to, src=None):
            return pltpu.make_async_remote_copy(
                src_ref=slot(a, *block) if src is None else src, dst_ref=slot(a, *block),
                send_sem=send_sems.at[7 * a + kk], recv_sem=recv_sems.at[7 * a + kk],
                device_id=to, device_id_type=MESH_ID)

        mine = [pltpu.make_async_copy(x_refs[a], slot(a, *me), local_sems.at[a]) for a in range(n)]
        first = []
        for a in range(n):
            first.append(copy(a, 0, me, sibling, src=x_refs[a]))
            first += [copy(a, 1 + j, me, (*chip, c), src=x_refs[a]) for j, chip in enumerate(chips)]
        for cp in mine + first:
            cp.start()
        passed = []
        for j, chip in enumerate(chips):
            for a in range(n):
                copy(a, 1 + j, (*chip, c), me).wait_recv()
                passed.append(copy(a, 4 + j, (*chip, c), sibling))
                passed[-1].start()
        for a in range(n):
            copy(a, 0, sibling, me).wait_recv()
        for j, chip in enumerate(chips):
            for a in range(n):
                copy(a, 4 + j, (*chip, 1 - c), me).wait_recv()
        for cp in first + passed:
            cp.wait_send()
        for cp in mine:
            cp.wait()

    hbm = pl.BlockSpec(memory_space=pl.ANY)
    return pl.pallas_call(
        body, name="weight_all_gather", in_specs=[hbm] * n, out_specs=[hbm] * n,
        out_shape=[jax.ShapeDtypeStruct((N_DEV,) + s.shape, s.dtype) for s in shards],
        scratch_shapes=[pltpu.SemaphoreType.DMA((7 * n,)), pltpu.SemaphoreType.DMA((7 * n,)),
                        pltpu.SemaphoreType.DMA((n,))],
    )(*shards)


def _grad_exchange(blocks, small):
    n = len(blocks)

    def body(*refs):
        src_refs, dst_refs = refs[:n + 1], refs[n + 1:2 * n + 2]
        send_sems, recv_sems, local_sems = refs[2 * n + 2:]
        x, y, c = _mesh_pos()
        me = 4 * x + 2 * y + c
        copies = []
        for kk in range(1, N_DEV):
            px = 1 - x if kk & 4 else x
            py = 1 - y if kk & 2 else y
            pc = 1 - c if kk & 1 else c
            peer = 4 * px + 2 * py + pc
            for a in range(n + 1):
                copies.append(pltpu.make_async_remote_copy(
                    src_ref=src_refs[a].at[peer] if a < n else src_refs[a], dst_ref=dst_refs[a].at[me],
                    send_sem=send_sems.at[7 * a + kk - 1], recv_sem=recv_sems.at[7 * a + kk - 1],
                    device_id=(px, py, pc), device_id_type=MESH_ID))
        own = [pltpu.make_async_copy(src_refs[a].at[me] if a < n else src_refs[a], dst_refs[a].at[me],
                                     local_sems.at[a]) for a in range(n + 1)]
        for cp in own + copies:
            cp.start()
        for cp in copies + own:
            cp.wait()

    hbm = pl.BlockSpec(memory_space=pl.ANY)
    return pl.pallas_call(
        body, name="grad_exchange", in_specs=[hbm] * (n + 1), out_specs=[hbm] * (n + 1),
        out_shape=[jax.ShapeDtypeStruct(b.shape, b.dtype) for b in blocks]
        + [jax.ShapeDtypeStruct((N_DEV,) + small.shape, small.dtype)],
        scratch_shapes=[pltpu.SemaphoreType.DMA((7 * (n + 1),)), pltpu.SemaphoreType.DMA((7 * (n + 1),)),
                        pltpu.SemaphoreType.DMA((n + 1,))],
    )(*blocks, small)


def _exchange_copies(src_refs, land_refs, send_sems, recv_sems):
    x, y, c = _mesh_pos()
    me = 4 * x + 2 * y + c
    copies = []
    for kk in range(1, N_DEV):
        px = 1 - x if kk & 4 else x
        py = 1 - y if kk & 2 else y
        pc = 1 - c if kk & 1 else c
        for a, (src, land) in enumerate(zip(src_refs, land_refs)):
            copies.append(pltpu.make_async_remote_copy(
                src_ref=src.at[4 * px + 2 * py + pc], dst_ref=land.at[me],
                send_sem=send_sems.at[7 * a + kk - 1], recv_sem=recv_sems.at[7 * a + kk - 1],
                device_id=(px, py, pc), device_id_type=MESH_ID))
    return copies


def _gather_copies(src_refs, land_refs, send_sems, recv_sems):
    x, y, c = _mesh_pos()
    me = 4 * x + 2 * y + c
    copies = []
    for kk in range(1, N_DEV):
        px = 1 - x if kk & 4 else x
        py = 1 - y if kk & 2 else y
        pc = 1 - c if kk & 1 else c
        for a, (src, land) in enumerate(zip(src_refs, land_refs)):
            copies.append(pltpu.make_async_remote_copy(
                src_ref=src, dst_ref=land.at[me],
                send_sem=send_sems.at[7 * a + kk - 1], recv_sem=recv_sems.at[7 * a + kk - 1],
                device_id=(px, py, pc), device_id_type=MESH_ID))
    return copies


_HBM = pl.BlockSpec(memory_space=pltpu.HBM)
_SEM = pl.BlockSpec(memory_space=pltpu.SEMAPHORE)
_DATAFLOW = pltpu.SideEffectType.DATAFLOW_SIDE_EFFECTING


def _split_start(name, make_copies, sources, land_shapes):
    n = len(sources)

    def body(*refs):
        src_refs, land_refs, send_sems, recv_sems = refs[:n], refs[n:2 * n], refs[2 * n], refs[2 * n + 1]
        for cp in make_copies(src_refs, land_refs, send_sems, recv_sems):
            cp.start()
        token = refs[-1]
        token[...] = jnp.zeros_like(token)

    in_hbm = lambda a: pltpu.with_memory_space_constraint(a, pltpu.HBM)
    hbm_shapes = [pltpu.HBM(s.shape, s.dtype) for s in list(sources) + list(land_shapes)]
    outs = pl.pallas_call(
        body, name=name, in_specs=[_HBM] * (2 * n),
        out_shape=(pltpu.SemaphoreType.DMA((7 * n,)), pltpu.SemaphoreType.DMA((7 * n,)), *hbm_shapes,
                   jax.ShapeDtypeStruct((8, LANES), F32)),
        out_specs=(_SEM, _SEM, *[_HBM] * (2 * n), pl.BlockSpec(memory_space=pltpu.VMEM)),
        input_output_aliases={a: 2 + a for a in range(2 * n)},
        compiler_params=pltpu.CompilerParams(has_side_effects=_DATAFLOW),
    )(*[in_hbm(s) for s in sources], *[in_hbm(lax.empty(s.shape, s.dtype)) for s in land_shapes])
    return outs[0], outs[1], outs[2:2 + n], outs[2 + n:2 + 2 * n], outs[-1]


def _split_wait(name, make_copies, send_sems, recv_sems, src_thru, land_thru, after):
    n = len(src_thru)

    def body(*refs):
        src_refs, land_refs, send_sems, recv_sems = refs[:n], refs[n:2 * n], refs[2 * n], refs[2 * n + 1]
        for cp in make_copies(src_refs, land_refs, send_sems, recv_sems):
            cp.wait_send()
            cp.wait_recv()

    outs = pl.pallas_call(
        body, name=name,
        in_specs=[_HBM] * (2 * n) + [_SEM, _SEM, pl.BlockSpec(memory_space=pl.ANY)],
        out_shape=tuple(pltpu.HBM(b.shape, b.dtype) for b in list(src_thru) + list(land_thru)),
        out_specs=[_HBM] * (2 * n), input_output_aliases={a: a for a in range(2 * n)},
        compiler_params=pltpu.CompilerParams(has_side_effects=_DATAFLOW),
    )(*src_thru, *land_thru, send_sems, recv_sems, after)
    return outs[:n], outs[n:]


def _exchange_start(blocks):
    return _split_start("grad_exchange_start", _exchange_copies, blocks, blocks)


def _exchange_wait(send_sems, recv_sems, src_thru, land_thru, after):
    return _split_wait("grad_exchange_wait", _exchange_copies, send_sems, recv_sems, src_thru, land_thru, after)


def _gather_start(shards):
    lands = [jax.ShapeDtypeStruct((N_DEV,) + s.shape, s.dtype) for s in shards]
    return _split_start("weight_gather_start", _gather_copies, shards, lands)


def _gather_wait(send_sems, recv_sems, src_thru, land_thru, after):
    return _split_wait("weight_gather_wait", _gather_copies, send_sems, recv_sems, src_thru, land_thru, after)


def _pad_flat(parts, rows):
    flat = jnp.concatenate([p.reshape(-1) for p in parts])
    return jnp.pad(flat, (0, rows * LANES - flat.shape[0])).reshape(rows, LANES)


def _rows_for(n_elems, mult=1024):
    rows = -(-n_elems // LANES)
    return -(-rows // mult) * mult


SHARDED = ("meta_tokens", "w_in", "gdn_conv_w", "w_branch_fox", "w_branch_gdn", "w_out", "ffn_w_up", "ffn_conv_w",
           "ffn_w_down")
MATMUL = ("w_in", "w_branch_fox", "w_branch_gdn", "w_out", "ffn_w_up", "ffn_w_down")
EXACT = ("meta_tokens", "gdn_conv_w", "ffn_conv_w")
REPLICATED = ("fgt_bias", "gdn_a_log", "gdn_dt_bias", "gdn_norm_w", "gate_bias", "norm_mix_w", "norm_ffn_w",
              "ffn_conv_b", "norm_final_w")
WEIGHTS = ("meta_tokens", "w_in", "fgt_bias", "gdn_conv_w", "gdn_a_log", "gdn_dt_bias", "gdn_norm_w", "gate_bias",
           "w_branch_fox", "w_branch_gdn", "w_out", "norm_mix_w", "norm_ffn_w", "ffn_w_up", "ffn_conv_w",
           "ffn_conv_b", "ffn_w_down", "norm_final_w")


def _unpack(buf, shapes):
    flat = buf.reshape(-1)
    out, off = [], 0
    for s in shapes:
        n = int(np.prod(s))
        out.append(flat[off:off + n].reshape(s))
        off += n
    return out


def _unpack_gathered(buf, shapes):
    flat = buf.reshape(N_DEV, -1)
    out, off = [], 0
    for s in shapes:
        n = int(np.prod(s))
        out.append(flat[:, off:off + n].reshape((N_DEV,) + tuple(s)))
        off += n
    return out


def _cat_cols(g):
    return g.transpose(1, 0, 2).reshape(g.shape[1], -1)


def _col_blocks(full, width):
    return full.reshape(full.shape[0], N_DEV, width).transpose(1, 0, 2)


def _local_step(x, target, w, early=None, late_weights=None):
    seq = x.shape[0]
    t = _padded_tokens(seq)
    pad = t - N_META - seq
    seg = _seg_matrix()
    zrows = jnp.zeros((pad, D_MODEL), F32)
    h0 = jnp.concatenate([w["meta_tokens"], x, zrows], axis=0)
    tgt = jnp.concatenate([jnp.zeros((N_META, D_MODEL), F32), target, zrows], axis=0)

    w_in = w["w_in"]
    o_f, o_g, o_z, o_b, o_a, o_gate = 1536, 1544, 3080, 3592, 3600, 3608
    w_small = jnp.concatenate([w_in[:, o_f:o_f + 8], w_in[:, o_b:o_b + 8], w_in[:, o_a:o_a + 8],
                               jnp.zeros((D_MODEL, LANES - 24), BF16)], axis=1)
    w_r = jnp.concatenate([w_in[:, :1536], w_in[:, o_g:o_z], w_in[:, o_z:o_b], w_in[:, o_gate:], w_small], axis=1)

    a1 = _rmsnorm_fwd(h0, w["norm_mix_w"])
    fq = _mm(a1, w_r[:, :1536], BF16, "proj_fox")
    xg = _mm(a1, w_r[:, 1536:3584], BF16, "proj_gdn")
    gt = _mm(a1, w_r[:, 3584:5632], BF16, "proj_gates")
    sm = _mm(a1, w_r[:, 5632:], F32, "proj_small")

    lanes_pad = lambda a, lo: jnp.pad(a, ((0, 0), (lo, LANES - lo - a.shape[1])))
    neg_exp_a = -jnp.exp(w["gdn_a_log"])
    pbias = lanes_pad(w["fgt_bias"], 0) + lanes_pad(w["gdn_dt_bias"], 2 * HEADS)
    if late_weights is not None:
        pbias = pbias + late_weights[0][0, 0]
    pscale = lanes_pad(neg_exp_a, 2 * HEADS)
    scal = _gate_fwd(sm, pbias, pscale)
    gct = scal[:, 2 * HEADS:3 * HEADS].T

    qa, ka, va, kat, vat = _fox_prep(fq, scal)
    oa, qb, qbt = _fox_fwd(qa, ka, vat)
    o_fox = _fox_post(oa)

    qh, kh, vh = _gdn_pre_fwd(xg, w["gdn_conv_w"], seg)
    og, s0s = _gdn_fwd(qh, kh, vh, scal, gct)
    norm_w = jnp.tile(w["gdn_norm_w"], (1, HEADS))
    ogn = _gdn_post_fwd(og, xg, norm_w, seg)

    if late_weights is not None:
        w = {**w, **late_weights[1](ogn)}
    yf = _mm(o_fox, w["w_branch_fox"], BF16, "branch_fox")
    yg = _mm(ogn, w["w_branch_gdn"], BF16, "branch_gdn")
    mix = _mix_fwd(yf, yg, gt, w["gate_bias"])
    h1, a2 = _mm(mix, w["w_out"], F32, "out_proj", res=h0, norm_gain=w["norm_ffn_w"])
    up = _mm(a2, w["ffn_w_up"], BF16, "ffn_up")
    act = _ffn_act_fwd(up, w["ffn_conv_w"], w["ffn_conv_b"])
    h2 = _mm(act, w["ffn_w_down"], F32, "ffn_down", res=h1)
    loss, dh2, dh2b, g_final = _final_loss(h2, tgt, w["norm_final_w"].reshape(1, D_MODEL), seq)

    grads = {"norm_final_w": g_final.reshape(D_MODEL)}
    grads["ffn_w_down"] = _mm_tn(act, dh2b, "wgrad_ffn_down")
    dact = _mm(dh2b, w["ffn_w_down"].T, F32, "dgrad_ffn_down")
    dup, g_cw, g_cb = _ffn_act_bwd(up, w["ffn_conv_w"], w["ffn_conv_b"], dact)
    grads["ffn_conv_w"], grads["ffn_conv_b"] = g_cw, g_cb
    grads["ffn_w_up"] = _mm_tn(a2, dup, "wgrad_ffn_up")
    da2 = _mm(dup, w["ffn_w_up"].T, BF16, "dgrad_ffn_up")
    dh1, dh1b, grads["norm_ffn_w"] = _rmsnorm_bwd(h1, da2, w["norm_ffn_w"], dh2)
    grads["w_out"] = _mm_tn(mix, dh1b, "wgrad_out")
    dmix = _mm(dh1b, w["w_out"].T, BF16, "dgrad_out")
    dyf, dyg, dgt, grads["gate_bias"] = _mix_bwd(dmix, yf, yg, gt, w["gate_bias"])
    grads["w_branch_fox"] = _mm_tn(o_fox, dyf, "wgrad_branch_fox")
    grads["w_branch_gdn"] = _mm_tn(ogn, dyg, "wgrad_branch_gdn")
    do_fox = _mm(dyf, w["w_branch_fox"].T, F32, "dgrad_branch_fox")
    dogn = _mm(dyg, w["w_branch_gdn"].T, F32, "dgrad_branch_gdn")

    dog, dz, g_nw = _gdn_post_bwd(og, xg, norm_w, seg, dogn)
    grads["gdn_norm_w"] = g_nw.reshape(HEADS, HEAD_DIM).sum(axis=0)[None]
    dqh, dkh, dvh, dscal_g, dgct = _gdn_bwd(qh, kh, vh, scal, gct, s0s, dog)
    dxg, grads["gdn_conv_w"] = _gdn_pre_bwd(xg, w["gdn_conv_w"], seg, dqh, dkh, dvh)

    doa, doat = _fox_bwd_prep(do_fox, oa)
    dfq, dscal_c = _fox_bwd_post(*_fox_bwd(qb, qbt, ka, kat, va, doa, doat))

    dscal = dscal_c + dscal_g + lanes_pad(dgct.T, 2 * HEADS)
    dsm, dpb, dps = _gate_bwd(sm, pbias, pscale, dscal)
    grads["fgt_bias"] = dpb[:, :HEADS]
    grads["gdn_dt_bias"] = dpb[:, 2 * HEADS:3 * HEADS]
    grads["gdn_a_log"] = dps[:, 2 * HEADS:3 * HEADS] * neg_exp_a

    dproj = jnp.concatenate([dfq, dxg, dz, dgt, dsm], axis=1)
    g_r = _mm_tn(a1, dproj, "wgrad_in")
    grads["w_in"] = jnp.concatenate([g_r[:, :1536], g_r[:, 5632:5640], g_r[:, 1536:3072], g_r[:, 3072:3584],
                                     g_r[:, 5640:5648], g_r[:, 5648:5656], g_r[:, 3584:5632]], axis=1)
    token, handle = early(grads) if early is not None else (jnp.zeros((8, LANES), F32), None)
    w_rt = w_r.T + token[0, 0].astype(BF16)
    da1 = _mm(dproj, w_rt, BF16, "dgrad_in")
    dh0, _, grads["norm_mix_w"] = _rmsnorm_bwd(h0, da1, w["norm_mix_w"], dh1)
    grads["meta_tokens"] = dh0[:N_META]
    return loss, dh0[N_META:N_META + seq], grads, handle


def _shard_pieces(arrs):
    return [arrs[n][0] if arrs[n].ndim == 3 else arrs[n] for n in SHARDED]


def _full_grad_blocks(grads):
    g = grads
    cols = lambda a, wd: _col_blocks(a, wd)
    rows = lambda a: a.reshape(N_DEV, a.shape[0] // N_DEV, a.shape[1])
    return [cols(g["w_in"], IN_WIDTH // N_DEV), cols(g["gdn_conv_w"], 3 * WIDTH // N_DEV),
            cols(g["w_branch_fox"], D_MODEL // N_DEV), cols(g["w_branch_gdn"], D_MODEL // N_DEV), rows(g["w_out"]),
            cols(g["ffn_w_up"], 2 * D_FF // N_DEV), cols(g["ffn_conv_w"], 2 * D_FF // N_DEV), rows(g["ffn_w_down"])]


def kernel(x, meta_tokens, w_in, fgt_bias, gdn_conv_w, gdn_a_log, gdn_dt_bias, gdn_norm_w, gate_bias, w_branch_fox, w_branch_gdn, w_out, norm_mix_w, norm_ffn_w, ffn_w_up, ffn_conv_w, ffn_conv_b, ffn_w_down, norm_final_w, loss_target, m_meta_tokens, m_w_in, m_fgt_bias, m_gdn_conv_w, m_gdn_a_log, m_gdn_dt_bias, m_gdn_norm_w, m_gate_bias, m_w_branch_fox, m_w_branch_gdn, m_w_out, m_norm_mix_w, m_norm_ffn_w, m_ffn_w_up, m_ffn_conv_w, m_ffn_conv_b, m_ffn_w_down, m_norm_final_w, v_meta_tokens, v_w_in, v_fgt_bias, v_gdn_conv_w, v_gdn_a_log, v_gdn_dt_bias, v_gdn_norm_w, v_gate_bias, v_w_branch_fox, v_w_branch_gdn, v_w_out, v_norm_mix_w, v_norm_ffn_w, v_ffn_w_up, v_ffn_conv_w, v_ffn_conv_b, v_ffn_w_down, v_norm_final_w):
    wts = dict(meta_tokens=meta_tokens, w_in=w_in, fgt_bias=fgt_bias, gdn_conv_w=gdn_conv_w, gdn_a_log=gdn_a_log,
               gdn_dt_bias=gdn_dt_bias, gdn_norm_w=gdn_norm_w, gate_bias=gate_bias, w_branch_fox=w_branch_fox,
               w_branch_gdn=w_branch_gdn, w_out=w_out, norm_mix_w=norm_mix_w, norm_ffn_w=norm_ffn_w,
               ffn_w_up=ffn_w_up, ffn_conv_w=ffn_conv_w, ffn_conv_b=ffn_conv_b, ffn_w_down=ffn_w_down,
               norm_final_w=norm_final_w)
    mom = dict(meta_tokens=m_meta_tokens, w_in=m_w_in, fgt_bias=m_fgt_bias, gdn_conv_w=m_gdn_conv_w,
               gdn_a_log=m_gdn_a_log, gdn_dt_bias=m_gdn_dt_bias, gdn_norm_w=m_gdn_norm_w, gate_bias=m_gate_bias,
               w_branch_fox=m_w_branch_fox, w_branch_gdn=m_w_branch_gdn, w_out=m_w_out, norm_mix_w=m_norm_mix_w,
               norm_ffn_w=m_norm_ffn_w, ffn_w_up=m_ffn_w_up, ffn_conv_w=m_ffn_conv_w, ffn_conv_b=m_ffn_conv_b,
               ffn_w_down=m_ffn_w_down, norm_final_w=m_norm_final_w)
    var = dict(meta_tokens=v_meta_tokens, w_in=v_w_in, fgt_bias=v_fgt_bias, gdn_conv_w=v_gdn_conv_w,
               gdn_a_log=v_gdn_a_log, gdn_dt_bias=v_gdn_dt_bias, gdn_norm_w=v_gdn_norm_w, gate_bias=v_gate_bias,
               w_branch_fox=v_w_branch_fox, w_branch_gdn=v_w_branch_gdn, w_out=v_w_out, norm_mix_w=v_norm_mix_w,
               norm_ffn_w=v_norm_ffn_w, ffn_w_up=v_ffn_w_up, ffn_conv_w=v_ffn_conv_w, ffn_conv_b=v_ffn_conv_b,
               ffn_w_down=v_ffn_w_down, norm_final_w=v_norm_final_w)

    sh = dict(zip(SHARDED, _shard_pieces(wts)))
    me = 4 * lax.axis_index("x") + 2 * lax.axis_index("y") + lax.axis_index("c")
    late_names = MATMUL[1:]
    late_sems_send, late_sems_recv, late_src, late_land, late_token = _gather_start(
        [sh[n].astype(BF16) for n in late_names])
    exact_shapes = [sh[n].shape for n in EXACT]
    rows_exact = _rows_for(sum(int(np.prod(s)) for s in exact_shapes), 8)
    g_in, g_exact = _all_gather([sh["w_in"].astype(BF16), _pad_flat([sh[n] for n in EXACT], rows_exact)])
    meta_full, conv_full, fconv_full = (_cat_cols(a) for a in _unpack_gathered(g_exact, exact_shapes))
    full = dict(
        meta_tokens=meta_full, w_in=_cat_cols(g_in), gdn_conv_w=conv_full, ffn_conv_w=fconv_full,
        fgt_bias=fgt_bias, gdn_a_log=gdn_a_log, gdn_dt_bias=gdn_dt_bias, gdn_norm_w=gdn_norm_w, gate_bias=gate_bias,
        norm_mix_w=norm_mix_w, norm_ffn_w=norm_ffn_w, ffn_conv_b=ffn_conv_b, norm_final_w=norm_final_w)

    def fetch_late_weights(after):
        shards, lands = _gather_wait(late_sems_send, late_sems_recv, late_src, late_land, after)
        g_bf, g_bg, g_out, g_up, g_down = (lax.dynamic_update_slice_in_dim(land, s[None], me, 0)
                                           for s, land in zip(shards, lands))
        return dict(w_branch_fox=_cat_cols(g_bf), w_branch_gdn=_cat_cols(g_bg), w_out=g_out.reshape(D_MODEL, D_MODEL),
                    ffn_w_up=_cat_cols(g_up), ffn_w_down=g_down.reshape(D_FF, D_MODEL))

    def start_exchange(grads_so_far):
        blocks = [b.astype(BF16) for b in _full_grad_blocks(grads_so_far)]
        send_sems, recv_sems, src_thru, land_thru, token = _exchange_start(blocks)
        return token, (send_sems, recv_sems, src_thru, land_thru)

    loss, grad_x, grads, handle = _local_step(x[0], loss_target[0], full, early=start_exchange,
                                              late_weights=(late_token, fetch_late_weights))
    sent, landed = _exchange_wait(*handle, after=grad_x)
    own = lambda src, land: lax.dynamic_update_slice_in_dim(land, lax.dynamic_slice_in_dim(src, me, 1, 0), me, 0)
    received = [own(src, land) for src, land in zip(sent, landed)]

    rep_parts = [grads[n] for n in REPLICATED] + [loss[:, :1]]
    rep_shapes = [wts[n].shape for n in REPLICATED]
    rows_small = _rows_for(sum(int(np.prod(p.shape)) for p in rep_parts), 8)
    meta_recv, small_recv = _grad_exchange([_col_blocks(grads["meta_tokens"], LANES).astype(BF16)],
                                           _pad_flat(rep_parts, rows_small))
    received = [meta_recv] + received + [small_recv]

    result = {}
    kinds = ("grad", "delta", "new_m", "new_v")
    for n, recv in zip(SHARDED, received[:-1]):
        outs = _adamw(sh[n], _shard_pieces(mom)[SHARDED.index(n)], _shard_pieces(var)[SHARDED.index(n)], recv,
                      "adamw_" + n)
        for kind, a in zip(kinds, outs):
            result[kind, n] = a.reshape(wts[n].shape)
    rep_w = _pad_flat([wts[n] for n in REPLICATED] + [jnp.zeros((1, 1), F32)], rows_small)
    rep_m = _pad_flat([mom[n] for n in REPLICATED] + [jnp.zeros((1, 1), F32)], rows_small)
    rep_v = _pad_flat([var[n] for n in REPLICATED] + [jnp.ones((1, 1), F32)], rows_small)
    outs_r = _adamw(rep_w, rep_m, rep_v, received[-1], "adamw_replicated")
    for kind, br in zip(kinds, outs_r):
        for n, a in zip(REPLICATED, _unpack(br, rep_shapes)):
            result[kind, n] = a
    n_rep = sum(int(np.prod(s)) for s in rep_shapes)
    total_loss = outs_r[0].reshape(-1)[n_rep]
    out = [total_loss, grad_x[None]]
    for kind in ("grad", "delta", "new_m", "new_v"):
        out += [result[kind, n] for n in WEIGHTS]
    return tuple(out)
```
